```python
import jax
import jax.numpy as jnp
from jax import lax
import numpy as np

D_MODEL = 2048
BATCH = 8
SEQ = 4096
DEPTH = 2

N_HEADS = 16
HEAD_DIM = D_MODEL // N_HEADS
D_ATT = N_HEADS * HEAD_DIM
D_CONV = D_MODEL
D_MIX = D_ATT + D_CONV
D_IN = 4 * D_ATT + 3 * D_CONV
CONV_WIDTH = 31
DILATED_PATTERNS = ((128, 1), (512, 4), (2048, 16))
Q_BLOCK = 128
ROPE_THETA = 10000.0
EPS = 1e-6
SPLIT_POINTS = (D_ATT, 2 * D_ATT, 3 * D_ATT, 4 * D_ATT, 4 * D_ATT + D_CONV, 4 * D_ATT + 2 * D_CONV)

kernel_name = "hybrid_dilated_attn_conformer_conv"


def rms_norm(x, gain):
    xf = x.astype(jnp.float32)
    y = xf * lax.rsqrt(jnp.mean(xf * xf, axis=-1, keepdims=True) + EPS)
    return (y * gain.astype(jnp.float32)).astype(x.dtype)


def layer_norm(x, gain, bias):
    xf = x.astype(jnp.float32)
    xc = xf - jnp.mean(xf, axis=-1, keepdims=True)
    y = xc * lax.rsqrt(jnp.mean(xc * xc, axis=-1, keepdims=True) + EPS)
    return (y * gain.astype(jnp.float32) + bias.astype(jnp.float32)).astype(x.dtype)


def rope_tables(seq):
    inv_freq = 1.0 / (ROPE_THETA ** (jnp.arange(0, HEAD_DIM, 2, dtype=jnp.float32) / HEAD_DIM))
    ang = jnp.arange(seq, dtype=jnp.float32)[:, None] * inv_freq[None, :]
    return jnp.cos(ang), jnp.sin(ang)


def apply_rope(x, cos, sin):
    x1, x2 = jnp.split(x.astype(jnp.float32), 2, axis=-1)
    c = cos[None, :, None, :]
    s = sin[None, :, None, :]
    return jnp.concatenate([x1 * c - x2 * s, x2 * c + x1 * s], axis=-1).astype(x.dtype)


def dilated_branch(q, k, v, window, dilation):
    B, S, H, Dh = q.shape
    L = S // dilation
    n_back = window // dilation
    assert n_back <= Q_BLOCK
    Lp = -(-L // Q_BLOCK) * Q_BLOCK
    nb = Lp // Q_BLOCK

    def strided(t):
        t = t.reshape(B, L, dilation, H, Dh).transpose(0, 2, 1, 3, 4)
        t = jnp.pad(t, ((0, 0), (0, 0), (0, Lp - L), (0, 0), (0, 0)))
        return t.reshape(B, dilation, nb, Q_BLOCK, H, Dh)

    def with_prev(t):
        prev = jnp.pad(t[:, :, :-1], ((0, 0), (0, 0), (1, 0), (0, 0), (0, 0), (0, 0)))
        return jnp.concatenate([prev, t], axis=3)

    qb = strided(q)
    kb = with_prev(strided(k))
    vb = with_prev(strided(v))
    s = jnp.einsum("brnqhd,brnkhd->brnhqk", qb, kb, preferred_element_type=jnp.float32) * (Dh ** -0.5)
    blk = jnp.arange(nb)[:, None, None]
    qi = blk * Q_BLOCK + jnp.arange(Q_BLOCK)[None, :, None]
    ki = (blk - 1) * Q_BLOCK + jnp.arange(2 * Q_BLOCK)[None, None, :]
    dist = qi - ki
    mask = (dist >= 0) & (dist <= n_back) & (ki >= 0)
    s = jnp.where(mask[None, None, :, None], s, -jnp.inf)
    m = jnp.max(s, axis=-1)
    p = jnp.exp(s - m[..., None])
    l = jnp.sum(p, axis=-1)
    o = jnp.einsum("brnhqk,brnkhd->brnqhd", p.astype(v.dtype), vb, preferred_element_type=jnp.float32)

    def unstride(t):
        tail = t.shape[5:]
        t = t.reshape((B, dilation, Lp, H) + tail)[:, :, :L]
        t = jnp.moveaxis(t, 1, 2)
        return t.reshape((B, S, H) + tail)

    return unstride(o), unstride(jnp.moveaxis(m, 3, 4)), unstride(jnp.moveaxis(l, 3, 4))


def dilated_mixture(q, k, v):
    branches = [dilated_branch(q, k, v, w, d) for (w, d) in DILATED_PATTERNS]
    ms = jnp.stack([b[1] for b in branches])
    ls = jnp.stack([b[2] for b in branches])
    wts = jnp.exp(ms - jnp.max(ms, axis=0))
    num = wts[0][..., None] * branches[0][0]
    for i in range(1, len(branches)):
        num = num + wts[i][..., None] * branches[i][0]
    den = jnp.sum(wts * ls, axis=0)
    return (num / den[..., None]).astype(q.dtype)


def conformer_conv(glu_a, glu_b, dw_kernel, dw_bias, ln_g, ln_b, w_pw):
    u = glu_a * jax.nn.sigmoid(glu_b)
    u = jnp.pad(u, ((0, 0), (CONV_WIDTH - 1, 0), (0, 0)))
    y = lax.conv_general_dilated(u, dw_kernel.astype(u.dtype)[:, None, :], window_strides=(1,),
                                 padding="VALID", dimension_numbers=("NWC", "WIO", "NWC"),
                                 feature_group_count=D_CONV) + dw_bias
    y = jax.nn.silu(layer_norm(y, ln_g, ln_b))
    return jnp.einsum("bsc,ce->bse", y, w_pw)


def _fwd_setup_inputs(seed: int = 0) -> dict:
    key = jax.random.key(seed)
    ks = jax.random.split(key, 13)
    nrm = jax.random.normal
    f32 = jnp.float32
    return {
        "x": nrm(ks[0], (BATCH, SEQ, D_MODEL), f32),
        "norm_g": 1.0 + 0.02 * nrm(ks[1], (DEPTH, D_MODEL), f32),
        "w_in": nrm(ks[2], (DEPTH, D_MODEL, D_IN), f32) * D_MODEL ** -0.5,
        "q_norm_g": 1.0 + 0.02 * nrm(ks[3], (DEPTH, HEAD_DIM), f32),
        "k_norm_g": 1.0 + 0.02 * nrm(ks[4], (DEPTH, HEAD_DIM), f32),
        "dw_kernel": nrm(ks[5], (DEPTH, CONV_WIDTH, D_CONV), f32) * CONV_WIDTH ** -0.5,
        "dw_bias": 0.02 * nrm(ks[6], (DEPTH, D_CONV), f32),
        "conv_ln_g": 1.0 + 0.02 * nrm(ks[7], (DEPTH, D_CONV), f32),
        "conv_ln_b": 0.02 * nrm(ks[8], (DEPTH, D_CONV), f32),
        "w_pw": nrm(ks[9], (DEPTH, D_CONV, D_CONV), f32) * D_CONV ** -0.5,
        "att_out_g": 1.0 + 0.02 * nrm(ks[10], (DEPTH, D_ATT), f32),
        "conv_out_g": 1.0 + 0.02 * nrm(ks[11], (DEPTH, D_CONV), f32),
        "w_out": nrm(ks[12], (DEPTH, D_MIX, D_MODEL), f32) * D_MIX ** -0.5,
    }


def _fwd_reference(x, norm_g, w_in, q_norm_g, k_norm_g, dw_kernel, dw_bias, conv_ln_g, conv_ln_b, w_pw,
              att_out_g, conv_out_g, w_out):
    B, S, _ = x.shape
    cos, sin = rope_tables(S)
    for layer in range(DEPTH):
        h = rms_norm(x, norm_g[layer])
        proj = jnp.einsum("bsd,de->bse", h, w_in[layer])
        q, k, v, g_att, glu_a, glu_b, g_conv = jnp.split(proj, SPLIT_POINTS, axis=-1)
        q = apply_rope(rms_norm(q.reshape(B, S, N_HEADS, HEAD_DIM), q_norm_g[layer]), cos, sin)
        k = apply_rope(rms_norm(k.reshape(B, S, N_HEADS, HEAD_DIM), k_norm_g[layer]), cos, sin)
        v = v.reshape(B, S, N_HEADS, HEAD_DIM)
        att = dilated_mixture(q, k, v).reshape(B, S, D_ATT)
        att_y = rms_norm(att, att_out_g[layer]) * jax.nn.silu(g_att)
        conv = conformer_conv(glu_a, glu_b, dw_kernel[layer], dw_bias[layer], conv_ln_g[layer],
                              conv_ln_b[layer], w_pw[layer])
        conv_y = rms_norm(conv, conv_out_g[layer]) * jax.nn.silu(g_conv)
        y = jnp.einsum("bse,ed->bsd", jnp.concatenate([att_y, conv_y], axis=-1), w_out[layer])
        x = x + y
    return x


import jax as _jax
import jax.numpy as _jnp

TWIN_FORMAT = 'train_step'
FWD_PARAMS = ['x', 'norm_g', 'w_in', 'q_norm_g', 'k_norm_g', 'dw_kernel', 'dw_bias', 'conv_ln_g', 'conv_ln_b', 'w_pw', 'att_out_g', 'conv_out_g', 'w_out']
TWIN_WEIGHTS = ['norm_g', 'w_in', 'q_norm_g', 'k_norm_g', 'dw_kernel', 'dw_bias', 'conv_ln_g', 'conv_ln_b', 'w_pw', 'att_out_g', 'conv_out_g', 'w_out']
TWIN_DIFF_INPUT = 'x'
TWIN_INPUTS = ['x', 'norm_g', 'w_in', 'q_norm_g', 'k_norm_g', 'dw_kernel', 'dw_bias', 'conv_ln_g', 'conv_ln_b', 'w_pw', 'att_out_g', 'conv_out_g', 'w_out', 'loss_target', 'm_norm_g', 'm_w_in', 'm_q_norm_g', 'm_k_norm_g', 'm_dw_kernel', 'm_dw_bias', 'm_conv_ln_g', 'm_conv_ln_b', 'm_w_pw', 'm_att_out_g', 'm_conv_out_g', 'm_w_out', 'v_norm_g', 'v_w_in', 'v_q_norm_g', 'v_k_norm_g', 'v_dw_kernel', 'v_dw_bias', 'v_conv_ln_g', 'v_conv_ln_b', 'v_w_pw', 'v_att_out_g', 'v_conv_out_g', 'v_w_out']
TWIN_OUTPUTS = ['loss', 'grad_x', 'grad_norm_g', 'grad_w_in', 'grad_q_norm_g', 'grad_k_norm_g', 'grad_dw_kernel', 'grad_dw_bias', 'grad_conv_ln_g', 'grad_conv_ln_b', 'grad_w_pw', 'grad_att_out_g', 'grad_conv_out_g', 'grad_w_out', 'delta_norm_g', 'delta_w_in', 'delta_q_norm_g', 'delta_k_norm_g', 'delta_dw_kernel', 'delta_dw_bias', 'delta_conv_ln_g', 'delta_conv_ln_b', 'delta_w_pw', 'delta_att_out_g', 'delta_conv_out_g', 'delta_w_out', 'new_m_norm_g', 'new_m_w_in', 'new_m_q_norm_g', 'new_m_k_norm_g', 'new_m_dw_kernel', 'new_m_dw_bias', 'new_m_conv_ln_g', 'new_m_conv_ln_b', 'new_m_w_pw', 'new_m_att_out_g', 'new_m_conv_out_g', 'new_m_w_out', 'new_v_norm_g', 'new_v_w_in', 'new_v_q_norm_g', 'new_v_k_norm_g', 'new_v_dw_kernel', 'new_v_dw_bias', 'new_v_conv_ln_g', 'new_v_conv_ln_b', 'new_v_w_pw', 'new_v_att_out_g', 'new_v_conv_out_g', 'new_v_w_out']
TWIN_LEAF_KINDS = {'loss': 'loss', 'grad_x': 'grad_x', 'grad_norm_g': 'grad_w', 'grad_w_in': 'grad_w', 'grad_q_norm_g': 'grad_w', 'grad_k_norm_g': 'grad_w', 'grad_dw_kernel': 'grad_w', 'grad_dw_bias': 'grad_w', 'grad_conv_ln_g': 'grad_w', 'grad_conv_ln_b': 'grad_w', 'grad_w_pw': 'grad_w', 'grad_att_out_g': 'grad_w', 'grad_conv_out_g': 'grad_w', 'grad_w_out': 'grad_w', 'delta_norm_g': 'delta_w', 'delta_w_in': 'delta_w', 'delta_q_norm_g': 'delta_w', 'delta_k_norm_g': 'delta_w', 'delta_dw_kernel': 'delta_w', 'delta_dw_bias': 'delta_w', 'delta_conv_ln_g': 'delta_w', 'delta_conv_ln_b': 'delta_w', 'delta_w_pw': 'delta_w', 'delta_att_out_g': 'delta_w', 'delta_conv_out_g': 'delta_w', 'delta_w_out': 'delta_w', 'new_m_norm_g': 'new_m', 'new_m_w_in': 'new_m', 'new_m_q_norm_g': 'new_m', 'new_m_k_norm_g': 'new_m', 'new_m_dw_kernel': 'new_m', 'new_m_dw_bias': 'new_m', 'new_m_conv_ln_g': 'new_m', 'new_m_conv_ln_b': 'new_m', 'new_m_w_pw': 'new_m', 'new_m_att_out_g': 'new_m', 'new_m_conv_out_g': 'new_m', 'new_m_w_out': 'new_m', 'new_v_norm_g': 'new_v', 'new_v_w_in': 'new_v', 'new_v_q_norm_g': 'new_v', 'new_v_k_norm_g': 'new_v', 'new_v_dw_kernel': 'new_v', 'new_v_dw_bias': 'new_v', 'new_v_conv_ln_g': 'new_v', 'new_v_conv_ln_b': 'new_v', 'new_v_w_pw': 'new_v', 'new_v_att_out_g': 'new_v', 'new_v_conv_out_g': 'new_v', 'new_v_w_out': 'new_v'}


def _forward(args):
    return _fwd_reference(*[args[k] for k in FWD_PARAMS])


def _output_shape():
    def fwd():
        inp = _fwd_setup_inputs(0)
        return _fwd_reference(*[inp[k] for k in FWD_PARAMS])
    out = _jax.eval_shape(fwd)
    return out.shape, out.dtype

N_MICROBATCH = 1
ADAM_LR = 0.001
ADAM_B1 = 0.9
ADAM_B2 = 0.999
ADAM_EPS = 1e-08
ADAM_WD = 0.01
ADAM_STEP = 10
PER_EXAMPLE_BATCH_AXIS = {'x': 0, 'loss_target': 0}
SHARED_INPUTS = []
_WEIGHT_DTYPES = {'norm_g': _jnp.float32, 'w_in': _jnp.float32, 'q_norm_g': _jnp.float32, 'k_norm_g': _jnp.float32, 'dw_kernel': _jnp.float32, 'dw_bias': _jnp.float32, 'conv_ln_g': _jnp.float32, 'conv_ln_b': _jnp.float32, 'w_pw': _jnp.float32, 'att_out_g': _jnp.float32, 'conv_out_g': _jnp.float32, 'w_out': _jnp.float32}
MOMENT_SCALE = {'norm_g': 6.655282e+00, 'w_in': 9.411612e-02, 'q_norm_g': 3.709322e-01, 'k_norm_g': 3.911080e-01, 'dw_kernel': 9.362016e-02, 'dw_bias': 5.604773e-01, 'conv_ln_g': 2.281873e-01, 'conv_ln_b': 3.286821e-01, 'w_pw': 1.398148e-01, 'att_out_g': 2.856454e+00, 'conv_out_g': 2.893215e+00, 'w_out': 1.759662e-01}


def _to_microbatches(a, axis):
    t = _jnp.moveaxis(a, axis, 0)
    t = t.reshape((N_MICROBATCH, t.shape[0] // N_MICROBATCH) + t.shape[1:])
    return _jnp.moveaxis(t, 1, axis + 1)


def setup_inputs(seed: int = 0) -> dict:
    inp = _fwd_setup_inputs(seed)
    key = _jax.random.fold_in(_jax.random.key(seed), 7919)
    shape, _ = _output_shape()
    out = dict(inp)
    out["loss_target"] = _jax.random.normal(_jax.random.fold_in(key, 0), shape, _jnp.float32)
    for i, name in enumerate(TWIN_WEIGHTS):
        w = inp[name].astype(_jnp.float32)
        if MOMENT_SCALE is None:
            s = _jnp.sqrt(_jnp.mean(_jnp.square(w)) + 1e-30)
        else:
            s = MOMENT_SCALE[name]
        km, kv = _jax.random.split(_jax.random.fold_in(key, i + 1))
        out[name] = w
        out["m_" + name] = s * _jax.random.normal(km, w.shape, _jnp.float32)
        out["v_" + name] = (s * s) * _jax.random.uniform(kv, w.shape, _jnp.float32, 0.5, 1.5)
    if N_MICROBATCH > 1:
        for name, axis in PER_EXAMPLE_BATCH_AXIS.items():
            out[name] = _to_microbatches(out[name], axis)
    return {'x': out['x'], 'norm_g': out['norm_g'], 'w_in': out['w_in'], 'q_norm_g': out['q_norm_g'], 'k_norm_g': out['k_norm_g'], 'dw_kernel': out['dw_kernel'], 'dw_bias': out['dw_bias'], 'conv_ln_g': out['conv_ln_g'], 'conv_ln_b': out['conv_ln_b'], 'w_pw': out['w_pw'], 'att_out_g': out['att_out_g'], 'conv_out_g': out['conv_out_g'], 'w_out': out['w_out'], 'loss_target': out['loss_target'], 'm_norm_g': out['m_norm_g'], 'm_w_in': out['m_w_in'], 'm_q_norm_g': out['m_q_norm_g'], 'm_k_norm_g': out['m_k_norm_g'], 'm_dw_kernel': out['m_dw_kernel'], 'm_dw_bias': out['m_dw_bias'], 'm_conv_ln_g': out['m_conv_ln_g'], 'm_conv_ln_b': out['m_conv_ln_b'], 'm_w_pw': out['m_w_pw'], 'm_att_out_g': out['m_att_out_g'], 'm_conv_out_g': out['m_conv_out_g'], 'm_w_out': out['m_w_out'], 'v_norm_g': out['v_norm_g'], 'v_w_in': out['v_w_in'], 'v_q_norm_g': out['v_q_norm_g'], 'v_k_norm_g': out['v_k_norm_g'], 'v_dw_kernel': out['v_dw_kernel'], 'v_dw_bias': out['v_dw_bias'], 'v_conv_ln_g': out['v_conv_ln_g'], 'v_conv_ln_b': out['v_conv_ln_b'], 'v_w_pw': out['v_w_pw'], 'v_att_out_g': out['v_att_out_g'], 'v_conv_out_g': out['v_conv_out_g'], 'v_w_out': out['v_w_out']}


def _loss(weights, diff, rest, loss_target):
    with _jax.named_scope("forward"):
        args = {**rest, TWIN_DIFF_INPUT: diff, **{k: w.astype(_WEIGHT_DTYPES[k]) for k, w in weights.items()}}
        y = _forward(args)
    with _jax.named_scope("loss_head"):
        err = _jnp.square(y.astype(_jnp.float32) - loss_target)
        return 0.5 * _jnp.sum(_jnp.mean(err, axis=-1)) if err.ndim else 0.5 * err


def _adamw(w, g, m, v):
    m = ADAM_B1 * m + (1.0 - ADAM_B1) * g
    v = ADAM_B2 * v + (1.0 - ADAM_B2) * _jnp.square(g)
    m_hat = m / (1.0 - ADAM_B1 ** ADAM_STEP)
    v_hat = v / (1.0 - ADAM_B2 ** ADAM_STEP)
    delta = -ADAM_LR * (m_hat / (_jnp.sqrt(v_hat) + ADAM_EPS) + ADAM_WD * w)
    return delta, m, v


def reference(x, norm_g, w_in, q_norm_g, k_norm_g, dw_kernel, dw_bias, conv_ln_g, conv_ln_b, w_pw, att_out_g, conv_out_g, w_out, loss_target, m_norm_g, m_w_in, m_q_norm_g, m_k_norm_g, m_dw_kernel, m_dw_bias, m_conv_ln_g, m_conv_ln_b, m_w_pw, m_att_out_g, m_conv_out_g, m_w_out, v_norm_g, v_w_in, v_q_norm_g, v_k_norm_g, v_dw_kernel, v_dw_bias, v_conv_ln_g, v_conv_ln_b, v_w_pw, v_att_out_g, v_conv_out_g, v_w_out):
    given = dict(x=x, norm_g=norm_g, w_in=w_in, q_norm_g=q_norm_g, k_norm_g=k_norm_g, dw_kernel=dw_kernel, dw_bias=dw_bias, conv_ln_g=conv_ln_g, conv_ln_b=conv_ln_b, w_pw=w_pw, att_out_g=att_out_g, conv_out_g=conv_out_g, w_out=w_out, loss_target=loss_target, m_norm_g=m_norm_g, m_w_in=m_w_in, m_q_norm_g=m_q_norm_g, m_k_norm_g=m_k_norm_g, m_dw_kernel=m_dw_kernel, m_dw_bias=m_dw_bias, m_conv_ln_g=m_conv_ln_g, m_conv_ln_b=m_conv_ln_b, m_w_pw=m_w_pw, m_att_out_g=m_att_out_g, m_conv_out_g=m_conv_out_g, m_w_out=m_w_out, v_norm_g=v_norm_g, v_w_in=v_w_in, v_q_norm_g=v_q_norm_g, v_k_norm_g=v_k_norm_g, v_dw_kernel=v_dw_kernel, v_dw_bias=v_dw_bias, v_conv_ln_g=v_conv_ln_g, v_conv_ln_b=v_conv_ln_b, v_w_pw=v_w_pw, v_att_out_g=v_att_out_g, v_conv_out_g=v_conv_out_g, v_w_out=v_w_out)
    weights = {n: given[n] for n in TWIN_WEIGHTS}
    shared = {n: given[n] for n in SHARED_INPUTS}
    per_example = {n: given[n] for n in ['x']}
    grad_fn = _jax.value_and_grad(_loss, argnums=(0, 1))

    def one_microbatch(ex, loss_target):
        ex = dict(ex)
        diff = ex.pop(TWIN_DIFF_INPUT)
        return grad_fn(weights, diff, {**shared, **ex}, loss_target)

    if N_MICROBATCH == 1:
        loss, (grad_w, grad_x) = one_microbatch(per_example, given["loss_target"])
    else:
        def body(carry, xs):
            loss_sum, grad_sum = carry
            l_k, (gw_k, gx_k) = one_microbatch(xs[0], xs[1])
            with _jax.named_scope("update"):
                return (loss_sum + l_k, _jax.tree.map(_jnp.add, grad_sum, gw_k)), gx_k

        init = (_jnp.zeros((), _jnp.float32), _jax.tree.map(_jnp.zeros_like, weights))
        (loss, grad_w), grad_x = _jax.lax.scan(body, init, (per_example, given["loss_target"]))
    with _jax.named_scope("update"):
        delta_w, new_m, new_v = {}, {}, {}
        for n in TWIN_WEIGHTS:
            delta_w[n], new_m[n], new_v[n] = _adamw(weights[n], grad_w[n], given["m_" + n], given["v_" + n])
    return (loss, grad_x, *[grad_w[n] for n in TWIN_WEIGHTS], *[delta_w[n] for n in TWIN_WEIGHTS],
            *[new_m[n] for n in TWIN_WEIGHTS], *[new_v[n] for n in TWIN_WEIGHTS])
```

```python
import jax
import jax.numpy as jnp
from jax import lax
from jax.experimental import pallas as pl
from jax.experimental.pallas import tpu as pltpu

F32 = jnp.float32
BF16 = jnp.bfloat16
SDS = jax.ShapeDtypeStruct
MESH = pl.DeviceIdType.MESH

N_DEV = 8
DEPTH = 2
HEAD_DIM = 128
CONV_WIDTH = 31
CONV_PAD = 32
DILATIONS = (1, 4, 16)
Q_BLOCK = 128
ROPE_THETA = 10000.0
EPS = 1e-6
NEG = -1e30
ADAM_LR, ADAM_B1, ADAM_B2, ADAM_EPS, ADAM_WD, ADAM_STEP = 0.001, 0.9, 0.999, 1e-08, 0.01, 10
LANE = 128
ROW_CHUNK = 64
MIB = 1 << 20
NT = (((1,), (1,)), ((), ()))
TN = (((0,), (0,)), ((), ()))


def _call(body, **kw):
    return pl.pallas_call(body, **kw)


def _params(vmem_mib):
    return pltpu.CompilerParams(vmem_limit_bytes=vmem_mib * MIB)


def _tile(dim, pref, mult):
    t = min(pref, dim)
    while dim % t or t % mult:
        t -= mult
    return t


def _sig(v):
    return jax.nn.sigmoid(v)


def _rstd(v):
    return lax.rsqrt(jnp.mean(v * v, axis=-1, keepdims=True) + EPS)


def _row(tm, cb, c=0):
    return pl.BlockSpec((tm, cb), lambda i: (i, c))


def _full(shape):
    return pl.BlockSpec(shape, lambda i: (0,) * len(shape))


def _halo_prev(tm, cb, c=0):
    k = tm // CONV_PAD
    return pl.BlockSpec((CONV_PAD, cb), lambda i: (jnp.maximum(i * k - 1, 0), c))


def _halo_next(tm, cb, nblk, c=0):
    k = tm // CONV_PAD
    return pl.BlockSpec((CONV_PAD, cb), lambda i: (jnp.minimum((i + 1) * k, nblk - 1), c))


def in_proj(x, g, w, name):
    S, D = x.shape
    N = w.shape[1]
    tm, tn = _tile(S, 512, 8), _tile(N, 1024, LANE)

    def body(x_ref, g_ref, w_ref, o_ref, h_ref):
        @pl.when(pl.program_id(1) == 0)
        def _():
            xf = x_ref[...]
            h_ref[...] = (xf * _rstd(xf) * g_ref[...]).astype(BF16)

        o_ref[...] = jnp.dot(h_ref[...], w_ref[...], preferred_element_type=F32)

    return _call(
        body, name=name, grid=(S // tm, N // tn),
        in_specs=[pl.BlockSpec((tm, D), lambda i, j: (i, 0)), pl.BlockSpec((1, D), lambda i, j: (0, 0)),
                  pl.BlockSpec((D, tn), lambda i, j: (0, j))],
        out_specs=[pl.BlockSpec((tm, tn), lambda i, j: (i, j)), pl.BlockSpec((tm, D), lambda i, j: (i, 0))],
        out_shape=[SDS((S, N), F32), SDS((S, D), BF16)],
        compiler_params=_params(40),
    )(x, g, w)


def mm_nn(a, b, out_dtype, name, add=None):
    M, K = a.shape
    N = b.shape[1]
    tm, tn = _tile(M, 512, 8), _tile(N, 1024, LANE)

    def body(*refs):
        a_ref, b_ref = refs[0], refs[1]
        o_ref = refs[-1]
        acc = jnp.dot(a_ref[...], b_ref[...], preferred_element_type=F32)
        if add is not None:
            acc = acc + refs[2][...]
        o_ref[...] = acc.astype(out_dtype)

    in_specs = [pl.BlockSpec((tm, K), lambda i, j: (i, 0)), pl.BlockSpec((K, tn), lambda i, j: (0, j))]
    args = [a, b]
    if add is not None:
        in_specs.append(pl.BlockSpec((tm, tn), lambda i, j: (i, j)))
        args.append(add)
    return _call(
        body, name=name, grid=(M // tm, N // tn), in_specs=in_specs,
        out_specs=pl.BlockSpec((tm, tn), lambda i, j: (i, j)), out_shape=SDS((M, N), out_dtype),
        compiler_params=_params(48),
    )(*args)


def mm_nt(a, b, out_dtype, name):
    M, K = a.shape
    N = b.shape[0]
    tm, tn = _tile(M, 512, 8), _tile(N, 1024, LANE)

    def body(a_ref, b_ref, o_ref):
        o_ref[...] = lax.dot_general(a_ref[...], b_ref[...], NT, preferred_element_type=F32).astype(out_dtype)

    return _call(
        body, name=name, grid=(M // tm, N // tn),
        in_specs=[pl.BlockSpec((tm, K), lambda i, j: (i, 0)), pl.BlockSpec((tn, K), lambda i, j: (j, 0))],
        out_specs=pl.BlockSpec((tm, tn), lambda i, j: (i, j)), out_shape=SDS((M, N), out_dtype),
        compiler_params=_params(40),
    )(a, b)


def in_proj_bwd(dproj, w, x, g, dx_out, name):
    S, K = dproj.shape
    D = w.shape[0]
    tm, tk = _tile(S, 256, 8), _tile(K, 1024, LANE)
    nk = K // tk

    def body(dp_ref, w_ref, x_ref, g_ref, dxo_ref, dx_ref, dxb_ref, gacc_ref, acc_ref):
        i, k = pl.program_id(0), pl.program_id(1)

        @pl.when(k == 0)
        def _():
            acc_ref[...] = jnp.zeros_like(acc_ref)

        @pl.when((k == 0) & (i == 0))
        def _():
            gacc_ref[...] = jnp.zeros_like(gacc_ref)

        acc_ref[...] += lax.dot_general(dp_ref[...], w_ref[...], NT, preferred_element_type=F32)

        @pl.when(k == nk - 1)
        def _():
            dh = acc_ref[...]
            xf = x_ref[...]
            r = _rstd(xf)
            n = xf * r
            gacc_ref[...] += jnp.sum(dh * n, axis=0, keepdims=True)
            dn = dh * g_ref[...]
            dx = r * (dn - n * jnp.mean(dn * n, axis=-1, keepdims=True)) + dxo_ref[...]
            dx_ref[...] = dx
            dxb_ref[...] = dx.astype(BF16)

    return _call(
        body, name=name, grid=(S // tm, nk),
        in_specs=[pl.BlockSpec((tm, tk), lambda i, k: (i, k)), pl.BlockSpec((D, tk), lambda i, k: (0, k)),
                  pl.BlockSpec((tm, D), lambda i, k: (i, 0)), pl.BlockSpec((1, D), lambda i, k: (0, 0)),
                  pl.BlockSpec((tm, D), lambda i, k: (i, 0))],
        out_specs=[pl.BlockSpec((tm, D), lambda i, k: (i, 0)), pl.BlockSpec((tm, D), lambda i, k: (i, 0)),
                   pl.BlockSpec((1, D), lambda i, k: (0, 0))],
        out_shape=[SDS((S, D), F32), SDS((S, D), BF16), SDS((1, D), F32)],
        scratch_shapes=[pltpu.VMEM((tm, D), F32)],
        compiler_params=_params(54),
    )(dproj, w, x, g, dx_out)


def qk_prep(proj, gq, gk, cos2, sin2, name):
    S = proj.shape[0]
    DA = proj.shape[1] // 7
    H = DA // HEAD_DIM
    tm = _tile(S, 256, 8)

    def body(q_ref, k_ref, v_ref, gq_ref, gk_ref, c_ref, s_ref, qo_ref, ko_ref, vo_ref):
        ct, st = c_ref[...], s_ref[...]
        for src, g_ref, dst in ((q_ref, gq_ref, qo_ref), (k_ref, gk_ref, ko_ref)):
            gain = g_ref[...]
            for h in range(H):
                sl = slice(h * HEAD_DIM, (h + 1) * HEAD_DIM)
                xh = src[:, sl]
                n = xh * _rstd(xh) * gain
                dst[:, sl] = (n * ct + pltpu.roll(n, HEAD_DIM // 2, 1) * st).astype(BF16)
        vo_ref[...] = v_ref[...].astype(BF16)

    return _call(
        body, name=name, grid=(S // tm,),
        in_specs=[_row(tm, DA, 0), _row(tm, DA, 1), _row(tm, DA, 2), _full((1, HEAD_DIM)), _full((1, HEAD_DIM)),
                  _row(tm, HEAD_DIM), _row(tm, HEAD_DIM)],
        out_specs=[_row(tm, DA)] * 3, out_shape=[SDS((S, DA), BF16)] * 3,
        compiler_params=_params(40),
    )(proj, proj, proj, gq, gk, cos2, sin2)


def qk_prep_bwd(dqs, dks, dvs, proj, gq, gk, cos2, sin2, name):
    S = proj.shape[0]
    DA = proj.shape[1] // 7
    H = DA // HEAD_DIM
    tm = _tile(S, 256, 8)
    nb = len(dqs)

    def body(*refs):
        dq_refs, dk_refs, dv_refs = refs[:nb], refs[nb:2 * nb], refs[2 * nb:3 * nb]
        q_ref, k_ref, gq_ref, gk_ref, c_ref, s_ref = refs[3 * nb:3 * nb + 6]
        dqo_ref, dko_ref, dvo_ref, gqa_ref, gka_ref = refs[3 * nb + 6:]
        ct, st = c_ref[...], s_ref[...]

        @pl.when(pl.program_id(0) == 0)
        def _():
            gqa_ref[...] = jnp.zeros_like(gqa_ref)
            gka_ref[...] = jnp.zeros_like(gka_ref)

        for parts, x_ref, g_ref, dst, gacc in ((dq_refs, q_ref, gq_ref, dqo_ref, gqa_ref),
                                               (dk_refs, k_ref, gk_ref, dko_ref, gka_ref)):
            gain = g_ref[...]
            gsum = jnp.zeros((1, HEAD_DIM), F32)
            for h in range(H):
                sl = slice(h * HEAD_DIM, (h + 1) * HEAD_DIM)
                dout = parts[0][:, sl].astype(F32)
                for p in parts[1:]:
                    dout = dout + p[:, sl].astype(F32)
                dn = dout * ct + pltpu.roll(dout * st, HEAD_DIM // 2, 1)
                xh = x_ref[:, sl]
                r = _rstd(xh)
                xn = xh * r
                gsum = gsum + jnp.sum(dn * xn, axis=0, keepdims=True)
                dnn = dn * gain
                dst[:, sl] = (r * (dnn - xn * jnp.mean(dnn * xn, axis=-1, keepdims=True))).astype(BF16)
            gacc[...] += gsum
        dv = dv_refs[0][...].astype(F32)
        for p in dv_refs[1:]:
            dv = dv + p[...].astype(F32)
        dvo_ref[...] = dv.astype(BF16)

    return _call(
        body, name=name, grid=(S // tm,),
        in_specs=[_row(tm, DA)] * (3 * nb) + [_row(tm, DA, 0), _row(tm, DA, 1), _full((1, HEAD_DIM)),
                                               _full((1, HEAD_DIM)), _row(tm, HEAD_DIM), _row(tm, HEAD_DIM)],
        out_specs=[_row(tm, DA)] * 3 + [_full((1, HEAD_DIM))] * 2,
        out_shape=[SDS((S, DA), BF16)] * 3 + [SDS((1, HEAD_DIM), F32)] * 2,
        compiler_params=_params(48),
    )(*dqs, *dks, *dvs, proj, proj, gq, gk, cos2, sin2)


def _band_masks(n):
    row = lax.broadcasted_iota(jnp.int32, (Q_BLOCK, Q_BLOCK), 0)
    col = lax.broadcasted_iota(jnp.int32, (Q_BLOCK, Q_BLOCK), 1)
    first = jnp.where(n > 0, 0, Q_BLOCK + 1)
    return col <= row, (col - row) >= first


def attn_fwd(qh, kh, vb, d, name):
    S, DA = qh.shape
    H = DA // HEAD_DIM
    L = S // d
    nb = L // Q_BLOCK
    scale = HEAD_DIM ** -0.5
    view = (L, d * DA)

    def body(q_ref, kc_ref, kp_ref, vc_ref, vp_ref, o_ref, lse_ref):
        mask_c, mask_p = _band_masks(pl.program_id(1))
        for h in range(H):
            sl = slice(h * HEAD_DIM, (h + 1) * HEAD_DIM)
            q = q_ref[:, sl]
            sc = lax.dot_general(q, kc_ref[:, sl], NT, preferred_element_type=F32) * scale
            sp = lax.dot_general(q, kp_ref[:, sl], NT, preferred_element_type=F32) * scale
            sc = jnp.where(mask_c, sc, NEG)
            sp = jnp.where(mask_p, sp, NEG)
            m = jnp.maximum(jnp.max(sc, axis=-1, keepdims=True), jnp.max(sp, axis=-1, keepdims=True))
            pc = jnp.exp(sc - m)
            pp = jnp.exp(sp - m)
            l = jnp.sum(pc, axis=-1, keepdims=True) + jnp.sum(pp, axis=-1, keepdims=True)
            o = jnp.dot(pc.astype(BF16), vc_ref[:, sl], preferred_element_type=F32)
            o = o + jnp.dot(pp.astype(BF16), vp_ref[:, sl], preferred_element_type=F32)
            o_ref[:, sl] = o / l
            lse_ref[0, :, h:h + 1] = m + jnp.log(l)

    cur = pl.BlockSpec((Q_BLOCK, DA), lambda r, n: (n, r))
    prev = pl.BlockSpec((Q_BLOCK, DA), lambda r, n: (jnp.maximum(n - 1, 0), r))
    o, lse = _call(
        body, name=name, grid=(d, nb), in_specs=[cur, cur, prev, cur, prev],
        out_specs=[cur, pl.BlockSpec((1, Q_BLOCK, H), lambda r, n: (r, n, 0))],
        out_shape=[SDS(view, F32), SDS((d, L, H), F32)],
        compiler_params=_params(32),
    )(qh.reshape(view), kh.reshape(view), kh.reshape(view), vb.reshape(view), vb.reshape(view))
    return o.reshape(S, DA), lse


def attn_bwd(qh, kh, vb, da, lse_d, delta_d, d, name):
    S, DA = qh.shape
    H = DA // HEAD_DIM
    L = S // d
    nb = L // Q_BLOCK
    scale = HEAD_DIM ** -0.5
    view = (L, d * DA)

    def body(q_ref, kc_ref, kp_ref, vc_ref, vp_ref, do_ref, lse_ref, dl_ref, dq_ref, dk_ref, dv_ref, dkc_ref, dvc_ref):
        n = pl.program_id(1)
        mask_c, mask_p = _band_masks(n)

        @pl.when(n == 0)
        def _():
            dkc_ref[...] = jnp.zeros_like(dkc_ref)
            dvc_ref[...] = jnp.zeros_like(dvc_ref)

        @pl.when(n < nb)
        def _():
            for h in range(H):
                sl = slice(h * HEAD_DIM, (h + 1) * HEAD_DIM)
                q, kc, kp, vc, vp, do = q_ref[:, sl], kc_ref[:, sl], kp_ref[:, sl], vc_ref[:, sl], vp_ref[:, sl], do_ref[:, sl]
                lse = lse_ref[0, :, h:h + 1]
                dl = dl_ref[0, :, h:h + 1]
                sc = lax.dot_general(q, kc, NT, preferred_element_type=F32) * scale
                sp = lax.dot_general(q, kp, NT, preferred_element_type=F32) * scale
                pc = jnp.exp(jnp.where(mask_c, sc, NEG) - lse)
                pp = jnp.exp(jnp.where(mask_p, sp, NEG) - lse)
                dpc = lax.dot_general(do, vc, NT, preferred_element_type=F32)
                dpp = lax.dot_general(do, vp, NT, preferred_element_type=F32)
                dsc = (pc * (dpc - dl) * scale).astype(BF16)
                dsp = (pp * (dpp - dl) * scale).astype(BF16)
                dq = jnp.dot(dsc, kc, preferred_element_type=F32) + jnp.dot(dsp, kp, preferred_element_type=F32)
                dq_ref[:, sl] = dq.astype(BF16)
                dk_prev = lax.dot_general(dsp, q, TN, preferred_element_type=F32)
                dv_prev = lax.dot_general(pp.astype(BF16), do, TN, preferred_element_type=F32)
                dk_ref[:, sl] = (dkc_ref[:, sl] + dk_prev).astype(BF16)
                dv_ref[:, sl] = (dvc_ref[:, sl] + dv_prev).astype(BF16)
                dkc_ref[:, sl] = lax.dot_general(dsc, q, TN, preferred_element_type=F32)
                dvc_ref[:, sl] = lax.dot_general(pc.astype(BF16), do, TN, preferred_element_type=F32)

        @pl.when(n == nb)
        def _():
            dk_ref[...] = dkc_ref[...].astype(BF16)
            dv_ref[...] = dvc_ref[...].astype(BF16)

    cur = pl.BlockSpec((Q_BLOCK, DA), lambda r, n: (jnp.minimum(n, nb - 1), r))
    prev = pl.BlockSpec((Q_BLOCK, DA), lambda r, n: (jnp.clip(n - 1, 0, nb - 1), r))
    late = pl.BlockSpec((Q_BLOCK, DA), lambda r, n: (jnp.maximum(n - 1, 0), r))
    stat = pl.BlockSpec((1, Q_BLOCK, H), lambda r, n: (r, jnp.minimum(n, nb - 1), 0))
    dq, dk, dv = _call(
        body, name=name, grid=(d, nb + 1), in_specs=[cur, cur, prev, cur, prev, cur, stat, stat],
        out_specs=[cur, late, late], out_shape=[SDS(view, BF16)] * 3,
        scratch_shapes=[pltpu.VMEM((Q_BLOCK, DA), F32), pltpu.VMEM((Q_BLOCK, DA), F32)],
        compiler_params=_params(32),
    )(qh.reshape(view), kh.reshape(view), kh.reshape(view), vb.reshape(view), vb.reshape(view), da.reshape(view),
      lse_d, delta_d)
    return dq.reshape(S, DA), dk.reshape(S, DA), dv.reshape(S, DA)


def _to_branch(stat, d):
    S, H = stat.shape
    return stat.reshape(S // d, d, H).transpose(1, 0, 2)


def _from_branch(stat):
    d, L, H = stat.shape
    return stat.transpose(1, 0, 2).reshape(L * d, H)


def att_combine(os_, lses, proj, gain, name):
    S, DA = os_[0].shape
    H = DA // HEAD_DIM
    tm = _tile(S, 256, 8)
    nb = len(os_)

    def body(*refs):
        o_refs, l_refs = refs[:nb], refs[nb:2 * nb]
        gate_ref, gain_ref, att_ref, y_ref, lse_ref = refs[2 * nb:]
        ls = [r[...] for r in l_refs]
        top = ls[0]
        for l in ls[1:]:
            top = jnp.maximum(top, l)
        den = jnp.exp(ls[0] - top)
        for l in ls[1:]:
            den = den + jnp.exp(l - top)
        lse = top + jnp.log(den)
        lse_ref[...] = lse
        ws = [jnp.exp(l - lse) for l in ls]
        for h in range(H):
            sl = slice(h * HEAD_DIM, (h + 1) * HEAD_DIM)
            acc = ws[0][:, h:h + 1] * o_refs[0][:, sl]
            for w, o_ref in zip(ws[1:], o_refs[1:]):
                acc = acc + w[:, h:h + 1] * o_ref[:, sl]
            att_ref[:, sl] = acc
        a = att_ref[...]
        g = gate_ref[...]
        y_ref[...] = (a * _rstd(a) * gain_ref[...] * (g * _sig(g))).astype(BF16)

    return _call(
        body, name=name, grid=(S // tm,),
        in_specs=[_row(tm, DA)] * nb + [_row(tm, H)] * nb + [_row(tm, DA, 3), _full((1, DA))],
        out_specs=[_row(tm, DA), _row(tm, DA), _row(tm, H)],
        out_shape=[SDS((S, DA), F32), SDS((S, DA), BF16), SDS((S, H), F32)],
        compiler_params=_params(40),
    )(*os_, *lses, proj, gain)


def gate_bwd(dcat, cblk, a, proj, gate_blk, gain, with_delta, name):
    S, DA = a.shape
    H = DA // HEAD_DIM
    tm = _tile(S, 256, 8)

    def body(dy_ref, a_ref, gate_ref, gain_ref, da_ref, dg_ref, gacc_ref, *rest):
        @pl.when(pl.program_id(0) == 0)
        def _():
            gacc_ref[...] = jnp.zeros_like(gacc_ref)

        dy, av, g, gain_v = dy_ref[...], a_ref[...], gate_ref[...], gain_ref[...]
        r = _rstd(av)
        n = av * r
        sg = _sig(g)
        dg_ref[...] = (dy * (n * gain_v) * (sg * (1.0 + g * (1.0 - sg)))).astype(BF16)
        drn = dy * (g * sg)
        gacc_ref[...] += jnp.sum(drn * n, axis=0, keepdims=True)
        dn = drn * gain_v
        da = r * (dn - n * jnp.mean(dn * n, axis=-1, keepdims=True))
        da_ref[...] = da.astype(BF16)
        if with_delta:
            prod = da * av
            for h in range(H):
                rest[0][:, h:h + 1] = jnp.sum(prod[:, h * HEAD_DIM:(h + 1) * HEAD_DIM], axis=-1, keepdims=True)

    out_specs = [_row(tm, DA), _row(tm, DA), _full((1, DA))]
    out_shape = [SDS((S, DA), BF16), SDS((S, DA), BF16), SDS((1, DA), F32)]
    if with_delta:
        out_specs.append(_row(tm, H))
        out_shape.append(SDS((S, H), F32))
    return _call(
        body, name=name, grid=(S // tm,),
        in_specs=[_row(tm, DA, cblk), _row(tm, DA), _row(tm, DA, gate_blk), _full((1, DA))],
        out_specs=out_specs, out_shape=out_shape, compiler_params=_params(40),
    )(dcat, a, proj, gain)


def _fill_u(i, a_ref, b_ref, ah_ref, bh_ref, uext_ref, tm):
    uext_ref[pl.ds(CONV_PAD, tm), :] = a_ref[...] * _sig(b_ref[...])
    uh = ah_ref[...] * _sig(bh_ref[...])
    uext_ref[pl.ds(0, CONV_PAD), :] = jnp.where(i > 0, uh, 0.0)


def conv_fwd(proj, wk, bias, ln_g, ln_b, out_g, wpw, name):
    S = proj.shape[0]
    DC = proj.shape[1] // 7
    tm = _tile(S, 128, ROW_CHUNK)
    lead = CONV_PAD - (CONV_WIDTH - 1)

    def body(a_ref, b_ref, ah_ref, bh_ref, gate_ref, wk_ref, bias_ref, lg_ref, lb_ref, og_ref, wpw_ref,
             cy_ref, z_ref, conv_ref, y_ref, uext_ref):
        _fill_u(pl.program_id(0), a_ref, b_ref, ah_ref, bh_ref, uext_ref, tm)

        def cols(cc, carry):
            c0 = pl.multiple_of(cc * LANE, LANE)
            for rr in range(tm // ROW_CHUNK):
                acc = jnp.broadcast_to(bias_ref[:, pl.ds(c0, LANE)], (ROW_CHUNK, LANE))
                for j in range(CONV_WIDTH):
                    acc = acc + wk_ref[j:j + 1, pl.ds(c0, LANE)] * uext_ref[pl.ds(rr * ROW_CHUNK + lead + j, ROW_CHUNK), pl.ds(c0, LANE)]
                y_ref[pl.ds(rr * ROW_CHUNK, ROW_CHUNK), pl.ds(c0, LANE)] = acc
            return carry

        lax.fori_loop(0, DC // LANE, cols, 0)
        y = y_ref[...]
        yc = y - jnp.mean(y, axis=-1, keepdims=True)
        ln = yc * _rstd(yc) * lg_ref[...] + lb_ref[...]
        zb = (ln * _sig(ln)).astype(BF16)
        z_ref[...] = zb
        conv = jnp.dot(zb, wpw_ref[...], preferred_element_type=F32)
        conv_ref[...] = conv
        g = gate_ref[...]
        cy_ref[...] = (conv * _rstd(conv) * og_ref[...] * (g * _sig(g))).astype(BF16)

    vec = _full((1, DC))
    return _call(
        body, name=name, grid=(S // tm,),
        in_specs=[_row(tm, DC, 4), _row(tm, DC, 5), _halo_prev(tm, DC, 4), _halo_prev(tm, DC, 5), _row(tm, DC, 6),
                  _full((CONV_PAD, DC)), vec, vec, vec, vec, _full((DC, DC))],
        out_specs=[_row(tm, DC)] * 4,
        out_shape=[SDS((S, DC), BF16), SDS((S, DC), BF16), SDS((S, DC), F32), SDS((S, DC), F32)],
        scratch_shapes=[pltpu.VMEM((CONV_PAD + tm, DC), F32)],
        compiler_params=_params(48),
    )(proj, proj, proj, proj, proj, wk, bias, ln_g, ln_b, out_g, wpw)


def conv_bwd_ln(dconv, wpw, y, ln_g, ln_b, name):
    S, DC = y.shape
    tm = _tile(S, 256, 8)

    def body(dc_ref, wpw_ref, y_ref, lg_ref, lb_ref, dy_ref, glg_ref, glb_ref, gb_ref):
        @pl.when(pl.program_id(0) == 0)
        def _():
            glg_ref[...] = jnp.zeros_like(glg_ref)
            glb_ref[...] = jnp.zeros_like(glb_ref)
            gb_ref[...] = jnp.zeros_like(gb_ref)

        dz = lax.dot_general(dc_ref[...], wpw_ref[...], NT, preferred_element_type=F32)
        yv = y_ref[...]
        yc = yv - jnp.mean(yv, axis=-1, keepdims=True)
        rstd = _rstd(yc)
        yhat = yc * rstd
        ln = yhat * lg_ref[...] + lb_ref[...]
        sg = _sig(ln)
        dln = dz * (sg * (1.0 + ln * (1.0 - sg)))
        glb_ref[...] += jnp.sum(dln, axis=0, keepdims=True)
        glg_ref[...] += jnp.sum(dln * yhat, axis=0, keepdims=True)
        dyh = dln * lg_ref[...]
        dy = rstd * (dyh - jnp.mean(dyh, axis=-1, keepdims=True) - yhat * jnp.mean(dyh * yhat, axis=-1, keepdims=True))
        dy_ref[...] = dy
        gb_ref[...] += jnp.sum(dy, axis=0, keepdims=True)

    vec = _full((1, DC))
    return _call(
        body, name=name, grid=(S // tm,),
        in_specs=[_row(tm, DC), _full((DC, DC)), _row(tm, DC), vec, vec],
        out_specs=[_row(tm, DC), vec, vec, vec],
        out_shape=[SDS((S, DC), F32)] + [SDS((1, DC), F32)] * 3,
        compiler_params=_params(48),
    )(dconv, wpw, y, ln_g, ln_b)


def conv_bwd_dw(dy, proj, wk, name):
    S, DC = dy.shape
    tm = _tile(S, 128, ROW_CHUNK)
    nsteps = S // tm
    lead = CONV_PAD - (CONV_WIDTH - 1)
    groups = ROW_CHUNK // 8

    def body(dy_ref, dyn_ref, a_ref, b_ref, ah_ref, bh_ref, wk_ref, da_ref, db_ref, gw_ref, uext_ref, dyext_ref, du_ref):
        i = pl.program_id(0)

        @pl.when(i == 0)
        def _():
            gw_ref[...] = jnp.zeros_like(gw_ref)

        _fill_u(i, a_ref, b_ref, ah_ref, bh_ref, uext_ref, tm)
        dyext_ref[pl.ds(0, tm), :] = dy_ref[...]
        dyext_ref[pl.ds(tm, CONV_PAD), :] = jnp.where(i < nsteps - 1, dyn_ref[...], 0.0)

        def cols(cc, carry):
            c0 = pl.multiple_of(cc * LANE, LANE)
            lanes = pl.ds(c0, LANE)
            for rr in range(tm // ROW_CHUNK):
                acc = jnp.zeros((ROW_CHUNK, LANE), F32)
                for j in range(CONV_WIDTH):
                    acc = acc + wk_ref[j:j + 1, lanes] * dyext_ref[pl.ds(rr * ROW_CHUNK + CONV_WIDTH - 1 - j, ROW_CHUNK), lanes]
                du_ref[pl.ds(rr * ROW_CHUNK, ROW_CHUNK), lanes] = acc
            for j in range(CONV_WIDTH):
                part = jnp.zeros((8, LANE), F32)
                for rr in range(tm // ROW_CHUNK):
                    prod = dyext_ref[pl.ds(rr * ROW_CHUNK, ROW_CHUNK), lanes] * uext_ref[pl.ds(rr * ROW_CHUNK + lead + j, ROW_CHUNK), lanes]
                    for k in range(groups):
                        part = part + prod[8 * k:8 * k + 8]
                gw_ref[j, :, lanes] += part
            return carry

        lax.fori_loop(0, DC // LANE, cols, 0)
        du = du_ref[...]
        sb = _sig(b_ref[...])
        da_ref[...] = (du * sb).astype(BF16)
        db_ref[...] = (du * a_ref[...] * sb * (1.0 - sb)).astype(BF16)

    return _call(
        body, name=name, grid=(nsteps,),
        in_specs=[_row(tm, DC), _halo_next(tm, DC, S // CONV_PAD), _row(tm, DC, 4), _row(tm, DC, 5),
                  _halo_prev(tm, DC, 4), _halo_prev(tm, DC, 5), _full((CONV_PAD, DC))],
        out_specs=[_row(tm, DC), _row(tm, DC), _full((CONV_PAD, 8, DC))],
        out_shape=[SDS((S, DC), BF16), SDS((S, DC), BF16), SDS((CONV_PAD, 8, DC), F32)],
        scratch_shapes=[pltpu.VMEM((CONV_PAD + tm, DC), F32), pltpu.VMEM((tm + CONV_PAD, DC), F32), pltpu.VMEM((tm, DC), F32)],
        compiler_params=_params(40),
    )(dy, dy, proj, proj, proj, proj, wk)


def loss_head(xo, target, name):
    S, D = xo.shape
    tm = _tile(S, 256, 8)

    def body(x_ref, t_ref, dy_ref, dyb_ref, acc_ref):
        @pl.when(pl.program_id(0) == 0)
        def _():
            acc_ref[...] = jnp.zeros_like(acc_ref)

        err = x_ref[...] - t_ref[...]
        dy = err * (1.0 / D)
        dy_ref[...] = dy
        dyb_ref[...] = dy.astype(BF16)
        acc_ref[...] += jnp.sum(err * dy, axis=0, keepdims=True) * 0.5

    return _call(
        body, name=name, grid=(S // tm,), in_specs=[_row(tm, D), _row(tm, D)],
        out_specs=[_row(tm, D), _row(tm, D), _full((1, D))],
        out_shape=[SDS((S, D), F32), SDS((S, D), BF16), SDS((1, D), F32)],
        compiler_params=_params(32),
    )(xo, target)


def _coords():
    x, y, c = lax.axis_index("x"), lax.axis_index("y"), lax.axis_index("c")
    return x, y, c


def _lin(p):
    return 4 * p[0] + 2 * p[1] + p[2]


def _slot(ref, axis, idx, size):
    index = [slice(None)] * len(ref.shape)
    index[axis] = pl.ds(idx * size, size)
    return ref.at[tuple(index)]


def all_gather(blocks, axes, name):
    na = len(blocks)
    sizes = [b.shape[ax] for b, ax in zip(blocks, axes)]
    fulls = [SDS(b.shape[:ax] + (N_DEV * b.shape[ax],) + b.shape[ax + 1:], b.dtype) for b, ax in zip(blocks, axes)]

    def body(*refs):
        in_refs, out_refs = refs[:na], refs[na:2 * na]
        send_sems, recv_sems, local_sems = refs[2 * na:]
        x, y, c = _coords()
        me, sibling = (x, y, c), (x, y, 1 - c)
        chips = [(1 - x, y), (x, 1 - y), (1 - x, 1 - y)]

        def place(a, p):
            return _slot(out_refs[a], axes[a], _lin(p), sizes[a])

        def copy(a, k, block, to, src=None):
            return pltpu.make_async_remote_copy(
                src_ref=place(a, block) if src is None else src, dst_ref=place(a, block),
                send_sem=send_sems.at[a, k], recv_sem=recv_sems.at[a, k], device_id=to, device_id_type=MESH)

        mine = [pltpu.make_async_copy(in_refs[a], place(a, me), local_sems.at[a]) for a in range(na)]
        for cp in mine:
            cp.start()
        first = []
        for a in range(na):
            first.append(copy(a, 0, me, sibling, src=in_refs[a]))
            first += [copy(a, 1 + j, me, (*chip, c), src=in_refs[a]) for j, chip in enumerate(chips)]
        for cp in first:
            cp.start()
        passed = []
        for j, chip in enumerate(chips):
            for a in range(na):
                copy(a, 1 + j, (*chip, c), me).wait_recv()
                cp = copy(a, 4 + j, (*chip, c), sibling)
                cp.start()
                passed.append(cp)
        for a in range(na):
            copy(a, 0, sibling, me).wait_recv()
            for j, chip in enumerate(chips):
                copy(a, 4 + j, (*chip, 1 - c), me).wait_recv()
        for cp in first + passed:
            cp.wait_send()
        for cp in mine:
            cp.wait()

    hbm = pl.BlockSpec(memory_space=pltpu.HBM)
    return _call(
        body, name=name, in_specs=[hbm] * na, out_specs=[hbm] * na, out_shape=fulls,
        scratch_shapes=[pltpu.SemaphoreType.DMA((na, 7)), pltpu.SemaphoreType.DMA((na, 7)), pltpu.SemaphoreType.DMA((na,))],
    )(*blocks)


def scatter_parts(fulls, axes, name):
    kinds = len(fulls)
    nl = len(fulls[0])
    sizes = [f[0].shape[ax] // N_DEV for f, ax in zip(fulls, axes)]
    blocks = [f[0].shape[:ax] + (sz,) + f[0].shape[ax + 1:] for f, ax, sz in zip(fulls, axes, sizes)]
    outs = [SDS((N_DEV, nl) + blk, f[0].dtype) for blk, f in zip(blocks, fulls)]
    flips = [(k >> 2 & 1, k >> 1 & 1, k & 1) for k in range(1, N_DEV)]

    def body(*refs):
        in_refs = [refs[a * nl:(a + 1) * nl] for a in range(kinds)]
        out_refs = refs[kinds * nl:kinds * nl + kinds]
        send_sems, recv_sems, local_sems = refs[kinds * nl + kinds:]
        me = _coords()
        peers = [tuple(1 - v if f else v for v, f in zip(me, flip)) for flip in flips]
        local, sends, recvs = [], [], []
        for a in range(kinds):
            for l in range(nl):
                src_of = lambda p: _slot(in_refs[a][l], axes[a], _lin(p), sizes[a])
                local.append(pltpu.make_async_copy(src_of(me), out_refs[a].at[_lin(me), l], local_sems.at[a, l]))
                for k, peer in enumerate(peers):
                    sends.append(pltpu.make_async_remote_copy(
                        src_ref=src_of(peer), dst_ref=out_refs[a].at[_lin(me), l],
                        send_sem=send_sems.at[a, l, k], recv_sem=recv_sems.at[a, l, k], device_id=peer, device_id_type=MESH))
                    recvs.append(pltpu.make_async_remote_copy(
                        src_ref=src_of(me), dst_ref=out_refs[a].at[_lin(peer), l],
                        send_sem=send_sems.at[a, l, k], recv_sem=recv_sems.at[a, l, k], device_id=peer, device_id_type=MESH))
        for cp in local + sends:
            cp.start()
        for cp in recvs:
            cp.wait_recv()
        for cp in sends:
            cp.wait_send()
        for cp in local:
            cp.wait()

    hbm = pl.BlockSpec(memory_space=pltpu.HBM)
    return _call(
        body, name=name, in_specs=[hbm] * (kinds * nl), out_specs=[hbm] * kinds, out_shape=outs,
        scratch_shapes=[pltpu.SemaphoreType.DMA((kinds, nl, 7)), pltpu.SemaphoreType.DMA((kinds, nl, 7)),
                        pltpu.SemaphoreType.DMA((kinds, nl))],
    )(*[g for f in fulls for g in f])


def adamw(w, m, v, parts, name):
    R, C = w.shape
    tr = _tile(R, 128, 8) if R % 8 == 0 else R

    def body(w_ref, m_ref, v_ref, p_ref, g_ref, d_ref, mo_ref, vo_ref):
        g = p_ref[0].astype(F32)
        for s in range(1, N_DEV):
            g = g + p_ref[s].astype(F32)
        mn = ADAM_B1 * m_ref[...] + (1.0 - ADAM_B1) * g
        vn = ADAM_B2 * v_ref[...] + (1.0 - ADAM_B2) * (g * g)
        m_hat = mn / (1.0 - ADAM_B1 ** ADAM_STEP)
        v_hat = vn / (1.0 - ADAM_B2 ** ADAM_STEP)
        g_ref[...] = g
        d_ref[...] = -ADAM_LR * (m_hat / (jnp.sqrt(v_hat) + ADAM_EPS) + ADAM_WD * w_ref[...])
        mo_ref[...] = mn
        vo_ref[...] = vn

    row = pl.BlockSpec((tr, C), lambda i: (i, 0))
    return _call(
        body, name=name, grid=(R // tr,),
        in_specs=[row, row, row, pl.BlockSpec((N_DEV, tr, C), lambda i: (0, i, 0))],
        out_specs=[row] * 4, out_shape=[SDS((R, C), F32)] * 4,
        compiler_params=_params(48),
    )(w, m, v, parts)


def _rope_tables(S):
    inv_freq = 1.0 / (ROPE_THETA ** (jnp.arange(0, HEAD_DIM, 2, dtype=F32) / HEAD_DIM))
    ang = jnp.arange(S, dtype=F32)[:, None] * inv_freq[None, :]
    cos, sin = jnp.cos(ang), jnp.sin(ang)
    return jnp.concatenate([cos, cos], axis=-1), jnp.concatenate([-sin, sin], axis=-1)


def _pack_small(D, norm_g, dw_bias, conv_ln_g, conv_ln_b, att_out_g, conv_out_g, q_norm_g, k_norm_g, extra=None):
    qk = jnp.concatenate([q_norm_g.reshape(-1), k_norm_g.reshape(-1)])
    qk = jnp.pad(qk, (0, D - qk.shape[0])).reshape(1, D)
    zero = jnp.zeros((1, D), F32)
    return jnp.concatenate([norm_g, dw_bias, conv_ln_g, conv_ln_b, att_out_g, conv_out_g, qk, zero,
                            zero if extra is None else extra, zero], axis=0)


def _unpack_small(p):
    rows = [p[2 * i:2 * i + 2] for i in range(6)]
    qk = p[12, :4 * HEAD_DIM].reshape(2, DEPTH, HEAD_DIM)
    return rows + [qk[0], qk[1]]


def kernel(x, norm_g, w_in, q_norm_g, k_norm_g, dw_kernel, dw_bias, conv_ln_g, conv_ln_b, w_pw, att_out_g, conv_out_g, w_out, loss_target, m_norm_g, m_w_in, m_q_norm_g, m_k_norm_g, m_dw_kernel, m_dw_bias, m_conv_ln_g, m_conv_ln_b, m_w_pw, m_att_out_g, m_conv_out_g, m_w_out, v_norm_g, v_w_in, v_q_norm_g, v_k_norm_g, v_dw_kernel, v_dw_bias, v_conv_ln_g, v_conv_ln_b, v_w_pw, v_att_out_g, v_conv_out_g, v_w_out):
    xs = x[0]
    D = xs.shape[1]
    win_f, wpw_f, wout_f, dwk_f = all_gather(
        [w_in.astype(BF16), w_pw.astype(BF16), w_out.astype(BF16), dw_kernel], [2, 1, 1, 2], "gather_weights")
    dx, loss_cols, g_win, g_wpw, g_wout, g_dwk, small = local_step(
        xs, loss_target[0], win_f, wpw_f, wout_f, dwk_f, norm_g, q_norm_g, k_norm_g, dw_bias, conv_ln_g, conv_ln_b,
        att_out_g, conv_out_g)

    p_win, p_wpw, p_wout, p_dwk = scatter_parts([g_win, g_wpw, g_wout, g_dwk], [1, 0, 0, 1], "scatter_grads")
    stack = lambda k: jnp.concatenate(small[k], axis=0)
    mine = _pack_small(D, stack("norm_g"), stack("dw_bias"), stack("conv_ln_g"), stack("conv_ln_b"), stack("att_out_g"),
                       stack("conv_out_g"), stack("q"), stack("k"), extra=loss_cols)
    (p_small,) = all_gather([mine], [0], "gather_small")

    def big(w, m, v, parts):
        shape = w.shape
        flat = lambda t: t.reshape(-1, shape[-1])
        outs = adamw(flat(w), flat(m), flat(v), parts.reshape(N_DEV, -1, shape[-1]), f"adamw_{shape[1]}_{shape[2]}")
        return [o.reshape(shape) for o in outs]

    r_win = big(w_in, m_w_in, v_w_in, p_win)
    r_wpw = big(w_pw, m_w_pw, v_w_pw, p_wpw)
    r_wout = big(w_out, m_w_out, v_w_out, p_wout)
    r_dwk = big(dw_kernel, m_dw_kernel, v_dw_kernel, p_dwk)
    pk = lambda n, dw, lg, lb, ao, co, q, k: _pack_small(D, n, dw, lg, lb, ao, co, q, k)
    r_small = adamw(pk(norm_g, dw_bias, conv_ln_g, conv_ln_b, att_out_g, conv_out_g, q_norm_g, k_norm_g),
                    pk(m_norm_g, m_dw_bias, m_conv_ln_g, m_conv_ln_b, m_att_out_g, m_conv_out_g, m_q_norm_g, m_k_norm_g),
                    pk(v_norm_g, v_dw_bias, v_conv_ln_g, v_conv_ln_b, v_att_out_g, v_conv_out_g, v_q_norm_g, v_k_norm_g),
                    p_small.reshape(N_DEV, 16, D), "adamw_small")
    loss = jnp.sum(r_small[0][14])

    outs = [loss, dx[None]]
    for i in range(4):
        n_, dwb, lg, lb, ao, co, q_, k_ = _unpack_small(r_small[i])
        outs += [n_, r_win[i], q_, k_, r_dwk[i], dwb, lg, lb, r_wpw[i], ao, co, r_wout[i]]
    return tuple(outs)


def local_step(xs, target, win_f, wpw_f, wout_f, dwk_f, norm_g, q_norm_g, k_norm_g, dw_bias, conv_ln_g, conv_ln_b,
               att_out_g, conv_out_g):
    S, D = xs.shape
    cos2, sin2 = _rope_tables(S)
    dwk_f = jnp.pad(dwk_f, ((0, 0), (0, CONV_PAD - CONV_WIDTH), (0, 0)))

    def vec(p, l):
        return p[l].reshape(1, -1)

    saved = []
    cur = xs
    for l in range(DEPTH):
        proj, h = in_proj(cur, vec(norm_g, l), win_f[l], f"in_proj_{l}")
        qh, kh, vb = qk_prep(proj, vec(q_norm_g, l), vec(k_norm_g, l), cos2, sin2, f"qk_prep_{l}")
        os_, lses = [], []
        for d in DILATIONS:
            o, lse = attn_fwd(qh, kh, vb, d, f"attn_fwd_{l}_d{d}")
            os_.append(o)
            lses.append(_from_branch(lse))
        att, att_y, lse = att_combine(os_, lses, proj, vec(att_out_g, l), f"att_combine_{l}")
        conv_y, z, conv, y = conv_fwd(proj, dwk_f[l], vec(dw_bias, l), vec(conv_ln_g, l), vec(conv_ln_b, l),
                                      vec(conv_out_g, l), wpw_f[l], f"conv_fwd_{l}")
        cat = jnp.concatenate([att_y, conv_y], axis=1)
        nxt = mm_nn(cat, wout_f[l], F32, f"out_proj_{l}", add=cur)
        saved.append(dict(x=cur, proj=proj, h=h, qh=qh, kh=kh, vb=vb, att=att, lse=lse, cat=cat, z=z, conv=conv, y=y))
        cur = nxt

    dx, dxb, loss_cols = loss_head(cur, target, "loss_head")

    g_win, g_wpw, g_wout, g_dwk = [None] * DEPTH, [None] * DEPTH, [None] * DEPTH, [None] * DEPTH
    small = {k: [None] * DEPTH for k in ("norm_g", "dw_bias", "conv_ln_g", "conv_ln_b", "att_out_g", "conv_out_g", "q", "k")}
    for l in reversed(range(DEPTH)):
        sv = saved[l]
        proj = sv["proj"]
        dcat = mm_nt(dxb, wout_f[l], F32, f"dcat_{l}")
        g_wout[l] = mm_nn(sv["cat"].T, dxb, BF16, f"dwout_{l}")
        datt, dgate_a, small["att_out_g"][l], delta = gate_bwd(dcat, 0, sv["att"], proj, 3, vec(att_out_g, l), True, f"att_gate_bwd_{l}")
        dconv, dgate_c, small["conv_out_g"][l] = gate_bwd(dcat, 1, sv["conv"], proj, 6, vec(conv_out_g, l), False, f"conv_gate_bwd_{l}")
        dqs, dks, dvs = [], [], []
        for d in DILATIONS:
            dq, dk, dv = attn_bwd(sv["qh"], sv["kh"], sv["vb"], datt, _to_branch(sv["lse"], d), _to_branch(delta, d), d,
                                  f"attn_bwd_{l}_d{d}")
            dqs.append(dq)
            dks.append(dk)
            dvs.append(dv)
        dq, dk, dv, small["q"][l], small["k"][l] = qk_prep_bwd(dqs, dks, dvs, proj, vec(q_norm_g, l), vec(k_norm_g, l),
                                                                cos2, sin2, f"qk_prep_bwd_{l}")
        dy, small["conv_ln_g"][l], small["conv_ln_b"][l], small["dw_bias"][l] = conv_bwd_ln(
            dconv, wpw_f[l], sv["y"], vec(conv_ln_g, l), vec(conv_ln_b, l), f"conv_bwd_ln_{l}")
        g_wpw[l] = mm_nn(sv["z"].T, dconv, BF16, f"dwpw_{l}")
        da, db, gw = conv_bwd_dw(dy, proj, dwk_f[l], f"conv_bwd_dw_{l}")
        g_dwk[l] = jnp.sum(gw, axis=1)[:CONV_WIDTH]
        dproj = jnp.concatenate([dq, dk, dv, dgate_a, da, db, dgate_c], axis=1)
        g_win[l] = mm_nn(sv["h"].T, dproj, BF16, f"dwin_{l}")
        dx, dxb, small["norm_g"][l] = in_proj_bwd(dproj, win_f[l], sv["x"], vec(norm_g, l), dx, f"in_proj_bwd_{l}")

    return dx, loss_cols, g_win, g_wpw, g_wout, g_dwk, small
```

```python
import jax
import jax.numpy as jnp
from jax import lax
from jax.experimental import pallas as pl
from jax.experimental.pallas import tpu as pltpu

F32 = jnp.float32
BF16 = jnp.bfloat16
SDS = jax.ShapeDtypeStruct
MESH = pl.DeviceIdType.MESH

N_DEV = 8
DEPTH = 2
HEAD_DIM = 128
CONV_WIDTH = 31
CONV_PAD = 32
DILATIONS = (1, 4, 16)
Q_BLOCK = 128
ROPE_THETA = 10000.0
EPS = 1e-6
NEG = -1e30
ADAM_LR, ADAM_B1, ADAM_B2, ADAM_EPS, ADAM_WD, ADAM_STEP = 0.001, 0.9, 0.999, 1e-08, 0.01, 10
LANE = 128
ROW_CHUNK = 64
MIB = 1 << 20
NT = (((1,), (1,)), ((), ()))
TN = (((0,), (0,)), ((), ()))


def _call(body, **kw):
    return pl.pallas_call(body, **kw)


def _params(vmem_mib):
    return pltpu.CompilerParams(vmem_limit_bytes=vmem_mib * MIB)


def _tile(dim, pref, mult):
    t = min(pref, dim)
    while dim % t or t % mult:
        t -= mult
    return t


def _sig(v):
    return jax.nn.sigmoid(v)


def _rstd(v):
    return lax.rsqrt(jnp.mean(v * v, axis=-1, keepdims=True) + EPS)


def _row(tm, cb, c=0):
    return pl.BlockSpec((tm, cb), lambda i: (i, c))


def _full(shape):
    return pl.BlockSpec(shape, lambda i: (0,) * len(shape))


def _halo_prev(tm, cb, c=0):
    k = tm // CONV_PAD
    return pl.BlockSpec((CONV_PAD, cb), lambda i: (jnp.maximum(i * k - 1, 0), c))


def _halo_next(tm, cb, nblk, c=0):
    k = tm // CONV_PAD
    return pl.BlockSpec((CONV_PAD, cb), lambda i: (jnp.minimum((i + 1) * k, nblk - 1), c))


def in_proj(x, g, w, name):
    S, D = x.shape
    N = w.shape[1]
    tm, tn = _tile(S, 512, 8), _tile(N, 1024, LANE)

    def body(x_ref, g_ref, w_ref, o_ref, h_ref):
        @pl.when(pl.program_id(1) == 0)
        def _():
            xf = x_ref[...]
            h_ref[...] = (xf * _rstd(xf) * g_ref[...]).astype(BF16)

        o_ref[...] = jnp.dot(h_ref[...], w_ref[...], preferred_element_type=F32)

    return _call(
        body, name=name, grid=(S // tm, N // tn),
        in_specs=[pl.BlockSpec((tm, D), lambda i, j: (i, 0)), pl.BlockSpec((1, D), lambda i, j: (0, 0)),
                  pl.BlockSpec((D, tn), lambda i, j: (0, j))],
        out_specs=[pl.BlockSpec((tm, tn), lambda i, j: (i, j)), pl.BlockSpec((tm, D), lambda i, j: (i, 0))],
        out_shape=[SDS((S, N), F32), SDS((S, D), BF16)],
        compiler_params=_params(40),
    )(x, g, w)


def mm_nn(a, b, out_dtype, name, add=None):
    M, K = a.shape
    N = b.shape[1]
    tm, tn = _tile(M, 512, 8), _tile(N, 1024, LANE)

    def body(*refs):
        a_ref, b_ref = refs[0], refs[1]
        o_ref = refs[-1]
        acc = jnp.dot(a_ref[...], b_ref[...], preferred_element_type=F32)
        if add is not None:
            acc = acc + refs[2][...]
        o_ref[...] = acc.astype(out_dtype)

    in_specs = [pl.BlockSpec((tm, K), lambda i, j: (i, 0)), pl.BlockSpec((K, tn), lambda i, j: (0, j))]
    args = [a, b]
    if add is not None:
        in_specs.append(pl.BlockSpec((tm, tn), lambda i, j: (i, j)))
        args.append(add)
    return _call(
        body, name=name, grid=(M // tm, N // tn), in_specs=in_specs,
        out_specs=pl.BlockSpec((tm, tn), lambda i, j: (i, j)), out_shape=SDS((M, N), out_dtype),
        compiler_params=_params(48),
    )(*args)


def mm_nt(a, b, out_dtype, name):
    M, K = a.shape
    N = b.shape[0]
    tm, tn = _tile(M, 512, 8), _tile(N, 1024, LANE)

    def body(a_ref, b_ref, o_ref):
        o_ref[...] = lax.dot_general(a_ref[...], b_ref[...], NT, preferred_element_type=F32).astype(out_dtype)

    return _call(
        body, name=name, grid=(M // tm, N // tn),
        in_specs=[pl.BlockSpec((tm, K), lambda i, j: (i, 0)), pl.BlockSpec((tn, K), lambda i, j: (j, 0))],
        out_specs=pl.BlockSpec((tm, tn), lambda i, j: (i, j)), out_shape=SDS((M, N), out_dtype),
        compiler_params=_params(40),
    )(a, b)


def in_proj_bwd(dproj, w, x, g, dx_out, name):
    S, K = dproj.shape
    D = w.shape[0]
    tm, tk = _tile(S, 256, 8), _tile(K, 1024, LANE)
    nk = K // tk

    def body(dp_ref, w_ref, x_ref, g_ref, dxo_ref, dx_ref, dxb_ref, gacc_ref, acc_ref):
        i, k = pl.program_id(0), pl.program_id(1)

        @pl.when(k == 0)
        def _():
            acc_ref[...] = jnp.zeros_like(acc_ref)

        @pl.when((k == 0) & (i == 0))
        def _():
            gacc_ref[...] = jnp.zeros_like(gacc_ref)

        acc_ref[...] += lax.dot_general(dp_ref[...], w_ref[...], NT, preferred_element_type=F32)

        @pl.when(k == nk - 1)
        def _():
            dh = acc_ref[...]
            xf = x_ref[...]
            r = _rstd(xf)
            n = xf * r
            gacc_ref[...] += jnp.sum(dh * n, axis=0, keepdims=True)
            dn = dh * g_ref[...]
            dx = r * (dn - n * jnp.mean(dn * n, axis=-1, keepdims=True)) + dxo_ref[...]
            dx_ref[...] = dx
            dxb_ref[...] = dx.astype(BF16)

    return _call(
        body, name=name, grid=(S // tm, nk),
        in_specs=[pl.BlockSpec((tm, tk), lambda i, k: (i, k)), pl.BlockSpec((D, tk), lambda i, k: (0, k)),
                  pl.BlockSpec((tm, D), lambda i, k: (i, 0)), pl.BlockSpec((1, D), lambda i, k: (0, 0)),
                  pl.BlockSpec((tm, D), lambda i, k: (i, 0))],
        out_specs=[pl.BlockSpec((tm, D), lambda i, k: (i, 0)), pl.BlockSpec((tm, D), lambda i, k: (i, 0)),
                   pl.BlockSpec((1, D), lambda i, k: (0, 0))],
        out_shape=[SDS((S, D), F32), SDS((S, D), BF16), SDS((1, D), F32)],
        scratch_shapes=[pltpu.VMEM((tm, D), F32)],
        compiler_params=_params(54),
    )(dproj, w, x, g, dx_out)


def _dil_specs(S, DA, tm, dtype):
    specs = [pl.BlockSpec((tm // d, d * DA), lambda i: (i, 0)) for d in DILATIONS]
    shapes = [SDS((S // d, d * DA), dtype) for d in DILATIONS]
    return specs, shapes


def _head_buf(tm, DA):
    return pltpu.VMEM((DA // HEAD_DIM, tm, HEAD_DIM), F32)


def _emit_dilated(buf_ref, dsts, tm, DA):
    for d, dst in zip(DILATIONS, dsts):
        for h in range(DA // HEAD_DIM):
            for r in range(d):
                rows = slice(None) if d == 1 else pl.ds(r, tm // d, stride=d)
                dst[:, r * DA + h * HEAD_DIM:r * DA + (h + 1) * HEAD_DIM] = buf_ref.at[h][rows, :].astype(BF16)


def _collect_dilated(acc_ref, parts, tm, DA):
    for d, p in zip(DILATIONS, parts):
        for h in range(DA // HEAD_DIM):
            for r in range(d):
                part = p[:, r * DA + h * HEAD_DIM:r * DA + (h + 1) * HEAD_DIM].astype(F32)
                if d == 1:
                    acc_ref[h] = part
                else:
                    rows = pl.ds(r, tm // d, stride=d)
                    acc_ref.at[h][rows, :] = acc_ref.at[h][rows, :] + part


def qk_prep(proj, gq, gk, cos2, sin2, name):
    S = proj.shape[0]
    DA = proj.shape[1] // 7
    H = DA // HEAD_DIM
    tm = _tile(S, 256, 16 * DILATIONS[-1])
    nd = len(DILATIONS)

    def body(q_ref, k_ref, v_ref, gq_ref, gk_ref, c_ref, s_ref, *rest):
        outs, buf_ref = rest[:3 * nd], rest[3 * nd]
        ct, st = c_ref[...], s_ref[...]
        for t, (src, g_ref) in enumerate(((q_ref, gq_ref), (k_ref, gk_ref))):
            gain = g_ref[...]
            for h in range(H):
                sl = slice(h * HEAD_DIM, (h + 1) * HEAD_DIM)
                xh = src[:, sl]
                n = xh * _rstd(xh) * gain
                buf_ref[h] = n * ct + pltpu.roll(n, HEAD_DIM // 2, 1) * st
            _emit_dilated(buf_ref, outs[t * nd:(t + 1) * nd], tm, DA)
        for h in range(H):
            buf_ref[h] = v_ref[:, h * HEAD_DIM:(h + 1) * HEAD_DIM]
        _emit_dilated(buf_ref, outs[2 * nd:], tm, DA)

    specs, shapes = _dil_specs(S, DA, tm, BF16)
    outs = _call(
        body, name=name, grid=(S // tm,),
        in_specs=[_row(tm, DA, 0), _row(tm, DA, 1), _row(tm, DA, 2), _full((1, HEAD_DIM)), _full((1, HEAD_DIM)),
                  _row(tm, HEAD_DIM), _row(tm, HEAD_DIM)],
        out_specs=specs * 3, out_shape=shapes * 3, scratch_shapes=[_head_buf(tm, DA)],
        compiler_params=_params(48),
    )(proj, proj, proj, gq, gk, cos2, sin2)
    return outs[:nd], outs[nd:2 * nd], outs[2 * nd:]


def qk_prep_bwd(dqs, dks, dvs, proj, gq, gk, cos2, sin2, name):
    S = proj.shape[0]
    DA = proj.shape[1] // 7
    H = DA // HEAD_DIM
    tm = _tile(S, 256, 16 * DILATIONS[-1])
    nb = len(dqs)

    def body(*refs):
        dq_refs, dk_refs, dv_refs = refs[:nb], refs[nb:2 * nb], refs[2 * nb:3 * nb]
        q_ref, k_ref, gq_ref, gk_ref, c_ref, s_ref = refs[3 * nb:3 * nb + 6]
        dqo_ref, dko_ref, dvo_ref, gqa_ref, gka_ref, acc_ref = refs[3 * nb + 6:]
        ct, st = c_ref[...], s_ref[...]

        @pl.when(pl.program_id(0) == 0)
        def _():
            gqa_ref[...] = jnp.zeros_like(gqa_ref)
            gka_ref[...] = jnp.zeros_like(gka_ref)

        for parts, x_ref, g_ref, dst, gacc in ((dq_refs, q_ref, gq_ref, dqo_ref, gqa_ref),
                                               (dk_refs, k_ref, gk_ref, dko_ref, gka_ref)):
            gain = g_ref[...]
            gsum = jnp.zeros((1, HEAD_DIM), F32)
            _collect_dilated(acc_ref, parts, tm, DA)
            for h in range(H):
                sl = slice(h * HEAD_DIM, (h + 1) * HEAD_DIM)
                dout = acc_ref[h]
                dn = dout * ct + pltpu.roll(dout * st, HEAD_DIM // 2, 1)
                xh = x_ref[:, sl]
                r = _rstd(xh)
                xn = xh * r
                gsum = gsum + jnp.sum(dn * xn, axis=0, keepdims=True)
                dnn = dn * gain
                dst[:, sl] = (r * (dnn - xn * jnp.mean(dnn * xn, axis=-1, keepdims=True))).astype(BF16)
            gacc[...] += gsum
        _collect_dilated(acc_ref, dv_refs, tm, DA)
        for h in range(H):
            dvo_ref[:, h * HEAD_DIM:(h + 1) * HEAD_DIM] = acc_ref[h].astype(BF16)

    specs, _ = _dil_specs(S, DA, tm, BF16)
    return _call(
        body, name=name, grid=(S // tm,),
        in_specs=specs * 3 + [_row(tm, DA, 0), _row(tm, DA, 1), _full((1, HEAD_DIM)),
                              _full((1, HEAD_DIM)), _row(tm, HEAD_DIM), _row(tm, HEAD_DIM)],
        out_specs=[_row(tm, DA)] * 3 + [_full((1, HEAD_DIM))] * 2,
        out_shape=[SDS((S, DA), BF16)] * 3 + [SDS((1, HEAD_DIM), F32)] * 2,
        scratch_shapes=[_head_buf(tm, DA)],
        compiler_params=_params(48),
    )(*dqs, *dks, *dvs, proj, proj, gq, gk, cos2, sin2)


def _band_masks(n):
    row = lax.broadcasted_iota(jnp.int32, (Q_BLOCK, Q_BLOCK), 0)
    col = lax.broadcasted_iota(jnp.int32, (Q_BLOCK, Q_BLOCK), 1)
    first = jnp.where(n > 0, 0, Q_BLOCK + 1)
    return col <= row, (col - row) >= first


def attn_fwd(qh, kh, vb, d, name):
    L = qh.shape[0]
    DA = qh.shape[1] // d
    H = DA // HEAD_DIM
    nb = L // Q_BLOCK
    scale = HEAD_DIM ** -0.5
    view = (L, d * DA)

    def body(q_ref, kc_ref, kp_ref, vc_ref, vp_ref, o_ref, lse_ref):
        mask_c, mask_p = _band_masks(pl.program_id(1))
        for h in range(H):
            sl = slice(h * HEAD_DIM, (h + 1) * HEAD_DIM)
            q = q_ref[:, sl]
            sc = lax.dot_general(q, kc_ref[:, sl], NT, preferred_element_type=F32) * scale
            sp = lax.dot_general(q, kp_ref[:, sl], NT, preferred_element_type=F32) * scale
            sc = jnp.where(mask_c, sc, NEG)
            sp = jnp.where(mask_p, sp, NEG)
            m = jnp.maximum(jnp.max(sc, axis=-1, keepdims=True), jnp.max(sp, axis=-1, keepdims=True))
            pc = jnp.exp(sc - m)
            pp = jnp.exp(sp - m)
            l = jnp.sum(pc, axis=-1, keepdims=True) + jnp.sum(pp, axis=-1, keepdims=True)
            o = jnp.dot(pc.astype(BF16), vc_ref[:, sl], preferred_element_type=F32)
            o = o + jnp.dot(pp.astype(BF16), vp_ref[:, sl], preferred_element_type=F32)
            o_ref[:, sl] = o / l
            lse_ref[0, :, h:h + 1] = m + jnp.log(l)

    cur = pl.BlockSpec((Q_BLOCK, DA), lambda r, n: (n, r))
    prev = pl.BlockSpec((Q_BLOCK, DA), lambda r, n: (jnp.maximum(n - 1, 0), r))
    o, lse = _call(
        body, name=name, grid=(d, nb), in_specs=[cur, cur, prev, cur, prev],
        out_specs=[cur, pl.BlockSpec((1, Q_BLOCK, H), lambda r, n: (r, n, 0))],
        out_shape=[SDS(view, F32), SDS((d, L, H), F32)],
        compiler_params=_params(32),
    )(qh, kh, kh, vb, vb)
    return o, lse


def attn_bwd(qh, kh, vb, da, lse_d, delta_d, d, name):
    L = qh.shape[0]
    DA = qh.shape[1] // d
    H = DA // HEAD_DIM
    nb = L // Q_BLOCK
    scale = HEAD_DIM ** -0.5
    view = (L, d * DA)

    def body(q_ref, kc_ref, kp_ref, vc_ref, vp_ref, do_ref, lse_ref, dl_ref, dq_ref, dk_ref, dv_ref, dkc_ref, dvc_ref):
        n = pl.program_id(1)
        mask_c, mask_p = _band_masks(n)

        @pl.when(n == 0)
        def _():
            dkc_ref[...] = jnp.zeros_like(dkc_ref)
            dvc_ref[...] = jnp.zeros_like(dvc_ref)

        @pl.when(n < nb)
        def _():
            for h in range(H):
                sl = slice(h * HEAD_DIM, (h + 1) * HEAD_DIM)
                q, kc, kp, vc, vp, do = q_ref[:, sl], kc_ref[:, sl], kp_ref[:, sl], vc_ref[:, sl], vp_ref[:, sl], do_ref[:, sl]
                lse = lse_ref[0, :, h:h + 1]
                dl = dl_ref[0, :, h:h + 1]
                sc = lax.dot_general(q, kc, NT, preferred_element_type=F32) * scale
                sp = lax.dot_general(q, kp, NT, preferred_element_type=F32) * scale
                pc = jnp.exp(jnp.where(mask_c, sc, NEG) - lse)
                pp = jnp.exp(jnp.where(mask_p, sp, NEG) - lse)
                dpc = lax.dot_general(do, vc, NT, preferred_element_type=F32)
                dpp = lax.dot_general(do, vp, NT, preferred_element_type=F32)
                dsc = (pc * (dpc - dl) * scale).astype(BF16)
                dsp = (pp * (dpp - dl) * scale).astype(BF16)
                dq = jnp.dot(dsc, kc, preferred_element_type=F32) + jnp.dot(dsp, kp, preferred_element_type=F32)
                dq_ref[:, sl] = dq.astype(BF16)
                dk_prev = lax.dot_general(dsp, q, TN, preferred_element_type=F32)
                dv_prev = lax.dot_general(pp.astype(BF16), do, TN, preferred_element_type=F32)
                dk_ref[:, sl] = (dkc_ref[:, sl] + dk_prev).astype(BF16)
                dv_ref[:, sl] = (dvc_ref[:, sl] + dv_prev).astype(BF16)
                dkc_ref[:, sl] = lax.dot_general(dsc, q, TN, preferred_element_type=F32)
                dvc_ref[:, sl] = lax.dot_general(pc.astype(BF16), do, TN, preferred_element_type=F32)

        @pl.when(n == nb)
        def _():
            dk_ref[...] = dkc_ref[...].astype(BF16)
            dv_ref[...] = dvc_ref[...].astype(BF16)

    cur = pl.BlockSpec((Q_BLOCK, DA), lambda r, n: (jnp.minimum(n, nb - 1), r))
    prev = pl.BlockSpec((Q_BLOCK, DA), lambda r, n: (jnp.clip(n - 1, 0, nb - 1), r))
    late = pl.BlockSpec((Q_BLOCK, DA), lambda r, n: (jnp.maximum(n - 1, 0), r))
    stat = pl.BlockSpec((1, Q_BLOCK, H), lambda r, n: (r, jnp.minimum(n, nb - 1), 0))
    dq, dk, dv = _call(
        body, name=name, grid=(d, nb + 1), in_specs=[cur, cur, prev, cur, prev, cur, stat, stat],
        out_specs=[cur, late, late], out_shape=[SDS(view, BF16)] * 3,
        scratch_shapes=[pltpu.VMEM((Q_BLOCK, DA), F32), pltpu.VMEM((Q_BLOCK, DA), F32)],
        compiler_params=_params(32),
    )(qh, kh, kh, vb, vb, da, lse_d, delta_d)
    return dq, dk, dv


def _to_branch(stat, d):
    S, H = stat.shape
    return stat.reshape(S // d, d, H).transpose(1, 0, 2)


def _from_branch(stat):
    d, L, H = stat.shape
    return stat.transpose(1, 0, 2).reshape(L * d, H)


def att_combine(os_, lses, proj, gain, name):
    S, DA = os_[0].shape
    H = DA // HEAD_DIM
    tm = _tile(S, 256, 16 * DILATIONS[-1])
    nb = len(os_)

    def body(*refs):
        o_views, l_refs = refs[:nb], refs[nb:2 * nb]
        gate_ref, gain_ref, att_ref, y_ref, lse_ref = refs[2 * nb:2 * nb + 5]
        bufs = refs[2 * nb + 5:]
        for d, view, buf in zip(DILATIONS[1:], o_views[1:], bufs):
            for h in range(H):
                for r in range(d):
                    buf.at[h][pl.ds(r, tm // d, stride=d), :] = view[:, r * DA + h * HEAD_DIM:r * DA + (h + 1) * HEAD_DIM]
        ls = [r[...] for r in l_refs]
        top = ls[0]
        for l in ls[1:]:
            top = jnp.maximum(top, l)
        den = jnp.exp(ls[0] - top)
        for l in ls[1:]:
            den = den + jnp.exp(l - top)
        lse = top + jnp.log(den)
        lse_ref[...] = lse
        ws = [jnp.exp(l - lse) for l in ls]
        for h in range(H):
            sl = slice(h * HEAD_DIM, (h + 1) * HEAD_DIM)
            acc = ws[0][:, h:h + 1] * o_views[0][:, sl]
            for w, buf in zip(ws[1:], bufs):
                acc = acc + w[:, h:h + 1] * buf[h]
            att_ref[:, sl] = acc
        a = att_ref[...]
        g = gate_ref[...]
        y_ref[...] = (a * _rstd(a) * gain_ref[...] * (g * _sig(g))).astype(BF16)

    specs, _ = _dil_specs(S, DA, tm, F32)
    return _call(
        body, name=name, grid=(S // tm,),
        in_specs=specs + [_row(tm, H)] * nb + [_row(tm, DA, 3), _full((1, DA))],
        out_specs=[_row(tm, DA), _row(tm, DA), _row(tm, H)],
        out_shape=[SDS((S, DA), F32), SDS((S, DA), BF16), SDS((S, H), F32)],
        scratch_shapes=[_head_buf(tm, DA)] * (nb - 1),
        compiler_params=_params(48),
    )(*os_, *lses, proj, gain)


def gate_bwd(dcat, cblk, a, proj, gate_blk, gain, dilated, name):
    S, DA = a.shape
    H = DA // HEAD_DIM
    tm = _tile(S, 256, 16 * DILATIONS[-1])
    nd = len(DILATIONS) if dilated else 1

    def body(dy_ref, a_ref, gate_ref, gain_ref, dg_ref, gacc_ref, *rest):
        @pl.when(pl.program_id(0) == 0)
        def _():
            gacc_ref[...] = jnp.zeros_like(gacc_ref)

        dy, av, g, gain_v = dy_ref[...], a_ref[...], gate_ref[...], gain_ref[...]
        r = _rstd(av)
        n = av * r
        sg = _sig(g)
        dg_ref[...] = (dy * (n * gain_v) * (sg * (1.0 + g * (1.0 - sg)))).astype(BF16)
        drn = dy * (g * sg)
        gacc_ref[...] += jnp.sum(drn * n, axis=0, keepdims=True)
        dn = drn * gain_v
        da = r * (dn - n * jnp.mean(dn * n, axis=-1, keepdims=True))
        if dilated:
            da_refs, delta_ref, buf_ref = rest[:nd], rest[nd], rest[nd + 1]
            for h in range(H):
                buf_ref[h] = da[:, h * HEAD_DIM:(h + 1) * HEAD_DIM]
            _emit_dilated(buf_ref, da_refs, tm, DA)
            prod = da * av
            for h in range(H):
                delta_ref[:, h:h + 1] = jnp.sum(prod[:, h * HEAD_DIM:(h + 1) * HEAD_DIM], axis=-1, keepdims=True)
        else:
            rest[0][...] = da.astype(BF16)

    out_specs = [_row(tm, DA), _full((1, DA))]
    out_shape = [SDS((S, DA), BF16), SDS((1, DA), F32)]
    scratch = []
    if dilated:
        specs, shapes = _dil_specs(S, DA, tm, BF16)
        out_specs += specs + [_row(tm, H)]
        out_shape += shapes + [SDS((S, H), F32)]
        scratch = [_head_buf(tm, DA)]
    else:
        out_specs.append(_row(tm, DA))
        out_shape.append(SDS((S, DA), BF16))
    return _call(
        body, name=name, grid=(S // tm,),
        in_specs=[_row(tm, DA, cblk), _row(tm, DA), _row(tm, DA, gate_blk), _full((1, DA))],
        out_specs=out_specs, out_shape=out_shape, scratch_shapes=scratch, compiler_params=_params(48),
    )(dcat, a, proj, gain)


def _fill_u(i, a_ref, b_ref, ah_ref, bh_ref, uext_ref, tm):
    uext_ref[pl.ds(CONV_PAD, tm), :] = a_ref[...] * _sig(b_ref[...])
    uh = ah_ref[...] * _sig(bh_ref[...])
    uext_ref[pl.ds(0, CONV_PAD), :] = jnp.where(i > 0, uh, 0.0)


def conv_fwd(proj, wk, bias, ln_g, ln_b, out_g, wpw, name):
    S = proj.shape[0]
    DC = proj.shape[1] // 7
    tm = _tile(S, 128, ROW_CHUNK)
    lead = CONV_PAD - (CONV_WIDTH - 1)

    def body(a_ref, b_ref, ah_ref, bh_ref, gate_ref, wk_ref, bias_ref, lg_ref, lb_ref, og_ref, wpw_ref,
             cy_ref, z_ref, conv_ref, y_ref, uext_ref):
        _fill_u(pl.program_id(0), a_ref, b_ref, ah_ref, bh_ref, uext_ref, tm)

        def cols(cc, carry):
            c0 = pl.multiple_of(cc * LANE, LANE)
            for rr in range(tm // ROW_CHUNK):
                acc = jnp.broadcast_to(bias_ref[:, pl.ds(c0, LANE)], (ROW_CHUNK, LANE))
                for j in range(CONV_WIDTH):
                    acc = acc + wk_ref[j:j + 1, pl.ds(c0, LANE)] * uext_ref[pl.ds(rr * ROW_CHUNK + lead + j, ROW_CHUNK), pl.ds(c0, LANE)]
                y_ref[pl.ds(rr * ROW_CHUNK, ROW_CHUNK), pl.ds(c0, LANE)] = acc
            return carry

        lax.fori_loop(0, DC // LANE, cols, 0)
        y = y_ref[...]
        yc = y - jnp.mean(y, axis=-1, keepdims=True)
        ln = yc * _rstd(yc) * lg_ref[...] + lb_ref[...]
        zb = (ln * _sig(ln)).astype(BF16)
        z_ref[...] = zb
        conv = jnp.dot(zb, wpw_ref[...], preferred_element_type=F32)
        conv_ref[...] = conv
        g = gate_ref[...]
        cy_ref[...] = (conv * _rstd(conv) * og_ref[...] * (g * _sig(g))).astype(BF16)

    vec = _full((1, DC))
    return _call(
        body, name=name, grid=(S // tm,),
        in_specs=[_row(tm, DC, 4), _row(tm, DC, 5), _halo_prev(tm, DC, 4), _halo_prev(tm, DC, 5), _row(tm, DC, 6),
                  _full((CONV_PAD, DC)), vec, vec, vec, vec, _full((DC, DC))],
        out_specs=[_row(tm, DC)] * 4,
        out_shape=[SDS((S, DC), BF16), SDS((S, DC), BF16), SDS((S, DC), F32), SDS((S, DC), F32)],
        scratch_shapes=[pltpu.VMEM((CONV_PAD + tm, DC), F32)],
        compiler_params=_params(48),
    )(proj, proj, proj, proj, proj, wk, bias, ln_g, ln_b, out_g, wpw)


def conv_bwd_ln(dconv, wpw, y, ln_g, ln_b, name):
    S, DC = y.shape
    tm = _tile(S, 256, 8)

    def body(dc_ref, wpw_ref, y_ref, lg_ref, lb_ref, dy_ref, glg_ref, glb_ref, gb_ref):
        @pl.when(pl.program_id(0) == 0)
        def _():
            glg_ref[...] = jnp.zeros_like(glg_ref)
            glb_ref[...] = jnp.zeros_like(glb_ref)
            gb_ref[...] = jnp.zeros_like(gb_ref)

        dz = lax.dot_general(dc_ref[...], wpw_ref[...], NT, preferred_element_type=F32)
        yv = y_ref[...]
        yc = yv - jnp.mean(yv, axis=-1, keepdims=True)
        rstd = _rstd(yc)
        yhat = yc * rstd
        ln = yhat * lg_ref[...] + lb_ref[...]
        sg = _sig(ln)
        dln = dz * (sg * (1.0 + ln * (1.0 - sg)))
        glb_ref[...] += jnp.sum(dln, axis=0, keepdims=True)
        glg_ref[...] += jnp.sum(dln * yhat, axis=0, keepdims=True)
        dyh = dln * lg_ref[...]
        dy = rstd * (dyh - jnp.mean(dyh, axis=-1, keepdims=True) - yhat * jnp.mean(dyh * yhat, axis=-1, keepdims=True))
        dy_ref[...] = dy
        gb_ref[...] += jnp.sum(dy, axis=0, keepdims=True)

    vec = _full((1, DC))
    return _call(
        body, name=name, grid=(S // tm,),
        in_specs=[_row(tm, DC), _full((DC, DC)), _row(tm, DC), vec, vec],
        out_specs=[_row(tm, DC), vec, vec, vec],
        out_shape=[SDS((S, DC), F32)] + [SDS((1, DC), F32)] * 3,
        compiler_params=_params(48),
    )(dconv, wpw, y, ln_g, ln_b)


def conv_bwd_dw(dy, proj, wk, name):
    S, DC = dy.shape
    tm = _tile(S, 128, ROW_CHUNK)
    nsteps = S // tm
    lead = CONV_PAD - (CONV_WIDTH - 1)
    groups = ROW_CHUNK // 8

    def body(dy_ref, dyn_ref, a_ref, b_ref, ah_ref, bh_ref, wk_ref, da_ref, db_ref, gw_ref, uext_ref, dyext_ref, du_ref):
        i = pl.program_id(0)

        @pl.when(i == 0)
        def _():
            gw_ref[...] = jnp.zeros_like(gw_ref)

        _fill_u(i, a_ref, b_ref, ah_ref, bh_ref, uext_ref, tm)
        dyext_ref[pl.ds(0, tm), :] = dy_ref[...]
        dyext_ref[pl.ds(tm, CONV_PAD), :] = jnp.where(i < nsteps - 1, dyn_ref[...], 0.0)

        def cols(cc, carry):
            c0 = pl.multiple_of(cc * LANE, LANE)
            lanes = pl.ds(c0, LANE)
            for rr in range(tm // ROW_CHUNK):
                acc = jnp.zeros((ROW_CHUNK, LANE), F32)
                for j in range(CONV_WIDTH):
                    acc = acc + wk_ref[j:j + 1, lanes] * dyext_ref[pl.ds(rr * ROW_CHUNK + CONV_WIDTH - 1 - j, ROW_CHUNK), lanes]
                du_ref[pl.ds(rr * ROW_CHUNK, ROW_CHUNK), lanes] = acc
            for j in range(CONV_WIDTH):
                part = jnp.zeros((8, LANE), F32)
                for rr in range(tm // ROW_CHUNK):
                    prod = dyext_ref[pl.ds(rr * ROW_CHUNK, ROW_CHUNK), lanes] * uext_ref[pl.ds(rr * ROW_CHUNK + lead + j, ROW_CHUNK), lanes]
                    for k in range(groups):
                        part = part + prod[8 * k:8 * k + 8]
                gw_ref[j, :, lanes] += part
            return carry

        lax.fori_loop(0, DC // LANE, cols, 0)
        du = du_ref[...]
        sb = _sig(b_ref[...])
        da_ref[...] = (du * sb).astype(BF16)
        db_ref[...] = (du * a_ref[...] * sb * (1.0 - sb)).astype(BF16)

    return _call(
        body, name=name, grid=(nsteps,),
        in_specs=[_row(tm, DC), _halo_next(tm, DC, S // CONV_PAD), _row(tm, DC, 4), _row(tm, DC, 5),
                  _halo_prev(tm, DC, 4), _halo_prev(tm, DC, 5), _full((CONV_PAD, DC))],
        out_specs=[_row(tm, DC), _row(tm, DC), _full((CONV_PAD, 8, DC))],
        out_shape=[SDS((S, DC), BF16), SDS((S, DC), BF16), SDS((CONV_PAD, 8, DC), F32)],
        scratch_shapes=[pltpu.VMEM((CONV_PAD + tm, DC), F32), pltpu.VMEM((tm + CONV_PAD, DC), F32), pltpu.VMEM((tm, DC), F32)],
        compiler_params=_params(40),
    )(dy, dy, proj, proj, proj, proj, wk)


def loss_head(xo, target, name):
    S, D = xo.shape
    tm = _tile(S, 256, 8)

    def body(x_ref, t_ref, dy_ref, dyb_ref, acc_ref):
        @pl.when(pl.program_id(0) == 0)
        def _():
            acc_ref[...] = jnp.zeros_like(acc_ref)

        err = x_ref[...] - t_ref[...]
        dy = err * (1.0 / D)
        dy_ref[...] = dy
        dyb_ref[...] = dy.astype(BF16)
        acc_ref[...] += jnp.sum(err * dy, axis=0, keepdims=True) * 0.5

    return _call(
        body, name=name, grid=(S // tm,), in_specs=[_row(tm, D), _row(tm, D)],
        out_specs=[_row(tm, D), _row(tm, D), _full((1, D))],
        out_shape=[SDS((S, D), F32), SDS((S, D), BF16), SDS((1, D), F32)],
        compiler_params=_params(32),
    )(xo, target)


def _coords():
    x, y, c = lax.axis_index("x"), lax.axis_index("y"), lax.axis_index("c")
    return x, y, c


def _lin(p):
    return 4 * p[0] + 2 * p[1] + p[2]


def _slot(ref, axis, idx, size):
    index = [slice(None)] * len(ref.shape)
    index[axis] = pl.ds(idx * size, size)
    return ref.at[tuple(index)]


def all_gather(blocks, axes, name):
    na = len(blocks)
    sizes = [b.shape[ax] for b, ax in zip(blocks, axes)]
    fulls = [SDS(b.shape[:ax] + (N_DEV * b.shape[ax],) + b.shape[ax + 1:], b.dtype) for b, ax in zip(blocks, axes)]

    def body(*refs):
        in_refs, out_refs = refs[:na], refs[na:2 * na]
        send_sems, recv_sems, local_sems = refs[2 * na:]
        x, y, c = _coords()
        me, sibling = (x, y, c), (x, y, 1 - c)
        chips = [(1 - x, y), (x, 1 - y), (1 - x, 1 - y)]

        def place(a, p):
            return _slot(out_refs[a], axes[a], _lin(p), sizes[a])

        def copy(a, k, block, to, src=None):
            return pltpu.make_async_remote_copy(
                src_ref=place(a, block) if src is None else src, dst_ref=place(a, block),
                send_sem=send_sems.at[a, k], recv_sem=recv_sems.at[a, k], device_id=to, device_id_type=MESH)

        mine = [pltpu.make_async_copy(in_refs[a], place(a, me), local_sems.at[a]) for a in range(na)]
        for cp in mine:
            cp.start()
        first = []
        for a in range(na):
            first.append(copy(a, 0, me, sibling, src=in_refs[a]))
            first += [copy(a, 1 + j, me, (*chip, c), src=in_refs[a]) for j, chip in enumerate(chips)]
        for cp in first:
            cp.start()
        passed = []
        for j, chip in enumerate(chips):
            for a in range(na):
                copy(a, 1 + j, (*chip, c), me).wait_recv()
                cp = copy(a, 4 + j, (*chip, c), sibling)
                cp.start()
                passed.append(cp)
        for a in range(na):
            copy(a, 0, sibling, me).wait_recv()
            for j, chip in enumerate(chips):
                copy(a, 4 + j, (*chip, 1 - c), me).wait_recv()
        for cp in first + passed:
            cp.wait_send()
        for cp in mine:
            cp.wait()

    hbm = pl.BlockSpec(memory_space=pltpu.HBM)
    return _call(
        body, name=name, in_specs=[hbm] * na, out_specs=[hbm] * na, out_shape=fulls,
        scratch_shapes=[pltpu.SemaphoreType.DMA((na, 7)), pltpu.SemaphoreType.DMA((na, 7)), pltpu.SemaphoreType.DMA((na,))],
    )(*blocks)


class _Exchange:
    def __init__(self, gather, srcs, axes, name, after):
        self.gather, self.axes, self.name, self.na = gather, axes, name, len(srcs)
        if gather:
            self.sizes = [s.shape[ax] for s, ax in zip(srcs, axes)]
            lands = [s.shape[:ax] + (N_DEV * s.shape[ax],) + s.shape[ax + 1:] for s, ax in zip(srcs, axes)]
        else:
            self.sizes = [s.shape[ax] // N_DEV for s, ax in zip(srcs, axes)]
            lands = [(N_DEV,) + s.shape[:ax] + (sz,) + s.shape[ax + 1:] for s, ax, sz in zip(srcs, axes, self.sizes)]
        self.kinds = [pltpu.HBM(s.shape, s.dtype) for s in srcs] + [pltpu.HBM(l, s.dtype) for l, s in zip(lands, srcs)]
        lands = [lax.empty(l, s.dtype) for l, s in zip(lands, srcs)]
        after = jnp.zeros((8, LANE), F32) if after is None else after
        self._start([pltpu.with_memory_space_constraint(t, pltpu.HBM) for t in list(srcs) + lands], after)

    def _src(self, a, ref, owner):
        return ref if self.gather else _slot(ref, self.axes[a], _lin(owner), self.sizes[a])

    def _dst(self, a, land, sender):
        return _slot(land, self.axes[a], _lin(sender), self.sizes[a]) if self.gather else land.at[_lin(sender)]

    def _copies(self, refs, send_sems, recv_sems):
        na = self.na
        me = _coords()
        flips = [(k >> 2 & 1, k >> 1 & 1, k & 1) for k in range(1, N_DEV)]
        peers = [tuple(1 - v if f else v for v, f in zip(me, flip)) for flip in flips]
        sends, arrivals = [], []
        for a in range(na):
            for k, peer in enumerate(peers):
                pair = dict(send_sem=send_sems.at[7 * a + k], recv_sem=recv_sems.at[7 * a + k], device_id=peer, device_id_type=MESH)
                sends.append(pltpu.make_async_remote_copy(
                    src_ref=self._src(a, refs[a], peer), dst_ref=self._dst(a, refs[na + a], me), **pair))
                arrivals.append(pltpu.make_async_remote_copy(
                    src_ref=self._src(a, refs[a], me), dst_ref=self._dst(a, refs[na + a], peer), **pair))
        return sends, arrivals

    def _start(self, operands, after):
        na = self.na

        def body(*refs):
            ins = refs[:2 * na]
            send_sems, recv_sems = refs[2 * na + 1], refs[2 * na + 2]
            token_ref, local_sems = refs[4 * na + 3], refs[4 * na + 4]
            me = _coords()
            own = [pltpu.make_async_copy(self._src(a, ins[a], me), self._dst(a, ins[na + a], me), local_sems.at[a])
                   for a in range(na)]
            for cp in own:
                cp.start()
            for cp in self._copies(ins, send_sems, recv_sems)[0]:
                cp.start()
            for cp in own:
                cp.wait()
            token_ref[...] = jnp.zeros_like(token_ref)

        hbm = pl.BlockSpec(memory_space=pltpu.HBM)
        sem = pl.BlockSpec(memory_space=pltpu.SEMAPHORE)
        outs = _call(
            body, name=self.name + "_start",
            in_specs=[hbm] * (2 * na) + [pl.BlockSpec(memory_space=pl.ANY)],
            out_specs=[sem, sem] + [hbm] * (2 * na) + [pl.BlockSpec(memory_space=pltpu.VMEM)],
            out_shape=[pltpu.SemaphoreType.DMA((7 * na,)), pltpu.SemaphoreType.DMA((7 * na,))] + self.kinds + [SDS((8, LANE), F32)],
            input_output_aliases={i: 2 + i for i in range(2 * na)},
            scratch_shapes=[pltpu.SemaphoreType.DMA((na,))],
            compiler_params=pltpu.CompilerParams(has_side_effects=pltpu.SideEffectType.DATAFLOW_SIDE_EFFECTING),
        )(*operands, after)
        self.sems, self.thru, self.token = outs[:2], outs[2:2 + 2 * na], outs[2 + 2 * na][0:1, 0:1]

    def wait(self, after):
        na = self.na

        def body(*refs):
            ins, send_sems, recv_sems = refs[:2 * na], refs[2 * na], refs[2 * na + 1]
            sends, arrivals = self._copies(ins, send_sems, recv_sems)
            for cp in sends:
                cp.wait_send()
            for cp in arrivals:
                cp.wait_recv()

        hbm = pl.BlockSpec(memory_space=pltpu.HBM)
        sem = pl.BlockSpec(memory_space=pltpu.SEMAPHORE)
        outs = _call(
            body, name=self.name + "_wait",
            in_specs=[hbm] * (2 * na) + [sem, sem, pl.BlockSpec(memory_space=pl.ANY)],
            out_specs=[hbm] * (2 * na), out_shape=self.kinds,
            input_output_aliases={i: i for i in range(2 * na)},
            compiler_params=pltpu.CompilerParams(has_side_effects=pltpu.SideEffectType.DATAFLOW_SIDE_EFFECTING),
        )(*self.thru, *self.sems, after)
        return outs[na:]


def adamw(w, m, v, parts, layer, prev, name):
    nl, R, C = w.shape
    tr = _tile(R, 128, 8) if R % 8 == 0 else R

    def body(w_ref, m_ref, v_ref, p_ref, *rest):
        g_ref, d_ref, mo_ref, vo_ref = rest[-4:]
        g = p_ref[0].astype(F32)
        for s in range(1, N_DEV):
            g = g + p_ref[s].astype(F32)
        mn = ADAM_B1 * m_ref[0] + (1.0 - ADAM_B1) * g
        vn = ADAM_B2 * v_ref[0] + (1.0 - ADAM_B2) * (g * g)
        m_hat = mn / (1.0 - ADAM_B1 ** ADAM_STEP)
        v_hat = vn / (1.0 - ADAM_B2 ** ADAM_STEP)
        g_ref[0] = g
        d_ref[0] = -ADAM_LR * (m_hat / (jnp.sqrt(v_hat) + ADAM_EPS) + ADAM_WD * w_ref[0])
        mo_ref[0] = mn
        vo_ref[0] = vn

    row = pl.BlockSpec((1, tr, C), lambda i: (layer, i, 0))
    carried = [] if prev is None else list(prev)
    return _call(
        body, name=name, grid=(R // tr,),
        in_specs=[row, row, row, pl.BlockSpec((N_DEV, tr, C), lambda i: (0, i, 0))] + [pl.BlockSpec(memory_space=pl.ANY)] * len(carried),
        out_specs=[row] * 4, out_shape=[SDS((nl, R, C), F32)] * 4,
        input_output_aliases={4 + k: k for k in range(len(carried))},
        compiler_params=_params(48),
    )(w, m, v, parts, *carried)


def _rope_tables(S):
    inv_freq = 1.0 / (ROPE_THETA ** (jnp.arange(0, HEAD_DIM, 2, dtype=F32) / HEAD_DIM))
    ang = jnp.arange(S, dtype=F32)[:, None] * inv_freq[None, :]
    cos, sin = jnp.cos(ang), jnp.sin(ang)
    return jnp.concatenate([cos, cos], axis=-1), jnp.concatenate([-sin, sin], axis=-1)


def _pack_small(D, norm_g, dw_bias, conv_ln_g, conv_ln_b, att_out_g, conv_out_g, q_norm_g, k_norm_g, extra=None):
    qk = jnp.concatenate([q_norm_g.reshape(-1), k_norm_g.reshape(-1)])
    qk = jnp.pad(qk, (0, D - qk.shape[0])).reshape(1, D)
    zero = jnp.zeros((1, D), F32)
    return jnp.concatenate([norm_g, dw_bias, conv_ln_g, conv_ln_b, att_out_g, conv_out_g, qk, zero,
                            zero if extra is None else extra, zero], axis=0)


def _unpack_small(p):
    rows = [p[2 * i:2 * i + 2] for i in range(6)]
    qk = p[12, :4 * HEAD_DIM].reshape(2, DEPTH, HEAD_DIM)
    return rows + [qk[0], qk[1]]


def kernel(x, norm_g, w_in, q_norm_g, k_norm_g, dw_kernel, dw_bias, conv_ln_g, conv_ln_b, w_pw, att_out_g, conv_out_g, w_out, loss_target, m_norm_g, m_w_in, m_q_norm_g, m_k_norm_g, m_dw_kernel, m_dw_bias, m_conv_ln_g, m_conv_ln_b, m_w_pw, m_att_out_g, m_conv_out_g, m_w_out, v_norm_g, v_w_in, v_q_norm_g, v_k_norm_g, v_dw_kernel, v_dw_bias, v_conv_ln_g, v_conv_ln_b, v_w_pw, v_att_out_g, v_conv_out_g, v_w_out):
    xs = x[0]
    D = xs.shape[1]
    bf = lambda t, l: t[l].astype(BF16)
    win0, wpw0, wout0, dwk_f = all_gather([bf(w_in, 0), bf(w_pw, 0), bf(w_out, 0), dw_kernel], [1, 0, 0, 2], "gather_layer0")
    later = _Exchange(True, [bf(w_in, 1), bf(w_pw, 1), bf(w_out, 1)], [1, 0, 0], "gather_layer1", after=win0)

    def weights(l, cur):
        if l == 0:
            return win0, wpw0, wout0, later.token
        return (*later.wait(cur), None)

    sent = [None] * DEPTH

    def send_grads(l, g_win, g_wpw, g_wout, g_dwk):
        sent[l] = _Exchange(False, [g_win, g_wpw, g_wout, g_dwk], [1, 0, 0, 1], f"scatter_layer{l}", after=None)
        return sent[l].token

    dx, loss_cols, small = local_step(xs, loss_target[0], weights, dwk_f, norm_g, q_norm_g, k_norm_g, dw_bias, conv_ln_g,
                                      conv_ln_b, att_out_g, conv_out_g, send_grads)

    big = ((w_in, m_w_in, v_w_in), (w_pw, m_w_pw, v_w_pw), (w_out, m_w_out, v_w_out), (dw_kernel, m_dw_kernel, v_dw_kernel))
    results = [None] * len(big)
    after = dx
    for l in reversed(range(DEPTH)):
        parts = sent[l].wait(after)
        for i, ((w, m, v), p) in enumerate(zip(big, parts)):
            results[i] = adamw(w, m, v, p, l, results[i], f"adamw_{w.shape[1]}_{w.shape[2]}_{l}")
        after = results[0][3]
    r_win, r_wpw, r_wout, r_dwk = results

    stack = lambda k: jnp.concatenate(small[k], axis=0)
    mine = _pack_small(D, stack("norm_g"), stack("dw_bias"), stack("conv_ln_g"), stack("conv_ln_b"), stack("att_out_g"),
                       stack("conv_out_g"), stack("q"), stack("k"), extra=loss_cols)
    (p_small,) = all_gather([mine], [0], "gather_small")
    pk = lambda n, dw, lg, lb, ao, co, q, k: _pack_small(D, n, dw, lg, lb, ao, co, q, k)[None]
    r_small = adamw(pk(norm_g, dw_bias, conv_ln_g, conv_ln_b, att_out_g, conv_out_g, q_norm_g, k_norm_g),
                    pk(m_norm_g, m_dw_bias, m_conv_ln_g, m_conv_ln_b, m_att_out_g, m_conv_out_g, m_q_norm_g, m_k_norm_g),
                    pk(v_norm_g, v_dw_bias, v_conv_ln_g, v_conv_ln_b, v_att_out_g, v_conv_out_g, v_q_norm_g, v_k_norm_g),
                    p_small.reshape(N_DEV, 16, D), 0, None, "adamw_small")
    r_small = [r[0] for r in r_small]
    loss = jnp.sum(r_small[0][14])

    outs = [loss, dx[None]]
    for i in range(4):
        n_, dwb, lg, lb, ao, co, q_, k_ = _unpack_small(r_small[i])
        outs += [n_, r_win[i], q_, k_, r_dwk[i], dwb, lg, lb, r_wpw[i], ao, co, r_wout[i]]
    return tuple(outs)


def local_step(xs, target, weights, dwk_f, norm_g, q_norm_g, k_norm_g, dw_bias, conv_ln_g, conv_ln_b,
               att_out_g, conv_out_g, send_grads):
    S, D = xs.shape
    cos2, sin2 = _rope_tables(S)
    dwk_f = jnp.pad(dwk_f, ((0, 0), (0, CONV_PAD - CONV_WIDTH), (0, 0)))

    def vec(p, l, zero=None):
        row = p[l].reshape(1, -1)
        return row if zero is None else row + zero

    saved = []
    cur = xs
    for l in range(DEPTH):
        win, wpw, wout, zero = weights(l, cur)
        proj, h = in_proj(cur, vec(norm_g, l, zero), win, f"in_proj_{l}")
        qs, ks, vs = qk_prep(proj, vec(q_norm_g, l), vec(k_norm_g, l), cos2, sin2, f"qk_prep_{l}")
        os_, lses = [], []
        for i, d in enumerate(DILATIONS):
            o, lse = attn_fwd(qs[i], ks[i], vs[i], d, f"attn_fwd_{l}_d{d}")
            os_.append(o)
            lses.append(_from_branch(lse))
        att, att_y, lse = att_combine(os_, lses, proj, vec(att_out_g, l), f"att_combine_{l}")
        conv_y, z, conv, y = conv_fwd(proj, dwk_f[l], vec(dw_bias, l), vec(conv_ln_g, l), vec(conv_ln_b, l),
                                      vec(conv_out_g, l), wpw, f"conv_fwd_{l}")
        cat = jnp.concatenate([att_y, conv_y], axis=1)
        nxt = mm_nn(cat, wout, F32, f"out_proj_{l}", add=cur)
        saved.append(dict(x=cur, proj=proj, h=h, qs=qs, ks=ks, vs=vs, att=att, lse=lse, cat=cat, z=z, conv=conv, y=y,
                          win=win, wpw=wpw, wout=wout))
        cur = nxt

    dx, dxb, loss_cols = loss_head(cur, target, "loss_head")

    small = {k: [None] * DEPTH for k in ("norm_g", "dw_bias", "conv_ln_g", "conv_ln_b", "att_out_g", "conv_out_g", "q", "k")}
    for l in reversed(range(DEPTH)):
        sv = saved[l]
        proj = sv["proj"]
        dcat = mm_nt(dxb, sv["wout"], F32, f"dcat_{l}")
        g_wout = mm_nn(sv["cat"].T, dxb, BF16, f"dwout_{l}")
        dgate_a, small["att_out_g"][l], *datts, delta = gate_bwd(dcat, 0, sv["att"], proj, 3, vec(att_out_g, l), True,
                                                                  f"att_gate_bwd_{l}")
        dgate_c, small["conv_out_g"][l], dconv = gate_bwd(dcat, 1, sv["conv"], proj, 6, vec(conv_out_g, l), False,
                                                          f"conv_gate_bwd_{l}")
        dqs, dks, dvs = [], [], []
        for i, d in enumerate(DILATIONS):
            dq, dk, dv = attn_bwd(sv["qs"][i], sv["ks"][i], sv["vs"][i], datts[i], _to_branch(sv["lse"], d),
                                  _to_branch(delta, d), d, f"attn_bwd_{l}_d{d}")
            dqs.append(dq)
            dks.append(dk)
            dvs.append(dv)
        dq, dk, dv, small["q"][l], small["k"][l] = qk_prep_bwd(dqs, dks, dvs, proj, vec(q_norm_g, l), vec(k_norm_g, l),
                                                                cos2, sin2, f"qk_prep_bwd_{l}")
        dy, small["conv_ln_g"][l], small["conv_ln_b"][l], small["dw_bias"][l] = conv_bwd_ln(
            dconv, sv["wpw"], sv["y"], vec(conv_ln_g, l), vec(conv_ln_b, l), f"conv_bwd_ln_{l}")
        g_wpw = mm_nn(sv["z"].T, dconv, BF16, f"dwpw_{l}")
        da, db, gw = conv_bwd_dw(dy, proj, dwk_f[l], f"conv_bwd_dw_{l}")
        g_dwk = jnp.sum(gw, axis=1)[:CONV_WIDTH]
        dproj = jnp.concatenate([dq, dk, dv, dgate_a, da, db, dgate_c], axis=1)
        g_win = mm_nn(sv["h"].T, dproj, BF16, f"dwin_{l}")
        zero = send_grads(l, g_win, g_wpw, g_wout, g_dwk)
        dx, dxb, small["norm_g"][l] = in_proj_bwd(dproj, sv["win"], sv["x"], vec(norm_g, l, zero), dx, f"in_proj_bwd_{l}")

    return dx, loss_cols, small
```

```python
import jax
import jax.numpy as jnp
from jax import lax
from jax.experimental import pallas as pl
from jax.experimental.pallas import tpu as pltpu

F32 = jnp.float32
BF16 = jnp.bfloat16
SDS = jax.ShapeDtypeStruct
MESH = pl.DeviceIdType.MESH

N_DEV = 8
DEPTH = 2
HEAD_DIM = 128
CONV_WIDTH = 31
CONV_PAD = 32
DILATIONS = (1, 4, 16)
Q_BLOCK = 128
ROPE_THETA = 10000.0
EPS = 1e-6
NEG = -1e30
ADAM_LR, ADAM_B1, ADAM_B2, ADAM_EPS, ADAM_WD, ADAM_STEP = 0.001, 0.9, 0.999, 1e-08, 0.01, 10
LANE = 128
ROW_CHUNK = 64
MIB = 1 << 20
NT = (((1,), (1,)), ((), ()))
TN = (((0,), (0,)), ((), ()))


def _call(body, **kw):
    return pl.pallas_call(body, **kw)


def _params(vmem_mib):
    return pltpu.CompilerParams(vmem_limit_bytes=vmem_mib * MIB)


def _tile(dim, pref, mult):
    t = min(pref, dim)
    while dim % t or t % mult:
        t -= mult
    return t


def _sig(v):
    return jax.nn.sigmoid(v)


def _rstd(v):
    return lax.rsqrt(jnp.mean(v * v, axis=-1, keepdims=True) + EPS)


def _row(tm, cb, c=0):
    return pl.BlockSpec((tm, cb), lambda i: (i, c))


def _full(shape):
    return pl.BlockSpec(shape, lambda i: (0,) * len(shape))


def _halo_prev(tm, cb, c=0):
    k = tm // CONV_PAD
    return pl.BlockSpec((CONV_PAD, cb), lambda i: (jnp.maximum(i * k - 1, 0), c))


def _halo_next(tm, cb, nblk, c=0):
    k = tm // CONV_PAD
    return pl.BlockSpec((CONV_PAD, cb), lambda i: (jnp.minimum((i + 1) * k, nblk - 1), c))


def in_proj(x, g, w, name):
    S, D = x.shape
    N = w.shape[1]
    tm, tn = _tile(S, 512, 8), _tile(N, 1024, LANE)

    def body(x_ref, g_ref, w_ref, o_ref, h_ref):
        @pl.when(pl.program_id(1) == 0)
        def _():
            xf = x_ref[...]
            h_ref[...] = (xf * _rstd(xf) * g_ref[...]).astype(BF16)

        o_ref[...] = jnp.dot(h_ref[...], w_ref[...], preferred_element_type=F32)

    return _call(
        body, name=name, grid=(S // tm, N // tn),
        in_specs=[pl.BlockSpec((tm, D), lambda i, j: (i, 0)), pl.BlockSpec((1, D), lambda i, j: (0, 0)),
                  pl.BlockSpec((D, tn), lambda i, j: (0, j))],
        out_specs=[pl.BlockSpec((tm, tn), lambda i, j: (i, j)), pl.BlockSpec((tm, D), lambda i, j: (i, 0))],
        out_shape=[SDS((S, N), F32), SDS((S, D), BF16)],
        compiler_params=_params(40),
    )(x, g, w)


def mm_nn(a, b, out_dtype, name, add=None):
    M, K = a.shape
    N = b.shape[1]
    tm, tn = _tile(M, 512, 8), _tile(N, 1024, LANE)

    def body(*refs):
        a_ref, b_ref = refs[0], refs[1]
        o_ref = refs[-1]
        acc = jnp.dot(a_ref[...], b_ref[...], preferred_element_type=F32)
        if add is not None:
            acc = acc + refs[2][...]
        o_ref[...] = acc.astype(out_dtype)

    in_specs = [pl.BlockSpec((tm, K), lambda i, j: (i, 0)), pl.BlockSpec((K, tn), lambda i, j: (0, j))]
    args = [a, b]
    if add is not None:
        in_specs.append(pl.BlockSpec((tm, tn), lambda i, j: (i, j)))
        args.append(add)
    return _call(
        body, name=name, grid=(M // tm, N // tn), in_specs=in_specs,
        out_specs=pl.BlockSpec((tm, tn), lambda i, j: (i, j)), out_shape=SDS((M, N), out_dtype),
        compiler_params=_params(48),
    )(*args)


def mm_nt(a, b, out_dtype, name):
    M, K = a.shape
    N = b.shape[0]
    tm, tn = _tile(M, 512, 8), _tile(N, 1024, LANE)

    def body(a_ref, b_ref, o_ref):
        o_ref[...] = lax.dot_general(a_ref[...], b_ref[...], NT, preferred_element_type=F32).astype(out_dtype)

    return _call(
        body, name=name, grid=(M // tm, N // tn),
        in_specs=[pl.BlockSpec((tm, K), lambda i, j: (i, 0)), pl.BlockSpec((tn, K), lambda i, j: (j, 0))],
        out_specs=pl.BlockSpec((tm, tn), lambda i, j: (i, j)), out_shape=SDS((M, N), out_dtype),
        compiler_params=_params(40),
    )(a, b)


def in_proj_bwd(dproj, wt, x, g, dx_out, name):
    S, K = dproj.shape
    D = wt.shape[1]
    tm, tk = _tile(S, 512, 8), _tile(K, 1024, LANE)
    nk = K // tk

    def body(dp_ref, w_ref, x_ref, g_ref, dxo_ref, dx_ref, dxb_ref, gacc_ref):
        i, k = pl.program_id(0), pl.program_id(1)
        part = jnp.dot(dp_ref[...], w_ref[...], preferred_element_type=F32)

        @pl.when(k == 0)
        def _():
            dx_ref[...] = part

        @pl.when(k > 0)
        def _():
            dx_ref[...] += part

        @pl.when((k == 0) & (i == 0))
        def _():
            gacc_ref[...] = jnp.zeros_like(gacc_ref)

        @pl.when(k == nk - 1)
        def _():
            dh = dx_ref[...]
            xf = x_ref[...]
            r = _rstd(xf)
            n = xf * r
            gacc_ref[...] += jnp.sum(dh * n, axis=0, keepdims=True)
            dn = dh * g_ref[...]
            dx = r * (dn - n * jnp.mean(dn * n, axis=-1, keepdims=True)) + dxo_ref[...]
            dx_ref[...] = dx
            dxb_ref[...] = dx.astype(BF16)

    return _call(
        body, name=name, grid=(S // tm, nk),
        in_specs=[pl.BlockSpec((tm, tk), lambda i, k: (i, k)), pl.BlockSpec((tk, D), lambda i, k: (k, 0)),
                  pl.BlockSpec((tm, D), lambda i, k: (i, 0)), pl.BlockSpec((1, D), lambda i, k: (0, 0)),
                  pl.BlockSpec((tm, D), lambda i, k: (i, 0))],
        out_specs=[pl.BlockSpec((tm, D), lambda i, k: (i, 0)), pl.BlockSpec((tm, D), lambda i, k: (i, 0)),
                   pl.BlockSpec((1, D), lambda i, k: (0, 0))],
        out_shape=[SDS((S, D), F32), SDS((S, D), BF16), SDS((1, D), F32)],
        compiler_params=_params(54),
    )(dproj, wt, x, g, dx_out)


def _dil_specs(S, DA, tm, dtype):
    specs = [pl.BlockSpec((tm // d, d * DA), lambda i: (i, 0)) for d in DILATIONS]
    shapes = [SDS((S // d, d * DA), dtype) for d in DILATIONS]
    return specs, shapes


def _head_buf(tm, DA):
    return pltpu.VMEM((DA // HEAD_DIM, tm, HEAD_DIM), F32)


def _emit_dilated(buf_ref, dsts, tm, DA):
    for d, dst in zip(DILATIONS, dsts):
        for h in range(DA // HEAD_DIM):
            for r in range(d):
                rows = slice(None) if d == 1 else pl.ds(r, tm // d, stride=d)
                dst[:, r * DA + h * HEAD_DIM:r * DA + (h + 1) * HEAD_DIM] = buf_ref.at[h][rows, :].astype(BF16)


def _collect_dilated(acc_ref, parts, tm, DA):
    for d, p in zip(DILATIONS, parts):
        for h in range(DA // HEAD_DIM):
            for r in range(d):
                part = p[:, r * DA + h * HEAD_DIM:r * DA + (h + 1) * HEAD_DIM].astype(F32)
                if d == 1:
                    acc_ref[h] = part
                else:
                    rows = pl.ds(r, tm // d, stride=d)
                    acc_ref.at[h][rows, :] = acc_ref.at[h][rows, :] + part


def qk_prep(proj, gq, gk, cos2, sin2, name):
    S = proj.shape[0]
    DA = proj.shape[1] // 7
    H = DA // HEAD_DIM
    tm = _tile(S, 256, 16 * DILATIONS[-1])
    nd = len(DILATIONS)

    def body(q_ref, k_ref, v_ref, gq_ref, gk_ref, c_ref, s_ref, *rest):
        outs, buf_ref = rest[:3 * nd], rest[3 * nd]
        ct, st = c_ref[...], s_ref[...]
        for t, (src, g_ref) in enumerate(((q_ref, gq_ref), (k_ref, gk_ref))):
            gain = g_ref[...]
            for h in range(H):
                sl = slice(h * HEAD_DIM, (h + 1) * HEAD_DIM)
                xh = src[:, sl]
                n = xh * _rstd(xh) * gain
                buf_ref[h] = n * ct + pltpu.roll(n, HEAD_DIM // 2, 1) * st
            _emit_dilated(buf_ref, outs[t * nd:(t + 1) * nd], tm, DA)
        for h in range(H):
            buf_ref[h] = v_ref[:, h * HEAD_DIM:(h + 1) * HEAD_DIM]
        _emit_dilated(buf_ref, outs[2 * nd:], tm, DA)

    specs, shapes = _dil_specs(S, DA, tm, BF16)
    outs = _call(
        body, name=name, grid=(S // tm,),
        in_specs=[_row(tm, DA, 0), _row(tm, DA, 1), _row(tm, DA, 2), _full((1, HEAD_DIM)), _full((1, HEAD_DIM)),
                  _row(tm, HEAD_DIM), _row(tm, HEAD_DIM)],
        out_specs=specs * 3, out_shape=shapes * 3, scratch_shapes=[_head_buf(tm, DA)],
        compiler_params=_params(48),
    )(proj, proj, proj, gq, gk, cos2, sin2)
    return outs[:nd], outs[nd:2 * nd], outs[2 * nd:]


def qk_prep_bwd(dqs, dks, dvs, proj, gq, gk, cos2, sin2, name):
    S = proj.shape[0]
    DA = proj.shape[1] // 7
    H = DA // HEAD_DIM
    tm = _tile(S, 256, 16 * DILATIONS[-1])
    nb = len(dqs)

    def body(*refs):
        dq_refs, dk_refs, dv_refs = refs[:nb], refs[nb:2 * nb], refs[2 * nb:3 * nb]
        q_ref, k_ref, gq_ref, gk_ref, c_ref, s_ref = refs[3 * nb:3 * nb + 6]
        dqo_ref, dko_ref, dvo_ref, gqa_ref, gka_ref, acc_ref = refs[3 * nb + 6:]
        ct, st = c_ref[...], s_ref[...]

        @pl.when(pl.program_id(0) == 0)
        def _():
            gqa_ref[...] = jnp.zeros_like(gqa_ref)
            gka_ref[...] = jnp.zeros_like(gka_ref)

        for parts, x_ref, g_ref, dst, gacc in ((dq_refs, q_ref, gq_ref, dqo_ref, gqa_ref),
                                               (dk_refs, k_ref, gk_ref, dko_ref, gka_ref)):
            gain = g_ref[...]
            gsum = jnp.zeros((1, HEAD_DIM), F32)
            _collect_dilated(acc_ref, parts, tm, DA)
            for h in range(H):
                sl = slice(h * HEAD_DIM, (h + 1) * HEAD_DIM)
                dout = acc_ref[h]
                dn = dout * ct + pltpu.roll(dout * st, HEAD_DIM // 2, 1)
                xh = x_ref[:, sl]
                r = _rstd(xh)
                xn = xh * r
                gsum = gsum + jnp.sum(dn * xn, axis=0, keepdims=True)
                dnn = dn * gain
                dst[:, sl] = (r * (dnn - xn * jnp.mean(dnn * xn, axis=-1, keepdims=True))).astype(BF16)
            gacc[...] += gsum
        _collect_dilated(acc_ref, dv_refs, tm, DA)
        for h in range(H):
            dvo_ref[:, h * HEAD_DIM:(h + 1) * HEAD_DIM] = acc_ref[h].astype(BF16)

    specs, _ = _dil_specs(S, DA, tm, BF16)
    return _call(
        body, name=name, grid=(S // tm,),
        in_specs=specs * 3 + [_row(tm, DA, 0), _row(tm, DA, 1), _full((1, HEAD_DIM)),
                              _full((1, HEAD_DIM)), _row(tm, HEAD_DIM), _row(tm, HEAD_DIM)],
        out_specs=[_row(tm, DA)] * 3 + [_full((1, HEAD_DIM))] * 2,
        out_shape=[SDS((S, DA), BF16)] * 3 + [SDS((1, HEAD_DIM), F32)] * 2,
        scratch_shapes=[_head_buf(tm, DA)],
        compiler_params=_params(48),
    )(*dqs, *dks, *dvs, proj, proj, gq, gk, cos2, sin2)


def _band_masks(n):
    row = lax.broadcasted_iota(jnp.int32, (Q_BLOCK, Q_BLOCK), 0)
    col = lax.broadcasted_iota(jnp.int32, (Q_BLOCK, Q_BLOCK), 1)
    first = jnp.where(n > 0, 0, Q_BLOCK + 1)
    return col <= row, (col - row) >= first


def attn_fwd(qh, kh, vb, d, name):
    L = qh.shape[0]
    DA = qh.shape[1] // d
    H = DA // HEAD_DIM
    nb = L // Q_BLOCK
    scale = HEAD_DIM ** -0.5
    view = (L, d * DA)

    def body(q_ref, kc_ref, kp_ref, vc_ref, vp_ref, o_ref, lse_ref):
        mask_c, mask_p = _band_masks(pl.program_id(1))
        for h in range(H):
            sl = slice(h * HEAD_DIM, (h + 1) * HEAD_DIM)
            q = q_ref[:, sl]
            sc = lax.dot_general(q, kc_ref[:, sl], NT, preferred_element_type=F32) * scale
            sp = lax.dot_general(q, kp_ref[:, sl], NT, preferred_element_type=F32) * scale
            sc = jnp.where(mask_c, sc, NEG)
            sp = jnp.where(mask_p, sp, NEG)
            m = jnp.maximum(jnp.max(sc, axis=-1, keepdims=True), jnp.max(sp, axis=-1, keepdims=True))
            pc = jnp.exp(sc - m)
            pp = jnp.exp(sp - m)
            l = jnp.sum(pc, axis=-1, keepdims=True) + jnp.sum(pp, axis=-1, keepdims=True)
            o = jnp.dot(pc.astype(BF16), vc_ref[:, sl], preferred_element_type=F32)
            o = o + jnp.dot(pp.astype(BF16), vp_ref[:, sl], preferred_element_type=F32)
            o_ref[:, sl] = o / l
            lse_ref[0, :, h:h + 1] = m + jnp.log(l)

    cur = pl.BlockSpec((Q_BLOCK, DA), lambda r, n: (n, r))
    prev = pl.BlockSpec((Q_BLOCK, DA), lambda r, n: (jnp.maximum(n - 1, 0), r))
    o, lse = _call(
        body, name=name, grid=(d, nb), in_specs=[cur, cur, prev, cur, prev],
        out_specs=[cur, pl.BlockSpec((1, Q_BLOCK, H), lambda r, n: (r, n, 0))],
        out_shape=[SDS(view, F32), SDS((d, L, H), F32)],
        compiler_params=_params(32),
    )(qh, kh, kh, vb, vb)
    return o, lse


def attn_bwd(qh, kh, vb, da, lse_d, delta_d, d, name):
    L = qh.shape[0]
    DA = qh.shape[1] // d
    H = DA // HEAD_DIM
    nb = L // Q_BLOCK
    scale = HEAD_DIM ** -0.5
    view = (L, d * DA)

    def body(q_ref, kc_ref, kp_ref, vc_ref, vp_ref, do_ref, lse_ref, dl_ref, dq_ref, dk_ref, dv_ref, dkc_ref, dvc_ref):
        n = pl.program_id(1)
        mask_c, mask_p = _band_masks(n)

        @pl.when(n == 0)
        def _():
            dkc_ref[...] = jnp.zeros_like(dkc_ref)
            dvc_ref[...] = jnp.zeros_like(dvc_ref)

        @pl.when(n < nb)
        def _():
            for h in range(H):
                sl = slice(h * HEAD_DIM, (h + 1) * HEAD_DIM)
                q, kc, kp, vc, vp, do = q_ref[:, sl], kc_ref[:, sl], kp_ref[:, sl], vc_ref[:, sl], vp_ref[:, sl], do_ref[:, sl]
                lse = lse_ref[0, :, h:h + 1]
                dl = dl_ref[0, :, h:h + 1]
                sc = lax.dot_general(q, kc, NT, preferred_element_type=F32) * scale
                sp = lax.dot_general(q, kp, NT, preferred_element_type=F32) * scale
                pc = jnp.exp(jnp.where(mask_c, sc, NEG) - lse)
                pp = jnp.exp(jnp.where(mask_p, sp, NEG) - lse)
                dpc = lax.dot_general(do, vc, NT, preferred_element_type=F32)
                dpp = lax.dot_general(do, vp, NT, preferred_element_type=F32)
                dsc = (pc * (dpc - dl) * scale).astype(BF16)
                dsp = (pp * (dpp - dl) * scale).astype(BF16)
                dq = jnp.dot(dsc, kc, preferred_element_type=F32) + jnp.dot(dsp, kp, preferred_element_type=F32)
                dq_ref[:, sl] = dq.astype(BF16)
                dk_prev = lax.dot_general(dsp, q, TN, preferred_element_type=F32)
                dv_prev = lax.dot_general(pp.astype(BF16), do, TN, preferred_element_type=F32)
                dk_ref[:, sl] = (dkc_ref[:, sl] + dk_prev).astype(BF16)
                dv_ref[:, sl] = (dvc_ref[:, sl] + dv_prev).astype(BF16)
                dkc_ref[:, sl] = lax.dot_general(dsc, q, TN, preferred_element_type=F32)
                dvc_ref[:, sl] = lax.dot_general(pc.astype(BF16), do, TN, preferred_element_type=F32)

        @pl.when(n == nb)
        def _():
            dk_ref[...] = dkc_ref[...].astype(BF16)
            dv_ref[...] = dvc_ref[...].astype(BF16)

    cur = pl.BlockSpec((Q_BLOCK, DA), lambda r, n: (jnp.minimum(n, nb - 1), r))
    prev = pl.BlockSpec((Q_BLOCK, DA), lambda r, n: (jnp.clip(n - 1, 0, nb - 1), r))
    late = pl.BlockSpec((Q_BLOCK, DA), lambda r, n: (jnp.maximum(n - 1, 0), r))
    stat = pl.BlockSpec((1, Q_BLOCK, H), lambda r, n: (r, jnp.minimum(n, nb - 1), 0))
    dq, dk, dv = _call(
        body, name=name, grid=(d, nb + 1), in_specs=[cur, cur, prev, cur, prev, cur, stat, stat],
        out_specs=[cur, late, late], out_shape=[SDS(view, BF16)] * 3,
        scratch_shapes=[pltpu.VMEM((Q_BLOCK, DA), F32), pltpu.VMEM((Q_BLOCK, DA), F32)],
        compiler_params=_params(32),
    )(qh, kh, kh, vb, vb, da, lse_d, delta_d)
    return dq, dk, dv


def _to_branch(stat, d):
    S, H = stat.shape
    return stat.reshape(S // d, d, H).transpose(1, 0, 2)


def _from_branch(stat):
    d, L, H = stat.shape
    return stat.transpose(1, 0, 2).reshape(L * d, H)


def att_combine(os_, lses, proj, gain, name):
    S, DA = os_[0].shape
    H = DA // HEAD_DIM
    tm = _tile(S, 256, 16 * DILATIONS[-1])
    nb = len(os_)

    def body(*refs):
        o_views, l_refs = refs[:nb], refs[nb:2 * nb]
        gate_ref, gain_ref, att_ref, y_ref, lse_ref = refs[2 * nb:2 * nb + 5]
        bufs = refs[2 * nb + 5:]
        for d, view, buf in zip(DILATIONS[1:], o_views[1:], bufs):
            for h in range(H):
                for r in range(d):
                    buf.at[h][pl.ds(r, tm // d, stride=d), :] = view[:, r * DA + h * HEAD_DIM:r * DA + (h + 1) * HEAD_DIM]
        ls = [r[...] for r in l_refs]
        top = ls[0]
        for l in ls[1:]:
            top = jnp.maximum(top, l)
        den = jnp.exp(ls[0] - top)
        for l in ls[1:]:
            den = den + jnp.exp(l - top)
        lse = top + jnp.log(den)
        lse_ref[...] = lse
        ws = [jnp.exp(l - lse) for l in ls]
        for h in range(H):
            sl = slice(h * HEAD_DIM, (h + 1) * HEAD_DIM)
            acc = ws[0][:, h:h + 1] * o_views[0][:, sl]
            for w, buf in zip(ws[1:], bufs):
                acc = acc + w[:, h:h + 1] * buf[h]
            att_ref[:, sl] = acc
        a = att_ref[...]
        g = gate_ref[...]
        y_ref[...] = (a * _rstd(a) * gain_ref[...] * (g * _sig(g))).astype(BF16)

    specs, _ = _dil_specs(S, DA, tm, F32)
    return _call(
        body, name=name, grid=(S // tm,),
        in_specs=specs + [_row(tm, H)] * nb + [_row(tm, DA, 3), _full((1, DA))],
        out_specs=[_row(tm, DA), _row(tm, DA), _row(tm, H)],
        out_shape=[SDS((S, DA), F32), SDS((S, DA), BF16), SDS((S, H), F32)],
        scratch_shapes=[_head_buf(tm, DA)] * (nb - 1),
        compiler_params=_params(48),
    )(*os_, *lses, proj, gain)


def gate_bwd(dcat, cblk, a, proj, gate_blk, gain, dilated, name):
    S, DA = a.shape
    H = DA // HEAD_DIM
    tm = _tile(S, 256, 16 * DILATIONS[-1])
    nd = len(DILATIONS) if dilated else 1

    def body(dy_ref, a_ref, gate_ref, gain_ref, dg_ref, gacc_ref, *rest):
        @pl.when(pl.program_id(0) == 0)
        def _():
            gacc_ref[...] = jnp.zeros_like(gacc_ref)

        dy, av, g, gain_v = dy_ref[...], a_ref[...], gate_ref[...], gain_ref[...]
        r = _rstd(av)
        n = av * r
        sg = _sig(g)
        dg_ref[...] = (dy * (n * gain_v) * (sg * (1.0 + g * (1.0 - sg)))).astype(BF16)
        drn = dy * (g * sg)
        gacc_ref[...] += jnp.sum(drn * n, axis=0, keepdims=True)
        dn = drn * gain_v
        da = r * (dn - n * jnp.mean(dn * n, axis=-1, keepdims=True))
        if dilated:
            da_refs, delta_ref, buf_ref = rest[:nd], rest[nd], rest[nd + 1]
            for h in range(H):
                buf_ref[h] = da[:, h * HEAD_DIM:(h + 1) * HEAD_DIM]
            _emit_dilated(buf_ref, da_refs, tm, DA)
            prod = da * av
            for h in range(H):
                delta_ref[:, h:h + 1] = jnp.sum(prod[:, h * HEAD_DIM:(h + 1) * HEAD_DIM], axis=-1, keepdims=True)
        else:
            rest[0][...] = da.astype(BF16)

    out_specs = [_row(tm, DA), _full((1, DA))]
    out_shape = [SDS((S, DA), BF16), SDS((1, DA), F32)]
    scratch = []
    if dilated:
        specs, shapes = _dil_specs(S, DA, tm, BF16)
        out_specs += specs + [_row(tm, H)]
        out_shape += shapes + [SDS((S, H), F32)]
        scratch = [_head_buf(tm, DA)]
    else:
        out_specs.append(_row(tm, DA))
        out_shape.append(SDS((S, DA), BF16))
    return _call(
        body, name=name, grid=(S // tm,),
        in_specs=[_row(tm, DA, cblk), _row(tm, DA), _row(tm, DA, gate_blk), _full((1, DA))],
        out_specs=out_specs, out_shape=out_shape, scratch_shapes=scratch, compiler_params=_params(48),
    )(dcat, a, proj, gain)


def _fill_u(i, a_ref, b_ref, ah_ref, bh_ref, uext_ref, tm):
    uext_ref[pl.ds(CONV_PAD, tm), :] = a_ref[...] * _sig(b_ref[...])
    uh = ah_ref[...] * _sig(bh_ref[...])
    uext_ref[pl.ds(0, CONV_PAD), :] = jnp.where(i > 0, uh, 0.0)


def conv_fwd(proj, wk, bias, ln_g, ln_b, out_g, wpw, name):
    S = proj.shape[0]
    DC = proj.shape[1] // 7
    tm = _tile(S, 128, ROW_CHUNK)
    lead = CONV_PAD - (CONV_WIDTH - 1)

    def body(a_ref, b_ref, ah_ref, bh_ref, gate_ref, wk_ref, bias_ref, lg_ref, lb_ref, og_ref, wpw_ref,
             cy_ref, z_ref, conv_ref, y_ref, uext_ref):
        _fill_u(pl.program_id(0), a_ref, b_ref, ah_ref, bh_ref, uext_ref, tm)

        def cols(cc, carry):
            c0 = pl.multiple_of(cc * LANE, LANE)
            for rr in range(tm // ROW_CHUNK):
                acc = jnp.broadcast_to(bias_ref[:, pl.ds(c0, LANE)], (ROW_CHUNK, LANE))
                for j in range(CONV_WIDTH):
                    acc = acc + wk_ref[j:j + 1, pl.ds(c0, LANE)] * uext_ref[pl.ds(rr * ROW_CHUNK + lead + j, ROW_CHUNK), pl.ds(c0, LANE)]
                y_ref[pl.ds(rr * ROW_CHUNK, ROW_CHUNK), pl.ds(c0, LANE)] = acc
            return carry

        lax.fori_loop(0, DC // LANE, cols, 0)
        y = y_ref[...]
        yc = y - jnp.mean(y, axis=-1, keepdims=True)
        ln = yc * _rstd(yc) * lg_ref[...] + lb_ref[...]
        zb = (ln * _sig(ln)).astype(BF16)
        z_ref[...] = zb
        conv = jnp.dot(zb, wpw_ref[...], preferred_element_type=F32)
        conv_ref[...] = conv
        g = gate_ref[...]
        cy_ref[...] = (conv * _rstd(conv) * og_ref[...] * (g * _sig(g))).astype(BF16)

    vec = _full((1, DC))
    return _call(
        body, name=name, grid=(S // tm,),
        in_specs=[_row(tm, DC, 4), _row(tm, DC, 5), _halo_prev(tm, DC, 4), _halo_prev(tm, DC, 5), _row(tm, DC, 6),
                  _full((CONV_PAD, DC)), vec, vec, vec, vec, _full((DC, DC))],
        out_specs=[_row(tm, DC)] * 4,
        out_shape=[SDS((S, DC), BF16), SDS((S, DC), BF16), SDS((S, DC), F32), SDS((S, DC), F32)],
        scratch_shapes=[pltpu.VMEM((CONV_PAD + tm, DC), F32)],
        compiler_params=_params(48),
    )(proj, proj, proj, proj, proj, wk, bias, ln_g, ln_b, out_g, wpw)


def conv_bwd_ln(dconv, wpw, y, ln_g, ln_b, name):
    S, DC = y.shape
    tm = _tile(S, 256, 8)

    def body(dc_ref, wpw_ref, y_ref, lg_ref, lb_ref, dy_ref, glg_ref, glb_ref, gb_ref):
        @pl.when(pl.program_id(0) == 0)
        def _():
            glg_ref[...] = jnp.zeros_like(glg_ref)
            glb_ref[...] = jnp.zeros_like(glb_ref)
            gb_ref[...] = jnp.zeros_like(gb_ref)

        dz = lax.dot_general(dc_ref[...], wpw_ref[...], NT, preferred_element_type=F32)
        yv = y_ref[...]
        yc = yv - jnp.mean(yv, axis=-1, keepdims=True)
        rstd = _rstd(yc)
        yhat = yc * rstd
        ln = yhat * lg_ref[...] + lb_ref[...]
        sg = _sig(ln)
        dln = dz * (sg * (1.0 + ln * (1.0 - sg)))
        glb_ref[...] += jnp.sum(dln, axis=0, keepdims=True)
        glg_ref[...] += jnp.sum(dln * yhat, axis=0, keepdims=True)
        dyh = dln * lg_ref[...]
        dy = rstd * (dyh - jnp.mean(dyh, axis=-1, keepdims=True) - yhat * jnp.mean(dyh * yhat, axis=-1, keepdims=True))
        dy_ref[...] = dy
        gb_ref[...] += jnp.sum(dy, axis=0, keepdims=True)

    vec = _full((1, DC))
    return _call(
        body, name=name, grid=(S // tm,),
        in_specs=[_row(tm, DC), _full((DC, DC)), _row(tm, DC), vec, vec],
        out_specs=[_row(tm, DC), vec, vec, vec],
        out_shape=[SDS((S, DC), F32)] + [SDS((1, DC), F32)] * 3,
        compiler_params=_params(48),
    )(dconv, wpw, y, ln_g, ln_b)


def conv_bwd_dw(dy, proj, wk, name):
    S, DC = dy.shape
    tm = _tile(S, 128, ROW_CHUNK)
    nsteps = S // tm
    lead = CONV_PAD - (CONV_WIDTH - 1)
    groups = ROW_CHUNK // 8

    def body(dy_ref, dyn_ref, a_ref, b_ref, ah_ref, bh_ref, wk_ref, da_ref, db_ref, gw_ref, uext_ref, dyext_ref, du_ref):
        i = pl.program_id(0)

        @pl.when(i == 0)
        def _():
            gw_ref[...] = jnp.zeros_like(gw_ref)

        _fill_u(i, a_ref, b_ref, ah_ref, bh_ref, uext_ref, tm)
        dyext_ref[pl.ds(0, tm), :] = dy_ref[...]
        dyext_ref[pl.ds(tm, CONV_PAD), :] = jnp.where(i < nsteps - 1, dyn_ref[...], 0.0)

        def cols(cc, carry):
            c0 = pl.multiple_of(cc * LANE, LANE)
            lanes = pl.ds(c0, LANE)
            for rr in range(tm // ROW_CHUNK):
                acc = jnp.zeros((ROW_CHUNK, LANE), F32)
                for j in range(CONV_WIDTH):
                    acc = acc + wk_ref[j:j + 1, lanes] * dyext_ref[pl.ds(rr * ROW_CHUNK + CONV_WIDTH - 1 - j, ROW_CHUNK), lanes]
                du_ref[pl.ds(rr * ROW_CHUNK, ROW_CHUNK), lanes] = acc
            for j in range(CONV_WIDTH):
                part = jnp.zeros((8, LANE), F32)
                for rr in range(tm // ROW_CHUNK):
                    prod = dyext_ref[pl.ds(rr * ROW_CHUNK, ROW_CHUNK), lanes] * uext_ref[pl.ds(rr * ROW_CHUNK + lead + j, ROW_CHUNK), lanes]
                    for k in range(groups):
                        part = part + prod[8 * k:8 * k + 8]
                gw_ref[j, :, lanes] += part
            return carry

        lax.fori_loop(0, DC // LANE, cols, 0)
        du = du_ref[...]
        sb = _sig(b_ref[...])
        da_ref[...] = (du * sb).astype(BF16)
        db_ref[...] = (du * a_ref[...] * sb * (1.0 - sb)).astype(BF16)

    return _call(
        body, name=name, grid=(nsteps,),
        in_specs=[_row(tm, DC), _halo_next(tm, DC, S // CONV_PAD), _row(tm, DC, 4), _row(tm, DC, 5),
                  _halo_prev(tm, DC, 4), _halo_prev(tm, DC, 5), _full((CONV_PAD, DC))],
        out_specs=[_row(tm, DC), _row(tm, DC), _full((CONV_PAD, 8, DC))],
        out_shape=[SDS((S, DC), BF16), SDS((S, DC), BF16), SDS((CONV_PAD, 8, DC), F32)],
        scratch_shapes=[pltpu.VMEM((CONV_PAD + tm, DC), F32), pltpu.VMEM((tm + CONV_PAD, DC), F32), pltpu.VMEM((tm, DC), F32)],
        compiler_params=_params(40),
    )(dy, dy, proj, proj, proj, proj, wk)


def loss_head(xo, target, name):
    S, D = xo.shape
    tm = _tile(S, 256, 8)

    def body(x_ref, t_ref, dy_ref, dyb_ref, acc_ref):
        @pl.when(pl.program_id(0) == 0)
        def _():
            acc_ref[...] = jnp.zeros_like(acc_ref)

        err = x_ref[...] - t_ref[...]
        dy = err * (1.0 / D)
        dy_ref[...] = dy
        dyb_ref[...] = dy.astype(BF16)
        acc_ref[...] += jnp.sum(err * dy, axis=0, keepdims=True) * 0.5

    return _call(
        body, name=name, grid=(S // tm,), in_specs=[_row(tm, D), _row(tm, D)],
        out_specs=[_row(tm, D), _row(tm, D), _full((1, D))],
        out_shape=[SDS((S, D), F32), SDS((S, D), BF16), SDS((1, D), F32)],
        compiler_params=_params(32),
    )(xo, target)


def _coords():
    x, y, c = lax.axis_index("x"), lax.axis_index("y"), lax.axis_index("c")
    return x, y, c


def _lin(p):
    return 4 * p[0] + 2 * p[1] + p[2]


def _slot(ref, axis, idx, size):
    index = [slice(None)] * len(ref.shape)
    index[axis] = pl.ds(idx * size, size)
    return ref.at[tuple(index)]


def all_gather(blocks, axes, name):
    na = len(blocks)
    sizes = [b.shape[ax] for b, ax in zip(blocks, axes)]
    fulls = [SDS(b.shape[:ax] + (N_DEV * b.shape[ax],) + b.shape[ax + 1:], b.dtype) for b, ax in zip(blocks, axes)]

    def body(*refs):
        in_refs, out_refs = refs[:na], refs[na:2 * na]
        send_sems, recv_sems, local_sems = refs[2 * na:]
        x, y, c = _coords()
        me, sibling = (x, y, c), (x, y, 1 - c)
        chips = [(1 - x, y), (x, 1 - y), (1 - x, 1 - y)]

        def place(a, p):
            return _slot(out_refs[a], axes[a], _lin(p), sizes[a])

        def copy(a, k, block, to, src=None):
            return pltpu.make_async_remote_copy(
                src_ref=place(a, block) if src is None else src, dst_ref=place(a, block),
                send_sem=send_sems.at[a, k], recv_sem=recv_sems.at[a, k], device_id=to, device_id_type=MESH)

        mine = [pltpu.make_async_copy(in_refs[a], place(a, me), local_sems.at[a]) for a in range(na)]
        for cp in mine:
            cp.start()
        first = []
        for a in range(na):
            first.append(copy(a, 0, me, sibling, src=in_refs[a]))
            first += [copy(a, 1 + j, me, (*chip, c), src=in_refs[a]) for j, chip in enumerate(chips)]
        for cp in first:
            cp.start()
        passed = []
        for j, chip in enumerate(chips):
            for a in range(na):
                copy(a, 1 + j, (*chip, c), me).wait_recv()
                cp = copy(a, 4 + j, (*chip, c), sibling)
                cp.start()
                passed.append(cp)
        for a in range(na):
            copy(a, 0, sibling, me).wait_recv()
            for j, chip in enumerate(chips):
                copy(a, 4 + j, (*chip, 1 - c), me).wait_recv()
        for cp in first + passed:
            cp.wait_send()
        for cp in mine:
            cp.wait()

    hbm = pl.BlockSpec(memory_space=pltpu.HBM)
    return _call(
        body, name=name, in_specs=[hbm] * na, out_specs=[hbm] * na, out_shape=fulls,
        scratch_shapes=[pltpu.SemaphoreType.DMA((na, 7)), pltpu.SemaphoreType.DMA((na, 7)), pltpu.SemaphoreType.DMA((na,))],
    )(*blocks)


class _Exchange:
    def __init__(self, gather, srcs, axes, name, after):
        self.gather, self.axes, self.name, self.na = gather, axes, name, len(srcs)
        if gather:
            self.sizes = [s.shape[ax] for s, ax in zip(srcs, axes)]
            lands = [s.shape[:ax] + (N_DEV * s.shape[ax],) + s.shape[ax + 1:] for s, ax in zip(srcs, axes)]
        else:
            self.sizes = [s.shape[ax] // N_DEV for s, ax in zip(srcs, axes)]
            lands = [(N_DEV,) + s.shape[:ax] + (sz,) + s.shape[ax + 1:] for s, ax, sz in zip(srcs, axes, self.sizes)]
        self.kinds = [pltpu.HBM(s.shape, s.dtype) for s in srcs] + [pltpu.HBM(l, s.dtype) for l, s in zip(lands, srcs)]
        lands = [lax.empty(l, s.dtype) for l, s in zip(lands, srcs)]
        after = jnp.zeros((8, LANE), F32) if after is None else after
        self._start([pltpu.with_memory_space_constraint(t, pltpu.HBM) for t in list(srcs) + lands], after)

    def _src(self, a, ref, owner):
        return ref if self.gather else _slot(ref, self.axes[a], _lin(owner), self.sizes[a])

    def _dst(self, a, land, sender):
        return _slot(land, self.axes[a], _lin(sender), self.sizes[a]) if self.gather else land.at[_lin(sender)]

    def _copies(self, refs, send_sems, recv_sems):
        na = self.na
        me = _coords()
        flips = [(k >> 2 & 1, k >> 1 & 1, k & 1) for k in range(1, N_DEV)]
        peers = [tuple(1 - v if f else v for v, f in zip(me, flip)) for flip in flips]
        sends, arrivals = [], []
        for a in range(na):
            for k, peer in enumerate(peers):
                pair = dict(send_sem=send_sems.at[7 * a + k], recv_sem=recv_sems.at[7 * a + k], device_id=peer, device_id_type=MESH)
                sends.append(pltpu.make_async_remote_copy(
                    src_ref=self._src(a, refs[a], peer), dst_ref=self._dst(a, refs[na + a], me), **pair))
                arrivals.append(pltpu.make_async_remote_copy(
                    src_ref=self._src(a, refs[a], me), dst_ref=self._dst(a, refs[na + a], peer), **pair))
        return sends, arrivals

    def _place_own(self, operands):
        na = self.na

        def body(*refs):
            sems = refs[3 * na]
            me = _coords()
            own = [pltpu.make_async_copy(self._src(a, refs[a], me), self._dst(a, refs[na + a], me), sems.at[a]) for a in range(na)]
            for cp in own:
                cp.start()
            for cp in own:
                cp.wait()

        hbm = pl.BlockSpec(memory_space=pltpu.HBM)
        lands = _call(
            body, name=self.name + "_own", in_specs=[hbm] * (2 * na), out_specs=[hbm] * na, out_shape=self.kinds[na:],
            input_output_aliases={na + a: a for a in range(na)}, scratch_shapes=[pltpu.SemaphoreType.DMA((na,))],
        )(*operands)
        return operands[:na] + list(lands)

    def _start(self, operands, after):
        na = self.na
        operands = self._place_own(operands)

        def body(*refs):
            ins = refs[:2 * na]
            send_sems, recv_sems, token_ref = refs[2 * na + 1], refs[2 * na + 2], refs[4 * na + 3]
            for cp in self._copies(ins, send_sems, recv_sems)[0]:
                cp.start()
            token_ref[...] = jnp.zeros_like(token_ref)

        hbm = pl.BlockSpec(memory_space=pltpu.HBM)
        sem = pl.BlockSpec(memory_space=pltpu.SEMAPHORE)
        outs = _call(
            body, name=self.name + "_start",
            in_specs=[hbm] * (2 * na) + [pl.BlockSpec(memory_space=pl.ANY)],
            out_specs=[sem, sem] + [hbm] * (2 * na) + [pl.BlockSpec(memory_space=pltpu.VMEM)],
            out_shape=[pltpu.SemaphoreType.DMA((7 * na,)), pltpu.SemaphoreType.DMA((7 * na,))] + self.kinds + [SDS((8, LANE), F32)],
            input_output_aliases={i: 2 + i for i in range(2 * na)},
            compiler_params=pltpu.CompilerParams(has_side_effects=pltpu.SideEffectType.DATAFLOW_SIDE_EFFECTING),
        )(*operands, after)
        self.sems, self.thru, self.token = outs[:2], outs[2:2 + 2 * na], outs[2 + 2 * na][0:1, 0:1]

    def wait(self, after):
        na = self.na

        def body(*refs):
            ins, send_sems, recv_sems = refs[:2 * na], refs[2 * na], refs[2 * na + 1]
            sends, arrivals = self._copies(ins, send_sems, recv_sems)
            for cp in sends:
                cp.wait_send()
            for cp in arrivals:
                cp.wait_recv()

        hbm = pl.BlockSpec(memory_space=pltpu.HBM)
        sem = pl.BlockSpec(memory_space=pltpu.SEMAPHORE)
        outs = _call(
            body, name=self.name + "_wait",
            in_specs=[hbm] * (2 * na) + [sem, sem, pl.BlockSpec(memory_space=pl.ANY)],
            out_specs=[hbm] * (2 * na), out_shape=self.kinds,
            input_output_aliases={i: i for i in range(2 * na)},
            compiler_params=pltpu.CompilerParams(has_side_effects=pltpu.SideEffectType.DATAFLOW_SIDE_EFFECTING),
        )(*self.thru, *self.sems, after)
        return outs[na:]


def adamw(w, m, v, parts, layer, prev, name):
    nl, R, C = w.shape
    tr = _tile(R, 128, 8) if R % 8 == 0 else R

    def body(w_ref, m_ref, v_ref, p_ref, *rest):
        g_ref, d_ref, mo_ref, vo_ref = rest[-4:]
        g = p_ref[0].astype(F32)
        for s in range(1, N_DEV):
            g = g + p_ref[s].astype(F32)
        mn = ADAM_B1 * m_ref[0] + (1.0 - ADAM_B1) * g
        vn = ADAM_B2 * v_ref[0] + (1.0 - ADAM_B2) * (g * g)
        m_hat = mn / (1.0 - ADAM_B1 ** ADAM_STEP)
        v_hat = vn / (1.0 - ADAM_B2 ** ADAM_STEP)
        g_ref[0] = g
        d_ref[0] = -ADAM_LR * (m_hat / (jnp.sqrt(v_hat) + ADAM_EPS) + ADAM_WD * w_ref[0])
        mo_ref[0] = mn
        vo_ref[0] = vn

    row = pl.BlockSpec((1, tr, C), lambda i: (layer, i, 0))
    carried = [] if prev is None else list(prev)
    return _call(
        body, name=name, grid=(R // tr,),
        in_specs=[row, row, row, pl.BlockSpec((N_DEV, tr, C), lambda i: (0, i, 0))] + [pl.BlockSpec(memory_space=pl.ANY)] * len(carried),
        out_specs=[row] * 4, out_shape=[SDS((nl, R, C), F32)] * 4,
        input_output_aliases={4 + k: k for k in range(len(carried))},
        compiler_params=_params(48),
    )(w, m, v, parts, *carried)


def _rope_tables(S):
    inv_freq = 1.0 / (ROPE_THETA ** (jnp.arange(0, HEAD_DIM, 2, dtype=F32) / HEAD_DIM))
    ang = jnp.arange(S, dtype=F32)[:, None] * inv_freq[None, :]
    cos, sin = jnp.cos(ang), jnp.sin(ang)
    return jnp.concatenate([cos, cos], axis=-1), jnp.concatenate([-sin, sin], axis=-1)


def _pack_small(D, norm_g, dw_bias, conv_ln_g, conv_ln_b, att_out_g, conv_out_g, q_norm_g, k_norm_g, extra=None):
    qk = jnp.concatenate([q_norm_g.reshape(-1), k_norm_g.reshape(-1)])
    qk = jnp.pad(qk, (0, D - qk.shape[0])).reshape(1, D)
    zero = jnp.zeros((1, D), F32)
    return jnp.concatenate([norm_g, dw_bias, conv_ln_g, conv_ln_b, att_out_g, conv_out_g, qk, zero,
                            zero if extra is None else extra, zero], axis=0)


def _unpack_small(p):
    rows = [p[2 * i:2 * i + 2] for i in range(6)]
    qk = p[12, :4 * HEAD_DIM].reshape(2, DEPTH, HEAD_DIM)
    return rows + [qk[0], qk[1]]


def kernel(x, norm_g, w_in, q_norm_g, k_norm_g, dw_kernel, dw_bias, conv_ln_g, conv_ln_b, w_pw, att_out_g, conv_out_g, w_out, loss_target, m_norm_g, m_w_in, m_q_norm_g, m_k_norm_g, m_dw_kernel, m_dw_bias, m_conv_ln_g, m_conv_ln_b, m_w_pw, m_att_out_g, m_conv_out_g, m_w_out, v_norm_g, v_w_in, v_q_norm_g, v_k_norm_g, v_dw_kernel, v_dw_bias, v_conv_ln_g, v_conv_ln_b, v_w_pw, v_att_out_g, v_conv_out_g, v_w_out):
    xs = x[0]
    D = xs.shape[1]
    bf = lambda t, l: t[l].astype(BF16)
    win0, wpw0, wout0, dwk_f = all_gather([bf(w_in, 0), bf(w_pw, 0), bf(w_out, 0), dw_kernel], [1, 0, 0, 2], "gather_layer0")
    later = _Exchange(True, [bf(w_in, 1), bf(w_pw, 1), bf(w_out, 1)], [1, 0, 0], "gather_layer1", after=win0)

    def weights(l, cur):
        if l == 0:
            return win0, wpw0, wout0, later.token
        return (*later.wait(cur), None)

    sent = [None] * DEPTH

    def send_grads(l, g_win, g_wpw, g_wout, g_dwk):
        sent[l] = _Exchange(False, [g_win, g_wpw, g_wout, g_dwk], [1, 0, 0, 1], f"scatter_layer{l}", after=None)
        return sent[l].token

    dx, loss_cols, small = local_step(xs, loss_target[0], weights, dwk_f, norm_g, q_norm_g, k_norm_g, dw_bias, conv_ln_g,
                                      conv_ln_b, att_out_g, conv_out_g, send_grads)

    big = ((w_in, m_w_in, v_w_in), (w_pw, m_w_pw, v_w_pw), (w_out, m_w_out, v_w_out), (dw_kernel, m_dw_kernel, v_dw_kernel))
    results = [None] * len(big)
    after = dx
    for l in reversed(range(DEPTH)):
        parts = sent[l].wait(after)
        for i, ((w, m, v), p) in enumerate(zip(big, parts)):
            results[i] = adamw(w, m, v, p, l, results[i], f"adamw_{w.shape[1]}_{w.shape[2]}_{l}")
        after = results[0][3]
    r_win, r_wpw, r_wout, r_dwk = results

    stack = lambda k: jnp.concatenate(small[k], axis=0)
    mine = _pack_small(D, stack("norm_g"), stack("dw_bias"), stack("conv_ln_g"), stack("conv_ln_b"), stack("att_out_g"),
                       stack("conv_out_g"), stack("q"), stack("k"), extra=loss_cols)
    (p_small,) = all_gather([mine], [0], "gather_small")
    pk = lambda n, dw, lg, lb, ao, co, q, k: _pack_small(D, n, dw, lg, lb, ao, co, q, k)[None]
    r_small = adamw(pk(norm_g, dw_bias, conv_ln_g, conv_ln_b, att_out_g, conv_out_g, q_norm_g, k_norm_g),
                    pk(m_norm_g, m_dw_bias, m_conv_ln_g, m_conv_ln_b, m_att_out_g, m_conv_out_g, m_q_norm_g, m_k_norm_g),
                    pk(v_norm_g, v_dw_bias, v_conv_ln_g, v_conv_ln_b, v_att_out_g, v_conv_out_g, v_q_norm_g, v_k_norm_g),
                    p_small.reshape(N_DEV, 16, D), 0, None, "adamw_small")
    r_small = [r[0] for r in r_small]
    loss = jnp.sum(r_small[0][14])

    outs = [loss, dx[None]]
    for i in range(4):
        n_, dwb, lg, lb, ao, co, q_, k_ = _unpack_small(r_small[i])
        outs += [n_, r_win[i], q_, k_, r_dwk[i], dwb, lg, lb, r_wpw[i], ao, co, r_wout[i]]
    return tuple(outs)


def local_step(xs, target, weights, dwk_f, norm_g, q_norm_g, k_norm_g, dw_bias, conv_ln_g, conv_ln_b,
               att_out_g, conv_out_g, send_grads):
    S, D = xs.shape
    cos2, sin2 = _rope_tables(S)
    dwk_f = jnp.pad(dwk_f, ((0, 0), (0, CONV_PAD - CONV_WIDTH), (0, 0)))

    def vec(p, l, zero=None):
        row = p[l].reshape(1, -1)
        return row if zero is None else row + zero

    saved = []
    cur = xs
    for l in range(DEPTH):
        win, wpw, wout, zero = weights(l, cur)
        proj, h = in_proj(cur, vec(norm_g, l, zero), win, f"in_proj_{l}")
        qs, ks, vs = qk_prep(proj, vec(q_norm_g, l), vec(k_norm_g, l), cos2, sin2, f"qk_prep_{l}")
        os_, lses = [], []
        for i, d in enumerate(DILATIONS):
            o, lse = attn_fwd(qs[i], ks[i], vs[i], d, f"attn_fwd_{l}_d{d}")
            os_.append(o)
            lses.append(_from_branch(lse))
        att, att_y, lse = att_combine(os_, lses, proj, vec(att_out_g, l), f"att_combine_{l}")
        conv_y, z, conv, y = conv_fwd(proj, dwk_f[l], vec(dw_bias, l), vec(conv_ln_g, l), vec(conv_ln_b, l),
                                      vec(conv_out_g, l), wpw, f"conv_fwd_{l}")
        cat = jnp.concatenate([att_y, conv_y], axis=1)
        nxt = mm_nn(cat, wout, F32, f"out_proj_{l}", add=cur)
        saved.append(dict(x=cur, proj=proj, h=h, qs=qs, ks=ks, vs=vs, att=att, lse=lse, cat=cat, z=z, conv=conv, y=y,
                          win=win, wpw=wpw, wout=wout))
        cur = nxt

    dx, dxb, loss_cols = loss_head(cur, target, "loss_head")

    small = {k: [None] * DEPTH for k in ("norm_g", "dw_bias", "conv_ln_g", "conv_ln_b", "att_out_g", "conv_out_g", "q", "k")}
    for l in reversed(range(DEPTH)):
        sv = saved[l]
        proj = sv["proj"]
        dcat = mm_nt(dxb, sv["wout"], F32, f"dcat_{l}")
        g_wout = mm_nn(sv["cat"].T, dxb, BF16, f"dwout_{l}")
        dgate_a, small["att_out_g"][l], *datts, delta = gate_bwd(dcat, 0, sv["att"], proj, 3, vec(att_out_g, l), True,
                                                                  f"att_gate_bwd_{l}")
        dgate_c, small["conv_out_g"][l], dconv = gate_bwd(dcat, 1, sv["conv"], proj, 6, vec(conv_out_g, l), False,
                                                          f"conv_gate_bwd_{l}")
        dqs, dks, dvs = [], [], []
        for i, d in enumerate(DILATIONS):
            dq, dk, dv = attn_bwd(sv["qs"][i], sv["ks"][i], sv["vs"][i], datts[i], _to_branch(sv["lse"], d),
                                  _to_branch(delta, d), d, f"attn_bwd_{l}_d{d}")
            dqs.append(dq)
            dks.append(dk)
            dvs.append(dv)
        dq, dk, dv, small["q"][l], small["k"][l] = qk_prep_bwd(dqs, dks, dvs, proj, vec(q_norm_g, l), vec(k_norm_g, l),
                                                                cos2, sin2, f"qk_prep_bwd_{l}")
        dy, small["conv_ln_g"][l], small["conv_ln_b"][l], small["dw_bias"][l] = conv_bwd_ln(
            dconv, sv["wpw"], sv["y"], vec(conv_ln_g, l), vec(conv_ln_b, l), f"conv_bwd_ln_{l}")
        g_wpw = mm_nn(sv["z"].T, dconv, BF16, f"dwpw_{l}")
        da, db, gw = conv_bwd_dw(dy, proj, dwk_f[l], f"conv_bwd_dw_{l}")
        g_dwk = jnp.sum(gw, axis=1)[:CONV_WIDTH]
        dproj = jnp.concatenate([dq, dk, dv, dgate_a, da, db, dgate_c], axis=1)
        g_win = mm_nn(sv["h"].T, dproj, BF16, f"dwin_{l}")
        zero = send_grads(l, g_win, g_wpw, g_wout, g_dwk)
        dx, dxb, small["norm_g"][l] = in_proj_bwd(dproj, sv["win"].T, sv["x"], vec(norm_g, l, zero), dx, f"in_proj_bwd_{l}")

    return dx, loss_cols, small
```

```python
import jax
import jax.numpy as jnp
from jax import lax
from jax.experimental import pallas as pl
from jax.experimental.pallas import tpu as pltpu

F32 = jnp.float32
BF16 = jnp.bfloat16
SDS = jax.ShapeDtypeStruct
MESH = pl.DeviceIdType.MESH

N_DEV = 8
DEPTH = 2
HEAD_DIM = 128
CONV_WIDTH = 31
CONV_PAD = 32
DILATIONS = (1, 4, 16)
Q_BLOCK = 128
ROPE_THETA = 10000.0
EPS = 1e-6
NEG = -1e30
ADAM_LR, ADAM_B1, ADAM_B2, ADAM_EPS, ADAM_WD, ADAM_STEP = 0.001, 0.9, 0.999, 1e-08, 0.01, 10
LANE = 128
ROW_CHUNK = 64
MIB = 1 << 20
NT = (((1,), (1,)), ((), ()))
TN = (((0,), (0,)), ((), ()))


def _call(body, **kw):
    return pl.pallas_call(body, **kw)


def _params(vmem_mib):
    return pltpu.CompilerParams(vmem_limit_bytes=vmem_mib * MIB)


def _tile(dim, pref, mult):
    t = min(pref, dim)
    while dim % t or t % mult:
        t -= mult
    return t


def _sig(v):
    return jax.nn.sigmoid(v)


def _rstd(v):
    return lax.rsqrt(jnp.mean(v * v, axis=-1, keepdims=True) + EPS)


def _row(tm, cb, c=0):
    return pl.BlockSpec((tm, cb), lambda i: (i, c))


def _full(shape):
    return pl.BlockSpec(shape, lambda i: (0,) * len(shape))


def _halo_prev(tm, cb, c=0):
    k = tm // CONV_PAD
    return pl.BlockSpec((CONV_PAD, cb), lambda i: (jnp.maximum(i * k - 1, 0), c))


def _halo_next(tm, cb, nblk, c=0):
    k = tm // CONV_PAD
    return pl.BlockSpec((CONV_PAD, cb), lambda i: (jnp.minimum((i + 1) * k, nblk - 1), c))


def in_proj(x, g, wt, name):
    S, D = x.shape
    N = wt.shape[0]
    tm, tn = _tile(S, 512, 8), _tile(N, 1024, LANE)

    def body(x_ref, g_ref, w_ref, o_ref, h_ref):
        @pl.when(pl.program_id(1) == 0)
        def _():
            xf = x_ref[...]
            h_ref[...] = (xf * _rstd(xf) * g_ref[...]).astype(BF16)

        o_ref[...] = lax.dot_general(h_ref[...], w_ref[...], NT, preferred_element_type=F32)

    return _call(
        body, name=name, grid=(S // tm, N // tn),
        in_specs=[pl.BlockSpec((tm, D), lambda i, j: (i, 0)), pl.BlockSpec((1, D), lambda i, j: (0, 0)),
                  pl.BlockSpec((tn, D), lambda i, j: (j, 0))],
        out_specs=[pl.BlockSpec((tm, tn), lambda i, j: (i, j)), pl.BlockSpec((tm, D), lambda i, j: (i, 0))],
        out_shape=[SDS((S, N), F32), SDS((S, D), BF16)],
        compiler_params=_params(40),
    )(x, g, wt)


def mm_nn(a, b, out_dtype, name, add=None, owners=1):
    M, K = a.shape
    N = b.shape[1]
    tm, tn = _tile(M, 512, 8), _tile(N // owners, 1024, LANE)
    per = N // owners // tn

    def body(*refs):
        a_ref, b_ref = refs[0], refs[1]
        o_ref = refs[-1]
        acc = jnp.dot(a_ref[...], b_ref[...], preferred_element_type=F32)
        if add is not None:
            acc = acc + refs[2][...]
        o_ref[...] = acc.astype(out_dtype)

    in_specs = [pl.BlockSpec((tm, K), lambda i, j: (i, 0)), pl.BlockSpec((K, tn), lambda i, j: (0, j))]
    args = [a, b]
    if add is not None:
        in_specs.append(pl.BlockSpec((tm, tn), lambda i, j: (i, j)))
        args.append(add)
    if owners > 1:
        out_spec = pl.BlockSpec((None, tm, tn), lambda i, j: (j // per, i, j % per))
        out_shape = SDS((owners, M, N // owners), out_dtype)
    else:
        out_spec, out_shape = pl.BlockSpec((tm, tn), lambda i, j: (i, j)), SDS((M, N), out_dtype)
    return _call(
        body, name=name, grid=(M // tm, N // tn), in_specs=in_specs, out_specs=out_spec, out_shape=out_shape,
        compiler_params=_params(48),
    )(*args)


def mm_nt(a, b, out_dtype, name):
    M, K = a.shape
    N = b.shape[0]
    tm, tn = _tile(M, 512, 8), _tile(N, 1024, LANE)

    def body(a_ref, b_ref, o_ref):
        o_ref[...] = lax.dot_general(a_ref[...], b_ref[...], NT, preferred_element_type=F32).astype(out_dtype)

    return _call(
        body, name=name, grid=(M // tm, N // tn),
        in_specs=[pl.BlockSpec((tm, K), lambda i, j: (i, 0)), pl.BlockSpec((tn, K), lambda i, j: (j, 0))],
        out_specs=pl.BlockSpec((tm, tn), lambda i, j: (i, j)), out_shape=SDS((M, N), out_dtype),
        compiler_params=_params(40),
    )(a, b)


def in_proj_bwd(dproj, wt, x, g, dx_out, name):
    S, K = dproj.shape
    D = wt.shape[1]
    tm, tk = _tile(S, 512, 8), _tile(K, 1024, LANE)
    nk = K // tk

    def body(dp_ref, w_ref, x_ref, g_ref, dxo_ref, dx_ref, dxb_ref, gacc_ref):
        i, k = pl.program_id(0), pl.program_id(1)
        part = jnp.dot(dp_ref[...], w_ref[...], preferred_element_type=F32)

        @pl.when(k == 0)
        def _():
            dx_ref[...] = part

        @pl.when(k > 0)
        def _():
            dx_ref[...] += part

        @pl.when((k == 0) & (i == 0))
        def _():
            gacc_ref[...] = jnp.zeros_like(gacc_ref)

        @pl.when(k == nk - 1)
        def _():
            dh = dx_ref[...]
            xf = x_ref[...]
            r = _rstd(xf)
            n = xf * r
            gacc_ref[...] += jnp.sum(dh * n, axis=0, keepdims=True)
            dn = dh * g_ref[...]
            dx = r * (dn - n * jnp.mean(dn * n, axis=-1, keepdims=True)) + dxo_ref[...]
            dx_ref[...] = dx
            dxb_ref[...] = dx.astype(BF16)

    return _call(
        body, name=name, grid=(S // tm, nk),
        in_specs=[pl.BlockSpec((tm, tk), lambda i, k: (i, k)), pl.BlockSpec((tk, D), lambda i, k: (k, 0)),
                  pl.BlockSpec((tm, D), lambda i, k: (i, 0)), pl.BlockSpec((1, D), lambda i, k: (0, 0)),
                  pl.BlockSpec((tm, D), lambda i, k: (i, 0))],
        out_specs=[pl.BlockSpec((tm, D), lambda i, k: (i, 0)), pl.BlockSpec((tm, D), lambda i, k: (i, 0)),
                   pl.BlockSpec((1, D), lambda i, k: (0, 0))],
        out_shape=[SDS((S, D), F32), SDS((S, D), BF16), SDS((1, D), F32)],
        compiler_params=_params(54),
    )(dproj, wt, x, g, dx_out)


def _dil_specs(S, DA, tm, dtype):
    specs = [pl.BlockSpec((tm // d, d * DA), lambda i: (i, 0)) for d in DILATIONS]
    shapes = [SDS((S // d, d * DA), dtype) for d in DILATIONS]
    return specs, shapes


def _head_buf(tm, DA):
    return pltpu.VMEM((DA // HEAD_DIM, tm, HEAD_DIM), F32)


def _emit_dilated(buf_ref, dsts, tm, DA):
    for d, dst in zip(DILATIONS, dsts):
        for h in range(DA // HEAD_DIM):
            for r in range(d):
                rows = slice(None) if d == 1 else pl.ds(r, tm // d, stride=d)
                dst[:, r * DA + h * HEAD_DIM:r * DA + (h + 1) * HEAD_DIM] = buf_ref.at[h][rows, :].astype(BF16)


def _collect_dilated(acc_ref, parts, tm, DA):
    for d, p in zip(DILATIONS, parts):
        for h in range(DA // HEAD_DIM):
            for r in range(d):
                part = p[:, r * DA + h * HEAD_DIM:r * DA + (h + 1) * HEAD_DIM].astype(F32)
                if d == 1:
                    acc_ref[h] = part
                else:
                    rows = pl.ds(r, tm // d, stride=d)
                    acc_ref.at[h][rows, :] = acc_ref.at[h][rows, :] + part


def qk_prep(proj, gq, gk, cos2, sin2, name):
    S = proj.shape[0]
    DA = proj.shape[1] // 7
    H = DA // HEAD_DIM
    tm = _tile(S, 256, 16 * DILATIONS[-1])
    nd = len(DILATIONS)

    def body(q_ref, k_ref, v_ref, gq_ref, gk_ref, c_ref, s_ref, *rest):
        outs, buf_ref = rest[:3 * nd], rest[3 * nd]
        ct, st = c_ref[...], s_ref[...]
        for t, (src, g_ref) in enumerate(((q_ref, gq_ref), (k_ref, gk_ref))):
            gain = g_ref[...]
            for h in range(H):
                sl = slice(h * HEAD_DIM, (h + 1) * HEAD_DIM)
                xh = src[:, sl]
                n = xh * _rstd(xh) * gain
                buf_ref[h] = n * ct + pltpu.roll(n, HEAD_DIM // 2, 1) * st
            _emit_dilated(buf_ref, outs[t * nd:(t + 1) * nd], tm, DA)
        for h in range(H):
            buf_ref[h] = v_ref[:, h * HEAD_DIM:(h + 1) * HEAD_DIM]
        _emit_dilated(buf_ref, outs[2 * nd:], tm, DA)

    specs, shapes = _dil_specs(S, DA, tm, BF16)
    outs = _call(
        body, name=name, grid=(S // tm,),
        in_specs=[_row(tm, DA, 0), _row(tm, DA, 1), _row(tm, DA, 2), _full((1, HEAD_DIM)), _full((1, HEAD_DIM)),
                  _row(tm, HEAD_DIM), _row(tm, HEAD_DIM)],
        out_specs=specs * 3, out_shape=shapes * 3, scratch_shapes=[_head_buf(tm, DA)],
        compiler_params=_params(48),
    )(proj, proj, proj, gq, gk, cos2, sin2)
    return outs[:nd], outs[nd:2 * nd], outs[2 * nd:]


def qk_prep_bwd(dqs, dks, dvs, proj, gq, gk, cos2, sin2, name):
    S = proj.shape[0]
    DA = proj.shape[1] // 7
    H = DA // HEAD_DIM
    tm = _tile(S, 256, 16 * DILATIONS[-1])
    nb = len(dqs)

    def body(*refs):
        dq_refs, dk_refs, dv_refs = refs[:nb], refs[nb:2 * nb], refs[2 * nb:3 * nb]
        q_ref, k_ref, gq_ref, gk_ref, c_ref, s_ref = refs[3 * nb:3 * nb + 6]
        dqo_ref, dko_ref, dvo_ref, gqa_ref, gka_ref, acc_ref = refs[3 * nb + 6:]
        ct, st = c_ref[...], s_ref[...]

        @pl.when(pl.program_id(0) == 0)
        def _():
            gqa_ref[...] = jnp.zeros_like(gqa_ref)
            gka_ref[...] = jnp.zeros_like(gka_ref)

        for parts, x_ref, g_ref, dst, gacc in ((dq_refs, q_ref, gq_ref, dqo_ref, gqa_ref),
                                               (dk_refs, k_ref, gk_ref, dko_ref, gka_ref)):
            gain = g_ref[...]
            gsum = jnp.zeros((1, HEAD_DIM), F32)
            _collect_dilated(acc_ref, parts, tm, DA)
            for h in range(H):
                sl = slice(h * HEAD_DIM, (h + 1) * HEAD_DIM)
                dout = acc_ref[h]
                dn = dout * ct + pltpu.roll(dout * st, HEAD_DIM // 2, 1)
                xh = x_ref[:, sl]
                r = _rstd(xh)
                xn = xh * r
                gsum = gsum + jnp.sum(dn * xn, axis=0, keepdims=True)
                dnn = dn * gain
                dst[:, sl] = (r * (dnn - xn * jnp.mean(dnn * xn, axis=-1, keepdims=True))).astype(BF16)
            gacc[...] += gsum
        _collect_dilated(acc_ref, dv_refs, tm, DA)
        for h in range(H):
            dvo_ref[:, h * HEAD_DIM:(h + 1) * HEAD_DIM] = acc_ref[h].astype(BF16)

    specs, _ = _dil_specs(S, DA, tm, BF16)
    return _call(
        body, name=name, grid=(S // tm,),
        in_specs=specs * 3 + [_row(tm, DA, 0), _row(tm, DA, 1), _full((1, HEAD_DIM)),
                              _full((1, HEAD_DIM)), _row(tm, HEAD_DIM), _row(tm, HEAD_DIM)],
        out_specs=[_row(tm, DA)] * 3 + [_full((1, HEAD_DIM))] * 2,
        out_shape=[SDS((S, DA), BF16)] * 3 + [SDS((1, HEAD_DIM), F32)] * 2,
        scratch_shapes=[_head_buf(tm, DA)],
        compiler_params=_params(48),
    )(*dqs, *dks, *dvs, proj, proj, gq, gk, cos2, sin2)


def _band_masks(n):
    row = lax.broadcasted_iota(jnp.int32, (Q_BLOCK, Q_BLOCK), 0)
    col = lax.broadcasted_iota(jnp.int32, (Q_BLOCK, Q_BLOCK), 1)
    first = jnp.where(n > 0, 0, Q_BLOCK + 1)
    return col <= row, (col - row) >= first


def attn_fwd(qh, kh, vb, d, name):
    L = qh.shape[0]
    DA = qh.shape[1] // d
    H = DA // HEAD_DIM
    nb = L // Q_BLOCK
    scale = HEAD_DIM ** -0.5
    view = (L, d * DA)

    def body(q_ref, kc_ref, kp_ref, vc_ref, vp_ref, o_ref, lse_ref):
        mask_c, mask_p = _band_masks(pl.program_id(1))
        for h in range(H):
            sl = slice(h * HEAD_DIM, (h + 1) * HEAD_DIM)
            q = q_ref[:, sl]
            sc = lax.dot_general(q, kc_ref[:, sl], NT, preferred_element_type=F32) * scale
            sp = lax.dot_general(q, kp_ref[:, sl], NT, preferred_element_type=F32) * scale
            sc = jnp.where(mask_c, sc, NEG)
            sp = jnp.where(mask_p, sp, NEG)
            m = jnp.maximum(jnp.max(sc, axis=-1, keepdims=True), jnp.max(sp, axis=-1, keepdims=True))
            pc = jnp.exp(sc - m)
            pp = jnp.exp(sp - m)
            l = jnp.sum(pc, axis=-1, keepdims=True) + jnp.sum(pp, axis=-1, keepdims=True)
            o = jnp.dot(pc.astype(BF16), vc_ref[:, sl], preferred_element_type=F32)
            o = o + jnp.dot(pp.astype(BF16), vp_ref[:, sl], preferred_element_type=F32)
            o_ref[:, sl] = o / l
            lse_ref[0, :, h:h + 1] = m + jnp.log(l)

    cur = pl.BlockSpec((Q_BLOCK, DA), lambda r, n: (n, r))
    prev = pl.BlockSpec((Q_BLOCK, DA), lambda r, n: (jnp.maximum(n - 1, 0), r))
    o, lse = _call(
        body, name=name, grid=(d, nb), in_specs=[cur, cur, prev, cur, prev],
        out_specs=[cur, pl.BlockSpec((1, Q_BLOCK, H), lambda r, n: (r, n, 0))],
        out_shape=[SDS(view, F32), SDS((d, L, H), F32)],
        compiler_params=_params(32),
    )(qh, kh, kh, vb, vb)
    return o, lse


def attn_bwd(qh, kh, vb, da, lse_d, delta_d, d, name):
    L = qh.shape[0]
    DA = qh.shape[1] // d
    H = DA // HEAD_DIM
    nb = L // Q_BLOCK
    scale = HEAD_DIM ** -0.5
    view = (L, d * DA)

    def body(q_ref, kc_ref, kp_ref, vc_ref, vp_ref, do_ref, lse_ref, dl_ref, dq_ref, dk_ref, dv_ref, dkc_ref, dvc_ref):
        n = pl.program_id(1)
        mask_c, mask_p = _band_masks(n)

        @pl.when(n == 0)
        def _():
            dkc_ref[...] = jnp.zeros_like(dkc_ref)
            dvc_ref[...] = jnp.zeros_like(dvc_ref)

        @pl.when(n < nb)
        def _():
            for h in range(H):
                sl = slice(h * HEAD_DIM, (h + 1) * HEAD_DIM)
                q, kc, kp, vc, vp, do = q_ref[:, sl], kc_ref[:, sl], kp_ref[:, sl], vc_ref[:, sl], vp_ref[:, sl], do_ref[:, sl]
                lse = lse_ref[0, :, h:h + 1]
                dl = dl_ref[0, :, h:h + 1]
                sc = lax.dot_general(q, kc, NT, preferred_element_type=F32) * scale
                sp = lax.dot_general(q, kp, NT, preferred_element_type=F32) * scale
                pc = jnp.exp(jnp.where(mask_c, sc, NEG) - lse)
                pp = jnp.exp(jnp.where(mask_p, sp, NEG) - lse)
                dpc = lax.dot_general(do, vc, NT, preferred_element_type=F32)
                dpp = lax.dot_general(do, vp, NT, preferred_element_type=F32)
                dsc = (pc * (dpc - dl) * scale).astype(BF16)
                dsp = (pp * (dpp - dl) * scale).astype(BF16)
                dq = jnp.dot(dsc, kc, preferred_element_type=F32) + jnp.dot(dsp, kp, preferred_element_type=F32)
                dq_ref[:, sl] = dq.astype(BF16)
                dk_prev = lax.dot_general(dsp, q, TN, preferred_element_type=F32)
                dv_prev = lax.dot_general(pp.astype(BF16), do, TN, preferred_element_type=F32)
                dk_ref[:, sl] = (dkc_ref[:, sl] + dk_prev).astype(BF16)
                dv_ref[:, sl] = (dvc_ref[:, sl] + dv_prev).astype(BF16)
                dkc_ref[:, sl] = lax.dot_general(dsc, q, TN, preferred_element_type=F32)
                dvc_ref[:, sl] = lax.dot_general(pc.astype(BF16), do, TN, preferred_element_type=F32)

        @pl.when(n == nb)
        def _():
            dk_ref[...] = dkc_ref[...].astype(BF16)
            dv_ref[...] = dvc_ref[...].astype(BF16)

    cur = pl.BlockSpec((Q_BLOCK, DA), lambda r, n: (jnp.minimum(n, nb - 1), r))
    prev = pl.BlockSpec((Q_BLOCK, DA), lambda r, n: (jnp.clip(n - 1, 0, nb - 1), r))
    late = pl.BlockSpec((Q_BLOCK, DA), lambda r, n: (jnp.maximum(n - 1, 0), r))
    stat = pl.BlockSpec((1, Q_BLOCK, H), lambda r, n: (r, jnp.minimum(n, nb - 1), 0))
    dq, dk, dv = _call(
        body, name=name, grid=(d, nb + 1), in_specs=[cur, cur, prev, cur, prev, cur, stat, stat],
        out_specs=[cur, late, late], out_shape=[SDS(view, BF16)] * 3,
        scratch_shapes=[pltpu.VMEM((Q_BLOCK, DA), F32), pltpu.VMEM((Q_BLOCK, DA), F32)],
        compiler_params=_params(32),
    )(qh, kh, kh, vb, vb, da, lse_d, delta_d)
    return dq, dk, dv


def _to_branch(stat, d):
    S, H = stat.shape
    return stat.reshape(S // d, d, H).transpose(1, 0, 2)


def _from_branch(stat):
    d, L, H = stat.shape
    return stat.transpose(1, 0, 2).reshape(L * d, H)


def att_combine(os_, lses, proj, gain, name):
    S, DA = os_[0].shape
    H = DA // HEAD_DIM
    tm = _tile(S, 256, 16 * DILATIONS[-1])
    nb = len(os_)

    def body(*refs):
        o_views, l_refs = refs[:nb], refs[nb:2 * nb]
        gate_ref, gain_ref, att_ref, y_ref, lse_ref = refs[2 * nb:2 * nb + 5]
        bufs = refs[2 * nb + 5:]
        for d, view, buf in zip(DILATIONS[1:], o_views[1:], bufs):
            for h in range(H):
                for r in range(d):
                    buf.at[h][pl.ds(r, tm // d, stride=d), :] = view[:, r * DA + h * HEAD_DIM:r * DA + (h + 1) * HEAD_DIM]
        ls = [r[...] for r in l_refs]
        top = ls[0]
        for l in ls[1:]:
            top = jnp.maximum(top, l)
        den = jnp.exp(ls[0] - top)
        for l in ls[1:]:
            den = den + jnp.exp(l - top)
        lse = top + jnp.log(den)
        lse_ref[...] = lse
        ws = [jnp.exp(l - lse) for l in ls]
        for h in range(H):
            sl = slice(h * HEAD_DIM, (h + 1) * HEAD_DIM)
            acc = ws[0][:, h:h + 1] * o_views[0][:, sl]
            for w, buf in zip(ws[1:], bufs):
                acc = acc + w[:, h:h + 1] * buf[h]
            att_ref[:, sl] = acc
        a = att_ref[...]
        g = gate_ref[...]
        y_ref[...] = (a * _rstd(a) * gain_ref[...] * (g * _sig(g))).astype(BF16)

    specs, _ = _dil_specs(S, DA, tm, F32)
    return _call(
        body, name=name, grid=(S // tm,),
        in_specs=specs + [_row(tm, H)] * nb + [_row(tm, DA, 3), _full((1, DA))],
        out_specs=[_row(tm, DA), _row(tm, DA), _row(tm, H)],
        out_shape=[SDS((S, DA), F32), SDS((S, DA), BF16), SDS((S, H), F32)],
        scratch_shapes=[_head_buf(tm, DA)] * (nb - 1),
        compiler_params=_params(48),
    )(*os_, *lses, proj, gain)


def gate_bwd(dcat, cblk, a, proj, gate_blk, gain, dilated, name):
    S, DA = a.shape
    H = DA // HEAD_DIM
    tm = _tile(S, 256, 16 * DILATIONS[-1])
    nd = len(DILATIONS) if dilated else 1

    def body(dy_ref, a_ref, gate_ref, gain_ref, dg_ref, gacc_ref, *rest):
        @pl.when(pl.program_id(0) == 0)
        def _():
            gacc_ref[...] = jnp.zeros_like(gacc_ref)

        dy, av, g, gain_v = dy_ref[...], a_ref[...], gate_ref[...], gain_ref[...]
        r = _rstd(av)
        n = av * r
        sg = _sig(g)
        dg_ref[...] = (dy * (n * gain_v) * (sg * (1.0 + g * (1.0 - sg)))).astype(BF16)
        drn = dy * (g * sg)
        gacc_ref[...] += jnp.sum(drn * n, axis=0, keepdims=True)
        dn = drn * gain_v
        da = r * (dn - n * jnp.mean(dn * n, axis=-1, keepdims=True))
        if dilated:
            da_refs, delta_ref, buf_ref = rest[:nd], rest[nd], rest[nd + 1]
            for h in range(H):
                buf_ref[h] = da[:, h * HEAD_DIM:(h + 1) * HEAD_DIM]
            _emit_dilated(buf_ref, da_refs, tm, DA)
            prod = da * av
            for h in range(H):
                delta_ref[:, h:h + 1] = jnp.sum(prod[:, h * HEAD_DIM:(h + 1) * HEAD_DIM], axis=-1, keepdims=True)
        else:
            rest[0][...] = da.astype(BF16)

    out_specs = [_row(tm, DA), _full((1, DA))]
    out_shape = [SDS((S, DA), BF16), SDS((1, DA), F32)]
    scratch = []
    if dilated:
        specs, shapes = _dil_specs(S, DA, tm, BF16)
        out_specs += specs + [_row(tm, H)]
        out_shape += shapes + [SDS((S, H), F32)]
        scratch = [_head_buf(tm, DA)]
    else:
        out_specs.append(_row(tm, DA))
        out_shape.append(SDS((S, DA), BF16))
    return _call(
        body, name=name, grid=(S // tm,),
        in_specs=[_row(tm, DA, cblk), _row(tm, DA), _row(tm, DA, gate_blk), _full((1, DA))],
        out_specs=out_specs, out_shape=out_shape, scratch_shapes=scratch, compiler_params=_params(48),
    )(dcat, a, proj, gain)


def _fill_u(i, a_ref, b_ref, ah_ref, bh_ref, uext_ref, tm):
    uext_ref[pl.ds(CONV_PAD, tm), :] = a_ref[...] * _sig(b_ref[...])
    uh = ah_ref[...] * _sig(bh_ref[...])
    uext_ref[pl.ds(0, CONV_PAD), :] = jnp.where(i > 0, uh, 0.0)


def conv_fwd(proj, wk, bias, ln_g, ln_b, out_g, wpw, name):
    S = proj.shape[0]
    DC = proj.shape[1] // 7
    tm = _tile(S, 128, ROW_CHUNK)
    lead = CONV_PAD - (CONV_WIDTH - 1)

    def body(a_ref, b_ref, ah_ref, bh_ref, gate_ref, wk_ref, bias_ref, lg_ref, lb_ref, og_ref, wpw_ref,
             cy_ref, z_ref, conv_ref, y_ref, uext_ref):
        _fill_u(pl.program_id(0), a_ref, b_ref, ah_ref, bh_ref, uext_ref, tm)

        def cols(cc, carry):
            c0 = pl.multiple_of(cc * LANE, LANE)
            for rr in range(tm // ROW_CHUNK):
                acc = jnp.broadcast_to(bias_ref[:, pl.ds(c0, LANE)], (ROW_CHUNK, LANE))
                for j in range(CONV_WIDTH):
                    acc = acc + wk_ref[j:j + 1, pl.ds(c0, LANE)] * uext_ref[pl.ds(rr * ROW_CHUNK + lead + j, ROW_CHUNK), pl.ds(c0, LANE)]
                y_ref[pl.ds(rr * ROW_CHUNK, ROW_CHUNK), pl.ds(c0, LANE)] = acc
            return carry

        lax.fori_loop(0, DC // LANE, cols, 0)
        y = y_ref[...]
        yc = y - jnp.mean(y, axis=-1, keepdims=True)
        ln = yc * _rstd(yc) * lg_ref[...] + lb_ref[...]
        zb = (ln * _sig(ln)).astype(BF16)
        z_ref[...] = zb
        conv = jnp.dot(zb, wpw_ref[...], preferred_element_type=F32)
        conv_ref[...] = conv
        g = gate_ref[...]
        cy_ref[...] = (conv * _rstd(conv) * og_ref[...] * (g * _sig(g))).astype(BF16)

    vec = _full((1, DC))
    return _call(
        body, name=name, grid=(S // tm,),
        in_specs=[_row(tm, DC, 4), _row(tm, DC, 5), _halo_prev(tm, DC, 4), _halo_prev(tm, DC, 5), _row(tm, DC, 6),
                  _full((CONV_PAD, DC)), vec, vec, vec, vec, _full((DC, DC))],
        out_specs=[_row(tm, DC)] * 4,
        out_shape=[SDS((S, DC), BF16), SDS((S, DC), BF16), SDS((S, DC), F32), SDS((S, DC), F32)],
        scratch_shapes=[pltpu.VMEM((CONV_PAD + tm, DC), F32)],
        compiler_params=_params(48),
    )(proj, proj, proj, proj, proj, wk, bias, ln_g, ln_b, out_g, wpw)


def conv_bwd_ln(dconv, wpw, y, ln_g, ln_b, name):
    S, DC = y.shape
    tm = _tile(S, 256, 8)

    def body(dc_ref, wpw_ref, y_ref, lg_ref, lb_ref, dy_ref, glg_ref, glb_ref, gb_ref):
        @pl.when(pl.program_id(0) == 0)
        def _():
            glg_ref[...] = jnp.zeros_like(glg_ref)
            glb_ref[...] = jnp.zeros_like(glb_ref)
            gb_ref[...] = jnp.zeros_like(gb_ref)

        dz = lax.dot_general(dc_ref[...], wpw_ref[...], NT, preferred_element_type=F32)
        yv = y_ref[...]
        yc = yv - jnp.mean(yv, axis=-1, keepdims=True)
        rstd = _rstd(yc)
        yhat = yc * rstd
        ln = yhat * lg_ref[...] + lb_ref[...]
        sg = _sig(ln)
        dln = dz * (sg * (1.0 + ln * (1.0 - sg)))
        glb_ref[...] += jnp.sum(dln, axis=0, keepdims=True)
        glg_ref[...] += jnp.sum(dln * yhat, axis=0, keepdims=True)
        dyh = dln * lg_ref[...]
        dy = rstd * (dyh - jnp.mean(dyh, axis=-1, keepdims=True) - yhat * jnp.mean(dyh * yhat, axis=-1, keepdims=True))
        dy_ref[...] = dy
        gb_ref[...] += jnp.sum(dy, axis=0, keepdims=True)

    vec = _full((1, DC))
    return _call(
        body, name=name, grid=(S // tm,),
        in_specs=[_row(tm, DC), _full((DC, DC)), _row(tm, DC), vec, vec],
        out_specs=[_row(tm, DC), vec, vec, vec],
        out_shape=[SDS((S, DC), F32)] + [SDS((1, DC), F32)] * 3,
        compiler_params=_params(48),
    )(dconv, wpw, y, ln_g, ln_b)


def conv_bwd_dw(dy, proj, wk, name):
    S, DC = dy.shape
    tm = _tile(S, 128, ROW_CHUNK)
    nsteps = S // tm
    lead = CONV_PAD - (CONV_WIDTH - 1)
    groups = ROW_CHUNK // 8

    def body(dy_ref, dyn_ref, a_ref, b_ref, ah_ref, bh_ref, wk_ref, da_ref, db_ref, gw_ref, uext_ref, dyext_ref, du_ref):
        i = pl.program_id(0)

        @pl.when(i == 0)
        def _():
            gw_ref[...] = jnp.zeros_like(gw_ref)

        _fill_u(i, a_ref, b_ref, ah_ref, bh_ref, uext_ref, tm)
        dyext_ref[pl.ds(0, tm), :] = dy_ref[...]
        dyext_ref[pl.ds(tm, CONV_PAD), :] = jnp.where(i < nsteps - 1, dyn_ref[...], 0.0)

        def cols(cc, carry):
            c0 = pl.multiple_of(cc * LANE, LANE)
            lanes = pl.ds(c0, LANE)
            for rr in range(tm // ROW_CHUNK):
                acc = jnp.zeros((ROW_CHUNK, LANE), F32)
                for j in range(CONV_WIDTH):
                    acc = acc + wk_ref[j:j + 1, lanes] * dyext_ref[pl.ds(rr * ROW_CHUNK + CONV_WIDTH - 1 - j, ROW_CHUNK), lanes]
                du_ref[pl.ds(rr * ROW_CHUNK, ROW_CHUNK), lanes] = acc
            for j in range(CONV_WIDTH):
                part = jnp.zeros((8, LANE), F32)
                for rr in range(tm // ROW_CHUNK):
                    prod = dyext_ref[pl.ds(rr * ROW_CHUNK, ROW_CHUNK), lanes] * uext_ref[pl.ds(rr * ROW_CHUNK + lead + j, ROW_CHUNK), lanes]
                    for k in range(groups):
                        part = part + prod[8 * k:8 * k + 8]
                gw_ref[j, :, lanes] += part
            return carry

        lax.fori_loop(0, DC // LANE, cols, 0)
        du = du_ref[...]
        sb = _sig(b_ref[...])
        da_ref[...] = (du * sb).astype(BF16)
        db_ref[...] = (du * a_ref[...] * sb * (1.0 - sb)).astype(BF16)

    return _call(
        body, name=name, grid=(nsteps,),
        in_specs=[_row(tm, DC), _halo_next(tm, DC, S // CONV_PAD), _row(tm, DC, 4), _row(tm, DC, 5),
                  _halo_prev(tm, DC, 4), _halo_prev(tm, DC, 5), _full((CONV_PAD, DC))],
        out_specs=[_row(tm, DC), _row(tm, DC), _full((CONV_PAD, 8, DC))],
        out_shape=[SDS((S, DC), BF16), SDS((S, DC), BF16), SDS((CONV_PAD, 8, DC), F32)],
        scratch_shapes=[pltpu.VMEM((CONV_PAD + tm, DC), F32), pltpu.VMEM((tm + CONV_PAD, DC), F32), pltpu.VMEM((tm, DC), F32)],
        compiler_params=_params(40),
    )(dy, dy, proj, proj, proj, proj, wk)


def loss_head(xo, target, name):
    S, D = xo.shape
    tm = _tile(S, 256, 8)

    def body(x_ref, t_ref, dy_ref, dyb_ref, acc_ref):
        @pl.when(pl.program_id(0) == 0)
        def _():
            acc_ref[...] = jnp.zeros_like(acc_ref)

        err = x_ref[...] - t_ref[...]
        dy = err * (1.0 / D)
        dy_ref[...] = dy
        dyb_ref[...] = dy.astype(BF16)
        acc_ref[...] += jnp.sum(err * dy, axis=0, keepdims=True) * 0.5

    return _call(
        body, name=name, grid=(S // tm,), in_specs=[_row(tm, D), _row(tm, D)],
        out_specs=[_row(tm, D), _row(tm, D), _full((1, D))],
        out_shape=[SDS((S, D), F32), SDS((S, D), BF16), SDS((1, D), F32)],
        compiler_params=_params(32),
    )(xo, target)


def _coords():
    x, y, c = lax.axis_index("x"), lax.axis_index("y"), lax.axis_index("c")
    return x, y, c


def _lin(p):
    return 4 * p[0] + 2 * p[1] + p[2]


def _slot(ref, axis, idx, size):
    index = [slice(None)] * len(ref.shape)
    index[axis] = pl.ds(idx * size, size)
    return ref.at[tuple(index)]


def all_gather(blocks, axes, name):
    na = len(blocks)
    sizes = [b.shape[ax] for b, ax in zip(blocks, axes)]
    fulls = [SDS(b.shape[:ax] + (N_DEV * b.shape[ax],) + b.shape[ax + 1:], b.dtype) for b, ax in zip(blocks, axes)]

    def body(*refs):
        in_refs, out_refs = refs[:na], refs[na:2 * na]
        send_sems, recv_sems, local_sems = refs[2 * na:]
        x, y, c = _coords()
        me, sibling = (x, y, c), (x, y, 1 - c)
        chips = [(1 - x, y), (x, 1 - y), (1 - x, 1 - y)]

        def place(a, p):
            return _slot(out_refs[a], axes[a], _lin(p), sizes[a])

        def copy(a, k, block, to, src=None):
            return pltpu.make_async_remote_copy(
                src_ref=place(a, block) if src is None else src, dst_ref=place(a, block),
                send_sem=send_sems.at[a, k], recv_sem=recv_sems.at[a, k], device_id=to, device_id_type=MESH)

        mine = [pltpu.make_async_copy(in_refs[a], place(a, me), local_sems.at[a]) for a in range(na)]
        for cp in mine:
            cp.start()
        first = []
        for a in range(na):
            first.append(copy(a, 0, me, sibling, src=in_refs[a]))
            first += [copy(a, 1 + j, me, (*chip, c), src=in_refs[a]) for j, chip in enumerate(chips)]
        for cp in first:
            cp.start()
        passed = []
        for j, chip in enumerate(chips):
            for a in range(na):
                copy(a, 1 + j, (*chip, c), me).wait_recv()
                cp = copy(a, 4 + j, (*chip, c), sibling)
                cp.start()
                passed.append(cp)
        for a in range(na):
            copy(a, 0, sibling, me).wait_recv()
            for j, chip in enumerate(chips):
                copy(a, 4 + j, (*chip, 1 - c), me).wait_recv()
        for cp in first + passed:
            cp.wait_send()
        for cp in mine:
            cp.wait()

    hbm = pl.BlockSpec(memory_space=pltpu.HBM)
    return _call(
        body, name=name, in_specs=[hbm] * na, out_specs=[hbm] * na, out_shape=fulls,
        scratch_shapes=[pltpu.SemaphoreType.DMA((na, 7)), pltpu.SemaphoreType.DMA((na, 7)), pltpu.SemaphoreType.DMA((na,))],
    )(*blocks)


class _Exchange:
    def __init__(self, gather, srcs, axes, name, after):
        self.gather, self.axes, self.name, self.na = gather, axes, name, len(srcs)
        if gather:
            self.sizes = [s.shape[ax] for s, ax in zip(srcs, axes)]
            lands = [s.shape[:ax] + (N_DEV * s.shape[ax],) + s.shape[ax + 1:] for s, ax in zip(srcs, axes)]
        else:
            self.sizes = [None if ax is None else s.shape[ax] // N_DEV for s, ax in zip(srcs, axes)]
            lands = [s.shape if ax is None else (N_DEV,) + s.shape[:ax] + (sz,) + s.shape[ax + 1:]
                     for s, ax, sz in zip(srcs, axes, self.sizes)]
        self.kinds = [pltpu.HBM(s.shape, s.dtype) for s in srcs] + [pltpu.HBM(l, s.dtype) for l, s in zip(lands, srcs)]
        lands = [lax.empty(l, s.dtype) for l, s in zip(lands, srcs)]
        after = jnp.zeros((8, LANE), F32) if after is None else after
        self._start([pltpu.with_memory_space_constraint(t, pltpu.HBM) for t in list(srcs) + lands], after)

    def _src(self, a, ref, owner):
        if self.gather:
            return ref
        return ref.at[_lin(owner)] if self.axes[a] is None else _slot(ref, self.axes[a], _lin(owner), self.sizes[a])

    def _dst(self, a, land, sender):
        return _slot(land, self.axes[a], _lin(sender), self.sizes[a]) if self.gather else land.at[_lin(sender)]

    def _copies(self, refs, send_sems, recv_sems):
        na = self.na
        me = _coords()
        flips = [(k >> 2 & 1, k >> 1 & 1, k & 1) for k in range(1, N_DEV)]
        peers = [tuple(1 - v if f else v for v, f in zip(me, flip)) for flip in flips]
        sends, arrivals = [], []
        for a in range(na):
            for k, peer in enumerate(peers):
                pair = dict(send_sem=send_sems.at[7 * a + k], recv_sem=recv_sems.at[7 * a + k], device_id=peer, device_id_type=MESH)
                sends.append(pltpu.make_async_remote_copy(
                    src_ref=self._src(a, refs[a], peer), dst_ref=self._dst(a, refs[na + a], me), **pair))
                arrivals.append(pltpu.make_async_remote_copy(
                    src_ref=self._src(a, refs[a], me), dst_ref=self._dst(a, refs[na + a], peer), **pair))
        return sends, arrivals

    def _place_own(self, operands):
        na = self.na

        def body(*refs):
            sems = refs[3 * na]
            me = _coords()
            own = [pltpu.make_async_copy(self._src(a, refs[a], me), self._dst(a, refs[na + a], me), sems.at[a]) for a in range(na)]
            for cp in own:
                cp.start()
            for cp in own:
                cp.wait()

        hbm = pl.BlockSpec(memory_space=pltpu.HBM)
        lands = _call(
            body, name=self.name + "_own", in_specs=[hbm] * (2 * na), out_specs=[hbm] * na, out_shape=self.kinds[na:],
            input_output_aliases={na + a: a for a in range(na)}, scratch_shapes=[pltpu.SemaphoreType.DMA((na,))],
        )(*operands)
        return operands[:na] + list(lands)

    def _start(self, operands, after):
        na = self.na
        operands = self._place_own(operands)

        def body(*refs):
            ins = refs[:2 * na]
            send_sems, recv_sems, token_ref = refs[2 * na + 1], refs[2 * na + 2], refs[4 * na + 3]
            for cp in self._copies(ins, send_sems, recv_sems)[0]:
                cp.start()
            token_ref[...] = jnp.zeros_like(token_ref)

        hbm = pl.BlockSpec(memory_space=pltpu.HBM)
        sem = pl.BlockSpec(memory_space=pltpu.SEMAPHORE)
        outs = _call(
            body, name=self.name + "_start",
            in_specs=[hbm] * (2 * na) + [pl.BlockSpec(memory_space=pl.ANY)],
            out_specs=[sem, sem] + [hbm] * (2 * na) + [pl.BlockSpec(memory_space=pltpu.VMEM)],
            out_shape=[pltpu.SemaphoreType.DMA((7 * na,)), pltpu.SemaphoreType.DMA((7 * na,))] + self.kinds + [SDS((8, LANE), F32)],
            input_output_aliases={i: 2 + i for i in range(2 * na)},
            compiler_params=pltpu.CompilerParams(has_side_effects=pltpu.SideEffectType.DATAFLOW_SIDE_EFFECTING),
        )(*operands, after)
        self.sems, self.thru, self.token = outs[:2], outs[2:2 + 2 * na], outs[2 + 2 * na][0:1, 0:1]

    def wait(self, after):
        na = self.na

        def body(*refs):
            ins, send_sems, recv_sems = refs[:2 * na], refs[2 * na], refs[2 * na + 1]
            sends, arrivals = self._copies(ins, send_sems, recv_sems)
            for cp in sends:
                cp.wait_send()
            for cp in arrivals:
                cp.wait_recv()

        hbm = pl.BlockSpec(memory_space=pltpu.HBM)
        sem = pl.BlockSpec(memory_space=pltpu.SEMAPHORE)
        outs = _call(
            body, name=self.name + "_wait",
            in_specs=[hbm] * (2 * na) + [sem, sem, pl.BlockSpec(memory_space=pl.ANY)],
            out_specs=[hbm] * (2 * na), out_shape=self.kinds,
            input_output_aliases={i: i for i in range(2 * na)},
            compiler_params=pltpu.CompilerParams(has_side_effects=pltpu.SideEffectType.DATAFLOW_SIDE_EFFECTING),
        )(*self.thru, *self.sems, after)
        return outs[na:]


def adamw(w, m, v, parts, layer, prev, name):
    nl, R, C = w.shape
    tr = _tile(R, 128, 8) if R % 8 == 0 else R

    def body(w_ref, m_ref, v_ref, p_ref, *rest):
        g_ref, d_ref, mo_ref, vo_ref = rest[-4:]
        g = p_ref[0].astype(F32)
        for s in range(1, N_DEV):
            g = g + p_ref[s].astype(F32)
        mn = ADAM_B1 * m_ref[0] + (1.0 - ADAM_B1) * g
        vn = ADAM_B2 * v_ref[0] + (1.0 - ADAM_B2) * (g * g)
        m_hat = mn / (1.0 - ADAM_B1 ** ADAM_STEP)
        v_hat = vn / (1.0 - ADAM_B2 ** ADAM_STEP)
        g_ref[0] = g
        d_ref[0] = -ADAM_LR * (m_hat / (jnp.sqrt(v_hat) + ADAM_EPS) + ADAM_WD * w_ref[0])
        mo_ref[0] = mn
        vo_ref[0] = vn

    row = pl.BlockSpec((1, tr, C), lambda i: (layer, i, 0))
    carried = [] if prev is None else list(prev)
    return _call(
        body, name=name, grid=(R // tr,),
        in_specs=[row, row, row, pl.BlockSpec((N_DEV, tr, C), lambda i: (0, i, 0))] + [pl.BlockSpec(memory_space=pl.ANY)] * len(carried),
        out_specs=[row] * 4, out_shape=[SDS((nl, R, C), F32)] * 4,
        input_output_aliases={4 + k: k for k in range(len(carried))},
        compiler_params=_params(48),
    )(w, m, v, parts, *carried)


def _rope_tables(S):
    inv_freq = 1.0 / (ROPE_THETA ** (jnp.arange(0, HEAD_DIM, 2, dtype=F32) / HEAD_DIM))
    ang = jnp.arange(S, dtype=F32)[:, None] * inv_freq[None, :]
    cos, sin = jnp.cos(ang), jnp.sin(ang)
    return jnp.concatenate([cos, cos], axis=-1), jnp.concatenate([-sin, sin], axis=-1)


def _pack_small(D, norm_g, dw_bias, conv_ln_g, conv_ln_b, att_out_g, conv_out_g, q_norm_g, k_norm_g, extra=None):
    qk = jnp.concatenate([q_norm_g.reshape(-1), k_norm_g.reshape(-1)])
    qk = jnp.pad(qk, (0, D - qk.shape[0])).reshape(1, D)
    zero = jnp.zeros((1, D), F32)
    return jnp.concatenate([norm_g, dw_bias, conv_ln_g, conv_ln_b, att_out_g, conv_out_g, qk, zero,
                            zero if extra is None else extra, zero], axis=0)


def _unpack_small(p):
    rows = [p[2 * i:2 * i + 2] for i in range(6)]
    qk = p[12, :4 * HEAD_DIM].reshape(2, DEPTH, HEAD_DIM)
    return rows + [qk[0], qk[1]]


def kernel(x, norm_g, w_in, q_norm_g, k_norm_g, dw_kernel, dw_bias, conv_ln_g, conv_ln_b, w_pw, att_out_g, conv_out_g, w_out, loss_target, m_norm_g, m_w_in, m_q_norm_g, m_k_norm_g, m_dw_kernel, m_dw_bias, m_conv_ln_g, m_conv_ln_b, m_w_pw, m_att_out_g, m_conv_out_g, m_w_out, v_norm_g, v_w_in, v_q_norm_g, v_k_norm_g, v_dw_kernel, v_dw_bias, v_conv_ln_g, v_conv_ln_b, v_w_pw, v_att_out_g, v_conv_out_g, v_w_out):
    xs = x[0]
    D = xs.shape[1]
    bf = lambda t, l: t[l].astype(BF16)
    wint0, dwk_f = all_gather([bf(w_in, 0).T, dw_kernel], [0, 2], "gather_first")
    early = _Exchange(True, [bf(w_pw, 0), bf(w_out, 0)], [0, 0], "gather_layer0", after=wint0)
    later = _Exchange(True, [bf(w_in, 1).T, bf(w_pw, 1), bf(w_out, 1)], [0, 0, 0], "gather_layer1", after=early.token)
    landed = {}

    def weights(l, cur):
        if l == 0:
            return wint0, later.token
        landed[1] = later.wait(cur)
        return landed[1][0], None

    def mixer_weights(l, cur):
        if l == 0:
            return early.wait(cur)
        return landed[1][1:]

    sent = [[] for _ in range(DEPTH)]

    def send_grads(l, grads, axes, tag):
        sent[l].append(_Exchange(False, grads, axes, f"scatter_{tag}_layer{l}", after=None))
        return sent[l][-1].token

    dx, loss_cols, small = local_step(xs, loss_target[0], weights, mixer_weights, dwk_f, norm_g, q_norm_g, k_norm_g, dw_bias,
                                      conv_ln_g, conv_ln_b, att_out_g, conv_out_g, send_grads, N_DEV)

    big = ((w_pw, m_w_pw, v_w_pw), (w_out, m_w_out, v_w_out), (dw_kernel, m_dw_kernel, v_dw_kernel), (w_in, m_w_in, v_w_in))
    results = [None] * len(big)
    after = dx
    for l in reversed(range(DEPTH)):
        parts = [p for ex in sent[l] for p in ex.wait(after)]
        for i, ((w, m, v), p) in enumerate(zip(big, parts)):
            results[i] = adamw(w, m, v, p, l, results[i], f"adamw_{w.shape[1]}_{w.shape[2]}_{l}")
        after = results[0][3]
    r_wpw, r_wout, r_dwk, r_win = results

    stack = lambda k: jnp.concatenate(small[k], axis=0)
    mine = _pack_small(D, stack("norm_g"), stack("dw_bias"), stack("conv_ln_g"), stack("conv_ln_b"), stack("att_out_g"),
                       stack("conv_out_g"), stack("q"), stack("k"), extra=loss_cols)
    (p_small,) = all_gather([mine], [0], "gather_small")
    pk = lambda n, dw, lg, lb, ao, co, q, k: _pack_small(D, n, dw, lg, lb, ao, co, q, k)[None]
    r_small = adamw(pk(norm_g, dw_bias, conv_ln_g, conv_ln_b, att_out_g, conv_out_g, q_norm_g, k_norm_g),
                    pk(m_norm_g, m_dw_bias, m_conv_ln_g, m_conv_ln_b, m_att_out_g, m_conv_out_g, m_q_norm_g, m_k_norm_g),
                    pk(v_norm_g, v_dw_bias, v_conv_ln_g, v_conv_ln_b, v_att_out_g, v_conv_out_g, v_q_norm_g, v_k_norm_g),
                    p_small.reshape(N_DEV, 16, D), 0, None, "adamw_small")
    r_small = [r[0] for r in r_small]
    loss = jnp.sum(r_small[0][14])

    outs = [loss, dx[None]]
    for i in range(4):
        n_, dwb, lg, lb, ao, co, q_, k_ = _unpack_small(r_small[i])
        outs += [n_, r_win[i], q_, k_, r_dwk[i], dwb, lg, lb, r_wpw[i], ao, co, r_wout[i]]
    return tuple(outs)


def local_step(xs, target, weights, mixer_weights, dwk_f, norm_g, q_norm_g, k_norm_g, dw_bias, conv_ln_g, conv_ln_b,
               att_out_g, conv_out_g, send_grads, owners):
    S, D = xs.shape
    cos2, sin2 = _rope_tables(S)
    dwk_f = jnp.pad(dwk_f, ((0, 0), (0, CONV_PAD - CONV_WIDTH), (0, 0)))

    def vec(p, l, zero=None):
        row = p[l].reshape(1, -1)
        return row if zero is None else row + zero

    saved = []
    cur = xs
    for l in range(DEPTH):
        wint, zero = weights(l, cur)
        proj, h = in_proj(cur, vec(norm_g, l, zero), wint, f"in_proj_{l}")
        qs, ks, vs = qk_prep(proj, vec(q_norm_g, l), vec(k_norm_g, l), cos2, sin2, f"qk_prep_{l}")
        os_, lses = [], []
        for i, d in enumerate(DILATIONS):
            o, lse = attn_fwd(qs[i], ks[i], vs[i], d, f"attn_fwd_{l}_d{d}")
            os_.append(o)
            lses.append(_from_branch(lse))
        att, att_y, lse = att_combine(os_, lses, proj, vec(att_out_g, l), f"att_combine_{l}")
        wpw, wout = mixer_weights(l, att_y)
        conv_y, z, conv, y = conv_fwd(proj, dwk_f[l], vec(dw_bias, l), vec(conv_ln_g, l), vec(conv_ln_b, l),
                                      vec(conv_out_g, l), wpw, f"conv_fwd_{l}")
        cat = jnp.concatenate([att_y, conv_y], axis=1)
        nxt = mm_nn(cat, wout, F32, f"out_proj_{l}", add=cur)
        saved.append(dict(x=cur, proj=proj, h=h, qs=qs, ks=ks, vs=vs, att=att, lse=lse, cat=cat, z=z, conv=conv, y=y,
                          wint=wint, wpw=wpw, wout=wout))
        cur = nxt

    dx, dxb, loss_cols = loss_head(cur, target, "loss_head")

    small = {k: [None] * DEPTH for k in ("norm_g", "dw_bias", "conv_ln_g", "conv_ln_b", "att_out_g", "conv_out_g", "q", "k")}
    for l in reversed(range(DEPTH)):
        sv = saved[l]
        proj = sv["proj"]
        dcat = mm_nt(dxb, sv["wout"], F32, f"dcat_{l}")
        g_wout = mm_nn(sv["cat"].T, dxb, BF16, f"dwout_{l}")
        dgate_c, small["conv_out_g"][l], dconv = gate_bwd(dcat, 1, sv["conv"], proj, 6, vec(conv_out_g, l), False,
                                                          f"conv_gate_bwd_{l}")
        dy, small["conv_ln_g"][l], small["conv_ln_b"][l], small["dw_bias"][l] = conv_bwd_ln(
            dconv, sv["wpw"], sv["y"], vec(conv_ln_g, l), vec(conv_ln_b, l), f"conv_bwd_ln_{l}")
        g_wpw = mm_nn(sv["z"].T, dconv, BF16, f"dwpw_{l}")
        da, db, gw = conv_bwd_dw(dy, proj, dwk_f[l], f"conv_bwd_dw_{l}")
        g_dwk = jnp.sum(gw, axis=1)[:CONV_WIDTH]
        zero = send_grads(l, [g_wpw, g_wout, g_dwk], [0, 0, 1], "mixer")
        dgate_a, small["att_out_g"][l], *datts, delta = gate_bwd(dcat, 0, sv["att"], proj, 3, vec(att_out_g, l, zero), True,
                                                                  f"att_gate_bwd_{l}")
        dqs, dks, dvs = [], [], []
        for i, d in enumerate(DILATIONS):
            dq, dk, dv = attn_bwd(sv["qs"][i], sv["ks"][i], sv["vs"][i], datts[i], _to_branch(sv["lse"], d),
                                  _to_branch(delta, d), d, f"attn_bwd_{l}_d{d}")
            dqs.append(dq)
            dks.append(dk)
            dvs.append(dv)
        dq, dk, dv, small["q"][l], small["k"][l] = qk_prep_bwd(dqs, dks, dvs, proj, vec(q_norm_g, l), vec(k_norm_g, l),
                                                                cos2, sin2, f"qk_prep_bwd_{l}")
        dproj = jnp.concatenate([dq, dk, dv, dgate_a, da, db, dgate_c], axis=1)
        g_win = mm_nn(sv["h"].T, dproj, BF16, f"dwin_{l}", owners=owners)
        zero = send_grads(l, [g_win], [None], "w_in")
        dx, dxb, small["norm_g"][l] = in_proj_bwd(dproj, sv["wint"], sv["x"], vec(norm_g, l, zero), dx, f"in_proj_bwd_{l}")

    return dx, loss_cols, small
```

```python
import jax
import jax.numpy as jnp
from jax import lax
from jax.experimental import pallas as pl
from jax.experimental.pallas import tpu as pltpu

F32 = jnp.float32
BF16 = jnp.bfloat16
SDS = jax.ShapeDtypeStruct
MESH = pl.DeviceIdType.MESH

N_DEV = 8
DEPTH = 2
HEAD_DIM = 128
CONV_WIDTH = 31
CONV_PAD = 32
DILATIONS = (1, 4, 16)
Q_BLOCK = 128
ROPE_THETA = 10000.0
EPS = 1e-6
NEG = -1e30
ADAM_LR, ADAM_B1, ADAM_B2, ADAM_EPS, ADAM_WD, ADAM_STEP = 0.001, 0.9, 0.999, 1e-08, 0.01, 10
LANE = 128
ROW_CHUNK = 64
MIB = 1 << 20
NT = (((1,), (1,)), ((), ()))
TN = (((0,), (0,)), ((), ()))


def _call(body, **kw):
    return pl.pallas_call(body, **kw)


def _params(vmem_mib):
    return pltpu.CompilerParams(vmem_limit_bytes=vmem_mib * MIB)


def _tile(dim, pref, mult):
    t = min(pref, dim)
    while dim % t or t % mult:
        t -= mult
    return t


def _sig(v):
    return jax.nn.sigmoid(v)


def _rstd(v):
    return lax.rsqrt(jnp.mean(v * v, axis=-1, keepdims=True) + EPS)


def _row(tm, cb, c=0):
    return pl.BlockSpec((tm, cb), lambda i: (i, c))


def _full(shape):
    return pl.BlockSpec(shape, lambda i: (0,) * len(shape))


def _halo_prev(tm, cb, c=0):
    k = tm // CONV_PAD
    return pl.BlockSpec((CONV_PAD, cb), lambda i: (jnp.maximum(i * k - 1, 0), c))


def _halo_next(tm, cb, nblk, c=0):
    k = tm // CONV_PAD
    return pl.BlockSpec((CONV_PAD, cb), lambda i: (jnp.minimum((i + 1) * k, nblk - 1), c))


def in_proj(x, g, wt, name):
    S, D = x.shape
    N = wt.shape[0]
    tm, tn = _tile(S, 512, 8), _tile(N, 1024, LANE)

    def body(x_ref, g_ref, w_ref, o_ref, h_ref):
        @pl.when(pl.program_id(1) == 0)
        def _():
            xf = x_ref[...]
            h_ref[...] = (xf * _rstd(xf) * g_ref[...]).astype(BF16)

        o_ref[...] = lax.dot_general(h_ref[...], w_ref[...], NT, preferred_element_type=F32)

    return _call(
        body, name=name, grid=(S // tm, N // tn),
        in_specs=[pl.BlockSpec((tm, D), lambda i, j: (i, 0)), pl.BlockSpec((1, D), lambda i, j: (0, 0)),
                  pl.BlockSpec((tn, D), lambda i, j: (j, 0))],
        out_specs=[pl.BlockSpec((tm, tn), lambda i, j: (i, j)), pl.BlockSpec((tm, D), lambda i, j: (i, 0))],
        out_shape=[SDS((S, N), F32), SDS((S, D), BF16)],
        compiler_params=_params(40),
    )(x, g, wt)


def mm_nn(a, b, out_dtype, name, add=None, owners=1):
    M, K = a.shape
    N = b.shape[1]
    tm, tn = _tile(M, 1024 if owners > 1 else 512, 8), _tile(N // owners, 1024, LANE)
    per = N // owners // tn

    def body(*refs):
        a_ref, b_ref = refs[0], refs[1]
        o_ref = refs[-1]
        acc = jnp.dot(a_ref[...], b_ref[...], preferred_element_type=F32)
        if add is not None:
            acc = acc + refs[2][...]
        o_ref[...] = acc.astype(out_dtype)

    in_specs = [pl.BlockSpec((tm, K), lambda i, j: (i, 0)), pl.BlockSpec((K, tn), lambda i, j: (0, j))]
    args = [a, b]
    if add is not None:
        in_specs.append(pl.BlockSpec((tm, tn), lambda i, j: (i, j)))
        args.append(add)
    if owners > 1:
        out_spec = pl.BlockSpec((None, tm, tn), lambda i, j: (j // per, i, j % per))
        out_shape = SDS((owners, M, N // owners), out_dtype)
    else:
        out_spec, out_shape = pl.BlockSpec((tm, tn), lambda i, j: (i, j)), SDS((M, N), out_dtype)
    return _call(
        body, name=name, grid=(M // tm, N // tn), in_specs=in_specs, out_specs=out_spec, out_shape=out_shape,
        compiler_params=_params(48),
    )(*args)


def mm_nt(a, b, out_dtype, name):
    M, K = a.shape
    N = b.shape[0]
    tm, tn = _tile(M, 512, 8), _tile(N, 1024, LANE)

    def body(a_ref, b_ref, o_ref):
        o_ref[...] = lax.dot_general(a_ref[...], b_ref[...], NT, preferred_element_type=F32).astype(out_dtype)

    return _call(
        body, name=name, grid=(M // tm, N // tn),
        in_specs=[pl.BlockSpec((tm, K), lambda i, j: (i, 0)), pl.BlockSpec((tn, K), lambda i, j: (j, 0))],
        out_specs=pl.BlockSpec((tm, tn), lambda i, j: (i, j)), out_shape=SDS((M, N), out_dtype),
        compiler_params=_params(40),
    )(a, b)


def in_proj_bwd(dproj, wt, x, g, dx_out, name):
    S, K = dproj.shape
    D = wt.shape[1]
    tm, tk = _tile(S, 512, 8), _tile(K, 1024, LANE)
    nk = K // tk

    def body(dp_ref, w_ref, x_ref, g_ref, dxo_ref, dx_ref, dxb_ref, gacc_ref):
        i, k = pl.program_id(0), pl.program_id(1)
        part = jnp.dot(dp_ref[...], w_ref[...], preferred_element_type=F32)

        @pl.when(k == 0)
        def _():
            dx_ref[...] = part

        @pl.when(k > 0)
        def _():
            dx_ref[...] += part

        @pl.when((k == 0) & (i == 0))
        def _():
            gacc_ref[...] = jnp.zeros_like(gacc_ref)

        @pl.when(k == nk - 1)
        def _():
            dh = dx_ref[...]
            xf = x_ref[...]
            r = _rstd(xf)
            n = xf * r
            gacc_ref[...] += jnp.sum(dh * n, axis=0, keepdims=True)
            dn = dh * g_ref[...]
            dx = r * (dn - n * jnp.mean(dn * n, axis=-1, keepdims=True)) + dxo_ref[...]
            dx_ref[...] = dx
            dxb_ref[...] = dx.astype(BF16)

    return _call(
        body, name=name, grid=(S // tm, nk),
        in_specs=[pl.BlockSpec((tm, tk), lambda i, k: (i, k)), pl.BlockSpec((tk, D), lambda i, k: (k, 0)),
                  pl.BlockSpec((tm, D), lambda i, k: (i, 0)), pl.BlockSpec((1, D), lambda i, k: (0, 0)),
                  pl.BlockSpec((tm, D), lambda i, k: (i, 0))],
        out_specs=[pl.BlockSpec((tm, D), lambda i, k: (i, 0)), pl.BlockSpec((tm, D), lambda i, k: (i, 0)),
                   pl.BlockSpec((1, D), lambda i, k: (0, 0))],
        out_shape=[SDS((S, D), F32), SDS((S, D), BF16), SDS((1, D), F32)],
        compiler_params=_params(54),
    )(dproj, wt, x, g, dx_out)


def _dil_specs(S, DA, tm, dtype):
    specs = [pl.BlockSpec((tm // d, d * DA), lambda i: (i, 0)) for d in DILATIONS]
    shapes = [SDS((S // d, d * DA), dtype) for d in DILATIONS]
    return specs, shapes


def _head_buf(tm, DA):
    return pltpu.VMEM((DA // HEAD_DIM, tm, HEAD_DIM), F32)


def _emit_dilated(buf_ref, dsts, tm, DA):
    for d, dst in zip(DILATIONS, dsts):
        for h in range(DA // HEAD_DIM):
            for r in range(d):
                rows = slice(None) if d == 1 else pl.ds(r, tm // d, stride=d)
                dst[:, r * DA + h * HEAD_DIM:r * DA + (h + 1) * HEAD_DIM] = buf_ref.at[h][rows, :].astype(BF16)


def _collect_dilated(acc_ref, parts, tm, DA):
    for d, p in zip(DILATIONS, parts):
        for h in range(DA // HEAD_DIM):
            for r in range(d):
                part = p[:, r * DA + h * HEAD_DIM:r * DA + (h + 1) * HEAD_DIM].astype(F32)
                if d == 1:
                    acc_ref[h] = part
                else:
                    rows = pl.ds(r, tm // d, stride=d)
                    acc_ref.at[h][rows, :] = acc_ref.at[h][rows, :] + part


def qk_prep(proj, gq, gk, cos2, sin2, name):
    S = proj.shape[0]
    DA = proj.shape[1] // 7
    H = DA // HEAD_DIM
    tm = _tile(S, 256, 16 * DILATIONS[-1])
    nd = len(DILATIONS)

    def body(q_ref, k_ref, v_ref, gq_ref, gk_ref, c_ref, s_ref, *rest):
        outs, buf_ref = rest[:3 * nd], rest[3 * nd]
        ct, st = c_ref[...], s_ref[...]
        for t, (src, g_ref) in enumerate(((q_ref, gq_ref), (k_ref, gk_ref))):
            gain = g_ref[...]
            for h in range(H):
                sl = slice(h * HEAD_DIM, (h + 1) * HEAD_DIM)
                xh = src[:, sl]
                n = xh * _rstd(xh) * gain
                buf_ref[h] = n * ct + pltpu.roll(n, HEAD_DIM // 2, 1) * st
            _emit_dilated(buf_ref, outs[t * nd:(t + 1) * nd], tm, DA)
        for h in range(H):
            buf_ref[h] = v_ref[:, h * HEAD_DIM:(h + 1) * HEAD_DIM]
        _emit_dilated(buf_ref, outs[2 * nd:], tm, DA)

    specs, shapes = _dil_specs(S, DA, tm, BF16)
    outs = _call(
        body, name=name, grid=(S // tm,),
        in_specs=[_row(tm, DA, 0), _row(tm, DA, 1), _row(tm, DA, 2), _full((1, HEAD_DIM)), _full((1, HEAD_DIM)),
                  _row(tm, HEAD_DIM), _row(tm, HEAD_DIM)],
        out_specs=specs * 3, out_shape=shapes * 3, scratch_shapes=[_head_buf(tm, DA)],
        compiler_params=_params(48),
    )(proj, proj, proj, gq, gk, cos2, sin2)
    return outs[:nd], outs[nd:2 * nd], outs[2 * nd:]


def qk_prep_bwd(dqs, dks, dvs, proj, gq, gk, cos2, sin2, name):
    S = proj.shape[0]
    DA = proj.shape[1] // 7
    H = DA // HEAD_DIM
    tm = _tile(S, 256, 16 * DILATIONS[-1])
    nb = len(dqs)

    def body(*refs):
        dq_refs, dk_refs, dv_refs = refs[:nb], refs[nb:2 * nb], refs[2 * nb:3 * nb]
        q_ref, k_ref, gq_ref, gk_ref, c_ref, s_ref = refs[3 * nb:3 * nb + 6]
        dqo_ref, dko_ref, dvo_ref, gqa_ref, gka_ref, acc_ref = refs[3 * nb + 6:]
        ct, st = c_ref[...], s_ref[...]

        @pl.when(pl.program_id(0) == 0)
        def _():
            gqa_ref[...] = jnp.zeros_like(gqa_ref)
            gka_ref[...] = jnp.zeros_like(gka_ref)

        for parts, x_ref, g_ref, dst, gacc in ((dq_refs, q_ref, gq_ref, dqo_ref, gqa_ref),
                                               (dk_refs, k_ref, gk_ref, dko_ref, gka_ref)):
            gain = g_ref[...]
            gsum = jnp.zeros((1, HEAD_DIM), F32)
            _collect_dilated(acc_ref, parts, tm, DA)
            for h in range(H):
                sl = slice(h * HEAD_DIM, (h + 1) * HEAD_DIM)
                dout = acc_ref[h]
                dn = dout * ct + pltpu.roll(dout * st, HEAD_DIM // 2, 1)
                xh = x_ref[:, sl]
                r = _rstd(xh)
                xn = xh * r
                gsum = gsum + jnp.sum(dn * xn, axis=0, keepdims=True)
                dnn = dn * gain
                dst[:, sl] = (r * (dnn - xn * jnp.mean(dnn * xn, axis=-1, keepdims=True))).astype(BF16)
            gacc[...] += gsum
        _collect_dilated(acc_ref, dv_refs, tm, DA)
        for h in range(H):
            dvo_ref[:, h * HEAD_DIM:(h + 1) * HEAD_DIM] = acc_ref[h].astype(BF16)

    specs, _ = _dil_specs(S, DA, tm, BF16)
    return _call(
        body, name=name, grid=(S // tm,),
        in_specs=specs * 3 + [_row(tm, DA, 0), _row(tm, DA, 1), _full((1, HEAD_DIM)),
                              _full((1, HEAD_DIM)), _row(tm, HEAD_DIM), _row(tm, HEAD_DIM)],
        out_specs=[_row(tm, DA)] * 3 + [_full((1, HEAD_DIM))] * 2,
        out_shape=[SDS((S, DA), BF16)] * 3 + [SDS((1, HEAD_DIM), F32)] * 2,
        scratch_shapes=[_head_buf(tm, DA)],
        compiler_params=_params(48),
    )(*dqs, *dks, *dvs, proj, proj, gq, gk, cos2, sin2)


def _band_masks(n):
    row = lax.broadcasted_iota(jnp.int32, (Q_BLOCK, Q_BLOCK), 0)
    col = lax.broadcasted_iota(jnp.int32, (Q_BLOCK, Q_BLOCK), 1)
    first = jnp.where(n > 0, 0, Q_BLOCK + 1)
    return col <= row, (col - row) >= first


def attn_fwd(qh, kh, vb, d, name):
    L = qh.shape[0]
    DA = qh.shape[1] // d
    H = DA // HEAD_DIM
    nb = L // Q_BLOCK
    scale = HEAD_DIM ** -0.5
    view = (L, d * DA)

    def body(q_ref, kc_ref, kp_ref, vc_ref, vp_ref, o_ref, lse_ref):
        mask_c, mask_p = _band_masks(pl.program_id(1))
        for h in range(H):
            sl = slice(h * HEAD_DIM, (h + 1) * HEAD_DIM)
            q = q_ref[:, sl]
            sc = lax.dot_general(q, kc_ref[:, sl], NT, preferred_element_type=F32) * scale
            sp = lax.dot_general(q, kp_ref[:, sl], NT, preferred_element_type=F32) * scale
            sc = jnp.where(mask_c, sc, NEG)
            sp = jnp.where(mask_p, sp, NEG)
            m = jnp.maximum(jnp.max(sc, axis=-1, keepdims=True), jnp.max(sp, axis=-1, keepdims=True))
            pc = jnp.exp(sc - m)
            pp = jnp.exp(sp - m)
            l = jnp.sum(pc, axis=-1, keepdims=True) + jnp.sum(pp, axis=-1, keepdims=True)
            o = jnp.dot(pc.astype(BF16), vc_ref[:, sl], preferred_element_type=F32)
            o = o + jnp.dot(pp.astype(BF16), vp_ref[:, sl], preferred_element_type=F32)
            o_ref[:, sl] = o / l
            lse_ref[0, :, h:h + 1] = m + jnp.log(l)

    cur = pl.BlockSpec((Q_BLOCK, DA), lambda r, n: (n, r))
    prev = pl.BlockSpec((Q_BLOCK, DA), lambda r, n: (jnp.maximum(n - 1, 0), r))
    o, lse = _call(
        body, name=name, grid=(d, nb), in_specs=[cur, cur, prev, cur, prev],
        out_specs=[cur, pl.BlockSpec((1, Q_BLOCK, H), lambda r, n: (r, n, 0))],
        out_shape=[SDS(view, F32), SDS((d, L, H), F32)],
        compiler_params=_params(32),
    )(qh, kh, kh, vb, vb)
    return o, lse


def attn_bwd(qh, kh, vb, da, lse_d, delta_d, d, name):
    L = qh.shape[0]
    DA = qh.shape[1] // d
    H = DA // HEAD_DIM
    nb = L // Q_BLOCK
    scale = HEAD_DIM ** -0.5
    view = (L, d * DA)

    def body(q_ref, kc_ref, kp_ref, vc_ref, vp_ref, do_ref, lse_ref, dl_ref, dq_ref, dk_ref, dv_ref, dkc_ref, dvc_ref):
        n = pl.program_id(1)
        mask_c, mask_p = _band_masks(n)

        @pl.when(n == 0)
        def _():
            dkc_ref[...] = jnp.zeros_like(dkc_ref)
            dvc_ref[...] = jnp.zeros_like(dvc_ref)

        @pl.when(n < nb)
        def _():
            for h in range(H):
                sl = slice(h * HEAD_DIM, (h + 1) * HEAD_DIM)
                q, kc, kp, vc, vp, do = q_ref[:, sl], kc_ref[:, sl], kp_ref[:, sl], vc_ref[:, sl], vp_ref[:, sl], do_ref[:, sl]
                lse = lse_ref[0, :, h:h + 1]
                dl = dl_ref[0, :, h:h + 1]
                sc = lax.dot_general(q, kc, NT, preferred_element_type=F32) * scale
                sp = lax.dot_general(q, kp, NT, preferred_element_type=F32) * scale
                pc = jnp.exp(jnp.where(mask_c, sc, NEG) - lse)
                pp = jnp.exp(jnp.where(mask_p, sp, NEG) - lse)
                dpc = lax.dot_general(do, vc, NT, preferred_element_type=F32)
                dpp = lax.dot_general(do, vp, NT, preferred_element_type=F32)
                dsc = (pc * (dpc - dl) * scale).astype(BF16)
                dsp = (pp * (dpp - dl) * scale).astype(BF16)
                dq = jnp.dot(dsc, kc, preferred_element_type=F32) + jnp.dot(dsp, kp, preferred_element_type=F32)
                dq_ref[:, sl] = dq.astype(BF16)
                dk_prev = lax.dot_general(dsp, q, TN, preferred_element_type=F32)
                dv_prev = lax.dot_general(pp.astype(BF16), do, TN, preferred_element_type=F32)
                dk_ref[:, sl] = (dkc_ref[:, sl] + dk_prev).astype(BF16)
                dv_ref[:, sl] = (dvc_ref[:, sl] + dv_prev).astype(BF16)
                dkc_ref[:, sl] = lax.dot_general(dsc, q, TN, preferred_element_type=F32)
                dvc_ref[:, sl] = lax.dot_general(pc.astype(BF16), do, TN, preferred_element_type=F32)

        @pl.when(n == nb)
        def _():
            dk_ref[...] = dkc_ref[...].astype(BF16)
            dv_ref[...] = dvc_ref[...].astype(BF16)

    cur = pl.BlockSpec((Q_BLOCK, DA), lambda r, n: (jnp.minimum(n, nb - 1), r))
    prev = pl.BlockSpec((Q_BLOCK, DA), lambda r, n: (jnp.clip(n - 1, 0, nb - 1), r))
    late = pl.BlockSpec((Q_BLOCK, DA), lambda r, n: (jnp.maximum(n - 1, 0), r))
    stat = pl.BlockSpec((1, Q_BLOCK, H), lambda r, n: (r, jnp.minimum(n, nb - 1), 0))
    dq, dk, dv = _call(
        body, name=name, grid=(d, nb + 1), in_specs=[cur, cur, prev, cur, prev, cur, stat, stat],
        out_specs=[cur, late, late], out_shape=[SDS(view, BF16)] * 3,
        scratch_shapes=[pltpu.VMEM((Q_BLOCK, DA), F32), pltpu.VMEM((Q_BLOCK, DA), F32)],
        compiler_params=_params(32),
    )(qh, kh, kh, vb, vb, da, lse_d, delta_d)
    return dq, dk, dv


def _to_branch(stat, d):
    S, H = stat.shape
    return stat.reshape(S // d, d, H).transpose(1, 0, 2)


def _from_branch(stat):
    d, L, H = stat.shape
    return stat.transpose(1, 0, 2).reshape(L * d, H)


def att_combine(os_, lses, proj, gain, name):
    S, DA = os_[0].shape
    H = DA // HEAD_DIM
    tm = _tile(S, 256, 16 * DILATIONS[-1])
    nb = len(os_)

    def body(*refs):
        o_views, l_refs = refs[:nb], refs[nb:2 * nb]
        gate_ref, gain_ref, att_ref, y_ref, lse_ref = refs[2 * nb:2 * nb + 5]
        bufs = refs[2 * nb + 5:]
        for d, view, buf in zip(DILATIONS[1:], o_views[1:], bufs):
            for h in range(H):
                for r in range(d):
                    buf.at[h][pl.ds(r, tm // d, stride=d), :] = view[:, r * DA + h * HEAD_DIM:r * DA + (h + 1) * HEAD_DIM]
        ls = [r[...] for r in l_refs]
        top = ls[0]
        for l in ls[1:]:
            top = jnp.maximum(top, l)
        den = jnp.exp(ls[0] - top)
        for l in ls[1:]:
            den = den + jnp.exp(l - top)
        lse = top + jnp.log(den)
        lse_ref[...] = lse
        ws = [jnp.exp(l - lse) for l in ls]
        for h in range(H):
            sl = slice(h * HEAD_DIM, (h + 1) * HEAD_DIM)
            acc = ws[0][:, h:h + 1] * o_views[0][:, sl]
            for w, buf in zip(ws[1:], bufs):
                acc = acc + w[:, h:h + 1] * buf[h]
            att_ref[:, sl] = acc
        a = att_ref[...]
        g = gate_ref[...]
        y_ref[...] = (a * _rstd(a) * gain_ref[...] * (g * _sig(g))).astype(BF16)

    specs, _ = _dil_specs(S, DA, tm, F32)
    return _call(
        body, name=name, grid=(S // tm,),
        in_specs=specs + [_row(tm, H)] * nb + [_row(tm, DA, 3), _full((1, DA))],
        out_specs=[_row(tm, DA), _row(tm, DA), _row(tm, H)],
        out_shape=[SDS((S, DA), F32), SDS((S, DA), BF16), SDS((S, H), F32)],
        scratch_shapes=[_head_buf(tm, DA)] * (nb - 1),
        compiler_params=_params(48),
    )(*os_, *lses, proj, gain)


def gate_bwd(dcat, cblk, a, proj, gate_blk, gain, dilated, name):
    S, DA = a.shape
    H = DA // HEAD_DIM
    tm = _tile(S, 256, 16 * DILATIONS[-1])
    nd = len(DILATIONS) if dilated else 1

    def body(dy_ref, a_ref, gate_ref, gain_ref, dg_ref, gacc_ref, *rest):
        @pl.when(pl.program_id(0) == 0)
        def _():
            gacc_ref[...] = jnp.zeros_like(gacc_ref)

        dy, av, g, gain_v = dy_ref[...], a_ref[...], gate_ref[...], gain_ref[...]
        r = _rstd(av)
        n = av * r
        sg = _sig(g)
        dg_ref[...] = (dy * (n * gain_v) * (sg * (1.0 + g * (1.0 - sg)))).astype(BF16)
        drn = dy * (g * sg)
        gacc_ref[...] += jnp.sum(drn * n, axis=0, keepdims=True)
        dn = drn * gain_v
        da = r * (dn - n * jnp.mean(dn * n, axis=-1, keepdims=True))
        if dilated:
            da_refs, delta_ref, buf_ref = rest[:nd], rest[nd], rest[nd + 1]
            for h in range(H):
                buf_ref[h] = da[:, h * HEAD_DIM:(h + 1) * HEAD_DIM]
            _emit_dilated(buf_ref, da_refs, tm, DA)
            prod = da * av
            for h in range(H):
                delta_ref[:, h:h + 1] = jnp.sum(prod[:, h * HEAD_DIM:(h + 1) * HEAD_DIM], axis=-1, keepdims=True)
        else:
            rest[0][...] = da.astype(BF16)

    out_specs = [_row(tm, DA), _full((1, DA))]
    out_shape = [SDS((S, DA), BF16), SDS((1, DA), F32)]
    scratch = []
    if dilated:
        specs, shapes = _dil_specs(S, DA, tm, BF16)
        out_specs += specs + [_row(tm, H)]
        out_shape += shapes + [SDS((S, H), F32)]
        scratch = [_head_buf(tm, DA)]
    else:
        out_specs.append(_row(tm, DA))
        out_shape.append(SDS((S, DA), BF16))
    return _call(
        body, name=name, grid=(S // tm,),
        in_specs=[_row(tm, DA, cblk), _row(tm, DA), _row(tm, DA, gate_blk), _full((1, DA))],
        out_specs=out_specs, out_shape=out_shape, scratch_shapes=scratch, compiler_params=_params(48),
    )(dcat, a, proj, gain)


def _fill_u(i, a_ref, b_ref, ah_ref, bh_ref, uext_ref, tm):
    uext_ref[pl.ds(CONV_PAD, tm), :] = a_ref[...] * _sig(b_ref[...])
    uh = ah_ref[...] * _sig(bh_ref[...])
    uext_ref[pl.ds(0, CONV_PAD), :] = jnp.where(i > 0, uh, 0.0)


def conv_fwd(proj, wk, bias, ln_g, ln_b, out_g, wpw, name):
    S = proj.shape[0]
    DC = proj.shape[1] // 7
    tm = _tile(S, 128, ROW_CHUNK)
    lead = CONV_PAD - (CONV_WIDTH - 1)

    def body(a_ref, b_ref, ah_ref, bh_ref, gate_ref, wk_ref, bias_ref, lg_ref, lb_ref, og_ref, wpw_ref,
             cy_ref, z_ref, conv_ref, y_ref, uext_ref):
        _fill_u(pl.program_id(0), a_ref, b_ref, ah_ref, bh_ref, uext_ref, tm)

        def cols(cc, carry):
            c0 = pl.multiple_of(cc * LANE, LANE)
            for rr in range(tm // ROW_CHUNK):
                acc = jnp.broadcast_to(bias_ref[:, pl.ds(c0, LANE)], (ROW_CHUNK, LANE))
                for j in range(CONV_WIDTH):
                    acc = acc + wk_ref[j:j + 1, pl.ds(c0, LANE)] * uext_ref[pl.ds(rr * ROW_CHUNK + lead + j, ROW_CHUNK), pl.ds(c0, LANE)]
                y_ref[pl.ds(rr * ROW_CHUNK, ROW_CHUNK), pl.ds(c0, LANE)] = acc
            return carry

        lax.fori_loop(0, DC // LANE, cols, 0)
        y = y_ref[...]
        yc = y - jnp.mean(y, axis=-1, keepdims=True)
        ln = yc * _rstd(yc) * lg_ref[...] + lb_ref[...]
        zb = (ln * _sig(ln)).astype(BF16)
        z_ref[...] = zb
        conv = jnp.dot(zb, wpw_ref[...], preferred_element_type=F32)
        conv_ref[...] = conv
        g = gate_ref[...]
        cy_ref[...] = (conv * _rstd(conv) * og_ref[...] * (g * _sig(g))).astype(BF16)

    vec = _full((1, DC))
    return _call(
        body, name=name, grid=(S // tm,),
        in_specs=[_row(tm, DC, 4), _row(tm, DC, 5), _halo_prev(tm, DC, 4), _halo_prev(tm, DC, 5), _row(tm, DC, 6),
                  _full((CONV_PAD, DC)), vec, vec, vec, vec, _full((DC, DC))],
        out_specs=[_row(tm, DC)] * 4,
        out_shape=[SDS((S, DC), BF16), SDS((S, DC), BF16), SDS((S, DC), F32), SDS((S, DC), F32)],
        scratch_shapes=[pltpu.VMEM((CONV_PAD + tm, DC), F32)],
        compiler_params=_params(48),
    )(proj, proj, proj, proj, proj, wk, bias, ln_g, ln_b, out_g, wpw)


def conv_bwd_ln(dconv, wpw, y, ln_g, ln_b, name):
    S, DC = y.shape
    tm = _tile(S, 256, 8)

    def body(dc_ref, wpw_ref, y_ref, lg_ref, lb_ref, dy_ref, glg_ref, glb_ref, gb_ref):
        @pl.when(pl.program_id(0) == 0)
        def _():
            glg_ref[...] = jnp.zeros_like(glg_ref)
            glb_ref[...] = jnp.zeros_like(glb_ref)
            gb_ref[...] = jnp.zeros_like(gb_ref)

        dz = lax.dot_general(dc_ref[...], wpw_ref[...], NT, preferred_element_type=F32)
        yv = y_ref[...]
        yc = yv - jnp.mean(yv, axis=-1, keepdims=True)
        rstd = _rstd(yc)
        yhat = yc * rstd
        ln = yhat * lg_ref[...] + lb_ref[...]
        sg = _sig(ln)
        dln = dz * (sg * (1.0 + ln * (1.0 - sg)))
        glb_ref[...] += jnp.sum(dln, axis=0, keepdims=True)
        glg_ref[...] += jnp.sum(dln * yhat, axis=0, keepdims=True)
        dyh = dln * lg_ref[...]
        dy = rstd * (dyh - jnp.mean(dyh, axis=-1, keepdims=True) - yhat * jnp.mean(dyh * yhat, axis=-1, keepdims=True))
        dy_ref[...] = dy
        gb_ref[...] += jnp.sum(dy, axis=0, keepdims=True)

    vec = _full((1, DC))
    return _call(
        body, name=name, grid=(S // tm,),
        in_specs=[_row(tm, DC), _full((DC, DC)), _row(tm, DC), vec, vec],
        out_specs=[_row(tm, DC), vec, vec, vec],
        out_shape=[SDS((S, DC), F32)] + [SDS((1, DC), F32)] * 3,
        compiler_params=_params(48),
    )(dconv, wpw, y, ln_g, ln_b)


def conv_bwd_dw(dy, proj, wk, name):
    S, DC = dy.shape
    tm = _tile(S, 128, ROW_CHUNK)
    nsteps = S // tm
    lead = CONV_PAD - (CONV_WIDTH - 1)
    groups = ROW_CHUNK // 8

    def body(dy_ref, dyn_ref, a_ref, b_ref, ah_ref, bh_ref, wk_ref, da_ref, db_ref, gw_ref, uext_ref, dyext_ref, du_ref):
        i = pl.program_id(0)

        @pl.when(i == 0)
        def _():
            gw_ref[...] = jnp.zeros_like(gw_ref)

        _fill_u(i, a_ref, b_ref, ah_ref, bh_ref, uext_ref, tm)
        dyext_ref[pl.ds(0, tm), :] = dy_ref[...]
        dyext_ref[pl.ds(tm, CONV_PAD), :] = jnp.where(i < nsteps - 1, dyn_ref[...], 0.0)

        def cols(cc, carry):
            c0 = pl.multiple_of(cc * LANE, LANE)
            lanes = pl.ds(c0, LANE)
            for rr in range(tm // ROW_CHUNK):
                acc = jnp.zeros((ROW_CHUNK, LANE), F32)
                for j in range(CONV_WIDTH):
                    acc = acc + wk_ref[j:j + 1, lanes] * dyext_ref[pl.ds(rr * ROW_CHUNK + CONV_WIDTH - 1 - j, ROW_CHUNK), lanes]
                du_ref[pl.ds(rr * ROW_CHUNK, ROW_CHUNK), lanes] = acc
            for j in range(CONV_WIDTH):
                part = jnp.zeros((8, LANE), F32)
                for rr in range(tm // ROW_CHUNK):
                    prod = dyext_ref[pl.ds(rr * ROW_CHUNK, ROW_CHUNK), lanes] * uext_ref[pl.ds(rr * ROW_CHUNK + lead + j, ROW_CHUNK), lanes]
                    for k in range(groups):
                        part = part + prod[8 * k:8 * k + 8]
                gw_ref[j, :, lanes] += part
            return carry

        lax.fori_loop(0, DC // LANE, cols, 0)
        du = du_ref[...]
        sb = _sig(b_ref[...])
        da_ref[...] = (du * sb).astype(BF16)
        db_ref[...] = (du * a_ref[...] * sb * (1.0 - sb)).astype(BF16)

    return _call(
        body, name=name, grid=(nsteps,),
        in_specs=[_row(tm, DC), _halo_next(tm, DC, S // CONV_PAD), _row(tm, DC, 4), _row(tm, DC, 5),
                  _halo_prev(tm, DC, 4), _halo_prev(tm, DC, 5), _full((CONV_PAD, DC))],
        out_specs=[_row(tm, DC), _row(tm, DC), _full((CONV_PAD, 8, DC))],
        out_shape=[SDS((S, DC), BF16), SDS((S, DC), BF16), SDS((CONV_PAD, 8, DC), F32)],
        scratch_shapes=[pltpu.VMEM((CONV_PAD + tm, DC), F32), pltpu.VMEM((tm + CONV_PAD, DC), F32), pltpu.VMEM((tm, DC), F32)],
        compiler_params=_params(40),
    )(dy, dy, proj, proj, proj, proj, wk)


def loss_head(xo, target, name):
    S, D = xo.shape
    tm = _tile(S, 256, 8)

    def body(x_ref, t_ref, dy_ref, dyb_ref, acc_ref):
        @pl.when(pl.program_id(0) == 0)
        def _():
            acc_ref[...] = jnp.zeros_like(acc_ref)

        err = x_ref[...] - t_ref[...]
        dy = err * (1.0 / D)
        dy_ref[...] = dy
        dyb_ref[...] = dy.astype(BF16)
        acc_ref[...] += jnp.sum(err * dy, axis=0, keepdims=True) * 0.5

    return _call(
        body, name=name, grid=(S // tm,), in_specs=[_row(tm, D), _row(tm, D)],
        out_specs=[_row(tm, D), _row(tm, D), _full((1, D))],
        out_shape=[SDS((S, D), F32), SDS((S, D), BF16), SDS((1, D), F32)],
        compiler_params=_params(32),
    )(xo, target)


def _coords():
    x, y, c = lax.axis_index("x"), lax.axis_index("y"), lax.axis_index("c")
    return x, y, c


def _lin(p):
    return 4 * p[0] + 2 * p[1] + p[2]


def _slot(ref, axis, idx, size):
    index = [slice(None)] * len(ref.shape)
    index[axis] = pl.ds(idx * size, size)
    return ref.at[tuple(index)]


def all_gather(blocks, axes, name):
    na = len(blocks)
    sizes = [b.shape[ax] for b, ax in zip(blocks, axes)]
    fulls = [SDS(b.shape[:ax] + (N_DEV * b.shape[ax],) + b.shape[ax + 1:], b.dtype) for b, ax in zip(blocks, axes)]

    def body(*refs):
        in_refs, out_refs = refs[:na], refs[na:2 * na]
        send_sems, recv_sems, local_sems = refs[2 * na:]
        x, y, c = _coords()
        me, sibling = (x, y, c), (x, y, 1 - c)
        chips = [(1 - x, y), (x, 1 - y), (1 - x, 1 - y)]

        def place(a, p):
            return _slot(out_refs[a], axes[a], _lin(p), sizes[a])

        def copy(a, k, block, to, src=None):
            return pltpu.make_async_remote_copy(
                src_ref=place(a, block) if src is None else src, dst_ref=place(a, block),
                send_sem=send_sems.at[a, k], recv_sem=recv_sems.at[a, k], device_id=to, device_id_type=MESH)

        mine = [pltpu.make_async_copy(in_refs[a], place(a, me), local_sems.at[a]) for a in range(na)]
        for cp in mine:
            cp.start()
        first = []
        for a in range(na):
            first.append(copy(a, 0, me, sibling, src=in_refs[a]))
            first += [copy(a, 1 + j, me, (*chip, c), src=in_refs[a]) for j, chip in enumerate(chips)]
        for cp in first:
            cp.start()
        passed = []
        for j, chip in enumerate(chips):
            for a in range(na):
                copy(a, 1 + j, (*chip, c), me).wait_recv()
                cp = copy(a, 4 + j, (*chip, c), sibling)
                cp.start()
                passed.append(cp)
        for a in range(na):
            copy(a, 0, sibling, me).wait_recv()
            for j, chip in enumerate(chips):
                copy(a, 4 + j, (*chip, 1 - c), me).wait_recv()
        for cp in first + passed:
            cp.wait_send()
        for cp in mine:
            cp.wait()

    hbm = pl.BlockSpec(memory_space=pltpu.HBM)
    return _call(
        body, name=name, in_specs=[hbm] * na, out_specs=[hbm] * na, out_shape=fulls,
        scratch_shapes=[pltpu.SemaphoreType.DMA((na, 7)), pltpu.SemaphoreType.DMA((na, 7)), pltpu.SemaphoreType.DMA((na,))],
    )(*blocks)


class _Exchange:
    def __init__(self, gather, srcs, axes, name, after):
        self.gather, self.axes, self.name, self.na = gather, axes, name, len(srcs)
        if gather:
            self.sizes = [s.shape[ax] for s, ax in zip(srcs, axes)]
            lands = [s.shape[:ax] + (N_DEV * s.shape[ax],) + s.shape[ax + 1:] for s, ax in zip(srcs, axes)]
        else:
            self.sizes = [None if ax is None else s.shape[ax] // N_DEV for s, ax in zip(srcs, axes)]
            lands = [s.shape if ax is None else (N_DEV,) + s.shape[:ax] + (sz,) + s.shape[ax + 1:]
                     for s, ax, sz in zip(srcs, axes, self.sizes)]
        self.kinds = [pltpu.HBM(s.shape, s.dtype) for s in srcs] + [pltpu.HBM(l, s.dtype) for l, s in zip(lands, srcs)]
        lands = [lax.empty(l, s.dtype) for l, s in zip(lands, srcs)]
        after = jnp.zeros((8, LANE), F32) if after is None else after
        self._start([pltpu.with_memory_space_constraint(t, pltpu.HBM) for t in list(srcs) + lands], after)

    def _src(self, a, ref, owner):
        if self.gather:
            return ref
        return ref.at[_lin(owner)] if self.axes[a] is None else _slot(ref, self.axes[a], _lin(owner), self.sizes[a])

    def _dst(self, a, land, sender):
        return _slot(land, self.axes[a], _lin(sender), self.sizes[a]) if self.gather else land.at[_lin(sender)]

    def _copies(self, refs, send_sems, recv_sems):
        na = self.na
        me = _coords()
        flips = [(k >> 2 & 1, k >> 1 & 1, k & 1) for k in range(1, N_DEV)]
        peers = [tuple(1 - v if f else v for v, f in zip(me, flip)) for flip in flips]
        sends, arrivals = [], []
        for a in range(na):
            for k, peer in enumerate(peers):
                pair = dict(send_sem=send_sems.at[7 * a + k], recv_sem=recv_sems.at[7 * a + k], device_id=peer, device_id_type=MESH)
                sends.append(pltpu.make_async_remote_copy(
                    src_ref=self._src(a, refs[a], peer), dst_ref=self._dst(a, refs[na + a], me), **pair))
                arrivals.append(pltpu.make_async_remote_copy(
                    src_ref=self._src(a, refs[a], me), dst_ref=self._dst(a, refs[na + a], peer), **pair))
        return sends, arrivals

    def _place_own(self, operands):
        na = self.na
        me = jnp.reshape(_lin(_coords()), (1,)).astype(jnp.int32)
        lands = []
        for a in range(na):
            src, land, ax = operands[a], operands[na + a], self.axes[a]
            if ax == 1:
                R, C = src.shape[0], self.sizes[a]
                steps, tile = 1, (R, C)
                in_map = lambda i, me_ref: (0, me_ref[0])
            elif ax is None:
                R, C = src.shape[1:]
                tr = _tile(R, 512, 16)
                steps, tile = R // tr, (None, tr, C)
                in_map = lambda i, me_ref: (me_ref[0], i, 0)
            else:
                R, C = (src.shape[0] if self.gather else self.sizes[a]), src.shape[1]
                tr = _tile(R, 512, 16)
                steps, tile = R // tr, (tr, C)
                in_map = (lambda i, me_ref: (i, 0)) if self.gather else (lambda i, me_ref, n=R // tr: (me_ref[0] * n + i, 0))
            if self.gather:
                out_spec = pl.BlockSpec(tile, lambda i, me_ref, n=steps: (me_ref[0] * n + i, 0))
            else:
                out_spec = pl.BlockSpec((None,) + tuple(t for t in tile if t is not None), lambda i, me_ref: (me_ref[0], i, 0))

            def body(me_ref, src_ref, land_ref, out_ref):
                out_ref[...] = src_ref[...]

            lands.append(_call(
                body, name=f"{self.name}_own{a}",
                grid_spec=pltpu.PrefetchScalarGridSpec(
                    num_scalar_prefetch=1, grid=(steps,),
                    in_specs=[pl.BlockSpec(tile, in_map), pl.BlockSpec(memory_space=pl.ANY)], out_specs=out_spec),
                out_shape=SDS(land.shape, land.dtype), input_output_aliases={2: 0}, compiler_params=_params(32),
            )(me, src, land))
        return operands[:na] + lands

    def _start(self, operands, after):
        na = self.na
        operands = self._place_own(operands)

        def body(*refs):
            ins = refs[:2 * na]
            send_sems, recv_sems, token_ref = refs[2 * na + 1], refs[2 * na + 2], refs[4 * na + 3]
            for cp in self._copies(ins, send_sems, recv_sems)[0]:
                cp.start()
            token_ref[...] = jnp.zeros_like(token_ref)

        hbm = pl.BlockSpec(memory_space=pltpu.HBM)
        sem = pl.BlockSpec(memory_space=pltpu.SEMAPHORE)
        outs = _call(
            body, name=self.name + "_start",
            in_specs=[hbm] * (2 * na) + [pl.BlockSpec(memory_space=pl.ANY)],
            out_specs=[sem, sem] + [hbm] * (2 * na) + [pl.BlockSpec(memory_space=pltpu.VMEM)],
            out_shape=[pltpu.SemaphoreType.DMA((7 * na,)), pltpu.SemaphoreType.DMA((7 * na,))] + self.kinds + [SDS((8, LANE), F32)],
            input_output_aliases={i: 2 + i for i in range(2 * na)},
            compiler_params=pltpu.CompilerParams(has_side_effects=pltpu.SideEffectType.DATAFLOW_SIDE_EFFECTING),
        )(*operands, after)
        self.sems, self.thru, self.token = outs[:2], outs[2:2 + 2 * na], outs[2 + 2 * na][0:1, 0:1]

    def wait(self, after):
        na = self.na

        def body(*refs):
            ins, send_sems, recv_sems = refs[:2 * na], refs[2 * na], refs[2 * na + 1]
            sends, arrivals = self._copies(ins, send_sems, recv_sems)
            for cp in sends:
                cp.wait_send()
            for cp in arrivals:
                cp.wait_recv()

        hbm = pl.BlockSpec(memory_space=pltpu.HBM)
        sem = pl.BlockSpec(memory_space=pltpu.SEMAPHORE)
        outs = _call(
            body, name=self.name + "_wait",
            in_specs=[hbm] * (2 * na) + [sem, sem, pl.BlockSpec(memory_space=pl.ANY)],
            out_specs=[hbm] * (2 * na), out_shape=self.kinds,
            input_output_aliases={i: i for i in range(2 * na)},
            compiler_params=pltpu.CompilerParams(has_side_effects=pltpu.SideEffectType.DATAFLOW_SIDE_EFFECTING),
        )(*self.thru, *self.sems, after)
        return outs[na:]


def adamw(w, m, v, parts, layer, prev, name):
    nl, R, C = w.shape
    tr = _tile(R, 128, 8) if R % 8 == 0 else R

    def body(w_ref, m_ref, v_ref, p_ref, *rest):
        g_ref, d_ref, mo_ref, vo_ref = rest[-4:]
        g = p_ref[0].astype(F32)
        for s in range(1, N_DEV):
            g = g + p_ref[s].astype(F32)
        mn = ADAM_B1 * m_ref[0] + (1.0 - ADAM_B1) * g
        vn = ADAM_B2 * v_ref[0] + (1.0 - ADAM_B2) * (g * g)
        m_hat = mn / (1.0 - ADAM_B1 ** ADAM_STEP)
        v_hat = vn / (1.0 - ADAM_B2 ** ADAM_STEP)
        g_ref[0] = g
        d_ref[0] = -ADAM_LR * (m_hat / (jnp.sqrt(v_hat) + ADAM_EPS) + ADAM_WD * w_ref[0])
        mo_ref[0] = mn
        vo_ref[0] = vn

    row = pl.BlockSpec((1, tr, C), lambda i: (layer, i, 0))
    carried = [] if prev is None else list(prev)
    return _call(
        body, name=name, grid=(R // tr,),
        in_specs=[row, row, row, pl.BlockSpec((N_DEV, tr, C), lambda i: (0, i, 0))] + [pl.BlockSpec(memory_space=pl.ANY)] * len(carried),
        out_specs=[row] * 4, out_shape=[SDS((nl, R, C), F32)] * 4,
        input_output_aliases={4 + k: k for k in range(len(carried))},
        compiler_params=_params(48),
    )(w, m, v, parts, *carried)


def _rope_tables(S):
    inv_freq = 1.0 / (ROPE_THETA ** (jnp.arange(0, HEAD_DIM, 2, dtype=F32) / HEAD_DIM))
    ang = jnp.arange(S, dtype=F32)[:, None] * inv_freq[None, :]
    cos, sin = jnp.cos(ang), jnp.sin(ang)
    return jnp.concatenate([cos, cos], axis=-1), jnp.concatenate([-sin, sin], axis=-1)


def _pack_small(D, norm_g, dw_bias, conv_ln_g, conv_ln_b, att_out_g, conv_out_g, q_norm_g, k_norm_g, extra=None):
    qk = jnp.concatenate([q_norm_g.reshape(-1), k_norm_g.reshape(-1)])
    qk = jnp.pad(qk, (0, D - qk.shape[0])).reshape(1, D)
    zero = jnp.zeros((1, D), F32)
    return jnp.concatenate([norm_g, dw_bias, conv_ln_g, conv_ln_b, att_out_g, conv_out_g, qk, zero,
                            zero if extra is None else extra, zero], axis=0)


def _unpack_small(p):
    rows = [p[2 * i:2 * i + 2] for i in range(6)]
    qk = p[12, :4 * HEAD_DIM].reshape(2, DEPTH, HEAD_DIM)
    return rows + [qk[0], qk[1]]


def kernel(x, norm_g, w_in, q_norm_g, k_norm_g, dw_kernel, dw_bias, conv_ln_g, conv_ln_b, w_pw, att_out_g, conv_out_g, w_out, loss_target, m_norm_g, m_w_in, m_q_norm_g, m_k_norm_g, m_dw_kernel, m_dw_bias, m_conv_ln_g, m_conv_ln_b, m_w_pw, m_att_out_g, m_conv_out_g, m_w_out, v_norm_g, v_w_in, v_q_norm_g, v_k_norm_g, v_dw_kernel, v_dw_bias, v_conv_ln_g, v_conv_ln_b, v_w_pw, v_att_out_g, v_conv_out_g, v_w_out):
    xs = x[0]
    D = xs.shape[1]
    bf = lambda t, l: t[l].astype(BF16)
    wint0, dwk_f = all_gather([bf(w_in, 0).T, dw_kernel], [0, 2], "gather_first")
    early = _Exchange(True, [bf(w_pw, 0), bf(w_out, 0)], [0, 0], "gather_layer0", after=wint0)
    later = _Exchange(True, [bf(w_in, 1).T, bf(w_pw, 1), bf(w_out, 1)], [0, 0, 0], "gather_layer1", after=early.token)
    landed = {}

    def weights(l, cur):
        if l == 0:
            return wint0, later.token
        landed[1] = later.wait(cur)
        return landed[1][0], None

    def mixer_weights(l, cur):
        if l == 0:
            return early.wait(cur)
        return landed[1][1:]

    sent = [[] for _ in range(DEPTH)]

    def send_grads(l, grads, axes, tag):
        sent[l].append(_Exchange(False, grads, axes, f"scatter_{tag}_layer{l}", after=None))
        return sent[l][-1].token

    dx, loss_cols, small = local_step(xs, loss_target[0], weights, mixer_weights, dwk_f, norm_g, q_norm_g, k_norm_g, dw_bias,
                                      conv_ln_g, conv_ln_b, att_out_g, conv_out_g, send_grads, N_DEV)

    big = ((w_pw, m_w_pw, v_w_pw), (w_out, m_w_out, v_w_out), (dw_kernel, m_dw_kernel, v_dw_kernel), (w_in, m_w_in, v_w_in))
    results = [None] * len(big)
    after = dx
    for l in reversed(range(DEPTH)):
        parts = [p for ex in sent[l] for p in ex.wait(after)]
        for i, ((w, m, v), p) in enumerate(zip(big, parts)):
            results[i] = adamw(w, m, v, p, l, results[i], f"adamw_{w.shape[1]}_{w.shape[2]}_{l}")
        after = results[0][3]
    r_wpw, r_wout, r_dwk, r_win = results

    stack = lambda k: jnp.concatenate(small[k], axis=0)
    mine = _pack_small(D, stack("norm_g"), stack("dw_bias"), stack("conv_ln_g"), stack("conv_ln_b"), stack("att_out_g"),
                       stack("conv_out_g"), stack("q"), stack("k"), extra=loss_cols)
    (p_small,) = all_gather([mine], [0], "gather_small")
    pk = lambda n, dw, lg, lb, ao, co, q, k: _pack_small(D, n, dw, lg, lb, ao, co, q, k)[None]
    r_small = adamw(pk(norm_g, dw_bias, conv_ln_g, conv_ln_b, att_out_g, conv_out_g, q_norm_g, k_norm_g),
                    pk(m_norm_g, m_dw_bias, m_conv_ln_g, m_conv_ln_b, m_att_out_g, m_conv_out_g, m_q_norm_g, m_k_norm_g),
                    pk(v_norm_g, v_dw_bias, v_conv_ln_g, v_conv_ln_b, v_att_out_g, v_conv_out_g, v_q_norm_g, v_k_norm_g),
                    p_small.reshape(N_DEV, 16, D), 0, None, "adamw_small")
    r_small = [r[0] for r in r_small]
    loss = jnp.sum(r_small[0][14])

    outs = [loss, dx[None]]
    for i in range(4):
        n_, dwb, lg, lb, ao, co, q_, k_ = _unpack_small(r_small[i])
        outs += [n_, r_win[i], q_, k_, r_dwk[i], dwb, lg, lb, r_wpw[i], ao, co, r_wout[i]]
    return tuple(outs)


def local_step(xs, target, weights, mixer_weights, dwk_f, norm_g, q_norm_g, k_norm_g, dw_bias, conv_ln_g, conv_ln_b,
               att_out_g, conv_out_g, send_grads, owners):
    S, D = xs.shape
    cos2, sin2 = _rope_tables(S)
    dwk_f = jnp.pad(dwk_f, ((0, 0), (0, CONV_PAD - CONV_WIDTH), (0, 0)))

    def vec(p, l, zero=None):
        row = p[l].reshape(1, -1)
        return row if zero is None else row + zero

    saved = []
    cur = xs
    for l in range(DEPTH):
        wint, zero = weights(l, cur)
        proj, h = in_proj(cur, vec(norm_g, l, zero), wint, f"in_proj_{l}")
        qs, ks, vs = qk_prep(proj, vec(q_norm_g, l), vec(k_norm_g, l), cos2, sin2, f"qk_prep_{l}")
        os_, lses = [], []
        for i, d in enumerate(DILATIONS):
            o, lse = attn_fwd(qs[i], ks[i], vs[i], d, f"attn_fwd_{l}_d{d}")
            os_.append(o)
            lses.append(_from_branch(lse))
        att, att_y, lse = att_combine(os_, lses, proj, vec(att_out_g, l), f"att_combine_{l}")
        wpw, wout = mixer_weights(l, att_y)
        conv_y, z, conv, y = conv_fwd(proj, dwk_f[l], vec(dw_bias, l), vec(conv_ln_g, l), vec(conv_ln_b, l),
                                      vec(conv_out_g, l), wpw, f"conv_fwd_{l}")
        cat = jnp.concatenate([att_y, conv_y], axis=1)
        nxt = mm_nn(cat, wout, F32, f"out_proj_{l}", add=cur)
        saved.append(dict(x=cur, proj=proj, h=h, qs=qs, ks=ks, vs=vs, att=att, lse=lse, cat=cat, z=z, conv=conv, y=y,
                          wint=wint, wpw=wpw, wout=wout))
        cur = nxt

    dx, dxb, loss_cols = loss_head(cur, target, "loss_head")

    small = {k: [None] * DEPTH for k in ("norm_g", "dw_bias", "conv_ln_g", "conv_ln_b", "att_out_g", "conv_out_g", "q", "k")}
    for l in reversed(range(DEPTH)):
        sv = saved[l]
        proj = sv["proj"]
        dcat = mm_nt(dxb, sv["wout"], F32, f"dcat_{l}")
        g_wout = mm_nn(sv["cat"].T, dxb, BF16, f"dwout_{l}")
        dgate_c, small["conv_out_g"][l], dconv = gate_bwd(dcat, 1, sv["conv"], proj, 6, vec(conv_out_g, l), False,
                                                          f"conv_gate_bwd_{l}")
        dy, small["conv_ln_g"][l], small["conv_ln_b"][l], small["dw_bias"][l] = conv_bwd_ln(
            dconv, sv["wpw"], sv["y"], vec(conv_ln_g, l), vec(conv_ln_b, l), f"conv_bwd_ln_{l}")
        g_wpw = mm_nn(sv["z"].T, dconv, BF16, f"dwpw_{l}")
        da, db, gw = conv_bwd_dw(dy, proj, dwk_f[l], f"conv_bwd_dw_{l}")
        g_dwk = jnp.sum(gw, axis=1)[:CONV_WIDTH]
        zero = send_grads(l, [g_wpw, g_wout, g_dwk], [0, 0, 1], "mixer")
        dgate_a, small["att_out_g"][l], *datts, delta = gate_bwd(dcat, 0, sv["att"], proj, 3, vec(att_out_g, l, zero), True,
                                                                  f"att_gate_bwd_{l}")
        dqs, dks, dvs = [], [], []
        for i, d in enumerate(DILATIONS):
            dq, dk, dv = attn_bwd(sv["qs"][i], sv["ks"][i], sv["vs"][i], datts[i], _to_branch(sv["lse"], d),
                                  _to_branch(delta, d), d, f"attn_bwd_{l}_d{d}")
            dqs.append(dq)
            dks.append(dk)
            dvs.append(dv)
        dq, dk, dv, small["q"][l], small["k"][l] = qk_prep_bwd(dqs, dks, dvs, proj, vec(q_norm_g, l), vec(k_norm_g, l),
                                                                cos2, sin2, f"qk_prep_bwd_{l}")
        dproj = jnp.concatenate([dq, dk, dv, dgate_a, da, db, dgate_c], axis=1)
        g_win = mm_nn(sv["h"].T, dproj, BF16, f"dwin_{l}", owners=owners)
        zero = send_grads(l, [g_win], [None], "w_in")
        dx, dxb, small["norm_g"][l] = in_proj_bwd(dproj, sv["wint"], sv["x"], vec(norm_g, l, zero), dx, f"in_proj_bwd_{l}")

    return dx, loss_cols, small
```

```python
import jax
import jax.numpy as jnp
from jax import lax
from jax.experimental import pallas as pl
from jax.experimental.pallas import tpu as pltpu

F32 = jnp.float32
BF16 = jnp.bfloat16
SDS = jax.ShapeDtypeStruct
MESH = pl.DeviceIdType.MESH

N_DEV = 8
DEPTH = 2
HEAD_DIM = 128
CONV_WIDTH = 31
CONV_PAD = 32
DILATIONS = (1, 4, 16)
Q_BLOCK = 128
ROPE_THETA = 10000.0
EPS = 1e-6
NEG = -1e30
ADAM_LR, ADAM_B1, ADAM_B2, ADAM_EPS, ADAM_WD, ADAM_STEP = 0.001, 0.9, 0.999, 1e-08, 0.01, 10
LANE = 128
ROW_CHUNK = 64
MIB = 1 << 20
NT = (((1,), (1,)), ((), ()))
TN = (((0,), (0,)), ((), ()))


def _call(body, **kw):
    return pl.pallas_call(body, **kw)


def _params(vmem_mib):
    return pltpu.CompilerParams(vmem_limit_bytes=vmem_mib * MIB)


def _tile(dim, pref, mult):
    t = min(pref, dim)
    while dim % t or t % mult:
        t -= mult
    return t


def _sig(v):
    return jax.nn.sigmoid(v)


def _rstd(v):
    return lax.rsqrt(jnp.mean(v * v, axis=-1, keepdims=True) + EPS)


def _row(tm, cb, c=0):
    return pl.BlockSpec((tm, cb), lambda i: (i, c))


def _full(shape):
    return pl.BlockSpec(shape, lambda i: (0,) * len(shape))


def _halo_prev(tm, cb, c=0):
    k = tm // CONV_PAD
    return pl.BlockSpec((CONV_PAD, cb), lambda i: (jnp.maximum(i * k - 1, 0), c))


def _halo_next(tm, cb, nblk, c=0):
    k = tm // CONV_PAD
    return pl.BlockSpec((CONV_PAD, cb), lambda i: (jnp.minimum((i + 1) * k, nblk - 1), c))


def in_proj(x, g, wt, name):
    S, D = x.shape
    N = wt.shape[0]
    tm, tn = _tile(S, 512, 8), _tile(N, 1024, LANE)

    def body(x_ref, g_ref, w_ref, o_ref, h_ref):
        @pl.when(pl.program_id(1) == 0)
        def _():
            xf = x_ref[...]
            h_ref[...] = (xf * _rstd(xf) * g_ref[...]).astype(BF16)

        o_ref[...] = lax.dot_general(h_ref[...], w_ref[...], NT, preferred_element_type=F32)

    return _call(
        body, name=name, grid=(S // tm, N // tn),
        in_specs=[pl.BlockSpec((tm, D), lambda i, j: (i, 0)), pl.BlockSpec((1, D), lambda i, j: (0, 0)),
                  pl.BlockSpec((tn, D), lambda i, j: (j, 0))],
        out_specs=[pl.BlockSpec((tm, tn), lambda i, j: (i, j)), pl.BlockSpec((tm, D), lambda i, j: (i, 0))],
        out_shape=[SDS((S, N), F32), SDS((S, D), BF16)],
        compiler_params=_params(40),
    )(x, g, wt)


def mm_nn(a, b, out_dtype, name, add=None, owners=1):
    M, K = a.shape
    N = b.shape[1]
    tm, tn = _tile(M, 1024 if owners > 1 else 512, 8), _tile(N // owners, 1024, LANE)
    per = N // owners // tn

    def body(*refs):
        a_ref, b_ref = refs[0], refs[1]
        o_ref = refs[-1]
        acc = jnp.dot(a_ref[...], b_ref[...], preferred_element_type=F32)
        if add is not None:
            acc = acc + refs[2][...]
        o_ref[...] = acc.astype(out_dtype)

    in_specs = [pl.BlockSpec((tm, K), lambda i, j: (i, 0)), pl.BlockSpec((K, tn), lambda i, j: (0, j))]
    args = [a, b]
    if add is not None:
        in_specs.append(pl.BlockSpec((tm, tn), lambda i, j: (i, j)))
        args.append(add)
    if owners > 1:
        out_spec = pl.BlockSpec((None, tm, tn), lambda i, j: (j // per, i, j % per))
        out_shape = SDS((owners, M, N // owners), out_dtype)
    else:
        out_spec, out_shape = pl.BlockSpec((tm, tn), lambda i, j: (i, j)), SDS((M, N), out_dtype)
    return _call(
        body, name=name, grid=(M // tm, N // tn), in_specs=in_specs, out_specs=out_spec, out_shape=out_shape,
        compiler_params=_params(48),
    )(*args)


def mm_nt(a, b, out_dtype, name):
    M, K = a.shape
    N = b.shape[0]
    tm, tn = _tile(M, 512, 8), _tile(N, 1024, LANE)

    def body(a_ref, b_ref, o_ref):
        o_ref[...] = lax.dot_general(a_ref[...], b_ref[...], NT, preferred_element_type=F32).astype(out_dtype)

    return _call(
        body, name=name, grid=(M // tm, N // tn),
        in_specs=[pl.BlockSpec((tm, K), lambda i, j: (i, 0)), pl.BlockSpec((tn, K), lambda i, j: (j, 0))],
        out_specs=pl.BlockSpec((tm, tn), lambda i, j: (i, j)), out_shape=SDS((M, N), out_dtype),
        compiler_params=_params(40),
    )(a, b)


def in_proj_bwd(dproj, wt, x, g, dx_out, name):
    S, K = dproj.shape
    D = wt.shape[1]
    tm, tk = _tile(S, 512, 8), _tile(K, 1024, LANE)
    nk = K // tk

    def body(dp_ref, w_ref, x_ref, g_ref, dxo_ref, dx_ref, dxb_ref, gacc_ref):
        i, k = pl.program_id(0), pl.program_id(1)
        part = jnp.dot(dp_ref[...], w_ref[...], preferred_element_type=F32)

        @pl.when(k == 0)
        def _():
            dx_ref[...] = part

        @pl.when(k > 0)
        def _():
            dx_ref[...] += part

        @pl.when((k == 0) & (i == 0))
        def _():
            gacc_ref[...] = jnp.zeros_like(gacc_ref)

        @pl.when(k == nk - 1)
        def _():
            dh = dx_ref[...]
            xf = x_ref[...]
            r = _rstd(xf)
            n = xf * r
            gacc_ref[...] += jnp.sum(dh * n, axis=0, keepdims=True)
            dn = dh * g_ref[...]
            dx = r * (dn - n * jnp.mean(dn * n, axis=-1, keepdims=True)) + dxo_ref[...]
            dx_ref[...] = dx
            dxb_ref[...] = dx.astype(BF16)

    return _call(
        body, name=name, grid=(S // tm, nk),
        in_specs=[pl.BlockSpec((tm, tk), lambda i, k: (i, k)), pl.BlockSpec((tk, D), lambda i, k: (k, 0)),
                  pl.BlockSpec((tm, D), lambda i, k: (i, 0)), pl.BlockSpec((1, D), lambda i, k: (0, 0)),
                  pl.BlockSpec((tm, D), lambda i, k: (i, 0))],
        out_specs=[pl.BlockSpec((tm, D), lambda i, k: (i, 0)), pl.BlockSpec((tm, D), lambda i, k: (i, 0)),
                   pl.BlockSpec((1, D), lambda i, k: (0, 0))],
        out_shape=[SDS((S, D), F32), SDS((S, D), BF16), SDS((1, D), F32)],
        compiler_params=_params(54),
    )(dproj, wt, x, g, dx_out)


def _dil_specs(S, DA, tm, dtype):
    specs = [pl.BlockSpec((tm // d, d * DA), lambda i: (i, 0)) for d in DILATIONS]
    shapes = [SDS((S // d, d * DA), dtype) for d in DILATIONS]
    return specs, shapes


def _head_buf(tm, DA):
    return pltpu.VMEM((DA // HEAD_DIM, tm, HEAD_DIM), F32)


def _emit_dilated(buf_ref, dsts, tm, DA):
    for d, dst in zip(DILATIONS, dsts):
        for h in range(DA // HEAD_DIM):
            for r in range(d):
                rows = slice(None) if d == 1 else pl.ds(r, tm // d, stride=d)
                dst[:, r * DA + h * HEAD_DIM:r * DA + (h + 1) * HEAD_DIM] = buf_ref.at[h][rows, :].astype(BF16)


def _collect_dilated(acc_ref, parts, tm, DA):
    for d, p in zip(DILATIONS, parts):
        for h in range(DA // HEAD_DIM):
            for r in range(d):
                part = p[:, r * DA + h * HEAD_DIM:r * DA + (h + 1) * HEAD_DIM].astype(F32)
                if d == 1:
                    acc_ref[h] = part
                else:
                    rows = pl.ds(r, tm // d, stride=d)
                    acc_ref.at[h][rows, :] = acc_ref.at[h][rows, :] + part


def qk_prep(proj, gq, gk, cos2, sin2, name):
    S = proj.shape[0]
    DA = proj.shape[1] // 7
    H = DA // HEAD_DIM
    tm = _tile(S, 256, 16 * DILATIONS[-1])
    nd = len(DILATIONS)

    def body(q_ref, k_ref, v_ref, gq_ref, gk_ref, c_ref, s_ref, *rest):
        outs, buf_ref = rest[:3 * nd], rest[3 * nd]
        ct, st = c_ref[...], s_ref[...]
        for t, (src, g_ref) in enumerate(((q_ref, gq_ref), (k_ref, gk_ref))):
            gain = g_ref[...]
            for h in range(H):
                sl = slice(h * HEAD_DIM, (h + 1) * HEAD_DIM)
                xh = src[:, sl]
                n = xh * _rstd(xh) * gain
                buf_ref[h] = n * ct + pltpu.roll(n, HEAD_DIM // 2, 1) * st
            _emit_dilated(buf_ref, outs[t * nd:(t + 1) * nd], tm, DA)
        for h in range(H):
            buf_ref[h] = v_ref[:, h * HEAD_DIM:(h + 1) * HEAD_DIM]
        _emit_dilated(buf_ref, outs[2 * nd:], tm, DA)

    specs, shapes = _dil_specs(S, DA, tm, BF16)
    outs = _call(
        body, name=name, grid=(S // tm,),
        in_specs=[_row(tm, DA, 0), _row(tm, DA, 1), _row(tm, DA, 2), _full((1, HEAD_DIM)), _full((1, HEAD_DIM)),
                  _row(tm, HEAD_DIM), _row(tm, HEAD_DIM)],
        out_specs=specs * 3, out_shape=shapes * 3, scratch_shapes=[_head_buf(tm, DA)],
        compiler_params=_params(48),
    )(proj, proj, proj, gq, gk, cos2, sin2)
    return outs[:nd], outs[nd:2 * nd], outs[2 * nd:]


def qk_prep_bwd(dqs, dks, dvs, proj, gq, gk, cos2, sin2, name):
    S = proj.shape[0]
    DA = proj.shape[1] // 7
    H = DA // HEAD_DIM
    tm = _tile(S, 256, 16 * DILATIONS[-1])
    nb = len(dqs)

    def body(*refs):
        dq_refs, dk_refs, dv_refs = refs[:nb], refs[nb:2 * nb], refs[2 * nb:3 * nb]
        q_ref, k_ref, gq_ref, gk_ref, c_ref, s_ref = refs[3 * nb:3 * nb + 6]
        dqo_ref, dko_ref, dvo_ref, gqa_ref, gka_ref, acc_ref = refs[3 * nb + 6:]
        ct, st = c_ref[...], s_ref[...]

        @pl.when(pl.program_id(0) == 0)
        def _():
            gqa_ref[...] = jnp.zeros_like(gqa_ref)
            gka_ref[...] = jnp.zeros_like(gka_ref)

        for parts, x_ref, g_ref, dst, gacc in ((dq_refs, q_ref, gq_ref, dqo_ref, gqa_ref),
                                               (dk_refs, k_ref, gk_ref, dko_ref, gka_ref)):
            gain = g_ref[...]
            gsum = jnp.zeros((1, HEAD_DIM), F32)
            _collect_dilated(acc_ref, parts, tm, DA)
            for h in range(H):
                sl = slice(h * HEAD_DIM, (h + 1) * HEAD_DIM)
                dout = acc_ref[h]
                dn = dout * ct + pltpu.roll(dout * st, HEAD_DIM // 2, 1)
                xh = x_ref[:, sl]
                r = _rstd(xh)
                xn = xh * r
                gsum = gsum + jnp.sum(dn * xn, axis=0, keepdims=True)
                dnn = dn * gain
                dst[:, sl] = (r * (dnn - xn * jnp.mean(dnn * xn, axis=-1, keepdims=True))).astype(BF16)
            gacc[...] += gsum
        _collect_dilated(acc_ref, dv_refs, tm, DA)
        for h in range(H):
            dvo_ref[:, h * HEAD_DIM:(h + 1) * HEAD_DIM] = acc_ref[h].astype(BF16)

    specs, _ = _dil_specs(S, DA, tm, BF16)
    return _call(
        body, name=name, grid=(S // tm,),
        in_specs=specs * 3 + [_row(tm, DA, 0), _row(tm, DA, 1), _full((1, HEAD_DIM)),
                              _full((1, HEAD_DIM)), _row(tm, HEAD_DIM), _row(tm, HEAD_DIM)],
        out_specs=[_row(tm, DA)] * 3 + [_full((1, HEAD_DIM))] * 2,
        out_shape=[SDS((S, DA), BF16)] * 3 + [SDS((1, HEAD_DIM), F32)] * 2,
        scratch_shapes=[_head_buf(tm, DA)],
        compiler_params=_params(48),
    )(*dqs, *dks, *dvs, proj, proj, gq, gk, cos2, sin2)


def _band_masks(n):
    row = lax.broadcasted_iota(jnp.int32, (Q_BLOCK, Q_BLOCK), 0)
    col = lax.broadcasted_iota(jnp.int32, (Q_BLOCK, Q_BLOCK), 1)
    first = jnp.where(n > 0, 0, Q_BLOCK + 1)
    return col <= row, (col - row) >= first


def attn_fwd(qh, kh, vb, d, name):
    L = qh.shape[0]
    DA = qh.shape[1] // d
    H = DA // HEAD_DIM
    nb = L // Q_BLOCK
    scale = HEAD_DIM ** -0.5
    view = (L, d * DA)

    def body(q_ref, kc_ref, kp_ref, vc_ref, vp_ref, o_ref, lse_ref):
        mask_c, mask_p = _band_masks(pl.program_id(1))
        for h in range(H):
            sl = slice(h * HEAD_DIM, (h + 1) * HEAD_DIM)
            q = q_ref[:, sl]
            sc = lax.dot_general(q, kc_ref[:, sl], NT, preferred_element_type=F32) * scale
            sp = lax.dot_general(q, kp_ref[:, sl], NT, preferred_element_type=F32) * scale
            sc = jnp.where(mask_c, sc, NEG)
            sp = jnp.where(mask_p, sp, NEG)
            m = jnp.maximum(jnp.max(sc, axis=-1, keepdims=True), jnp.max(sp, axis=-1, keepdims=True))
            pc = jnp.exp(sc - m)
            pp = jnp.exp(sp - m)
            l = jnp.sum(pc, axis=-1, keepdims=True) + jnp.sum(pp, axis=-1, keepdims=True)
            o = jnp.dot(pc.astype(BF16), vc_ref[:, sl], preferred_element_type=F32)
            o = o + jnp.dot(pp.astype(BF16), vp_ref[:, sl], preferred_element_type=F32)
            o_ref[:, sl] = o / l
            lse_ref[0, :, h:h + 1] = m + jnp.log(l)

    cur = pl.BlockSpec((Q_BLOCK, DA), lambda r, n: (n, r))
    prev = pl.BlockSpec((Q_BLOCK, DA), lambda r, n: (jnp.maximum(n - 1, 0), r))
    o, lse = _call(
        body, name=name, grid=(d, nb), in_specs=[cur, cur, prev, cur, prev],
        out_specs=[cur, pl.BlockSpec((1, Q_BLOCK, H), lambda r, n: (r, n, 0))],
        out_shape=[SDS(view, F32), SDS((d, L, H), F32)],
        compiler_params=_params(32),
    )(qh, kh, kh, vb, vb)
    return o, lse


def attn_bwd(qh, kh, vb, da, lse_d, delta_d, d, name):
    L = qh.shape[0]
    DA = qh.shape[1] // d
    H = DA // HEAD_DIM
    nb = L // Q_BLOCK
    scale = HEAD_DIM ** -0.5
    view = (L, d * DA)

    def body(q_ref, kc_ref, kp_ref, vc_ref, vp_ref, do_ref, lse_ref, dl_ref, dq_ref, dk_ref, dv_ref, dkc_ref, dvc_ref):
        n = pl.program_id(1)
        mask_c, mask_p = _band_masks(n)

        @pl.when(n == 0)
        def _():
            dkc_ref[...] = jnp.zeros_like(dkc_ref)
            dvc_ref[...] = jnp.zeros_like(dvc_ref)

        @pl.when(n < nb)
        def _():
            for h in range(H):
                sl = slice(h * HEAD_DIM, (h + 1) * HEAD_DIM)
                q, kc, kp, vc, vp, do = q_ref[:, sl], kc_ref[:, sl], kp_ref[:, sl], vc_ref[:, sl], vp_ref[:, sl], do_ref[:, sl]
                lse = lse_ref[0, :, h:h + 1]
                dl = dl_ref[0, :, h:h + 1]
                sc = lax.dot_general(q, kc, NT, preferred_element_type=F32) * scale
                sp = lax.dot_general(q, kp, NT, preferred_element_type=F32) * scale
                pc = jnp.exp(jnp.where(mask_c, sc, NEG) - lse)
                pp = jnp.exp(jnp.where(mask_p, sp, NEG) - lse)
                dpc = lax.dot_general(do, vc, NT, preferred_element_type=F32)
                dpp = lax.dot_general(do, vp, NT, preferred_element_type=F32)
                dsc = (pc * (dpc - dl) * scale).astype(BF16)
                dsp = (pp * (dpp - dl) * scale).astype(BF16)
                dq = jnp.dot(dsc, kc, preferred_element_type=F32) + jnp.dot(dsp, kp, preferred_element_type=F32)
                dq_ref[:, sl] = dq.astype(BF16)
                dk_prev = lax.dot_general(dsp, q, TN, preferred_element_type=F32)
                dv_prev = lax.dot_general(pp.astype(BF16), do, TN, preferred_element_type=F32)
                dk_ref[:, sl] = (dkc_ref[:, sl] + dk_prev).astype(BF16)
                dv_ref[:, sl] = (dvc_ref[:, sl] + dv_prev).astype(BF16)
                dkc_ref[:, sl] = lax.dot_general(dsc, q, TN, preferred_element_type=F32)
                dvc_ref[:, sl] = lax.dot_general(pc.astype(BF16), do, TN, preferred_element_type=F32)

        @pl.when(n == nb)
        def _():
            dk_ref[...] = dkc_ref[...].astype(BF16)
            dv_ref[...] = dvc_ref[...].astype(BF16)

    cur = pl.BlockSpec((Q_BLOCK, DA), lambda r, n: (jnp.minimum(n, nb - 1), r))
    prev = pl.BlockSpec((Q_BLOCK, DA), lambda r, n: (jnp.clip(n - 1, 0, nb - 1), r))
    late = pl.BlockSpec((Q_BLOCK, DA), lambda r, n: (jnp.maximum(n - 1, 0), r))
    stat = pl.BlockSpec((1, Q_BLOCK, H), lambda r, n: (r, jnp.minimum(n, nb - 1), 0))
    dq, dk, dv = _call(
        body, name=name, grid=(d, nb + 1), in_specs=[cur, cur, prev, cur, prev, cur, stat, stat],
        out_specs=[cur, late, late], out_shape=[SDS(view, BF16)] * 3,
        scratch_shapes=[pltpu.VMEM((Q_BLOCK, DA), F32), pltpu.VMEM((Q_BLOCK, DA), F32)],
        compiler_params=_params(32),
    )(qh, kh, kh, vb, vb, da, lse_d, delta_d)
    return dq, dk, dv


def _to_branch(stat, d):
    S, H = stat.shape
    return stat.reshape(S // d, d, H).transpose(1, 0, 2)


def _from_branch(stat):
    d, L, H = stat.shape
    return stat.transpose(1, 0, 2).reshape(L * d, H)


def att_combine(os_, lses, proj, gain, name):
    S, DA = os_[0].shape
    H = DA // HEAD_DIM
    tm = _tile(S, 256, 16 * DILATIONS[-1])
    nb = len(os_)

    def body(*refs):
        o_views, l_refs = refs[:nb], refs[nb:2 * nb]
        gate_ref, gain_ref, att_ref, y_ref, lse_ref = refs[2 * nb:2 * nb + 5]
        bufs = refs[2 * nb + 5:]
        for d, view, buf in zip(DILATIONS[1:], o_views[1:], bufs):
            for h in range(H):
                for r in range(d):
                    buf.at[h][pl.ds(r, tm // d, stride=d), :] = view[:, r * DA + h * HEAD_DIM:r * DA + (h + 1) * HEAD_DIM]
        ls = [r[...] for r in l_refs]
        top = ls[0]
        for l in ls[1:]:
            top = jnp.maximum(top, l)
        den = jnp.exp(ls[0] - top)
        for l in ls[1:]:
            den = den + jnp.exp(l - top)
        lse = top + jnp.log(den)
        lse_ref[...] = lse
        ws = [jnp.exp(l - lse) for l in ls]
        for h in range(H):
            sl = slice(h * HEAD_DIM, (h + 1) * HEAD_DIM)
            acc = ws[0][:, h:h + 1] * o_views[0][:, sl]
            for w, buf in zip(ws[1:], bufs):
                acc = acc + w[:, h:h + 1] * buf[h]
            att_ref[:, sl] = acc
        a = att_ref[...]
        g = gate_ref[...]
        y_ref[...] = (a * _rstd(a) * gain_ref[...] * (g * _sig(g))).astype(BF16)

    specs, _ = _dil_specs(S, DA, tm, F32)
    return _call(
        body, name=name, grid=(S // tm,),
        in_specs=specs + [_row(tm, H)] * nb + [_row(tm, DA, 3), _full((1, DA))],
        out_specs=[_row(tm, DA), _row(tm, DA), _row(tm, H)],
        out_shape=[SDS((S, DA), F32), SDS((S, DA), BF16), SDS((S, H), F32)],
        scratch_shapes=[_head_buf(tm, DA)] * (nb - 1),
        compiler_params=_params(48),
    )(*os_, *lses, proj, gain)


def gate_bwd(dcat, cblk, a, proj, gate_blk, gain, dilated, name):
    S, DA = a.shape
    H = DA // HEAD_DIM
    tm = _tile(S, 256, 16 * DILATIONS[-1])
    nd = len(DILATIONS) if dilated else 1

    def body(dy_ref, a_ref, gate_ref, gain_ref, dg_ref, gacc_ref, *rest):
        @pl.when(pl.program_id(0) == 0)
        def _():
            gacc_ref[...] = jnp.zeros_like(gacc_ref)

        dy, av, g, gain_v = dy_ref[...], a_ref[...], gate_ref[...], gain_ref[...]
        r = _rstd(av)
        n = av * r
        sg = _sig(g)
        dg_ref[...] = (dy * (n * gain_v) * (sg * (1.0 + g * (1.0 - sg)))).astype(BF16)
        drn = dy * (g * sg)
        gacc_ref[...] += jnp.sum(drn * n, axis=0, keepdims=True)
        dn = drn * gain_v
        da = r * (dn - n * jnp.mean(dn * n, axis=-1, keepdims=True))
        if dilated:
            da_refs, delta_ref, buf_ref = rest[:nd], rest[nd], rest[nd + 1]
            for h in range(H):
                buf_ref[h] = da[:, h * HEAD_DIM:(h + 1) * HEAD_DIM]
            _emit_dilated(buf_ref, da_refs, tm, DA)
            prod = da * av
            for h in range(H):
                delta_ref[:, h:h + 1] = jnp.sum(prod[:, h * HEAD_DIM:(h + 1) * HEAD_DIM], axis=-1, keepdims=True)
        else:
            rest[0][...] = da.astype(BF16)

    out_specs = [_row(tm, DA), _full((1, DA))]
    out_shape = [SDS((S, DA), BF16), SDS((1, DA), F32)]
    scratch = []
    if dilated:
        specs, shapes = _dil_specs(S, DA, tm, BF16)
        out_specs += specs + [_row(tm, H)]
        out_shape += shapes + [SDS((S, H), F32)]
        scratch = [_head_buf(tm, DA)]
    else:
        out_specs.append(_row(tm, DA))
        out_shape.append(SDS((S, DA), BF16))
    return _call(
        body, name=name, grid=(S // tm,),
        in_specs=[_row(tm, DA, cblk), _row(tm, DA), _row(tm, DA, gate_blk), _full((1, DA))],
        out_specs=out_specs, out_shape=out_shape, scratch_shapes=scratch, compiler_params=_params(48),
    )(dcat, a, proj, gain)


def _by_sublane_phase(offsets):
    groups = [(p, [o for o in offsets if o % 8 == p]) for p in range(8)]
    return [(p, sorted(os_)) for p, os_ in groups if os_]


def _shifted_rows(src_ref, tmp_ref, base, phase, offsets, lanes):
    if phase == 0:
        return lambda o: src_ref[pl.ds(base + o, ROW_CHUNK), lanes]
    span = offsets[-1] - phase + ROW_CHUNK
    tmp_ref[phase, pl.ds(0, span), :] = src_ref[pl.ds(base + phase, span), lanes]
    return lambda o: tmp_ref[phase, pl.ds(o - phase, ROW_CHUNK), :]


def _shift_scratch():
    return pltpu.VMEM((8, CONV_PAD + ROW_CHUNK, LANE), F32)


def _fill_u(i, a_ref, b_ref, ah_ref, bh_ref, uext_ref, tm):
    uext_ref[pl.ds(CONV_PAD, tm), :] = a_ref[...] * _sig(b_ref[...])
    uh = ah_ref[...] * _sig(bh_ref[...])
    uext_ref[pl.ds(0, CONV_PAD), :] = jnp.where(i > 0, uh, 0.0)


def conv_fwd(proj, wk, bias, ln_g, ln_b, out_g, wpw, name):
    S = proj.shape[0]
    DC = proj.shape[1] // 7
    tm = _tile(S, 128, ROW_CHUNK)
    lead = CONV_PAD - (CONV_WIDTH - 1)

    def body(a_ref, b_ref, ah_ref, bh_ref, gate_ref, wk_ref, bias_ref, lg_ref, lb_ref, og_ref, wpw_ref,
             cy_ref, z_ref, conv_ref, y_ref, uext_ref, tmp_ref):
        _fill_u(pl.program_id(0), a_ref, b_ref, ah_ref, bh_ref, uext_ref, tm)

        def cols(cc, carry):
            c0 = pl.multiple_of(cc * LANE, LANE)
            lanes = pl.ds(c0, LANE)
            for rr in range(tm // ROW_CHUNK):
                acc = jnp.broadcast_to(bias_ref[:, lanes], (ROW_CHUNK, LANE))
                for phase, offsets in _by_sublane_phase(range(lead, lead + CONV_WIDTH)):
                    rows = _shifted_rows(uext_ref, tmp_ref, rr * ROW_CHUNK, phase, offsets, lanes)
                    for o in offsets:
                        acc = acc + wk_ref[o - lead:o - lead + 1, lanes] * rows(o)
                y_ref[pl.ds(rr * ROW_CHUNK, ROW_CHUNK), lanes] = acc
            return carry

        lax.fori_loop(0, DC // LANE, cols, 0)
        y = y_ref[...]
        yc = y - jnp.mean(y, axis=-1, keepdims=True)
        ln = yc * _rstd(yc) * lg_ref[...] + lb_ref[...]
        zb = (ln * _sig(ln)).astype(BF16)
        z_ref[...] = zb
        conv = jnp.dot(zb, wpw_ref[...], preferred_element_type=F32)
        conv_ref[...] = conv
        g = gate_ref[...]
        cy_ref[...] = (conv * _rstd(conv) * og_ref[...] * (g * _sig(g))).astype(BF16)

    vec = _full((1, DC))
    return _call(
        body, name=name, grid=(S // tm,),
        in_specs=[_row(tm, DC, 4), _row(tm, DC, 5), _halo_prev(tm, DC, 4), _halo_prev(tm, DC, 5), _row(tm, DC, 6),
                  _full((CONV_PAD, DC)), vec, vec, vec, vec, _full((DC, DC))],
        out_specs=[_row(tm, DC)] * 4,
        out_shape=[SDS((S, DC), BF16), SDS((S, DC), BF16), SDS((S, DC), F32), SDS((S, DC), F32)],
        scratch_shapes=[pltpu.VMEM((CONV_PAD + tm, DC), F32), _shift_scratch()],
        compiler_params=_params(48),
    )(proj, proj, proj, proj, proj, wk, bias, ln_g, ln_b, out_g, wpw)


def conv_bwd_ln(dconv, wpw, y, ln_g, ln_b, name):
    S, DC = y.shape
    tm = _tile(S, 256, 8)

    def body(dc_ref, wpw_ref, y_ref, lg_ref, lb_ref, dy_ref, glg_ref, glb_ref, gb_ref):
        @pl.when(pl.program_id(0) == 0)
        def _():
            glg_ref[...] = jnp.zeros_like(glg_ref)
            glb_ref[...] = jnp.zeros_like(glb_ref)
            gb_ref[...] = jnp.zeros_like(gb_ref)

        dz = lax.dot_general(dc_ref[...], wpw_ref[...], NT, preferred_element_type=F32)
        yv = y_ref[...]
        yc = yv - jnp.mean(yv, axis=-1, keepdims=True)
        rstd = _rstd(yc)
        yhat = yc * rstd
        ln = yhat * lg_ref[...] + lb_ref[...]
        sg = _sig(ln)
        dln = dz * (sg * (1.0 + ln * (1.0 - sg)))
        glb_ref[...] += jnp.sum(dln, axis=0, keepdims=True)
        glg_ref[...] += jnp.sum(dln * yhat, axis=0, keepdims=True)
        dyh = dln * lg_ref[...]
        dy = rstd * (dyh - jnp.mean(dyh, axis=-1, keepdims=True) - yhat * jnp.mean(dyh * yhat, axis=-1, keepdims=True))
        dy_ref[...] = dy
        gb_ref[...] += jnp.sum(dy, axis=0, keepdims=True)

    vec = _full((1, DC))
    return _call(
        body, name=name, grid=(S // tm,),
        in_specs=[_row(tm, DC), _full((DC, DC)), _row(tm, DC), vec, vec],
        out_specs=[_row(tm, DC), vec, vec, vec],
        out_shape=[SDS((S, DC), F32)] + [SDS((1, DC), F32)] * 3,
        compiler_params=_params(48),
    )(dconv, wpw, y, ln_g, ln_b)


def conv_bwd_dw(dy, proj, wk, name):
    S, DC = dy.shape
    tm = _tile(S, 128, ROW_CHUNK)
    nsteps = S // tm
    lead = CONV_PAD - (CONV_WIDTH - 1)
    groups = ROW_CHUNK // 8

    def body(dy_ref, dyn_ref, a_ref, b_ref, ah_ref, bh_ref, wk_ref, da_ref, db_ref, gw_ref, uext_ref, dyext_ref, du_ref,
             tmp_dy_ref, tmp_u_ref):
        i = pl.program_id(0)

        @pl.when(i == 0)
        def _():
            gw_ref[...] = jnp.zeros_like(gw_ref)

        _fill_u(i, a_ref, b_ref, ah_ref, bh_ref, uext_ref, tm)
        dyext_ref[pl.ds(0, tm), :] = dy_ref[...]
        dyext_ref[pl.ds(tm, CONV_PAD), :] = jnp.where(i < nsteps - 1, dyn_ref[...], 0.0)

        def cols(cc, carry):
            c0 = pl.multiple_of(cc * LANE, LANE)
            lanes = pl.ds(c0, LANE)
            for rr in range(tm // ROW_CHUNK):
                base = rr * ROW_CHUNK
                acc = jnp.zeros((ROW_CHUNK, LANE), F32)
                for phase, offsets in _by_sublane_phase(range(CONV_WIDTH)):
                    rows = _shifted_rows(dyext_ref, tmp_dy_ref, base, phase, offsets, lanes)
                    for o in offsets:
                        j = CONV_WIDTH - 1 - o
                        acc = acc + wk_ref[j:j + 1, lanes] * rows(o)
                du_ref[pl.ds(base, ROW_CHUNK), lanes] = acc
                dyc = dyext_ref[pl.ds(base, ROW_CHUNK), lanes]
                for phase, offsets in _by_sublane_phase(range(lead, lead + CONV_WIDTH)):
                    rows = _shifted_rows(uext_ref, tmp_u_ref, base, phase, offsets, lanes)
                    for o in offsets:
                        prod = dyc * rows(o)
                        part = prod[0:8]
                        for k in range(1, groups):
                            part = part + prod[8 * k:8 * k + 8]
                        gw_ref[o - lead, :, lanes] += part
            return carry

        lax.fori_loop(0, DC // LANE, cols, 0)
        du = du_ref[...]
        sb = _sig(b_ref[...])
        da_ref[...] = (du * sb).astype(BF16)
        db_ref[...] = (du * a_ref[...] * sb * (1.0 - sb)).astype(BF16)

    return _call(
        body, name=name, grid=(nsteps,),
        in_specs=[_row(tm, DC), _halo_next(tm, DC, S // CONV_PAD), _row(tm, DC, 4), _row(tm, DC, 5),
                  _halo_prev(tm, DC, 4), _halo_prev(tm, DC, 5), _full((CONV_PAD, DC))],
        out_specs=[_row(tm, DC), _row(tm, DC), _full((CONV_PAD, 8, DC))],
        out_shape=[SDS((S, DC), BF16), SDS((S, DC), BF16), SDS((CONV_PAD, 8, DC), F32)],
        scratch_shapes=[pltpu.VMEM((CONV_PAD + tm, DC), F32), pltpu.VMEM((tm + CONV_PAD, DC), F32), pltpu.VMEM((tm, DC), F32),
                        _shift_scratch(), _shift_scratch()],
        compiler_params=_params(40),
    )(dy, dy, proj, proj, proj, proj, wk)


def loss_head(xo, target, name):
    S, D = xo.shape
    tm = _tile(S, 256, 8)

    def body(x_ref, t_ref, dy_ref, dyb_ref, acc_ref):
        @pl.when(pl.program_id(0) == 0)
        def _():
            acc_ref[...] = jnp.zeros_like(acc_ref)

        err = x_ref[...] - t_ref[...]
        dy = err * (1.0 / D)
        dy_ref[...] = dy
        dyb_ref[...] = dy.astype(BF16)
        acc_ref[...] += jnp.sum(err * dy, axis=0, keepdims=True) * 0.5

    return _call(
        body, name=name, grid=(S // tm,), in_specs=[_row(tm, D), _row(tm, D)],
        out_specs=[_row(tm, D), _row(tm, D), _full((1, D))],
        out_shape=[SDS((S, D), F32), SDS((S, D), BF16), SDS((1, D), F32)],
        compiler_params=_params(32),
    )(xo, target)


def _coords():
    x, y, c = lax.axis_index("x"), lax.axis_index("y"), lax.axis_index("c")
    return x, y, c


def _lin(p):
    return 4 * p[0] + 2 * p[1] + p[2]


def _slot(ref, axis, idx, size):
    index = [slice(None)] * len(ref.shape)
    index[axis] = pl.ds(idx * size, size)
    return ref.at[tuple(index)]


def all_gather(blocks, axes, name):
    na = len(blocks)
    sizes = [b.shape[ax] for b, ax in zip(blocks, axes)]
    fulls = [SDS(b.shape[:ax] + (N_DEV * b.shape[ax],) + b.shape[ax + 1:], b.dtype) for b, ax in zip(blocks, axes)]

    def body(*refs):
        in_refs, out_refs = refs[:na], refs[na:2 * na]
        send_sems, recv_sems, local_sems = refs[2 * na:]
        x, y, c = _coords()
        me, sibling = (x, y, c), (x, y, 1 - c)
        chips = [(1 - x, y), (x, 1 - y), (1 - x, 1 - y)]

        def place(a, p):
            return _slot(out_refs[a], axes[a], _lin(p), sizes[a])

        def copy(a, k, block, to, src=None):
            return pltpu.make_async_remote_copy(
                src_ref=place(a, block) if src is None else src, dst_ref=place(a, block),
                send_sem=send_sems.at[a, k], recv_sem=recv_sems.at[a, k], device_id=to, device_id_type=MESH)

        mine = [pltpu.make_async_copy(in_refs[a], place(a, me), local_sems.at[a]) for a in range(na)]
        for cp in mine:
            cp.start()
        first = []
        for a in range(na):
            first.append(copy(a, 0, me, sibling, src=in_refs[a]))
            first += [copy(a, 1 + j, me, (*chip, c), src=in_refs[a]) for j, chip in enumerate(chips)]
        for cp in first:
            cp.start()
        passed = []
        for j, chip in enumerate(chips):
            for a in range(na):
                copy(a, 1 + j, (*chip, c), me).wait_recv()
                cp = copy(a, 4 + j, (*chip, c), sibling)
                cp.start()
                passed.append(cp)
        for a in range(na):
            copy(a, 0, sibling, me).wait_recv()
            for j, chip in enumerate(chips):
                copy(a, 4 + j, (*chip, 1 - c), me).wait_recv()
        for cp in first + passed:
            cp.wait_send()
        for cp in mine:
            cp.wait()

    hbm = pl.BlockSpec(memory_space=pltpu.HBM)
    return _call(
        body, name=name, in_specs=[hbm] * na, out_specs=[hbm] * na, out_shape=fulls,
        scratch_shapes=[pltpu.SemaphoreType.DMA((na, 7)), pltpu.SemaphoreType.DMA((na, 7)), pltpu.SemaphoreType.DMA((na,))],
    )(*blocks)


class _Exchange:
    def __init__(self, gather, srcs, axes, name, after):
        self.gather, self.axes, self.name, self.na = gather, axes, name, len(srcs)
        if gather:
            self.sizes = [s.shape[ax] for s, ax in zip(srcs, axes)]
            lands = [s.shape[:ax] + (N_DEV * s.shape[ax],) + s.shape[ax + 1:] for s, ax in zip(srcs, axes)]
        else:
            self.sizes = [None if ax is None else s.shape[ax] // N_DEV for s, ax in zip(srcs, axes)]
            lands = [s.shape if ax is None else (N_DEV,) + s.shape[:ax] + (sz,) + s.shape[ax + 1:]
                     for s, ax, sz in zip(srcs, axes, self.sizes)]
        self.kinds = [pltpu.HBM(s.shape, s.dtype) for s in srcs] + [pltpu.HBM(l, s.dtype) for l, s in zip(lands, srcs)]
        lands = [lax.empty(l, s.dtype) for l, s in zip(lands, srcs)]
        after = jnp.zeros((8, LANE), F32) if after is None else after
        self._start([pltpu.with_memory_space_constraint(t, pltpu.HBM) for t in list(srcs) + lands], after)

    def _src(self, a, ref, owner):
        if self.gather:
            return ref
        return ref.at[_lin(owner)] if self.axes[a] is None else _slot(ref, self.axes[a], _lin(owner), self.sizes[a])

    def _dst(self, a, land, sender):
        return _slot(land, self.axes[a], _lin(sender), self.sizes[a]) if self.gather else land.at[_lin(sender)]

    def _copies(self, refs, send_sems, recv_sems):
        na = self.na
        me = _coords()
        flips = [(k >> 2 & 1, k >> 1 & 1, k & 1) for k in range(1, N_DEV)]
        peers = [tuple(1 - v if f else v for v, f in zip(me, flip)) for flip in flips]
        sends, arrivals = [], []
        for a in range(na):
            for k, peer in enumerate(peers):
                pair = dict(send_sem=send_sems.at[7 * a + k], recv_sem=recv_sems.at[7 * a + k], device_id=peer, device_id_type=MESH)
                sends.append(pltpu.make_async_remote_copy(
                    src_ref=self._src(a, refs[a], peer), dst_ref=self._dst(a, refs[na + a], me), **pair))
                arrivals.append(pltpu.make_async_remote_copy(
                    src_ref=self._src(a, refs[a], me), dst_ref=self._dst(a, refs[na + a], peer), **pair))
        return sends, arrivals

    def _place_own(self, operands):
        na = self.na
        me = jnp.reshape(_lin(_coords()), (1,)).astype(jnp.int32)
        lands = []
        for a in range(na):
            src, land, ax = operands[a], operands[na + a], self.axes[a]
            if ax == 1:
                R, C = src.shape[0], self.sizes[a]
                steps, tile = 1, (R, C)
                in_map = lambda i, me_ref: (0, me_ref[0])
            elif ax is None:
                R, C = src.shape[1:]
                tr = _tile(R, 512, 16)
                steps, tile = R // tr, (None, tr, C)
                in_map = lambda i, me_ref: (me_ref[0], i, 0)
            else:
                R, C = (src.shape[0] if self.gather else self.sizes[a]), src.shape[1]
                tr = _tile(R, 512, 16)
                steps, tile = R // tr, (tr, C)
                in_map = (lambda i, me_ref: (i, 0)) if self.gather else (lambda i, me_ref, n=R // tr: (me_ref[0] * n + i, 0))
            if self.gather:
                out_spec = pl.BlockSpec(tile, lambda i, me_ref, n=steps: (me_ref[0] * n + i, 0))
            else:
                out_spec = pl.BlockSpec((None,) + tuple(t for t in tile if t is not None), lambda i, me_ref: (me_ref[0], i, 0))

            def body(me_ref, src_ref, land_ref, out_ref):
                out_ref[...] = src_ref[...]

            lands.append(_call(
                body, name=f"{self.name}_own{a}",
                grid_spec=pltpu.PrefetchScalarGridSpec(
                    num_scalar_prefetch=1, grid=(steps,),
                    in_specs=[pl.BlockSpec(tile, in_map), pl.BlockSpec(memory_space=pl.ANY)], out_specs=out_spec),
                out_shape=SDS(land.shape, land.dtype), input_output_aliases={2: 0}, compiler_params=_params(32),
            )(me, src, land))
        return operands[:na] + lands

    def _start(self, operands, after):
        na = self.na
        operands = self._place_own(operands)

        def body(*refs):
            ins = refs[:2 * na]
            send_sems, recv_sems, token_ref = refs[2 * na + 1], refs[2 * na + 2], refs[4 * na + 3]
            for cp in self._copies(ins, send_sems, recv_sems)[0]:
                cp.start()
            token_ref[...] = jnp.zeros_like(token_ref)

        hbm = pl.BlockSpec(memory_space=pltpu.HBM)
        sem = pl.BlockSpec(memory_space=pltpu.SEMAPHORE)
        outs = _call(
            body, name=self.name + "_start",
            in_specs=[hbm] * (2 * na) + [pl.BlockSpec(memory_space=pl.ANY)],
            out_specs=[sem, sem] + [hbm] * (2 * na) + [pl.BlockSpec(memory_space=pltpu.VMEM)],
            out_shape=[pltpu.SemaphoreType.DMA((7 * na,)), pltpu.SemaphoreType.DMA((7 * na,))] + self.kinds + [SDS((8, LANE), F32)],
            input_output_aliases={i: 2 + i for i in range(2 * na)},
            compiler_params=pltpu.CompilerParams(has_side_effects=pltpu.SideEffectType.DATAFLOW_SIDE_EFFECTING),
        )(*operands, after)
        self.sems, self.thru, self.token = outs[:2], outs[2:2 + 2 * na], outs[2 + 2 * na][0:1, 0:1]

    def wait(self, after):
        na = self.na

        def body(*refs):
            ins, send_sems, recv_sems = refs[:2 * na], refs[2 * na], refs[2 * na + 1]
            sends, arrivals = self._copies(ins, send_sems, recv_sems)
            for cp in sends:
                cp.wait_send()
            for cp in arrivals:
                cp.wait_recv()

        hbm = pl.BlockSpec(memory_space=pltpu.HBM)
        sem = pl.BlockSpec(memory_space=pltpu.SEMAPHORE)
        outs = _call(
            body, name=self.name + "_wait",
            in_specs=[hbm] * (2 * na) + [sem, sem, pl.BlockSpec(memory_space=pl.ANY)],
            out_specs=[hbm] * (2 * na), out_shape=self.kinds,
            input_output_aliases={i: i for i in range(2 * na)},
            compiler_params=pltpu.CompilerParams(has_side_effects=pltpu.SideEffectType.DATAFLOW_SIDE_EFFECTING),
        )(*self.thru, *self.sems, after)
        return outs[na:]


def adamw(w, m, v, parts, layer, prev, name):
    nl, R, C = w.shape
    tr = _tile(R, 128, 8) if R % 8 == 0 else R

    def body(w_ref, m_ref, v_ref, p_ref, *rest):
        g_ref, d_ref, mo_ref, vo_ref = rest[-4:]
        g = p_ref[0].astype(F32)
        for s in range(1, N_DEV):
            g = g + p_ref[s].astype(F32)
        mn = ADAM_B1 * m_ref[0] + (1.0 - ADAM_B1) * g
        vn = ADAM_B2 * v_ref[0] + (1.0 - ADAM_B2) * (g * g)
        m_hat = mn / (1.0 - ADAM_B1 ** ADAM_STEP)
        v_hat = vn / (1.0 - ADAM_B2 ** ADAM_STEP)
        g_ref[0] = g
        d_ref[0] = -ADAM_LR * (m_hat / (jnp.sqrt(v_hat) + ADAM_EPS) + ADAM_WD * w_ref[0])
        mo_ref[0] = mn
        vo_ref[0] = vn

    row = pl.BlockSpec((1, tr, C), lambda i: (layer, i, 0))
    carried = [] if prev is None else list(prev)
    return _call(
        body, name=name, grid=(R // tr,),
        in_specs=[row, row, row, pl.BlockSpec((N_DEV, tr, C), lambda i: (0, i, 0))] + [pl.BlockSpec(memory_space=pl.ANY)] * len(carried),
        out_specs=[row] * 4, out_shape=[SDS((nl, R, C), F32)] * 4,
        input_output_aliases={4 + k: k for k in range(len(carried))},
        compiler_params=_params(48),
    )(w, m, v, parts, *carried)


def _rope_tables(S):
    inv_freq = 1.0 / (ROPE_THETA ** (jnp.arange(0, HEAD_DIM, 2, dtype=F32) / HEAD_DIM))
    ang = jnp.arange(S, dtype=F32)[:, None] * inv_freq[None, :]
    cos, sin = jnp.cos(ang), jnp.sin(ang)
    return jnp.concatenate([cos, cos], axis=-1), jnp.concatenate([-sin, sin], axis=-1)


def _pack_small(D, norm_g, dw_bias, conv_ln_g, conv_ln_b, att_out_g, conv_out_g, q_norm_g, k_norm_g, extra=None):
    qk = jnp.concatenate([q_norm_g.reshape(-1), k_norm_g.reshape(-1)])
    qk = jnp.pad(qk, (0, D - qk.shape[0])).reshape(1, D)
    zero = jnp.zeros((1, D), F32)
    return jnp.concatenate([norm_g, dw_bias, conv_ln_g, conv_ln_b, att_out_g, conv_out_g, qk, zero,
                            zero if extra is None else extra, zero], axis=0)


def _unpack_small(p):
    rows = [p[2 * i:2 * i + 2] for i in range(6)]
    qk = p[12, :4 * HEAD_DIM].reshape(2, DEPTH, HEAD_DIM)
    return rows + [qk[0], qk[1]]


def kernel(x, norm_g, w_in, q_norm_g, k_norm_g, dw_kernel, dw_bias, conv_ln_g, conv_ln_b, w_pw, att_out_g, conv_out_g, w_out, loss_target, m_norm_g, m_w_in, m_q_norm_g, m_k_norm_g, m_dw_kernel, m_dw_bias, m_conv_ln_g, m_conv_ln_b, m_w_pw, m_att_out_g, m_conv_out_g, m_w_out, v_norm_g, v_w_in, v_q_norm_g, v_k_norm_g, v_dw_kernel, v_dw_bias, v_conv_ln_g, v_conv_ln_b, v_w_pw, v_att_out_g, v_conv_out_g, v_w_out):
    xs = x[0]
    D = xs.shape[1]
    bf = lambda t, l: t[l].astype(BF16)
    wint0, dwk_f = all_gather([bf(w_in, 0).T, dw_kernel], [0, 2], "gather_first")
    early = _Exchange(True, [bf(w_pw, 0), bf(w_out, 0)], [0, 0], "gather_layer0", after=wint0)
    later = _Exchange(True, [bf(w_in, 1).T, bf(w_pw, 1), bf(w_out, 1)], [0, 0, 0], "gather_layer1", after=early.token)
    landed = {}

    def weights(l, cur):
        if l == 0:
            return wint0, later.token
        landed[1] = later.wait(cur)
        return landed[1][0], None

    def mixer_weights(l, cur):
        if l == 0:
            return early.wait(cur)
        return landed[1][1:]

    sent = [[] for _ in range(DEPTH)]

    def send_grads(l, grads, axes, tag):
        sent[l].append(_Exchange(False, grads, axes, f"scatter_{tag}_layer{l}", after=None))
        return sent[l][-1].token

    dx, loss_cols, small = local_step(xs, loss_target[0], weights, mixer_weights, dwk_f, norm_g, q_norm_g, k_norm_g, dw_bias,
                                      conv_ln_g, conv_ln_b, att_out_g, conv_out_g, send_grads, N_DEV)

    big = ((w_pw, m_w_pw, v_w_pw), (w_out, m_w_out, v_w_out), (dw_kernel, m_dw_kernel, v_dw_kernel), (w_in, m_w_in, v_w_in))
    results = [None] * len(big)
    after = dx
    for l in reversed(range(DEPTH)):
        parts = [p for ex in sent[l] for p in ex.wait(after)]
        for i, ((w, m, v), p) in enumerate(zip(big, parts)):
            results[i] = adamw(w, m, v, p, l, results[i], f"adamw_{w.shape[1]}_{w.shape[2]}_{l}")
        after = results[0][3]
    r_wpw, r_wout, r_dwk, r_win = results

    stack = lambda k: jnp.concatenate(small[k], axis=0)
    mine = _pack_small(D, stack("norm_g"), stack("dw_bias"), stack("conv_ln_g"), stack("conv_ln_b"), stack("att_out_g"),
                       stack("conv_out_g"), stack("q"), stack("k"), extra=loss_cols)
    (p_small,) = all_gather([mine], [0], "gather_small")
    pk = lambda n, dw, lg, lb, ao, co, q, k: _pack_small(D, n, dw, lg, lb, ao, co, q, k)[None]
    r_small = adamw(pk(norm_g, dw_bias, conv_ln_g, conv_ln_b, att_out_g, conv_out_g, q_norm_g, k_norm_g),
                    pk(m_norm_g, m_dw_bias, m_conv_ln_g, m_conv_ln_b, m_att_out_g, m_conv_out_g, m_q_norm_g, m_k_norm_g),
                    pk(v_norm_g, v_dw_bias, v_conv_ln_g, v_conv_ln_b, v_att_out_g, v_conv_out_g, v_q_norm_g, v_k_norm_g),
                    p_small.reshape(N_DEV, 16, D), 0, None, "adamw_small")
    r_small = [r[0] for r in r_small]
    loss = jnp.sum(r_small[0][14])

    outs = [loss, dx[None]]
    for i in range(4):
        n_, dwb, lg, lb, ao, co, q_, k_ = _unpack_small(r_small[i])
        outs += [n_, r_win[i], q_, k_, r_dwk[i], dwb, lg, lb, r_wpw[i], ao, co, r_wout[i]]
    return tuple(outs)


def local_step(xs, target, weights, mixer_weights, dwk_f, norm_g, q_norm_g, k_norm_g, dw_bias, conv_ln_g, conv_ln_b,
               att_out_g, conv_out_g, send_grads, owners):
    S, D = xs.shape
    cos2, sin2 = _rope_tables(S)
    dwk_f = jnp.pad(dwk_f, ((0, 0), (0, CONV_PAD - CONV_WIDTH), (0, 0)))

    def vec(p, l, zero=None):
        row = p[l].reshape(1, -1)
        return row if zero is None else row + zero

    saved = []
    cur = xs
    for l in range(DEPTH):
        wint, zero = weights(l, cur)
        proj, h = in_proj(cur, vec(norm_g, l, zero), wint, f"in_proj_{l}")
        qs, ks, vs = qk_prep(proj, vec(q_norm_g, l), vec(k_norm_g, l), cos2, sin2, f"qk_prep_{l}")
        os_, lses = [], []
        for i, d in enumerate(DILATIONS):
            o, lse = attn_fwd(qs[i], ks[i], vs[i], d, f"attn_fwd_{l}_d{d}")
            os_.append(o)
            lses.append(_from_branch(lse))
        att, att_y, lse = att_combine(os_, lses, proj, vec(att_out_g, l), f"att_combine_{l}")
        wpw, wout = mixer_weights(l, att_y)
        conv_y, z, conv, y = conv_fwd(proj, dwk_f[l], vec(dw_bias, l), vec(conv_ln_g, l), vec(conv_ln_b, l),
                                      vec(conv_out_g, l), wpw, f"conv_fwd_{l}")
        cat = jnp.concatenate([att_y, conv_y], axis=1)
        nxt = mm_nn(cat, wout, F32, f"out_proj_{l}", add=cur)
        saved.append(dict(x=cur, proj=proj, h=h, qs=qs, ks=ks, vs=vs, att=att, lse=lse, cat=cat, z=z, conv=conv, y=y,
                          wint=wint, wpw=wpw, wout=wout))
        cur = nxt

    dx, dxb, loss_cols = loss_head(cur, target, "loss_head")

    small = {k: [None] * DEPTH for k in ("norm_g", "dw_bias", "conv_ln_g", "conv_ln_b", "att_out_g", "conv_out_g", "q", "k")}
    for l in reversed(range(DEPTH)):
        sv = saved[l]
        proj = sv["proj"]
        dcat = mm_nt(dxb, sv["wout"], F32, f"dcat_{l}")
        g_wout = mm_nn(sv["cat"].T, dxb, BF16, f"dwout_{l}")
        dgate_c, small["conv_out_g"][l], dconv = gate_bwd(dcat, 1, sv["conv"], proj, 6, vec(conv_out_g, l), False,
                                                          f"conv_gate_bwd_{l}")
        dy, small["conv_ln_g"][l], small["conv_ln_b"][l], small["dw_bias"][l] = conv_bwd_ln(
            dconv, sv["wpw"], sv["y"], vec(conv_ln_g, l), vec(conv_ln_b, l), f"conv_bwd_ln_{l}")
        g_wpw = mm_nn(sv["z"].T, dconv, BF16, f"dwpw_{l}")
        da, db, gw = conv_bwd_dw(dy, proj, dwk_f[l], f"conv_bwd_dw_{l}")
        g_dwk = jnp.sum(gw, axis=1)[:CONV_WIDTH]
        zero = send_grads(l, [g_wpw, g_wout, g_dwk], [0, 0, 1], "mixer")
        dgate_a, small["att_out_g"][l], *datts, delta = gate_bwd(dcat, 0, sv["att"], proj, 3, vec(att_out_g, l, zero), True,
                                                                  f"att_gate_bwd_{l}")
        dqs, dks, dvs = [], [], []
        for i, d in enumerate(DILATIONS):
            dq, dk, dv = attn_bwd(sv["qs"][i], sv["ks"][i], sv["vs"][i], datts[i], _to_branch(sv["lse"], d),
                                  _to_branch(delta, d), d, f"attn_bwd_{l}_d{d}")
            dqs.append(dq)
            dks.append(dk)
            dvs.append(dv)
        dq, dk, dv, small["q"][l], small["k"][l] = qk_prep_bwd(dqs, dks, dvs, proj, vec(q_norm_g, l), vec(k_norm_g, l),
                                                                cos2, sin2, f"qk_prep_bwd_{l}")
        dproj = jnp.concatenate([dq, dk, dv, dgate_a, da, db, dgate_c], axis=1)
        g_win = mm_nn(sv["h"].T, dproj, BF16, f"dwin_{l}", owners=owners)
        zero = send_grads(l, [g_win], [None], "w_in")
        dx, dxb, small["norm_g"][l] = in_proj_bwd(dproj, sv["wint"], sv["x"], vec(norm_g, l, zero), dx, f"in_proj_bwd_{l}")

    return dx, loss_cols, small
```

```python
import jax
import jax.numpy as jnp
from jax import lax
from jax.experimental import pallas as pl
from jax.experimental.pallas import tpu as pltpu

F32 = jnp.float32
BF16 = jnp.bfloat16
SDS = jax.ShapeDtypeStruct
MESH = pl.DeviceIdType.MESH

N_DEV = 8
DEPTH = 2
HEAD_DIM = 128
CONV_WIDTH = 31
CONV_PAD = 32
DILATIONS = (1, 4, 16)
Q_BLOCK = 128
ROPE_THETA = 10000.0
EPS = 1e-6
NEG = -1e30
ADAM_LR, ADAM_B1, ADAM_B2, ADAM_EPS, ADAM_WD, ADAM_STEP = 0.001, 0.9, 0.999, 1e-08, 0.01, 10
LANE = 128
ROW_CHUNK = 64
MIB = 1 << 20
NT = (((1,), (1,)), ((), ()))
TN = (((0,), (0,)), ((), ()))


def _call(body, **kw):
    return pl.pallas_call(body, **kw)


def _params(vmem_mib):
    return pltpu.CompilerParams(vmem_limit_bytes=vmem_mib * MIB)


def _tile(dim, pref, mult):
    t = min(pref, dim)
    while dim % t or t % mult:
        t -= mult
    return t


def _sig(v):
    return jax.nn.sigmoid(v)


def _rstd(v):
    return lax.rsqrt(jnp.mean(v * v, axis=-1, keepdims=True) + EPS)


def _row(tm, cb, c=0):
    return pl.BlockSpec((tm, cb), lambda i: (i, c))


def _full(shape):
    return pl.BlockSpec(shape, lambda i: (0,) * len(shape))


def _halo_prev(tm, cb, c=0):
    k = tm // CONV_PAD
    return pl.BlockSpec((CONV_PAD, cb), lambda i: (jnp.maximum(i * k - 1, 0), c))


def _halo_next(tm, cb, nblk, c=0):
    k = tm // CONV_PAD
    return pl.BlockSpec((CONV_PAD, cb), lambda i: (jnp.minimum((i + 1) * k, nblk - 1), c))


def in_proj(x, g, wt, name):
    S, D = x.shape
    N = wt.shape[0]
    tm, tn = _tile(S, 512, 8), _tile(N, 1024, LANE)

    def body(x_ref, g_ref, w_ref, o_ref, h_ref):
        @pl.when(pl.program_id(1) == 0)
        def _():
            xf = x_ref[...]
            h_ref[...] = (xf * _rstd(xf) * g_ref[...]).astype(BF16)

        o_ref[...] = lax.dot_general(h_ref[...], w_ref[...], NT, preferred_element_type=F32)

    return _call(
        body, name=name, grid=(S // tm, N // tn),
        in_specs=[pl.BlockSpec((tm, D), lambda i, j: (i, 0)), pl.BlockSpec((1, D), lambda i, j: (0, 0)),
                  pl.BlockSpec((tn, D), lambda i, j: (j, 0))],
        out_specs=[pl.BlockSpec((tm, tn), lambda i, j: (i, j)), pl.BlockSpec((tm, D), lambda i, j: (i, 0))],
        out_shape=[SDS((S, N), F32), SDS((S, D), BF16)],
        compiler_params=_params(40),
    )(x, g, wt)


def mm_nn(a, b, out_dtype, name, add=None, owners=1):
    M, K = a.shape
    N = b.shape[1]
    tm, tn = _tile(M, 1024 if owners > 1 else 512, 8), _tile(N // owners, 1024, LANE)
    per = N // owners // tn

    def body(*refs):
        a_ref, b_ref = refs[0], refs[1]
        o_ref = refs[-1]
        acc = jnp.dot(a_ref[...], b_ref[...], preferred_element_type=F32)
        if add is not None:
            acc = acc + refs[2][...]
        o_ref[...] = acc.astype(out_dtype)

    in_specs = [pl.BlockSpec((tm, K), lambda i, j: (i, 0)), pl.BlockSpec((K, tn), lambda i, j: (0, j))]
    args = [a, b]
    if add is not None:
        in_specs.append(pl.BlockSpec((tm, tn), lambda i, j: (i, j)))
        args.append(add)
    if owners > 1:
        out_spec = pl.BlockSpec((None, tm, tn), lambda i, j: (j // per, i, j % per))
        out_shape = SDS((owners, M, N // owners), out_dtype)
    else:
        out_spec, out_shape = pl.BlockSpec((tm, tn), lambda i, j: (i, j)), SDS((M, N), out_dtype)
    return _call(
        body, name=name, grid=(M // tm, N // tn), in_specs=in_specs, out_specs=out_spec, out_shape=out_shape,
        compiler_params=_params(48),
    )(*args)


def mm_nt(a, b, out_dtype, name):
    M, K = a.shape
    N = b.shape[0]
    tm, tn = _tile(M, 512, 8), _tile(N, 1024, LANE)

    def body(a_ref, b_ref, o_ref):
        o_ref[...] = lax.dot_general(a_ref[...], b_ref[...], NT, preferred_element_type=F32).astype(out_dtype)

    return _call(
        body, name=name, grid=(M // tm, N // tn),
        in_specs=[pl.BlockSpec((tm, K), lambda i, j: (i, 0)), pl.BlockSpec((tn, K), lambda i, j: (j, 0))],
        out_specs=pl.BlockSpec((tm, tn), lambda i, j: (i, j)), out_shape=SDS((M, N), out_dtype),
        compiler_params=_params(40),
    )(a, b)


def in_proj_bwd(dproj, wt, x, g, dx_out, name):
    S, K = dproj.shape
    D = wt.shape[1]
    tm, tk = _tile(S, 512, 8), _tile(K, 1024, LANE)
    nk = K // tk

    def body(dp_ref, w_ref, x_ref, g_ref, dxo_ref, dx_ref, dxb_ref, gacc_ref):
        i, k = pl.program_id(0), pl.program_id(1)
        part = jnp.dot(dp_ref[...], w_ref[...], preferred_element_type=F32)

        @pl.when(k == 0)
        def _():
            dx_ref[...] = part

        @pl.when(k > 0)
        def _():
            dx_ref[...] += part

        @pl.when((k == 0) & (i == 0))
        def _():
            gacc_ref[...] = jnp.zeros_like(gacc_ref)

        @pl.when(k == nk - 1)
        def _():
            dh = dx_ref[...]
            xf = x_ref[...]
            r = _rstd(xf)
            n = xf * r
            gacc_ref[...] += jnp.sum(dh * n, axis=0, keepdims=True)
            dn = dh * g_ref[...]
            dx = r * (dn - n * jnp.mean(dn * n, axis=-1, keepdims=True)) + dxo_ref[...]
            dx_ref[...] = dx
            dxb_ref[...] = dx.astype(BF16)

    return _call(
        body, name=name, grid=(S // tm, nk),
        in_specs=[pl.BlockSpec((tm, tk), lambda i, k: (i, k)), pl.BlockSpec((tk, D), lambda i, k: (k, 0)),
                  pl.BlockSpec((tm, D), lambda i, k: (i, 0)), pl.BlockSpec((1, D), lambda i, k: (0, 0)),
                  pl.BlockSpec((tm, D), lambda i, k: (i, 0))],
        out_specs=[pl.BlockSpec((tm, D), lambda i, k: (i, 0)), pl.BlockSpec((tm, D), lambda i, k: (i, 0)),
                   pl.BlockSpec((1, D), lambda i, k: (0, 0))],
        out_shape=[SDS((S, D), F32), SDS((S, D), BF16), SDS((1, D), F32)],
        compiler_params=_params(54),
    )(dproj, wt, x, g, dx_out)


def _dil_specs(S, DA, tm, dtype):
    specs = [pl.BlockSpec((tm // d, d * DA), lambda i: (i, 0)) for d in DILATIONS]
    shapes = [SDS((S // d, d * DA), dtype) for d in DILATIONS]
    return specs, shapes


def _head_buf(tm, DA):
    return pltpu.VMEM((DA // HEAD_DIM, tm, HEAD_DIM), F32)


def _emit_dilated(buf_ref, dsts, tm, DA):
    for d, dst in zip(DILATIONS, dsts):
        for h in range(DA // HEAD_DIM):
            for r in range(d):
                rows = slice(None) if d == 1 else pl.ds(r, tm // d, stride=d)
                dst[:, r * DA + h * HEAD_DIM:r * DA + (h + 1) * HEAD_DIM] = buf_ref.at[h][rows, :].astype(BF16)


def _collect_dilated(acc_ref, parts, tm, DA):
    for d, p in zip(DILATIONS, parts):
        for h in range(DA // HEAD_DIM):
            for r in range(d):
                part = p[:, r * DA + h * HEAD_DIM:r * DA + (h + 1) * HEAD_DIM].astype(F32)
                if d == 1:
                    acc_ref[h] = part
                else:
                    rows = pl.ds(r, tm // d, stride=d)
                    acc_ref.at[h][rows, :] = acc_ref.at[h][rows, :] + part


def qk_prep(proj, gq, gk, cos2, sin2, name):
    S = proj.shape[0]
    DA = proj.shape[1] // 7
    H = DA // HEAD_DIM
    tm = _tile(S, 256, 16 * DILATIONS[-1])
    nd = len(DILATIONS)

    def body(q_ref, k_ref, v_ref, gq_ref, gk_ref, c_ref, s_ref, *rest):
        outs, buf_ref = rest[:3 * nd], rest[3 * nd]
        ct, st = c_ref[...], s_ref[...]
        for t, (src, g_ref) in enumerate(((q_ref, gq_ref), (k_ref, gk_ref))):
            gain = g_ref[...]
            for h in range(H):
                sl = slice(h * HEAD_DIM, (h + 1) * HEAD_DIM)
                xh = src[:, sl]
                n = xh * _rstd(xh) * gain
                buf_ref[h] = n * ct + pltpu.roll(n, HEAD_DIM // 2, 1) * st
            _emit_dilated(buf_ref, outs[t * nd:(t + 1) * nd], tm, DA)
        for h in range(H):
            buf_ref[h] = v_ref[:, h * HEAD_DIM:(h + 1) * HEAD_DIM]
        _emit_dilated(buf_ref, outs[2 * nd:], tm, DA)

    specs, shapes = _dil_specs(S, DA, tm, BF16)
    outs = _call(
        body, name=name, grid=(S // tm,),
        in_specs=[_row(tm, DA, 0), _row(tm, DA, 1), _row(tm, DA, 2), _full((1, HEAD_DIM)), _full((1, HEAD_DIM)),
                  _row(tm, HEAD_DIM), _row(tm, HEAD_DIM)],
        out_specs=specs * 3, out_shape=shapes * 3, scratch_shapes=[_head_buf(tm, DA)],
        compiler_params=_params(48),
    )(proj, proj, proj, gq, gk, cos2, sin2)
    return outs[:nd], outs[nd:2 * nd], outs[2 * nd:]


def qk_prep_bwd(dqs, dks, dvs, proj, gq, gk, cos2, sin2, name):
    S = proj.shape[0]
    DA = proj.shape[1] // 7
    H = DA // HEAD_DIM
    tm = _tile(S, 256, 16 * DILATIONS[-1])
    nb = len(dqs)

    def body(*refs):
        dq_refs, dk_refs, dv_refs = refs[:nb], refs[nb:2 * nb], refs[2 * nb:3 * nb]
        q_ref, k_ref, gq_ref, gk_ref, c_ref, s_ref = refs[3 * nb:3 * nb + 6]
        dqo_ref, dko_ref, dvo_ref, gqa_ref, gka_ref, acc_ref = refs[3 * nb + 6:]
        ct, st = c_ref[...], s_ref[...]

        @pl.when(pl.program_id(0) == 0)
        def _():
            gqa_ref[...] = jnp.zeros_like(gqa_ref)
            gka_ref[...] = jnp.zeros_like(gka_ref)

        for parts, x_ref, g_ref, dst, gacc in ((dq_refs, q_ref, gq_ref, dqo_ref, gqa_ref),
                                               (dk_refs, k_ref, gk_ref, dko_ref, gka_ref)):
            gain = g_ref[...]
            gsum = jnp.zeros((1, HEAD_DIM), F32)
            _collect_dilated(acc_ref, parts, tm, DA)
            for h in range(H):
                sl = slice(h * HEAD_DIM, (h + 1) * HEAD_DIM)
                dout = acc_ref[h]
                dn = dout * ct + pltpu.roll(dout * st, HEAD_DIM // 2, 1)
                xh = x_ref[:, sl]
                r = _rstd(xh)
                xn = xh * r
                gsum = gsum + jnp.sum(dn * xn, axis=0, keepdims=True)
                dnn = dn * gain
                dst[:, sl] = (r * (dnn - xn * jnp.mean(dnn * xn, axis=-1, keepdims=True))).astype(BF16)
            gacc[...] += gsum
        _collect_dilated(acc_ref, dv_refs, tm, DA)
        for h in range(H):
            dvo_ref[:, h * HEAD_DIM:(h + 1) * HEAD_DIM] = acc_ref[h].astype(BF16)

    specs, _ = _dil_specs(S, DA, tm, BF16)
    return _call(
        body, name=name, grid=(S // tm,),
        in_specs=specs * 3 + [_row(tm, DA, 0), _row(tm, DA, 1), _full((1, HEAD_DIM)),
                              _full((1, HEAD_DIM)), _row(tm, HEAD_DIM), _row(tm, HEAD_DIM)],
        out_specs=[_row(tm, DA)] * 3 + [_full((1, HEAD_DIM))] * 2,
        out_shape=[SDS((S, DA), BF16)] * 3 + [SDS((1, HEAD_DIM), F32)] * 2,
        scratch_shapes=[_head_buf(tm, DA)],
        compiler_params=_params(48),
    )(*dqs, *dks, *dvs, proj, proj, gq, gk, cos2, sin2)


def _band_mask(n):
    row = lax.broadcasted_iota(jnp.int32, (Q_BLOCK, 2 * Q_BLOCK), 0)
    col = lax.broadcasted_iota(jnp.int32, (Q_BLOCK, 2 * Q_BLOCK), 1)
    first = jnp.where(n > 0, Q_BLOCK, 2 * Q_BLOCK + 1)
    return (col <= row) | ((col - row) >= first)


def attn_fwd(qh, kh, vb, d, name):
    L = qh.shape[0]
    DA = qh.shape[1] // d
    H = DA // HEAD_DIM
    nb = L // Q_BLOCK
    scale = HEAD_DIM ** -0.5
    view = (L, d * DA)

    def body(q_ref, kc_ref, kp_ref, vc_ref, vp_ref, o_ref, lse_ref):
        mask = _band_mask(pl.program_id(1))
        ones = jnp.ones((2 * Q_BLOCK, HEAD_DIM), BF16)
        for h in range(H):
            sl = slice(h * HEAD_DIM, (h + 1) * HEAD_DIM)
            keys = jnp.concatenate([kc_ref[:, sl], kp_ref[:, sl]], axis=0)
            s = lax.dot_general(q_ref[:, sl], keys, NT, preferred_element_type=F32) * scale
            s = jnp.where(mask, s, NEG)
            m = jnp.max(s, axis=-1, keepdims=True)
            p = jnp.exp(s - m).astype(BF16)
            vals = jnp.concatenate([jnp.concatenate([vc_ref[:, sl], vp_ref[:, sl]], axis=0), ones], axis=1)
            ol = jnp.dot(p, vals, preferred_element_type=F32)
            l = ol[:, HEAD_DIM:]
            o_ref[:, sl] = ol[:, :HEAD_DIM] / l
            lse_ref[0, :, h:h + 1] = m + jnp.log(l[:, 0:1])

    cur = pl.BlockSpec((Q_BLOCK, DA), lambda r, n: (n, r))
    prev = pl.BlockSpec((Q_BLOCK, DA), lambda r, n: (jnp.maximum(n - 1, 0), r))
    o, lse = _call(
        body, name=name, grid=(d, nb), in_specs=[cur, cur, prev, cur, prev],
        out_specs=[cur, pl.BlockSpec((1, Q_BLOCK, H), lambda r, n: (r, n, 0))],
        out_shape=[SDS(view, F32), SDS((d, L, H), F32)],
        compiler_params=_params(32),
    )(qh, kh, kh, vb, vb)
    return o, lse


def attn_bwd(qh, kh, vb, da, lse_d, delta_d, d, name):
    L = qh.shape[0]
    DA = qh.shape[1] // d
    H = DA // HEAD_DIM
    nb = L // Q_BLOCK
    scale = HEAD_DIM ** -0.5
    view = (L, d * DA)

    def body(q_ref, kc_ref, kp_ref, vc_ref, vp_ref, do_ref, lse_ref, dl_ref, dq_ref, dk_ref, dv_ref, dkc_ref, dvc_ref):
        n = pl.program_id(1)
        mask = _band_mask(n)

        @pl.when(n == 0)
        def _():
            dkc_ref[...] = jnp.zeros_like(dkc_ref)
            dvc_ref[...] = jnp.zeros_like(dvc_ref)

        @pl.when(n < nb)
        def _():
            for h in range(H):
                sl = slice(h * HEAD_DIM, (h + 1) * HEAD_DIM)
                q, do = q_ref[:, sl], do_ref[:, sl]
                keys = jnp.concatenate([kc_ref[:, sl], kp_ref[:, sl]], axis=0)
                vals = jnp.concatenate([vc_ref[:, sl], vp_ref[:, sl]], axis=0)
                lse = jnp.broadcast_to(lse_ref[0, :, h:h + 1], (Q_BLOCK, 2 * Q_BLOCK))
                dl = jnp.broadcast_to(dl_ref[0, :, h:h + 1], (Q_BLOCK, 2 * Q_BLOCK))
                s = lax.dot_general(q, keys, NT, preferred_element_type=F32) * scale
                p = jnp.exp(jnp.where(mask, s, NEG) - lse)
                dp = lax.dot_general(do, vals, NT, preferred_element_type=F32)
                ds = (p * (dp - dl) * scale).astype(BF16)
                dq_ref[:, sl] = jnp.dot(ds, keys, preferred_element_type=F32).astype(BF16)
                dk = lax.dot_general(ds, q, TN, preferred_element_type=F32)
                dv = lax.dot_general(p.astype(BF16), do, TN, preferred_element_type=F32)
                dk_ref[:, sl] = (dkc_ref[:, sl] + dk[Q_BLOCK:]).astype(BF16)
                dv_ref[:, sl] = (dvc_ref[:, sl] + dv[Q_BLOCK:]).astype(BF16)
                dkc_ref[:, sl] = dk[:Q_BLOCK]
                dvc_ref[:, sl] = dv[:Q_BLOCK]

        @pl.when(n == nb)
        def _():
            dk_ref[...] = dkc_ref[...].astype(BF16)
            dv_ref[...] = dvc_ref[...].astype(BF16)

    cur = pl.BlockSpec((Q_BLOCK, DA), lambda r, n: (jnp.minimum(n, nb - 1), r))
    prev = pl.BlockSpec((Q_BLOCK, DA), lambda r, n: (jnp.clip(n - 1, 0, nb - 1), r))
    late = pl.BlockSpec((Q_BLOCK, DA), lambda r, n: (jnp.maximum(n - 1, 0), r))
    stat = pl.BlockSpec((1, Q_BLOCK, H), lambda r, n: (r, jnp.minimum(n, nb - 1), 0))
    dq, dk, dv = _call(
        body, name=name, grid=(d, nb + 1), in_specs=[cur, cur, prev, cur, prev, cur, stat, stat],
        out_specs=[cur, late, late], out_shape=[SDS(view, BF16)] * 3,
        scratch_shapes=[pltpu.VMEM((Q_BLOCK, DA), F32), pltpu.VMEM((Q_BLOCK, DA), F32)],
        compiler_params=_params(32),
    )(qh, kh, kh, vb, vb, da, lse_d, delta_d)
    return dq, dk, dv


def _to_branch(stat, d):
    S, H = stat.shape
    return stat.reshape(S // d, d, H).transpose(1, 0, 2)


def _from_branch(stat):
    d, L, H = stat.shape
    return stat.transpose(1, 0, 2).reshape(L * d, H)


def att_combine(os_, lses, proj, gain, name):
    S, DA = os_[0].shape
    H = DA // HEAD_DIM
    tm = _tile(S, 256, 16 * DILATIONS[-1])
    nb = len(os_)

    def body(*refs):
        o_views, l_refs = refs[:nb], refs[nb:2 * nb]
        gate_ref, gain_ref, att_ref, y_ref, lse_ref = refs[2 * nb:2 * nb + 5]
        bufs = refs[2 * nb + 5:]
        for d, view, buf in zip(DILATIONS[1:], o_views[1:], bufs):
            for h in range(H):
                for r in range(d):
                    buf.at[h][pl.ds(r, tm // d, stride=d), :] = view[:, r * DA + h * HEAD_DIM:r * DA + (h + 1) * HEAD_DIM]
        ls = [r[...] for r in l_refs]
        top = ls[0]
        for l in ls[1:]:
            top = jnp.maximum(top, l)
        den = jnp.exp(ls[0] - top)
        for l in ls[1:]:
            den = den + jnp.exp(l - top)
        lse = top + jnp.log(den)
        lse_ref[...] = lse
        ws = [jnp.exp(l - lse) for l in ls]
        for h in range(H):
            sl = slice(h * HEAD_DIM, (h + 1) * HEAD_DIM)
            acc = ws[0][:, h:h + 1] * o_views[0][:, sl]
            for w, buf in zip(ws[1:], bufs):
                acc = acc + w[:, h:h + 1] * buf[h]
            att_ref[:, sl] = acc
        a = att_ref[...]
        g = gate_ref[...]
        y_ref[...] = (a * _rstd(a) * gain_ref[...] * (g * _sig(g))).astype(BF16)

    specs, _ = _dil_specs(S, DA, tm, F32)
    return _call(
        body, name=name, grid=(S // tm,),
        in_specs=specs + [_row(tm, H)] * nb + [_row(tm, DA, 3), _full((1, DA))],
        out_specs=[_row(tm, DA), _row(tm, DA), _row(tm, H)],
        out_shape=[SDS((S, DA), F32), SDS((S, DA), BF16), SDS((S, H), F32)],
        scratch_shapes=[_head_buf(tm, DA)] * (nb - 1),
        compiler_params=_params(48),
    )(*os_, *lses, proj, gain)


def gate_bwd(dcat, cblk, a, proj, gate_blk, gain, dilated, name):
    S, DA = a.shape
    H = DA // HEAD_DIM
    tm = _tile(S, 256, 16 * DILATIONS[-1])
    nd = len(DILATIONS) if dilated else 1

    def body(dy_ref, a_ref, gate_ref, gain_ref, dg_ref, gacc_ref, *rest):
        @pl.when(pl.program_id(0) == 0)
        def _():
            gacc_ref[...] = jnp.zeros_like(gacc_ref)

        dy, av, g, gain_v = dy_ref[...], a_ref[...], gate_ref[...], gain_ref[...]
        r = _rstd(av)
        n = av * r
        sg = _sig(g)
        dg_ref[...] = (dy * (n * gain_v) * (sg * (1.0 + g * (1.0 - sg)))).astype(BF16)
        drn = dy * (g * sg)
        gacc_ref[...] += jnp.sum(drn * n, axis=0, keepdims=True)
        dn = drn * gain_v
        da = r * (dn - n * jnp.mean(dn * n, axis=-1, keepdims=True))
        if dilated:
            da_refs, delta_ref, buf_ref = rest[:nd], rest[nd], rest[nd + 1]
            for h in range(H):
                buf_ref[h] = da[:, h * HEAD_DIM:(h + 1) * HEAD_DIM]
            _emit_dilated(buf_ref, da_refs, tm, DA)
            prod = da * av
            for h in range(H):
                delta_ref[:, h:h + 1] = jnp.sum(prod[:, h * HEAD_DIM:(h + 1) * HEAD_DIM], axis=-1, keepdims=True)
        else:
            rest[0][...] = da.astype(BF16)

    out_specs = [_row(tm, DA), _full((1, DA))]
    out_shape = [SDS((S, DA), BF16), SDS((1, DA), F32)]
    scratch = []
    if dilated:
        specs, shapes = _dil_specs(S, DA, tm, BF16)
        out_specs += specs + [_row(tm, H)]
        out_shape += shapes + [SDS((S, H), F32)]
        scratch = [_head_buf(tm, DA)]
    else:
        out_specs.append(_row(tm, DA))
        out_shape.append(SDS((S, DA), BF16))
    return _call(
        body, name=name, grid=(S // tm,),
        in_specs=[_row(tm, DA, cblk), _row(tm, DA), _row(tm, DA, gate_blk), _full((1, DA))],
        out_specs=out_specs, out_shape=out_shape, scratch_shapes=scratch, compiler_params=_params(48),
    )(dcat, a, proj, gain)


def _by_sublane_phase(offsets):
    groups = [(p, [o for o in offsets if o % 8 == p]) for p in range(8)]
    return [(p, sorted(os_)) for p, os_ in groups if os_]


def _shifted_rows(src_ref, tmp_ref, base, phase, offsets, lanes):
    if phase == 0:
        return lambda o: src_ref[pl.ds(base + o, ROW_CHUNK), lanes]
    span = offsets[-1] - phase + ROW_CHUNK
    tmp_ref[phase, pl.ds(0, span), :] = src_ref[pl.ds(base + phase, span), lanes]
    return lambda o: tmp_ref[phase, pl.ds(o - phase, ROW_CHUNK), :]


def _shift_scratch():
    return pltpu.VMEM((8, CONV_PAD + ROW_CHUNK, LANE), F32)


def _fill_u(i, a_ref, b_ref, ah_ref, bh_ref, uext_ref, tm):
    uext_ref[pl.ds(CONV_PAD, tm), :] = a_ref[...] * _sig(b_ref[...])
    uh = ah_ref[...] * _sig(bh_ref[...])
    uext_ref[pl.ds(0, CONV_PAD), :] = jnp.where(i > 0, uh, 0.0)


def conv_fwd(proj, wk, bias, ln_g, ln_b, out_g, wpw, name):
    S = proj.shape[0]
    DC = proj.shape[1] // 7
    tm = _tile(S, 128, ROW_CHUNK)
    lead = CONV_PAD - (CONV_WIDTH - 1)

    def body(a_ref, b_ref, ah_ref, bh_ref, gate_ref, wk_ref, bias_ref, lg_ref, lb_ref, og_ref, wpw_ref,
             cy_ref, z_ref, conv_ref, y_ref, uext_ref, tmp_ref):
        _fill_u(pl.program_id(0), a_ref, b_ref, ah_ref, bh_ref, uext_ref, tm)

        def cols(cc, carry):
            c0 = pl.multiple_of(cc * LANE, LANE)
            lanes = pl.ds(c0, LANE)
            for rr in range(tm // ROW_CHUNK):
                acc = jnp.broadcast_to(bias_ref[:, lanes], (ROW_CHUNK, LANE))
                for phase, offsets in _by_sublane_phase(range(lead, lead + CONV_WIDTH)):
                    rows = _shifted_rows(uext_ref, tmp_ref, rr * ROW_CHUNK, phase, offsets, lanes)
                    for o in offsets:
                        acc = acc + wk_ref[o - lead:o - lead + 1, lanes] * rows(o)
                y_ref[pl.ds(rr * ROW_CHUNK, ROW_CHUNK), lanes] = acc
            return carry

        lax.fori_loop(0, DC // LANE, cols, 0)
        y = y_ref[...]
        yc = y - jnp.mean(y, axis=-1, keepdims=True)
        ln = yc * _rstd(yc) * lg_ref[...] + lb_ref[...]
        zb = (ln * _sig(ln)).astype(BF16)
        z_ref[...] = zb
        conv = jnp.dot(zb, wpw_ref[...], preferred_element_type=F32)
        conv_ref[...] = conv
        g = gate_ref[...]
        cy_ref[...] = (conv * _rstd(conv) * og_ref[...] * (g * _sig(g))).astype(BF16)

    vec = _full((1, DC))
    return _call(
        body, name=name, grid=(S // tm,),
        in_specs=[_row(tm, DC, 4), _row(tm, DC, 5), _halo_prev(tm, DC, 4), _halo_prev(tm, DC, 5), _row(tm, DC, 6),
                  _full((CONV_PAD, DC)), vec, vec, vec, vec, _full((DC, DC))],
        out_specs=[_row(tm, DC)] * 4,
        out_shape=[SDS((S, DC), BF16), SDS((S, DC), BF16), SDS((S, DC), F32), SDS((S, DC), F32)],
        scratch_shapes=[pltpu.VMEM((CONV_PAD + tm, DC), F32), _shift_scratch()],
        compiler_params=_params(48),
    )(proj, proj, proj, proj, proj, wk, bias, ln_g, ln_b, out_g, wpw)


def conv_bwd_ln(dconv, wpw, y, ln_g, ln_b, name):
    S, DC = y.shape
    tm = _tile(S, 256, 8)

    def body(dc_ref, wpw_ref, y_ref, lg_ref, lb_ref, dy_ref, glg_ref, glb_ref, gb_ref):
        @pl.when(pl.program_id(0) == 0)
        def _():
            glg_ref[...] = jnp.zeros_like(glg_ref)
            glb_ref[...] = jnp.zeros_like(glb_ref)
            gb_ref[...] = jnp.zeros_like(gb_ref)

        dz = lax.dot_general(dc_ref[...], wpw_ref[...], NT, preferred_element_type=F32)
        yv = y_ref[...]
        yc = yv - jnp.mean(yv, axis=-1, keepdims=True)
        rstd = _rstd(yc)
        yhat = yc * rstd
        ln = yhat * lg_ref[...] + lb_ref[...]
        sg = _sig(ln)
        dln = dz * (sg * (1.0 + ln * (1.0 - sg)))
        glb_ref[...] += jnp.sum(dln, axis=0, keepdims=True)
        glg_ref[...] += jnp.sum(dln * yhat, axis=0, keepdims=True)
        dyh = dln * lg_ref[...]
        dy = rstd * (dyh - jnp.mean(dyh, axis=-1, keepdims=True) - yhat * jnp.mean(dyh * yhat, axis=-1, keepdims=True))
        dy_ref[...] = dy
        gb_ref[...] += jnp.sum(dy, axis=0, keepdims=True)

    vec = _full((1, DC))
    return _call(
        body, name=name, grid=(S // tm,),
        in_specs=[_row(tm, DC), _full((DC, DC)), _row(tm, DC), vec, vec],
        out_specs=[_row(tm, DC), vec, vec, vec],
        out_shape=[SDS((S, DC), F32)] + [SDS((1, DC), F32)] * 3,
        compiler_params=_params(48),
    )(dconv, wpw, y, ln_g, ln_b)


def conv_bwd_dw(dy, proj, wk, name):
    S, DC = dy.shape
    tm = _tile(S, 128, ROW_CHUNK)
    nsteps = S // tm
    lead = CONV_PAD - (CONV_WIDTH - 1)
    groups = ROW_CHUNK // 8

    def body(dy_ref, dyn_ref, a_ref, b_ref, ah_ref, bh_ref, wk_ref, da_ref, db_ref, gw_ref, uext_ref, dyext_ref, du_ref,
             tmp_dy_ref, tmp_u_ref):
        i = pl.program_id(0)

        @pl.when(i == 0)
        def _():
            gw_ref[...] = jnp.zeros_like(gw_ref)

        _fill_u(i, a_ref, b_ref, ah_ref, bh_ref, uext_ref, tm)
        dyext_ref[pl.ds(0, tm), :] = dy_ref[...]
        dyext_ref[pl.ds(tm, CONV_PAD), :] = jnp.where(i < nsteps - 1, dyn_ref[...], 0.0)

        def cols(cc, carry):
            c0 = pl.multiple_of(cc * LANE, LANE)
            lanes = pl.ds(c0, LANE)
            for rr in range(tm // ROW_CHUNK):
                base = rr * ROW_CHUNK
                acc = jnp.zeros((ROW_CHUNK, LANE), F32)
                for phase, offsets in _by_sublane_phase(range(CONV_WIDTH)):
                    rows = _shifted_rows(dyext_ref, tmp_dy_ref, base, phase, offsets, lanes)
                    for o in offsets:
                        j = CONV_WIDTH - 1 - o
                        acc = acc + wk_ref[j:j + 1, lanes] * rows(o)
                du_ref[pl.ds(base, ROW_CHUNK), lanes] = acc
                dyc = dyext_ref[pl.ds(base, ROW_CHUNK), lanes]
                for phase, offsets in _by_sublane_phase(range(lead, lead + CONV_WIDTH)):
                    rows = _shifted_rows(uext_ref, tmp_u_ref, base, phase, offsets, lanes)
                    for o in offsets:
                        prod = dyc * rows(o)
                        part = prod[0:8]
                        for k in range(1, groups):
                            part = part + prod[8 * k:8 * k + 8]
                        gw_ref[o - lead, :, lanes] += part
            return carry

        lax.fori_loop(0, DC // LANE, cols, 0)
        du = du_ref[...]
        sb = _sig(b_ref[...])
        da_ref[...] = (du * sb).astype(BF16)
        db_ref[...] = (du * a_ref[...] * sb * (1.0 - sb)).astype(BF16)

    return _call(
        body, name=name, grid=(nsteps,),
        in_specs=[_row(tm, DC), _halo_next(tm, DC, S // CONV_PAD), _row(tm, DC, 4), _row(tm, DC, 5),
                  _halo_prev(tm, DC, 4), _halo_prev(tm, DC, 5), _full((CONV_PAD, DC))],
        out_specs=[_row(tm, DC), _row(tm, DC), _full((CONV_PAD, 8, DC))],
        out_shape=[SDS((S, DC), BF16), SDS((S, DC), BF16), SDS((CONV_PAD, 8, DC), F32)],
        scratch_shapes=[pltpu.VMEM((CONV_PAD + tm, DC), F32), pltpu.VMEM((tm + CONV_PAD, DC), F32), pltpu.VMEM((tm, DC), F32),
                        _shift_scratch(), _shift_scratch()],
        compiler_params=_params(40),
    )(dy, dy, proj, proj, proj, proj, wk)


def loss_head(xo, target, name):
    S, D = xo.shape
    tm = _tile(S, 256, 8)

    def body(x_ref, t_ref, dy_ref, dyb_ref, acc_ref):
        @pl.when(pl.program_id(0) == 0)
        def _():
            acc_ref[...] = jnp.zeros_like(acc_ref)

        err = x_ref[...] - t_ref[...]
        dy = err * (1.0 / D)
        dy_ref[...] = dy
        dyb_ref[...] = dy.astype(BF16)
        acc_ref[...] += jnp.sum(err * dy, axis=0, keepdims=True) * 0.5

    return _call(
        body, name=name, grid=(S // tm,), in_specs=[_row(tm, D), _row(tm, D)],
        out_specs=[_row(tm, D), _row(tm, D), _full((1, D))],
        out_shape=[SDS((S, D), F32), SDS((S, D), BF16), SDS((1, D), F32)],
        compiler_params=_params(32),
    )(xo, target)


def _coords():
    x, y, c = lax.axis_index("x"), lax.axis_index("y"), lax.axis_index("c")
    return x, y, c


def _lin(p):
    return 4 * p[0] + 2 * p[1] + p[2]


def _slot(ref, axis, idx, size):
    index = [slice(None)] * len(ref.shape)
    index[axis] = pl.ds(idx * size, size)
    return ref.at[tuple(index)]


def all_gather(blocks, axes, name):
    na = len(blocks)
    sizes = [b.shape[ax] for b, ax in zip(blocks, axes)]
    fulls = [SDS(b.shape[:ax] + (N_DEV * b.shape[ax],) + b.shape[ax + 1:], b.dtype) for b, ax in zip(blocks, axes)]

    def body(*refs):
        in_refs, out_refs = refs[:na], refs[na:2 * na]
        send_sems, recv_sems, local_sems = refs[2 * na:]
        x, y, c = _coords()
        me, sibling = (x, y, c), (x, y, 1 - c)
        chips = [(1 - x, y), (x, 1 - y), (1 - x, 1 - y)]

        def place(a, p):
            return _slot(out_refs[a], axes[a], _lin(p), sizes[a])

        def copy(a, k, block, to, src=None):
            return pltpu.make_async_remote_copy(
                src_ref=place(a, block) if src is None else src, dst_ref=place(a, block),
                send_sem=send_sems.at[a, k], recv_sem=recv_sems.at[a, k], device_id=to, device_id_type=MESH)

        mine = [pltpu.make_async_copy(in_refs[a], place(a, me), local_sems.at[a]) for a in range(na)]
        for cp in mine:
            cp.start()
        first = []
        for a in range(na):
            first.append(copy(a, 0, me, sibling, src=in_refs[a]))
            first += [copy(a, 1 + j, me, (*chip, c), src=in_refs[a]) for j, chip in enumerate(chips)]
        for cp in first:
            cp.start()
        passed = []
        for j, chip in enumerate(chips):
            for a in range(na):
                copy(a, 1 + j, (*chip, c), me).wait_recv()
                cp = copy(a, 4 + j, (*chip, c), sibling)
                cp.start()
                passed.append(cp)
        for a in range(na):
            copy(a, 0, sibling, me).wait_recv()
            for j, chip in enumerate(chips):
                copy(a, 4 + j, (*chip, 1 - c), me).wait_recv()
        for cp in first + passed:
            cp.wait_send()
        for cp in mine:
            cp.wait()

    hbm = pl.BlockSpec(memory_space=pltpu.HBM)
    return _call(
        body, name=name, in_specs=[hbm] * na, out_specs=[hbm] * na, out_shape=fulls,
        scratch_shapes=[pltpu.SemaphoreType.DMA((na, 7)), pltpu.SemaphoreType.DMA((na, 7)), pltpu.SemaphoreType.DMA((na,))],
    )(*blocks)


class _Exchange:
    def __init__(self, gather, srcs, axes, name, after):
        self.gather, self.axes, self.name, self.na = gather, axes, name, len(srcs)
        if gather:
            self.sizes = [s.shape[ax] for s, ax in zip(srcs, axes)]
            lands = [s.shape[:ax] + (N_DEV * s.shape[ax],) + s.shape[ax + 1:] for s, ax in zip(srcs, axes)]
        else:
            self.sizes = [None if ax is None else s.shape[ax] // N_DEV for s, ax in zip(srcs, axes)]
            lands = [s.shape if ax is None else (N_DEV,) + s.shape[:ax] + (sz,) + s.shape[ax + 1:]
                     for s, ax, sz in zip(srcs, axes, self.sizes)]
        self.kinds = [pltpu.HBM(s.shape, s.dtype) for s in srcs] + [pltpu.HBM(l, s.dtype) for l, s in zip(lands, srcs)]
        lands = [lax.empty(l, s.dtype) for l, s in zip(lands, srcs)]
        after = jnp.zeros((8, LANE), F32) if after is None else after
        self._start([pltpu.with_memory_space_constraint(t, pltpu.HBM) for t in list(srcs) + lands], after)

    def _src(self, a, ref, owner):
        if self.gather:
            return ref
        return ref.at[_lin(owner)] if self.axes[a] is None else _slot(ref, self.axes[a], _lin(owner), self.sizes[a])

    def _dst(self, a, land, sender):
        return _slot(land, self.axes[a], _lin(sender), self.sizes[a]) if self.gather else land.at[_lin(sender)]

    def _copies(self, refs, send_sems, recv_sems):
        na = self.na
        me = _coords()
        flips = [(k >> 2 & 1, k >> 1 & 1, k & 1) for k in range(1, N_DEV)]
        peers = [tuple(1 - v if f else v for v, f in zip(me, flip)) for flip in flips]
        sends, arrivals = [], []
        for a in range(na):
            for k, peer in enumerate(peers):
                pair = dict(send_sem=send_sems.at[7 * a + k], recv_sem=recv_sems.at[7 * a + k], device_id=peer, device_id_type=MESH)
                sends.append(pltpu.make_async_remote_copy(
                    src_ref=self._src(a, refs[a], peer), dst_ref=self._dst(a, refs[na + a], me), **pair))
                arrivals.append(pltpu.make_async_remote_copy(
                    src_ref=self._src(a, refs[a], me), dst_ref=self._dst(a, refs[na + a], peer), **pair))
        return sends, arrivals

    def _place_own(self, operands):
        na = self.na
        me = jnp.reshape(_lin(_coords()), (1,)).astype(jnp.int32)
        lands = []
        for a in range(na):
            src, land, ax = operands[a], operands[na + a], self.axes[a]
            if ax == 1:
                R, C = src.shape[0], self.sizes[a]
                steps, tile = 1, (R, C)
                in_map = lambda i, me_ref: (0, me_ref[0])
            elif ax is None:
                R, C = src.shape[1:]
                tr = _tile(R, 512, 16)
                steps, tile = R // tr, (None, tr, C)
                in_map = lambda i, me_ref: (me_ref[0], i, 0)
            else:
                R, C = (src.shape[0] if self.gather else self.sizes[a]), src.shape[1]
                tr = _tile(R, 512, 16)
                steps, tile = R // tr, (tr, C)
                in_map = (lambda i, me_ref: (i, 0)) if self.gather else (lambda i, me_ref, n=R // tr: (me_ref[0] * n + i, 0))
            if self.gather:
                out_spec = pl.BlockSpec(tile, lambda i, me_ref, n=steps: (me_ref[0] * n + i, 0))
            else:
                out_spec = pl.BlockSpec((None,) + tuple(t for t in tile if t is not None), lambda i, me_ref: (me_ref[0], i, 0))

            def body(me_ref, src_ref, land_ref, out_ref):
                out_ref[...] = src_ref[...]

            lands.append(_call(
                body, name=f"{self.name}_own{a}",
                grid_spec=pltpu.PrefetchScalarGridSpec(
                    num_scalar_prefetch=1, grid=(steps,),
                    in_specs=[pl.BlockSpec(tile, in_map), pl.BlockSpec(memory_space=pl.ANY)], out_specs=out_spec),
                out_shape=SDS(land.shape, land.dtype), input_output_aliases={2: 0}, compiler_params=_params(32),
            )(me, src, land))
        return operands[:na] + lands

    def _start(self, operands, after):
        na = self.na
        operands = self._place_own(operands)

        def body(*refs):
            ins = refs[:2 * na]
            send_sems, recv_sems, token_ref = refs[2 * na + 1], refs[2 * na + 2], refs[4 * na + 3]
            for cp in self._copies(ins, send_sems, recv_sems)[0]:
                cp.start()
            token_ref[...] = jnp.zeros_like(token_ref)

        hbm = pl.BlockSpec(memory_space=pltpu.HBM)
        sem = pl.BlockSpec(memory_space=pltpu.SEMAPHORE)
        outs = _call(
            body, name=self.name + "_start",
            in_specs=[hbm] * (2 * na) + [pl.BlockSpec(memory_space=pl.ANY)],
            out_specs=[sem, sem] + [hbm] * (2 * na) + [pl.BlockSpec(memory_space=pltpu.VMEM)],
            out_shape=[pltpu.SemaphoreType.DMA((7 * na,)), pltpu.SemaphoreType.DMA((7 * na,))] + self.kinds + [SDS((8, LANE), F32)],
            input_output_aliases={i: 2 + i for i in range(2 * na)},
            compiler_params=pltpu.CompilerParams(has_side_effects=pltpu.SideEffectType.DATAFLOW_SIDE_EFFECTING),
        )(*operands, after)
        self.sems, self.thru, self.token = outs[:2], outs[2:2 + 2 * na], outs[2 + 2 * na][0:1, 0:1]

    def wait(self, after):
        na = self.na

        def body(*refs):
            ins, send_sems, recv_sems = refs[:2 * na], refs[2 * na], refs[2 * na + 1]
            sends, arrivals = self._copies(ins, send_sems, recv_sems)
            for cp in sends:
                cp.wait_send()
            for cp in arrivals:
                cp.wait_recv()

        hbm = pl.BlockSpec(memory_space=pltpu.HBM)
        sem = pl.BlockSpec(memory_space=pltpu.SEMAPHORE)
        outs = _call(
            body, name=self.name + "_wait",
            in_specs=[hbm] * (2 * na) + [sem, sem, pl.BlockSpec(memory_space=pl.ANY)],
            out_specs=[hbm] * (2 * na), out_shape=self.kinds,
            input_output_aliases={i: i for i in range(2 * na)},
            compiler_params=pltpu.CompilerParams(has_side_effects=pltpu.SideEffectType.DATAFLOW_SIDE_EFFECTING),
        )(*self.thru, *self.sems, after)
        return outs[na:]


def adamw(w, m, v, parts, layer, prev, name):
    nl, R, C = w.shape
    tr = _tile(R, 128, 8) if R % 8 == 0 else R

    def body(w_ref, m_ref, v_ref, p_ref, *rest):
        g_ref, d_ref, mo_ref, vo_ref = rest[-4:]
        g = p_ref[0].astype(F32)
        for s in range(1, N_DEV):
            g = g + p_ref[s].astype(F32)
        mn = ADAM_B1 * m_ref[0] + (1.0 - ADAM_B1) * g
        vn = ADAM_B2 * v_ref[0] + (1.0 - ADAM_B2) * (g * g)
        m_hat = mn / (1.0 - ADAM_B1 ** ADAM_STEP)
        v_hat = vn / (1.0 - ADAM_B2 ** ADAM_STEP)
        g_ref[0] = g
        d_ref[0] = -ADAM_LR * (m_hat / (jnp.sqrt(v_hat) + ADAM_EPS) + ADAM_WD * w_ref[0])
        mo_ref[0] = mn
        vo_ref[0] = vn

    row = pl.BlockSpec((1, tr, C), lambda i: (layer, i, 0))
    carried = [] if prev is None else list(prev)
    return _call(
        body, name=name, grid=(R // tr,),
        in_specs=[row, row, row, pl.BlockSpec((N_DEV, tr, C), lambda i: (0, i, 0))] + [pl.BlockSpec(memory_space=pl.ANY)] * len(carried),
        out_specs=[row] * 4, out_shape=[SDS((nl, R, C), F32)] * 4,
        input_output_aliases={4 + k: k for k in range(len(carried))},
        compiler_params=_params(48),
    )(w, m, v, parts, *carried)


def _rope_tables(S):
    inv_freq = 1.0 / (ROPE_THETA ** (jnp.arange(0, HEAD_DIM, 2, dtype=F32) / HEAD_DIM))
    ang = jnp.arange(S, dtype=F32)[:, None] * inv_freq[None, :]
    cos, sin = jnp.cos(ang), jnp.sin(ang)
    return jnp.concatenate([cos, cos], axis=-1), jnp.concatenate([-sin, sin], axis=-1)


def _pack_small(D, norm_g, dw_bias, conv_ln_g, conv_ln_b, att_out_g, conv_out_g, q_norm_g, k_norm_g, extra=None):
    qk = jnp.concatenate([q_norm_g.reshape(-1), k_norm_g.reshape(-1)])
    qk = jnp.pad(qk, (0, D - qk.shape[0])).reshape(1, D)
    zero = jnp.zeros((1, D), F32)
    return jnp.concatenate([norm_g, dw_bias, conv_ln_g, conv_ln_b, att_out_g, conv_out_g, qk, zero,
                            zero if extra is None else extra, zero], axis=0)


def _unpack_small(p):
    rows = [p[2 * i:2 * i + 2] for i in range(6)]
    qk = p[12, :4 * HEAD_DIM].reshape(2, DEPTH, HEAD_DIM)
    return rows + [qk[0], qk[1]]


def kernel(x, norm_g, w_in, q_norm_g, k_norm_g, dw_kernel, dw_bias, conv_ln_g, conv_ln_b, w_pw, att_out_g, conv_out_g, w_out, loss_target, m_norm_g, m_w_in, m_q_norm_g, m_k_norm_g, m_dw_kernel, m_dw_bias, m_conv_ln_g, m_conv_ln_b, m_w_pw, m_att_out_g, m_conv_out_g, m_w_out, v_norm_g, v_w_in, v_q_norm_g, v_k_norm_g, v_dw_kernel, v_dw_bias, v_conv_ln_g, v_conv_ln_b, v_w_pw, v_att_out_g, v_conv_out_g, v_w_out):
    xs = x[0]
    D = xs.shape[1]
    bf = lambda t, l: t[l].astype(BF16)
    wint0, dwk_f = all_gather([bf(w_in, 0).T, dw_kernel], [0, 2], "gather_first")
    early = _Exchange(True, [bf(w_pw, 0), bf(w_out, 0)], [0, 0], "gather_layer0", after=wint0)
    later = _Exchange(True, [bf(w_in, 1).T, bf(w_pw, 1), bf(w_out, 1)], [0, 0, 0], "gather_layer1", after=early.token)
    landed = {}

    def weights(l, cur):
        if l == 0:
            return wint0, later.token
        landed[1] = later.wait(cur)
        return landed[1][0], None

    def mixer_weights(l, cur):
        if l == 0:
            return early.wait(cur)
        return landed[1][1:]

    sent = [[] for _ in range(DEPTH)]

    def send_grads(l, grads, axes, tag):
        sent[l].append(_Exchange(False, grads, axes, f"scatter_{tag}_layer{l}", after=None))
        return sent[l][-1].token

    dx, loss_cols, small = local_step(xs, loss_target[0], weights, mixer_weights, dwk_f, norm_g, q_norm_g, k_norm_g, dw_bias,
                                      conv_ln_g, conv_ln_b, att_out_g, conv_out_g, send_grads, N_DEV)

    big = ((w_pw, m_w_pw, v_w_pw), (w_out, m_w_out, v_w_out), (dw_kernel, m_dw_kernel, v_dw_kernel), (w_in, m_w_in, v_w_in))
    results = [None] * len(big)
    after = dx
    for l in reversed(range(DEPTH)):
        parts = [p for ex in sent[l] for p in ex.wait(after)]
        for i, ((w, m, v), p) in enumerate(zip(big, parts)):
            results[i] = adamw(w, m, v, p, l, results[i], f"adamw_{w.shape[1]}_{w.shape[2]}_{l}")
        after = results[0][3]
    r_wpw, r_wout, r_dwk, r_win = results

    stack = lambda k: jnp.concatenate(small[k], axis=0)
    mine = _pack_small(D, stack("norm_g"), stack("dw_bias"), stack("conv_ln_g"), stack("conv_ln_b"), stack("att_out_g"),
                       stack("conv_out_g"), stack("q"), stack("k"), extra=loss_cols)
    (p_small,) = all_gather([mine], [0], "gather_small")
    pk = lambda n, dw, lg, lb, ao, co, q, k: _pack_small(D, n, dw, lg, lb, ao, co, q, k)[None]
    r_small = adamw(pk(norm_g, dw_bias, conv_ln_g, conv_ln_b, att_out_g, conv_out_g, q_norm_g, k_norm_g),
                    pk(m_norm_g, m_dw_bias, m_conv_ln_g, m_conv_ln_b, m_att_out_g, m_conv_out_g, m_q_norm_g, m_k_norm_g),
                    pk(v_norm_g, v_dw_bias, v_conv_ln_g, v_conv_ln_b, v_att_out_g, v_conv_out_g, v_q_norm_g, v_k_norm_g),
                    p_small.reshape(N_DEV, 16, D), 0, None, "adamw_small")
    r_small = [r[0] for r in r_small]
    loss = jnp.sum(r_small[0][14])

    outs = [loss, dx[None]]
    for i in range(4):
        n_, dwb, lg, lb, ao, co, q_, k_ = _unpack_small(r_small[i])
        outs += [n_, r_win[i], q_, k_, r_dwk[i], dwb, lg, lb, r_wpw[i], ao, co, r_wout[i]]
    return tuple(outs)


def local_step(xs, target, weights, mixer_weights, dwk_f, norm_g, q_norm_g, k_norm_g, dw_bias, conv_ln_g, conv_ln_b,
               att_out_g, conv_out_g, send_grads, owners):
    S, D = xs.shape
    cos2, sin2 = _rope_tables(S)
    dwk_f = jnp.pad(dwk_f, ((0, 0), (0, CONV_PAD - CONV_WIDTH), (0, 0)))

    def vec(p, l, zero=None):
        row = p[l].reshape(1, -1)
        return row if zero is None else row + zero

    saved = []
    cur = xs
    for l in range(DEPTH):
        wint, zero = weights(l, cur)
        proj, h = in_proj(cur, vec(norm_g, l, zero), wint, f"in_proj_{l}")
        qs, ks, vs = qk_prep(proj, vec(q_norm_g, l), vec(k_norm_g, l), cos2, sin2, f"qk_prep_{l}")
        os_, lses = [], []
        for i, d in enumerate(DILATIONS):
            o, lse = attn_fwd(qs[i], ks[i], vs[i], d, f"attn_fwd_{l}_d{d}")
            os_.append(o)
            lses.append(_from_branch(lse))
        att, att_y, lse = att_combine(os_, lses, proj, vec(att_out_g, l), f"att_combine_{l}")
        wpw, wout = mixer_weights(l, att_y)
        conv_y, z, conv, y = conv_fwd(proj, dwk_f[l], vec(dw_bias, l), vec(conv_ln_g, l), vec(conv_ln_b, l),
                                      vec(conv_out_g, l), wpw, f"conv_fwd_{l}")
        cat = jnp.concatenate([att_y, conv_y], axis=1)
        nxt = mm_nn(cat, wout, F32, f"out_proj_{l}", add=cur)
        saved.append(dict(x=cur, proj=proj, h=h, qs=qs, ks=ks, vs=vs, att=att, lse=lse, cat=cat, z=z, conv=conv, y=y,
                          wint=wint, wpw=wpw, wout=wout))
        cur = nxt

    dx, dxb, loss_cols = loss_head(cur, target, "loss_head")

    small = {k: [None] * DEPTH for k in ("norm_g", "dw_bias", "conv_ln_g", "conv_ln_b", "att_out_g", "conv_out_g", "q", "k")}
    for l in reversed(range(DEPTH)):
        sv = saved[l]
        proj = sv["proj"]
        dcat = mm_nt(dxb, sv["wout"], F32, f"dcat_{l}")
        g_wout = mm_nn(sv["cat"].T, dxb, BF16, f"dwout_{l}")
        dgate_c, small["conv_out_g"][l], dconv = gate_bwd(dcat, 1, sv["conv"], proj, 6, vec(conv_out_g, l), False,
                                                          f"conv_gate_bwd_{l}")
        dy, small["conv_ln_g"][l], small["conv_ln_b"][l], small["dw_bias"][l] = conv_bwd_ln(
            dconv, sv["wpw"], sv["y"], vec(conv_ln_g, l), vec(conv_ln_b, l), f"conv_bwd_ln_{l}")
        g_wpw = mm_nn(sv["z"].T, dconv, BF16, f"dwpw_{l}")
        da, db, gw = conv_bwd_dw(dy, proj, dwk_f[l], f"conv_bwd_dw_{l}")
        g_dwk = jnp.sum(gw, axis=1)[:CONV_WIDTH]
        zero = send_grads(l, [g_wpw, g_wout, g_dwk], [0, 0, 1], "mixer")
        dgate_a, small["att_out_g"][l], *datts, delta = gate_bwd(dcat, 0, sv["att"], proj, 3, vec(att_out_g, l, zero), True,
                                                                  f"att_gate_bwd_{l}")
        dqs, dks, dvs = [], [], []
        for i, d in enumerate(DILATIONS):
            dq, dk, dv = attn_bwd(sv["qs"][i], sv["ks"][i], sv["vs"][i], datts[i], _to_branch(sv["lse"], d),
                                  _to_branch(delta, d), d, f"attn_bwd_{l}_d{d}")
            dqs.append(dq)
            dks.append(dk)
            dvs.append(dv)
        dq, dk, dv, small["q"][l], small["k"][l] = qk_prep_bwd(dqs, dks, dvs, proj, vec(q_norm_g, l), vec(k_norm_g, l),
                                                                cos2, sin2, f"qk_prep_bwd_{l}")
        dproj = jnp.concatenate([dq, dk, dv, dgate_a, da, db, dgate_c], axis=1)
        g_win = mm_nn(sv["h"].T, dproj, BF16, f"dwin_{l}", owners=owners)
        zero = send_grads(l, [g_win], [None], "w_in")
        dx, dxb, small["norm_g"][l] = in_proj_bwd(dproj, sv["wint"], sv["x"], vec(norm_g, l, zero), dx, f"in_proj_bwd_{l}")

    return dx, loss_cols, small
```

```python
import jax
import jax.numpy as jnp
from jax import lax
from jax.experimental import pallas as pl
from jax.experimental.pallas import tpu as pltpu

F32 = jnp.float32
BF16 = jnp.bfloat16
SDS = jax.ShapeDtypeStruct
MESH = pl.DeviceIdType.MESH

N_DEV = 8
DEPTH = 2
HEAD_DIM = 128
CONV_WIDTH = 31
CONV_PAD = 32
DILATIONS = (1, 4, 16)
Q_BLOCK = 128
ROPE_THETA = 10000.0
EPS = 1e-6
NEG = -1e30
ADAM_LR, ADAM_B1, ADAM_B2, ADAM_EPS, ADAM_WD, ADAM_STEP = 0.001, 0.9, 0.999, 1e-08, 0.01, 10
LANE = 128
ROW_CHUNK = 64
MIB = 1 << 20
NT = (((1,), (1,)), ((), ()))
TN = (((0,), (0,)), ((), ()))


def _call(body, **kw):
    return pl.pallas_call(body, **kw)


def _params(vmem_mib):
    return pltpu.CompilerParams(vmem_limit_bytes=vmem_mib * MIB)


def _tile(dim, pref, mult):
    t = min(pref, dim)
    while dim % t or t % mult:
        t -= mult
    return t


def _sig(v):
    return jax.nn.sigmoid(v)


def _rstd(v):
    return lax.rsqrt(jnp.mean(v * v, axis=-1, keepdims=True) + EPS)


def _row(tm, cb, c=0):
    return pl.BlockSpec((tm, cb), lambda i: (i, c))


def _full(shape):
    return pl.BlockSpec(shape, lambda i: (0,) * len(shape))


def _halo_prev(tm, cb, c=0):
    k = tm // CONV_PAD
    return pl.BlockSpec((CONV_PAD, cb), lambda i: (jnp.maximum(i * k - 1, 0), c))


def _halo_next(tm, cb, nblk, c=0):
    k = tm // CONV_PAD
    return pl.BlockSpec((CONV_PAD, cb), lambda i: (jnp.minimum((i + 1) * k, nblk - 1), c))


def in_proj(x, g, wt, name):
    S, D = x.shape
    N = wt.shape[0]
    tm, tn = _tile(S, 512, 8), _tile(N, 1024, LANE)

    def body(x_ref, g_ref, w_ref, o_ref, h_ref):
        @pl.when(pl.program_id(1) == 0)
        def _():
            xf = x_ref[...]
            h_ref[...] = (xf * _rstd(xf) * g_ref[...]).astype(BF16)

        o_ref[...] = lax.dot_general(h_ref[...], w_ref[...], NT, preferred_element_type=F32)

    return _call(
        body, name=name, grid=(S // tm, N // tn),
        in_specs=[pl.BlockSpec((tm, D), lambda i, j: (i, 0)), pl.BlockSpec((1, D), lambda i, j: (0, 0)),
                  pl.BlockSpec((tn, D), lambda i, j: (j, 0))],
        out_specs=[pl.BlockSpec((tm, tn), lambda i, j: (i, j)), pl.BlockSpec((tm, D), lambda i, j: (i, 0))],
        out_shape=[SDS((S, N), F32), SDS((S, D), BF16)],
        compiler_params=_params(40),
    )(x, g, wt)


def mm_nn(a, b, out_dtype, name, add=None, owners=1):
    M, K = a.shape
    N = b.shape[1]
    tm, tn = _tile(M, 1024 if owners > 1 else 512, 8), _tile(N // owners, 1024, LANE)
    per = N // owners // tn

    def body(*refs):
        a_ref, b_ref = refs[0], refs[1]
        o_ref = refs[-1]
        acc = jnp.dot(a_ref[...], b_ref[...], preferred_element_type=F32)
        if add is not None:
            acc = acc + refs[2][...]
        o_ref[...] = acc.astype(out_dtype)

    in_specs = [pl.BlockSpec((tm, K), lambda i, j: (i, 0)), pl.BlockSpec((K, tn), lambda i, j: (0, j))]
    args = [a, b]
    if add is not None:
        in_specs.append(pl.BlockSpec((tm, tn), lambda i, j: (i, j)))
        args.append(add)
    if owners > 1:
        out_spec = pl.BlockSpec((None, tm, tn), lambda i, j: (j // per, i, j % per))
        out_shape = SDS((owners, M, N // owners), out_dtype)
    else:
        out_spec, out_shape = pl.BlockSpec((tm, tn), lambda i, j: (i, j)), SDS((M, N), out_dtype)
    return _call(
        body, name=name, grid=(M // tm, N // tn), in_specs=in_specs, out_specs=out_spec, out_shape=out_shape,
        compiler_params=_params(48),
    )(*args)


def mm_nt(a, b, out_dtype, name):
    M, K = a.shape
    N = b.shape[0]
    tm, tn = _tile(M, 512, 8), _tile(N, 1024, LANE)

    def body(a_ref, b_ref, o_ref):
        o_ref[...] = lax.dot_general(a_ref[...], b_ref[...], NT, preferred_element_type=F32).astype(out_dtype)

    return _call(
        body, name=name, grid=(M // tm, N // tn),
        in_specs=[pl.BlockSpec((tm, K), lambda i, j: (i, 0)), pl.BlockSpec((tn, K), lambda i, j: (j, 0))],
        out_specs=pl.BlockSpec((tm, tn), lambda i, j: (i, j)), out_shape=SDS((M, N), out_dtype),
        compiler_params=_params(40),
    )(a, b)


def in_proj_bwd(dproj, wt, x, g, dx_out, name):
    S, K = dproj.shape
    D = wt.shape[1]
    tm, tk = _tile(S, 512, 8), _tile(K, 1024, LANE)
    nk = K // tk

    def body(dp_ref, w_ref, x_ref, g_ref, dxo_ref, dx_ref, dxb_ref, gacc_ref):
        i, k = pl.program_id(0), pl.program_id(1)
        part = jnp.dot(dp_ref[...], w_ref[...], preferred_element_type=F32)

        @pl.when(k == 0)
        def _():
            dx_ref[...] = part

        @pl.when(k > 0)
        def _():
            dx_ref[...] += part

        @pl.when((k == 0) & (i == 0))
        def _():
            gacc_ref[...] = jnp.zeros_like(gacc_ref)

        @pl.when(k == nk - 1)
        def _():
            dh = dx_ref[...]
            xf = x_ref[...]
            r = _rstd(xf)
            n = xf * r
            gacc_ref[...] += jnp.sum(dh * n, axis=0, keepdims=True)
            dn = dh * g_ref[...]
            dx = r * (dn - n * jnp.mean(dn * n, axis=-1, keepdims=True)) + dxo_ref[...]
            dx_ref[...] = dx
            dxb_ref[...] = dx.astype(BF16)

    return _call(
        body, name=name, grid=(S // tm, nk),
        in_specs=[pl.BlockSpec((tm, tk), lambda i, k: (i, k)), pl.BlockSpec((tk, D), lambda i, k: (k, 0)),
                  pl.BlockSpec((tm, D), lambda i, k: (i, 0)), pl.BlockSpec((1, D), lambda i, k: (0, 0)),
                  pl.BlockSpec((tm, D), lambda i, k: (i, 0))],
        out_specs=[pl.BlockSpec((tm, D), lambda i, k: (i, 0)), pl.BlockSpec((tm, D), lambda i, k: (i, 0)),
                   pl.BlockSpec((1, D), lambda i, k: (0, 0))],
        out_shape=[SDS((S, D), F32), SDS((S, D), BF16), SDS((1, D), F32)],
        compiler_params=_params(54),
    )(dproj, wt, x, g, dx_out)


def _dil_specs(S, DA, tm, dtype):
    specs = [pl.BlockSpec((tm // d, d * DA), lambda i: (i, 0)) for d in DILATIONS]
    shapes = [SDS((S // d, d * DA), dtype) for d in DILATIONS]
    return specs, shapes


def _head_buf(tm, DA):
    return pltpu.VMEM((DA // HEAD_DIM, tm, HEAD_DIM), F32)


def _emit_dilated(buf_ref, dsts, tm, DA):
    for d, dst in zip(DILATIONS, dsts):
        for h in range(DA // HEAD_DIM):
            for r in range(d):
                rows = slice(None) if d == 1 else pl.ds(r, tm // d, stride=d)
                dst[:, r * DA + h * HEAD_DIM:r * DA + (h + 1) * HEAD_DIM] = buf_ref.at[h][rows, :].astype(BF16)


def _collect_dilated(acc_ref, parts, tm, DA):
    for d, p in zip(DILATIONS, parts):
        for h in range(DA // HEAD_DIM):
            for r in range(d):
                part = p[:, r * DA + h * HEAD_DIM:r * DA + (h + 1) * HEAD_DIM].astype(F32)
                if d == 1:
                    acc_ref[h] = part
                else:
                    rows = pl.ds(r, tm // d, stride=d)
                    acc_ref.at[h][rows, :] = acc_ref.at[h][rows, :] + part


def qk_prep(proj, gq, gk, cos2, sin2, name):
    S = proj.shape[0]
    DA = proj.shape[1] // 7
    H = DA // HEAD_DIM
    tm = _tile(S, 256, 16 * DILATIONS[-1])
    nd = len(DILATIONS)

    def body(q_ref, k_ref, v_ref, gq_ref, gk_ref, c_ref, s_ref, *rest):
        outs, buf_ref = rest[:3 * nd], rest[3 * nd]
        ct, st = c_ref[...], s_ref[...]
        for t, (src, g_ref) in enumerate(((q_ref, gq_ref), (k_ref, gk_ref))):
            gain = g_ref[...]
            for h in range(H):
                sl = slice(h * HEAD_DIM, (h + 1) * HEAD_DIM)
                xh = src[:, sl]
                n = xh * _rstd(xh) * gain
                buf_ref[h] = n * ct + pltpu.roll(n, HEAD_DIM // 2, 1) * st
            _emit_dilated(buf_ref, outs[t * nd:(t + 1) * nd], tm, DA)
        for h in range(H):
            buf_ref[h] = v_ref[:, h * HEAD_DIM:(h + 1) * HEAD_DIM]
        _emit_dilated(buf_ref, outs[2 * nd:], tm, DA)

    specs, shapes = _dil_specs(S, DA, tm, BF16)
    outs = _call(
        body, name=name, grid=(S // tm,),
        in_specs=[_row(tm, DA, 0), _row(tm, DA, 1), _row(tm, DA, 2), _full((1, HEAD_DIM)), _full((1, HEAD_DIM)),
                  _row(tm, HEAD_DIM), _row(tm, HEAD_DIM)],
        out_specs=specs * 3, out_shape=shapes * 3, scratch_shapes=[_head_buf(tm, DA)],
        compiler_params=_params(48),
    )(proj, proj, proj, gq, gk, cos2, sin2)
    return outs[:nd], outs[nd:2 * nd], outs[2 * nd:]


def qk_prep_bwd(dqs, dks, dvs, proj, gq, gk, cos2, sin2, dproj, name):
    S = proj.shape[0]
    DA = proj.shape[1] // 7
    H = DA // HEAD_DIM
    tm = _tile(S, 256, 16 * DILATIONS[-1])
    nb = len(dqs)

    def body(*refs):
        dq_refs, dk_refs, dv_refs = refs[:nb], refs[nb:2 * nb], refs[2 * nb:3 * nb]
        q_ref, k_ref, gq_ref, gk_ref, c_ref, s_ref = refs[3 * nb:3 * nb + 6]
        out_ref, gqa_ref, gka_ref, acc_ref = refs[3 * nb + 7:]
        ct, st = c_ref[...], s_ref[...]

        @pl.when(pl.program_id(0) == 0)
        def _():
            gqa_ref[...] = jnp.zeros_like(gqa_ref)
            gka_ref[...] = jnp.zeros_like(gka_ref)

        for parts, x_ref, g_ref, col, gacc in ((dq_refs, q_ref, gq_ref, 0, gqa_ref),
                                               (dk_refs, k_ref, gk_ref, DA, gka_ref)):
            gain = g_ref[...]
            gsum = jnp.zeros((1, HEAD_DIM), F32)
            _collect_dilated(acc_ref, parts, tm, DA)
            for h in range(H):
                sl = slice(h * HEAD_DIM, (h + 1) * HEAD_DIM)
                dout = acc_ref[h]
                dn = dout * ct + pltpu.roll(dout * st, HEAD_DIM // 2, 1)
                xh = x_ref[:, sl]
                r = _rstd(xh)
                xn = xh * r
                gsum = gsum + jnp.sum(dn * xn, axis=0, keepdims=True)
                dnn = dn * gain
                dx = r * (dnn - xn * jnp.mean(dnn * xn, axis=-1, keepdims=True))
                out_ref[:, col + h * HEAD_DIM:col + (h + 1) * HEAD_DIM] = dx.astype(BF16)
            gacc[...] += gsum
        _collect_dilated(acc_ref, dv_refs, tm, DA)
        for h in range(H):
            out_ref[:, 2 * DA + h * HEAD_DIM:2 * DA + (h + 1) * HEAD_DIM] = acc_ref[h].astype(BF16)

    specs, _ = _dil_specs(S, DA, tm, BF16)
    return _call(
        body, name=name, grid=(S // tm,),
        in_specs=specs * 3 + [_row(tm, DA, 0), _row(tm, DA, 1), _full((1, HEAD_DIM)),
                              _full((1, HEAD_DIM)), _row(tm, HEAD_DIM), _row(tm, HEAD_DIM), pl.BlockSpec(memory_space=pl.ANY)],
        out_specs=[_row(tm, 3 * DA)] + [_full((1, HEAD_DIM))] * 2,
        out_shape=[SDS((S, 7 * DA), BF16)] + [SDS((1, HEAD_DIM), F32)] * 2,
        input_output_aliases={3 * nb + 6: 0},
        scratch_shapes=[_head_buf(tm, DA)],
        compiler_params=_params(48),
    )(*dqs, *dks, *dvs, proj, proj, gq, gk, cos2, sin2, dproj)


def _band_mask(n):
    row = lax.broadcasted_iota(jnp.int32, (Q_BLOCK, 2 * Q_BLOCK), 0)
    col = lax.broadcasted_iota(jnp.int32, (Q_BLOCK, 2 * Q_BLOCK), 1)
    first = jnp.where(n > 0, Q_BLOCK, 2 * Q_BLOCK + 1)
    return (col <= row) | ((col - row) >= first)


def attn_fwd(qh, kh, vb, d, name):
    L = qh.shape[0]
    DA = qh.shape[1] // d
    H = DA // HEAD_DIM
    nb = L // Q_BLOCK
    scale = HEAD_DIM ** -0.5
    view = (L, d * DA)

    def body(q_ref, kc_ref, kp_ref, vc_ref, vp_ref, o_ref, lse_ref):
        mask = _band_mask(pl.program_id(1))
        ones = jnp.ones((2 * Q_BLOCK, HEAD_DIM), BF16)
        for h in range(H):
            sl = slice(h * HEAD_DIM, (h + 1) * HEAD_DIM)
            keys = jnp.concatenate([kc_ref[:, sl], kp_ref[:, sl]], axis=0)
            s = lax.dot_general(q_ref[:, sl], keys, NT, preferred_element_type=F32) * scale
            s = jnp.where(mask, s, NEG)
            m = jnp.max(s, axis=-1, keepdims=True)
            p = jnp.exp(s - m).astype(BF16)
            vals = jnp.concatenate([jnp.concatenate([vc_ref[:, sl], vp_ref[:, sl]], axis=0), ones], axis=1)
            ol = jnp.dot(p, vals, preferred_element_type=F32)
            l = ol[:, HEAD_DIM:]
            o_ref[:, sl] = ol[:, :HEAD_DIM] / l
            lse_ref[0, :, h:h + 1] = m + jnp.log(l[:, 0:1])

    cur = pl.BlockSpec((Q_BLOCK, DA), lambda r, n: (n, r))
    prev = pl.BlockSpec((Q_BLOCK, DA), lambda r, n: (jnp.maximum(n - 1, 0), r))
    o, lse = _call(
        body, name=name, grid=(d, nb), in_specs=[cur, cur, prev, cur, prev],
        out_specs=[cur, pl.BlockSpec((1, Q_BLOCK, H), lambda r, n: (r, n, 0))],
        out_shape=[SDS(view, F32), SDS((d, L, H), F32)],
        compiler_params=_params(32),
    )(qh, kh, kh, vb, vb)
    return o, lse


def attn_bwd(qh, kh, vb, da, lse_d, delta_d, d, name):
    L = qh.shape[0]
    DA = qh.shape[1] // d
    H = DA // HEAD_DIM
    nb = L // Q_BLOCK
    scale = HEAD_DIM ** -0.5
    view = (L, d * DA)
    resident = d > 1

    def body(q_ref, kc_ref, kp_ref, vc_ref, vp_ref, do_ref, lse_ref, dl_ref, dq_ref, dk_ref, dv_ref, dkc_ref, dvc_ref):
        n = pl.program_id(1)
        mask = _band_mask(n)
        if resident:
            done = pl.ds(pl.multiple_of(jnp.maximum(n - 1, 0) * Q_BLOCK, Q_BLOCK), Q_BLOCK)
            fresh = pl.ds(pl.multiple_of(n * Q_BLOCK, Q_BLOCK), Q_BLOCK)
        else:
            done = slice(None)

        @pl.when(n == 0)
        def _():
            dkc_ref[...] = jnp.zeros_like(dkc_ref)
            dvc_ref[...] = jnp.zeros_like(dvc_ref)

        @pl.when(n < nb)
        def _():
            for h in range(H):
                sl = slice(h * HEAD_DIM, (h + 1) * HEAD_DIM)
                q, do = q_ref[:, sl], do_ref[:, sl]
                keys = jnp.concatenate([kc_ref[:, sl], kp_ref[:, sl]], axis=0)
                vals = jnp.concatenate([vc_ref[:, sl], vp_ref[:, sl]], axis=0)
                lse = jnp.broadcast_to(lse_ref[0, :, h:h + 1], (Q_BLOCK, 2 * Q_BLOCK))
                dl = jnp.broadcast_to(dl_ref[0, :, h:h + 1], (Q_BLOCK, 2 * Q_BLOCK))
                s = lax.dot_general(q, keys, NT, preferred_element_type=F32) * scale
                p = jnp.exp(jnp.where(mask, s, NEG) - lse)
                dp = lax.dot_general(do, vals, NT, preferred_element_type=F32)
                ds = (p * (dp - dl) * scale).astype(BF16)
                dq_ref[:, sl] = jnp.dot(ds, keys, preferred_element_type=F32).astype(BF16)
                dk = lax.dot_general(ds, q, TN, preferred_element_type=F32)
                dv = lax.dot_general(p.astype(BF16), do, TN, preferred_element_type=F32)
                dk_ref[done, sl] = (dkc_ref[:, sl] + dk[Q_BLOCK:]).astype(BF16)
                dv_ref[done, sl] = (dvc_ref[:, sl] + dv[Q_BLOCK:]).astype(BF16)
                dkc_ref[:, sl] = dk[:Q_BLOCK]
                dvc_ref[:, sl] = dv[:Q_BLOCK]
                if resident:
                    dk_ref[fresh, sl] = dk[:Q_BLOCK].astype(BF16)
                    dv_ref[fresh, sl] = dv[:Q_BLOCK].astype(BF16)

        if not resident:
            @pl.when(n == nb)
            def _():
                dk_ref[...] = dkc_ref[...].astype(BF16)
                dv_ref[...] = dvc_ref[...].astype(BF16)

    steps = nb if resident else nb + 1
    cur = pl.BlockSpec((Q_BLOCK, DA), lambda r, n: (jnp.minimum(n, nb - 1), r))
    prev = pl.BlockSpec((Q_BLOCK, DA), lambda r, n: (jnp.clip(n - 1, 0, nb - 1), r))
    if resident:
        keyside = pl.BlockSpec((L, DA), lambda r, n: (0, r))
    else:
        keyside = pl.BlockSpec((Q_BLOCK, DA), lambda r, n: (jnp.maximum(n - 1, 0), r))
    stat = pl.BlockSpec((1, Q_BLOCK, H), lambda r, n: (r, jnp.minimum(n, nb - 1), 0))
    dq, dk, dv = _call(
        body, name=name, grid=(d, steps), in_specs=[cur, cur, prev, cur, prev, cur, stat, stat],
        out_specs=[cur, keyside, keyside], out_shape=[SDS(view, BF16)] * 3,
        scratch_shapes=[pltpu.VMEM((Q_BLOCK, DA), F32), pltpu.VMEM((Q_BLOCK, DA), F32)],
        compiler_params=_params(40),
    )(qh, kh, kh, vb, vb, da, lse_d, delta_d)
    return dq, dk, dv


def _to_branch(stat, d):
    S, H = stat.shape
    return stat.reshape(S // d, d, H).transpose(1, 0, 2)


def _from_branch(stat):
    d, L, H = stat.shape
    return stat.transpose(1, 0, 2).reshape(L * d, H)


def att_combine(os_, lses, proj, gain, name):
    S, DA = os_[0].shape
    H = DA // HEAD_DIM
    tm = _tile(S, 256, 16 * DILATIONS[-1])
    nb = len(os_)

    def body(*refs):
        o_views, l_refs = refs[:nb], refs[nb:2 * nb]
        gate_ref, gain_ref, att_ref, y_ref, lse_ref = refs[2 * nb:2 * nb + 5]
        bufs = refs[2 * nb + 5:]
        for d, view, buf in zip(DILATIONS[1:], o_views[1:], bufs):
            for h in range(H):
                for r in range(d):
                    buf.at[h][pl.ds(r, tm // d, stride=d), :] = view[:, r * DA + h * HEAD_DIM:r * DA + (h + 1) * HEAD_DIM]
        ls = [r[...] for r in l_refs]
        top = ls[0]
        for l in ls[1:]:
            top = jnp.maximum(top, l)
        den = jnp.exp(ls[0] - top)
        for l in ls[1:]:
            den = den + jnp.exp(l - top)
        lse = top + jnp.log(den)
        lse_ref[...] = lse
        ws = [jnp.exp(l - lse) for l in ls]
        for h in range(H):
            sl = slice(h * HEAD_DIM, (h + 1) * HEAD_DIM)
            acc = ws[0][:, h:h + 1] * o_views[0][:, sl]
            for w, buf in zip(ws[1:], bufs):
                acc = acc + w[:, h:h + 1] * buf[h]
            att_ref[:, sl] = acc
        a = att_ref[...]
        g = gate_ref[...]
        y_ref[...] = (a * _rstd(a) * gain_ref[...] * (g * _sig(g))).astype(BF16)

    specs, _ = _dil_specs(S, DA, tm, F32)
    return _call(
        body, name=name, grid=(S // tm,),
        in_specs=specs + [_row(tm, H)] * nb + [_row(tm, DA, 3), _full((1, DA))],
        out_specs=[_row(tm, DA), _row(tm, DA), _row(tm, H)],
        out_shape=[SDS((S, DA), F32), SDS((S, 2 * DA), BF16), SDS((S, H), F32)],
        scratch_shapes=[_head_buf(tm, DA)] * (nb - 1),
        compiler_params=_params(48),
    )(*os_, *lses, proj, gain)


def gate_bwd(dcat, cblk, a, proj, gate_blk, gain, dilated, dproj, name):
    S, DA = a.shape
    H = DA // HEAD_DIM
    tm = _tile(S, 256, 16 * DILATIONS[-1])
    nd = len(DILATIONS) if dilated else 1

    def body(dy_ref, a_ref, gate_ref, gain_ref, *rest):
        dg_ref, gacc_ref, *rest = rest[0 if dproj is None else 1:]

        @pl.when(pl.program_id(0) == 0)
        def _():
            gacc_ref[...] = jnp.zeros_like(gacc_ref)

        dy, av, g, gain_v = dy_ref[...], a_ref[...], gate_ref[...], gain_ref[...]
        r = _rstd(av)
        n = av * r
        sg = _sig(g)
        dg_ref[...] = (dy * (n * gain_v) * (sg * (1.0 + g * (1.0 - sg)))).astype(BF16)
        drn = dy * (g * sg)
        gacc_ref[...] += jnp.sum(drn * n, axis=0, keepdims=True)
        dn = drn * gain_v
        da = r * (dn - n * jnp.mean(dn * n, axis=-1, keepdims=True))
        if dilated:
            da_refs, delta_ref, buf_ref = rest[:nd], rest[nd], rest[nd + 1]
            for h in range(H):
                buf_ref[h] = da[:, h * HEAD_DIM:(h + 1) * HEAD_DIM]
            _emit_dilated(buf_ref, da_refs, tm, DA)
            prod = da * av
            for h in range(H):
                delta_ref[:, h:h + 1] = jnp.sum(prod[:, h * HEAD_DIM:(h + 1) * HEAD_DIM], axis=-1, keepdims=True)
        else:
            rest[0][...] = da.astype(BF16)

    out_specs = [_row(tm, DA, gate_blk), _full((1, DA))]
    out_shape = [SDS((S, 7 * DA), BF16), SDS((1, DA), F32)]
    scratch = []
    carried = [] if dproj is None else [dproj]
    if dilated:
        specs, shapes = _dil_specs(S, DA, tm, BF16)
        out_specs += specs + [_row(tm, H)]
        out_shape += shapes + [SDS((S, H), F32)]
        scratch = [_head_buf(tm, DA)]
    else:
        out_specs.append(_row(tm, DA))
        out_shape.append(SDS((S, DA), BF16))
    return _call(
        body, name=name, grid=(S // tm,),
        in_specs=[_row(tm, DA, cblk), _row(tm, DA), _row(tm, DA, gate_blk), _full((1, DA))] + [pl.BlockSpec(memory_space=pl.ANY)] * len(carried),
        out_specs=out_specs, out_shape=out_shape, scratch_shapes=scratch, compiler_params=_params(48),
        input_output_aliases={4: 0} if carried else {},
    )(dcat, a, proj, gain, *carried)


def _by_sublane_phase(offsets):
    groups = [(p, [o for o in offsets if o % 8 == p]) for p in range(8)]
    return [(p, sorted(os_)) for p, os_ in groups if os_]


def _shifted_rows(src_ref, tmp_ref, base, phase, offsets, lanes):
    if phase == 0:
        return lambda o: src_ref[pl.ds(base + o, ROW_CHUNK), lanes]
    span = offsets[-1] - phase + ROW_CHUNK
    tmp_ref[phase, pl.ds(0, span), :] = src_ref[pl.ds(base + phase, span), lanes]
    return lambda o: tmp_ref[phase, pl.ds(o - phase, ROW_CHUNK), :]


def _shift_scratch():
    return pltpu.VMEM((8, CONV_PAD + ROW_CHUNK, LANE), F32)


def _fill_u(i, a_ref, b_ref, ah_ref, bh_ref, uext_ref, tm):
    uext_ref[pl.ds(CONV_PAD, tm), :] = a_ref[...] * _sig(b_ref[...])
    uh = ah_ref[...] * _sig(bh_ref[...])
    uext_ref[pl.ds(0, CONV_PAD), :] = jnp.where(i > 0, uh, 0.0)


def conv_fwd(proj, wk, bias, ln_g, ln_b, out_g, wpw, cat, name):
    S = proj.shape[0]
    DC = proj.shape[1] // 7
    tm = _tile(S, 128, ROW_CHUNK)
    lead = CONV_PAD - (CONV_WIDTH - 1)

    def body(a_ref, b_ref, ah_ref, bh_ref, gate_ref, wk_ref, bias_ref, lg_ref, lb_ref, og_ref, wpw_ref, cat_ref,
             cy_ref, z_ref, conv_ref, y_ref, uext_ref, tmp_ref):
        _fill_u(pl.program_id(0), a_ref, b_ref, ah_ref, bh_ref, uext_ref, tm)

        def cols(cc, carry):
            c0 = pl.multiple_of(cc * LANE, LANE)
            lanes = pl.ds(c0, LANE)
            for rr in range(tm // ROW_CHUNK):
                acc = jnp.broadcast_to(bias_ref[:, lanes], (ROW_CHUNK, LANE))
                for phase, offsets in _by_sublane_phase(range(lead, lead + CONV_WIDTH)):
                    rows = _shifted_rows(uext_ref, tmp_ref, rr * ROW_CHUNK, phase, offsets, lanes)
                    for o in offsets:
                        acc = acc + wk_ref[o - lead:o - lead + 1, lanes] * rows(o)
                y_ref[pl.ds(rr * ROW_CHUNK, ROW_CHUNK), lanes] = acc
            return carry

        lax.fori_loop(0, DC // LANE, cols, 0)
        y = y_ref[...]
        yc = y - jnp.mean(y, axis=-1, keepdims=True)
        ln = yc * _rstd(yc) * lg_ref[...] + lb_ref[...]
        zb = (ln * _sig(ln)).astype(BF16)
        z_ref[...] = zb
        conv = jnp.dot(zb, wpw_ref[...], preferred_element_type=F32)
        conv_ref[...] = conv
        g = gate_ref[...]
        cy_ref[...] = (conv * _rstd(conv) * og_ref[...] * (g * _sig(g))).astype(BF16)

    vec = _full((1, DC))
    return _call(
        body, name=name, grid=(S // tm,),
        in_specs=[_row(tm, DC, 4), _row(tm, DC, 5), _halo_prev(tm, DC, 4), _halo_prev(tm, DC, 5), _row(tm, DC, 6),
                  _full((CONV_PAD, DC)), vec, vec, vec, vec, _full((DC, DC)), pl.BlockSpec(memory_space=pl.ANY)],
        out_specs=[_row(tm, DC, 1)] + [_row(tm, DC)] * 3,
        out_shape=[SDS((S, 2 * DC), BF16), SDS((S, DC), BF16), SDS((S, DC), F32), SDS((S, DC), F32)],
        input_output_aliases={11: 0},
        scratch_shapes=[pltpu.VMEM((CONV_PAD + tm, DC), F32), _shift_scratch()],
        compiler_params=_params(48),
    )(proj, proj, proj, proj, proj, wk, bias, ln_g, ln_b, out_g, wpw, cat)


def conv_bwd_ln(dconv, wpw, y, ln_g, ln_b, name):
    S, DC = y.shape
    tm = _tile(S, 256, 8)

    def body(dc_ref, wpw_ref, y_ref, lg_ref, lb_ref, dy_ref, glg_ref, glb_ref, gb_ref):
        @pl.when(pl.program_id(0) == 0)
        def _():
            glg_ref[...] = jnp.zeros_like(glg_ref)
            glb_ref[...] = jnp.zeros_like(glb_ref)
            gb_ref[...] = jnp.zeros_like(gb_ref)

        dz = lax.dot_general(dc_ref[...], wpw_ref[...], NT, preferred_element_type=F32)
        yv = y_ref[...]
        yc = yv - jnp.mean(yv, axis=-1, keepdims=True)
        rstd = _rstd(yc)
        yhat = yc * rstd
        ln = yhat * lg_ref[...] + lb_ref[...]
        sg = _sig(ln)
        dln = dz * (sg * (1.0 + ln * (1.0 - sg)))
        glb_ref[...] += jnp.sum(dln, axis=0, keepdims=True)
        glg_ref[...] += jnp.sum(dln * yhat, axis=0, keepdims=True)
        dyh = dln * lg_ref[...]
        dy = rstd * (dyh - jnp.mean(dyh, axis=-1, keepdims=True) - yhat * jnp.mean(dyh * yhat, axis=-1, keepdims=True))
        dy_ref[...] = dy
        gb_ref[...] += jnp.sum(dy, axis=0, keepdims=True)

    vec = _full((1, DC))
    return _call(
        body, name=name, grid=(S // tm,),
        in_specs=[_row(tm, DC), _full((DC, DC)), _row(tm, DC), vec, vec],
        out_specs=[_row(tm, DC), vec, vec, vec],
        out_shape=[SDS((S, DC), F32)] + [SDS((1, DC), F32)] * 3,
        compiler_params=_params(48),
    )(dconv, wpw, y, ln_g, ln_b)


def conv_bwd_dw(dy, proj, wk, dproj, name):
    S, DC = dy.shape
    tm = _tile(S, 128, ROW_CHUNK)
    nsteps = S // tm
    lead = CONV_PAD - (CONV_WIDTH - 1)
    groups = ROW_CHUNK // 8

    def body(dy_ref, dyn_ref, a_ref, b_ref, ah_ref, bh_ref, wk_ref, dproj_ref, dab_ref, gw_ref, uext_ref, dyext_ref, du_ref,
             tmp_dy_ref, tmp_u_ref):
        i = pl.program_id(0)

        @pl.when(i == 0)
        def _():
            gw_ref[...] = jnp.zeros_like(gw_ref)

        _fill_u(i, a_ref, b_ref, ah_ref, bh_ref, uext_ref, tm)
        dyext_ref[pl.ds(0, tm), :] = dy_ref[...]
        dyext_ref[pl.ds(tm, CONV_PAD), :] = jnp.where(i < nsteps - 1, dyn_ref[...], 0.0)

        def cols(cc, carry):
            c0 = pl.multiple_of(cc * LANE, LANE)
            lanes = pl.ds(c0, LANE)
            for rr in range(tm // ROW_CHUNK):
                base = rr * ROW_CHUNK
                acc = jnp.zeros((ROW_CHUNK, LANE), F32)
                for phase, offsets in _by_sublane_phase(range(CONV_WIDTH)):
                    rows = _shifted_rows(dyext_ref, tmp_dy_ref, base, phase, offsets, lanes)
                    for o in offsets:
                        j = CONV_WIDTH - 1 - o
                        acc = acc + wk_ref[j:j + 1, lanes] * rows(o)
                du_ref[pl.ds(base, ROW_CHUNK), lanes] = acc
                dyc = dyext_ref[pl.ds(base, ROW_CHUNK), lanes]
                for phase, offsets in _by_sublane_phase(range(lead, lead + CONV_WIDTH)):
                    rows = _shifted_rows(uext_ref, tmp_u_ref, base, phase, offsets, lanes)
                    for o in offsets:
                        prod = dyc * rows(o)
                        part = prod[0:8]
                        for k in range(1, groups):
                            part = part + prod[8 * k:8 * k + 8]
                        gw_ref[o - lead, :, lanes] += part
            return carry

        lax.fori_loop(0, DC // LANE, cols, 0)
        du = du_ref[...]
        sb = _sig(b_ref[...])
        dab_ref[:, :DC] = (du * sb).astype(BF16)
        dab_ref[:, DC:] = (du * a_ref[...] * sb * (1.0 - sb)).astype(BF16)

    return _call(
        body, name=name, grid=(nsteps,),
        in_specs=[_row(tm, DC), _halo_next(tm, DC, S // CONV_PAD), _row(tm, DC, 4), _row(tm, DC, 5),
                  _halo_prev(tm, DC, 4), _halo_prev(tm, DC, 5), _full((CONV_PAD, DC)), pl.BlockSpec(memory_space=pl.ANY)],
        out_specs=[_row(tm, 2 * DC, 2), _full((CONV_PAD, 8, DC))],
        out_shape=[SDS((S, 7 * DC), BF16), SDS((CONV_PAD, 8, DC), F32)],
        input_output_aliases={7: 0},
        scratch_shapes=[pltpu.VMEM((CONV_PAD + tm, DC), F32), pltpu.VMEM((tm + CONV_PAD, DC), F32), pltpu.VMEM((tm, DC), F32),
                        _shift_scratch(), _shift_scratch()],
        compiler_params=_params(40),
    )(dy, dy, proj, proj, proj, proj, wk, dproj)


def loss_head(xo, target, name):
    S, D = xo.shape
    tm = _tile(S, 256, 8)

    def body(x_ref, t_ref, dy_ref, dyb_ref, acc_ref):
        @pl.when(pl.program_id(0) == 0)
        def _():
            acc_ref[...] = jnp.zeros_like(acc_ref)

        err = x_ref[...] - t_ref[...]
        dy = err * (1.0 / D)
        dy_ref[...] = dy
        dyb_ref[...] = dy.astype(BF16)
        acc_ref[...] += jnp.sum(err * dy, axis=0, keepdims=True) * 0.5

    return _call(
        body, name=name, grid=(S // tm,), in_specs=[_row(tm, D), _row(tm, D)],
        out_specs=[_row(tm, D), _row(tm, D), _full((1, D))],
        out_shape=[SDS((S, D), F32), SDS((S, D), BF16), SDS((1, D), F32)],
        compiler_params=_params(32),
    )(xo, target)


def _coords():
    x, y, c = lax.axis_index("x"), lax.axis_index("y"), lax.axis_index("c")
    return x, y, c


def _lin(p):
    return 4 * p[0] + 2 * p[1] + p[2]


def _slot(ref, axis, idx, size):
    index = [slice(None)] * len(ref.shape)
    index[axis] = pl.ds(idx * size, size)
    return ref.at[tuple(index)]


def all_gather(blocks, axes, name):
    na = len(blocks)
    sizes = [b.shape[ax] for b, ax in zip(blocks, axes)]
    fulls = [SDS(b.shape[:ax] + (N_DEV * b.shape[ax],) + b.shape[ax + 1:], b.dtype) for b, ax in zip(blocks, axes)]

    def body(*refs):
        in_refs, out_refs = refs[:na], refs[na:2 * na]
        send_sems, recv_sems, local_sems = refs[2 * na:]
        x, y, c = _coords()
        me, sibling = (x, y, c), (x, y, 1 - c)
        chips = [(1 - x, y), (x, 1 - y), (1 - x, 1 - y)]

        def place(a, p):
            return _slot(out_refs[a], axes[a], _lin(p), sizes[a])

        def copy(a, k, block, to, src=None):
            return pltpu.make_async_remote_copy(
                src_ref=place(a, block) if src is None else src, dst_ref=place(a, block),
                send_sem=send_sems.at[a, k], recv_sem=recv_sems.at[a, k], device_id=to, device_id_type=MESH)

        mine = [pltpu.make_async_copy(in_refs[a], place(a, me), local_sems.at[a]) for a in range(na)]
        for cp in mine:
            cp.start()
        first = []
        for a in range(na):
            first.append(copy(a, 0, me, sibling, src=in_refs[a]))
            first += [copy(a, 1 + j, me, (*chip, c), src=in_refs[a]) for j, chip in enumerate(chips)]
        for cp in first:
            cp.start()
        passed = []
        for j, chip in enumerate(chips):
            for a in range(na):
                copy(a, 1 + j, (*chip, c), me).wait_recv()
                cp = copy(a, 4 + j, (*chip, c), sibling)
                cp.start()
                passed.append(cp)
        for a in range(na):
            copy(a, 0, sibling, me).wait_recv()
            for j, chip in enumerate(chips):
                copy(a, 4 + j, (*chip, 1 - c), me).wait_recv()
        for cp in first + passed:
            cp.wait_send()
        for cp in mine:
            cp.wait()

    hbm = pl.BlockSpec(memory_space=pltpu.HBM)
    return _call(
        body, name=name, in_specs=[hbm] * na, out_specs=[hbm] * na, out_shape=fulls,
        scratch_shapes=[pltpu.SemaphoreType.DMA((na, 7)), pltpu.SemaphoreType.DMA((na, 7)), pltpu.SemaphoreType.DMA((na,))],
    )(*blocks)


class _Exchange:
    def __init__(self, gather, srcs, axes, name, after):
        self.gather, self.axes, self.name, self.na = gather, axes, name, len(srcs)
        if gather:
            self.sizes = [s.shape[ax] for s, ax in zip(srcs, axes)]
            lands = [s.shape[:ax] + (N_DEV * s.shape[ax],) + s.shape[ax + 1:] for s, ax in zip(srcs, axes)]
        else:
            self.sizes = [None if ax is None else s.shape[ax] // N_DEV for s, ax in zip(srcs, axes)]
            lands = [s.shape if ax is None else (N_DEV,) + s.shape[:ax] + (sz,) + s.shape[ax + 1:]
                     for s, ax, sz in zip(srcs, axes, self.sizes)]
        self.kinds = [pltpu.HBM(s.shape, s.dtype) for s in srcs] + [pltpu.HBM(l, s.dtype) for l, s in zip(lands, srcs)]
        lands = [lax.empty(l, s.dtype) for l, s in zip(lands, srcs)]
        after = jnp.zeros((8, LANE), F32) if after is None else after
        self._start([pltpu.with_memory_space_constraint(t, pltpu.HBM) for t in list(srcs) + lands], after)

    def _src(self, a, ref, owner):
        if self.gather:
            return ref
        return ref.at[_lin(owner)] if self.axes[a] is None else _slot(ref, self.axes[a], _lin(owner), self.sizes[a])

    def _dst(self, a, land, sender):
        return _slot(land, self.axes[a], _lin(sender), self.sizes[a]) if self.gather else land.at[_lin(sender)]

    def _copies(self, refs, send_sems, recv_sems):
        na = self.na
        me = _coords()
        flips = [(k >> 2 & 1, k >> 1 & 1, k & 1) for k in range(1, N_DEV)]
        peers = [tuple(1 - v if f else v for v, f in zip(me, flip)) for flip in flips]
        sends, arrivals = [], []
        for a in range(na):
            for k, peer in enumerate(peers):
                pair = dict(send_sem=send_sems.at[7 * a + k], recv_sem=recv_sems.at[7 * a + k], device_id=peer, device_id_type=MESH)
                sends.append(pltpu.make_async_remote_copy(
                    src_ref=self._src(a, refs[a], peer), dst_ref=self._dst(a, refs[na + a], me), **pair))
                arrivals.append(pltpu.make_async_remote_copy(
                    src_ref=self._src(a, refs[a], me), dst_ref=self._dst(a, refs[na + a], peer), **pair))
        return sends, arrivals

    def _place_own(self, operands):
        na = self.na
        me = jnp.reshape(_lin(_coords()), (1,)).astype(jnp.int32)
        lands = []
        for a in range(na):
            src, land, ax = operands[a], operands[na + a], self.axes[a]
            if ax == 1:
                R, C = src.shape[0], self.sizes[a]
                steps, tile = 1, (R, C)
                in_map = lambda i, me_ref: (0, me_ref[0])
            elif ax is None:
                R, C = src.shape[1:]
                tr = _tile(R, 512, 16)
                steps, tile = R // tr, (None, tr, C)
                in_map = lambda i, me_ref: (me_ref[0], i, 0)
            else:
                R, C = (src.shape[0] if self.gather else self.sizes[a]), src.shape[1]
                tr = _tile(R, 512, 16)
                steps, tile = R // tr, (tr, C)
                in_map = (lambda i, me_ref: (i, 0)) if self.gather else (lambda i, me_ref, n=R // tr: (me_ref[0] * n + i, 0))
            if self.gather:
                out_spec = pl.BlockSpec(tile, lambda i, me_ref, n=steps: (me_ref[0] * n + i, 0))
            else:
                out_spec = pl.BlockSpec((None,) + tuple(t for t in tile if t is not None), lambda i, me_ref: (me_ref[0], i, 0))

            def body(me_ref, src_ref, land_ref, out_ref):
                out_ref[...] = src_ref[...]

            lands.append(_call(
                body, name=f"{self.name}_own{a}",
                grid_spec=pltpu.PrefetchScalarGridSpec(
                    num_scalar_prefetch=1, grid=(steps,),
                    in_specs=[pl.BlockSpec(tile, in_map), pl.BlockSpec(memory_space=pl.ANY)], out_specs=out_spec),
                out_shape=SDS(land.shape, land.dtype), input_output_aliases={2: 0}, compiler_params=_params(32),
            )(me, src, land))
        return operands[:na] + lands

    def _start(self, operands, after):
        na = self.na
        operands = self._place_own(operands)

        def body(*refs):
            ins = refs[:2 * na]
            send_sems, recv_sems, token_ref = refs[2 * na + 1], refs[2 * na + 2], refs[4 * na + 3]
            for cp in self._copies(ins, send_sems, recv_sems)[0]:
                cp.start()
            token_ref[...] = jnp.zeros_like(token_ref)

        hbm = pl.BlockSpec(memory_space=pltpu.HBM)
        sem = pl.BlockSpec(memory_space=pltpu.SEMAPHORE)
        outs = _call(
            body, name=self.name + "_start",
            in_specs=[hbm] * (2 * na) + [pl.BlockSpec(memory_space=pl.ANY)],
            out_specs=[sem, sem] + [hbm] * (2 * na) + [pl.BlockSpec(memory_space=pltpu.VMEM)],
            out_shape=[pltpu.SemaphoreType.DMA((7 * na,)), pltpu.SemaphoreType.DMA((7 * na,))] + self.kinds + [SDS((8, LANE), F32)],
            input_output_aliases={i: 2 + i for i in range(2 * na)},
            compiler_params=pltpu.CompilerParams(has_side_effects=pltpu.SideEffectType.DATAFLOW_SIDE_EFFECTING),
        )(*operands, after)
        self.sems, self.thru, self.token = outs[:2], outs[2:2 + 2 * na], outs[2 + 2 * na][0:1, 0:1]

    def wait(self, after):
        na = self.na

        def body(*refs):
            ins, send_sems, recv_sems = refs[:2 * na], refs[2 * na], refs[2 * na + 1]
            sends, arrivals = self._copies(ins, send_sems, recv_sems)
            for cp in sends:
                cp.wait_send()
            for cp in arrivals:
                cp.wait_recv()

        hbm = pl.BlockSpec(memory_space=pltpu.HBM)
        sem = pl.BlockSpec(memory_space=pltpu.SEMAPHORE)
        outs = _call(
            body, name=self.name + "_wait",
            in_specs=[hbm] * (2 * na) + [sem, sem, pl.BlockSpec(memory_space=pl.ANY)],
            out_specs=[hbm] * (2 * na), out_shape=self.kinds,
            input_output_aliases={i: i for i in range(2 * na)},
            compiler_params=pltpu.CompilerParams(has_side_effects=pltpu.SideEffectType.DATAFLOW_SIDE_EFFECTING),
        )(*self.thru, *self.sems, after)
        return outs[na:]


def adamw(w, m, v, parts, layer, prev, name):
    nl, R, C = w.shape
    tr = _tile(R, 128, 8) if R % 8 == 0 else R

    def body(w_ref, m_ref, v_ref, p_ref, *rest):
        g_ref, d_ref, mo_ref, vo_ref = rest[-4:]
        g = p_ref[0].astype(F32)
        for s in range(1, N_DEV):
            g = g + p_ref[s].astype(F32)
        mn = ADAM_B1 * m_ref[0] + (1.0 - ADAM_B1) * g
        vn = ADAM_B2 * v_ref[0] + (1.0 - ADAM_B2) * (g * g)
        m_hat = mn / (1.0 - ADAM_B1 ** ADAM_STEP)
        v_hat = vn / (1.0 - ADAM_B2 ** ADAM_STEP)
        g_ref[0] = g
        d_ref[0] = -ADAM_LR * (m_hat / (jnp.sqrt(v_hat) + ADAM_EPS) + ADAM_WD * w_ref[0])
        mo_ref[0] = mn
        vo_ref[0] = vn

    row = pl.BlockSpec((1, tr, C), lambda i: (layer, i, 0))
    carried = [] if prev is None else list(prev)
    return _call(
        body, name=name, grid=(R // tr,),
        in_specs=[row, row, row, pl.BlockSpec((N_DEV, tr, C), lambda i: (0, i, 0))] + [pl.BlockSpec(memory_space=pl.ANY)] * len(carried),
        out_specs=[row] * 4, out_shape=[SDS((nl, R, C), F32)] * 4,
        input_output_aliases={4 + k: k for k in range(len(carried))},
        compiler_params=_params(48),
    )(w, m, v, parts, *carried)


def _rope_tables(S):
    inv_freq = 1.0 / (ROPE_THETA ** (jnp.arange(0, HEAD_DIM, 2, dtype=F32) / HEAD_DIM))
    ang = jnp.arange(S, dtype=F32)[:, None] * inv_freq[None, :]
    cos, sin = jnp.cos(ang), jnp.sin(ang)
    return jnp.concatenate([cos, cos], axis=-1), jnp.concatenate([-sin, sin], axis=-1)


def _pack_small(D, norm_g, dw_bias, conv_ln_g, conv_ln_b, att_out_g, conv_out_g, q_norm_g, k_norm_g, extra=None):
    qk = jnp.concatenate([q_norm_g.reshape(-1), k_norm_g.reshape(-1)])
    qk = jnp.pad(qk, (0, D - qk.shape[0])).reshape(1, D)
    zero = jnp.zeros((1, D), F32)
    return jnp.concatenate([norm_g, dw_bias, conv_ln_g, conv_ln_b, att_out_g, conv_out_g, qk, zero,
                            zero if extra is None else extra, zero], axis=0)


def _unpack_small(p):
    rows = [p[2 * i:2 * i + 2] for i in range(6)]
    qk = p[12, :4 * HEAD_DIM].reshape(2, DEPTH, HEAD_DIM)
    return rows + [qk[0], qk[1]]


def kernel(x, norm_g, w_in, q_norm_g, k_norm_g, dw_kernel, dw_bias, conv_ln_g, conv_ln_b, w_pw, att_out_g, conv_out_g, w_out, loss_target, m_norm_g, m_w_in, m_q_norm_g, m_k_norm_g, m_dw_kernel, m_dw_bias, m_conv_ln_g, m_conv_ln_b, m_w_pw, m_att_out_g, m_conv_out_g, m_w_out, v_norm_g, v_w_in, v_q_norm_g, v_k_norm_g, v_dw_kernel, v_dw_bias, v_conv_ln_g, v_conv_ln_b, v_w_pw, v_att_out_g, v_conv_out_g, v_w_out):
    xs = x[0]
    D = xs.shape[1]
    bf = lambda t, l: t[l].astype(BF16)
    wint0, dwk_f = all_gather([bf(w_in, 0).T, dw_kernel], [0, 2], "gather_first")
    early = _Exchange(True, [bf(w_pw, 0), bf(w_out, 0)], [0, 0], "gather_layer0", after=wint0)
    later = _Exchange(True, [bf(w_in, 1).T, bf(w_pw, 1), bf(w_out, 1)], [0, 0, 0], "gather_layer1", after=early.token)
    landed = {}

    def weights(l, cur):
        if l == 0:
            return wint0, later.token
        landed[1] = later.wait(cur)
        return landed[1][0], None

    def mixer_weights(l, cur):
        if l == 0:
            return early.wait(cur)
        return landed[1][1:]

    sent = [[] for _ in range(DEPTH)]

    def send_grads(l, grads, axes, tag):
        sent[l].append(_Exchange(False, grads, axes, f"scatter_{tag}_layer{l}", after=None))
        return sent[l][-1].token

    dx, loss_cols, small = local_step(xs, loss_target[0], weights, mixer_weights, dwk_f, norm_g, q_norm_g, k_norm_g, dw_bias,
                                      conv_ln_g, conv_ln_b, att_out_g, conv_out_g, send_grads, N_DEV)

    big = ((w_pw, m_w_pw, v_w_pw), (w_out, m_w_out, v_w_out), (dw_kernel, m_dw_kernel, v_dw_kernel), (w_in, m_w_in, v_w_in))
    results = [None] * len(big)
    after = dx
    for l in reversed(range(DEPTH)):
        parts = [p for ex in sent[l] for p in ex.wait(after)]
        for i, ((w, m, v), p) in enumerate(zip(big, parts)):
            results[i] = adamw(w, m, v, p, l, results[i], f"adamw_{w.shape[1]}_{w.shape[2]}_{l}")
        after = results[0][3]
    r_wpw, r_wout, r_dwk, r_win = results

    stack = lambda k: jnp.concatenate(small[k], axis=0)
    mine = _pack_small(D, stack("norm_g"), stack("dw_bias"), stack("conv_ln_g"), stack("conv_ln_b"), stack("att_out_g"),
                       stack("conv_out_g"), stack("q"), stack("k"), extra=loss_cols)
    (p_small,) = all_gather([mine], [0], "gather_small")
    pk = lambda n, dw, lg, lb, ao, co, q, k: _pack_small(D, n, dw, lg, lb, ao, co, q, k)[None]
    r_small = adamw(pk(norm_g, dw_bias, conv_ln_g, conv_ln_b, att_out_g, conv_out_g, q_norm_g, k_norm_g),
                    pk(m_norm_g, m_dw_bias, m_conv_ln_g, m_conv_ln_b, m_att_out_g, m_conv_out_g, m_q_norm_g, m_k_norm_g),
                    pk(v_norm_g, v_dw_bias, v_conv_ln_g, v_conv_ln_b, v_att_out_g, v_conv_out_g, v_q_norm_g, v_k_norm_g),
                    p_small.reshape(N_DEV, 16, D), 0, None, "adamw_small")
    r_small = [r[0] for r in r_small]
    loss = jnp.sum(r_small[0][14])

    outs = [loss, dx[None]]
    for i in range(4):
        n_, dwb, lg, lb, ao, co, q_, k_ = _unpack_small(r_small[i])
        outs += [n_, r_win[i], q_, k_, r_dwk[i], dwb, lg, lb, r_wpw[i], ao, co, r_wout[i]]
    return tuple(outs)


def local_step(xs, target, weights, mixer_weights, dwk_f, norm_g, q_norm_g, k_norm_g, dw_bias, conv_ln_g, conv_ln_b,
               att_out_g, conv_out_g, send_grads, owners):
    S, D = xs.shape
    cos2, sin2 = _rope_tables(S)
    dwk_f = jnp.pad(dwk_f, ((0, 0), (0, CONV_PAD - CONV_WIDTH), (0, 0)))

    def vec(p, l, zero=None):
        row = p[l].reshape(1, -1)
        return row if zero is None else row + zero

    saved = []
    cur = xs
    for l in range(DEPTH):
        wint, zero = weights(l, cur)
        proj, h = in_proj(cur, vec(norm_g, l, zero), wint, f"in_proj_{l}")
        qs, ks, vs = qk_prep(proj, vec(q_norm_g, l), vec(k_norm_g, l), cos2, sin2, f"qk_prep_{l}")
        os_, lses = [], []
        for i, d in enumerate(DILATIONS):
            o, lse = attn_fwd(qs[i], ks[i], vs[i], d, f"attn_fwd_{l}_d{d}")
            os_.append(o)
            lses.append(_from_branch(lse))
        att, cat, lse = att_combine(os_, lses, proj, vec(att_out_g, l), f"att_combine_{l}")
        wpw, wout = mixer_weights(l, lse)
        cat, z, conv, y = conv_fwd(proj, dwk_f[l], vec(dw_bias, l), vec(conv_ln_g, l), vec(conv_ln_b, l),
                                   vec(conv_out_g, l), wpw, cat, f"conv_fwd_{l}")
        nxt = mm_nn(cat, wout, F32, f"out_proj_{l}", add=cur)
        saved.append(dict(x=cur, proj=proj, h=h, qs=qs, ks=ks, vs=vs, att=att, lse=lse, cat=cat, z=z, conv=conv, y=y,
                          wint=wint, wpw=wpw, wout=wout))
        cur = nxt

    dx, dxb, loss_cols = loss_head(cur, target, "loss_head")

    small = {k: [None] * DEPTH for k in ("norm_g", "dw_bias", "conv_ln_g", "conv_ln_b", "att_out_g", "conv_out_g", "q", "k")}
    for l in reversed(range(DEPTH)):
        sv = saved[l]
        proj = sv["proj"]
        dcat = mm_nt(dxb, sv["wout"], F32, f"dcat_{l}")
        g_wout = mm_nn(sv["cat"].T, dxb, BF16, f"dwout_{l}")
        dproj, small["conv_out_g"][l], dconv = gate_bwd(dcat, 1, sv["conv"], proj, 6, vec(conv_out_g, l), False, None,
                                                        f"conv_gate_bwd_{l}")
        dy, small["conv_ln_g"][l], small["conv_ln_b"][l], small["dw_bias"][l] = conv_bwd_ln(
            dconv, sv["wpw"], sv["y"], vec(conv_ln_g, l), vec(conv_ln_b, l), f"conv_bwd_ln_{l}")
        g_wpw = mm_nn(sv["z"].T, dconv, BF16, f"dwpw_{l}")
        dproj, gw = conv_bwd_dw(dy, proj, dwk_f[l], dproj, f"conv_bwd_dw_{l}")
        g_dwk = jnp.sum(gw, axis=1)[:CONV_WIDTH]
        zero = send_grads(l, [g_wpw, g_wout, g_dwk], [0, 0, 1], "mixer")
        dproj, small["att_out_g"][l], *datts, delta = gate_bwd(dcat, 0, sv["att"], proj, 3, vec(att_out_g, l, zero), True,
                                                                dproj, f"att_gate_bwd_{l}")
        dqs, dks, dvs = [], [], []
        for i, d in enumerate(DILATIONS):
            dq, dk, dv = attn_bwd(sv["qs"][i], sv["ks"][i], sv["vs"][i], datts[i], _to_branch(sv["lse"], d),
                                  _to_branch(delta, d), d, f"attn_bwd_{l}_d{d}")
            dqs.append(dq)
            dks.append(dk)
            dvs.append(dv)
        dproj, small["q"][l], small["k"][l] = qk_prep_bwd(dqs, dks, dvs, proj, vec(q_norm_g, l), vec(k_norm_g, l),
                                                          cos2, sin2, dproj, f"qk_prep_bwd_{l}")
        g_win = mm_nn(sv["h"].T, dproj, BF16, f"dwin_{l}", owners=owners)
        zero = send_grads(l, [g_win], [None], "w_in")
        dx, dxb, small["norm_g"][l] = in_proj_bwd(dproj, sv["wint"], sv["x"], vec(norm_g, l, zero), dx, f"in_proj_bwd_{l}")

    return dx, loss_cols, small
```

```python
import jax
import jax.numpy as jnp
from jax import lax
from jax.experimental import pallas as pl
from jax.experimental.pallas import tpu as pltpu

F32 = jnp.float32
BF16 = jnp.bfloat16
SDS = jax.ShapeDtypeStruct
MESH = pl.DeviceIdType.MESH

N_DEV = 8
DEPTH = 2
HEAD_DIM = 128
CONV_WIDTH = 31
CONV_PAD = 32
DILATIONS = (1, 4, 16)
Q_BLOCK = 128
ROPE_THETA = 10000.0
EPS = 1e-6
NEG = -1e30
ADAM_LR, ADAM_B1, ADAM_B2, ADAM_EPS, ADAM_WD, ADAM_STEP = 0.001, 0.9, 0.999, 1e-08, 0.01, 10
LANE = 128
ROW_CHUNK = 64
MIB = 1 << 20
NT = (((1,), (1,)), ((), ()))
TN = (((0,), (0,)), ((), ()))


def _call(body, **kw):
    return pl.pallas_call(body, **kw)


def _params(vmem_mib):
    return pltpu.CompilerParams(vmem_limit_bytes=vmem_mib * MIB)


def _tile(dim, pref, mult):
    t = min(pref, dim)
    while dim % t or t % mult:
        t -= mult
    return t


def _sig(v):
    return jax.nn.sigmoid(v)


def _rstd(v):
    return lax.rsqrt(jnp.mean(v * v, axis=-1, keepdims=True) + EPS)


def _row(tm, cb, c=0):
    return pl.BlockSpec((tm, cb), lambda i: (i, c))


def _full(shape):
    return pl.BlockSpec(shape, lambda i: (0,) * len(shape))


def _halo_prev(tm, cb, c=0):
    k = tm // CONV_PAD
    return pl.BlockSpec((CONV_PAD, cb), lambda i: (jnp.maximum(i * k - 1, 0), c))


def _halo_next(tm, cb, nblk, c=0):
    k = tm // CONV_PAD
    return pl.BlockSpec((CONV_PAD, cb), lambda i: (jnp.minimum((i + 1) * k, nblk - 1), c))


def in_proj(x, g, wt, name):
    S, D = x.shape
    N = wt.shape[0]
    tm, tn = _tile(S, 512, 8), _tile(N, 1024, LANE)

    def body(x_ref, g_ref, w_ref, o_ref, h_ref):
        @pl.when(pl.program_id(1) == 0)
        def _():
            xf = x_ref[...]
            h_ref[...] = (xf * _rstd(xf) * g_ref[...]).astype(BF16)

        o_ref[...] = lax.dot_general(h_ref[...], w_ref[...], NT, preferred_element_type=F32)

    return _call(
        body, name=name, grid=(S // tm, N // tn),
        in_specs=[pl.BlockSpec((tm, D), lambda i, j: (i, 0)), pl.BlockSpec((1, D), lambda i, j: (0, 0)),
                  pl.BlockSpec((tn, D), lambda i, j: (j, 0))],
        out_specs=[pl.BlockSpec((tm, tn), lambda i, j: (i, j)), pl.BlockSpec((tm, D), lambda i, j: (i, 0))],
        out_shape=[SDS((S, N), F32), SDS((S, D), BF16)],
        compiler_params=_params(40),
    )(x, g, wt)


def mm_nn(a, b, out_dtype, name, add=None, owners=1):
    M, K = a.shape
    N = b.shape[1]
    tm, tn = _tile(M, 1024 if owners > 1 else 512, 8), _tile(N // owners, 1024, LANE)
    per = N // owners // tn

    def body(*refs):
        a_ref, b_ref = refs[0], refs[1]
        o_ref = refs[-1]
        acc = jnp.dot(a_ref[...], b_ref[...], preferred_element_type=F32)
        if add is not None:
            acc = acc + refs[2][...]
        o_ref[...] = acc.astype(out_dtype)

    in_specs = [pl.BlockSpec((tm, K), lambda i, j: (i, 0)), pl.BlockSpec((K, tn), lambda i, j: (0, j))]
    args = [a, b]
    if add is not None:
        in_specs.append(pl.BlockSpec((tm, tn), lambda i, j: (i, j)))
        args.append(add)
    if owners > 1:
        out_spec = pl.BlockSpec((None, tm, tn), lambda i, j: (j // per, i, j % per))
        out_shape = SDS((owners, M, N // owners), out_dtype)
    else:
        out_spec, out_shape = pl.BlockSpec((tm, tn), lambda i, j: (i, j)), SDS((M, N), out_dtype)
    return _call(
        body, name=name, grid=(M // tm, N // tn), in_specs=in_specs, out_specs=out_spec, out_shape=out_shape,
        compiler_params=_params(48),
    )(*args)


def mm_nt(a, b, out_dtype, name):
    M, K = a.shape
    N = b.shape[0]
    tm, tn = _tile(M, 512, 8), _tile(N, 1024, LANE)

    def body(a_ref, b_ref, o_ref):
        o_ref[...] = lax.dot_general(a_ref[...], b_ref[...], NT, preferred_element_type=F32).astype(out_dtype)

    return _call(
        body, name=name, grid=(M // tm, N // tn),
        in_specs=[pl.BlockSpec((tm, K), lambda i, j: (i, 0)), pl.BlockSpec((tn, K), lambda i, j: (j, 0))],
        out_specs=pl.BlockSpec((tm, tn), lambda i, j: (i, j)), out_shape=SDS((M, N), out_dtype),
        compiler_params=_params(40),
    )(a, b)


def in_proj_bwd(dproj, wt, x, g, dx_out, name):
    S, K = dproj.shape
    D = wt.shape[1]
    tm, tk = _tile(S, 512, 8), _tile(K, 1024, LANE)
    nk = K // tk

    def body(dp_ref, w_ref, x_ref, g_ref, dxo_ref, dx_ref, dxb_ref, gacc_ref):
        i, k = pl.program_id(0), pl.program_id(1)
        part = jnp.dot(dp_ref[...], w_ref[...], preferred_element_type=F32)

        @pl.when(k == 0)
        def _():
            dx_ref[...] = part

        @pl.when(k > 0)
        def _():
            dx_ref[...] += part

        @pl.when((k == 0) & (i == 0))
        def _():
            gacc_ref[...] = jnp.zeros_like(gacc_ref)

        @pl.when(k == nk - 1)
        def _():
            dh = dx_ref[...]
            xf = x_ref[...]
            r = _rstd(xf)
            n = xf * r
            gacc_ref[...] += jnp.sum(dh * n, axis=0, keepdims=True)
            dn = dh * g_ref[...]
            dx = r * (dn - n * jnp.mean(dn * n, axis=-1, keepdims=True)) + dxo_ref[...]
            dx_ref[...] = dx
            dxb_ref[...] = dx.astype(BF16)

    return _call(
        body, name=name, grid=(S // tm, nk),
        in_specs=[pl.BlockSpec((tm, tk), lambda i, k: (i, k)), pl.BlockSpec((tk, D), lambda i, k: (k, 0)),
                  pl.BlockSpec((tm, D), lambda i, k: (i, 0)), pl.BlockSpec((1, D), lambda i, k: (0, 0)),
                  pl.BlockSpec((tm, D), lambda i, k: (i, 0))],
        out_specs=[pl.BlockSpec((tm, D), lambda i, k: (i, 0)), pl.BlockSpec((tm, D), lambda i, k: (i, 0)),
                   pl.BlockSpec((1, D), lambda i, k: (0, 0))],
        out_shape=[SDS((S, D), F32), SDS((S, D), BF16), SDS((1, D), F32)],
        compiler_params=_params(54),
    )(dproj, wt, x, g, dx_out)


def _dil_specs(S, DA, tm, dtype):
    specs = [pl.BlockSpec((tm // d, d * DA), lambda i: (i, 0)) for d in DILATIONS]
    shapes = [SDS((S // d, d * DA), dtype) for d in DILATIONS]
    return specs, shapes


def _head_buf(tm, DA):
    return pltpu.VMEM((DA // HEAD_DIM, tm, HEAD_DIM), F32)


def _emit_dilated(buf_ref, dsts, tm, DA):
    for d, dst in zip(DILATIONS, dsts):
        for h in range(DA // HEAD_DIM):
            for r in range(d):
                rows = slice(None) if d == 1 else pl.ds(r, tm // d, stride=d)
                dst[:, r * DA + h * HEAD_DIM:r * DA + (h + 1) * HEAD_DIM] = buf_ref.at[h][rows, :].astype(BF16)


def _collect_dilated(acc_ref, parts, tm, DA):
    for d, p in zip(DILATIONS, parts):
        for h in range(DA // HEAD_DIM):
            for r in range(d):
                part = p[:, r * DA + h * HEAD_DIM:r * DA + (h + 1) * HEAD_DIM].astype(F32)
                if d == 1:
                    acc_ref[h] = part
                else:
                    rows = pl.ds(r, tm // d, stride=d)
                    acc_ref.at[h][rows, :] = acc_ref.at[h][rows, :] + part


def qk_prep(proj, gq, gk, cos2, sin2, name):
    S = proj.shape[0]
    DA = proj.shape[1] // 7
    H = DA // HEAD_DIM
    tm = _tile(S, 256, 16 * DILATIONS[-1])
    nd = len(DILATIONS)

    def body(q_ref, k_ref, v_ref, gq_ref, gk_ref, c_ref, s_ref, *rest):
        outs, buf_ref = rest[:3 * nd], rest[3 * nd]
        ct, st = c_ref[...], s_ref[...]
        for t, (src, g_ref) in enumerate(((q_ref, gq_ref), (k_ref, gk_ref))):
            gain = g_ref[...]
            for h in range(H):
                sl = slice(h * HEAD_DIM, (h + 1) * HEAD_DIM)
                xh = src[:, sl]
                n = xh * _rstd(xh) * gain
                buf_ref[h] = n * ct + pltpu.roll(n, HEAD_DIM // 2, 1) * st
            _emit_dilated(buf_ref, outs[t * nd:(t + 1) * nd], tm, DA)
        for h in range(H):
            buf_ref[h] = v_ref[:, h * HEAD_DIM:(h + 1) * HEAD_DIM]
        _emit_dilated(buf_ref, outs[2 * nd:], tm, DA)

    specs, shapes = _dil_specs(S, DA, tm, BF16)
    outs = _call(
        body, name=name, grid=(S // tm,),
        in_specs=[_row(tm, DA, 0), _row(tm, DA, 1), _row(tm, DA, 2), _full((1, HEAD_DIM)), _full((1, HEAD_DIM)),
                  _row(tm, HEAD_DIM), _row(tm, HEAD_DIM)],
        out_specs=specs * 3, out_shape=shapes * 3, scratch_shapes=[_head_buf(tm, DA)],
        compiler_params=_params(48),
    )(proj, proj, proj, gq, gk, cos2, sin2)
    return outs[:nd], outs[nd:2 * nd], outs[2 * nd:]


def qk_prep_bwd(dqs, dks, dvs, proj, gq, gk, cos2, sin2, dproj, name):
    S = proj.shape[0]
    DA = proj.shape[1] // 7
    H = DA // HEAD_DIM
    tm = _tile(S, 256, 16 * DILATIONS[-1])
    nb = len(dqs)

    def body(*refs):
        dq_refs, dk_refs, dv_refs = refs[:nb], refs[nb:2 * nb], refs[2 * nb:3 * nb]
        q_ref, k_ref, gq_ref, gk_ref, c_ref, s_ref = refs[3 * nb:3 * nb + 6]
        out_ref, gqa_ref, gka_ref, acc_ref = refs[3 * nb + 7:]
        ct, st = c_ref[...], s_ref[...]

        @pl.when(pl.program_id(0) == 0)
        def _():
            gqa_ref[...] = jnp.zeros_like(gqa_ref)
            gka_ref[...] = jnp.zeros_like(gka_ref)

        for parts, x_ref, g_ref, col, gacc in ((dq_refs, q_ref, gq_ref, 0, gqa_ref),
                                               (dk_refs, k_ref, gk_ref, DA, gka_ref)):
            gain = g_ref[...]
            gsum = jnp.zeros((1, HEAD_DIM), F32)
            _collect_dilated(acc_ref, parts, tm, DA)
            for h in range(H):
                sl = slice(h * HEAD_DIM, (h + 1) * HEAD_DIM)
                dout = acc_ref[h]
                dn = dout * ct + pltpu.roll(dout * st, HEAD_DIM // 2, 1)
                xh = x_ref[:, sl]
                r = _rstd(xh)
                xn = xh * r
                gsum = gsum + jnp.sum(dn * xn, axis=0, keepdims=True)
                dnn = dn * gain
                dx = r * (dnn - xn * jnp.mean(dnn * xn, axis=-1, keepdims=True))
                out_ref[:, col + h * HEAD_DIM:col + (h + 1) * HEAD_DIM] = dx.astype(BF16)
            gacc[...] += gsum
        _collect_dilated(acc_ref, dv_refs, tm, DA)
        for h in range(H):
            out_ref[:, 2 * DA + h * HEAD_DIM:2 * DA + (h + 1) * HEAD_DIM] = acc_ref[h].astype(BF16)

    specs, _ = _dil_specs(S, DA, tm, BF16)
    return _call(
        body, name=name, grid=(S // tm,),
        in_specs=specs * 3 + [_row(tm, DA, 0), _row(tm, DA, 1), _full((1, HEAD_DIM)),
                              _full((1, HEAD_DIM)), _row(tm, HEAD_DIM), _row(tm, HEAD_DIM), pl.BlockSpec(memory_space=pl.ANY)],
        out_specs=[_row(tm, 3 * DA)] + [_full((1, HEAD_DIM))] * 2,
        out_shape=[SDS((S, 7 * DA), BF16)] + [SDS((1, HEAD_DIM), F32)] * 2,
        input_output_aliases={3 * nb + 6: 0},
        scratch_shapes=[_head_buf(tm, DA)],
        compiler_params=_params(48),
    )(*dqs, *dks, *dvs, proj, proj, gq, gk, cos2, sin2, dproj)


def _band_mask(n):
    row = lax.broadcasted_iota(jnp.int32, (Q_BLOCK, 2 * Q_BLOCK), 0)
    col = lax.broadcasted_iota(jnp.int32, (Q_BLOCK, 2 * Q_BLOCK), 1)
    first = jnp.where(n > 0, Q_BLOCK, 2 * Q_BLOCK + 1)
    return (col <= row) | ((col - row) >= first)


def attn_fwd(qh, kh, vb, d, name):
    L = qh.shape[0]
    DA = qh.shape[1] // d
    H = DA // HEAD_DIM
    nb = L // Q_BLOCK
    scale = HEAD_DIM ** -0.5
    view = (L, d * DA)

    def body(q_ref, kc_ref, kp_ref, vc_ref, vp_ref, o_ref, lse_ref):
        mask = _band_mask(pl.program_id(1))
        ones = jnp.ones((2 * Q_BLOCK, HEAD_DIM), BF16)
        for h in range(H):
            sl = slice(h * HEAD_DIM, (h + 1) * HEAD_DIM)
            keys = jnp.concatenate([kc_ref[:, sl], kp_ref[:, sl]], axis=0)
            s = lax.dot_general(q_ref[:, sl], keys, NT, preferred_element_type=F32) * scale
            s = jnp.where(mask, s, NEG)
            m = jnp.max(s, axis=-1, keepdims=True)
            p = jnp.exp(s - m).astype(BF16)
            vals = jnp.concatenate([jnp.concatenate([vc_ref[:, sl], vp_ref[:, sl]], axis=0), ones], axis=1)
            ol = jnp.dot(p, vals, preferred_element_type=F32)
            l = ol[:, HEAD_DIM:]
            o_ref[:, sl] = ol[:, :HEAD_DIM] / l
            lse_ref[0, :, h:h + 1] = m + jnp.log(l[:, 0:1])

    cur = pl.BlockSpec((Q_BLOCK, DA), lambda r, n: (n, r))
    prev = pl.BlockSpec((Q_BLOCK, DA), lambda r, n: (jnp.maximum(n - 1, 0), r))
    o, lse = _call(
        body, name=name, grid=(d, nb), in_specs=[cur, cur, prev, cur, prev],
        out_specs=[cur, pl.BlockSpec((1, Q_BLOCK, H), lambda r, n: (r, n, 0))],
        out_shape=[SDS(view, F32), SDS((d, L, H), F32)],
        compiler_params=_params(32),
    )(qh, kh, kh, vb, vb)
    return o, lse


def attn_bwd(qh, kh, vb, da, lse_d, delta_d, d, name):
    L = qh.shape[0]
    DA = qh.shape[1] // d
    H = DA // HEAD_DIM
    nb = L // Q_BLOCK
    scale = HEAD_DIM ** -0.5
    view = (L, d * DA)
    resident = d > 1

    def body(q_ref, kc_ref, kp_ref, vc_ref, vp_ref, do_ref, lse_ref, dl_ref, dq_ref, dk_ref, dv_ref, dkc_ref, dvc_ref):
        n = pl.program_id(1)
        mask = _band_mask(n)
        if resident:
            done = pl.ds(pl.multiple_of(jnp.maximum(n - 1, 0) * Q_BLOCK, Q_BLOCK), Q_BLOCK)
            fresh = pl.ds(pl.multiple_of(n * Q_BLOCK, Q_BLOCK), Q_BLOCK)
        else:
            done = slice(None)

        @pl.when(n == 0)
        def _():
            dkc_ref[...] = jnp.zeros_like(dkc_ref)
            dvc_ref[...] = jnp.zeros_like(dvc_ref)

        @pl.when(n < nb)
        def _():
            for h in range(H):
                sl = slice(h * HEAD_DIM, (h + 1) * HEAD_DIM)
                q, do = q_ref[:, sl], do_ref[:, sl]
                keys = jnp.concatenate([kc_ref[:, sl], kp_ref[:, sl]], axis=0)
                vals = jnp.concatenate([vc_ref[:, sl], vp_ref[:, sl]], axis=0)
                lse = jnp.broadcast_to(lse_ref[0, :, h:h + 1], (Q_BLOCK, 2 * Q_BLOCK))
                dl = jnp.broadcast_to(dl_ref[0, :, h:h + 1], (Q_BLOCK, 2 * Q_BLOCK))
                s = lax.dot_general(q, keys, NT, preferred_element_type=F32) * scale
                p = jnp.exp(jnp.where(mask, s, NEG) - lse)
                dp = lax.dot_general(do, vals, NT, preferred_element_type=F32)
                ds = (p * (dp - dl) * scale).astype(BF16)
                dq_ref[:, sl] = jnp.dot(ds, keys, preferred_element_type=F32).astype(BF16)
                dk = lax.dot_general(ds, q, TN, preferred_element_type=F32)
                dv = lax.dot_general(p.astype(BF16), do, TN, preferred_element_type=F32)
                dk_ref[done, sl] = (dkc_ref[:, sl] + dk[Q_BLOCK:]).astype(BF16)
                dv_ref[done, sl] = (dvc_ref[:, sl] + dv[Q_BLOCK:]).astype(BF16)
                dkc_ref[:, sl] = dk[:Q_BLOCK]
                dvc_ref[:, sl] = dv[:Q_BLOCK]
                if resident:
                    dk_ref[fresh, sl] = dk[:Q_BLOCK].astype(BF16)
                    dv_ref[fresh, sl] = dv[:Q_BLOCK].astype(BF16)

        if not resident:
            @pl.when(n == nb)
            def _():
                dk_ref[...] = dkc_ref[...].astype(BF16)
                dv_ref[...] = dvc_ref[...].astype(BF16)

    steps = nb if resident else nb + 1
    cur = pl.BlockSpec((Q_BLOCK, DA), lambda r, n: (jnp.minimum(n, nb - 1), r))
    prev = pl.BlockSpec((Q_BLOCK, DA), lambda r, n: (jnp.clip(n - 1, 0, nb - 1), r))
    if resident:
        keyside = pl.BlockSpec((L, DA), lambda r, n: (0, r))
    else:
        keyside = pl.BlockSpec((Q_BLOCK, DA), lambda r, n: (jnp.maximum(n - 1, 0), r))
    stat = pl.BlockSpec((1, Q_BLOCK, H), lambda r, n: (r, jnp.minimum(n, nb - 1), 0))
    dq, dk, dv = _call(
        body, name=name, grid=(d, steps), in_specs=[cur, cur, prev, cur, prev, cur, stat, stat],
        out_specs=[cur, keyside, keyside], out_shape=[SDS(view, BF16)] * 3,
        scratch_shapes=[pltpu.VMEM((Q_BLOCK, DA), F32), pltpu.VMEM((Q_BLOCK, DA), F32)],
        compiler_params=_params(40),
    )(qh, kh, kh, vb, vb, da, lse_d, delta_d)
    return dq, dk, dv


def _to_branch(stat, d):
    S, H = stat.shape
    return stat.reshape(S // d, d, H).transpose(1, 0, 2)


def _from_branch(stat):
    d, L, H = stat.shape
    return stat.transpose(1, 0, 2).reshape(L * d, H)


def att_combine(os_, lses, proj, gain, name):
    S, DA = os_[0].shape
    H = DA // HEAD_DIM
    tm = _tile(S, 256, 16 * DILATIONS[-1])
    nb = len(os_)

    def body(*refs):
        o_views, l_refs = refs[:nb], refs[nb:2 * nb]
        gate_ref, gain_ref, att_ref, y_ref, lse_ref = refs[2 * nb:2 * nb + 5]
        bufs = refs[2 * nb + 5:]
        for d, view, buf in zip(DILATIONS[1:], o_views[1:], bufs):
            for h in range(H):
                for r in range(d):
                    buf.at[h][pl.ds(r, tm // d, stride=d), :] = view[:, r * DA + h * HEAD_DIM:r * DA + (h + 1) * HEAD_DIM]
        ls = [r[...] for r in l_refs]
        top = ls[0]
        for l in ls[1:]:
            top = jnp.maximum(top, l)
        den = jnp.exp(ls[0] - top)
        for l in ls[1:]:
            den = den + jnp.exp(l - top)
        lse = top + jnp.log(den)
        lse_ref[...] = lse
        ws = [jnp.exp(l - lse) for l in ls]
        for h in range(H):
            sl = slice(h * HEAD_DIM, (h + 1) * HEAD_DIM)
            acc = ws[0][:, h:h + 1] * o_views[0][:, sl]
            for w, buf in zip(ws[1:], bufs):
                acc = acc + w[:, h:h + 1] * buf[h]
            att_ref[:, sl] = acc
        a = att_ref[...]
        g = gate_ref[...]
        y_ref[...] = (a * _rstd(a) * gain_ref[...] * (g * _sig(g))).astype(BF16)

    specs, _ = _dil_specs(S, DA, tm, F32)
    return _call(
        body, name=name, grid=(S // tm,),
        in_specs=specs + [_row(tm, H)] * nb + [_row(tm, DA, 3), _full((1, DA))],
        out_specs=[_row(tm, DA), _row(tm, DA), _row(tm, H)],
        out_shape=[SDS((S, DA), F32), SDS((S, 2 * DA), BF16), SDS((S, H), F32)],
        scratch_shapes=[_head_buf(tm, DA)] * (nb - 1),
        compiler_params=_params(48),
    )(*os_, *lses, proj, gain)


def gate_bwd(dcat, cblk, a, proj, gate_blk, gain, dilated, dproj, name):
    S, DA = a.shape
    H = DA // HEAD_DIM
    tm = _tile(S, 256, 16 * DILATIONS[-1])
    nd = len(DILATIONS) if dilated else 1

    def body(dy_ref, a_ref, gate_ref, gain_ref, *rest):
        dg_ref, gacc_ref, *rest = rest[0 if dproj is None else 1:]

        @pl.when(pl.program_id(0) == 0)
        def _():
            gacc_ref[...] = jnp.zeros_like(gacc_ref)

        dy, av, g, gain_v = dy_ref[...], a_ref[...], gate_ref[...], gain_ref[...]
        r = _rstd(av)
        n = av * r
        sg = _sig(g)
        dg_ref[...] = (dy * (n * gain_v) * (sg * (1.0 + g * (1.0 - sg)))).astype(BF16)
        drn = dy * (g * sg)
        gacc_ref[...] += jnp.sum(drn * n, axis=0, keepdims=True)
        dn = drn * gain_v
        da = r * (dn - n * jnp.mean(dn * n, axis=-1, keepdims=True))
        if dilated:
            da_refs, delta_ref, buf_ref = rest[:nd], rest[nd], rest[nd + 1]
            for h in range(H):
                buf_ref[h] = da[:, h * HEAD_DIM:(h + 1) * HEAD_DIM]
            _emit_dilated(buf_ref, da_refs, tm, DA)
            prod = da * av
            for h in range(H):
                delta_ref[:, h:h + 1] = jnp.sum(prod[:, h * HEAD_DIM:(h + 1) * HEAD_DIM], axis=-1, keepdims=True)
        else:
            rest[0][...] = da.astype(BF16)

    out_specs = [_row(tm, DA, gate_blk), _full((1, DA))]
    out_shape = [SDS((S, 7 * DA), BF16), SDS((1, DA), F32)]
    scratch = []
    carried = [] if dproj is None else [dproj]
    if dilated:
        specs, shapes = _dil_specs(S, DA, tm, BF16)
        out_specs += specs + [_row(tm, H)]
        out_shape += shapes + [SDS((S, H), F32)]
        scratch = [_head_buf(tm, DA)]
    else:
        out_specs.append(_row(tm, DA))
        out_shape.append(SDS((S, DA), BF16))
    return _call(
        body, name=name, grid=(S // tm,),
        in_specs=[_row(tm, DA, cblk), _row(tm, DA), _row(tm, DA, gate_blk), _full((1, DA))] + [pl.BlockSpec(memory_space=pl.ANY)] * len(carried),
        out_specs=out_specs, out_shape=out_shape, scratch_shapes=scratch, compiler_params=_params(48),
        input_output_aliases={4: 0} if carried else {},
    )(dcat, a, proj, gain, *carried)


def _by_sublane_phase(offsets):
    groups = [(p, [o for o in offsets if o % 8 == p]) for p in range(8)]
    return [(p, sorted(os_)) for p, os_ in groups if os_]


def _shifted_rows(src_ref, tmp_ref, base, phase, offsets, lanes):
    if phase == 0:
        return lambda o: src_ref[pl.ds(base + o, ROW_CHUNK), lanes]
    span = offsets[-1] - phase + ROW_CHUNK
    tmp_ref[phase, pl.ds(0, span), :] = src_ref[pl.ds(base + phase, span), lanes]
    return lambda o: tmp_ref[phase, pl.ds(o - phase, ROW_CHUNK), :]


def _shift_scratch():
    return pltpu.VMEM((8, CONV_PAD + ROW_CHUNK, LANE), F32)


def _fill_u(i, a_ref, b_ref, ah_ref, bh_ref, uext_ref, tm):
    uext_ref[pl.ds(CONV_PAD, tm), :] = a_ref[...] * _sig(b_ref[...])
    uh = ah_ref[...] * _sig(bh_ref[...])
    uext_ref[pl.ds(0, CONV_PAD), :] = jnp.where(i > 0, uh, 0.0)


def conv_fwd(proj, wk, bias, ln_g, ln_b, out_g, wpw, cat, name):
    S = proj.shape[0]
    DC = proj.shape[1] // 7
    tm = _tile(S, 128, ROW_CHUNK)
    lead = CONV_PAD - (CONV_WIDTH - 1)

    def body(a_ref, b_ref, ah_ref, bh_ref, gate_ref, wk_ref, bias_ref, lg_ref, lb_ref, og_ref, wpw_ref, cat_ref,
             cy_ref, z_ref, conv_ref, y_ref, uext_ref, tmp_ref):
        _fill_u(pl.program_id(0), a_ref, b_ref, ah_ref, bh_ref, uext_ref, tm)

        def cols(cc, carry):
            c0 = pl.multiple_of(cc * LANE, LANE)
            lanes = pl.ds(c0, LANE)
            for rr in range(tm // ROW_CHUNK):
                acc = jnp.broadcast_to(bias_ref[:, lanes], (ROW_CHUNK, LANE))
                for phase, offsets in _by_sublane_phase(range(lead, lead + CONV_WIDTH)):
                    rows = _shifted_rows(uext_ref, tmp_ref, rr * ROW_CHUNK, phase, offsets, lanes)
                    for o in offsets:
                        acc = acc + wk_ref[o - lead:o - lead + 1, lanes] * rows(o)
                y_ref[pl.ds(rr * ROW_CHUNK, ROW_CHUNK), lanes] = acc
            return carry

        lax.fori_loop(0, DC // LANE, cols, 0)
        y = y_ref[...]
        yc = y - jnp.mean(y, axis=-1, keepdims=True)
        ln = yc * _rstd(yc) * lg_ref[...] + lb_ref[...]
        zb = (ln * _sig(ln)).astype(BF16)
        z_ref[...] = zb
        conv = jnp.dot(zb, wpw_ref[...], preferred_element_type=F32)
        conv_ref[...] = conv
        g = gate_ref[...]
        cy_ref[...] = (conv * _rstd(conv) * og_ref[...] * (g * _sig(g))).astype(BF16)

    vec = _full((1, DC))
    return _call(
        body, name=name, grid=(S // tm,),
        in_specs=[_row(tm, DC, 4), _row(tm, DC, 5), _halo_prev(tm, DC, 4), _halo_prev(tm, DC, 5), _row(tm, DC, 6),
                  _full((CONV_PAD, DC)), vec, vec, vec, vec, _full((DC, DC)), pl.BlockSpec(memory_space=pl.ANY)],
        out_specs=[_row(tm, DC, 1)] + [_row(tm, DC)] * 3,
        out_shape=[SDS((S, 2 * DC), BF16), SDS((S, DC), BF16), SDS((S, DC), F32), SDS((S, DC), F32)],
        input_output_aliases={11: 0},
        scratch_shapes=[pltpu.VMEM((CONV_PAD + tm, DC), F32), _shift_scratch()],
        compiler_params=_params(48),
    )(proj, proj, proj, proj, proj, wk, bias, ln_g, ln_b, out_g, wpw, cat)


def conv_bwd_ln(dconv, wpw, y, ln_g, ln_b, name):
    S, DC = y.shape
    tm = _tile(S, 256, 8)

    def body(dc_ref, wpw_ref, y_ref, lg_ref, lb_ref, dy_ref, glg_ref, glb_ref, gb_ref):
        @pl.when(pl.program_id(0) == 0)
        def _():
            glg_ref[...] = jnp.zeros_like(glg_ref)
            glb_ref[...] = jnp.zeros_like(glb_ref)
            gb_ref[...] = jnp.zeros_like(gb_ref)

        dz = lax.dot_general(dc_ref[...], wpw_ref[...], NT, preferred_element_type=F32)
        yv = y_ref[...]
        yc = yv - jnp.mean(yv, axis=-1, keepdims=True)
        rstd = _rstd(yc)
        yhat = yc * rstd
        ln = yhat * lg_ref[...] + lb_ref[...]
        sg = _sig(ln)
        dln = dz * (sg * (1.0 + ln * (1.0 - sg)))
        glb_ref[...] += jnp.sum(dln, axis=0, keepdims=True)
        glg_ref[...] += jnp.sum(dln * yhat, axis=0, keepdims=True)
        dyh = dln * lg_ref[...]
        dy = rstd * (dyh - jnp.mean(dyh, axis=-1, keepdims=True) - yhat * jnp.mean(dyh * yhat, axis=-1, keepdims=True))
        dy_ref[...] = dy
        gb_ref[...] += jnp.sum(dy, axis=0, keepdims=True)

    vec = _full((1, DC))
    return _call(
        body, name=name, grid=(S // tm,),
        in_specs=[_row(tm, DC), _full((DC, DC)), _row(tm, DC), vec, vec],
        out_specs=[_row(tm, DC), vec, vec, vec],
        out_shape=[SDS((S, DC), F32)] + [SDS((1, DC), F32)] * 3,
        compiler_params=_params(48),
    )(dconv, wpw, y, ln_g, ln_b)


def conv_bwd_dw(dy, proj, wk, dproj, name):
    S, DC = dy.shape
    tm = _tile(S, 128, ROW_CHUNK)
    nsteps = S // tm
    lead = CONV_PAD - (CONV_WIDTH - 1)
    groups = ROW_CHUNK // 8

    def body(dy_ref, dyn_ref, a_ref, b_ref, ah_ref, bh_ref, wk_ref, dproj_ref, dab_ref, gw_ref, uext_ref, dyext_ref, du_ref,
             tmp_dy_ref, tmp_u_ref):
        i = pl.program_id(0)

        @pl.when(i == 0)
        def _():
            gw_ref[...] = jnp.zeros_like(gw_ref)

        _fill_u(i, a_ref, b_ref, ah_ref, bh_ref, uext_ref, tm)
        dyext_ref[pl.ds(0, tm), :] = dy_ref[...]
        dyext_ref[pl.ds(tm, CONV_PAD), :] = jnp.where(i < nsteps - 1, dyn_ref[...], 0.0)

        def cols(cc, carry):
            c0 = pl.multiple_of(cc * LANE, LANE)
            lanes = pl.ds(c0, LANE)
            for rr in range(tm // ROW_CHUNK):
                base = rr * ROW_CHUNK
                acc = jnp.zeros((ROW_CHUNK, LANE), F32)
                for phase, offsets in _by_sublane_phase(range(CONV_WIDTH)):
                    rows = _shifted_rows(dyext_ref, tmp_dy_ref, base, phase, offsets, lanes)
                    for o in offsets:
                        j = CONV_WIDTH - 1 - o
                        acc = acc + wk_ref[j:j + 1, lanes] * rows(o)
                du_ref[pl.ds(base, ROW_CHUNK), lanes] = acc
                dyc = dyext_ref[pl.ds(base, ROW_CHUNK), lanes]
                for phase, offsets in _by_sublane_phase(range(lead, lead + CONV_WIDTH)):
                    rows = _shifted_rows(uext_ref, tmp_u_ref, base, phase, offsets, lanes)
                    for o in offsets:
                        prod = dyc * rows(o)
                        part = prod[0:8]
                        for k in range(1, groups):
                            part = part + prod[8 * k:8 * k + 8]
                        gw_ref[o - lead, :, lanes] += part
            return carry

        lax.fori_loop(0, DC // LANE, cols, 0)
        du = du_ref[...]
        sb = _sig(b_ref[...])
        dab_ref[:, :DC] = (du * sb).astype(BF16)
        dab_ref[:, DC:] = (du * a_ref[...] * sb * (1.0 - sb)).astype(BF16)

    return _call(
        body, name=name, grid=(nsteps,),
        in_specs=[_row(tm, DC), _halo_next(tm, DC, S // CONV_PAD), _row(tm, DC, 4), _row(tm, DC, 5),
                  _halo_prev(tm, DC, 4), _halo_prev(tm, DC, 5), _full((CONV_PAD, DC)), pl.BlockSpec(memory_space=pl.ANY)],
        out_specs=[_row(tm, 2 * DC, 2), _full((CONV_PAD, 8, DC))],
        out_shape=[SDS((S, 7 * DC), BF16), SDS((CONV_PAD, 8, DC), F32)],
        input_output_aliases={7: 0},
        scratch_shapes=[pltpu.VMEM((CONV_PAD + tm, DC), F32), pltpu.VMEM((tm + CONV_PAD, DC), F32), pltpu.VMEM((tm, DC), F32),
                        _shift_scratch(), _shift_scratch()],
        compiler_params=_params(40),
    )(dy, dy, proj, proj, proj, proj, wk, dproj)


def loss_head(xo, target, name):
    S, D = xo.shape
    tm = _tile(S, 256, 8)

    def body(x_ref, t_ref, dy_ref, dyb_ref, acc_ref):
        @pl.when(pl.program_id(0) == 0)
        def _():
            acc_ref[...] = jnp.zeros_like(acc_ref)

        err = x_ref[...] - t_ref[...]
        dy = err * (1.0 / D)
        dy_ref[...] = dy
        dyb_ref[...] = dy.astype(BF16)
        acc_ref[...] += jnp.sum(err * dy, axis=0, keepdims=True) * 0.5

    return _call(
        body, name=name, grid=(S // tm,), in_specs=[_row(tm, D), _row(tm, D)],
        out_specs=[_row(tm, D), _row(tm, D), _full((1, D))],
        out_shape=[SDS((S, D), F32), SDS((S, D), BF16), SDS((1, D), F32)],
        compiler_params=_params(32),
    )(xo, target)


def _coords():
    x, y, c = lax.axis_index("x"), lax.axis_index("y"), lax.axis_index("c")
    return x, y, c


def _lin(p):
    return 4 * p[0] + 2 * p[1] + p[2]


def _slot(ref, axis, idx, size):
    index = [slice(None)] * len(ref.shape)
    index[axis] = pl.ds(idx * size, size)
    return ref.at[tuple(index)]


def all_gather(blocks, axes, name):
    na = len(blocks)
    sizes = [b.shape[ax] for b, ax in zip(blocks, axes)]
    fulls = [SDS(b.shape[:ax] + (N_DEV * b.shape[ax],) + b.shape[ax + 1:], b.dtype) for b, ax in zip(blocks, axes)]

    def body(*refs):
        in_refs, out_refs = refs[:na], refs[na:2 * na]
        send_sems, recv_sems, local_sems = refs[2 * na:]
        x, y, c = _coords()
        me, sibling = (x, y, c), (x, y, 1 - c)
        chips = [(1 - x, y), (x, 1 - y), (1 - x, 1 - y)]

        def place(a, p):
            return _slot(out_refs[a], axes[a], _lin(p), sizes[a])

        def copy(a, k, block, to, src=None):
            return pltpu.make_async_remote_copy(
                src_ref=place(a, block) if src is None else src, dst_ref=place(a, block),
                send_sem=send_sems.at[a, k], recv_sem=recv_sems.at[a, k], device_id=to, device_id_type=MESH)

        mine = [pltpu.make_async_copy(in_refs[a], place(a, me), local_sems.at[a]) for a in range(na)]
        for cp in mine:
            cp.start()
        first = []
        for a in range(na):
            first.append(copy(a, 0, me, sibling, src=in_refs[a]))
            first += [copy(a, 1 + j, me, (*chip, c), src=in_refs[a]) for j, chip in enumerate(chips)]
        for cp in first:
            cp.start()
        passed = []
        for j, chip in enumerate(chips):
            for a in range(na):
                copy(a, 1 + j, (*chip, c), me).wait_recv()
                cp = copy(a, 4 + j, (*chip, c), sibling)
                cp.start()
                passed.append(cp)
        for a in range(na):
            copy(a, 0, sibling, me).wait_recv()
            for j, chip in enumerate(chips):
                copy(a, 4 + j, (*chip, 1 - c), me).wait_recv()
        for cp in first + passed:
            cp.wait_send()
        for cp in mine:
            cp.wait()

    hbm = pl.BlockSpec(memory_space=pltpu.HBM)
    return _call(
        body, name=name, in_specs=[hbm] * na, out_specs=[hbm] * na, out_shape=fulls,
        scratch_shapes=[pltpu.SemaphoreType.DMA((na, 7)), pltpu.SemaphoreType.DMA((na, 7)), pltpu.SemaphoreType.DMA((na,))],
    )(*blocks)


class _Exchange:
    def __init__(self, gather, srcs, axes, name, after, route="all", lands=None):
        self.gather, self.axes, self.name, self.route = gather, axes, name, route
        if route == "pass":
            self.na, self.ns = len(lands), 0
            self.sizes = [l.shape[ax] // N_DEV for l, ax in zip(lands, axes)]
            self.kinds = [pltpu.HBM(l.shape, l.dtype) for l in lands]
            self._start([pltpu.with_memory_space_constraint(t, pltpu.HBM) for t in lands], after)
            return
        self.na = self.ns = len(srcs)
        if gather:
            self.sizes = [s.shape[ax] for s, ax in zip(srcs, axes)]
            lands = [s.shape[:ax] + (N_DEV * s.shape[ax],) + s.shape[ax + 1:] for s, ax in zip(srcs, axes)]
        else:
            self.sizes = [None if ax is None else s.shape[ax] // N_DEV for s, ax in zip(srcs, axes)]
            lands = [s.shape if ax is None else (N_DEV,) + s.shape[:ax] + (sz,) + s.shape[ax + 1:]
                     for s, ax, sz in zip(srcs, axes, self.sizes)]
        self.kinds = [pltpu.HBM(s.shape, s.dtype) for s in srcs] + [pltpu.HBM(l, s.dtype) for l, s in zip(lands, srcs)]
        lands = [lax.empty(l, s.dtype) for l, s in zip(lands, srcs)]
        after = jnp.zeros((8, LANE), F32) if after is None else after
        self._start([pltpu.with_memory_space_constraint(t, pltpu.HBM) for t in list(srcs) + lands], after)

    def _src(self, a, ref, owner):
        if self.gather:
            return ref
        return ref.at[_lin(owner)] if self.axes[a] is None else _slot(ref, self.axes[a], _lin(owner), self.sizes[a])

    def _dst(self, a, land, sender):
        return _slot(land, self.axes[a], _lin(sender), self.sizes[a]) if self.gather else land.at[_lin(sender)]

    def _flips(self):
        if self.route == "all":
            return [(k >> 2 & 1, k >> 1 & 1, k & 1) for k in range(1, N_DEV)]
        if self.route == "chips":
            return [(0, 0, 1), (1, 0, 0), (0, 1, 0), (1, 1, 0)]
        return [(1, 0, 0), (0, 1, 0), (1, 1, 0)]

    def _copies(self, refs, send_sems, recv_sems):
        na, ns = self.na, self.ns
        me = _coords()
        flips = self._flips()
        n = len(flips)
        others = [tuple(1 - v if f else v for v, f in zip(me, flip)) for flip in flips]
        sends, arrivals = [], []
        for a in range(na):
            land = refs[ns + a]
            for k, other in enumerate(others):
                if self.route == "pass":
                    peer = (me[0], me[1], 1 - me[2])
                    theirs = (other[0], other[1], 1 - me[2])
                    send = dict(src_ref=self._dst(a, land, other), dst_ref=self._dst(a, land, other))
                    arrive = dict(src_ref=self._dst(a, land, theirs), dst_ref=self._dst(a, land, theirs))
                else:
                    peer = other
                    send = dict(src_ref=self._src(a, refs[a], peer), dst_ref=self._dst(a, land, me))
                    arrive = dict(src_ref=self._src(a, refs[a], me), dst_ref=self._dst(a, land, peer))
                pair = dict(send_sem=send_sems.at[n * a + k], recv_sem=recv_sems.at[n * a + k], device_id=peer, device_id_type=MESH)
                sends.append(pltpu.make_async_remote_copy(**send, **pair))
                arrivals.append(pltpu.make_async_remote_copy(**arrive, **pair))
        return sends, arrivals

    def _place_own(self, operands):
        na = self.na
        me = jnp.reshape(_lin(_coords()), (1,)).astype(jnp.int32)
        lands = []
        for a in range(na):
            src, land, ax = operands[a], operands[na + a], self.axes[a]
            if ax == 1:
                R, C = src.shape[0], self.sizes[a]
                steps, tile = 1, (R, C)
                in_map = lambda i, me_ref: (0, me_ref[0])
            elif ax is None:
                R, C = src.shape[1:]
                tr = _tile(R, 512, 16)
                steps, tile = R // tr, (None, tr, C)
                in_map = lambda i, me_ref: (me_ref[0], i, 0)
            else:
                R, C = (src.shape[0] if self.gather else self.sizes[a]), src.shape[1]
                tr = _tile(R, 512, 16)
                steps, tile = R // tr, (tr, C)
                in_map = (lambda i, me_ref: (i, 0)) if self.gather else (lambda i, me_ref, n=R // tr: (me_ref[0] * n + i, 0))
            if self.gather:
                out_spec = pl.BlockSpec(tile, lambda i, me_ref, n=steps: (me_ref[0] * n + i, 0))
            else:
                out_spec = pl.BlockSpec((None,) + tuple(t for t in tile if t is not None), lambda i, me_ref: (me_ref[0], i, 0))

            def body(me_ref, src_ref, land_ref, out_ref):
                out_ref[...] = src_ref[...]

            lands.append(_call(
                body, name=f"{self.name}_own{a}",
                grid_spec=pltpu.PrefetchScalarGridSpec(
                    num_scalar_prefetch=1, grid=(steps,),
                    in_specs=[pl.BlockSpec(tile, in_map), pl.BlockSpec(memory_space=pl.ANY)], out_specs=out_spec),
                out_shape=SDS(land.shape, land.dtype), input_output_aliases={2: 0}, compiler_params=_params(32),
            )(me, src, land))
        return operands[:na] + lands

    def _start(self, operands, after):
        nops = self.ns + self.na
        nsem = len(self._flips()) * self.na
        if self.ns:
            operands = self._place_own(operands)

        def body(*refs):
            ins = refs[:nops]
            send_sems, recv_sems, token_ref = refs[nops + 1], refs[nops + 2], refs[2 * nops + 3]
            for cp in self._copies(ins, send_sems, recv_sems)[0]:
                cp.start()
            token_ref[...] = jnp.zeros_like(token_ref)

        hbm = pl.BlockSpec(memory_space=pltpu.HBM)
        sem = pl.BlockSpec(memory_space=pltpu.SEMAPHORE)
        outs = _call(
            body, name=self.name + "_start",
            in_specs=[hbm] * nops + [pl.BlockSpec(memory_space=pl.ANY)],
            out_specs=[sem, sem] + [hbm] * nops + [pl.BlockSpec(memory_space=pltpu.VMEM)],
            out_shape=[pltpu.SemaphoreType.DMA((nsem,)), pltpu.SemaphoreType.DMA((nsem,))] + self.kinds + [SDS((8, LANE), F32)],
            input_output_aliases={i: 2 + i for i in range(nops)},
            compiler_params=pltpu.CompilerParams(has_side_effects=pltpu.SideEffectType.DATAFLOW_SIDE_EFFECTING),
        )(*operands, after)
        self.sems, self.thru, self.token = outs[:2], outs[2:2 + nops], outs[2 + nops][0:1, 0:1]

    def wait(self, after):
        nops = self.ns + self.na

        def body(*refs):
            ins, send_sems, recv_sems = refs[:nops], refs[nops], refs[nops + 1]
            sends, arrivals = self._copies(ins, send_sems, recv_sems)
            for cp in sends:
                cp.wait_send()
            for cp in arrivals:
                cp.wait_recv()

        hbm = pl.BlockSpec(memory_space=pltpu.HBM)
        sem = pl.BlockSpec(memory_space=pltpu.SEMAPHORE)
        outs = _call(
            body, name=self.name + "_wait",
            in_specs=[hbm] * nops + [sem, sem, pl.BlockSpec(memory_space=pl.ANY)],
            out_specs=[hbm] * nops, out_shape=self.kinds,
            input_output_aliases={i: i for i in range(nops)},
            compiler_params=pltpu.CompilerParams(has_side_effects=pltpu.SideEffectType.DATAFLOW_SIDE_EFFECTING),
        )(*self.thru, *self.sems, after)
        return outs[self.ns:]


def adamw(w, m, v, parts, layer, prev, name):
    nl, R, C = w.shape
    tr = _tile(R, 128, 8) if R % 8 == 0 else R

    def body(w_ref, m_ref, v_ref, p_ref, *rest):
        g_ref, d_ref, mo_ref, vo_ref = rest[-4:]
        g = p_ref[0].astype(F32)
        for s in range(1, N_DEV):
            g = g + p_ref[s].astype(F32)
        mn = ADAM_B1 * m_ref[0] + (1.0 - ADAM_B1) * g
        vn = ADAM_B2 * v_ref[0] + (1.0 - ADAM_B2) * (g * g)
        m_hat = mn / (1.0 - ADAM_B1 ** ADAM_STEP)
        v_hat = vn / (1.0 - ADAM_B2 ** ADAM_STEP)
        g_ref[0] = g
        d_ref[0] = -ADAM_LR * (m_hat / (jnp.sqrt(v_hat) + ADAM_EPS) + ADAM_WD * w_ref[0])
        mo_ref[0] = mn
        vo_ref[0] = vn

    row = pl.BlockSpec((1, tr, C), lambda i: (layer, i, 0))
    carried = [] if prev is None else list(prev)
    return _call(
        body, name=name, grid=(R // tr,),
        in_specs=[row, row, row, pl.BlockSpec((N_DEV, tr, C), lambda i: (0, i, 0))] + [pl.BlockSpec(memory_space=pl.ANY)] * len(carried),
        out_specs=[row] * 4, out_shape=[SDS((nl, R, C), F32)] * 4,
        input_output_aliases={4 + k: k for k in range(len(carried))},
        compiler_params=_params(48),
    )(w, m, v, parts, *carried)


def _rope_tables(S):
    inv_freq = 1.0 / (ROPE_THETA ** (jnp.arange(0, HEAD_DIM, 2, dtype=F32) / HEAD_DIM))
    ang = jnp.arange(S, dtype=F32)[:, None] * inv_freq[None, :]
    cos, sin = jnp.cos(ang), jnp.sin(ang)
    return jnp.concatenate([cos, cos], axis=-1), jnp.concatenate([-sin, sin], axis=-1)


def _pack_small(D, norm_g, dw_bias, conv_ln_g, conv_ln_b, att_out_g, conv_out_g, q_norm_g, k_norm_g, extra=None):
    qk = jnp.concatenate([q_norm_g.reshape(-1), k_norm_g.reshape(-1)])
    qk = jnp.pad(qk, (0, D - qk.shape[0])).reshape(1, D)
    zero = jnp.zeros((1, D), F32)
    return jnp.concatenate([norm_g, dw_bias, conv_ln_g, conv_ln_b, att_out_g, conv_out_g, qk, zero,
                            zero if extra is None else extra, zero], axis=0)


def _unpack_small(p):
    rows = [p[2 * i:2 * i + 2] for i in range(6)]
    qk = p[12, :4 * HEAD_DIM].reshape(2, DEPTH, HEAD_DIM)
    return rows + [qk[0], qk[1]]


def kernel(x, norm_g, w_in, q_norm_g, k_norm_g, dw_kernel, dw_bias, conv_ln_g, conv_ln_b, w_pw, att_out_g, conv_out_g, w_out, loss_target, m_norm_g, m_w_in, m_q_norm_g, m_k_norm_g, m_dw_kernel, m_dw_bias, m_conv_ln_g, m_conv_ln_b, m_w_pw, m_att_out_g, m_conv_out_g, m_w_out, v_norm_g, v_w_in, v_q_norm_g, v_k_norm_g, v_dw_kernel, v_dw_bias, v_conv_ln_g, v_conv_ln_b, v_w_pw, v_att_out_g, v_conv_out_g, v_w_out):
    xs = x[0]
    D = xs.shape[1]
    bf = lambda t, l: t[l].astype(BF16)
    wint0, dwk_f = all_gather([bf(w_in, 0).T, dw_kernel], [0, 2], "gather_first")
    early = _Exchange(True, [bf(w_pw, 0), bf(w_out, 0)], [0, 0], "gather_layer0", after=wint0)
    later = _Exchange(True, [bf(w_in, 1).T, bf(w_pw, 1), bf(w_out, 1)], [0, 0, 0], "gather_layer1", after=early.token,
                      route="chips")
    landed = {}

    def weights(l, cur):
        if l == 0:
            return wint0, later.token
        landed[1] = landed["passing"].wait(cur)
        return landed[1][0], None

    def mixer_weights(l, cur):
        if l == 0:
            mine = early.wait(cur)
            landed["passing"] = _Exchange(True, None, [0, 0, 0], "pass_layer1", after=mine[0], route="pass",
                                          lands=later.wait(mine[0]))
            return (*mine, landed["passing"].token)
        return (*landed[1][1:], None)

    sent = [[] for _ in range(DEPTH)]

    def send_grads(l, grads, axes, tag):
        sent[l].append(_Exchange(False, grads, axes, f"scatter_{tag}_layer{l}", after=None))
        return sent[l][-1].token

    dx, loss_cols, small = local_step(xs, loss_target[0], weights, mixer_weights, dwk_f, norm_g, q_norm_g, k_norm_g, dw_bias,
                                      conv_ln_g, conv_ln_b, att_out_g, conv_out_g, send_grads, N_DEV)

    big = ((w_pw, m_w_pw, v_w_pw), (w_out, m_w_out, v_w_out), (dw_kernel, m_dw_kernel, v_dw_kernel), (w_in, m_w_in, v_w_in))
    results = [None] * len(big)
    after = dx
    for l in reversed(range(DEPTH)):
        parts = [p for ex in sent[l] for p in ex.wait(after)]
        for i, ((w, m, v), p) in enumerate(zip(big, parts)):
            results[i] = adamw(w, m, v, p, l, results[i], f"adamw_{w.shape[1]}_{w.shape[2]}_{l}")
        after = results[0][3]
    r_wpw, r_wout, r_dwk, r_win = results

    stack = lambda k: jnp.concatenate(small[k], axis=0)
    mine = _pack_small(D, stack("norm_g"), stack("dw_bias"), stack("conv_ln_g"), stack("conv_ln_b"), stack("att_out_g"),
                       stack("conv_out_g"), stack("q"), stack("k"), extra=loss_cols)
    (p_small,) = all_gather([mine], [0], "gather_small")
    pk = lambda n, dw, lg, lb, ao, co, q, k: _pack_small(D, n, dw, lg, lb, ao, co, q, k)[None]
    r_small = adamw(pk(norm_g, dw_bias, conv_ln_g, conv_ln_b, att_out_g, conv_out_g, q_norm_g, k_norm_g),
                    pk(m_norm_g, m_dw_bias, m_conv_ln_g, m_conv_ln_b, m_att_out_g, m_conv_out_g, m_q_norm_g, m_k_norm_g),
                    pk(v_norm_g, v_dw_bias, v_conv_ln_g, v_conv_ln_b, v_att_out_g, v_conv_out_g, v_q_norm_g, v_k_norm_g),
                    p_small.reshape(N_DEV, 16, D), 0, None, "adamw_small")
    r_small = [r[0] for r in r_small]
    loss = jnp.sum(r_small[0][14])

    outs = [loss, dx[None]]
    for i in range(4):
        n_, dwb, lg, lb, ao, co, q_, k_ = _unpack_small(r_small[i])
        outs += [n_, r_win[i], q_, k_, r_dwk[i], dwb, lg, lb, r_wpw[i], ao, co, r_wout[i]]
    return tuple(outs)


def local_step(xs, target, weights, mixer_weights, dwk_f, norm_g, q_norm_g, k_norm_g, dw_bias, conv_ln_g, conv_ln_b,
               att_out_g, conv_out_g, send_grads, owners):
    S, D = xs.shape
    cos2, sin2 = _rope_tables(S)
    dwk_f = jnp.pad(dwk_f, ((0, 0), (0, CONV_PAD - CONV_WIDTH), (0, 0)))

    def vec(p, l, zero=None):
        row = p[l].reshape(1, -1)
        return row if zero is None else row + zero

    saved = []
    cur = xs
    for l in range(DEPTH):
        wint, zero = weights(l, cur)
        proj, h = in_proj(cur, vec(norm_g, l, zero), wint, f"in_proj_{l}")
        qs, ks, vs = qk_prep(proj, vec(q_norm_g, l), vec(k_norm_g, l), cos2, sin2, f"qk_prep_{l}")
        os_, lses = [], []
        for i, d in enumerate(DILATIONS):
            o, lse = attn_fwd(qs[i], ks[i], vs[i], d, f"attn_fwd_{l}_d{d}")
            os_.append(o)
            lses.append(_from_branch(lse))
        att, cat, lse = att_combine(os_, lses, proj, vec(att_out_g, l), f"att_combine_{l}")
        wpw, wout, zero = mixer_weights(l, lse)
        cat, z, conv, y = conv_fwd(proj, dwk_f[l], vec(dw_bias, l, zero), vec(conv_ln_g, l), vec(conv_ln_b, l),
                                   vec(conv_out_g, l), wpw, cat, f"conv_fwd_{l}")
        nxt = mm_nn(cat, wout, F32, f"out_proj_{l}", add=cur)
        saved.append(dict(x=cur, proj=proj, h=h, qs=qs, ks=ks, vs=vs, att=att, lse=lse, cat=cat, z=z, conv=conv, y=y,
                          wint=wint, wpw=wpw, wout=wout))
        cur = nxt

    dx, dxb, loss_cols = loss_head(cur, target, "loss_head")

    small = {k: [None] * DEPTH for k in ("norm_g", "dw_bias", "conv_ln_g", "conv_ln_b", "att_out_g", "conv_out_g", "q", "k")}
    for l in reversed(range(DEPTH)):
        sv = saved[l]
        proj = sv["proj"]
        dcat = mm_nt(dxb, sv["wout"], F32, f"dcat_{l}")
        g_wout = mm_nn(sv["cat"].T, dxb, BF16, f"dwout_{l}")
        dproj, small["conv_out_g"][l], dconv = gate_bwd(dcat, 1, sv["conv"], proj, 6, vec(conv_out_g, l), False, None,
                                                        f"conv_gate_bwd_{l}")
        dy, small["conv_ln_g"][l], small["conv_ln_b"][l], small["dw_bias"][l] = conv_bwd_ln(
            dconv, sv["wpw"], sv["y"], vec(conv_ln_g, l), vec(conv_ln_b, l), f"conv_bwd_ln_{l}")
        g_wpw = mm_nn(sv["z"].T, dconv, BF16, f"dwpw_{l}")
        dproj, gw = conv_bwd_dw(dy, proj, dwk_f[l], dproj, f"conv_bwd_dw_{l}")
        g_dwk = jnp.sum(gw, axis=1)[:CONV_WIDTH]
        zero = send_grads(l, [g_wpw, g_wout, g_dwk], [0, 0, 1], "mixer")
        dproj, small["att_out_g"][l], *datts, delta = gate_bwd(dcat, 0, sv["att"], proj, 3, vec(att_out_g, l, zero), True,
                                                                dproj, f"att_gate_bwd_{l}")
        dqs, dks, dvs = [], [], []
        for i, d in enumerate(DILATIONS):
            dq, dk, dv = attn_bwd(sv["qs"][i], sv["ks"][i], sv["vs"][i], datts[i], _to_branch(sv["lse"], d),
                                  _to_branch(delta, d), d, f"attn_bwd_{l}_d{d}")
            dqs.append(dq)
            dks.append(dk)
            dvs.append(dv)
        dproj, small["q"][l], small["k"][l] = qk_prep_bwd(dqs, dks, dvs, proj, vec(q_norm_g, l), vec(k_norm_g, l),
                                                          cos2, sin2, dproj, f"qk_prep_bwd_{l}")
        g_win = mm_nn(sv["h"].T, dproj, BF16, f"dwin_{l}", owners=owners)
        zero = send_grads(l, [g_win], [None], "w_in")
        dx, dxb, small["norm_g"][l] = in_proj_bwd(dproj, sv["wint"], sv["x"], vec(norm_g, l, zero), dx, f"in_proj_bwd_{l}")

    return dx, loss_cols, small
```

```python
import jax
import jax.numpy as jnp
from jax import lax
from jax.experimental import pallas as pl
from jax.experimental.pallas import tpu as pltpu

F32 = jnp.float32
BF16 = jnp.bfloat16
SDS = jax.ShapeDtypeStruct
MESH = pl.DeviceIdType.MESH

N_DEV = 8
DEPTH = 2
HEAD_DIM = 128
CONV_WIDTH = 31
CONV_PAD = 32
DILATIONS = (1, 4, 16)
Q_BLOCK = 128
ROPE_THETA = 10000.0
EPS = 1e-6
NEG = -1e30
ADAM_LR, ADAM_B1, ADAM_B2, ADAM_EPS, ADAM_WD, ADAM_STEP = 0.001, 0.9, 0.999, 1e-08, 0.01, 10
LANE = 128
ROW_CHUNK = 64
MIB = 1 << 20
NT = (((1,), (1,)), ((), ()))
TN = (((0,), (0,)), ((), ()))


def _call(body, **kw):
    return pl.pallas_call(body, **kw)


def _params(vmem_mib):
    return pltpu.CompilerParams(vmem_limit_bytes=vmem_mib * MIB)


def _tile(dim, pref, mult):
    t = min(pref, dim)
    while dim % t or t % mult:
        t -= mult
    return t


def _sig(v):
    return jax.nn.sigmoid(v)


def _rstd(v):
    return lax.rsqrt(jnp.mean(v * v, axis=-1, keepdims=True) + EPS)


def _row(tm, cb, c=0):
    return pl.BlockSpec((tm, cb), lambda i: (i, c))


def _full(shape):
    return pl.BlockSpec(shape, lambda i: (0,) * len(shape))


def _halo_prev(tm, cb, c=0):
    k = tm // CONV_PAD
    return pl.BlockSpec((CONV_PAD, cb), lambda i: (jnp.maximum(i * k - 1, 0), c))


def _halo_next(tm, cb, nblk, c=0):
    k = tm // CONV_PAD
    return pl.BlockSpec((CONV_PAD, cb), lambda i: (jnp.minimum((i + 1) * k, nblk - 1), c))


def in_proj(x, g, wt, name):
    S, D = x.shape
    N = wt.shape[0]
    tm, tn = _tile(S, 512, 8), _tile(N, 1024, LANE)

    def body(x_ref, g_ref, w_ref, o_ref, h_ref):
        @pl.when(pl.program_id(1) == 0)
        def _():
            xf = x_ref[...]
            h_ref[...] = (xf * _rstd(xf) * g_ref[...]).astype(BF16)

        o_ref[...] = lax.dot_general(h_ref[...], w_ref[...], NT, preferred_element_type=F32)

    return _call(
        body, name=name, grid=(S // tm, N // tn),
        in_specs=[pl.BlockSpec((tm, D), lambda i, j: (i, 0)), pl.BlockSpec((1, D), lambda i, j: (0, 0)),
                  pl.BlockSpec((tn, D), lambda i, j: (j, 0))],
        out_specs=[pl.BlockSpec((tm, tn), lambda i, j: (i, j)), pl.BlockSpec((tm, D), lambda i, j: (i, 0))],
        out_shape=[SDS((S, N), F32), SDS((S, D), BF16)],
        compiler_params=_params(40),
    )(x, g, wt)


def mm_nn(a, b, out_dtype, name, add=None, owners=1):
    M, K = a.shape
    N = b.shape[1]
    tm, tn = _tile(M, 1024 if owners > 1 else 512, 8), _tile(N // owners, 1024, LANE)
    per = N // owners // tn

    def body(*refs):
        a_ref, b_ref = refs[0], refs[1]
        o_ref = refs[-1]
        acc = jnp.dot(a_ref[...], b_ref[...], preferred_element_type=F32)
        if add is not None:
            acc = acc + refs[2][...]
        o_ref[...] = acc.astype(out_dtype)

    in_specs = [pl.BlockSpec((tm, K), lambda i, j: (i, 0)), pl.BlockSpec((K, tn), lambda i, j: (0, j))]
    args = [a, b]
    if add is not None:
        in_specs.append(pl.BlockSpec((tm, tn), lambda i, j: (i, j)))
        args.append(add)
    if owners > 1:
        out_spec = pl.BlockSpec((None, tm, tn), lambda i, j: (j // per, i, j % per))
        out_shape = SDS((owners, M, N // owners), out_dtype)
    else:
        out_spec, out_shape = pl.BlockSpec((tm, tn), lambda i, j: (i, j)), SDS((M, N), out_dtype)
    return _call(
        body, name=name, grid=(M // tm, N // tn), in_specs=in_specs, out_specs=out_spec, out_shape=out_shape,
        compiler_params=_params(48),
    )(*args)


def mm_core_blocks(a, b, core, name):
    M, K = a.shape
    blk = b.shape[1] // N_DEV
    tm, tn = _tile(M, 1024, 8), _tile(blk, 1024, LANE)
    per = blk // tn

    def body(core_ref, a_ref, b_ref, o_ref):
        o_ref[...] = jnp.dot(a_ref[...], b_ref[...], preferred_element_type=F32).astype(BF16)

    return _call(
        body, name=name,
        grid_spec=pltpu.PrefetchScalarGridSpec(
            num_scalar_prefetch=1, grid=(M // tm, 4 * per),
            in_specs=[pl.BlockSpec((tm, K), lambda i, j, core_ref: (i, 0)),
                      pl.BlockSpec((K, tn), lambda i, j, core_ref: (0, (2 * (j // per) + core_ref[0]) * per + j % per))],
            out_specs=pl.BlockSpec((None, tm, tn), lambda i, j, core_ref: (j // per, i, j % per))),
        out_shape=SDS((4, M, blk), BF16), compiler_params=_params(48),
    )(core, a, b)


def add_blocks(a, b, name):
    n, R, C = a.shape
    tr = _tile(R, 512, 16)
    spec = pl.BlockSpec((None, tr, C), lambda k, i: (k, i, 0))

    def body(a_ref, b_ref, o_ref):
        o_ref[...] = (a_ref[...].astype(F32) + b_ref[...].astype(F32)).astype(BF16)

    return _call(body, name=name, grid=(n, R // tr), in_specs=[spec, spec], out_specs=spec, out_shape=SDS(a.shape, BF16),
                 compiler_params=_params(32))(a, b)


def mm_nt(a, b, out_dtype, name):
    M, K = a.shape
    N = b.shape[0]
    tm, tn = _tile(M, 512, 8), _tile(N, 1024, LANE)

    def body(a_ref, b_ref, o_ref):
        o_ref[...] = lax.dot_general(a_ref[...], b_ref[...], NT, preferred_element_type=F32).astype(out_dtype)

    return _call(
        body, name=name, grid=(M // tm, N // tn),
        in_specs=[pl.BlockSpec((tm, K), lambda i, j: (i, 0)), pl.BlockSpec((tn, K), lambda i, j: (j, 0))],
        out_specs=pl.BlockSpec((tm, tn), lambda i, j: (i, j)), out_shape=SDS((M, N), out_dtype),
        compiler_params=_params(40),
    )(a, b)


def in_proj_bwd(dproj, wt, x, g, dx_out, name):
    S, K = dproj.shape
    D = wt.shape[1]
    tm, tk = _tile(S, 512, 8), _tile(K, 1024, LANE)
    nk = K // tk

    def body(dp_ref, w_ref, x_ref, g_ref, dxo_ref, dx_ref, dxb_ref, gacc_ref):
        i, k = pl.program_id(0), pl.program_id(1)
        part = jnp.dot(dp_ref[...], w_ref[...], preferred_element_type=F32)

        @pl.when(k == 0)
        def _():
            dx_ref[...] = part

        @pl.when(k > 0)
        def _():
            dx_ref[...] += part

        @pl.when((k == 0) & (i == 0))
        def _():
            gacc_ref[...] = jnp.zeros_like(gacc_ref)

        @pl.when(k == nk - 1)
        def _():
            dh = dx_ref[...]
            xf = x_ref[...]
            r = _rstd(xf)
            n = xf * r
            gacc_ref[...] += jnp.sum(dh * n, axis=0, keepdims=True)
            dn = dh * g_ref[...]
            dx = r * (dn - n * jnp.mean(dn * n, axis=-1, keepdims=True)) + dxo_ref[...]
            dx_ref[...] = dx
            dxb_ref[...] = dx.astype(BF16)

    return _call(
        body, name=name, grid=(S // tm, nk),
        in_specs=[pl.BlockSpec((tm, tk), lambda i, k: (i, k)), pl.BlockSpec((tk, D), lambda i, k: (k, 0)),
                  pl.BlockSpec((tm, D), lambda i, k: (i, 0)), pl.BlockSpec((1, D), lambda i, k: (0, 0)),
                  pl.BlockSpec((tm, D), lambda i, k: (i, 0))],
        out_specs=[pl.BlockSpec((tm, D), lambda i, k: (i, 0)), pl.BlockSpec((tm, D), lambda i, k: (i, 0)),
                   pl.BlockSpec((1, D), lambda i, k: (0, 0))],
        out_shape=[SDS((S, D), F32), SDS((S, D), BF16), SDS((1, D), F32)],
        compiler_params=_params(54),
    )(dproj, wt, x, g, dx_out)


def _dil_specs(S, DA, tm, dtype):
    specs = [pl.BlockSpec((tm // d, d * DA), lambda i: (i, 0)) for d in DILATIONS]
    shapes = [SDS((S // d, d * DA), dtype) for d in DILATIONS]
    return specs, shapes


def _head_buf(tm, DA):
    return pltpu.VMEM((DA // HEAD_DIM, tm, HEAD_DIM), F32)


def _emit_dilated(buf_ref, dsts, tm, DA):
    for d, dst in zip(DILATIONS, dsts):
        for h in range(DA // HEAD_DIM):
            for r in range(d):
                rows = slice(None) if d == 1 else pl.ds(r, tm // d, stride=d)
                dst[:, r * DA + h * HEAD_DIM:r * DA + (h + 1) * HEAD_DIM] = buf_ref.at[h][rows, :].astype(BF16)


def _collect_dilated(acc_ref, parts, tm, DA):
    for d, p in zip(DILATIONS, parts):
        for h in range(DA // HEAD_DIM):
            for r in range(d):
                part = p[:, r * DA + h * HEAD_DIM:r * DA + (h + 1) * HEAD_DIM].astype(F32)
                if d == 1:
                    acc_ref[h] = part
                else:
                    rows = pl.ds(r, tm // d, stride=d)
                    acc_ref.at[h][rows, :] = acc_ref.at[h][rows, :] + part


def qk_prep(proj, gq, gk, cos2, sin2, name):
    S = proj.shape[0]
    DA = proj.shape[1] // 7
    H = DA // HEAD_DIM
    tm = _tile(S, 256, 16 * DILATIONS[-1])
    nd = len(DILATIONS)

    def body(q_ref, k_ref, v_ref, gq_ref, gk_ref, c_ref, s_ref, *rest):
        outs, buf_ref = rest[:3 * nd], rest[3 * nd]
        ct, st = c_ref[...], s_ref[...]
        for t, (src, g_ref) in enumerate(((q_ref, gq_ref), (k_ref, gk_ref))):
            gain = g_ref[...]
            for h in range(H):
                sl = slice(h * HEAD_DIM, (h + 1) * HEAD_DIM)
                xh = src[:, sl]
                n = xh * _rstd(xh) * gain
                buf_ref[h] = n * ct + pltpu.roll(n, HEAD_DIM // 2, 1) * st
            _emit_dilated(buf_ref, outs[t * nd:(t + 1) * nd], tm, DA)
        for h in range(H):
            buf_ref[h] = v_ref[:, h * HEAD_DIM:(h + 1) * HEAD_DIM]
        _emit_dilated(buf_ref, outs[2 * nd:], tm, DA)

    specs, shapes = _dil_specs(S, DA, tm, BF16)
    outs = _call(
        body, name=name, grid=(S // tm,),
        in_specs=[_row(tm, DA, 0), _row(tm, DA, 1), _row(tm, DA, 2), _full((1, HEAD_DIM)), _full((1, HEAD_DIM)),
                  _row(tm, HEAD_DIM), _row(tm, HEAD_DIM)],
        out_specs=specs * 3, out_shape=shapes * 3, scratch_shapes=[_head_buf(tm, DA)],
        compiler_params=_params(48),
    )(proj, proj, proj, gq, gk, cos2, sin2)
    return outs[:nd], outs[nd:2 * nd], outs[2 * nd:]


def qk_prep_bwd(dqs, dks, dvs, proj, gq, gk, cos2, sin2, dproj, name):
    S = proj.shape[0]
    DA = proj.shape[1] // 7
    H = DA // HEAD_DIM
    tm = _tile(S, 256, 16 * DILATIONS[-1])
    nb = len(dqs)

    def body(*refs):
        dq_refs, dk_refs, dv_refs = refs[:nb], refs[nb:2 * nb], refs[2 * nb:3 * nb]
        q_ref, k_ref, gq_ref, gk_ref, c_ref, s_ref = refs[3 * nb:3 * nb + 6]
        out_ref, gqa_ref, gka_ref, acc_ref = refs[3 * nb + 7:]
        ct, st = c_ref[...], s_ref[...]

        @pl.when(pl.program_id(0) == 0)
        def _():
            gqa_ref[...] = jnp.zeros_like(gqa_ref)
            gka_ref[...] = jnp.zeros_like(gka_ref)

        for parts, x_ref, g_ref, col, gacc in ((dq_refs, q_ref, gq_ref, 0, gqa_ref),
                                               (dk_refs, k_ref, gk_ref, DA, gka_ref)):
            gain = g_ref[...]
            gsum = jnp.zeros((1, HEAD_DIM), F32)
            _collect_dilated(acc_ref, parts, tm, DA)
            for h in range(H):
                sl = slice(h * HEAD_DIM, (h + 1) * HEAD_DIM)
                dout = acc_ref[h]
                dn = dout * ct + pltpu.roll(dout * st, HEAD_DIM // 2, 1)
                xh = x_ref[:, sl]
                r = _rstd(xh)
                xn = xh * r
                gsum = gsum + jnp.sum(dn * xn, axis=0, keepdims=True)
                dnn = dn * gain
                dx = r * (dnn - xn * jnp.mean(dnn * xn, axis=-1, keepdims=True))
                out_ref[:, col + h * HEAD_DIM:col + (h + 1) * HEAD_DIM] = dx.astype(BF16)
            gacc[...] += gsum
        _collect_dilated(acc_ref, dv_refs, tm, DA)
        for h in range(H):
            out_ref[:, 2 * DA + h * HEAD_DIM:2 * DA + (h + 1) * HEAD_DIM] = acc_ref[h].astype(BF16)

    specs, _ = _dil_specs(S, DA, tm, BF16)
    return _call(
        body, name=name, grid=(S // tm,),
        in_specs=specs * 3 + [_row(tm, DA, 0), _row(tm, DA, 1), _full((1, HEAD_DIM)),
                              _full((1, HEAD_DIM)), _row(tm, HEAD_DIM), _row(tm, HEAD_DIM), pl.BlockSpec(memory_space=pl.ANY)],
        out_specs=[_row(tm, 3 * DA)] + [_full((1, HEAD_DIM))] * 2,
        out_shape=[SDS((S, 7 * DA), BF16)] + [SDS((1, HEAD_DIM), F32)] * 2,
        input_output_aliases={3 * nb + 6: 0},
        scratch_shapes=[_head_buf(tm, DA)],
        compiler_params=_params(48),
    )(*dqs, *dks, *dvs, proj, proj, gq, gk, cos2, sin2, dproj)


def _band_mask(n):
    row = lax.broadcasted_iota(jnp.int32, (Q_BLOCK, 2 * Q_BLOCK), 0)
    col = lax.broadcasted_iota(jnp.int32, (Q_BLOCK, 2 * Q_BLOCK), 1)
    first = jnp.where(n > 0, Q_BLOCK, 2 * Q_BLOCK + 1)
    return (col <= row) | ((col - row) >= first)


def attn_fwd(qh, kh, vb, d, name):
    L = qh.shape[0]
    DA = qh.shape[1] // d
    H = DA // HEAD_DIM
    nb = L // Q_BLOCK
    scale = HEAD_DIM ** -0.5
    view = (L, d * DA)

    def body(q_ref, kc_ref, kp_ref, vc_ref, vp_ref, o_ref, lse_ref):
        mask = _band_mask(pl.program_id(1))
        ones = jnp.ones((2 * Q_BLOCK, HEAD_DIM), BF16)
        for h in range(H):
            sl = slice(h * HEAD_DIM, (h + 1) * HEAD_DIM)
            keys = jnp.concatenate([kc_ref[:, sl], kp_ref[:, sl]], axis=0)
            s = lax.dot_general(q_ref[:, sl], keys, NT, preferred_element_type=F32) * scale
            s = jnp.where(mask, s, NEG)
            m = jnp.max(s, axis=-1, keepdims=True)
            p = jnp.exp(s - m).astype(BF16)
            vals = jnp.concatenate([jnp.concatenate([vc_ref[:, sl], vp_ref[:, sl]], axis=0), ones], axis=1)
            ol = jnp.dot(p, vals, preferred_element_type=F32)
            l = ol[:, HEAD_DIM:]
            o_ref[:, sl] = ol[:, :HEAD_DIM] / l
            lse_ref[0, :, h:h + 1] = m + jnp.log(l[:, 0:1])

    cur = pl.BlockSpec((Q_BLOCK, DA), lambda r, n: (n, r))
    prev = pl.BlockSpec((Q_BLOCK, DA), lambda r, n: (jnp.maximum(n - 1, 0), r))
    o, lse = _call(
        body, name=name, grid=(d, nb), in_specs=[cur, cur, prev, cur, prev],
        out_specs=[cur, pl.BlockSpec((1, Q_BLOCK, H), lambda r, n: (r, n, 0))],
        out_shape=[SDS(view, F32), SDS((d, L, H), F32)],
        compiler_params=_params(32),
    )(qh, kh, kh, vb, vb)
    return o, lse


def attn_bwd(qh, kh, vb, da, lse_d, delta_d, d, name):
    L = qh.shape[0]
    DA = qh.shape[1] // d
    H = DA // HEAD_DIM
    nb = L // Q_BLOCK
    scale = HEAD_DIM ** -0.5
    view = (L, d * DA)
    resident = d > 1

    def body(q_ref, kc_ref, kp_ref, vc_ref, vp_ref, do_ref, lse_ref, dl_ref, dq_ref, dk_ref, dv_ref, dkc_ref, dvc_ref):
        n = pl.program_id(1)
        mask = _band_mask(n)
        if resident:
            done = pl.ds(pl.multiple_of(jnp.maximum(n - 1, 0) * Q_BLOCK, Q_BLOCK), Q_BLOCK)
            fresh = pl.ds(pl.multiple_of(n * Q_BLOCK, Q_BLOCK), Q_BLOCK)
        else:
            done = slice(None)

        @pl.when(n == 0)
        def _():
            dkc_ref[...] = jnp.zeros_like(dkc_ref)
            dvc_ref[...] = jnp.zeros_like(dvc_ref)

        @pl.when(n < nb)
        def _():
            for h in range(H):
                sl = slice(h * HEAD_DIM, (h + 1) * HEAD_DIM)
                q, do = q_ref[:, sl], do_ref[:, sl]
                keys = jnp.concatenate([kc_ref[:, sl], kp_ref[:, sl]], axis=0)
                vals = jnp.concatenate([vc_ref[:, sl], vp_ref[:, sl]], axis=0)
                lse = jnp.broadcast_to(lse_ref[0, :, h:h + 1], (Q_BLOCK, 2 * Q_BLOCK))
                dl = jnp.broadcast_to(dl_ref[0, :, h:h + 1], (Q_BLOCK, 2 * Q_BLOCK))
                s = lax.dot_general(q, keys, NT, preferred_element_type=F32) * scale
                p = jnp.exp(jnp.where(mask, s, NEG) - lse)
                dp = lax.dot_general(do, vals, NT, preferred_element_type=F32)
                ds = (p * (dp - dl) * scale).astype(BF16)
                dq_ref[:, sl] = jnp.dot(ds, keys, preferred_element_type=F32).astype(BF16)
                dk = lax.dot_general(ds, q, TN, preferred_element_type=F32)
                dv = lax.dot_general(p.astype(BF16), do, TN, preferred_element_type=F32)
                dk_ref[done, sl] = (dkc_ref[:, sl] + dk[Q_BLOCK:]).astype(BF16)
                dv_ref[done, sl] = (dvc_ref[:, sl] + dv[Q_BLOCK:]).astype(BF16)
                dkc_ref[:, sl] = dk[:Q_BLOCK]
                dvc_ref[:, sl] = dv[:Q_BLOCK]
                if resident:
                    dk_ref[fresh, sl] = dk[:Q_BLOCK].astype(BF16)
                    dv_ref[fresh, sl] = dv[:Q_BLOCK].astype(BF16)

        if not resident:
            @pl.when(n == nb)
            def _():
                dk_ref[...] = dkc_ref[...].astype(BF16)
                dv_ref[...] = dvc_ref[...].astype(BF16)

    steps = nb if resident else nb + 1
    cur = pl.BlockSpec((Q_BLOCK, DA), lambda r, n: (jnp.minimum(n, nb - 1), r))
    prev = pl.BlockSpec((Q_BLOCK, DA), lambda r, n: (jnp.clip(n - 1, 0, nb - 1), r))
    if resident:
        keyside = pl.BlockSpec((L, DA), lambda r, n: (0, r))
    else:
        keyside = pl.BlockSpec((Q_BLOCK, DA), lambda r, n: (jnp.maximum(n - 1, 0), r))
    stat = pl.BlockSpec((1, Q_BLOCK, H), lambda r, n: (r, jnp.minimum(n, nb - 1), 0))
    dq, dk, dv = _call(
        body, name=name, grid=(d, steps), in_specs=[cur, cur, prev, cur, prev, cur, stat, stat],
        out_specs=[cur, keyside, keyside], out_shape=[SDS(view, BF16)] * 3,
        scratch_shapes=[pltpu.VMEM((Q_BLOCK, DA), F32), pltpu.VMEM((Q_BLOCK, DA), F32)],
        compiler_params=_params(40),
    )(qh, kh, kh, vb, vb, da, lse_d, delta_d)
    return dq, dk, dv


def _to_branch(stat, d):
    S, H = stat.shape
    return stat.reshape(S // d, d, H).transpose(1, 0, 2)


def _from_branch(stat):
    d, L, H = stat.shape
    return stat.transpose(1, 0, 2).reshape(L * d, H)


def att_combine(os_, lses, proj, gain, name):
    S, DA = os_[0].shape
    H = DA // HEAD_DIM
    tm = _tile(S, 256, 16 * DILATIONS[-1])
    nb = len(os_)

    def body(*refs):
        o_views, l_refs = refs[:nb], refs[nb:2 * nb]
        gate_ref, gain_ref, att_ref, y_ref, lse_ref = refs[2 * nb:2 * nb + 5]
        bufs = refs[2 * nb + 5:]
        for d, view, buf in zip(DILATIONS[1:], o_views[1:], bufs):
            for h in range(H):
                for r in range(d):
                    buf.at[h][pl.ds(r, tm // d, stride=d), :] = view[:, r * DA + h * HEAD_DIM:r * DA + (h + 1) * HEAD_DIM]
        ls = [r[...] for r in l_refs]
        top = ls[0]
        for l in ls[1:]:
            top = jnp.maximum(top, l)
        den = jnp.exp(ls[0] - top)
        for l in ls[1:]:
            den = den + jnp.exp(l - top)
        lse = top + jnp.log(den)
        lse_ref[...] = lse
        ws = [jnp.exp(l - lse) for l in ls]
        for h in range(H):
            sl = slice(h * HEAD_DIM, (h + 1) * HEAD_DIM)
            acc = ws[0][:, h:h + 1] * o_views[0][:, sl]
            for w, buf in zip(ws[1:], bufs):
                acc = acc + w[:, h:h + 1] * buf[h]
            att_ref[:, sl] = acc
        a = att_ref[...]
        g = gate_ref[...]
        y_ref[...] = (a * _rstd(a) * gain_ref[...] * (g * _sig(g))).astype(BF16)

    specs, _ = _dil_specs(S, DA, tm, F32)
    return _call(
        body, name=name, grid=(S // tm,),
        in_specs=specs + [_row(tm, H)] * nb + [_row(tm, DA, 3), _full((1, DA))],
        out_specs=[_row(tm, DA), _row(tm, DA), _row(tm, H)],
        out_shape=[SDS((S, DA), F32), SDS((S, 2 * DA), BF16), SDS((S, H), F32)],
        scratch_shapes=[_head_buf(tm, DA)] * (nb - 1),
        compiler_params=_params(48),
    )(*os_, *lses, proj, gain)


def gate_bwd(dcat, cblk, a, proj, gate_blk, gain, dilated, dproj, name):
    S, DA = a.shape
    H = DA // HEAD_DIM
    tm = _tile(S, 256, 16 * DILATIONS[-1])
    nd = len(DILATIONS) if dilated else 1

    def body(dy_ref, a_ref, gate_ref, gain_ref, *rest):
        dg_ref, gacc_ref, *rest = rest[0 if dproj is None else 1:]

        @pl.when(pl.program_id(0) == 0)
        def _():
            gacc_ref[...] = jnp.zeros_like(gacc_ref)

        dy, av, g, gain_v = dy_ref[...], a_ref[...], gate_ref[...], gain_ref[...]
        r = _rstd(av)
        n = av * r
        sg = _sig(g)
        dg_ref[...] = (dy * (n * gain_v) * (sg * (1.0 + g * (1.0 - sg)))).astype(BF16)
        drn = dy * (g * sg)
        gacc_ref[...] += jnp.sum(drn * n, axis=0, keepdims=True)
        dn = drn * gain_v
        da = r * (dn - n * jnp.mean(dn * n, axis=-1, keepdims=True))
        if dilated:
            da_refs, delta_ref, buf_ref = rest[:nd], rest[nd], rest[nd + 1]
            for h in range(H):
                buf_ref[h] = da[:, h * HEAD_DIM:(h + 1) * HEAD_DIM]
            _emit_dilated(buf_ref, da_refs, tm, DA)
            prod = da * av
            for h in range(H):
                delta_ref[:, h:h + 1] = jnp.sum(prod[:, h * HEAD_DIM:(h + 1) * HEAD_DIM], axis=-1, keepdims=True)
        else:
            rest[0][...] = da.astype(BF16)

    out_specs = [_row(tm, DA, gate_blk), _full((1, DA))]
    out_shape = [SDS((S, 7 * DA), BF16), SDS((1, DA), F32)]
    scratch = []
    carried = [] if dproj is None else [dproj]
    if dilated:
        specs, shapes = _dil_specs(S, DA, tm, BF16)
        out_specs += specs + [_row(tm, H)]
        out_shape += shapes + [SDS((S, H), F32)]
        scratch = [_head_buf(tm, DA)]
    else:
        out_specs.append(_row(tm, DA))
        out_shape.append(SDS((S, DA), BF16))
    return _call(
        body, name=name, grid=(S // tm,),
        in_specs=[_row(tm, DA, cblk), _row(tm, DA), _row(tm, DA, gate_blk), _full((1, DA))] + [pl.BlockSpec(memory_space=pl.ANY)] * len(carried),
        out_specs=out_specs, out_shape=out_shape, scratch_shapes=scratch, compiler_params=_params(48),
        input_output_aliases={4: 0} if carried else {},
    )(dcat, a, proj, gain, *carried)


def _by_sublane_phase(offsets):
    groups = [(p, [o for o in offsets if o % 8 == p]) for p in range(8)]
    return [(p, sorted(os_)) for p, os_ in groups if os_]


def _shifted_rows(src_ref, tmp_ref, base, phase, offsets, lanes):
    if phase == 0:
        return lambda o: src_ref[pl.ds(base + o, ROW_CHUNK), lanes]
    span = offsets[-1] - phase + ROW_CHUNK
    tmp_ref[phase, pl.ds(0, span), :] = src_ref[pl.ds(base + phase, span), lanes]
    return lambda o: tmp_ref[phase, pl.ds(o - phase, ROW_CHUNK), :]


def _shift_scratch():
    return pltpu.VMEM((8, CONV_PAD + ROW_CHUNK, LANE), F32)


def _fill_u(i, a_ref, b_ref, ah_ref, bh_ref, uext_ref, tm):
    uext_ref[pl.ds(CONV_PAD, tm), :] = a_ref[...] * _sig(b_ref[...])
    uh = ah_ref[...] * _sig(bh_ref[...])
    uext_ref[pl.ds(0, CONV_PAD), :] = jnp.where(i > 0, uh, 0.0)


def conv_fwd(proj, wk, bias, ln_g, ln_b, out_g, wpw, cat, name):
    S = proj.shape[0]
    DC = proj.shape[1] // 7
    tm = _tile(S, 128, ROW_CHUNK)
    lead = CONV_PAD - (CONV_WIDTH - 1)

    def body(a_ref, b_ref, ah_ref, bh_ref, gate_ref, wk_ref, bias_ref, lg_ref, lb_ref, og_ref, wpw_ref, cat_ref,
             cy_ref, z_ref, conv_ref, y_ref, uext_ref, tmp_ref):
        _fill_u(pl.program_id(0), a_ref, b_ref, ah_ref, bh_ref, uext_ref, tm)

        def cols(cc, carry):
            c0 = pl.multiple_of(cc * LANE, LANE)
            lanes = pl.ds(c0, LANE)
            for rr in range(tm // ROW_CHUNK):
                acc = jnp.broadcast_to(bias_ref[:, lanes], (ROW_CHUNK, LANE))
                for phase, offsets in _by_sublane_phase(range(lead, lead + CONV_WIDTH)):
                    rows = _shifted_rows(uext_ref, tmp_ref, rr * ROW_CHUNK, phase, offsets, lanes)
                    for o in offsets:
                        acc = acc + wk_ref[o - lead:o - lead + 1, lanes] * rows(o)
                y_ref[pl.ds(rr * ROW_CHUNK, ROW_CHUNK), lanes] = acc
            return carry

        lax.fori_loop(0, DC // LANE, cols, 0)
        y = y_ref[...]
        yc = y - jnp.mean(y, axis=-1, keepdims=True)
        ln = yc * _rstd(yc) * lg_ref[...] + lb_ref[...]
        zb = (ln * _sig(ln)).astype(BF16)
        z_ref[...] = zb
        conv = jnp.dot(zb, wpw_ref[...], preferred_element_type=F32)
        conv_ref[...] = conv
        g = gate_ref[...]
        cy_ref[...] = (conv * _rstd(conv) * og_ref[...] * (g * _sig(g))).astype(BF16)

    vec = _full((1, DC))
    return _call(
        body, name=name, grid=(S // tm,),
        in_specs=[_row(tm, DC, 4), _row(tm, DC, 5), _halo_prev(tm, DC, 4), _halo_prev(tm, DC, 5), _row(tm, DC, 6),
                  _full((CONV_PAD, DC)), vec, vec, vec, vec, _full((DC, DC)), pl.BlockSpec(memory_space=pl.ANY)],
        out_specs=[_row(tm, DC, 1)] + [_row(tm, DC)] * 3,
        out_shape=[SDS((S, 2 * DC), BF16), SDS((S, DC), BF16), SDS((S, DC), F32), SDS((S, DC), F32)],
        input_output_aliases={11: 0},
        scratch_shapes=[pltpu.VMEM((CONV_PAD + tm, DC), F32), _shift_scratch()],
        compiler_params=_params(48),
    )(proj, proj, proj, proj, proj, wk, bias, ln_g, ln_b, out_g, wpw, cat)


def conv_bwd_ln(dconv, wpw, y, ln_g, ln_b, name):
    S, DC = y.shape
    tm = _tile(S, 256, 8)

    def body(dc_ref, wpw_ref, y_ref, lg_ref, lb_ref, dy_ref, glg_ref, glb_ref, gb_ref):
        @pl.when(pl.program_id(0) == 0)
        def _():
            glg_ref[...] = jnp.zeros_like(glg_ref)
            glb_ref[...] = jnp.zeros_like(glb_ref)
            gb_ref[...] = jnp.zeros_like(gb_ref)

        dz = lax.dot_general(dc_ref[...], wpw_ref[...], NT, preferred_element_type=F32)
        yv = y_ref[...]
        yc = yv - jnp.mean(yv, axis=-1, keepdims=True)
        rstd = _rstd(yc)
        yhat = yc * rstd
        ln = yhat * lg_ref[...] + lb_ref[...]
        sg = _sig(ln)
        dln = dz * (sg * (1.0 + ln * (1.0 - sg)))
        glb_ref[...] += jnp.sum(dln, axis=0, keepdims=True)
        glg_ref[...] += jnp.sum(dln * yhat, axis=0, keepdims=True)
        dyh = dln * lg_ref[...]
        dy = rstd * (dyh - jnp.mean(dyh, axis=-1, keepdims=True) - yhat * jnp.mean(dyh * yhat, axis=-1, keepdims=True))
        dy_ref[...] = dy
        gb_ref[...] += jnp.sum(dy, axis=0, keepdims=True)

    vec = _full((1, DC))
    return _call(
        body, name=name, grid=(S // tm,),
        in_specs=[_row(tm, DC), _full((DC, DC)), _row(tm, DC), vec, vec],
        out_specs=[_row(tm, DC), vec, vec, vec],
        out_shape=[SDS((S, DC), F32)] + [SDS((1, DC), F32)] * 3,
        compiler_params=_params(48),
    )(dconv, wpw, y, ln_g, ln_b)


def conv_bwd_dw(dy, proj, wk, dproj, name):
    S, DC = dy.shape
    tm = _tile(S, 128, ROW_CHUNK)
    nsteps = S // tm
    lead = CONV_PAD - (CONV_WIDTH - 1)
    groups = ROW_CHUNK // 8

    def body(dy_ref, dyn_ref, a_ref, b_ref, ah_ref, bh_ref, wk_ref, dproj_ref, dab_ref, gw_ref, uext_ref, dyext_ref, du_ref,
             tmp_dy_ref, tmp_u_ref):
        i = pl.program_id(0)

        @pl.when(i == 0)
        def _():
            gw_ref[...] = jnp.zeros_like(gw_ref)

        _fill_u(i, a_ref, b_ref, ah_ref, bh_ref, uext_ref, tm)
        dyext_ref[pl.ds(0, tm), :] = dy_ref[...]
        dyext_ref[pl.ds(tm, CONV_PAD), :] = jnp.where(i < nsteps - 1, dyn_ref[...], 0.0)

        def cols(cc, carry):
            c0 = pl.multiple_of(cc * LANE, LANE)
            lanes = pl.ds(c0, LANE)
            for rr in range(tm // ROW_CHUNK):
                base = rr * ROW_CHUNK
                acc = jnp.zeros((ROW_CHUNK, LANE), F32)
                for phase, offsets in _by_sublane_phase(range(CONV_WIDTH)):
                    rows = _shifted_rows(dyext_ref, tmp_dy_ref, base, phase, offsets, lanes)
                    for o in offsets:
                        j = CONV_WIDTH - 1 - o
                        acc = acc + wk_ref[j:j + 1, lanes] * rows(o)
                du_ref[pl.ds(base, ROW_CHUNK), lanes] = acc
                dyc = dyext_ref[pl.ds(base, ROW_CHUNK), lanes]
                for phase, offsets in _by_sublane_phase(range(lead, lead + CONV_WIDTH)):
                    rows = _shifted_rows(uext_ref, tmp_u_ref, base, phase, offsets, lanes)
                    for o in offsets:
                        prod = dyc * rows(o)
                        part = prod[0:8]
                        for k in range(1, groups):
                            part = part + prod[8 * k:8 * k + 8]
                        gw_ref[o - lead, :, lanes] += part
            return carry

        lax.fori_loop(0, DC // LANE, cols, 0)
        du = du_ref[...]
        sb = _sig(b_ref[...])
        dab_ref[:, :DC] = (du * sb).astype(BF16)
        dab_ref[:, DC:] = (du * a_ref[...] * sb * (1.0 - sb)).astype(BF16)

    return _call(
        body, name=name, grid=(nsteps,),
        in_specs=[_row(tm, DC), _halo_next(tm, DC, S // CONV_PAD), _row(tm, DC, 4), _row(tm, DC, 5),
                  _halo_prev(tm, DC, 4), _halo_prev(tm, DC, 5), _full((CONV_PAD, DC)), pl.BlockSpec(memory_space=pl.ANY)],
        out_specs=[_row(tm, 2 * DC, 2), _full((CONV_PAD, 8, DC))],
        out_shape=[SDS((S, 7 * DC), BF16), SDS((CONV_PAD, 8, DC), F32)],
        input_output_aliases={7: 0},
        scratch_shapes=[pltpu.VMEM((CONV_PAD + tm, DC), F32), pltpu.VMEM((tm + CONV_PAD, DC), F32), pltpu.VMEM((tm, DC), F32),
                        _shift_scratch(), _shift_scratch()],
        compiler_params=_params(40),
    )(dy, dy, proj, proj, proj, proj, wk, dproj)


def loss_head(xo, target, name):
    S, D = xo.shape
    tm = _tile(S, 256, 8)

    def body(x_ref, t_ref, dy_ref, dyb_ref, acc_ref):
        @pl.when(pl.program_id(0) == 0)
        def _():
            acc_ref[...] = jnp.zeros_like(acc_ref)

        err = x_ref[...] - t_ref[...]
        dy = err * (1.0 / D)
        dy_ref[...] = dy
        dyb_ref[...] = dy.astype(BF16)
        acc_ref[...] += jnp.sum(err * dy, axis=0, keepdims=True) * 0.5

    return _call(
        body, name=name, grid=(S // tm,), in_specs=[_row(tm, D), _row(tm, D)],
        out_specs=[_row(tm, D), _row(tm, D), _full((1, D))],
        out_shape=[SDS((S, D), F32), SDS((S, D), BF16), SDS((1, D), F32)],
        compiler_params=_params(32),
    )(xo, target)


def _coords():
    x, y, c = lax.axis_index("x"), lax.axis_index("y"), lax.axis_index("c")
    return x, y, c


def _lin(p):
    return 4 * p[0] + 2 * p[1] + p[2]


def _chip(p):
    return 2 * p[0] + p[1]


def _slot(ref, axis, idx, size):
    index = [slice(None)] * len(ref.shape)
    index[axis] = pl.ds(idx * size, size)
    return ref.at[tuple(index)]


def all_gather(blocks, axes, name):
    na = len(blocks)
    sizes = [b.shape[ax] for b, ax in zip(blocks, axes)]
    fulls = [SDS(b.shape[:ax] + (N_DEV * b.shape[ax],) + b.shape[ax + 1:], b.dtype) for b, ax in zip(blocks, axes)]

    def body(*refs):
        in_refs, out_refs = refs[:na], refs[na:2 * na]
        send_sems, recv_sems, local_sems = refs[2 * na:]
        x, y, c = _coords()
        me, sibling = (x, y, c), (x, y, 1 - c)
        chips = [(1 - x, y), (x, 1 - y), (1 - x, 1 - y)]

        def place(a, p):
            return _slot(out_refs[a], axes[a], _lin(p), sizes[a])

        def copy(a, k, block, to, src=None):
            return pltpu.make_async_remote_copy(
                src_ref=place(a, block) if src is None else src, dst_ref=place(a, block),
                send_sem=send_sems.at[a, k], recv_sem=recv_sems.at[a, k], device_id=to, device_id_type=MESH)

        mine = [pltpu.make_async_copy(in_refs[a], place(a, me), local_sems.at[a]) for a in range(na)]
        for cp in mine:
            cp.start()
        first = []
        for a in range(na):
            first.append(copy(a, 0, me, sibling, src=in_refs[a]))
            first += [copy(a, 1 + j, me, (*chip, c), src=in_refs[a]) for j, chip in enumerate(chips)]
        for cp in first:
            cp.start()
        passed = []
        for j, chip in enumerate(chips):
            for a in range(na):
                copy(a, 1 + j, (*chip, c), me).wait_recv()
                cp = copy(a, 4 + j, (*chip, c), sibling)
                cp.start()
                passed.append(cp)
        for a in range(na):
            copy(a, 0, sibling, me).wait_recv()
            for j, chip in enumerate(chips):
                copy(a, 4 + j, (*chip, 1 - c), me).wait_recv()
        for cp in first + passed:
            cp.wait_send()
        for cp in mine:
            cp.wait()

    hbm = pl.BlockSpec(memory_space=pltpu.HBM)
    return _call(
        body, name=name, in_specs=[hbm] * na, out_specs=[hbm] * na, out_shape=fulls,
        scratch_shapes=[pltpu.SemaphoreType.DMA((na, 7)), pltpu.SemaphoreType.DMA((na, 7)), pltpu.SemaphoreType.DMA((na,))],
    )(*blocks)


class _Exchange:
    def __init__(self, gather, srcs, axes, name, after, route="all", lands=None):
        self.gather, self.axes, self.name, self.route = gather, axes, name, route
        if route == "pass":
            self.na, self.ns = len(lands), 0
            self.sizes = [l.shape[ax] // N_DEV for l, ax in zip(lands, axes)]
            self.kinds = [pltpu.HBM(l.shape, l.dtype) for l in lands]
            self._start([pltpu.with_memory_space_constraint(t, pltpu.HBM) for t in lands], after)
            return
        self.na = self.ns = len(srcs)
        if gather:
            self.sizes = [s.shape[ax] for s, ax in zip(srcs, axes)]
            lands = [s.shape[:ax] + (N_DEV * s.shape[ax],) + s.shape[ax + 1:] for s, ax in zip(srcs, axes)]
        else:
            self.sizes = [None if ax is None else s.shape[ax] // N_DEV for s, ax in zip(srcs, axes)]
            lands = [s.shape if ax is None else (N_DEV,) + s.shape[:ax] + (sz,) + s.shape[ax + 1:]
                     for s, ax, sz in zip(srcs, axes, self.sizes)]
        self.kinds = [pltpu.HBM(s.shape, s.dtype) for s in srcs] + [pltpu.HBM(l, s.dtype) for l, s in zip(lands, srcs)]
        lands = [lax.empty(l, s.dtype) for l, s in zip(lands, srcs)]
        after = jnp.zeros((8, LANE), F32) if after is None else after
        self._start([pltpu.with_memory_space_constraint(t, pltpu.HBM) for t in list(srcs) + lands], after)

    def _src(self, a, ref, owner):
        if self.gather:
            return ref
        return ref.at[_lin(owner)] if self.axes[a] is None else _slot(ref, self.axes[a], _lin(owner), self.sizes[a])

    def _dst(self, a, land, sender):
        return _slot(land, self.axes[a], _lin(sender), self.sizes[a]) if self.gather else land.at[_lin(sender)]

    def _flips(self):
        if self.route == "all":
            return [(k >> 2 & 1, k >> 1 & 1, k & 1) for k in range(1, N_DEV)]
        if self.route == "chips":
            return [(0, 0, 1), (1, 0, 0), (0, 1, 0), (1, 1, 0)]
        if self.route == "sibling":
            return [(0, 0, 1)] * 4
        return [(1, 0, 0), (0, 1, 0), (1, 1, 0)]

    def _copies(self, refs, send_sems, recv_sems):
        na, ns = self.na, self.ns
        me = _coords()
        flips = self._flips()
        n = len(flips)
        others = [tuple(1 - v if f else v for v, f in zip(me, flip)) for flip in flips]
        sends, arrivals = [], []
        for a in range(na):
            land = refs[ns + a]
            for k, other in enumerate(others):
                if self.route == "pass":
                    peer = (me[0], me[1], 1 - me[2])
                    theirs = (other[0], other[1], 1 - me[2])
                    send = dict(src_ref=self._dst(a, land, other), dst_ref=self._dst(a, land, other))
                    arrive = dict(src_ref=self._dst(a, land, theirs), dst_ref=self._dst(a, land, theirs))
                elif self.route == "sibling":
                    peer = other
                    send = dict(src_ref=refs[a].at[k], dst_ref=land.at[k])
                    arrive = send
                elif self.route == "chip_parts":
                    peer = other
                    send = dict(src_ref=refs[a].at[_chip(peer)], dst_ref=land.at[_chip(me)])
                    arrive = dict(src_ref=refs[a].at[_chip(me)], dst_ref=land.at[_chip(peer)])
                else:
                    peer = other
                    send = dict(src_ref=self._src(a, refs[a], peer), dst_ref=self._dst(a, land, me))
                    arrive = dict(src_ref=self._src(a, refs[a], me), dst_ref=self._dst(a, land, peer))
                pair = dict(send_sem=send_sems.at[n * a + k], recv_sem=recv_sems.at[n * a + k], device_id=peer, device_id_type=MESH)
                sends.append(pltpu.make_async_remote_copy(**send, **pair))
                arrivals.append(pltpu.make_async_remote_copy(**arrive, **pair))
        return sends, arrivals

    def _place_own(self, operands):
        na = self.na
        mine = _chip(_coords()) if self.route == "chip_parts" else _lin(_coords())
        me = jnp.reshape(mine, (1,)).astype(jnp.int32)
        lands = []
        for a in range(na):
            src, land, ax = operands[a], operands[na + a], self.axes[a]
            if ax == 1:
                R, C = src.shape[0], self.sizes[a]
                steps, tile = 1, (R, C)
                in_map = lambda i, me_ref: (0, me_ref[0])
            elif ax is None:
                R, C = src.shape[1:]
                tr = _tile(R, 512, 16)
                steps, tile = R // tr, (None, tr, C)
                in_map = lambda i, me_ref: (me_ref[0], i, 0)
            else:
                R, C = (src.shape[0] if self.gather else self.sizes[a]), src.shape[1]
                tr = _tile(R, 512, 16)
                steps, tile = R // tr, (tr, C)
                in_map = (lambda i, me_ref: (i, 0)) if self.gather else (lambda i, me_ref, n=R // tr: (me_ref[0] * n + i, 0))
            if self.gather:
                out_spec = pl.BlockSpec(tile, lambda i, me_ref, n=steps: (me_ref[0] * n + i, 0))
            else:
                out_spec = pl.BlockSpec((None,) + tuple(t for t in tile if t is not None), lambda i, me_ref: (me_ref[0], i, 0))

            def body(me_ref, src_ref, land_ref, out_ref):
                out_ref[...] = src_ref[...]

            lands.append(_call(
                body, name=f"{self.name}_own{a}",
                grid_spec=pltpu.PrefetchScalarGridSpec(
                    num_scalar_prefetch=1, grid=(steps,),
                    in_specs=[pl.BlockSpec(tile, in_map), pl.BlockSpec(memory_space=pl.ANY)], out_specs=out_spec),
                out_shape=SDS(land.shape, land.dtype), input_output_aliases={2: 0}, compiler_params=_params(32),
            )(me, src, land))
        return operands[:na] + lands

    def _start(self, operands, after):
        nops = self.ns + self.na
        nsem = len(self._flips()) * self.na
        if self.ns and self.route != "sibling":
            operands = self._place_own(operands)

        def body(*refs):
            ins = refs[:nops]
            send_sems, recv_sems, token_ref = refs[nops + 1], refs[nops + 2], refs[2 * nops + 3]
            for cp in self._copies(ins, send_sems, recv_sems)[0]:
                cp.start()
            token_ref[...] = jnp.zeros_like(token_ref)

        hbm = pl.BlockSpec(memory_space=pltpu.HBM)
        sem = pl.BlockSpec(memory_space=pltpu.SEMAPHORE)
        outs = _call(
            body, name=self.name + "_start",
            in_specs=[hbm] * nops + [pl.BlockSpec(memory_space=pl.ANY)],
            out_specs=[sem, sem] + [hbm] * nops + [pl.BlockSpec(memory_space=pltpu.VMEM)],
            out_shape=[pltpu.SemaphoreType.DMA((nsem,)), pltpu.SemaphoreType.DMA((nsem,))] + self.kinds + [SDS((8, LANE), F32)],
            input_output_aliases={i: 2 + i for i in range(nops)},
            compiler_params=pltpu.CompilerParams(has_side_effects=pltpu.SideEffectType.DATAFLOW_SIDE_EFFECTING),
        )(*operands, after)
        self.sems, self.thru, self.token = outs[:2], outs[2:2 + nops], outs[2 + nops][0:1, 0:1]

    def wait(self, after):
        nops = self.ns + self.na

        def body(*refs):
            ins, send_sems, recv_sems = refs[:nops], refs[nops], refs[nops + 1]
            sends, arrivals = self._copies(ins, send_sems, recv_sems)
            for cp in sends:
                cp.wait_send()
            for cp in arrivals:
                cp.wait_recv()

        hbm = pl.BlockSpec(memory_space=pltpu.HBM)
        sem = pl.BlockSpec(memory_space=pltpu.SEMAPHORE)
        outs = _call(
            body, name=self.name + "_wait",
            in_specs=[hbm] * nops + [sem, sem, pl.BlockSpec(memory_space=pl.ANY)],
            out_specs=[hbm] * nops, out_shape=self.kinds,
            input_output_aliases={i: i for i in range(nops)},
            compiler_params=pltpu.CompilerParams(has_side_effects=pltpu.SideEffectType.DATAFLOW_SIDE_EFFECTING),
        )(*self.thru, *self.sems, after)
        return outs[self.ns:]


def adamw(w, m, v, parts, layer, prev, name):
    nl, R, C = w.shape
    nparts = parts.shape[0]
    tr = _tile(R, 128, 8) if R % 8 == 0 else R

    def body(w_ref, m_ref, v_ref, p_ref, *rest):
        g_ref, d_ref, mo_ref, vo_ref = rest[-4:]
        g = p_ref[0].astype(F32)
        for s in range(1, nparts):
            g = g + p_ref[s].astype(F32)
        mn = ADAM_B1 * m_ref[0] + (1.0 - ADAM_B1) * g
        vn = ADAM_B2 * v_ref[0] + (1.0 - ADAM_B2) * (g * g)
        m_hat = mn / (1.0 - ADAM_B1 ** ADAM_STEP)
        v_hat = vn / (1.0 - ADAM_B2 ** ADAM_STEP)
        g_ref[0] = g
        d_ref[0] = -ADAM_LR * (m_hat / (jnp.sqrt(v_hat) + ADAM_EPS) + ADAM_WD * w_ref[0])
        mo_ref[0] = mn
        vo_ref[0] = vn

    row = pl.BlockSpec((1, tr, C), lambda i: (layer, i, 0))
    carried = [] if prev is None else list(prev)
    return _call(
        body, name=name, grid=(R // tr,),
        in_specs=[row, row, row, pl.BlockSpec((nparts, tr, C), lambda i: (0, i, 0))] + [pl.BlockSpec(memory_space=pl.ANY)] * len(carried),
        out_specs=[row] * 4, out_shape=[SDS((nl, R, C), F32)] * 4,
        input_output_aliases={4 + k: k for k in range(len(carried))},
        compiler_params=_params(48),
    )(w, m, v, parts, *carried)


def _rope_tables(S):
    inv_freq = 1.0 / (ROPE_THETA ** (jnp.arange(0, HEAD_DIM, 2, dtype=F32) / HEAD_DIM))
    ang = jnp.arange(S, dtype=F32)[:, None] * inv_freq[None, :]
    cos, sin = jnp.cos(ang), jnp.sin(ang)
    return jnp.concatenate([cos, cos], axis=-1), jnp.concatenate([-sin, sin], axis=-1)


def _pack_small(D, norm_g, dw_bias, conv_ln_g, conv_ln_b, att_out_g, conv_out_g, q_norm_g, k_norm_g, extra=None):
    qk = jnp.concatenate([q_norm_g.reshape(-1), k_norm_g.reshape(-1)])
    qk = jnp.pad(qk, (0, D - qk.shape[0])).reshape(1, D)
    zero = jnp.zeros((1, D), F32)
    return jnp.concatenate([norm_g, dw_bias, conv_ln_g, conv_ln_b, att_out_g, conv_out_g, qk, zero,
                            zero if extra is None else extra, zero], axis=0)


def _unpack_small(p):
    rows = [p[2 * i:2 * i + 2] for i in range(6)]
    qk = p[12, :4 * HEAD_DIM].reshape(2, DEPTH, HEAD_DIM)
    return rows + [qk[0], qk[1]]


def kernel(x, norm_g, w_in, q_norm_g, k_norm_g, dw_kernel, dw_bias, conv_ln_g, conv_ln_b, w_pw, att_out_g, conv_out_g, w_out, loss_target, m_norm_g, m_w_in, m_q_norm_g, m_k_norm_g, m_dw_kernel, m_dw_bias, m_conv_ln_g, m_conv_ln_b, m_w_pw, m_att_out_g, m_conv_out_g, m_w_out, v_norm_g, v_w_in, v_q_norm_g, v_k_norm_g, v_dw_kernel, v_dw_bias, v_conv_ln_g, v_conv_ln_b, v_w_pw, v_att_out_g, v_conv_out_g, v_w_out):
    xs = x[0]
    D = xs.shape[1]
    bf = lambda t, l: t[l].astype(BF16)
    wint0, dwk_f = all_gather([bf(w_in, 0).T, dw_kernel], [0, 2], "gather_first")
    early = _Exchange(True, [bf(w_pw, 0), bf(w_out, 0)], [0, 0], "gather_layer0", after=wint0)
    later = _Exchange(True, [bf(w_in, 1).T, bf(w_pw, 1), bf(w_out, 1)], [0, 0, 0], "gather_layer1", after=early.token,
                      route="chips")
    landed = {}

    def weights(l, cur):
        if l == 0:
            return wint0, later.token
        landed[1] = landed["passing"].wait(cur)
        return landed[1][0], None

    def mixer_weights(l, cur):
        if l == 0:
            mine = early.wait(cur)
            landed["passing"] = _Exchange(True, None, [0, 0, 0], "pass_layer1", after=mine[0], route="pass",
                                          lands=later.wait(mine[0]))
            return (*mine, landed["passing"].token)
        return (*landed[1][1:], None)

    sent = [[] for _ in range(DEPTH)]

    def send_grads(l, grads, axes, tag):
        sent[l].append(_Exchange(False, grads, axes, f"scatter_{tag}_layer{l}", after=None))
        return sent[l][-1].token

    def win_grads(l, ht, dproj):
        if l > 0:
            return send_grads(l, [mm_nn(ht, dproj, BF16, f"dwin_{l}", owners=N_DEV)], [None], "w_in")
        core = jnp.reshape(lax.axis_index("c"), (1,)).astype(jnp.int32)
        theirs = mm_core_blocks(ht, dproj, 1 - core, f"dwin_{l}_for_sibling")
        to_sibling = _Exchange(False, [theirs], [None], f"scatter_w_in_sibling_layer{l}", after=None, route="sibling")
        mine = mm_core_blocks(ht, dproj, core + to_sibling.token[0].astype(jnp.int32), f"dwin_{l}_own")
        (from_sibling,) = to_sibling.wait(mine)
        both = add_blocks(mine, from_sibling, f"dwin_{l}_chip_sum")
        sent[l].append(_Exchange(False, [both], [None], f"scatter_w_in_chips_layer{l}", after=None, route="chip_parts"))
        return sent[l][-1].token

    dx, loss_cols, small = local_step(xs, loss_target[0], weights, mixer_weights, dwk_f, norm_g, q_norm_g, k_norm_g, dw_bias,
                                      conv_ln_g, conv_ln_b, att_out_g, conv_out_g, send_grads, win_grads)

    big = ((w_pw, m_w_pw, v_w_pw), (w_out, m_w_out, v_w_out), (dw_kernel, m_dw_kernel, v_dw_kernel), (w_in, m_w_in, v_w_in))
    results = [None] * len(big)
    after = dx
    for l in reversed(range(DEPTH)):
        parts = [p for ex in sent[l] for p in ex.wait(after)]
        for i, ((w, m, v), p) in enumerate(zip(big, parts)):
            results[i] = adamw(w, m, v, p, l, results[i], f"adamw_{w.shape[1]}_{w.shape[2]}_{l}")
        after = results[0][3]
    r_wpw, r_wout, r_dwk, r_win = results

    stack = lambda k: jnp.concatenate(small[k], axis=0)
    mine = _pack_small(D, stack("norm_g"), stack("dw_bias"), stack("conv_ln_g"), stack("conv_ln_b"), stack("att_out_g"),
                       stack("conv_out_g"), stack("q"), stack("k"), extra=loss_cols)
    (p_small,) = all_gather([mine], [0], "gather_small")
    pk = lambda n, dw, lg, lb, ao, co, q, k: _pack_small(D, n, dw, lg, lb, ao, co, q, k)[None]
    r_small = adamw(pk(norm_g, dw_bias, conv_ln_g, conv_ln_b, att_out_g, conv_out_g, q_norm_g, k_norm_g),
                    pk(m_norm_g, m_dw_bias, m_conv_ln_g, m_conv_ln_b, m_att_out_g, m_conv_out_g, m_q_norm_g, m_k_norm_g),
                    pk(v_norm_g, v_dw_bias, v_conv_ln_g, v_conv_ln_b, v_att_out_g, v_conv_out_g, v_q_norm_g, v_k_norm_g),
                    p_small.reshape(N_DEV, 16, D), 0, None, "adamw_small")
    r_small = [r[0] for r in r_small]
    loss = jnp.sum(r_small[0][14])

    outs = [loss, dx[None]]
    for i in range(4):
        n_, dwb, lg, lb, ao, co, q_, k_ = _unpack_small(r_small[i])
        outs += [n_, r_win[i], q_, k_, r_dwk[i], dwb, lg, lb, r_wpw[i], ao, co, r_wout[i]]
    return tuple(outs)


def local_step(xs, target, weights, mixer_weights, dwk_f, norm_g, q_norm_g, k_norm_g, dw_bias, conv_ln_g, conv_ln_b,
               att_out_g, conv_out_g, send_grads, win_grads):
    S, D = xs.shape
    cos2, sin2 = _rope_tables(S)
    dwk_f = jnp.pad(dwk_f, ((0, 0), (0, CONV_PAD - CONV_WIDTH), (0, 0)))

    def vec(p, l, zero=None):
        row = p[l].reshape(1, -1)
        return row if zero is None else row + zero

    saved = []
    cur = xs
    for l in range(DEPTH):
        wint, zero = weights(l, cur)
        proj, h = in_proj(cur, vec(norm_g, l, zero), wint, f"in_proj_{l}")
        qs, ks, vs = qk_prep(proj, vec(q_norm_g, l), vec(k_norm_g, l), cos2, sin2, f"qk_prep_{l}")
        os_, lses = [], []
        for i, d in enumerate(DILATIONS):
            o, lse = attn_fwd(qs[i], ks[i], vs[i], d, f"attn_fwd_{l}_d{d}")
            os_.append(o)
            lses.append(_from_branch(lse))
        att, cat, lse = att_combine(os_, lses, proj, vec(att_out_g, l), f"att_combine_{l}")
        wpw, wout, zero = mixer_weights(l, lse)
        cat, z, conv, y = conv_fwd(proj, dwk_f[l], vec(dw_bias, l, zero), vec(conv_ln_g, l), vec(conv_ln_b, l),
                                   vec(conv_out_g, l), wpw, cat, f"conv_fwd_{l}")
        nxt = mm_nn(cat, wout, F32, f"out_proj_{l}", add=cur)
        saved.append(dict(x=cur, proj=proj, h=h, qs=qs, ks=ks, vs=vs, att=att, lse=lse, cat=cat, z=z, conv=conv, y=y,
                          wint=wint, wpw=wpw, wout=wout))
        cur = nxt

    dx, dxb, loss_cols = loss_head(cur, target, "loss_head")

    small = {k: [None] * DEPTH for k in ("norm_g", "dw_bias", "conv_ln_g", "conv_ln_b", "att_out_g", "conv_out_g", "q", "k")}
    for l in reversed(range(DEPTH)):
        sv = saved[l]
        proj = sv["proj"]
        dcat = mm_nt(dxb, sv["wout"], F32, f"dcat_{l}")
        g_wout = mm_nn(sv["cat"].T, dxb, BF16, f"dwout_{l}")
        dproj, small["conv_out_g"][l], dconv = gate_bwd(dcat, 1, sv["conv"], proj, 6, vec(conv_out_g, l), False, None,
                                                        f"conv_gate_bwd_{l}")
        dy, small["conv_ln_g"][l], small["conv_ln_b"][l], small["dw_bias"][l] = conv_bwd_ln(
            dconv, sv["wpw"], sv["y"], vec(conv_ln_g, l), vec(conv_ln_b, l), f"conv_bwd_ln_{l}")
        g_wpw = mm_nn(sv["z"].T, dconv, BF16, f"dwpw_{l}")
        dproj, gw = conv_bwd_dw(dy, proj, dwk_f[l], dproj, f"conv_bwd_dw_{l}")
        g_dwk = jnp.sum(gw, axis=1)[:CONV_WIDTH]
        zero = send_grads(l, [g_wpw, g_wout, g_dwk], [0, 0, 1], "mixer")
        dproj, small["att_out_g"][l], *datts, delta = gate_bwd(dcat, 0, sv["att"], proj, 3, vec(att_out_g, l, zero), True,
                                                                dproj, f"att_gate_bwd_{l}")
        dqs, dks, dvs = [], [], []
        for i, d in enumerate(DILATIONS):
            dq, dk, dv = attn_bwd(sv["qs"][i], sv["ks"][i], sv["vs"][i], datts[i], _to_branch(sv["lse"], d),
                                  _to_branch(delta, d), d, f"attn_bwd_{l}_d{d}")
            dqs.append(dq)
            dks.append(dk)
            dvs.append(dv)
        dproj, small["q"][l], small["k"][l] = qk_prep_bwd(dqs, dks, dvs, proj, vec(q_norm_g, l), vec(k_norm_g, l),
                                                          cos2, sin2, dproj, f"qk_prep_bwd_{l}")
        zero = win_grads(l, sv["h"].T, dproj)
        dx, dxb, small["norm_g"][l] = in_proj_bwd(dproj, sv["wint"], sv["x"], vec(norm_g, l, zero), dx, f"in_proj_bwd_{l}")

    return dx, loss_cols, small
```

```python
import jax
import jax.numpy as jnp
from jax import lax
from jax.experimental import pallas as pl
from jax.experimental.pallas import tpu as pltpu

F32 = jnp.float32
BF16 = jnp.bfloat16
SDS = jax.ShapeDtypeStruct
MESH = pl.DeviceIdType.MESH

N_DEV = 8
DEPTH = 2
HEAD_DIM = 128
CONV_WIDTH = 31
CONV_PAD = 32
DILATIONS = (1, 4, 16)
Q_BLOCK = 128
ROPE_THETA = 10000.0
EPS = 1e-6
NEG = -1e30
ADAM_LR, ADAM_B1, ADAM_B2, ADAM_EPS, ADAM_WD, ADAM_STEP = 0.001, 0.9, 0.999, 1e-08, 0.01, 10
LANE = 128
ROW_CHUNK = 64
MIB = 1 << 20
NT = (((1,), (1,)), ((), ()))
TN = (((0,), (0,)), ((), ()))


def _call(body, **kw):
    return pl.pallas_call(body, **kw)


def _params(vmem_mib):
    return pltpu.CompilerParams(vmem_limit_bytes=vmem_mib * MIB)


def _tile(dim, pref, mult):
    t = min(pref, dim)
    while dim % t or t % mult:
        t -= mult
    return t


def _sig(v):
    return jax.nn.sigmoid(v)


def _rstd(v):
    return lax.rsqrt(jnp.mean(v * v, axis=-1, keepdims=True) + EPS)


def _row(tm, cb, c=0):
    return pl.BlockSpec((tm, cb), lambda i: (i, c))


def _full(shape):
    return pl.BlockSpec(shape, lambda i: (0,) * len(shape))


def _halo_prev(tm, cb, c=0):
    k = tm // CONV_PAD
    return pl.BlockSpec((CONV_PAD, cb), lambda i: (jnp.maximum(i * k - 1, 0), c))


def _halo_next(tm, cb, nblk, c=0):
    k = tm // CONV_PAD
    return pl.BlockSpec((CONV_PAD, cb), lambda i: (jnp.minimum((i + 1) * k, nblk - 1), c))


def in_proj(x, g, wt, name):
    S, D = x.shape
    N = wt.shape[0]
    tm, tn = _tile(S, 512, 8), _tile(N, 1024, LANE)

    def body(x_ref, g_ref, w_ref, o_ref, h_ref):
        @pl.when(pl.program_id(1) == 0)
        def _():
            xf = x_ref[...]
            h_ref[...] = (xf * _rstd(xf) * g_ref[...]).astype(BF16)

        o_ref[...] = lax.dot_general(h_ref[...], w_ref[...], NT, preferred_element_type=F32)

    return _call(
        body, name=name, grid=(S // tm, N // tn),
        in_specs=[pl.BlockSpec((tm, D), lambda i, j: (i, 0)), pl.BlockSpec((1, D), lambda i, j: (0, 0)),
                  pl.BlockSpec((tn, D), lambda i, j: (j, 0))],
        out_specs=[pl.BlockSpec((tm, tn), lambda i, j: (i, j)), pl.BlockSpec((tm, D), lambda i, j: (i, 0))],
        out_shape=[SDS((S, N), F32), SDS((S, D), BF16)],
        compiler_params=_params(40),
    )(x, g, wt)


def mm_nn(a, b, out_dtype, name, add=None, owners=1):
    M, K = a.shape
    N = b.shape[1]
    tm, tn = _tile(M, 1024 if owners > 1 else 512, 8), _tile(N // owners, 1024, LANE)
    per = N // owners // tn

    def body(*refs):
        a_ref, b_ref = refs[0], refs[1]
        o_ref = refs[-1]
        acc = jnp.dot(a_ref[...], b_ref[...], preferred_element_type=F32)
        if add is not None:
            acc = acc + refs[2][...]
        o_ref[...] = acc.astype(out_dtype)

    in_specs = [pl.BlockSpec((tm, K), lambda i, j: (i, 0)), pl.BlockSpec((K, tn), lambda i, j: (0, j))]
    args = [a, b]
    if add is not None:
        in_specs.append(pl.BlockSpec((tm, tn), lambda i, j: (i, j)))
        args.append(add)
    if owners > 1:
        out_spec = pl.BlockSpec((None, tm, tn), lambda i, j: (j // per, i, j % per))
        out_shape = SDS((owners, M, N // owners), out_dtype)
    else:
        out_spec, out_shape = pl.BlockSpec((tm, tn), lambda i, j: (i, j)), SDS((M, N), out_dtype)
    return _call(
        body, name=name, grid=(M // tm, N // tn), in_specs=in_specs, out_specs=out_spec, out_shape=out_shape,
        compiler_params=_params(48),
    )(*args)


def mm_core_blocks(a, b, core, name):
    M, K = a.shape
    blk = b.shape[1] // N_DEV
    tm, tn = _tile(M, 1024, 8), _tile(blk, 1024, LANE)
    per = blk // tn

    def body(core_ref, a_ref, b_ref, o_ref):
        o_ref[...] = jnp.dot(a_ref[...], b_ref[...], preferred_element_type=F32).astype(BF16)

    return _call(
        body, name=name,
        grid_spec=pltpu.PrefetchScalarGridSpec(
            num_scalar_prefetch=1, grid=(M // tm, 4 * per),
            in_specs=[pl.BlockSpec((tm, K), lambda i, j, core_ref: (i, 0)),
                      pl.BlockSpec((K, tn), lambda i, j, core_ref: (0, (2 * (j // per) + core_ref[0]) * per + j % per))],
            out_specs=pl.BlockSpec((None, tm, tn), lambda i, j, core_ref: (j // per, i, j % per))),
        out_shape=SDS((4, M, blk), BF16), compiler_params=_params(48),
    )(core, a, b)


def add_blocks(a, b, name):
    n, R, C = a.shape
    tr = _tile(R, 512, 16)
    spec = pl.BlockSpec((None, tr, C), lambda k, i: (k, i, 0))

    def body(a_ref, b_ref, o_ref):
        o_ref[...] = (a_ref[...].astype(F32) + b_ref[...].astype(F32)).astype(BF16)

    return _call(body, name=name, grid=(n, R // tr), in_specs=[spec, spec], out_specs=spec, out_shape=SDS(a.shape, BF16),
                 compiler_params=_params(32))(a, b)


def mm_nt(a, b, out_dtype, name):
    M, K = a.shape
    N = b.shape[0]
    tm, tn = _tile(M, 512, 8), _tile(N, 1024, LANE)

    def body(a_ref, b_ref, o_ref):
        o_ref[...] = lax.dot_general(a_ref[...], b_ref[...], NT, preferred_element_type=F32).astype(out_dtype)

    return _call(
        body, name=name, grid=(M // tm, N // tn),
        in_specs=[pl.BlockSpec((tm, K), lambda i, j: (i, 0)), pl.BlockSpec((tn, K), lambda i, j: (j, 0))],
        out_specs=pl.BlockSpec((tm, tn), lambda i, j: (i, j)), out_shape=SDS((M, N), out_dtype),
        compiler_params=_params(40),
    )(a, b)


def in_proj_bwd(dproj, wt, x, g, dx_out, name):
    S, K = dproj.shape
    D = wt.shape[1]
    tm, tk = _tile(S, 512, 8), _tile(K, 1024, LANE)
    nk = K // tk

    def body(dp_ref, w_ref, x_ref, g_ref, dxo_ref, dx_ref, dxb_ref, gacc_ref):
        i, k = pl.program_id(0), pl.program_id(1)
        part = jnp.dot(dp_ref[...], w_ref[...], preferred_element_type=F32)

        @pl.when(k == 0)
        def _():
            dx_ref[...] = part

        @pl.when(k > 0)
        def _():
            dx_ref[...] += part

        @pl.when((k == 0) & (i == 0))
        def _():
            gacc_ref[...] = jnp.zeros_like(gacc_ref)

        @pl.when(k == nk - 1)
        def _():
            dh = dx_ref[...]
            xf = x_ref[...]
            r = _rstd(xf)
            n = xf * r
            gacc_ref[...] += jnp.sum(dh * n, axis=0, keepdims=True)
            dn = dh * g_ref[...]
            dx = r * (dn - n * jnp.mean(dn * n, axis=-1, keepdims=True)) + dxo_ref[...]
            dx_ref[...] = dx
            dxb_ref[...] = dx.astype(BF16)

    return _call(
        body, name=name, grid=(S // tm, nk),
        in_specs=[pl.BlockSpec((tm, tk), lambda i, k: (i, k)), pl.BlockSpec((tk, D), lambda i, k: (k, 0)),
                  pl.BlockSpec((tm, D), lambda i, k: (i, 0)), pl.BlockSpec((1, D), lambda i, k: (0, 0)),
                  pl.BlockSpec((tm, D), lambda i, k: (i, 0))],
        out_specs=[pl.BlockSpec((tm, D), lambda i, k: (i, 0)), pl.BlockSpec((tm, D), lambda i, k: (i, 0)),
                   pl.BlockSpec((1, D), lambda i, k: (0, 0))],
        out_shape=[SDS((S, D), F32), SDS((S, D), BF16), SDS((1, D), F32)],
        compiler_params=_params(54),
    )(dproj, wt, x, g, dx_out)


def _dil_specs(S, DA, tm, dtype):
    specs = [pl.BlockSpec((tm // d, d * DA), lambda i: (i, 0)) for d in DILATIONS]
    shapes = [SDS((S // d, d * DA), dtype) for d in DILATIONS]
    return specs, shapes


def _head_buf(tm, DA):
    return pltpu.VMEM((DA // HEAD_DIM, tm, HEAD_DIM), F32)


def _emit_dilated(buf_ref, dsts, tm, DA):
    for d, dst in zip(DILATIONS, dsts):
        for h in range(DA // HEAD_DIM):
            for r in range(d):
                rows = slice(None) if d == 1 else pl.ds(r, tm // d, stride=d)
                dst[:, r * DA + h * HEAD_DIM:r * DA + (h + 1) * HEAD_DIM] = buf_ref.at[h][rows, :].astype(BF16)


def _collect_dilated(acc_ref, parts, tm, DA):
    for d, p in zip(DILATIONS, parts):
        for h in range(DA // HEAD_DIM):
            for r in range(d):
                part = p[:, r * DA + h * HEAD_DIM:r * DA + (h + 1) * HEAD_DIM].astype(F32)
                if d == 1:
                    acc_ref[h] = part
                else:
                    rows = pl.ds(r, tm // d, stride=d)
                    acc_ref.at[h][rows, :] = acc_ref.at[h][rows, :] + part


def qk_prep(proj, gq, gk, cos2, sin2, name):
    S = proj.shape[0]
    DA = proj.shape[1] // 7
    H = DA // HEAD_DIM
    tm = _tile(S, 256, 16 * DILATIONS[-1])
    nd = len(DILATIONS)

    def body(q_ref, k_ref, v_ref, gq_ref, gk_ref, c_ref, s_ref, *rest):
        outs, buf_ref = rest[:3 * nd], rest[3 * nd]
        ct, st = c_ref[...], s_ref[...]
        for t, (src, g_ref) in enumerate(((q_ref, gq_ref), (k_ref, gk_ref))):
            gain = g_ref[...]
            for h in range(H):
                sl = slice(h * HEAD_DIM, (h + 1) * HEAD_DIM)
                xh = src[:, sl]
                n = xh * _rstd(xh) * gain
                buf_ref[h] = n * ct + pltpu.roll(n, HEAD_DIM // 2, 1) * st
            _emit_dilated(buf_ref, outs[t * nd:(t + 1) * nd], tm, DA)
        for h in range(H):
            buf_ref[h] = v_ref[:, h * HEAD_DIM:(h + 1) * HEAD_DIM]
        _emit_dilated(buf_ref, outs[2 * nd:], tm, DA)

    specs, shapes = _dil_specs(S, DA, tm, BF16)
    outs = _call(
        body, name=name, grid=(S // tm,),
        in_specs=[_row(tm, DA, 0), _row(tm, DA, 1), _row(tm, DA, 2), _full((1, HEAD_DIM)), _full((1, HEAD_DIM)),
                  _row(tm, HEAD_DIM), _row(tm, HEAD_DIM)],
        out_specs=specs * 3, out_shape=shapes * 3, scratch_shapes=[_head_buf(tm, DA)],
        compiler_params=_params(48),
    )(proj, proj, proj, gq, gk, cos2, sin2)
    return outs[:nd], outs[nd:2 * nd], outs[2 * nd:]


def qk_prep_bwd(dqs, dks, dvs, proj, gq, gk, cos2, sin2, dproj, name):
    S = proj.shape[0]
    DA = proj.shape[1] // 7
    H = DA // HEAD_DIM
    tm = _tile(S, 256, 16 * DILATIONS[-1])
    nb = len(dqs)

    def body(*refs):
        dq_refs, dk_refs, dv_refs = refs[:nb], refs[nb:2 * nb], refs[2 * nb:3 * nb]
        q_ref, k_ref, gq_ref, gk_ref, c_ref, s_ref = refs[3 * nb:3 * nb + 6]
        out_ref, gqa_ref, gka_ref, acc_ref = refs[3 * nb + 7:]
        ct, st = c_ref[...], s_ref[...]

        @pl.when(pl.program_id(0) == 0)
        def _():
            gqa_ref[...] = jnp.zeros_like(gqa_ref)
            gka_ref[...] = jnp.zeros_like(gka_ref)

        for parts, x_ref, g_ref, col, gacc in ((dq_refs, q_ref, gq_ref, 0, gqa_ref),
                                               (dk_refs, k_ref, gk_ref, DA, gka_ref)):
            gain = g_ref[...]
            gsum = jnp.zeros((1, HEAD_DIM), F32)
            _collect_dilated(acc_ref, parts, tm, DA)
            for h in range(H):
                sl = slice(h * HEAD_DIM, (h + 1) * HEAD_DIM)
                dout = acc_ref[h]
                dn = dout * ct + pltpu.roll(dout * st, HEAD_DIM // 2, 1)
                xh = x_ref[:, sl]
                r = _rstd(xh)
                xn = xh * r
                gsum = gsum + jnp.sum(dn * xn, axis=0, keepdims=True)
                dnn = dn * gain
                dx = r * (dnn - xn * jnp.mean(dnn * xn, axis=-1, keepdims=True))
                out_ref[:, col + h * HEAD_DIM:col + (h + 1) * HEAD_DIM] = dx.astype(BF16)
            gacc[...] += gsum
        _collect_dilated(acc_ref, dv_refs, tm, DA)
        for h in range(H):
            out_ref[:, 2 * DA + h * HEAD_DIM:2 * DA + (h + 1) * HEAD_DIM] = acc_ref[h].astype(BF16)

    specs, _ = _dil_specs(S, DA, tm, BF16)
    return _call(
        body, name=name, grid=(S // tm,),
        in_specs=specs * 3 + [_row(tm, DA, 0), _row(tm, DA, 1), _full((1, HEAD_DIM)),
                              _full((1, HEAD_DIM)), _row(tm, HEAD_DIM), _row(tm, HEAD_DIM), pl.BlockSpec(memory_space=pl.ANY)],
        out_specs=[_row(tm, 3 * DA)] + [_full((1, HEAD_DIM))] * 2,
        out_shape=[SDS((S, 7 * DA), BF16)] + [SDS((1, HEAD_DIM), F32)] * 2,
        input_output_aliases={3 * nb + 6: 0},
        scratch_shapes=[_head_buf(tm, DA)],
        compiler_params=_params(48),
    )(*dqs, *dks, *dvs, proj, proj, gq, gk, cos2, sin2, dproj)


def _band_mask(n):
    row = lax.broadcasted_iota(jnp.int32, (Q_BLOCK, 2 * Q_BLOCK), 0)
    col = lax.broadcasted_iota(jnp.int32, (Q_BLOCK, 2 * Q_BLOCK), 1)
    first = jnp.where(n > 0, Q_BLOCK, 2 * Q_BLOCK + 1)
    return (col <= row) | ((col - row) >= first)


def attn_fwd(qh, kh, vb, d, name):
    L = qh.shape[0]
    DA = qh.shape[1] // d
    H = DA // HEAD_DIM
    nb = L // Q_BLOCK
    scale = HEAD_DIM ** -0.5
    view = (L, d * DA)

    def body(q_ref, kc_ref, kp_ref, vc_ref, vp_ref, o_ref, lse_ref):
        mask = _band_mask(pl.program_id(1))
        ones = jnp.ones((2 * Q_BLOCK, HEAD_DIM), BF16)
        for h in range(H):
            sl = slice(h * HEAD_DIM, (h + 1) * HEAD_DIM)
            keys = jnp.concatenate([kc_ref[:, sl], kp_ref[:, sl]], axis=0)
            s = lax.dot_general(q_ref[:, sl], keys, NT, preferred_element_type=F32) * scale
            s = jnp.where(mask, s, NEG)
            m = jnp.max(s, axis=-1, keepdims=True)
            p = jnp.exp(s - m).astype(BF16)
            vals = jnp.concatenate([jnp.concatenate([vc_ref[:, sl], vp_ref[:, sl]], axis=0), ones], axis=1)
            ol = jnp.dot(p, vals, preferred_element_type=F32)
            l = ol[:, HEAD_DIM:]
            o_ref[:, sl] = ol[:, :HEAD_DIM] / l
            lse_ref[0, :, h:h + 1] = m + jnp.log(l[:, 0:1])

    cur = pl.BlockSpec((Q_BLOCK, DA), lambda r, n: (n, r))
    prev = pl.BlockSpec((Q_BLOCK, DA), lambda r, n: (jnp.maximum(n - 1, 0), r))
    o, lse = _call(
        body, name=name, grid=(d, nb), in_specs=[cur, cur, prev, cur, prev],
        out_specs=[cur, pl.BlockSpec((1, Q_BLOCK, H), lambda r, n: (r, n, 0))],
        out_shape=[SDS(view, F32), SDS((d, L, H), F32)],
        compiler_params=_params(32),
    )(qh, kh, kh, vb, vb)
    return o, lse


def attn_bwd(qh, kh, vb, da, lse_d, delta_d, d, name):
    L = qh.shape[0]
    DA = qh.shape[1] // d
    H = DA // HEAD_DIM
    nb = L // Q_BLOCK
    scale = HEAD_DIM ** -0.5
    view = (L, d * DA)
    resident = d > 1

    def body(q_ref, kc_ref, kp_ref, vc_ref, vp_ref, do_ref, lse_ref, dl_ref, dq_ref, dk_ref, dv_ref, dkc_ref, dvc_ref):
        n = pl.program_id(1)
        mask = _band_mask(n)
        if resident:
            done = pl.ds(pl.multiple_of(jnp.maximum(n - 1, 0) * Q_BLOCK, Q_BLOCK), Q_BLOCK)
            fresh = pl.ds(pl.multiple_of(n * Q_BLOCK, Q_BLOCK), Q_BLOCK)
        else:
            done = slice(None)

        @pl.when(n == 0)
        def _():
            dkc_ref[...] = jnp.zeros_like(dkc_ref)
            dvc_ref[...] = jnp.zeros_like(dvc_ref)

        @pl.when(n < nb)
        def _():
            for h in range(H):
                sl = slice(h * HEAD_DIM, (h + 1) * HEAD_DIM)
                q, do = q_ref[:, sl], do_ref[:, sl]
                keys = jnp.concatenate([kc_ref[:, sl], kp_ref[:, sl]], axis=0)
                vals = jnp.concatenate([vc_ref[:, sl], vp_ref[:, sl]], axis=0)
                lse = jnp.broadcast_to(lse_ref[0, :, h:h + 1], (Q_BLOCK, 2 * Q_BLOCK))
                dl = jnp.broadcast_to(dl_ref[0, :, h:h + 1], (Q_BLOCK, 2 * Q_BLOCK))
                s = lax.dot_general(q, keys, NT, preferred_element_type=F32) * scale
                p = jnp.exp(jnp.where(mask, s, NEG) - lse)
                dp = lax.dot_general(do, vals, NT, preferred_element_type=F32)
                ds = (p * (dp - dl) * scale).astype(BF16)
                dq_ref[:, sl] = jnp.dot(ds, keys, preferred_element_type=F32).astype(BF16)
                dk = lax.dot_general(ds, q, TN, preferred_element_type=F32)
                dv = lax.dot_general(p.astype(BF16), do, TN, preferred_element_type=F32)
                dk_ref[done, sl] = (dkc_ref[:, sl] + dk[Q_BLOCK:]).astype(BF16)
                dv_ref[done, sl] = (dvc_ref[:, sl] + dv[Q_BLOCK:]).astype(BF16)
                dkc_ref[:, sl] = dk[:Q_BLOCK]
                dvc_ref[:, sl] = dv[:Q_BLOCK]
                if resident:
                    dk_ref[fresh, sl] = dk[:Q_BLOCK].astype(BF16)
                    dv_ref[fresh, sl] = dv[:Q_BLOCK].astype(BF16)

        if not resident:
            @pl.when(n == nb)
            def _():
                dk_ref[...] = dkc_ref[...].astype(BF16)
                dv_ref[...] = dvc_ref[...].astype(BF16)

    steps = nb if resident else nb + 1
    cur = pl.BlockSpec((Q_BLOCK, DA), lambda r, n: (jnp.minimum(n, nb - 1), r))
    prev = pl.BlockSpec((Q_BLOCK, DA), lambda r, n: (jnp.clip(n - 1, 0, nb - 1), r))
    if resident:
        keyside = pl.BlockSpec((L, DA), lambda r, n: (0, r))
    else:
        keyside = pl.BlockSpec((Q_BLOCK, DA), lambda r, n: (jnp.maximum(n - 1, 0), r))
    stat = pl.BlockSpec((1, Q_BLOCK, H), lambda r, n: (r, jnp.minimum(n, nb - 1), 0))
    dq, dk, dv = _call(
        body, name=name, grid=(d, steps), in_specs=[cur, cur, prev, cur, prev, cur, stat, stat],
        out_specs=[cur, keyside, keyside], out_shape=[SDS(view, BF16)] * 3,
        scratch_shapes=[pltpu.VMEM((Q_BLOCK, DA), F32), pltpu.VMEM((Q_BLOCK, DA), F32)],
        compiler_params=_params(40),
    )(qh, kh, kh, vb, vb, da, lse_d, delta_d)
    return dq, dk, dv


def _to_branch(stat, d):
    S, H = stat.shape
    return stat.reshape(S // d, d, H).transpose(1, 0, 2)


def _from_branch(stat):
    d, L, H = stat.shape
    return stat.transpose(1, 0, 2).reshape(L * d, H)


def att_combine(os_, lses, proj, gain, name):
    S, DA = os_[0].shape
    H = DA // HEAD_DIM
    tm = _tile(S, 256, 16 * DILATIONS[-1])
    nb = len(os_)

    def body(*refs):
        o_views, l_refs = refs[:nb], refs[nb:2 * nb]
        gate_ref, gain_ref, att_ref, y_ref, lse_ref = refs[2 * nb:2 * nb + 5]
        bufs = refs[2 * nb + 5:]
        for d, view, buf in zip(DILATIONS[1:], o_views[1:], bufs):
            for h in range(H):
                for r in range(d):
                    buf.at[h][pl.ds(r, tm // d, stride=d), :] = view[:, r * DA + h * HEAD_DIM:r * DA + (h + 1) * HEAD_DIM]
        ls = [r[...] for r in l_refs]
        top = ls[0]
        for l in ls[1:]:
            top = jnp.maximum(top, l)
        den = jnp.exp(ls[0] - top)
        for l in ls[1:]:
            den = den + jnp.exp(l - top)
        lse = top + jnp.log(den)
        lse_ref[...] = lse
        ws = [jnp.exp(l - lse) for l in ls]
        for h in range(H):
            sl = slice(h * HEAD_DIM, (h + 1) * HEAD_DIM)
            acc = ws[0][:, h:h + 1] * o_views[0][:, sl]
            for w, buf in zip(ws[1:], bufs):
                acc = acc + w[:, h:h + 1] * buf[h]
            att_ref[:, sl] = acc
        a = att_ref[...]
        g = gate_ref[...]
        y_ref[...] = (a * _rstd(a) * gain_ref[...] * (g * _sig(g))).astype(BF16)

    specs, _ = _dil_specs(S, DA, tm, F32)
    return _call(
        body, name=name, grid=(S // tm,),
        in_specs=specs + [_row(tm, H)] * nb + [_row(tm, DA, 3), _full((1, DA))],
        out_specs=[_row(tm, DA), _row(tm, DA), _row(tm, H)],
        out_shape=[SDS((S, DA), F32), SDS((S, 2 * DA), BF16), SDS((S, H), F32)],
        scratch_shapes=[_head_buf(tm, DA)] * (nb - 1),
        compiler_params=_params(48),
    )(*os_, *lses, proj, gain)


def gate_bwd(dcat, cblk, a, proj, gate_blk, gain, dilated, dproj, name):
    S, DA = a.shape
    H = DA // HEAD_DIM
    tm = _tile(S, 256, 16 * DILATIONS[-1])
    nd = len(DILATIONS) if dilated else 1

    def body(dy_ref, a_ref, gate_ref, gain_ref, *rest):
        dg_ref, gacc_ref, *rest = rest[0 if dproj is None else 1:]

        @pl.when(pl.program_id(0) == 0)
        def _():
            gacc_ref[...] = jnp.zeros_like(gacc_ref)

        dy, av, g, gain_v = dy_ref[...], a_ref[...], gate_ref[...], gain_ref[...]
        r = _rstd(av)
        n = av * r
        sg = _sig(g)
        dg_ref[...] = (dy * (n * gain_v) * (sg * (1.0 + g * (1.0 - sg)))).astype(BF16)
        drn = dy * (g * sg)
        gacc_ref[...] += jnp.sum(drn * n, axis=0, keepdims=True)
        dn = drn * gain_v
        da = r * (dn - n * jnp.mean(dn * n, axis=-1, keepdims=True))
        if dilated:
            da_refs, delta_ref, buf_ref = rest[:nd], rest[nd], rest[nd + 1]
            for h in range(H):
                buf_ref[h] = da[:, h * HEAD_DIM:(h + 1) * HEAD_DIM]
            _emit_dilated(buf_ref, da_refs, tm, DA)
            prod = da * av
            for h in range(H):
                delta_ref[:, h:h + 1] = jnp.sum(prod[:, h * HEAD_DIM:(h + 1) * HEAD_DIM], axis=-1, keepdims=True)
        else:
            rest[0][...] = da.astype(BF16)

    out_specs = [_row(tm, DA, gate_blk), _full((1, DA))]
    out_shape = [SDS((S, 7 * DA), BF16), SDS((1, DA), F32)]
    scratch = []
    carried = [] if dproj is None else [dproj]
    if dilated:
        specs, shapes = _dil_specs(S, DA, tm, BF16)
        out_specs += specs + [_row(tm, H)]
        out_shape += shapes + [SDS((S, H), F32)]
        scratch = [_head_buf(tm, DA)]
    else:
        out_specs.append(_row(tm, DA))
        out_shape.append(SDS((S, DA), BF16))
    return _call(
        body, name=name, grid=(S // tm,),
        in_specs=[_row(tm, DA, cblk), _row(tm, DA), _row(tm, DA, gate_blk), _full((1, DA))] + [pl.BlockSpec(memory_space=pl.ANY)] * len(carried),
        out_specs=out_specs, out_shape=out_shape, scratch_shapes=scratch, compiler_params=_params(48),
        input_output_aliases={4: 0} if carried else {},
    )(dcat, a, proj, gain, *carried)


def _by_sublane_phase(offsets):
    groups = [(p, [o for o in offsets if o % 8 == p]) for p in range(8)]
    return [(p, sorted(os_)) for p, os_ in groups if os_]


def _shifted_rows(src_ref, tmp_ref, base, phase, offsets, lanes):
    if phase == 0:
        return lambda o: src_ref[pl.ds(base + o, ROW_CHUNK), lanes]
    span = offsets[-1] - phase + ROW_CHUNK
    tmp_ref[phase, pl.ds(0, span), :] = src_ref[pl.ds(base + phase, span), lanes]
    return lambda o: tmp_ref[phase, pl.ds(o - phase, ROW_CHUNK), :]


def _shift_scratch():
    return pltpu.VMEM((8, CONV_PAD + ROW_CHUNK, LANE), F32)


def _fill_u(i, a_ref, b_ref, ah_ref, bh_ref, uext_ref, tm):
    uext_ref[pl.ds(CONV_PAD, tm), :] = a_ref[...] * _sig(b_ref[...])
    uh = ah_ref[...] * _sig(bh_ref[...])
    uext_ref[pl.ds(0, CONV_PAD), :] = jnp.where(i > 0, uh, 0.0)


def conv_fwd(proj, wk, bias, ln_g, ln_b, out_g, wpw, cat, name):
    S = proj.shape[0]
    DC = proj.shape[1] // 7
    tm = _tile(S, 128, ROW_CHUNK)
    lead = CONV_PAD - (CONV_WIDTH - 1)

    def body(a_ref, b_ref, ah_ref, bh_ref, gate_ref, wk_ref, bias_ref, lg_ref, lb_ref, og_ref, wpw_ref, cat_ref,
             cy_ref, z_ref, conv_ref, y_ref, uext_ref, tmp_ref):
        _fill_u(pl.program_id(0), a_ref, b_ref, ah_ref, bh_ref, uext_ref, tm)

        def cols(cc, carry):
            c0 = pl.multiple_of(cc * LANE, LANE)
            lanes = pl.ds(c0, LANE)
            for rr in range(tm // ROW_CHUNK):
                acc = jnp.broadcast_to(bias_ref[:, lanes], (ROW_CHUNK, LANE))
                for phase, offsets in _by_sublane_phase(range(lead, lead + CONV_WIDTH)):
                    rows = _shifted_rows(uext_ref, tmp_ref, rr * ROW_CHUNK, phase, offsets, lanes)
                    for o in offsets:
                        acc = acc + wk_ref[o - lead:o - lead + 1, lanes] * rows(o)
                y_ref[pl.ds(rr * ROW_CHUNK, ROW_CHUNK), lanes] = acc
            return carry

        lax.fori_loop(0, DC // LANE, cols, 0)
        y = y_ref[...]
        yc = y - jnp.mean(y, axis=-1, keepdims=True)
        ln = yc * _rstd(yc) * lg_ref[...] + lb_ref[...]
        zb = (ln * _sig(ln)).astype(BF16)
        z_ref[...] = zb
        conv = jnp.dot(zb, wpw_ref[...], preferred_element_type=F32)
        conv_ref[...] = conv
        g = gate_ref[...]
        cy_ref[...] = (conv * _rstd(conv) * og_ref[...] * (g * _sig(g))).astype(BF16)

    vec = _full((1, DC))
    return _call(
        body, name=name, grid=(S // tm,),
        in_specs=[_row(tm, DC, 4), _row(tm, DC, 5), _halo_prev(tm, DC, 4), _halo_prev(tm, DC, 5), _row(tm, DC, 6),
                  _full((CONV_PAD, DC)), vec, vec, vec, vec, _full((DC, DC)), pl.BlockSpec(memory_space=pl.ANY)],
        out_specs=[_row(tm, DC, 1)] + [_row(tm, DC)] * 3,
        out_shape=[SDS((S, 2 * DC), BF16), SDS((S, DC), BF16), SDS((S, DC), F32), SDS((S, DC), F32)],
        input_output_aliases={11: 0},
        scratch_shapes=[pltpu.VMEM((CONV_PAD + tm, DC), F32), _shift_scratch()],
        compiler_params=_params(48),
    )(proj, proj, proj, proj, proj, wk, bias, ln_g, ln_b, out_g, wpw, cat)


def conv_bwd_ln(dconv, wpw, y, ln_g, ln_b, name):
    S, DC = y.shape
    tm = _tile(S, 256, 8)

    def body(dc_ref, wpw_ref, y_ref, lg_ref, lb_ref, dy_ref, glg_ref, glb_ref, gb_ref):
        @pl.when(pl.program_id(0) == 0)
        def _():
            glg_ref[...] = jnp.zeros_like(glg_ref)
            glb_ref[...] = jnp.zeros_like(glb_ref)
            gb_ref[...] = jnp.zeros_like(gb_ref)

        dz = lax.dot_general(dc_ref[...], wpw_ref[...], NT, preferred_element_type=F32)
        yv = y_ref[...]
        yc = yv - jnp.mean(yv, axis=-1, keepdims=True)
        rstd = _rstd(yc)
        yhat = yc * rstd
        ln = yhat * lg_ref[...] + lb_ref[...]
        sg = _sig(ln)
        dln = dz * (sg * (1.0 + ln * (1.0 - sg)))
        glb_ref[...] += jnp.sum(dln, axis=0, keepdims=True)
        glg_ref[...] += jnp.sum(dln * yhat, axis=0, keepdims=True)
        dyh = dln * lg_ref[...]
        dy = rstd * (dyh - jnp.mean(dyh, axis=-1, keepdims=True) - yhat * jnp.mean(dyh * yhat, axis=-1, keepdims=True))
        dy_ref[...] = dy
        gb_ref[...] += jnp.sum(dy, axis=0, keepdims=True)

    vec = _full((1, DC))
    return _call(
        body, name=name, grid=(S // tm,),
        in_specs=[_row(tm, DC), _full((DC, DC)), _row(tm, DC), vec, vec],
        out_specs=[_row(tm, DC), vec, vec, vec],
        out_shape=[SDS((S, DC), F32)] + [SDS((1, DC), F32)] * 3,
        compiler_params=_params(48),
    )(dconv, wpw, y, ln_g, ln_b)


def conv_bwd_dw(dy, proj, wk, dproj, name):
    S, DC = dy.shape
    tm = _tile(S, 128, ROW_CHUNK)
    nsteps = S // tm
    lead = CONV_PAD - (CONV_WIDTH - 1)
    groups = ROW_CHUNK // 8

    def body(dy_ref, dyn_ref, a_ref, b_ref, ah_ref, bh_ref, wk_ref, dproj_ref, dab_ref, gw_ref, uext_ref, dyext_ref, du_ref,
             tmp_dy_ref, tmp_u_ref):
        i = pl.program_id(0)

        @pl.when(i == 0)
        def _():
            gw_ref[...] = jnp.zeros_like(gw_ref)

        _fill_u(i, a_ref, b_ref, ah_ref, bh_ref, uext_ref, tm)
        dyext_ref[pl.ds(0, tm), :] = dy_ref[...]
        dyext_ref[pl.ds(tm, CONV_PAD), :] = jnp.where(i < nsteps - 1, dyn_ref[...], 0.0)

        def cols(cc, carry):
            c0 = pl.multiple_of(cc * LANE, LANE)
            lanes = pl.ds(c0, LANE)
            for rr in range(tm // ROW_CHUNK):
                base = rr * ROW_CHUNK
                acc = jnp.zeros((ROW_CHUNK, LANE), F32)
                for phase, offsets in _by_sublane_phase(range(CONV_WIDTH)):
                    rows = _shifted_rows(dyext_ref, tmp_dy_ref, base, phase, offsets, lanes)
                    for o in offsets:
                        j = CONV_WIDTH - 1 - o
                        acc = acc + wk_ref[j:j + 1, lanes] * rows(o)
                du_ref[pl.ds(base, ROW_CHUNK), lanes] = acc
                dyc = dyext_ref[pl.ds(base, ROW_CHUNK), lanes]
                for phase, offsets in _by_sublane_phase(range(lead, lead + CONV_WIDTH)):
                    rows = _shifted_rows(uext_ref, tmp_u_ref, base, phase, offsets, lanes)
                    for o in offsets:
                        prod = dyc * rows(o)
                        part = prod[0:8]
                        for k in range(1, groups):
                            part = part + prod[8 * k:8 * k + 8]
                        gw_ref[o - lead, :, lanes] += part
            return carry

        lax.fori_loop(0, DC // LANE, cols, 0)
        du = du_ref[...]
        sb = _sig(b_ref[...])
        dab_ref[:, :DC] = (du * sb).astype(BF16)
        dab_ref[:, DC:] = (du * a_ref[...] * sb * (1.0 - sb)).astype(BF16)

    return _call(
        body, name=name, grid=(nsteps,),
        in_specs=[_row(tm, DC), _halo_next(tm, DC, S // CONV_PAD), _row(tm, DC, 4), _row(tm, DC, 5),
                  _halo_prev(tm, DC, 4), _halo_prev(tm, DC, 5), _full((CONV_PAD, DC)), pl.BlockSpec(memory_space=pl.ANY)],
        out_specs=[_row(tm, 2 * DC, 2), _full((CONV_PAD, 8, DC))],
        out_shape=[SDS((S, 7 * DC), BF16), SDS((CONV_PAD, 8, DC), F32)],
        input_output_aliases={7: 0},
        scratch_shapes=[pltpu.VMEM((CONV_PAD + tm, DC), F32), pltpu.VMEM((tm + CONV_PAD, DC), F32), pltpu.VMEM((tm, DC), F32),
                        _shift_scratch(), _shift_scratch()],
        compiler_params=_params(40),
    )(dy, dy, proj, proj, proj, proj, wk, dproj)


def loss_head(xo, target, name):
    S, D = xo.shape
    tm = _tile(S, 256, 8)

    def body(x_ref, t_ref, dy_ref, dyb_ref, acc_ref):
        @pl.when(pl.program_id(0) == 0)
        def _():
            acc_ref[...] = jnp.zeros_like(acc_ref)

        err = x_ref[...] - t_ref[...]
        dy = err * (1.0 / D)
        dy_ref[...] = dy
        dyb_ref[...] = dy.astype(BF16)
        acc_ref[...] += jnp.sum(err * dy, axis=0, keepdims=True) * 0.5

    return _call(
        body, name=name, grid=(S // tm,), in_specs=[_row(tm, D), _row(tm, D)],
        out_specs=[_row(tm, D), _row(tm, D), _full((1, D))],
        out_shape=[SDS((S, D), F32), SDS((S, D), BF16), SDS((1, D), F32)],
        compiler_params=_params(32),
    )(xo, target)


def _coords():
    x, y, c = lax.axis_index("x"), lax.axis_index("y"), lax.axis_index("c")
    return x, y, c


def _lin(p):
    return 4 * p[0] + 2 * p[1] + p[2]


def _chip(p):
    return 2 * p[0] + p[1]


def _slot(ref, axis, idx, size):
    index = [slice(None)] * len(ref.shape)
    index[axis] = pl.ds(idx * size, size)
    return ref.at[tuple(index)]


def all_gather(blocks, axes, name):
    na = len(blocks)
    sizes = [b.shape[ax] for b, ax in zip(blocks, axes)]
    fulls = [SDS(b.shape[:ax] + (N_DEV * b.shape[ax],) + b.shape[ax + 1:], b.dtype) for b, ax in zip(blocks, axes)]

    def body(*refs):
        in_refs, out_refs = refs[:na], refs[na:2 * na]
        send_sems, recv_sems, local_sems = refs[2 * na:]
        x, y, c = _coords()
        me, sibling = (x, y, c), (x, y, 1 - c)
        chips = [(1 - x, y), (x, 1 - y), (1 - x, 1 - y)]
        south = c == 0
        relayed = (jnp.where(south, 1 - x, x), jnp.where(south, y, 1 - y), c)
        onward = (jnp.where(south, x, 1 - x), jnp.where(south, 1 - y, y), c)

        def place(a, p):
            return _slot(out_refs[a], axes[a], _lin(p), sizes[a])

        def copy(a, k, block, to, src=None):
            return pltpu.make_async_remote_copy(
                src_ref=place(a, block) if src is None else src, dst_ref=place(a, block),
                send_sem=send_sems.at[a, k], recv_sem=recv_sems.at[a, k], device_id=to, device_id_type=MESH)

        mine = [pltpu.make_async_copy(in_refs[a], place(a, me), local_sems.at[a]) for a in range(na)]
        for cp in mine:
            cp.start()
        first = []
        for a in range(na):
            first.append(copy(a, 0, me, sibling, src=in_refs[a]))
            first += [copy(a, 1 + j, me, (*chip, c), src=in_refs[a]) for j, chip in enumerate(chips[:2])]
        for cp in first:
            cp.start()
        later = []
        for a in range(na):
            for j, chip in enumerate(chips[:2]):
                copy(a, 1 + j, (*chip, c), me).wait_recv()
            later.append(copy(a, 3, relayed, onward))
            later += [copy(a, 4 + j, (*chip, c), sibling) for j, chip in enumerate(chips[:2])]
            for cp in later[-3:]:
                cp.start()
        for a in range(na):
            copy(a, 3, (*chips[2], c), me).wait_recv()
            later.append(copy(a, 6, (*chips[2], c), sibling))
            later[-1].start()
        for a in range(na):
            copy(a, 0, sibling, me).wait_recv()
            for j, chip in enumerate(chips):
                copy(a, 4 + j, (*chip, 1 - c), me).wait_recv()
        for cp in first + later:
            cp.wait_send()
        for cp in mine:
            cp.wait()

    hbm = pl.BlockSpec(memory_space=pltpu.HBM)
    return _call(
        body, name=name, in_specs=[hbm] * na, out_specs=[hbm] * na, out_shape=fulls,
        scratch_shapes=[pltpu.SemaphoreType.DMA((na, 7)), pltpu.SemaphoreType.DMA((na, 7)), pltpu.SemaphoreType.DMA((na,))],
    )(*blocks)


class _Exchange:
    def __init__(self, gather, srcs, axes, name, after, route="all", lands=None):
        self.gather, self.axes, self.name, self.route = gather, axes, name, route
        if route == "pass":
            self.na, self.ns = len(lands), 0
            self.sizes = [l.shape[ax] // N_DEV for l, ax in zip(lands, axes)]
            self.kinds = [pltpu.HBM(l.shape, l.dtype) for l in lands]
            self._start([pltpu.with_memory_space_constraint(t, pltpu.HBM) for t in lands], after)
            return
        self.na = self.ns = len(srcs)
        if gather:
            self.sizes = [s.shape[ax] for s, ax in zip(srcs, axes)]
            lands = [s.shape[:ax] + (N_DEV * s.shape[ax],) + s.shape[ax + 1:] for s, ax in zip(srcs, axes)]
        else:
            self.sizes = [None if ax is None else s.shape[ax] // N_DEV for s, ax in zip(srcs, axes)]
            lands = [s.shape if ax is None else (N_DEV,) + s.shape[:ax] + (sz,) + s.shape[ax + 1:]
                     for s, ax, sz in zip(srcs, axes, self.sizes)]
        self.kinds = [pltpu.HBM(s.shape, s.dtype) for s in srcs] + [pltpu.HBM(l, s.dtype) for l, s in zip(lands, srcs)]
        lands = [lax.empty(l, s.dtype) for l, s in zip(lands, srcs)]
        after = jnp.zeros((8, LANE), F32) if after is None else after
        self._start([pltpu.with_memory_space_constraint(t, pltpu.HBM) for t in list(srcs) + lands], after)

    def _src(self, a, ref, owner):
        if self.gather:
            return ref
        return ref.at[_lin(owner)] if self.axes[a] is None else _slot(ref, self.axes[a], _lin(owner), self.sizes[a])

    def _dst(self, a, land, sender):
        return _slot(land, self.axes[a], _lin(sender), self.sizes[a]) if self.gather else land.at[_lin(sender)]

    def _flips(self):
        if self.route == "all":
            return [(k >> 2 & 1, k >> 1 & 1, k & 1) for k in range(1, N_DEV)]
        if self.route == "chips":
            return [(0, 0, 1), (1, 0, 0), (0, 1, 0), (1, 1, 0)]
        if self.route == "sibling":
            return [(0, 0, 1)] * 4
        return [(1, 0, 0), (0, 1, 0), (1, 1, 0)]

    def _copies(self, refs, send_sems, recv_sems):
        na, ns = self.na, self.ns
        me = _coords()
        flips = self._flips()
        n = len(flips)
        others = [tuple(1 - v if f else v for v, f in zip(me, flip)) for flip in flips]
        sends, arrivals = [], []
        for a in range(na):
            land = refs[ns + a]
            for k, other in enumerate(others):
                if self.route == "pass":
                    peer = (me[0], me[1], 1 - me[2])
                    theirs = (other[0], other[1], 1 - me[2])
                    send = dict(src_ref=self._dst(a, land, other), dst_ref=self._dst(a, land, other))
                    arrive = dict(src_ref=self._dst(a, land, theirs), dst_ref=self._dst(a, land, theirs))
                elif self.route == "sibling":
                    peer = other
                    send = dict(src_ref=refs[a].at[k], dst_ref=land.at[k])
                    arrive = send
                elif self.route == "chip_parts":
                    peer = other
                    send = dict(src_ref=refs[a].at[_chip(peer)], dst_ref=land.at[_chip(me)])
                    arrive = dict(src_ref=refs[a].at[_chip(me)], dst_ref=land.at[_chip(peer)])
                else:
                    peer = other
                    send = dict(src_ref=self._src(a, refs[a], peer), dst_ref=self._dst(a, land, me))
                    arrive = dict(src_ref=self._src(a, refs[a], me), dst_ref=self._dst(a, land, peer))
                pair = dict(send_sem=send_sems.at[n * a + k], recv_sem=recv_sems.at[n * a + k], device_id=peer, device_id_type=MESH)
                sends.append(pltpu.make_async_remote_copy(**send, **pair))
                arrivals.append(pltpu.make_async_remote_copy(**arrive, **pair))
        return sends, arrivals

    def _place_own(self, operands):
        na = self.na
        mine = _chip(_coords()) if self.route == "chip_parts" else _lin(_coords())
        me = jnp.reshape(mine, (1,)).astype(jnp.int32)
        lands = []
        for a in range(na):
            src, land, ax = operands[a], operands[na + a], self.axes[a]
            if ax == 1:
                R, C = src.shape[0], self.sizes[a]
                steps, tile = 1, (R, C)
                in_map = lambda i, me_ref: (0, me_ref[0])
            elif ax is None:
                R, C = src.shape[1:]
                tr = _tile(R, 512, 16)
                steps, tile = R // tr, (None, tr, C)
                in_map = lambda i, me_ref: (me_ref[0], i, 0)
            else:
                R, C = (src.shape[0] if self.gather else self.sizes[a]), src.shape[1]
                tr = _tile(R, 512, 16)
                steps, tile = R // tr, (tr, C)
                in_map = (lambda i, me_ref: (i, 0)) if self.gather else (lambda i, me_ref, n=R // tr: (me_ref[0] * n + i, 0))
            if self.gather:
                out_spec = pl.BlockSpec(tile, lambda i, me_ref, n=steps: (me_ref[0] * n + i, 0))
            else:
                out_spec = pl.BlockSpec((None,) + tuple(t for t in tile if t is not None), lambda i, me_ref: (me_ref[0], i, 0))

            def body(me_ref, src_ref, land_ref, out_ref):
                out_ref[...] = src_ref[...]

            lands.append(_call(
                body, name=f"{self.name}_own{a}",
                grid_spec=pltpu.PrefetchScalarGridSpec(
                    num_scalar_prefetch=1, grid=(steps,),
                    in_specs=[pl.BlockSpec(tile, in_map), pl.BlockSpec(memory_space=pl.ANY)], out_specs=out_spec),
                out_shape=SDS(land.shape, land.dtype), input_output_aliases={2: 0}, compiler_params=_params(32),
            )(me, src, land))
        return operands[:na] + lands

    def _start(self, operands, after):
        nops = self.ns + self.na
        nsem = len(self._flips()) * self.na
        if self.ns and self.route != "sibling":
            operands = self._place_own(operands)

        def body(*refs):
            ins = refs[:nops]
            send_sems, recv_sems, token_ref = refs[nops + 1], refs[nops + 2], refs[2 * nops + 3]
            for cp in self._copies(ins, send_sems, recv_sems)[0]:
                cp.start()
            token_ref[...] = jnp.zeros_like(token_ref)

        hbm = pl.BlockSpec(memory_space=pltpu.HBM)
        sem = pl.BlockSpec(memory_space=pltpu.SEMAPHORE)
        outs = _call(
            body, name=self.name + "_start",
            in_specs=[hbm] * nops + [pl.BlockSpec(memory_space=pl.ANY)],
            out_specs=[sem, sem] + [hbm] * nops + [pl.BlockSpec(memory_space=pltpu.VMEM)],
            out_shape=[pltpu.SemaphoreType.DMA((nsem,)), pltpu.SemaphoreType.DMA((nsem,))] + self.kinds + [SDS((8, LANE), F32)],
            input_output_aliases={i: 2 + i for i in range(nops)},
            compiler_params=pltpu.CompilerParams(has_side_effects=pltpu.SideEffectType.DATAFLOW_SIDE_EFFECTING),
        )(*operands, after)
        self.sems, self.thru, self.token = outs[:2], outs[2:2 + nops], outs[2 + nops][0:1, 0:1]

    def wait(self, after):
        nops = self.ns + self.na

        def body(*refs):
            ins, send_sems, recv_sems = refs[:nops], refs[nops], refs[nops + 1]
            sends, arrivals = self._copies(ins, send_sems, recv_sems)
            for cp in sends:
                cp.wait_send()
            for cp in arrivals:
                cp.wait_recv()

        hbm = pl.BlockSpec(memory_space=pltpu.HBM)
        sem = pl.BlockSpec(memory_space=pltpu.SEMAPHORE)
        outs = _call(
            body, name=self.name + "_wait",
            in_specs=[hbm] * nops + [sem, sem, pl.BlockSpec(memory_space=pl.ANY)],
            out_specs=[hbm] * nops, out_shape=self.kinds,
            input_output_aliases={i: i for i in range(nops)},
            compiler_params=pltpu.CompilerParams(has_side_effects=pltpu.SideEffectType.DATAFLOW_SIDE_EFFECTING),
        )(*self.thru, *self.sems, after)
        return outs[self.ns:]


def adamw(w, m, v, parts, layer, prev, name):
    nl, R, C = w.shape
    nparts = parts.shape[0]
    tr = _tile(R, 128, 8) if R % 8 == 0 else R

    def body(w_ref, m_ref, v_ref, p_ref, *rest):
        g_ref, d_ref, mo_ref, vo_ref = rest[-4:]
        g = p_ref[0].astype(F32)
        for s in range(1, nparts):
            g = g + p_ref[s].astype(F32)
        mn = ADAM_B1 * m_ref[0] + (1.0 - ADAM_B1) * g
        vn = ADAM_B2 * v_ref[0] + (1.0 - ADAM_B2) * (g * g)
        m_hat = mn / (1.0 - ADAM_B1 ** ADAM_STEP)
        v_hat = vn / (1.0 - ADAM_B2 ** ADAM_STEP)
        g_ref[0] = g
        d_ref[0] = -ADAM_LR * (m_hat / (jnp.sqrt(v_hat) + ADAM_EPS) + ADAM_WD * w_ref[0])
        mo_ref[0] = mn
        vo_ref[0] = vn

    row = pl.BlockSpec((1, tr, C), lambda i: (layer, i, 0))
    carried = [] if prev is None else list(prev)
    return _call(
        body, name=name, grid=(R // tr,),
        in_specs=[row, row, row, pl.BlockSpec((nparts, tr, C), lambda i: (0, i, 0))] + [pl.BlockSpec(memory_space=pl.ANY)] * len(carried),
        out_specs=[row] * 4, out_shape=[SDS((nl, R, C), F32)] * 4,
        input_output_aliases={4 + k: k for k in range(len(carried))},
        compiler_params=_params(48),
    )(w, m, v, parts, *carried)


def _rope_tables(S):
    inv_freq = 1.0 / (ROPE_THETA ** (jnp.arange(0, HEAD_DIM, 2, dtype=F32) / HEAD_DIM))
    ang = jnp.arange(S, dtype=F32)[:, None] * inv_freq[None, :]
    cos, sin = jnp.cos(ang), jnp.sin(ang)
    return jnp.concatenate([cos, cos], axis=-1), jnp.concatenate([-sin, sin], axis=-1)


def _pack_small(D, norm_g, dw_bias, conv_ln_g, conv_ln_b, att_out_g, conv_out_g, q_norm_g, k_norm_g, extra=None):
    qk = jnp.concatenate([q_norm_g.reshape(-1), k_norm_g.reshape(-1)])
    qk = jnp.pad(qk, (0, D - qk.shape[0])).reshape(1, D)
    zero = jnp.zeros((1, D), F32)
    return jnp.concatenate([norm_g, dw_bias, conv_ln_g, conv_ln_b, att_out_g, conv_out_g, qk, zero,
                            zero if extra is None else extra, zero], axis=0)


def _unpack_small(p):
    rows = [p[2 * i:2 * i + 2] for i in range(6)]
    qk = p[12, :4 * HEAD_DIM].reshape(2, DEPTH, HEAD_DIM)
    return rows + [qk[0], qk[1]]


def kernel(x, norm_g, w_in, q_norm_g, k_norm_g, dw_kernel, dw_bias, conv_ln_g, conv_ln_b, w_pw, att_out_g, conv_out_g, w_out, loss_target, m_norm_g, m_w_in, m_q_norm_g, m_k_norm_g, m_dw_kernel, m_dw_bias, m_conv_ln_g, m_conv_ln_b, m_w_pw, m_att_out_g, m_conv_out_g, m_w_out, v_norm_g, v_w_in, v_q_norm_g, v_k_norm_g, v_dw_kernel, v_dw_bias, v_conv_ln_g, v_conv_ln_b, v_w_pw, v_att_out_g, v_conv_out_g, v_w_out):
    xs = x[0]
    D = xs.shape[1]
    bf = lambda t, l: t[l].astype(BF16)
    wint0, dwk_f = all_gather([bf(w_in, 0).T, dw_kernel], [0, 2], "gather_first")
    early = _Exchange(True, [bf(w_pw, 0), bf(w_out, 0)], [0, 0], "gather_layer0", after=wint0)
    later = _Exchange(True, [bf(w_in, 1).T, bf(w_pw, 1), bf(w_out, 1)], [0, 0, 0], "gather_layer1", after=early.token,
                      route="chips")
    landed = {}

    def weights(l, cur):
        if l == 0:
            return wint0, later.token
        landed[1] = landed["passing"].wait(cur)
        return landed[1][0], None

    def mixer_weights(l, cur):
        if l == 0:
            mine = early.wait(cur)
            landed["passing"] = _Exchange(True, None, [0, 0, 0], "pass_layer1", after=mine[0], route="pass",
                                          lands=later.wait(mine[0]))
            return (*mine, landed["passing"].token)
        return (*landed[1][1:], None)

    sent = [[] for _ in range(DEPTH)]

    def send_grads(l, grads, axes, tag):
        sent[l].append(_Exchange(False, grads, axes, f"scatter_{tag}_layer{l}", after=None))
        return sent[l][-1].token

    def win_grads(l, ht, dproj):
        if l > 0:
            return send_grads(l, [mm_nn(ht, dproj, BF16, f"dwin_{l}", owners=N_DEV)], [None], "w_in")
        core = jnp.reshape(lax.axis_index("c"), (1,)).astype(jnp.int32)
        theirs = mm_core_blocks(ht, dproj, 1 - core, f"dwin_{l}_for_sibling")
        to_sibling = _Exchange(False, [theirs], [None], f"scatter_w_in_sibling_layer{l}", after=None, route="sibling")
        mine = mm_core_blocks(ht, dproj, core + to_sibling.token[0].astype(jnp.int32), f"dwin_{l}_own")
        (from_sibling,) = to_sibling.wait(mine)
        both = add_blocks(mine, from_sibling, f"dwin_{l}_chip_sum")
        sent[l].append(_Exchange(False, [both], [None], f"scatter_w_in_chips_layer{l}", after=None, route="chip_parts"))
        return sent[l][-1].token

    dx, loss_cols, small = local_step(xs, loss_target[0], weights, mixer_weights, dwk_f, norm_g, q_norm_g, k_norm_g, dw_bias,
                                      conv_ln_g, conv_ln_b, att_out_g, conv_out_g, send_grads, win_grads)

    big = ((w_pw, m_w_pw, v_w_pw), (w_out, m_w_out, v_w_out), (dw_kernel, m_dw_kernel, v_dw_kernel), (w_in, m_w_in, v_w_in))
    results = [None] * len(big)
    after = dx
    for l in reversed(range(DEPTH)):
        parts = [p for ex in sent[l] for p in ex.wait(after)]
        for i, ((w, m, v), p) in enumerate(zip(big, parts)):
            results[i] = adamw(w, m, v, p, l, results[i], f"adamw_{w.shape[1]}_{w.shape[2]}_{l}")
        after = results[0][3]
    r_wpw, r_wout, r_dwk, r_win = results

    stack = lambda k: jnp.concatenate(small[k], axis=0)
    mine = _pack_small(D, stack("norm_g"), stack("dw_bias"), stack("conv_ln_g"), stack("conv_ln_b"), stack("att_out_g"),
                       stack("conv_out_g"), stack("q"), stack("k"), extra=loss_cols)
    (p_small,) = all_gather([mine], [0], "gather_small")
    pk = lambda n, dw, lg, lb, ao, co, q, k: _pack_small(D, n, dw, lg, lb, ao, co, q, k)[None]
    r_small = adamw(pk(norm_g, dw_bias, conv_ln_g, conv_ln_b, att_out_g, conv_out_g, q_norm_g, k_norm_g),
                    pk(m_norm_g, m_dw_bias, m_conv_ln_g, m_conv_ln_b, m_att_out_g, m_conv_out_g, m_q_norm_g, m_k_norm_g),
                    pk(v_norm_g, v_dw_bias, v_conv_ln_g, v_conv_ln_b, v_att_out_g, v_conv_out_g, v_q_norm_g, v_k_norm_g),
                    p_small.reshape(N_DEV, 16, D), 0, None, "adamw_small")
    r_small = [r[0] for r in r_small]
    loss = jnp.sum(r_small[0][14])

    outs = [loss, dx[None]]
    for i in range(4):
        n_, dwb, lg, lb, ao, co, q_, k_ = _unpack_small(r_small[i])
        outs += [n_, r_win[i], q_, k_, r_dwk[i], dwb, lg, lb, r_wpw[i], ao, co, r_wout[i]]
    return tuple(outs)


def local_step(xs, target, weights, mixer_weights, dwk_f, norm_g, q_norm_g, k_norm_g, dw_bias, conv_ln_g, conv_ln_b,
               att_out_g, conv_out_g, send_grads, win_grads):
    S, D = xs.shape
    cos2, sin2 = _rope_tables(S)
    dwk_f = jnp.pad(dwk_f, ((0, 0), (0, CONV_PAD - CONV_WIDTH), (0, 0)))

    def vec(p, l, zero=None):
        row = p[l].reshape(1, -1)
        return row if zero is None else row + zero

    saved = []
    cur = xs
    for l in range(DEPTH):
        wint, zero = weights(l, cur)
        proj, h = in_proj(cur, vec(norm_g, l, zero), wint, f"in_proj_{l}")
        qs, ks, vs = qk_prep(proj, vec(q_norm_g, l), vec(k_norm_g, l), cos2, sin2, f"qk_prep_{l}")
        os_, lses = [], []
        for i, d in enumerate(DILATIONS):
            o, lse = attn_fwd(qs[i], ks[i], vs[i], d, f"attn_fwd_{l}_d{d}")
            os_.append(o)
            lses.append(_from_branch(lse))
        att, cat, lse = att_combine(os_, lses, proj, vec(att_out_g, l), f"att_combine_{l}")
        wpw, wout, zero = mixer_weights(l, lse)
        cat, z, conv, y = conv_fwd(proj, dwk_f[l], vec(dw_bias, l, zero), vec(conv_ln_g, l), vec(conv_ln_b, l),
                                   vec(conv_out_g, l), wpw, cat, f"conv_fwd_{l}")
        nxt = mm_nn(cat, wout, F32, f"out_proj_{l}", add=cur)
        saved.append(dict(x=cur, proj=proj, h=h, qs=qs, ks=ks, vs=vs, att=att, lse=lse, cat=cat, z=z, conv=conv, y=y,
                          wint=wint, wpw=wpw, wout=wout))
        cur = nxt

    dx, dxb, loss_cols = loss_head(cur, target, "loss_head")

    small = {k: [None] * DEPTH for k in ("norm_g", "dw_bias", "conv_ln_g", "conv_ln_b", "att_out_g", "conv_out_g", "q", "k")}
    for l in reversed(range(DEPTH)):
        sv = saved[l]
        proj = sv["proj"]
        dcat = mm_nt(dxb, sv["wout"], F32, f"dcat_{l}")
        g_wout = mm_nn(sv["cat"].T, dxb, BF16, f"dwout_{l}")
        dproj, small["conv_out_g"][l], dconv = gate_bwd(dcat, 1, sv["conv"], proj, 6, vec(conv_out_g, l), False, None,
                                                        f"conv_gate_bwd_{l}")
        dy, small["conv_ln_g"][l], small["conv_ln_b"][l], small["dw_bias"][l] = conv_bwd_ln(
            dconv, sv["wpw"], sv["y"], vec(conv_ln_g, l), vec(conv_ln_b, l), f"conv_bwd_ln_{l}")
        g_wpw = mm_nn(sv["z"].T, dconv, BF16, f"dwpw_{l}")
        dproj, gw = conv_bwd_dw(dy, proj, dwk_f[l], dproj, f"conv_bwd_dw_{l}")
        g_dwk = jnp.sum(gw, axis=1)[:CONV_WIDTH]
        zero = send_grads(l, [g_wpw, g_wout, g_dwk], [0, 0, 1], "mixer")
        dproj, small["att_out_g"][l], *datts, delta = gate_bwd(dcat, 0, sv["att"], proj, 3, vec(att_out_g, l, zero), True,
                                                                dproj, f"att_gate_bwd_{l}")
        dqs, dks, dvs = [], [], []
        for i, d in enumerate(DILATIONS):
            dq, dk, dv = attn_bwd(sv["qs"][i], sv["ks"][i], sv["vs"][i], datts[i], _to_branch(sv["lse"], d),
                                  _to_branch(delta, d), d, f"attn_bwd_{l}_d{d}")
            dqs.append(dq)
            dks.append(dk)
            dvs.append(dv)
        dproj, small["q"][l], small["k"][l] = qk_prep_bwd(dqs, dks, dvs, proj, vec(q_norm_g, l), vec(k_norm_g, l),
                                                          cos2, sin2, dproj, f"qk_prep_bwd_{l}")
        zero = win_grads(l, sv["h"].T, dproj)
        dx, dxb, small["norm_g"][l] = in_proj_bwd(dproj, sv["wint"], sv["x"], vec(norm_g, l, zero), dx, f"in_proj_bwd_{l}")

    return dx, loss_cols, small
```

```python
import jax
import jax.numpy as jnp
from jax import lax
from jax.experimental import pallas as pl
from jax.experimental.pallas import tpu as pltpu

F32 = jnp.float32
BF16 = jnp.bfloat16
SDS = jax.ShapeDtypeStruct
MESH = pl.DeviceIdType.MESH

N_DEV = 8
DEPTH = 2
HEAD_DIM = 128
CONV_WIDTH = 31
CONV_PAD = 32
DILATIONS = (1, 4, 16)
Q_BLOCK = 128
ROPE_THETA = 10000.0
EPS = 1e-6
NEG = -1e30
ADAM_LR, ADAM_B1, ADAM_B2, ADAM_EPS, ADAM_WD, ADAM_STEP = 0.001, 0.9, 0.999, 1e-08, 0.01, 10
LANE = 128
ROW_CHUNK = 64
MIB = 1 << 20
NT = (((1,), (1,)), ((), ()))
TN = (((0,), (0,)), ((), ()))


def _call(body, **kw):
    return pl.pallas_call(body, **kw)


def _params(vmem_mib):
    return pltpu.CompilerParams(vmem_limit_bytes=vmem_mib * MIB)


def _tile(dim, pref, mult):
    t = min(pref, dim)
    while dim % t or t % mult:
        t -= mult
    return t


def _sig(v):
    return jax.nn.sigmoid(v)


def _rstd(v):
    return lax.rsqrt(jnp.mean(v * v, axis=-1, keepdims=True) + EPS)


def _row(tm, cb, c=0):
    return pl.BlockSpec((tm, cb), lambda i: (i, c))


def _full(shape):
    return pl.BlockSpec(shape, lambda i: (0,) * len(shape))


def _halo_prev(tm, cb, c=0):
    k = tm // CONV_PAD
    return pl.BlockSpec((CONV_PAD, cb), lambda i: (jnp.maximum(i * k - 1, 0), c))


def _halo_next(tm, cb, nblk, c=0):
    k = tm // CONV_PAD
    return pl.BlockSpec((CONV_PAD, cb), lambda i: (jnp.minimum((i + 1) * k, nblk - 1), c))


def in_proj(x, g, wt, name):
    S, D = x.shape
    N = wt.shape[0]
    tm, tn = _tile(S, 512, 8), _tile(N, 1024, LANE)

    def body(x_ref, g_ref, w_ref, o_ref, h_ref, ht_ref):
        @pl.when(pl.program_id(1) == 0)
        def _():
            xf = x_ref[...]
            hf = xf * _rstd(xf) * g_ref[...]
            h_ref[...] = hf.astype(BF16)
            ht_ref[...] = hf.T.astype(BF16)

        o_ref[...] = lax.dot_general(h_ref[...], w_ref[...], NT, preferred_element_type=F32)

    return _call(
        body, name=name, grid=(S // tm, N // tn),
        in_specs=[pl.BlockSpec((tm, D), lambda i, j: (i, 0)), pl.BlockSpec((1, D), lambda i, j: (0, 0)),
                  pl.BlockSpec((tn, D), lambda i, j: (j, 0))],
        out_specs=[pl.BlockSpec((tm, tn), lambda i, j: (i, j)), pl.BlockSpec((tm, D), lambda i, j: (i, 0)),
                   pl.BlockSpec((D, tm), lambda i, j: (0, i))],
        out_shape=[SDS((S, N), F32), SDS((S, D), BF16), SDS((D, S), BF16)],
        compiler_params=_params(48),
    )(x, g, wt)


def mm_nn(a, b, out_dtype, name, add=None, owners=1):
    M, K = a.shape
    N = b.shape[1]
    tm, tn = _tile(M, 1024 if owners > 1 else 512, 8), _tile(N // owners, 1024, LANE)
    per = N // owners // tn

    def body(*refs):
        a_ref, b_ref = refs[0], refs[1]
        o_ref = refs[-1]
        acc = jnp.dot(a_ref[...], b_ref[...], preferred_element_type=F32)
        if add is not None:
            acc = acc + refs[2][...]
        o_ref[...] = acc.astype(out_dtype)

    in_specs = [pl.BlockSpec((tm, K), lambda i, j: (i, 0)), pl.BlockSpec((K, tn), lambda i, j: (0, j))]
    args = [a, b]
    if add is not None:
        in_specs.append(pl.BlockSpec((tm, tn), lambda i, j: (i, j)))
        args.append(add)
    if owners > 1:
        out_spec = pl.BlockSpec((None, tm, tn), lambda i, j: (j // per, i, j % per))
        out_shape = SDS((owners, M, N // owners), out_dtype)
    else:
        out_spec, out_shape = pl.BlockSpec((tm, tn), lambda i, j: (i, j)), SDS((M, N), out_dtype)
    return _call(
        body, name=name, grid=(M // tm, N // tn), in_specs=in_specs, out_specs=out_spec, out_shape=out_shape,
        compiler_params=_params(48),
    )(*args)


def mm_core_blocks(a, b, core, name):
    M, K = a.shape
    blk = b.shape[1] // N_DEV
    tm, tn = _tile(M, 1024, 8), _tile(blk, 1024, LANE)
    per = blk // tn

    def body(core_ref, a_ref, b_ref, o_ref):
        o_ref[...] = jnp.dot(a_ref[...], b_ref[...], preferred_element_type=F32).astype(BF16)

    return _call(
        body, name=name,
        grid_spec=pltpu.PrefetchScalarGridSpec(
            num_scalar_prefetch=1, grid=(M // tm, 4 * per),
            in_specs=[pl.BlockSpec((tm, K), lambda i, j, core_ref: (i, 0)),
                      pl.BlockSpec((K, tn), lambda i, j, core_ref: (0, (2 * (j // per) + core_ref[0]) * per + j % per))],
            out_specs=pl.BlockSpec((None, tm, tn), lambda i, j, core_ref: (j // per, i, j % per))),
        out_shape=SDS((4, M, blk), BF16), compiler_params=_params(48),
    )(core, a, b)


def add_blocks(a, b, name):
    n, R, C = a.shape
    tr = _tile(R, 512, 16)
    spec = pl.BlockSpec((None, tr, C), lambda k, i: (k, i, 0))

    def body(a_ref, b_ref, o_ref):
        o_ref[...] = (a_ref[...].astype(F32) + b_ref[...].astype(F32)).astype(BF16)

    return _call(body, name=name, grid=(n, R // tr), in_specs=[spec, spec], out_specs=spec, out_shape=SDS(a.shape, BF16),
                 compiler_params=_params(32))(a, b)


def mm_nt(a, b, out_dtype, name):
    M, K = a.shape
    N = b.shape[0]
    tm, tn = _tile(M, 512, 8), _tile(N, 1024, LANE)

    def body(a_ref, b_ref, o_ref):
        o_ref[...] = lax.dot_general(a_ref[...], b_ref[...], NT, preferred_element_type=F32).astype(out_dtype)

    return _call(
        body, name=name, grid=(M // tm, N // tn),
        in_specs=[pl.BlockSpec((tm, K), lambda i, j: (i, 0)), pl.BlockSpec((tn, K), lambda i, j: (j, 0))],
        out_specs=pl.BlockSpec((tm, tn), lambda i, j: (i, j)), out_shape=SDS((M, N), out_dtype),
        compiler_params=_params(40),
    )(a, b)


def in_proj_bwd(dproj, wt, x, g, dx_out, name):
    S, K = dproj.shape
    D = wt.shape[1]
    tm, tk = _tile(S, 512, 8), _tile(K, 1024, LANE)
    nk = K // tk

    def body(dp_ref, w_ref, x_ref, g_ref, dxo_ref, dx_ref, dxb_ref, gacc_ref):
        i, k = pl.program_id(0), pl.program_id(1)
        part = jnp.dot(dp_ref[...], w_ref[...], preferred_element_type=F32)

        @pl.when(k == 0)
        def _():
            dx_ref[...] = part

        @pl.when(k > 0)
        def _():
            dx_ref[...] += part

        @pl.when((k == 0) & (i == 0))
        def _():
            gacc_ref[...] = jnp.zeros_like(gacc_ref)

        @pl.when(k == nk - 1)
        def _():
            dh = dx_ref[...]
            xf = x_ref[...]
            r = _rstd(xf)
            n = xf * r
            gacc_ref[...] += jnp.sum(dh * n, axis=0, keepdims=True)
            dn = dh * g_ref[...]
            dx = r * (dn - n * jnp.mean(dn * n, axis=-1, keepdims=True)) + dxo_ref[...]
            dx_ref[...] = dx
            dxb_ref[...] = dx.astype(BF16)

    return _call(
        body, name=name, grid=(S // tm, nk),
        in_specs=[pl.BlockSpec((tm, tk), lambda i, k: (i, k)), pl.BlockSpec((tk, D), lambda i, k: (k, 0)),
                  pl.BlockSpec((tm, D), lambda i, k: (i, 0)), pl.BlockSpec((1, D), lambda i, k: (0, 0)),
                  pl.BlockSpec((tm, D), lambda i, k: (i, 0))],
        out_specs=[pl.BlockSpec((tm, D), lambda i, k: (i, 0)), pl.BlockSpec((tm, D), lambda i, k: (i, 0)),
                   pl.BlockSpec((1, D), lambda i, k: (0, 0))],
        out_shape=[SDS((S, D), F32), SDS((S, D), BF16), SDS((1, D), F32)],
        compiler_params=_params(54),
    )(dproj, wt, x, g, dx_out)


def _dil_specs(S, DA, tm, dtype):
    specs = [pl.BlockSpec((tm // d, d * DA), lambda i: (i, 0)) for d in DILATIONS]
    shapes = [SDS((S // d, d * DA), dtype) for d in DILATIONS]
    return specs, shapes


def _head_buf(tm, DA):
    return pltpu.VMEM((DA // HEAD_DIM, tm, HEAD_DIM), F32)


def _emit_dilated(buf_ref, dsts, tm, DA):
    for d, dst in zip(DILATIONS, dsts):
        for h in range(DA // HEAD_DIM):
            for r in range(d):
                rows = slice(None) if d == 1 else pl.ds(r, tm // d, stride=d)
                dst[:, r * DA + h * HEAD_DIM:r * DA + (h + 1) * HEAD_DIM] = buf_ref.at[h][rows, :].astype(BF16)


def _collect_dilated(acc_ref, parts, tm, DA):
    for d, p in zip(DILATIONS, parts):
        for h in range(DA // HEAD_DIM):
            for r in range(d):
                part = p[:, r * DA + h * HEAD_DIM:r * DA + (h + 1) * HEAD_DIM].astype(F32)
                if d == 1:
                    acc_ref[h] = part
                else:
                    rows = pl.ds(r, tm // d, stride=d)
                    acc_ref.at[h][rows, :] = acc_ref.at[h][rows, :] + part


def qk_prep(proj, gq, gk, cos2, sin2, name):
    S = proj.shape[0]
    DA = proj.shape[1] // 7
    H = DA // HEAD_DIM
    tm = _tile(S, 256, 16 * DILATIONS[-1])
    nd = len(DILATIONS)

    def body(q_ref, k_ref, v_ref, gq_ref, gk_ref, c_ref, s_ref, *rest):
        outs, buf_ref = rest[:3 * nd], rest[3 * nd]
        ct, st = c_ref[...], s_ref[...]
        for t, (src, g_ref) in enumerate(((q_ref, gq_ref), (k_ref, gk_ref))):
            gain = g_ref[...]
            for h in range(H):
                sl = slice(h * HEAD_DIM, (h + 1) * HEAD_DIM)
                xh = src[:, sl]
                n = xh * _rstd(xh) * gain
                buf_ref[h] = n * ct + pltpu.roll(n, HEAD_DIM // 2, 1) * st
            _emit_dilated(buf_ref, outs[t * nd:(t + 1) * nd], tm, DA)
        for h in range(H):
            buf_ref[h] = v_ref[:, h * HEAD_DIM:(h + 1) * HEAD_DIM]
        _emit_dilated(buf_ref, outs[2 * nd:], tm, DA)

    specs, shapes = _dil_specs(S, DA, tm, BF16)
    outs = _call(
        body, name=name, grid=(S // tm,),
        in_specs=[_row(tm, DA, 0), _row(tm, DA, 1), _row(tm, DA, 2), _full((1, HEAD_DIM)), _full((1, HEAD_DIM)),
                  _row(tm, HEAD_DIM), _row(tm, HEAD_DIM)],
        out_specs=specs * 3, out_shape=shapes * 3, scratch_shapes=[_head_buf(tm, DA)],
        compiler_params=_params(48),
    )(proj, proj, proj, gq, gk, cos2, sin2)
    return outs[:nd], outs[nd:2 * nd], outs[2 * nd:]


def qk_prep_bwd(dqs, dks, dvs, proj, gq, gk, cos2, sin2, dproj, name):
    S = proj.shape[0]
    DA = proj.shape[1] // 7
    H = DA // HEAD_DIM
    tm = _tile(S, 256, 16 * DILATIONS[-1])
    nb = len(dqs)

    def body(*refs):
        dq_refs, dk_refs, dv_refs = refs[:nb], refs[nb:2 * nb], refs[2 * nb:3 * nb]
        q_ref, k_ref, gq_ref, gk_ref, c_ref, s_ref = refs[3 * nb:3 * nb + 6]
        out_ref, gqa_ref, gka_ref, acc_ref = refs[3 * nb + 7:]
        ct, st = c_ref[...], s_ref[...]

        @pl.when(pl.program_id(0) == 0)
        def _():
            gqa_ref[...] = jnp.zeros_like(gqa_ref)
            gka_ref[...] = jnp.zeros_like(gka_ref)

        for parts, x_ref, g_ref, col, gacc in ((dq_refs, q_ref, gq_ref, 0, gqa_ref),
                                               (dk_refs, k_ref, gk_ref, DA, gka_ref)):
            gain = g_ref[...]
            gsum = jnp.zeros((1, HEAD_DIM), F32)
            _collect_dilated(acc_ref, parts, tm, DA)
            for h in range(H):
                sl = slice(h * HEAD_DIM, (h + 1) * HEAD_DIM)
                dout = acc_ref[h]
                dn = dout * ct + pltpu.roll(dout * st, HEAD_DIM // 2, 1)
                xh = x_ref[:, sl]
                r = _rstd(xh)
                xn = xh * r
                gsum = gsum + jnp.sum(dn * xn, axis=0, keepdims=True)
                dnn = dn * gain
                dx = r * (dnn - xn * jnp.mean(dnn * xn, axis=-1, keepdims=True))
                out_ref[:, col + h * HEAD_DIM:col + (h + 1) * HEAD_DIM] = dx.astype(BF16)
            gacc[...] += gsum
        _collect_dilated(acc_ref, dv_refs, tm, DA)
        for h in range(H):
            out_ref[:, 2 * DA + h * HEAD_DIM:2 * DA + (h + 1) * HEAD_DIM] = acc_ref[h].astype(BF16)

    specs, _ = _dil_specs(S, DA, tm, BF16)
    return _call(
        body, name=name, grid=(S // tm,),
        in_specs=specs * 3 + [_row(tm, DA, 0), _row(tm, DA, 1), _full((1, HEAD_DIM)),
                              _full((1, HEAD_DIM)), _row(tm, HEAD_DIM), _row(tm, HEAD_DIM), pl.BlockSpec(memory_space=pl.ANY)],
        out_specs=[_row(tm, 3 * DA)] + [_full((1, HEAD_DIM))] * 2,
        out_shape=[SDS((S, 7 * DA), BF16)] + [SDS((1, HEAD_DIM), F32)] * 2,
        input_output_aliases={3 * nb + 6: 0},
        scratch_shapes=[_head_buf(tm, DA)],
        compiler_params=_params(48),
    )(*dqs, *dks, *dvs, proj, proj, gq, gk, cos2, sin2, dproj)


def _band_mask(n):
    row = lax.broadcasted_iota(jnp.int32, (Q_BLOCK, 2 * Q_BLOCK), 0)
    col = lax.broadcasted_iota(jnp.int32, (Q_BLOCK, 2 * Q_BLOCK), 1)
    first = jnp.where(n > 0, Q_BLOCK, 2 * Q_BLOCK + 1)
    return (col <= row) | ((col - row) >= first)


def attn_fwd(qh, kh, vb, d, name):
    L = qh.shape[0]
    DA = qh.shape[1] // d
    H = DA // HEAD_DIM
    nb = L // Q_BLOCK
    scale = HEAD_DIM ** -0.5
    view = (L, d * DA)

    def body(q_ref, kc_ref, kp_ref, vc_ref, vp_ref, o_ref, lse_ref):
        mask = _band_mask(pl.program_id(1))
        ones = jnp.ones((2 * Q_BLOCK, HEAD_DIM), BF16)
        for h in range(H):
            sl = slice(h * HEAD_DIM, (h + 1) * HEAD_DIM)
            keys = jnp.concatenate([kc_ref[:, sl], kp_ref[:, sl]], axis=0)
            s = lax.dot_general(q_ref[:, sl], keys, NT, preferred_element_type=F32) * scale
            s = jnp.where(mask, s, NEG)
            m = jnp.max(s, axis=-1, keepdims=True)
            p = jnp.exp(s - m).astype(BF16)
            vals = jnp.concatenate([jnp.concatenate([vc_ref[:, sl], vp_ref[:, sl]], axis=0), ones], axis=1)
            ol = jnp.dot(p, vals, preferred_element_type=F32)
            l = ol[:, HEAD_DIM:]
            o_ref[:, sl] = ol[:, :HEAD_DIM] / l
            lse_ref[0, :, h:h + 1] = m + jnp.log(l[:, 0:1])

    cur = pl.BlockSpec((Q_BLOCK, DA), lambda r, n: (n, r))
    prev = pl.BlockSpec((Q_BLOCK, DA), lambda r, n: (jnp.maximum(n - 1, 0), r))
    o, lse = _call(
        body, name=name, grid=(d, nb), in_specs=[cur, cur, prev, cur, prev],
        out_specs=[cur, pl.BlockSpec((1, Q_BLOCK, H), lambda r, n: (r, n, 0))],
        out_shape=[SDS(view, F32), SDS((d, L, H), F32)],
        compiler_params=_params(32),
    )(qh, kh, kh, vb, vb)
    return o, lse


def attn_bwd(qh, kh, vb, da, lse_d, delta_d, d, name):
    L = qh.shape[0]
    DA = qh.shape[1] // d
    H = DA // HEAD_DIM
    nb = L // Q_BLOCK
    scale = HEAD_DIM ** -0.5
    view = (L, d * DA)
    resident = d > 1

    def body(q_ref, kc_ref, kp_ref, vc_ref, vp_ref, do_ref, lse_ref, dl_ref, dq_ref, dk_ref, dv_ref, dkc_ref, dvc_ref):
        n = pl.program_id(1)
        mask = _band_mask(n)
        if resident:
            done = pl.ds(pl.multiple_of(jnp.maximum(n - 1, 0) * Q_BLOCK, Q_BLOCK), Q_BLOCK)
            fresh = pl.ds(pl.multiple_of(n * Q_BLOCK, Q_BLOCK), Q_BLOCK)
        else:
            done = slice(None)

        @pl.when(n == 0)
        def _():
            dkc_ref[...] = jnp.zeros_like(dkc_ref)
            dvc_ref[...] = jnp.zeros_like(dvc_ref)

        @pl.when(n < nb)
        def _():
            for h in range(H):
                sl = slice(h * HEAD_DIM, (h + 1) * HEAD_DIM)
                q, do = q_ref[:, sl], do_ref[:, sl]
                keys = jnp.concatenate([kc_ref[:, sl], kp_ref[:, sl]], axis=0)
                vals = jnp.concatenate([vc_ref[:, sl], vp_ref[:, sl]], axis=0)
                lse = jnp.broadcast_to(lse_ref[0, :, h:h + 1], (Q_BLOCK, 2 * Q_BLOCK))
                dl = jnp.broadcast_to(dl_ref[0, :, h:h + 1], (Q_BLOCK, 2 * Q_BLOCK))
                s = lax.dot_general(q, keys, NT, preferred_element_type=F32) * scale
                p = jnp.exp(jnp.where(mask, s, NEG) - lse)
                dp = lax.dot_general(do, vals, NT, preferred_element_type=F32)
                ds = (p * (dp - dl) * scale).astype(BF16)
                dq_ref[:, sl] = jnp.dot(ds, keys, preferred_element_type=F32).astype(BF16)
                dk = lax.dot_general(ds, q, TN, preferred_element_type=F32)
                dv = lax.dot_general(p.astype(BF16), do, TN, preferred_element_type=F32)
                dk_ref[done, sl] = (dkc_ref[:, sl] + dk[Q_BLOCK:]).astype(BF16)
                dv_ref[done, sl] = (dvc_ref[:, sl] + dv[Q_BLOCK:]).astype(BF16)
                dkc_ref[:, sl] = dk[:Q_BLOCK]
                dvc_ref[:, sl] = dv[:Q_BLOCK]
                if resident:
                    dk_ref[fresh, sl] = dk[:Q_BLOCK].astype(BF16)
                    dv_ref[fresh, sl] = dv[:Q_BLOCK].astype(BF16)

        if not resident:
            @pl.when(n == nb)
            def _():
                dk_ref[...] = dkc_ref[...].astype(BF16)
                dv_ref[...] = dvc_ref[...].astype(BF16)

    steps = nb if resident else nb + 1
    cur = pl.BlockSpec((Q_BLOCK, DA), lambda r, n: (jnp.minimum(n, nb - 1), r))
    prev = pl.BlockSpec((Q_BLOCK, DA), lambda r, n: (jnp.clip(n - 1, 0, nb - 1), r))
    if resident:
        keyside = pl.BlockSpec((L, DA), lambda r, n: (0, r))
    else:
        keyside = pl.BlockSpec((Q_BLOCK, DA), lambda r, n: (jnp.maximum(n - 1, 0), r))
    stat = pl.BlockSpec((1, Q_BLOCK, H), lambda r, n: (r, jnp.minimum(n, nb - 1), 0))
    dq, dk, dv = _call(
        body, name=name, grid=(d, steps), in_specs=[cur, cur, prev, cur, prev, cur, stat, stat],
        out_specs=[cur, keyside, keyside], out_shape=[SDS(view, BF16)] * 3,
        scratch_shapes=[pltpu.VMEM((Q_BLOCK, DA), F32), pltpu.VMEM((Q_BLOCK, DA), F32)],
        compiler_params=_params(40),
    )(qh, kh, kh, vb, vb, da, lse_d, delta_d)
    return dq, dk, dv


def _to_branch(stat, d):
    S, H = stat.shape
    return stat.reshape(S // d, d, H).transpose(1, 0, 2)


def _from_branch(stat):
    d, L, H = stat.shape
    return stat.transpose(1, 0, 2).reshape(L * d, H)


def att_combine(os_, lses, proj, gain, name):
    S, DA = os_[0].shape
    H = DA // HEAD_DIM
    tm = _tile(S, 256, 16 * DILATIONS[-1])
    nb = len(os_)

    def body(*refs):
        o_views, l_refs = refs[:nb], refs[nb:2 * nb]
        gate_ref, gain_ref, att_ref, y_ref, lse_ref, yt_ref = refs[2 * nb:2 * nb + 6]
        bufs = refs[2 * nb + 6:]
        for d, view, buf in zip(DILATIONS[1:], o_views[1:], bufs):
            for h in range(H):
                for r in range(d):
                    buf.at[h][pl.ds(r, tm // d, stride=d), :] = view[:, r * DA + h * HEAD_DIM:r * DA + (h + 1) * HEAD_DIM]
        ls = [r[...] for r in l_refs]
        top = ls[0]
        for l in ls[1:]:
            top = jnp.maximum(top, l)
        den = jnp.exp(ls[0] - top)
        for l in ls[1:]:
            den = den + jnp.exp(l - top)
        lse = top + jnp.log(den)
        lse_ref[...] = lse
        ws = [jnp.exp(l - lse) for l in ls]
        for h in range(H):
            sl = slice(h * HEAD_DIM, (h + 1) * HEAD_DIM)
            acc = ws[0][:, h:h + 1] * o_views[0][:, sl]
            for w, buf in zip(ws[1:], bufs):
                acc = acc + w[:, h:h + 1] * buf[h]
            att_ref[:, sl] = acc
        a = att_ref[...]
        g = gate_ref[...]
        y = a * _rstd(a) * gain_ref[...] * (g * _sig(g))
        y_ref[...] = y.astype(BF16)
        yt_ref[...] = y.T.astype(BF16)

    specs, _ = _dil_specs(S, DA, tm, F32)
    return _call(
        body, name=name, grid=(S // tm,),
        in_specs=specs + [_row(tm, H)] * nb + [_row(tm, DA, 3), _full((1, DA))],
        out_specs=[_row(tm, DA), _row(tm, DA), _row(tm, H), pl.BlockSpec((DA, tm), lambda i: (0, i))],
        out_shape=[SDS((S, DA), F32), SDS((S, 2 * DA), BF16), SDS((S, H), F32), SDS((2 * DA, S), BF16)],
        scratch_shapes=[_head_buf(tm, DA)] * (nb - 1),
        compiler_params=_params(48),
    )(*os_, *lses, proj, gain)


def gate_bwd(dcat, cblk, a, proj, gate_blk, gain, dilated, dproj, name):
    S, DA = a.shape
    H = DA // HEAD_DIM
    tm = _tile(S, 256, 16 * DILATIONS[-1])
    nd = len(DILATIONS) if dilated else 1

    def body(dy_ref, a_ref, gate_ref, gain_ref, *rest):
        dg_ref, gacc_ref, *rest = rest[0 if dproj is None else 1:]

        @pl.when(pl.program_id(0) == 0)
        def _():
            gacc_ref[...] = jnp.zeros_like(gacc_ref)

        dy, av, g, gain_v = dy_ref[...], a_ref[...], gate_ref[...], gain_ref[...]
        r = _rstd(av)
        n = av * r
        sg = _sig(g)
        dg_ref[...] = (dy * (n * gain_v) * (sg * (1.0 + g * (1.0 - sg)))).astype(BF16)
        drn = dy * (g * sg)
        gacc_ref[...] += jnp.sum(drn * n, axis=0, keepdims=True)
        dn = drn * gain_v
        da = r * (dn - n * jnp.mean(dn * n, axis=-1, keepdims=True))
        if dilated:
            da_refs, delta_ref, buf_ref = rest[:nd], rest[nd], rest[nd + 1]
            for h in range(H):
                buf_ref[h] = da[:, h * HEAD_DIM:(h + 1) * HEAD_DIM]
            _emit_dilated(buf_ref, da_refs, tm, DA)
            prod = da * av
            for h in range(H):
                delta_ref[:, h:h + 1] = jnp.sum(prod[:, h * HEAD_DIM:(h + 1) * HEAD_DIM], axis=-1, keepdims=True)
        else:
            rest[0][...] = da.astype(BF16)

    out_specs = [_row(tm, DA, gate_blk), _full((1, DA))]
    out_shape = [SDS((S, 7 * DA), BF16), SDS((1, DA), F32)]
    scratch = []
    carried = [] if dproj is None else [dproj]
    if dilated:
        specs, shapes = _dil_specs(S, DA, tm, BF16)
        out_specs += specs + [_row(tm, H)]
        out_shape += shapes + [SDS((S, H), F32)]
        scratch = [_head_buf(tm, DA)]
    else:
        out_specs.append(_row(tm, DA))
        out_shape.append(SDS((S, DA), BF16))
    return _call(
        body, name=name, grid=(S // tm,),
        in_specs=[_row(tm, DA, cblk), _row(tm, DA), _row(tm, DA, gate_blk), _full((1, DA))] + [pl.BlockSpec(memory_space=pl.ANY)] * len(carried),
        out_specs=out_specs, out_shape=out_shape, scratch_shapes=scratch, compiler_params=_params(48),
        input_output_aliases={4: 0} if carried else {},
    )(dcat, a, proj, gain, *carried)


def _by_sublane_phase(offsets):
    groups = [(p, [o for o in offsets if o % 8 == p]) for p in range(8)]
    return [(p, sorted(os_)) for p, os_ in groups if os_]


def _shifted_rows(src_ref, tmp_ref, base, phase, offsets, lanes):
    if phase == 0:
        return lambda o: src_ref[pl.ds(base + o, ROW_CHUNK), lanes]
    span = offsets[-1] - phase + ROW_CHUNK
    tmp_ref[phase, pl.ds(0, span), :] = src_ref[pl.ds(base + phase, span), lanes]
    return lambda o: tmp_ref[phase, pl.ds(o - phase, ROW_CHUNK), :]


def _shift_scratch():
    return pltpu.VMEM((8, CONV_PAD + ROW_CHUNK, LANE), F32)


def _fill_u(i, a_ref, b_ref, ah_ref, bh_ref, uext_ref, tm):
    uext_ref[pl.ds(CONV_PAD, tm), :] = a_ref[...] * _sig(b_ref[...])
    uh = ah_ref[...] * _sig(bh_ref[...])
    uext_ref[pl.ds(0, CONV_PAD), :] = jnp.where(i > 0, uh, 0.0)


def conv_fwd(proj, wk, bias, ln_g, ln_b, out_g, wpw, cat, cat_t, name):
    S = proj.shape[0]
    DC = proj.shape[1] // 7
    tm = _tile(S, 128, ROW_CHUNK)
    lead = CONV_PAD - (CONV_WIDTH - 1)

    def body(a_ref, b_ref, ah_ref, bh_ref, gate_ref, wk_ref, bias_ref, lg_ref, lb_ref, og_ref, wpw_ref, cat_ref, catt_ref,
             cy_ref, cyt_ref, zt_ref, conv_ref, y_ref, uext_ref, tmp_ref):
        _fill_u(pl.program_id(0), a_ref, b_ref, ah_ref, bh_ref, uext_ref, tm)

        def cols(cc, carry):
            c0 = pl.multiple_of(cc * LANE, LANE)
            lanes = pl.ds(c0, LANE)
            for rr in range(tm // ROW_CHUNK):
                acc = jnp.broadcast_to(bias_ref[:, lanes], (ROW_CHUNK, LANE))
                for phase, offsets in _by_sublane_phase(range(lead, lead + CONV_WIDTH)):
                    rows = _shifted_rows(uext_ref, tmp_ref, rr * ROW_CHUNK, phase, offsets, lanes)
                    for o in offsets:
                        acc = acc + wk_ref[o - lead:o - lead + 1, lanes] * rows(o)
                y_ref[pl.ds(rr * ROW_CHUNK, ROW_CHUNK), lanes] = acc
            return carry

        lax.fori_loop(0, DC // LANE, cols, 0)
        y = y_ref[...]
        yc = y - jnp.mean(y, axis=-1, keepdims=True)
        ln = yc * _rstd(yc) * lg_ref[...] + lb_ref[...]
        z = ln * _sig(ln)
        zt_ref[...] = z.T.astype(BF16)
        conv = jnp.dot(z.astype(BF16), wpw_ref[...], preferred_element_type=F32)
        conv_ref[...] = conv
        g = gate_ref[...]
        cy = conv * _rstd(conv) * og_ref[...] * (g * _sig(g))
        cy_ref[...] = cy.astype(BF16)
        cyt_ref[...] = cy.T.astype(BF16)

    vec = _full((1, DC))
    anywhere = pl.BlockSpec(memory_space=pl.ANY)
    return _call(
        body, name=name, grid=(S // tm,),
        in_specs=[_row(tm, DC, 4), _row(tm, DC, 5), _halo_prev(tm, DC, 4), _halo_prev(tm, DC, 5), _row(tm, DC, 6),
                  _full((CONV_PAD, DC)), vec, vec, vec, vec, _full((DC, DC)), anywhere, anywhere],
        out_specs=[_row(tm, DC, 1), pl.BlockSpec((DC, tm), lambda i: (1, i)), pl.BlockSpec((DC, tm), lambda i: (0, i)),
                   _row(tm, DC), _row(tm, DC)],
        out_shape=[SDS((S, 2 * DC), BF16), SDS((2 * DC, S), BF16), SDS((DC, S), BF16), SDS((S, DC), F32), SDS((S, DC), F32)],
        input_output_aliases={11: 0, 12: 1},
        scratch_shapes=[pltpu.VMEM((CONV_PAD + tm, DC), F32), _shift_scratch()],
        compiler_params=_params(48),
    )(proj, proj, proj, proj, proj, wk, bias, ln_g, ln_b, out_g, wpw, cat, cat_t)


def conv_bwd_ln(dconv, wpw, y, ln_g, ln_b, name):
    S, DC = y.shape
    tm = _tile(S, 256, 8)

    def body(dc_ref, wpw_ref, y_ref, lg_ref, lb_ref, dy_ref, glg_ref, glb_ref, gb_ref):
        @pl.when(pl.program_id(0) == 0)
        def _():
            glg_ref[...] = jnp.zeros_like(glg_ref)
            glb_ref[...] = jnp.zeros_like(glb_ref)
            gb_ref[...] = jnp.zeros_like(gb_ref)

        dz = lax.dot_general(dc_ref[...], wpw_ref[...], NT, preferred_element_type=F32)
        yv = y_ref[...]
        yc = yv - jnp.mean(yv, axis=-1, keepdims=True)
        rstd = _rstd(yc)
        yhat = yc * rstd
        ln = yhat * lg_ref[...] + lb_ref[...]
        sg = _sig(ln)
        dln = dz * (sg * (1.0 + ln * (1.0 - sg)))
        glb_ref[...] += jnp.sum(dln, axis=0, keepdims=True)
        glg_ref[...] += jnp.sum(dln * yhat, axis=0, keepdims=True)
        dyh = dln * lg_ref[...]
        dy = rstd * (dyh - jnp.mean(dyh, axis=-1, keepdims=True) - yhat * jnp.mean(dyh * yhat, axis=-1, keepdims=True))
        dy_ref[...] = dy
        gb_ref[...] += jnp.sum(dy, axis=0, keepdims=True)

    vec = _full((1, DC))
    return _call(
        body, name=name, grid=(S // tm,),
        in_specs=[_row(tm, DC), _full((DC, DC)), _row(tm, DC), vec, vec],
        out_specs=[_row(tm, DC), vec, vec, vec],
        out_shape=[SDS((S, DC), F32)] + [SDS((1, DC), F32)] * 3,
        compiler_params=_params(48),
    )(dconv, wpw, y, ln_g, ln_b)


def conv_bwd_dw(dy, proj, wk, dproj, name):
    S, DC = dy.shape
    tm = _tile(S, 128, ROW_CHUNK)
    nsteps = S // tm
    lead = CONV_PAD - (CONV_WIDTH - 1)
    groups = ROW_CHUNK // 8

    def body(dy_ref, dyn_ref, a_ref, b_ref, ah_ref, bh_ref, wk_ref, dproj_ref, dab_ref, gw_ref, uext_ref, dyext_ref, du_ref,
             tmp_dy_ref, tmp_u_ref):
        i = pl.program_id(0)

        @pl.when(i == 0)
        def _():
            gw_ref[...] = jnp.zeros_like(gw_ref)

        _fill_u(i, a_ref, b_ref, ah_ref, bh_ref, uext_ref, tm)
        dyext_ref[pl.ds(0, tm), :] = dy_ref[...]
        dyext_ref[pl.ds(tm, CONV_PAD), :] = jnp.where(i < nsteps - 1, dyn_ref[...], 0.0)

        def cols(cc, carry):
            c0 = pl.multiple_of(cc * LANE, LANE)
            lanes = pl.ds(c0, LANE)
            for rr in range(tm // ROW_CHUNK):
                base = rr * ROW_CHUNK
                acc = jnp.zeros((ROW_CHUNK, LANE), F32)
                for phase, offsets in _by_sublane_phase(range(CONV_WIDTH)):
                    rows = _shifted_rows(dyext_ref, tmp_dy_ref, base, phase, offsets, lanes)
                    for o in offsets:
                        j = CONV_WIDTH - 1 - o
                        acc = acc + wk_ref[j:j + 1, lanes] * rows(o)
                du_ref[pl.ds(base, ROW_CHUNK), lanes] = acc
                dyc = dyext_ref[pl.ds(base, ROW_CHUNK), lanes]
                for phase, offsets in _by_sublane_phase(range(lead, lead + CONV_WIDTH)):
                    rows = _shifted_rows(uext_ref, tmp_u_ref, base, phase, offsets, lanes)
                    for o in offsets:
                        prod = dyc * rows(o)
                        part = prod[0:8]
                        for k in range(1, groups):
                            part = part + prod[8 * k:8 * k + 8]
                        gw_ref[o - lead, :, lanes] += part
            return carry

        lax.fori_loop(0, DC // LANE, cols, 0)
        du = du_ref[...]
        sb = _sig(b_ref[...])
        dab_ref[:, :DC] = (du * sb).astype(BF16)
        dab_ref[:, DC:] = (du * a_ref[...] * sb * (1.0 - sb)).astype(BF16)

    return _call(
        body, name=name, grid=(nsteps,),
        in_specs=[_row(tm, DC), _halo_next(tm, DC, S // CONV_PAD), _row(tm, DC, 4), _row(tm, DC, 5),
                  _halo_prev(tm, DC, 4), _halo_prev(tm, DC, 5), _full((CONV_PAD, DC)), pl.BlockSpec(memory_space=pl.ANY)],
        out_specs=[_row(tm, 2 * DC, 2), _full((CONV_PAD, 8, DC))],
        out_shape=[SDS((S, 7 * DC), BF16), SDS((CONV_PAD, 8, DC), F32)],
        input_output_aliases={7: 0},
        scratch_shapes=[pltpu.VMEM((CONV_PAD + tm, DC), F32), pltpu.VMEM((tm + CONV_PAD, DC), F32), pltpu.VMEM((tm, DC), F32),
                        _shift_scratch(), _shift_scratch()],
        compiler_params=_params(40),
    )(dy, dy, proj, proj, proj, proj, wk, dproj)


def loss_head(xo, target, name):
    S, D = xo.shape
    tm = _tile(S, 256, 8)

    def body(x_ref, t_ref, dy_ref, dyb_ref, acc_ref):
        @pl.when(pl.program_id(0) == 0)
        def _():
            acc_ref[...] = jnp.zeros_like(acc_ref)

        err = x_ref[...] - t_ref[...]
        dy = err * (1.0 / D)
        dy_ref[...] = dy
        dyb_ref[...] = dy.astype(BF16)
        acc_ref[...] += jnp.sum(err * dy, axis=0, keepdims=True) * 0.5

    return _call(
        body, name=name, grid=(S // tm,), in_specs=[_row(tm, D), _row(tm, D)],
        out_specs=[_row(tm, D), _row(tm, D), _full((1, D))],
        out_shape=[SDS((S, D), F32), SDS((S, D), BF16), SDS((1, D), F32)],
        compiler_params=_params(32),
    )(xo, target)


def _coords():
    x, y, c = lax.axis_index("x"), lax.axis_index("y"), lax.axis_index("c")
    return x, y, c


def _lin(p):
    return 4 * p[0] + 2 * p[1] + p[2]


def _chip(p):
    return 2 * p[0] + p[1]


def _slot(ref, axis, idx, size):
    index = [slice(None)] * len(ref.shape)
    index[axis] = pl.ds(idx * size, size)
    return ref.at[tuple(index)]


def all_gather(blocks, axes, name):
    na = len(blocks)
    sizes = [b.shape[ax] for b, ax in zip(blocks, axes)]
    fulls = [SDS(b.shape[:ax] + (N_DEV * b.shape[ax],) + b.shape[ax + 1:], b.dtype) for b, ax in zip(blocks, axes)]

    def body(*refs):
        in_refs, out_refs = refs[:na], refs[na:2 * na]
        send_sems, recv_sems, local_sems = refs[2 * na:]
        x, y, c = _coords()
        me, sibling = (x, y, c), (x, y, 1 - c)
        chips = [(1 - x, y), (x, 1 - y), (1 - x, 1 - y)]
        south = c == 0
        relayed = (jnp.where(south, 1 - x, x), jnp.where(south, y, 1 - y), c)
        onward = (jnp.where(south, x, 1 - x), jnp.where(south, 1 - y, y), c)

        def place(a, p):
            return _slot(out_refs[a], axes[a], _lin(p), sizes[a])

        def copy(a, k, block, to, src=None):
            return pltpu.make_async_remote_copy(
                src_ref=place(a, block) if src is None else src, dst_ref=place(a, block),
                send_sem=send_sems.at[a, k], recv_sem=recv_sems.at[a, k], device_id=to, device_id_type=MESH)

        mine = [pltpu.make_async_copy(in_refs[a], place(a, me), local_sems.at[a]) for a in range(na)]
        for cp in mine:
            cp.start()
        first = []
        for a in range(na):
            first.append(copy(a, 0, me, sibling, src=in_refs[a]))
            first += [copy(a, 1 + j, me, (*chip, c), src=in_refs[a]) for j, chip in enumerate(chips[:2])]
        for cp in first:
            cp.start()
        later = []
        for a in range(na):
            for j, chip in enumerate(chips[:2]):
                copy(a, 1 + j, (*chip, c), me).wait_recv()
            later.append(copy(a, 3, relayed, onward))
            later += [copy(a, 4 + j, (*chip, c), sibling) for j, chip in enumerate(chips[:2])]
            for cp in later[-3:]:
                cp.start()
        for a in range(na):
            copy(a, 3, (*chips[2], c), me).wait_recv()
            later.append(copy(a, 6, (*chips[2], c), sibling))
            later[-1].start()
        for a in range(na):
            copy(a, 0, sibling, me).wait_recv()
            for j, chip in enumerate(chips):
                copy(a, 4 + j, (*chip, 1 - c), me).wait_recv()
        for cp in first + later:
            cp.wait_send()
        for cp in mine:
            cp.wait()

    hbm = pl.BlockSpec(memory_space=pltpu.HBM)
    return _call(
        body, name=name, in_specs=[hbm] * na, out_specs=[hbm] * na, out_shape=fulls,
        scratch_shapes=[pltpu.SemaphoreType.DMA((na, 7)), pltpu.SemaphoreType.DMA((na, 7)), pltpu.SemaphoreType.DMA((na,))],
    )(*blocks)


class _Exchange:
    def __init__(self, gather, srcs, axes, name, after, route="all", lands=None):
        self.gather, self.axes, self.name, self.route = gather, axes, name, route
        if route == "pass":
            self.na, self.ns = len(lands), 0
            self.sizes = [l.shape[ax] // N_DEV for l, ax in zip(lands, axes)]
            self.kinds = [pltpu.HBM(l.shape, l.dtype) for l in lands]
            self._start([pltpu.with_memory_space_constraint(t, pltpu.HBM) for t in lands], after)
            return
        self.na = self.ns = len(srcs)
        if gather:
            self.sizes = [s.shape[ax] for s, ax in zip(srcs, axes)]
            lands = [s.shape[:ax] + (N_DEV * s.shape[ax],) + s.shape[ax + 1:] for s, ax in zip(srcs, axes)]
        else:
            self.sizes = [None if ax is None else s.shape[ax] // N_DEV for s, ax in zip(srcs, axes)]
            lands = [s.shape if ax is None else (N_DEV,) + s.shape[:ax] + (sz,) + s.shape[ax + 1:]
                     for s, ax, sz in zip(srcs, axes, self.sizes)]
        self.kinds = [pltpu.HBM(s.shape, s.dtype) for s in srcs] + [pltpu.HBM(l, s.dtype) for l, s in zip(lands, srcs)]
        lands = [lax.empty(l, s.dtype) for l, s in zip(lands, srcs)]
        after = jnp.zeros((8, LANE), F32) if after is None else after
        self._start([pltpu.with_memory_space_constraint(t, pltpu.HBM) for t in list(srcs) + lands], after)

    def _src(self, a, ref, owner):
        if self.gather:
            return ref
        return ref.at[_lin(owner)] if self.axes[a] is None else _slot(ref, self.axes[a], _lin(owner), self.sizes[a])

    def _dst(self, a, land, sender):
        return _slot(land, self.axes[a], _lin(sender), self.sizes[a]) if self.gather else land.at[_lin(sender)]

    def _flips(self):
        if self.route == "all":
            return [(k >> 2 & 1, k >> 1 & 1, k & 1) for k in range(1, N_DEV)]
        if self.route == "chips":
            return [(0, 0, 1), (1, 0, 0), (0, 1, 0), (1, 1, 0)]
        if self.route == "sibling":
            return [(0, 0, 1)] * 4
        return [(1, 0, 0), (0, 1, 0), (1, 1, 0)]

    def _copies(self, refs, send_sems, recv_sems):
        na, ns = self.na, self.ns
        me = _coords()
        flips = self._flips()
        n = len(flips)
        others = [tuple(1 - v if f else v for v, f in zip(me, flip)) for flip in flips]
        sends, arrivals = [], []
        for a in range(na):
            land = refs[ns + a]
            for k, other in enumerate(others):
                if self.route == "pass":
                    peer = (me[0], me[1], 1 - me[2])
                    theirs = (other[0], other[1], 1 - me[2])
                    send = dict(src_ref=self._dst(a, land, other), dst_ref=self._dst(a, land, other))
                    arrive = dict(src_ref=self._dst(a, land, theirs), dst_ref=self._dst(a, land, theirs))
                elif self.route == "sibling":
                    peer = other
                    send = dict(src_ref=refs[a].at[k], dst_ref=land.at[k])
                    arrive = send
                elif self.route == "chip_parts":
                    peer = other
                    send = dict(src_ref=refs[a].at[_chip(peer)], dst_ref=land.at[_chip(me)])
                    arrive = dict(src_ref=refs[a].at[_chip(me)], dst_ref=land.at[_chip(peer)])
                else:
                    peer = other
                    send = dict(src_ref=self._src(a, refs[a], peer), dst_ref=self._dst(a, land, me))
                    arrive = dict(src_ref=self._src(a, refs[a], me), dst_ref=self._dst(a, land, peer))
                pair = dict(send_sem=send_sems.at[n * a + k], recv_sem=recv_sems.at[n * a + k], device_id=peer, device_id_type=MESH)
                sends.append(pltpu.make_async_remote_copy(**send, **pair))
                arrivals.append(pltpu.make_async_remote_copy(**arrive, **pair))
        return sends, arrivals

    def _place_own(self, operands):
        na = self.na
        mine = _chip(_coords()) if self.route == "chip_parts" else _lin(_coords())
        me = jnp.reshape(mine, (1,)).astype(jnp.int32)
        lands = []
        for a in range(na):
            src, land, ax = operands[a], operands[na + a], self.axes[a]
            if ax == 1:
                R, C = src.shape[0], self.sizes[a]
                steps, tile = 1, (R, C)
                in_map = lambda i, me_ref: (0, me_ref[0])
            elif ax is None:
                R, C = src.shape[1:]
                tr = _tile(R, 512, 16)
                steps, tile = R // tr, (None, tr, C)
                in_map = lambda i, me_ref: (me_ref[0], i, 0)
            else:
                R, C = (src.shape[0] if self.gather else self.sizes[a]), src.shape[1]
                tr = _tile(R, 512, 16)
                steps, tile = R // tr, (tr, C)
                in_map = (lambda i, me_ref: (i, 0)) if self.gather else (lambda i, me_ref, n=R // tr: (me_ref[0] * n + i, 0))
            if self.gather:
                out_spec = pl.BlockSpec(tile, lambda i, me_ref, n=steps: (me_ref[0] * n + i, 0))
            else:
                out_spec = pl.BlockSpec((None,) + tuple(t for t in tile if t is not None), lambda i, me_ref: (me_ref[0], i, 0))

            def body(me_ref, src_ref, land_ref, out_ref):
                out_ref[...] = src_ref[...]

            lands.append(_call(
                body, name=f"{self.name}_own{a}",
                grid_spec=pltpu.PrefetchScalarGridSpec(
                    num_scalar_prefetch=1, grid=(steps,),
                    in_specs=[pl.BlockSpec(tile, in_map), pl.BlockSpec(memory_space=pl.ANY)], out_specs=out_spec),
                out_shape=SDS(land.shape, land.dtype), input_output_aliases={2: 0}, compiler_params=_params(32),
            )(me, src, land))
        return operands[:na] + lands

    def _start(self, operands, after):
        nops = self.ns + self.na
        nsem = len(self._flips()) * self.na
        if self.ns and self.route != "sibling":
            operands = self._place_own(operands)

        def body(*refs):
            ins = refs[:nops]
            send_sems, recv_sems, token_ref = refs[nops + 1], refs[nops + 2], refs[2 * nops + 3]
            for cp in self._copies(ins, send_sems, recv_sems)[0]:
                cp.start()
            token_ref[...] = jnp.zeros_like(token_ref)

        hbm = pl.BlockSpec(memory_space=pltpu.HBM)
        sem = pl.BlockSpec(memory_space=pltpu.SEMAPHORE)
        outs = _call(
            body, name=self.name + "_start",
            in_specs=[hbm] * nops + [pl.BlockSpec(memory_space=pl.ANY)],
            out_specs=[sem, sem] + [hbm] * nops + [pl.BlockSpec(memory_space=pltpu.VMEM)],
            out_shape=[pltpu.SemaphoreType.DMA((nsem,)), pltpu.SemaphoreType.DMA((nsem,))] + self.kinds + [SDS((8, LANE), F32)],
            input_output_aliases={i: 2 + i for i in range(nops)},
            compiler_params=pltpu.CompilerParams(has_side_effects=pltpu.SideEffectType.DATAFLOW_SIDE_EFFECTING),
        )(*operands, after)
        self.sems, self.thru, self.token = outs[:2], outs[2:2 + nops], outs[2 + nops][0:1, 0:1]

    def wait(self, after):
        nops = self.ns + self.na

        def body(*refs):
            ins, send_sems, recv_sems = refs[:nops], refs[nops], refs[nops + 1]
            sends, arrivals = self._copies(ins, send_sems, recv_sems)
            for cp in sends:
                cp.wait_send()
            for cp in arrivals:
                cp.wait_recv()

        hbm = pl.BlockSpec(memory_space=pltpu.HBM)
        sem = pl.BlockSpec(memory_space=pltpu.SEMAPHORE)
        outs = _call(
            body, name=self.name + "_wait",
            in_specs=[hbm] * nops + [sem, sem, pl.BlockSpec(memory_space=pl.ANY)],
            out_specs=[hbm] * nops, out_shape=self.kinds,
            input_output_aliases={i: i for i in range(nops)},
            compiler_params=pltpu.CompilerParams(has_side_effects=pltpu.SideEffectType.DATAFLOW_SIDE_EFFECTING),
        )(*self.thru, *self.sems, after)
        return outs[self.ns:]


def adamw(w, m, v, parts, layer, prev, name):
    nl, R, C = w.shape
    nparts = parts.shape[0]
    tr = _tile(R, 128, 8) if R % 8 == 0 else R

    def body(w_ref, m_ref, v_ref, p_ref, *rest):
        g_ref, d_ref, mo_ref, vo_ref = rest[-4:]
        g = p_ref[0].astype(F32)
        for s in range(1, nparts):
            g = g + p_ref[s].astype(F32)
        mn = ADAM_B1 * m_ref[0] + (1.0 - ADAM_B1) * g
        vn = ADAM_B2 * v_ref[0] + (1.0 - ADAM_B2) * (g * g)
        m_hat = mn / (1.0 - ADAM_B1 ** ADAM_STEP)
        v_hat = vn / (1.0 - ADAM_B2 ** ADAM_STEP)
        g_ref[0] = g
        d_ref[0] = -ADAM_LR * (m_hat / (jnp.sqrt(v_hat) + ADAM_EPS) + ADAM_WD * w_ref[0])
        mo_ref[0] = mn
        vo_ref[0] = vn

    row = pl.BlockSpec((1, tr, C), lambda i: (layer, i, 0))
    carried = [] if prev is None else list(prev)
    return _call(
        body, name=name, grid=(R // tr,),
        in_specs=[row, row, row, pl.BlockSpec((nparts, tr, C), lambda i: (0, i, 0))] + [pl.BlockSpec(memory_space=pl.ANY)] * len(carried),
        out_specs=[row] * 4, out_shape=[SDS((nl, R, C), F32)] * 4,
        input_output_aliases={4 + k: k for k in range(len(carried))},
        compiler_params=_params(48),
    )(w, m, v, parts, *carried)


def _rope_tables(S):
    inv_freq = 1.0 / (ROPE_THETA ** (jnp.arange(0, HEAD_DIM, 2, dtype=F32) / HEAD_DIM))
    ang = jnp.arange(S, dtype=F32)[:, None] * inv_freq[None, :]
    cos, sin = jnp.cos(ang), jnp.sin(ang)
    return jnp.concatenate([cos, cos], axis=-1), jnp.concatenate([-sin, sin], axis=-1)


def _pack_small(D, norm_g, dw_bias, conv_ln_g, conv_ln_b, att_out_g, conv_out_g, q_norm_g, k_norm_g, extra=None):
    qk = jnp.concatenate([q_norm_g.reshape(-1), k_norm_g.reshape(-1)])
    qk = jnp.pad(qk, (0, D - qk.shape[0])).reshape(1, D)
    zero = jnp.zeros((1, D), F32)
    return jnp.concatenate([norm_g, dw_bias, conv_ln_g, conv_ln_b, att_out_g, conv_out_g, qk, zero,
                            zero if extra is None else extra, zero], axis=0)


def _unpack_small(p):
    rows = [p[2 * i:2 * i + 2] for i in range(6)]
    qk = p[12, :4 * HEAD_DIM].reshape(2, DEPTH, HEAD_DIM)
    return rows + [qk[0], qk[1]]


def kernel(x, norm_g, w_in, q_norm_g, k_norm_g, dw_kernel, dw_bias, conv_ln_g, conv_ln_b, w_pw, att_out_g, conv_out_g, w_out, loss_target, m_norm_g, m_w_in, m_q_norm_g, m_k_norm_g, m_dw_kernel, m_dw_bias, m_conv_ln_g, m_conv_ln_b, m_w_pw, m_att_out_g, m_conv_out_g, m_w_out, v_norm_g, v_w_in, v_q_norm_g, v_k_norm_g, v_dw_kernel, v_dw_bias, v_conv_ln_g, v_conv_ln_b, v_w_pw, v_att_out_g, v_conv_out_g, v_w_out):
    xs = x[0]
    D = xs.shape[1]
    bf = lambda t, l: t[l].astype(BF16)
    wint0, dwk_f = all_gather([bf(w_in, 0).T, dw_kernel], [0, 2], "gather_first")
    early = _Exchange(True, [bf(w_pw, 0), bf(w_out, 0)], [0, 0], "gather_layer0", after=wint0)
    later = _Exchange(True, [bf(w_in, 1).T, bf(w_pw, 1), bf(w_out, 1)], [0, 0, 0], "gather_layer1", after=early.token,
                      route="chips")
    landed = {}

    def weights(l, cur):
        if l == 0:
            return wint0, later.token
        landed[1] = landed["passing"].wait(cur)
        return landed[1][0], None

    def mixer_weights(l, cur):
        if l == 0:
            mine = early.wait(cur)
            landed["passing"] = _Exchange(True, None, [0, 0, 0], "pass_layer1", after=mine[0], route="pass",
                                          lands=later.wait(mine[0]))
            return (*mine, landed["passing"].token)
        return (*landed[1][1:], None)

    sent = [[] for _ in range(DEPTH)]

    def send_grads(l, grads, axes, tag):
        sent[l].append(_Exchange(False, grads, axes, f"scatter_{tag}_layer{l}", after=None))
        return sent[l][-1].token

    def win_grads(l, ht, dproj):
        if l > 0:
            return send_grads(l, [mm_nn(ht, dproj, BF16, f"dwin_{l}", owners=N_DEV)], [None], "w_in")
        core = jnp.reshape(lax.axis_index("c"), (1,)).astype(jnp.int32)
        theirs = mm_core_blocks(ht, dproj, 1 - core, f"dwin_{l}_for_sibling")
        to_sibling = _Exchange(False, [theirs], [None], f"scatter_w_in_sibling_layer{l}", after=None, route="sibling")
        mine = mm_core_blocks(ht, dproj, core + to_sibling.token[0].astype(jnp.int32), f"dwin_{l}_own")
        (from_sibling,) = to_sibling.wait(mine)
        both = add_blocks(mine, from_sibling, f"dwin_{l}_chip_sum")
        sent[l].append(_Exchange(False, [both], [None], f"scatter_w_in_chips_layer{l}", after=None, route="chip_parts"))
        return sent[l][-1].token

    dx, loss_cols, small = local_step(xs, loss_target[0], weights, mixer_weights, dwk_f, norm_g, q_norm_g, k_norm_g, dw_bias,
                                      conv_ln_g, conv_ln_b, att_out_g, conv_out_g, send_grads, win_grads)

    big = ((w_pw, m_w_pw, v_w_pw), (w_out, m_w_out, v_w_out), (dw_kernel, m_dw_kernel, v_dw_kernel), (w_in, m_w_in, v_w_in))
    results = [None] * len(big)
    after = dx
    for l in reversed(range(DEPTH)):
        parts = [p for ex in sent[l] for p in ex.wait(after)]
        for i, ((w, m, v), p) in enumerate(zip(big, parts)):
            results[i] = adamw(w, m, v, p, l, results[i], f"adamw_{w.shape[1]}_{w.shape[2]}_{l}")
        after = results[0][3]
    r_wpw, r_wout, r_dwk, r_win = results

    stack = lambda k: jnp.concatenate(small[k], axis=0)
    mine = _pack_small(D, stack("norm_g"), stack("dw_bias"), stack("conv_ln_g"), stack("conv_ln_b"), stack("att_out_g"),
                       stack("conv_out_g"), stack("q"), stack("k"), extra=loss_cols)
    (p_small,) = all_gather([mine], [0], "gather_small")
    pk = lambda n, dw, lg, lb, ao, co, q, k: _pack_small(D, n, dw, lg, lb, ao, co, q, k)[None]
    r_small = adamw(pk(norm_g, dw_bias, conv_ln_g, conv_ln_b, att_out_g, conv_out_g, q_norm_g, k_norm_g),
                    pk(m_norm_g, m_dw_bias, m_conv_ln_g, m_conv_ln_b, m_att_out_g, m_conv_out_g, m_q_norm_g, m_k_norm_g),
                    pk(v_norm_g, v_dw_bias, v_conv_ln_g, v_conv_ln_b, v_att_out_g, v_conv_out_g, v_q_norm_g, v_k_norm_g),
                    p_small.reshape(N_DEV, 16, D), 0, None, "adamw_small")
    r_small = [r[0] for r in r_small]
    loss = jnp.sum(r_small[0][14])

    outs = [loss, dx[None]]
    for i in range(4):
        n_, dwb, lg, lb, ao, co, q_, k_ = _unpack_small(r_small[i])
        outs += [n_, r_win[i], q_, k_, r_dwk[i], dwb, lg, lb, r_wpw[i], ao, co, r_wout[i]]
    return tuple(outs)


def local_step(xs, target, weights, mixer_weights, dwk_f, norm_g, q_norm_g, k_norm_g, dw_bias, conv_ln_g, conv_ln_b,
               att_out_g, conv_out_g, send_grads, win_grads):
    S, D = xs.shape
    cos2, sin2 = _rope_tables(S)
    dwk_f = jnp.pad(dwk_f, ((0, 0), (0, CONV_PAD - CONV_WIDTH), (0, 0)))

    def vec(p, l, zero=None):
        row = p[l].reshape(1, -1)
        return row if zero is None else row + zero

    saved = []
    cur = xs
    for l in range(DEPTH):
        wint, zero = weights(l, cur)
        proj, _, h_t = in_proj(cur, vec(norm_g, l, zero), wint, f"in_proj_{l}")
        qs, ks, vs = qk_prep(proj, vec(q_norm_g, l), vec(k_norm_g, l), cos2, sin2, f"qk_prep_{l}")
        os_, lses = [], []
        for i, d in enumerate(DILATIONS):
            o, lse = attn_fwd(qs[i], ks[i], vs[i], d, f"attn_fwd_{l}_d{d}")
            os_.append(o)
            lses.append(_from_branch(lse))
        att, cat, lse, cat_t = att_combine(os_, lses, proj, vec(att_out_g, l), f"att_combine_{l}")
        wpw, wout, zero = mixer_weights(l, lse)
        cat, cat_t, z_t, conv, y = conv_fwd(proj, dwk_f[l], vec(dw_bias, l, zero), vec(conv_ln_g, l), vec(conv_ln_b, l),
                                            vec(conv_out_g, l), wpw, cat, cat_t, f"conv_fwd_{l}")
        nxt = mm_nn(cat, wout, F32, f"out_proj_{l}", add=cur)
        saved.append(dict(x=cur, proj=proj, h_t=h_t, qs=qs, ks=ks, vs=vs, att=att, lse=lse, cat_t=cat_t, z_t=z_t, conv=conv,
                          y=y, wint=wint, wpw=wpw, wout=wout))
        cur = nxt

    dx, dxb, loss_cols = loss_head(cur, target, "loss_head")

    small = {k: [None] * DEPTH for k in ("norm_g", "dw_bias", "conv_ln_g", "conv_ln_b", "att_out_g", "conv_out_g", "q", "k")}
    for l in reversed(range(DEPTH)):
        sv = saved[l]
        proj = sv["proj"]
        dcat = mm_nt(dxb, sv["wout"], F32, f"dcat_{l}")
        g_wout = mm_nn(sv["cat_t"], dxb, BF16, f"dwout_{l}")
        dproj, small["conv_out_g"][l], dconv = gate_bwd(dcat, 1, sv["conv"], proj, 6, vec(conv_out_g, l), False, None,
                                                        f"conv_gate_bwd_{l}")
        dy, small["conv_ln_g"][l], small["conv_ln_b"][l], small["dw_bias"][l] = conv_bwd_ln(
            dconv, sv["wpw"], sv["y"], vec(conv_ln_g, l), vec(conv_ln_b, l), f"conv_bwd_ln_{l}")
        g_wpw = mm_nn(sv["z_t"], dconv, BF16, f"dwpw_{l}")
        dproj, gw = conv_bwd_dw(dy, proj, dwk_f[l], dproj, f"conv_bwd_dw_{l}")
        g_dwk = jnp.sum(gw, axis=1)[:CONV_WIDTH]
        zero = send_grads(l, [g_wpw, g_wout, g_dwk], [0, 0, 1], "mixer")
        dproj, small["att_out_g"][l], *datts, delta = gate_bwd(dcat, 0, sv["att"], proj, 3, vec(att_out_g, l, zero), True,
                                                                dproj, f"att_gate_bwd_{l}")
        dqs, dks, dvs = [], [], []
        for i, d in enumerate(DILATIONS):
            dq, dk, dv = attn_bwd(sv["qs"][i], sv["ks"][i], sv["vs"][i], datts[i], _to_branch(sv["lse"], d),
                                  _to_branch(delta, d), d, f"attn_bwd_{l}_d{d}")
            dqs.append(dq)
            dks.append(dk)
            dvs.append(dv)
        dproj, small["q"][l], small["k"][l] = qk_prep_bwd(dqs, dks, dvs, proj, vec(q_norm_g, l), vec(k_norm_g, l),
                                                          cos2, sin2, dproj, f"qk_prep_bwd_{l}")
        zero = win_grads(l, sv["h_t"], dproj)
        dx, dxb, small["norm_g"][l] = in_proj_bwd(dproj, sv["wint"], sv["x"], vec(norm_g, l, zero), dx, f"in_proj_bwd_{l}")

    return dx, loss_cols, small
```

```python
import jax
import jax.numpy as jnp
from jax import lax
from jax.experimental import pallas as pl
from jax.experimental.pallas import tpu as pltpu

F32 = jnp.float32
BF16 = jnp.bfloat16
SDS = jax.ShapeDtypeStruct
MESH = pl.DeviceIdType.MESH

N_DEV = 8
DEPTH = 2
HEAD_DIM = 128
CONV_WIDTH = 31
CONV_PAD = 32
DILATIONS = (1, 4, 16)
Q_BLOCK = 128
ROPE_THETA = 10000.0
EPS = 1e-6
NEG = -1e30
ADAM_LR, ADAM_B1, ADAM_B2, ADAM_EPS, ADAM_WD, ADAM_STEP = 0.001, 0.9, 0.999, 1e-08, 0.01, 10
LANE = 128
ROW_CHUNK = 64
MIB = 1 << 20
NT = (((1,), (1,)), ((), ()))
TN = (((0,), (0,)), ((), ()))


def _call(body, **kw):
    return pl.pallas_call(body, **kw)


def _params(vmem_mib):
    return pltpu.CompilerParams(vmem_limit_bytes=vmem_mib * MIB)


def _tile(dim, pref, mult):
    t = min(pref, dim)
    while dim % t or t % mult:
        t -= mult
    return t


def _sig(v):
    return jax.nn.sigmoid(v)


def _rstd(v):
    return lax.rsqrt(jnp.mean(v * v, axis=-1, keepdims=True) + EPS)


def _row(tm, cb, c=0):
    return pl.BlockSpec((tm, cb), lambda i: (i, c))


def _full(shape):
    return pl.BlockSpec(shape, lambda i: (0,) * len(shape))


def _halo_prev(tm, cb, c=0):
    k = tm // CONV_PAD
    return pl.BlockSpec((CONV_PAD, cb), lambda i: (jnp.maximum(i * k - 1, 0), c))


def _halo_next(tm, cb, nblk, c=0):
    k = tm // CONV_PAD
    return pl.BlockSpec((CONV_PAD, cb), lambda i: (jnp.minimum((i + 1) * k, nblk - 1), c))


def in_proj(x, g, wt, name):
    S, D = x.shape
    N = wt.shape[0]
    tm, tn = _tile(S, 512, 8), _tile(N, 1792, LANE)

    def body(x_ref, g_ref, w_ref, o_ref, h_ref, ht_ref):
        @pl.when(pl.program_id(1) == 0)
        def _():
            xf = x_ref[...]
            hf = xf * _rstd(xf) * g_ref[...]
            h_ref[...] = hf.astype(BF16)
            ht_ref[...] = hf.T.astype(BF16)

        o_ref[...] = lax.dot_general(h_ref[...], w_ref[...], NT, preferred_element_type=F32)

    return _call(
        body, name=name, grid=(S // tm, N // tn),
        in_specs=[pl.BlockSpec((tm, D), lambda i, j: (i, 0)), pl.BlockSpec((1, D), lambda i, j: (0, 0)),
                  pl.BlockSpec((tn, D), lambda i, j: (j, 0))],
        out_specs=[pl.BlockSpec((tm, tn), lambda i, j: (i, j)), pl.BlockSpec((tm, D), lambda i, j: (i, 0)),
                   pl.BlockSpec((D, tm), lambda i, j: (0, i))],
        out_shape=[SDS((S, N), F32), SDS((S, D), BF16), SDS((D, S), BF16)],
        compiler_params=_params(48),
    )(x, g, wt)


def mm_nn(a, b, out_dtype, name, add=None, owners=1):
    M, K = a.shape
    N = b.shape[1]
    tm, tn = _tile(M, 1024 if owners > 1 else 512, 8), _tile(N // owners, 1024, LANE)
    per = N // owners // tn

    def body(*refs):
        a_ref, b_ref = refs[0], refs[1]
        o_ref = refs[-1]
        acc = jnp.dot(a_ref[...], b_ref[...], preferred_element_type=F32)
        if add is not None:
            acc = acc + refs[2][...]
        o_ref[...] = acc.astype(out_dtype)

    in_specs = [pl.BlockSpec((tm, K), lambda i, j: (i, 0)), pl.BlockSpec((K, tn), lambda i, j: (0, j))]
    args = [a, b]
    if add is not None:
        in_specs.append(pl.BlockSpec((tm, tn), lambda i, j: (i, j)))
        args.append(add)
    if owners > 1:
        out_spec = pl.BlockSpec((None, tm, tn), lambda i, j: (j // per, i, j % per))
        out_shape = SDS((owners, M, N // owners), out_dtype)
    else:
        out_spec, out_shape = pl.BlockSpec((tm, tn), lambda i, j: (i, j)), SDS((M, N), out_dtype)
    return _call(
        body, name=name, grid=(M // tm, N // tn), in_specs=in_specs, out_specs=out_spec, out_shape=out_shape,
        compiler_params=_params(48),
    )(*args)


def mm_core_blocks(a, b, core, name):
    M, K = a.shape
    blk = b.shape[1] // N_DEV
    tm, tn = _tile(M, 1024, 8), _tile(blk, 1024, LANE)
    per = blk // tn

    def body(core_ref, a_ref, b_ref, o_ref):
        o_ref[...] = jnp.dot(a_ref[...], b_ref[...], preferred_element_type=F32).astype(BF16)

    return _call(
        body, name=name,
        grid_spec=pltpu.PrefetchScalarGridSpec(
            num_scalar_prefetch=1, grid=(M // tm, 4 * per),
            in_specs=[pl.BlockSpec((tm, K), lambda i, j, core_ref: (i, 0)),
                      pl.BlockSpec((K, tn), lambda i, j, core_ref: (0, (2 * (j // per) + core_ref[0]) * per + j % per))],
            out_specs=pl.BlockSpec((None, tm, tn), lambda i, j, core_ref: (j // per, i, j % per))),
        out_shape=SDS((4, M, blk), BF16), compiler_params=_params(48),
    )(core, a, b)


def add_blocks(a, b, name):
    n, R, C = a.shape
    tr = _tile(R, 512, 16)
    spec = pl.BlockSpec((None, tr, C), lambda k, i: (k, i, 0))

    def body(a_ref, b_ref, o_ref):
        o_ref[...] = (a_ref[...].astype(F32) + b_ref[...].astype(F32)).astype(BF16)

    return _call(body, name=name, grid=(n, R // tr), in_specs=[spec, spec], out_specs=spec, out_shape=SDS(a.shape, BF16),
                 compiler_params=_params(32))(a, b)


def mm_nt(a, b, out_dtype, name):
    M, K = a.shape
    N = b.shape[0]
    tm, tn = _tile(M, 512, 8), _tile(N, 1024, LANE)

    def body(a_ref, b_ref, o_ref):
        o_ref[...] = lax.dot_general(a_ref[...], b_ref[...], NT, preferred_element_type=F32).astype(out_dtype)

    return _call(
        body, name=name, grid=(M // tm, N // tn),
        in_specs=[pl.BlockSpec((tm, K), lambda i, j: (i, 0)), pl.BlockSpec((tn, K), lambda i, j: (j, 0))],
        out_specs=pl.BlockSpec((tm, tn), lambda i, j: (i, j)), out_shape=SDS((M, N), out_dtype),
        compiler_params=_params(40),
    )(a, b)


def in_proj_bwd(dproj, wt, x, g, dx_out, name):
    S, K = dproj.shape
    D = wt.shape[1]
    tm, tk = _tile(S, 512, 8), _tile(K, 1024, LANE)
    nk = K // tk

    def body(dp_ref, w_ref, x_ref, g_ref, dxo_ref, dx_ref, dxb_ref, gacc_ref):
        i, k = pl.program_id(0), pl.program_id(1)
        part = jnp.dot(dp_ref[...], w_ref[...], preferred_element_type=F32)

        @pl.when(k == 0)
        def _():
            dx_ref[...] = part

        @pl.when(k > 0)
        def _():
            dx_ref[...] += part

        @pl.when((k == 0) & (i == 0))
        def _():
            gacc_ref[...] = jnp.zeros_like(gacc_ref)

        @pl.when(k == nk - 1)
        def _():
            dh = dx_ref[...]
            xf = x_ref[...]
            r = _rstd(xf)
            n = xf * r
            gacc_ref[...] += jnp.sum(dh * n, axis=0, keepdims=True)
            dn = dh * g_ref[...]
            dx = r * (dn - n * jnp.mean(dn * n, axis=-1, keepdims=True)) + dxo_ref[...]
            dx_ref[...] = dx
            dxb_ref[...] = dx.astype(BF16)

    return _call(
        body, name=name, grid=(S // tm, nk),
        in_specs=[pl.BlockSpec((tm, tk), lambda i, k: (i, k)), pl.BlockSpec((tk, D), lambda i, k: (k, 0)),
                  pl.BlockSpec((tm, D), lambda i, k: (i, 0)), pl.BlockSpec((1, D), lambda i, k: (0, 0)),
                  pl.BlockSpec((tm, D), lambda i, k: (i, 0))],
        out_specs=[pl.BlockSpec((tm, D), lambda i, k: (i, 0)), pl.BlockSpec((tm, D), lambda i, k: (i, 0)),
                   pl.BlockSpec((1, D), lambda i, k: (0, 0))],
        out_shape=[SDS((S, D), F32), SDS((S, D), BF16), SDS((1, D), F32)],
        compiler_params=_params(54),
    )(dproj, wt, x, g, dx_out)


def _dil_specs(S, DA, tm, dtype):
    specs = [pl.BlockSpec((tm // d, d * DA), lambda i: (i, 0)) for d in DILATIONS]
    shapes = [SDS((S // d, d * DA), dtype) for d in DILATIONS]
    return specs, shapes


def _head_buf(tm, DA):
    return pltpu.VMEM((DA // HEAD_DIM, tm, HEAD_DIM), F32)


def _emit_dilated(buf_ref, dsts, tm, DA):
    for d, dst in zip(DILATIONS, dsts):
        for h in range(DA // HEAD_DIM):
            for r in range(d):
                rows = slice(None) if d == 1 else pl.ds(r, tm // d, stride=d)
                dst[:, r * DA + h * HEAD_DIM:r * DA + (h + 1) * HEAD_DIM] = buf_ref.at[h][rows, :].astype(BF16)


def _collect_dilated(acc_ref, parts, tm, DA):
    for d, p in zip(DILATIONS, parts):
        for h in range(DA // HEAD_DIM):
            for r in range(d):
                part = p[:, r * DA + h * HEAD_DIM:r * DA + (h + 1) * HEAD_DIM].astype(F32)
                if d == 1:
                    acc_ref[h] = part
                else:
                    rows = pl.ds(r, tm // d, stride=d)
                    acc_ref.at[h][rows, :] = acc_ref.at[h][rows, :] + part


def qk_prep(proj, gq, gk, cos2, sin2, name):
    S = proj.shape[0]
    DA = proj.shape[1] // 7
    H = DA // HEAD_DIM
    tm = _tile(S, 256, 16 * DILATIONS[-1])
    nd = len(DILATIONS)

    def body(q_ref, k_ref, v_ref, gq_ref, gk_ref, c_ref, s_ref, *rest):
        outs, buf_ref = rest[:3 * nd], rest[3 * nd]
        ct, st = c_ref[...], s_ref[...]
        for t, (src, g_ref) in enumerate(((q_ref, gq_ref), (k_ref, gk_ref))):
            gain = g_ref[...]
            for h in range(H):
                sl = slice(h * HEAD_DIM, (h + 1) * HEAD_DIM)
                xh = src[:, sl]
                n = xh * _rstd(xh) * gain
                buf_ref[h] = n * ct + pltpu.roll(n, HEAD_DIM // 2, 1) * st
            _emit_dilated(buf_ref, outs[t * nd:(t + 1) * nd], tm, DA)
        for h in range(H):
            buf_ref[h] = v_ref[:, h * HEAD_DIM:(h + 1) * HEAD_DIM]
        _emit_dilated(buf_ref, outs[2 * nd:], tm, DA)

    specs, shapes = _dil_specs(S, DA, tm, BF16)
    outs = _call(
        body, name=name, grid=(S // tm,),
        in_specs=[_row(tm, DA, 0), _row(tm, DA, 1), _row(tm, DA, 2), _full((1, HEAD_DIM)), _full((1, HEAD_DIM)),
                  _row(tm, HEAD_DIM), _row(tm, HEAD_DIM)],
        out_specs=specs * 3, out_shape=shapes * 3, scratch_shapes=[_head_buf(tm, DA)],
        compiler_params=_params(48),
    )(proj, proj, proj, gq, gk, cos2, sin2)
    return outs[:nd], outs[nd:2 * nd], outs[2 * nd:]


def qk_prep_bwd(dqs, dks, dvs, proj, gq, gk, cos2, sin2, dproj, name):
    S = proj.shape[0]
    DA = proj.shape[1] // 7
    H = DA // HEAD_DIM
    tm = _tile(S, 256, 16 * DILATIONS[-1])
    nb = len(dqs)

    def body(*refs):
        dq_refs, dk_refs, dv_refs = refs[:nb], refs[nb:2 * nb], refs[2 * nb:3 * nb]
        q_ref, k_ref, gq_ref, gk_ref, c_ref, s_ref = refs[3 * nb:3 * nb + 6]
        out_ref, gqa_ref, gka_ref, acc_ref = refs[3 * nb + 7:]
        ct, st = c_ref[...], s_ref[...]

        @pl.when(pl.program_id(0) == 0)
        def _():
            gqa_ref[...] = jnp.zeros_like(gqa_ref)
            gka_ref[...] = jnp.zeros_like(gka_ref)

        for parts, x_ref, g_ref, col, gacc in ((dq_refs, q_ref, gq_ref, 0, gqa_ref),
                                               (dk_refs, k_ref, gk_ref, DA, gka_ref)):
            gain = g_ref[...]
            gsum = jnp.zeros((1, HEAD_DIM), F32)
            _collect_dilated(acc_ref, parts, tm, DA)
            for h in range(H):
                sl = slice(h * HEAD_DIM, (h + 1) * HEAD_DIM)
                dout = acc_ref[h]
                dn = dout * ct + pltpu.roll(dout * st, HEAD_DIM // 2, 1)
                xh = x_ref[:, sl]
                r = _rstd(xh)
                xn = xh * r
                gsum = gsum + jnp.sum(dn * xn, axis=0, keepdims=True)
                dnn = dn * gain
                dx = r * (dnn - xn * jnp.mean(dnn * xn, axis=-1, keepdims=True))
                out_ref[:, col + h * HEAD_DIM:col + (h + 1) * HEAD_DIM] = dx.astype(BF16)
            gacc[...] += gsum
        _collect_dilated(acc_ref, dv_refs, tm, DA)
        for h in range(H):
            out_ref[:, 2 * DA + h * HEAD_DIM:2 * DA + (h + 1) * HEAD_DIM] = acc_ref[h].astype(BF16)

    specs, _ = _dil_specs(S, DA, tm, BF16)
    return _call(
        body, name=name, grid=(S // tm,),
        in_specs=specs * 3 + [_row(tm, DA, 0), _row(tm, DA, 1), _full((1, HEAD_DIM)),
                              _full((1, HEAD_DIM)), _row(tm, HEAD_DIM), _row(tm, HEAD_DIM), pl.BlockSpec(memory_space=pl.ANY)],
        out_specs=[_row(tm, 3 * DA)] + [_full((1, HEAD_DIM))] * 2,
        out_shape=[SDS((S, 7 * DA), BF16)] + [SDS((1, HEAD_DIM), F32)] * 2,
        input_output_aliases={3 * nb + 6: 0},
        scratch_shapes=[_head_buf(tm, DA)],
        compiler_params=_params(48),
    )(*dqs, *dks, *dvs, proj, proj, gq, gk, cos2, sin2, dproj)


def _band_mask(n):
    row = lax.broadcasted_iota(jnp.int32, (Q_BLOCK, 2 * Q_BLOCK), 0)
    col = lax.broadcasted_iota(jnp.int32, (Q_BLOCK, 2 * Q_BLOCK), 1)
    first = jnp.where(n > 0, Q_BLOCK, 2 * Q_BLOCK + 1)
    return (col <= row) | ((col - row) >= first)


def attn_fwd(qh, kh, vb, d, name):
    L = qh.shape[0]
    DA = qh.shape[1] // d
    H = DA // HEAD_DIM
    nb = L // Q_BLOCK
    scale = HEAD_DIM ** -0.5
    view = (L, d * DA)

    def body(q_ref, kc_ref, kp_ref, vc_ref, vp_ref, o_ref, lse_ref):
        mask = _band_mask(pl.program_id(1))
        ones = jnp.ones((2 * Q_BLOCK, HEAD_DIM), BF16)
        for h in range(H):
            sl = slice(h * HEAD_DIM, (h + 1) * HEAD_DIM)
            keys = jnp.concatenate([kc_ref[:, sl], kp_ref[:, sl]], axis=0)
            s = lax.dot_general(q_ref[:, sl], keys, NT, preferred_element_type=F32) * scale
            s = jnp.where(mask, s, NEG)
            m = jnp.max(s, axis=-1, keepdims=True)
            p = jnp.exp(s - m).astype(BF16)
            vals = jnp.concatenate([jnp.concatenate([vc_ref[:, sl], vp_ref[:, sl]], axis=0), ones], axis=1)
            ol = jnp.dot(p, vals, preferred_element_type=F32)
            l = ol[:, HEAD_DIM:]
            o_ref[:, sl] = ol[:, :HEAD_DIM] / l
            lse_ref[0, :, h:h + 1] = m + jnp.log(l[:, 0:1])

    cur = pl.BlockSpec((Q_BLOCK, DA), lambda r, n: (n, r))
    prev = pl.BlockSpec((Q_BLOCK, DA), lambda r, n: (jnp.maximum(n - 1, 0), r))
    o, lse = _call(
        body, name=name, grid=(d, nb), in_specs=[cur, cur, prev, cur, prev],
        out_specs=[cur, pl.BlockSpec((1, Q_BLOCK, H), lambda r, n: (r, n, 0))],
        out_shape=[SDS(view, F32), SDS((d, L, H), F32)],
        compiler_params=_params(32),
    )(qh, kh, kh, vb, vb)
    return o, lse


def attn_bwd(qh, kh, vb, da, lse_d, delta_d, d, name):
    L = qh.shape[0]
    DA = qh.shape[1] // d
    H = DA // HEAD_DIM
    nb = L // Q_BLOCK
    scale = HEAD_DIM ** -0.5
    view = (L, d * DA)
    resident = d > 1

    def body(q_ref, kc_ref, kp_ref, vc_ref, vp_ref, do_ref, lse_ref, dl_ref, dq_ref, dk_ref, dv_ref, dkc_ref, dvc_ref):
        n = pl.program_id(1)
        mask = _band_mask(n)
        if resident:
            done = pl.ds(pl.multiple_of(jnp.maximum(n - 1, 0) * Q_BLOCK, Q_BLOCK), Q_BLOCK)
            fresh = pl.ds(pl.multiple_of(n * Q_BLOCK, Q_BLOCK), Q_BLOCK)
        else:
            done = slice(None)

        @pl.when(n == 0)
        def _():
            dkc_ref[...] = jnp.zeros_like(dkc_ref)
            dvc_ref[...] = jnp.zeros_like(dvc_ref)

        @pl.when(n < nb)
        def _():
            for h in range(H):
                sl = slice(h * HEAD_DIM, (h + 1) * HEAD_DIM)
                q, do = q_ref[:, sl], do_ref[:, sl]
                keys = jnp.concatenate([kc_ref[:, sl], kp_ref[:, sl]], axis=0)
                vals = jnp.concatenate([vc_ref[:, sl], vp_ref[:, sl]], axis=0)
                lse = jnp.broadcast_to(lse_ref[0, :, h:h + 1], (Q_BLOCK, 2 * Q_BLOCK))
                dl = jnp.broadcast_to(dl_ref[0, :, h:h + 1], (Q_BLOCK, 2 * Q_BLOCK))
                s = lax.dot_general(q, keys, NT, preferred_element_type=F32) * scale
                p = jnp.exp(jnp.where(mask, s, NEG) - lse)
                dp = lax.dot_general(do, vals, NT, preferred_element_type=F32)
                ds = (p * (dp - dl) * scale).astype(BF16)
                dq_ref[:, sl] = jnp.dot(ds, keys, preferred_element_type=F32).astype(BF16)
                dk = lax.dot_general(ds, q, TN, preferred_element_type=F32)
                dv = lax.dot_general(p.astype(BF16), do, TN, preferred_element_type=F32)
                dk_ref[done, sl] = (dkc_ref[:, sl] + dk[Q_BLOCK:]).astype(BF16)
                dv_ref[done, sl] = (dvc_ref[:, sl] + dv[Q_BLOCK:]).astype(BF16)
                dkc_ref[:, sl] = dk[:Q_BLOCK]
                dvc_ref[:, sl] = dv[:Q_BLOCK]
                if resident:
                    dk_ref[fresh, sl] = dk[:Q_BLOCK].astype(BF16)
                    dv_ref[fresh, sl] = dv[:Q_BLOCK].astype(BF16)

        if not resident:
            @pl.when(n == nb)
            def _():
                dk_ref[...] = dkc_ref[...].astype(BF16)
                dv_ref[...] = dvc_ref[...].astype(BF16)

    steps = nb if resident else nb + 1
    cur = pl.BlockSpec((Q_BLOCK, DA), lambda r, n: (jnp.minimum(n, nb - 1), r))
    prev = pl.BlockSpec((Q_BLOCK, DA), lambda r, n: (jnp.clip(n - 1, 0, nb - 1), r))
    if resident:
        keyside = pl.BlockSpec((L, DA), lambda r, n: (0, r))
    else:
        keyside = pl.BlockSpec((Q_BLOCK, DA), lambda r, n: (jnp.maximum(n - 1, 0), r))
    stat = pl.BlockSpec((1, Q_BLOCK, H), lambda r, n: (r, jnp.minimum(n, nb - 1), 0))
    dq, dk, dv = _call(
        body, name=name, grid=(d, steps), in_specs=[cur, cur, prev, cur, prev, cur, stat, stat],
        out_specs=[cur, keyside, keyside], out_shape=[SDS(view, BF16)] * 3,
        scratch_shapes=[pltpu.VMEM((Q_BLOCK, DA), F32), pltpu.VMEM((Q_BLOCK, DA), F32)],
        compiler_params=_params(40),
    )(qh, kh, kh, vb, vb, da, lse_d, delta_d)
    return dq, dk, dv


def _to_branch(stat, d):
    S, H = stat.shape
    return stat.reshape(S // d, d, H).transpose(1, 0, 2)


def _from_branch(stat):
    d, L, H = stat.shape
    return stat.transpose(1, 0, 2).reshape(L * d, H)


def att_combine(os_, lses, proj, gain, name):
    S, DA = os_[0].shape
    H = DA // HEAD_DIM
    tm = _tile(S, 256, 16 * DILATIONS[-1])
    nb = len(os_)

    def body(*refs):
        o_views, l_refs = refs[:nb], refs[nb:2 * nb]
        gate_ref, gain_ref, att_ref, y_ref, lse_ref, yt_ref = refs[2 * nb:2 * nb + 6]
        bufs = refs[2 * nb + 6:]
        for d, view, buf in zip(DILATIONS[1:], o_views[1:], bufs):
            for h in range(H):
                for r in range(d):
                    buf.at[h][pl.ds(r, tm // d, stride=d), :] = view[:, r * DA + h * HEAD_DIM:r * DA + (h + 1) * HEAD_DIM]
        ls = [r[...] for r in l_refs]
        top = ls[0]
        for l in ls[1:]:
            top = jnp.maximum(top, l)
        den = jnp.exp(ls[0] - top)
        for l in ls[1:]:
            den = den + jnp.exp(l - top)
        lse = top + jnp.log(den)
        lse_ref[...] = lse
        ws = [jnp.exp(l - lse) for l in ls]
        for h in range(H):
            sl = slice(h * HEAD_DIM, (h + 1) * HEAD_DIM)
            acc = ws[0][:, h:h + 1] * o_views[0][:, sl]
            for w, buf in zip(ws[1:], bufs):
                acc = acc + w[:, h:h + 1] * buf[h]
            att_ref[:, sl] = acc
        a = att_ref[...]
        g = gate_ref[...]
        y = a * _rstd(a) * gain_ref[...] * (g * _sig(g))
        y_ref[...] = y.astype(BF16)
        yt_ref[...] = y.T.astype(BF16)

    specs, _ = _dil_specs(S, DA, tm, F32)
    return _call(
        body, name=name, grid=(S // tm,),
        in_specs=specs + [_row(tm, H)] * nb + [_row(tm, DA, 3), _full((1, DA))],
        out_specs=[_row(tm, DA), _row(tm, DA), _row(tm, H), pl.BlockSpec((DA, tm), lambda i: (0, i))],
        out_shape=[SDS((S, DA), F32), SDS((S, 2 * DA), BF16), SDS((S, H), F32), SDS((2 * DA, S), BF16)],
        scratch_shapes=[_head_buf(tm, DA)] * (nb - 1),
        compiler_params=_params(48),
    )(*os_, *lses, proj, gain)


def gate_bwd(dcat, cblk, a, proj, gate_blk, gain, dilated, dproj, name):
    S, DA = a.shape
    H = DA // HEAD_DIM
    tm = _tile(S, 256, 16 * DILATIONS[-1])
    nd = len(DILATIONS) if dilated else 1

    def body(dy_ref, a_ref, gate_ref, gain_ref, *rest):
        dg_ref, gacc_ref, *rest = rest[0 if dproj is None else 1:]

        @pl.when(pl.program_id(0) == 0)
        def _():
            gacc_ref[...] = jnp.zeros_like(gacc_ref)

        dy, av, g, gain_v = dy_ref[...], a_ref[...], gate_ref[...], gain_ref[...]
        r = _rstd(av)
        n = av * r
        sg = _sig(g)
        dg_ref[...] = (dy * (n * gain_v) * (sg * (1.0 + g * (1.0 - sg)))).astype(BF16)
        drn = dy * (g * sg)
        gacc_ref[...] += jnp.sum(drn * n, axis=0, keepdims=True)
        dn = drn * gain_v
        da = r * (dn - n * jnp.mean(dn * n, axis=-1, keepdims=True))
        if dilated:
            da_refs, delta_ref, buf_ref = rest[:nd], rest[nd], rest[nd + 1]
            for h in range(H):
                buf_ref[h] = da[:, h * HEAD_DIM:(h + 1) * HEAD_DIM]
            _emit_dilated(buf_ref, da_refs, tm, DA)
            prod = da * av
            for h in range(H):
                delta_ref[:, h:h + 1] = jnp.sum(prod[:, h * HEAD_DIM:(h + 1) * HEAD_DIM], axis=-1, keepdims=True)
        else:
            rest[0][...] = da.astype(BF16)

    out_specs = [_row(tm, DA, gate_blk), _full((1, DA))]
    out_shape = [SDS((S, 7 * DA), BF16), SDS((1, DA), F32)]
    scratch = []
    carried = [] if dproj is None else [dproj]
    if dilated:
        specs, shapes = _dil_specs(S, DA, tm, BF16)
        out_specs += specs + [_row(tm, H)]
        out_shape += shapes + [SDS((S, H), F32)]
        scratch = [_head_buf(tm, DA)]
    else:
        out_specs.append(_row(tm, DA))
        out_shape.append(SDS((S, DA), BF16))
    return _call(
        body, name=name, grid=(S // tm,),
        in_specs=[_row(tm, DA, cblk), _row(tm, DA), _row(tm, DA, gate_blk), _full((1, DA))] + [pl.BlockSpec(memory_space=pl.ANY)] * len(carried),
        out_specs=out_specs, out_shape=out_shape, scratch_shapes=scratch, compiler_params=_params(48),
        input_output_aliases={4: 0} if carried else {},
    )(dcat, a, proj, gain, *carried)


def _by_sublane_phase(offsets):
    groups = [(p, [o for o in offsets if o % 8 == p]) for p in range(8)]
    return [(p, sorted(os_)) for p, os_ in groups if os_]


def _shifted_rows(src_ref, tmp_ref, base, phase, offsets, lanes):
    if phase == 0:
        return lambda o: src_ref[pl.ds(base + o, ROW_CHUNK), lanes]
    span = offsets[-1] - phase + ROW_CHUNK
    tmp_ref[phase, pl.ds(0, span), :] = src_ref[pl.ds(base + phase, span), lanes]
    return lambda o: tmp_ref[phase, pl.ds(o - phase, ROW_CHUNK), :]


def _shift_scratch():
    return pltpu.VMEM((8, CONV_PAD + ROW_CHUNK, LANE), F32)


def _fill_u(i, a_ref, b_ref, ah_ref, bh_ref, uext_ref, tm):
    uext_ref[pl.ds(CONV_PAD, tm), :] = a_ref[...] * _sig(b_ref[...])
    uh = ah_ref[...] * _sig(bh_ref[...])
    uext_ref[pl.ds(0, CONV_PAD), :] = jnp.where(i > 0, uh, 0.0)


def conv_fwd(proj, wk, bias, ln_g, ln_b, out_g, wpw, cat, cat_t, name):
    S = proj.shape[0]
    DC = proj.shape[1] // 7
    tm = _tile(S, 128, ROW_CHUNK)
    lead = CONV_PAD - (CONV_WIDTH - 1)

    def body(a_ref, b_ref, ah_ref, bh_ref, gate_ref, wk_ref, bias_ref, lg_ref, lb_ref, og_ref, wpw_ref, cat_ref, catt_ref,
             cy_ref, cyt_ref, zt_ref, conv_ref, y_ref, uext_ref, tmp_ref):
        _fill_u(pl.program_id(0), a_ref, b_ref, ah_ref, bh_ref, uext_ref, tm)

        def cols(cc, carry):
            c0 = pl.multiple_of(cc * LANE, LANE)
            lanes = pl.ds(c0, LANE)
            for rr in range(tm // ROW_CHUNK):
                acc = jnp.broadcast_to(bias_ref[:, lanes], (ROW_CHUNK, LANE))
                for phase, offsets in _by_sublane_phase(range(lead, lead + CONV_WIDTH)):
                    rows = _shifted_rows(uext_ref, tmp_ref, rr * ROW_CHUNK, phase, offsets, lanes)
                    for o in offsets:
                        acc = acc + wk_ref[o - lead:o - lead + 1, lanes] * rows(o)
                y_ref[pl.ds(rr * ROW_CHUNK, ROW_CHUNK), lanes] = acc
            return carry

        lax.fori_loop(0, DC // LANE, cols, 0)
        y = y_ref[...]
        yc = y - jnp.mean(y, axis=-1, keepdims=True)
        ln = yc * _rstd(yc) * lg_ref[...] + lb_ref[...]
        z = ln * _sig(ln)
        zt_ref[...] = z.T.astype(BF16)
        conv = jnp.dot(z.astype(BF16), wpw_ref[...], preferred_element_type=F32)
        conv_ref[...] = conv
        g = gate_ref[...]
        cy = conv * _rstd(conv) * og_ref[...] * (g * _sig(g))
        cy_ref[...] = cy.astype(BF16)
        cyt_ref[...] = cy.T.astype(BF16)

    vec = _full((1, DC))
    anywhere = pl.BlockSpec(memory_space=pl.ANY)
    return _call(
        body, name=name, grid=(S // tm,),
        in_specs=[_row(tm, DC, 4), _row(tm, DC, 5), _halo_prev(tm, DC, 4), _halo_prev(tm, DC, 5), _row(tm, DC, 6),
                  _full((CONV_PAD, DC)), vec, vec, vec, vec, _full((DC, DC)), anywhere, anywhere],
        out_specs=[_row(tm, DC, 1), pl.BlockSpec((DC, tm), lambda i: (1, i)), pl.BlockSpec((DC, tm), lambda i: (0, i)),
                   _row(tm, DC), _row(tm, DC)],
        out_shape=[SDS((S, 2 * DC), BF16), SDS((2 * DC, S), BF16), SDS((DC, S), BF16), SDS((S, DC), F32), SDS((S, DC), F32)],
        input_output_aliases={11: 0, 12: 1},
        scratch_shapes=[pltpu.VMEM((CONV_PAD + tm, DC), F32), _shift_scratch()],
        compiler_params=_params(48),
    )(proj, proj, proj, proj, proj, wk, bias, ln_g, ln_b, out_g, wpw, cat, cat_t)


def conv_bwd_ln(dconv, wpw, y, ln_g, ln_b, name):
    S, DC = y.shape
    tm = _tile(S, 256, 8)

    def body(dc_ref, wpw_ref, y_ref, lg_ref, lb_ref, dy_ref, glg_ref, glb_ref, gb_ref):
        @pl.when(pl.program_id(0) == 0)
        def _():
            glg_ref[...] = jnp.zeros_like(glg_ref)
            glb_ref[...] = jnp.zeros_like(glb_ref)
            gb_ref[...] = jnp.zeros_like(gb_ref)

        dz = lax.dot_general(dc_ref[...], wpw_ref[...], NT, preferred_element_type=F32)
        yv = y_ref[...]
        yc = yv - jnp.mean(yv, axis=-1, keepdims=True)
        rstd = _rstd(yc)
        yhat = yc * rstd
        ln = yhat * lg_ref[...] + lb_ref[...]
        sg = _sig(ln)
        dln = dz * (sg * (1.0 + ln * (1.0 - sg)))
        glb_ref[...] += jnp.sum(dln, axis=0, keepdims=True)
        glg_ref[...] += jnp.sum(dln * yhat, axis=0, keepdims=True)
        dyh = dln * lg_ref[...]
        dy = rstd * (dyh - jnp.mean(dyh, axis=-1, keepdims=True) - yhat * jnp.mean(dyh * yhat, axis=-1, keepdims=True))
        dy_ref[...] = dy
        gb_ref[...] += jnp.sum(dy, axis=0, keepdims=True)

    vec = _full((1, DC))
    return _call(
        body, name=name, grid=(S // tm,),
        in_specs=[_row(tm, DC), _full((DC, DC)), _row(tm, DC), vec, vec],
        out_specs=[_row(tm, DC), vec, vec, vec],
        out_shape=[SDS((S, DC), F32)] + [SDS((1, DC), F32)] * 3,
        compiler_params=_params(48),
    )(dconv, wpw, y, ln_g, ln_b)


def conv_bwd_dw(dy, proj, wk, dproj, name):
    S, DC = dy.shape
    tm = _tile(S, 128, ROW_CHUNK)
    nsteps = S // tm
    lead = CONV_PAD - (CONV_WIDTH - 1)
    groups = ROW_CHUNK // 8

    def body(dy_ref, dyn_ref, a_ref, b_ref, ah_ref, bh_ref, wk_ref, dproj_ref, dab_ref, gw_ref, uext_ref, dyext_ref, du_ref,
             tmp_dy_ref, tmp_u_ref):
        i = pl.program_id(0)

        @pl.when(i == 0)
        def _():
            gw_ref[...] = jnp.zeros_like(gw_ref)

        _fill_u(i, a_ref, b_ref, ah_ref, bh_ref, uext_ref, tm)
        dyext_ref[pl.ds(0, tm), :] = dy_ref[...]
        dyext_ref[pl.ds(tm, CONV_PAD), :] = jnp.where(i < nsteps - 1, dyn_ref[...], 0.0)

        def cols(cc, carry):
            c0 = pl.multiple_of(cc * LANE, LANE)
            lanes = pl.ds(c0, LANE)
            for rr in range(tm // ROW_CHUNK):
                base = rr * ROW_CHUNK
                acc = jnp.zeros((ROW_CHUNK, LANE), F32)
                for phase, offsets in _by_sublane_phase(range(CONV_WIDTH)):
                    rows = _shifted_rows(dyext_ref, tmp_dy_ref, base, phase, offsets, lanes)
                    for o in offsets:
                        j = CONV_WIDTH - 1 - o
                        acc = acc + wk_ref[j:j + 1, lanes] * rows(o)
                du_ref[pl.ds(base, ROW_CHUNK), lanes] = acc
                dyc = dyext_ref[pl.ds(base, ROW_CHUNK), lanes]
                for phase, offsets in _by_sublane_phase(range(lead, lead + CONV_WIDTH)):
                    rows = _shifted_rows(uext_ref, tmp_u_ref, base, phase, offsets, lanes)
                    for o in offsets:
                        prod = dyc * rows(o)
                        part = prod[0:8]
                        for k in range(1, groups):
                            part = part + prod[8 * k:8 * k + 8]
                        gw_ref[o - lead, :, lanes] += part
            return carry

        lax.fori_loop(0, DC // LANE, cols, 0)
        du = du_ref[...]
        sb = _sig(b_ref[...])
        dab_ref[:, :DC] = (du * sb).astype(BF16)
        dab_ref[:, DC:] = (du * a_ref[...] * sb * (1.0 - sb)).astype(BF16)

    return _call(
        body, name=name, grid=(nsteps,),
        in_specs=[_row(tm, DC), _halo_next(tm, DC, S // CONV_PAD), _row(tm, DC, 4), _row(tm, DC, 5),
                  _halo_prev(tm, DC, 4), _halo_prev(tm, DC, 5), _full((CONV_PAD, DC)), pl.BlockSpec(memory_space=pl.ANY)],
        out_specs=[_row(tm, 2 * DC, 2), _full((CONV_PAD, 8, DC))],
        out_shape=[SDS((S, 7 * DC), BF16), SDS((CONV_PAD, 8, DC), F32)],
        input_output_aliases={7: 0},
        scratch_shapes=[pltpu.VMEM((CONV_PAD + tm, DC), F32), pltpu.VMEM((tm + CONV_PAD, DC), F32), pltpu.VMEM((tm, DC), F32),
                        _shift_scratch(), _shift_scratch()],
        compiler_params=_params(40),
    )(dy, dy, proj, proj, proj, proj, wk, dproj)


def loss_head(xo, target, name):
    S, D = xo.shape
    tm = _tile(S, 256, 8)

    def body(x_ref, t_ref, dy_ref, dyb_ref, acc_ref):
        @pl.when(pl.program_id(0) == 0)
        def _():
            acc_ref[...] = jnp.zeros_like(acc_ref)

        err = x_ref[...] - t_ref[...]
        dy = err * (1.0 / D)
        dy_ref[...] = dy
        dyb_ref[...] = dy.astype(BF16)
        acc_ref[...] += jnp.sum(err * dy, axis=0, keepdims=True) * 0.5

    return _call(
        body, name=name, grid=(S // tm,), in_specs=[_row(tm, D), _row(tm, D)],
        out_specs=[_row(tm, D), _row(tm, D), _full((1, D))],
        out_shape=[SDS((S, D), F32), SDS((S, D), BF16), SDS((1, D), F32)],
        compiler_params=_params(32),
    )(xo, target)


def _coords():
    x, y, c = lax.axis_index("x"), lax.axis_index("y"), lax.axis_index("c")
    return x, y, c


def _lin(p):
    return 4 * p[0] + 2 * p[1] + p[2]


def _chip(p):
    return 2 * p[0] + p[1]


def _slot(ref, axis, idx, size):
    index = [slice(None)] * len(ref.shape)
    index[axis] = pl.ds(idx * size, size)
    return ref.at[tuple(index)]


def all_gather(blocks, axes, name):
    na = len(blocks)
    sizes = [b.shape[ax] for b, ax in zip(blocks, axes)]
    fulls = [SDS(b.shape[:ax] + (N_DEV * b.shape[ax],) + b.shape[ax + 1:], b.dtype) for b, ax in zip(blocks, axes)]

    def body(*refs):
        in_refs, out_refs = refs[:na], refs[na:2 * na]
        send_sems, recv_sems, local_sems = refs[2 * na:]
        x, y, c = _coords()
        me, sibling = (x, y, c), (x, y, 1 - c)
        chips = [(1 - x, y), (x, 1 - y), (1 - x, 1 - y)]
        south = c == 0
        relayed = (jnp.where(south, 1 - x, x), jnp.where(south, y, 1 - y), c)
        onward = (jnp.where(south, x, 1 - x), jnp.where(south, 1 - y, y), c)

        def place(a, p):
            return _slot(out_refs[a], axes[a], _lin(p), sizes[a])

        def copy(a, k, block, to, src=None):
            return pltpu.make_async_remote_copy(
                src_ref=place(a, block) if src is None else src, dst_ref=place(a, block),
                send_sem=send_sems.at[a, k], recv_sem=recv_sems.at[a, k], device_id=to, device_id_type=MESH)

        mine = [pltpu.make_async_copy(in_refs[a], place(a, me), local_sems.at[a]) for a in range(na)]
        for cp in mine:
            cp.start()
        first = []
        for a in range(na):
            first.append(copy(a, 0, me, sibling, src=in_refs[a]))
            first += [copy(a, 1 + j, me, (*chip, c), src=in_refs[a]) for j, chip in enumerate(chips[:2])]
        for cp in first:
            cp.start()
        later = []
        for a in range(na):
            for j, chip in enumerate(chips[:2]):
                copy(a, 1 + j, (*chip, c), me).wait_recv()
            later.append(copy(a, 3, relayed, onward))
            later += [copy(a, 4 + j, (*chip, c), sibling) for j, chip in enumerate(chips[:2])]
            for cp in later[-3:]:
                cp.start()
        for a in range(na):
            copy(a, 3, (*chips[2], c), me).wait_recv()
            later.append(copy(a, 6, (*chips[2], c), sibling))
            later[-1].start()
        for a in range(na):
            copy(a, 0, sibling, me).wait_recv()
            for j, chip in enumerate(chips):
                copy(a, 4 + j, (*chip, 1 - c), me).wait_recv()
        for cp in first + later:
            cp.wait_send()
        for cp in mine:
            cp.wait()

    hbm = pl.BlockSpec(memory_space=pltpu.HBM)
    return _call(
        body, name=name, in_specs=[hbm] * na, out_specs=[hbm] * na, out_shape=fulls,
        scratch_shapes=[pltpu.SemaphoreType.DMA((na, 7)), pltpu.SemaphoreType.DMA((na, 7)), pltpu.SemaphoreType.DMA((na,))],
    )(*blocks)


class _Exchange:
    def __init__(self, gather, srcs, axes, name, after, route="all", lands=None):
        self.gather, self.axes, self.name, self.route = gather, axes, name, route
        if route == "pass":
            self.na, self.ns = len(lands), 0
            self.sizes = [l.shape[ax] // N_DEV for l, ax in zip(lands, axes)]
            self.kinds = [pltpu.HBM(l.shape, l.dtype) for l in lands]
            self._start([pltpu.with_memory_space_constraint(t, pltpu.HBM) for t in lands], after)
            return
        self.na = self.ns = len(srcs)
        if gather:
            self.sizes = [s.shape[ax] for s, ax in zip(srcs, axes)]
            lands = [s.shape[:ax] + (N_DEV * s.shape[ax],) + s.shape[ax + 1:] for s, ax in zip(srcs, axes)]
        else:
            self.sizes = [None if ax is None else s.shape[ax] // N_DEV for s, ax in zip(srcs, axes)]
            lands = [s.shape if ax is None else (N_DEV,) + s.shape[:ax] + (sz,) + s.shape[ax + 1:]
                     for s, ax, sz in zip(srcs, axes, self.sizes)]
        self.kinds = [pltpu.HBM(s.shape, s.dtype) for s in srcs] + [pltpu.HBM(l, s.dtype) for l, s in zip(lands, srcs)]
        lands = [lax.empty(l, s.dtype) for l, s in zip(lands, srcs)]
        after = jnp.zeros((8, LANE), F32) if after is None else after
        self._start([pltpu.with_memory_space_constraint(t, pltpu.HBM) for t in list(srcs) + lands], after)

    def _src(self, a, ref, owner):
        if self.gather:
            return ref
        return ref.at[_lin(owner)] if self.axes[a] is None else _slot(ref, self.axes[a], _lin(owner), self.sizes[a])

    def _dst(self, a, land, sender):
        return _slot(land, self.axes[a], _lin(sender), self.sizes[a]) if self.gather else land.at[_lin(sender)]

    def _flips(self):
        if self.route == "all":
            return [(k >> 2 & 1, k >> 1 & 1, k & 1) for k in range(1, N_DEV)]
        if self.route == "chips":
            return [(0, 0, 1), (1, 0, 0), (0, 1, 0), (1, 1, 0)]
        if self.route == "sibling":
            return [(0, 0, 1)] * 4
        return [(1, 0, 0), (0, 1, 0), (1, 1, 0)]

    def _copies(self, refs, send_sems, recv_sems):
        na, ns = self.na, self.ns
        me = _coords()
        flips = self._flips()
        n = len(flips)
        others = [tuple(1 - v if f else v for v, f in zip(me, flip)) for flip in flips]
        sends, arrivals = [], []
        for a in range(na):
            land = refs[ns + a]
            for k, other in enumerate(others):
                if self.route == "pass":
                    peer = (me[0], me[1], 1 - me[2])
                    theirs = (other[0], other[1], 1 - me[2])
                    send = dict(src_ref=self._dst(a, land, other), dst_ref=self._dst(a, land, other))
                    arrive = dict(src_ref=self._dst(a, land, theirs), dst_ref=self._dst(a, land, theirs))
                elif self.route == "sibling":
                    peer = other
                    send = dict(src_ref=refs[a].at[k], dst_ref=land.at[k])
                    arrive = send
                elif self.route == "chip_parts":
                    peer = other
                    send = dict(src_ref=refs[a].at[_chip(peer)], dst_ref=land.at[_chip(me)])
                    arrive = dict(src_ref=refs[a].at[_chip(me)], dst_ref=land.at[_chip(peer)])
                else:
                    peer = other
                    send = dict(src_ref=self._src(a, refs[a], peer), dst_ref=self._dst(a, land, me))
                    arrive = dict(src_ref=self._src(a, refs[a], me), dst_ref=self._dst(a, land, peer))
                pair = dict(send_sem=send_sems.at[n * a + k], recv_sem=recv_sems.at[n * a + k], device_id=peer, device_id_type=MESH)
                sends.append(pltpu.make_async_remote_copy(**send, **pair))
                arrivals.append(pltpu.make_async_remote_copy(**arrive, **pair))
        return sends, arrivals

    def _place_own(self, operands):
        na = self.na
        mine = _chip(_coords()) if self.route == "chip_parts" else _lin(_coords())
        me = jnp.reshape(mine, (1,)).astype(jnp.int32)
        lands = []
        for a in range(na):
            src, land, ax = operands[a], operands[na + a], self.axes[a]
            if ax == 1:
                R, C = src.shape[0], self.sizes[a]
                steps, tile = 1, (R, C)
                in_map = lambda i, me_ref: (0, me_ref[0])
            elif ax is None:
                R, C = src.shape[1:]
                tr = _tile(R, 512, 16)
                steps, tile = R // tr, (None, tr, C)
                in_map = lambda i, me_ref: (me_ref[0], i, 0)
            else:
                R, C = (src.shape[0] if self.gather else self.sizes[a]), src.shape[1]
                tr = _tile(R, 512, 16)
                steps, tile = R // tr, (tr, C)
                in_map = (lambda i, me_ref: (i, 0)) if self.gather else (lambda i, me_ref, n=R // tr: (me_ref[0] * n + i, 0))
            if self.gather:
                out_spec = pl.BlockSpec(tile, lambda i, me_ref, n=steps: (me_ref[0] * n + i, 0))
            else:
                out_spec = pl.BlockSpec((None,) + tuple(t for t in tile if t is not None), lambda i, me_ref: (me_ref[0], i, 0))

            def body(me_ref, src_ref, land_ref, out_ref):
                out_ref[...] = src_ref[...]

            lands.append(_call(
                body, name=f"{self.name}_own{a}",
                grid_spec=pltpu.PrefetchScalarGridSpec(
                    num_scalar_prefetch=1, grid=(steps,),
                    in_specs=[pl.BlockSpec(tile, in_map), pl.BlockSpec(memory_space=pl.ANY)], out_specs=out_spec),
                out_shape=SDS(land.shape, land.dtype), input_output_aliases={2: 0}, compiler_params=_params(32),
            )(me, src, land))
        return operands[:na] + lands

    def _start(self, operands, after):
        nops = self.ns + self.na
        nsem = len(self._flips()) * self.na
        if self.ns and self.route != "sibling":
            operands = self._place_own(operands)

        def body(*refs):
            ins = refs[:nops]
            send_sems, recv_sems, token_ref = refs[nops + 1], refs[nops + 2], refs[2 * nops + 3]
            for cp in self._copies(ins, send_sems, recv_sems)[0]:
                cp.start()
            token_ref[...] = jnp.zeros_like(token_ref)

        hbm = pl.BlockSpec(memory_space=pltpu.HBM)
        sem = pl.BlockSpec(memory_space=pltpu.SEMAPHORE)
        outs = _call(
            body, name=self.name + "_start",
            in_specs=[hbm] * nops + [pl.BlockSpec(memory_space=pl.ANY)],
            out_specs=[sem, sem] + [hbm] * nops + [pl.BlockSpec(memory_space=pltpu.VMEM)],
            out_shape=[pltpu.SemaphoreType.DMA((nsem,)), pltpu.SemaphoreType.DMA((nsem,))] + self.kinds + [SDS((8, LANE), F32)],
            input_output_aliases={i: 2 + i for i in range(nops)},
            compiler_params=pltpu.CompilerParams(has_side_effects=pltpu.SideEffectType.DATAFLOW_SIDE_EFFECTING),
        )(*operands, after)
        self.sems, self.thru, self.token = outs[:2], outs[2:2 + nops], outs[2 + nops][0:1, 0:1]

    def wait(self, after):
        nops = self.ns + self.na

        def body(*refs):
            ins, send_sems, recv_sems = refs[:nops], refs[nops], refs[nops + 1]
            sends, arrivals = self._copies(ins, send_sems, recv_sems)
            for cp in sends:
                cp.wait_send()
            for cp in arrivals:
                cp.wait_recv()

        hbm = pl.BlockSpec(memory_space=pltpu.HBM)
        sem = pl.BlockSpec(memory_space=pltpu.SEMAPHORE)
        outs = _call(
            body, name=self.name + "_wait",
            in_specs=[hbm] * nops + [sem, sem, pl.BlockSpec(memory_space=pl.ANY)],
            out_specs=[hbm] * nops, out_shape=self.kinds,
            input_output_aliases={i: i for i in range(nops)},
            compiler_params=pltpu.CompilerParams(has_side_effects=pltpu.SideEffectType.DATAFLOW_SIDE_EFFECTING),
        )(*self.thru, *self.sems, after)
        return outs[self.ns:]


def adamw(w, m, v, parts, layer, prev, name):
    nl, R, C = w.shape
    nparts = parts.shape[0]
    tr = _tile(R, 128, 8) if R % 8 == 0 else R

    def body(w_ref, m_ref, v_ref, p_ref, *rest):
        g_ref, d_ref, mo_ref, vo_ref = rest[-4:]
        g = p_ref[0].astype(F32)
        for s in range(1, nparts):
            g = g + p_ref[s].astype(F32)
        mn = ADAM_B1 * m_ref[0] + (1.0 - ADAM_B1) * g
        vn = ADAM_B2 * v_ref[0] + (1.0 - ADAM_B2) * (g * g)
        m_hat = mn / (1.0 - ADAM_B1 ** ADAM_STEP)
        v_hat = vn / (1.0 - ADAM_B2 ** ADAM_STEP)
        g_ref[0] = g
        d_ref[0] = -ADAM_LR * (m_hat / (jnp.sqrt(v_hat) + ADAM_EPS) + ADAM_WD * w_ref[0])
        mo_ref[0] = mn
        vo_ref[0] = vn

    row = pl.BlockSpec((1, tr, C), lambda i: (layer, i, 0))
    carried = [] if prev is None else list(prev)
    return _call(
        body, name=name, grid=(R // tr,),
        in_specs=[row, row, row, pl.BlockSpec((nparts, tr, C), lambda i: (0, i, 0))] + [pl.BlockSpec(memory_space=pl.ANY)] * len(carried),
        out_specs=[row] * 4, out_shape=[SDS((nl, R, C), F32)] * 4,
        input_output_aliases={4 + k: k for k in range(len(carried))},
        compiler_params=_params(48),
    )(w, m, v, parts, *carried)


def _rope_tables(S):
    inv_freq = 1.0 / (ROPE_THETA ** (jnp.arange(0, HEAD_DIM, 2, dtype=F32) / HEAD_DIM))
    ang = jnp.arange(S, dtype=F32)[:, None] * inv_freq[None, :]
    cos, sin = jnp.cos(ang), jnp.sin(ang)
    return jnp.concatenate([cos, cos], axis=-1), jnp.concatenate([-sin, sin], axis=-1)


def _pack_small(D, norm_g, dw_bias, conv_ln_g, conv_ln_b, att_out_g, conv_out_g, q_norm_g, k_norm_g, extra=None):
    qk = jnp.concatenate([q_norm_g.reshape(-1), k_norm_g.reshape(-1)])
    qk = jnp.pad(qk, (0, D - qk.shape[0])).reshape(1, D)
    zero = jnp.zeros((1, D), F32)
    return jnp.concatenate([norm_g, dw_bias, conv_ln_g, conv_ln_b, att_out_g, conv_out_g, qk, zero,
                            zero if extra is None else extra, zero], axis=0)


def _unpack_small(p):
    rows = [p[2 * i:2 * i + 2] for i in range(6)]
    qk = p[12, :4 * HEAD_DIM].reshape(2, DEPTH, HEAD_DIM)
    return rows + [qk[0], qk[1]]


def kernel(x, norm_g, w_in, q_norm_g, k_norm_g, dw_kernel, dw_bias, conv_ln_g, conv_ln_b, w_pw, att_out_g, conv_out_g, w_out, loss_target, m_norm_g, m_w_in, m_q_norm_g, m_k_norm_g, m_dw_kernel, m_dw_bias, m_conv_ln_g, m_conv_ln_b, m_w_pw, m_att_out_g, m_conv_out_g, m_w_out, v_norm_g, v_w_in, v_q_norm_g, v_k_norm_g, v_dw_kernel, v_dw_bias, v_conv_ln_g, v_conv_ln_b, v_w_pw, v_att_out_g, v_conv_out_g, v_w_out):
    xs = x[0]
    D = xs.shape[1]
    bf = lambda t, l: t[l].astype(BF16)
    wint0, dwk_f = all_gather([bf(w_in, 0).T, dw_kernel], [0, 2], "gather_first")
    early = _Exchange(True, [bf(w_pw, 0), bf(w_out, 0)], [0, 0], "gather_layer0", after=wint0)
    later = _Exchange(True, [bf(w_in, 1).T, bf(w_pw, 1), bf(w_out, 1)], [0, 0, 0], "gather_layer1", after=early.token,
                      route="chips")
    landed = {}

    def weights(l, cur):
        if l == 0:
            return wint0, later.token
        landed[1] = landed["passing"].wait(cur)
        return landed[1][0], None

    def mixer_weights(l, cur):
        if l == 0:
            mine = early.wait(cur)
            landed["passing"] = _Exchange(True, None, [0, 0, 0], "pass_layer1", after=mine[0], route="pass",
                                          lands=later.wait(mine[0]))
            return (*mine, landed["passing"].token)
        return (*landed[1][1:], None)

    sent = [[] for _ in range(DEPTH)]

    def send_grads(l, grads, axes, tag):
        sent[l].append(_Exchange(False, grads, axes, f"scatter_{tag}_layer{l}", after=None))
        return sent[l][-1].token

    def win_grads(l, ht, dproj):
        if l > 0:
            return send_grads(l, [mm_nn(ht, dproj, BF16, f"dwin_{l}", owners=N_DEV)], [None], "w_in")
        core = jnp.reshape(lax.axis_index("c"), (1,)).astype(jnp.int32)
        theirs = mm_core_blocks(ht, dproj, 1 - core, f"dwin_{l}_for_sibling")
        to_sibling = _Exchange(False, [theirs], [None], f"scatter_w_in_sibling_layer{l}", after=None, route="sibling")
        mine = mm_core_blocks(ht, dproj, core + to_sibling.token[0].astype(jnp.int32), f"dwin_{l}_own")
        (from_sibling,) = to_sibling.wait(mine)
        both = add_blocks(mine, from_sibling, f"dwin_{l}_chip_sum")
        sent[l].append(_Exchange(False, [both], [None], f"scatter_w_in_chips_layer{l}", after=None, route="chip_parts"))
        return sent[l][-1].token

    dx, loss_cols, small = local_step(xs, loss_target[0], weights, mixer_weights, dwk_f, norm_g, q_norm_g, k_norm_g, dw_bias,
                                      conv_ln_g, conv_ln_b, att_out_g, conv_out_g, send_grads, win_grads)

    big = ((w_pw, m_w_pw, v_w_pw), (w_out, m_w_out, v_w_out), (dw_kernel, m_dw_kernel, v_dw_kernel), (w_in, m_w_in, v_w_in))
    results = [None] * len(big)
    after = dx
    for l in reversed(range(DEPTH)):
        parts = [p for ex in sent[l] for p in ex.wait(after)]
        for i, ((w, m, v), p) in enumerate(zip(big, parts)):
            results[i] = adamw(w, m, v, p, l, results[i], f"adamw_{w.shape[1]}_{w.shape[2]}_{l}")
        after = results[0][3]
    r_wpw, r_wout, r_dwk, r_win = results

    stack = lambda k: jnp.concatenate(small[k], axis=0)
    mine = _pack_small(D, stack("norm_g"), stack("dw_bias"), stack("conv_ln_g"), stack("conv_ln_b"), stack("att_out_g"),
                       stack("conv_out_g"), stack("q"), stack("k"), extra=loss_cols)
    (p_small,) = all_gather([mine], [0], "gather_small")
    pk = lambda n, dw, lg, lb, ao, co, q, k: _pack_small(D, n, dw, lg, lb, ao, co, q, k)[None]
    r_small = adamw(pk(norm_g, dw_bias, conv_ln_g, conv_ln_b, att_out_g, conv_out_g, q_norm_g, k_norm_g),
                    pk(m_norm_g, m_dw_bias, m_conv_ln_g, m_conv_ln_b, m_att_out_g, m_conv_out_g, m_q_norm_g, m_k_norm_g),
                    pk(v_norm_g, v_dw_bias, v_conv_ln_g, v_conv_ln_b, v_att_out_g, v_conv_out_g, v_q_norm_g, v_k_norm_g),
                    p_small.reshape(N_DEV, 16, D), 0, None, "adamw_small")
    r_small = [r[0] for r in r_small]
    loss = jnp.sum(r_small[0][14])

    outs = [loss, dx[None]]
    for i in range(4):
        n_, dwb, lg, lb, ao, co, q_, k_ = _unpack_small(r_small[i])
        outs += [n_, r_win[i], q_, k_, r_dwk[i], dwb, lg, lb, r_wpw[i], ao, co, r_wout[i]]
    return tuple(outs)


def local_step(xs, target, weights, mixer_weights, dwk_f, norm_g, q_norm_g, k_norm_g, dw_bias, conv_ln_g, conv_ln_b,
               att_out_g, conv_out_g, send_grads, win_grads):
    S, D = xs.shape
    cos2, sin2 = _rope_tables(S)
    dwk_f = jnp.pad(dwk_f, ((0, 0), (0, CONV_PAD - CONV_WIDTH), (0, 0)))

    def vec(p, l, zero=None):
        row = p[l].reshape(1, -1)
        return row if zero is None else row + zero

    saved = []
    cur = xs
    for l in range(DEPTH):
        wint, zero = weights(l, cur)
        proj, _, h_t = in_proj(cur, vec(norm_g, l, zero), wint, f"in_proj_{l}")
        qs, ks, vs = qk_prep(proj, vec(q_norm_g, l), vec(k_norm_g, l), cos2, sin2, f"qk_prep_{l}")
        os_, lses = [], []
        for i, d in enumerate(DILATIONS):
            o, lse = attn_fwd(qs[i], ks[i], vs[i], d, f"attn_fwd_{l}_d{d}")
            os_.append(o)
            lses.append(_from_branch(lse))
        att, cat, lse, cat_t = att_combine(os_, lses, proj, vec(att_out_g, l), f"att_combine_{l}")
        wpw, wout, zero = mixer_weights(l, lse)
        cat, cat_t, z_t, conv, y = conv_fwd(proj, dwk_f[l], vec(dw_bias, l, zero), vec(conv_ln_g, l), vec(conv_ln_b, l),
                                            vec(conv_out_g, l), wpw, cat, cat_t, f"conv_fwd_{l}")
        nxt = mm_nn(cat, wout, F32, f"out_proj_{l}", add=cur)
        saved.append(dict(x=cur, proj=proj, h_t=h_t, qs=qs, ks=ks, vs=vs, att=att, lse=lse, cat_t=cat_t, z_t=z_t, conv=conv,
                          y=y, wint=wint, wpw=wpw, wout=wout))
        cur = nxt

    dx, dxb, loss_cols = loss_head(cur, target, "loss_head")

    small = {k: [None] * DEPTH for k in ("norm_g", "dw_bias", "conv_ln_g", "conv_ln_b", "att_out_g", "conv_out_g", "q", "k")}
    for l in reversed(range(DEPTH)):
        sv = saved[l]
        proj = sv["proj"]
        dcat = mm_nt(dxb, sv["wout"], F32, f"dcat_{l}")
        g_wout = mm_nn(sv["cat_t"], dxb, BF16, f"dwout_{l}")
        dproj, small["conv_out_g"][l], dconv = gate_bwd(dcat, 1, sv["conv"], proj, 6, vec(conv_out_g, l), False, None,
                                                        f"conv_gate_bwd_{l}")
        dy, small["conv_ln_g"][l], small["conv_ln_b"][l], small["dw_bias"][l] = conv_bwd_ln(
            dconv, sv["wpw"], sv["y"], vec(conv_ln_g, l), vec(conv_ln_b, l), f"conv_bwd_ln_{l}")
        g_wpw = mm_nn(sv["z_t"], dconv, BF16, f"dwpw_{l}")
        dproj, gw = conv_bwd_dw(dy, proj, dwk_f[l], dproj, f"conv_bwd_dw_{l}")
        g_dwk = jnp.sum(gw, axis=1)[:CONV_WIDTH]
        zero = send_grads(l, [g_wpw, g_wout, g_dwk], [0, 0, 1], "mixer")
        dproj, small["att_out_g"][l], *datts, delta = gate_bwd(dcat, 0, sv["att"], proj, 3, vec(att_out_g, l, zero), True,
                                                                dproj, f"att_gate_bwd_{l}")
        dqs, dks, dvs = [], [], []
        for i, d in enumerate(DILATIONS):
            dq, dk, dv = attn_bwd(sv["qs"][i], sv["ks"][i], sv["vs"][i], datts[i], _to_branch(sv["lse"], d),
                                  _to_branch(delta, d), d, f"attn_bwd_{l}_d{d}")
            dqs.append(dq)
            dks.append(dk)
            dvs.append(dv)
        dproj, small["q"][l], small["k"][l] = qk_prep_bwd(dqs, dks, dvs, proj, vec(q_norm_g, l), vec(k_norm_g, l),
                                                          cos2, sin2, dproj, f"qk_prep_bwd_{l}")
        zero = win_grads(l, sv["h_t"], dproj)
        dx, dxb, small["norm_g"][l] = in_proj_bwd(dproj, sv["wint"], sv["x"], vec(norm_g, l, zero), dx, f"in_proj_bwd_{l}")

    return dx, loss_cols, small
```

```python
import jax
import jax.numpy as jnp
from jax import lax
from jax.experimental import pallas as pl
from jax.experimental.pallas import tpu as pltpu

F32 = jnp.float32
BF16 = jnp.bfloat16
SDS = jax.ShapeDtypeStruct
MESH = pl.DeviceIdType.MESH

N_DEV = 8
DEPTH = 2
HEAD_DIM = 128
CONV_WIDTH = 31
CONV_PAD = 32
DILATIONS = (1, 4, 16)
Q_BLOCK = 128
ROPE_THETA = 10000.0
EPS = 1e-6
NEG = -1e30
ADAM_LR, ADAM_B1, ADAM_B2, ADAM_EPS, ADAM_WD, ADAM_STEP = 0.001, 0.9, 0.999, 1e-08, 0.01, 10
LANE = 128
ROW_CHUNK = 64
MIB = 1 << 20
NT = (((1,), (1,)), ((), ()))
TN = (((0,), (0,)), ((), ()))


def _call(body, **kw):
    return pl.pallas_call(body, **kw)


def _params(vmem_mib):
    return pltpu.CompilerParams(vmem_limit_bytes=vmem_mib * MIB)


def _tile(dim, pref, mult):
    t = min(pref, dim)
    while dim % t or t % mult:
        t -= mult
    return t


def _sig(v):
    return jax.nn.sigmoid(v)


def _rstd(v):
    return lax.rsqrt(jnp.mean(v * v, axis=-1, keepdims=True) + EPS)


def _row(tm, cb, c=0):
    return pl.BlockSpec((tm, cb), lambda i: (i, c))


def _full(shape):
    return pl.BlockSpec(shape, lambda i: (0,) * len(shape))


def _halo_prev(tm, cb, c=0):
    k = tm // CONV_PAD
    return pl.BlockSpec((CONV_PAD, cb), lambda i: (jnp.maximum(i * k - 1, 0), c))


def _halo_next(tm, cb, nblk, c=0):
    k = tm // CONV_PAD
    return pl.BlockSpec((CONV_PAD, cb), lambda i: (jnp.minimum((i + 1) * k, nblk - 1), c))


def in_proj(x, g, wt, name):
    S, D = x.shape
    N = wt.shape[0]
    tm, tn = _tile(S, 512, 8), _tile(N, 1792, LANE)

    def body(x_ref, g_ref, w_ref, o_ref, h_ref, ht_ref):
        @pl.when(pl.program_id(1) == 0)
        def _():
            xf = x_ref[...]
            hf = xf * _rstd(xf) * g_ref[...]
            h_ref[...] = hf.astype(BF16)
            ht_ref[...] = hf.T.astype(BF16)

        o_ref[...] = lax.dot_general(h_ref[...], w_ref[...], NT, preferred_element_type=F32)

    return _call(
        body, name=name, grid=(S // tm, N // tn),
        in_specs=[pl.BlockSpec((tm, D), lambda i, j: (i, 0)), pl.BlockSpec((1, D), lambda i, j: (0, 0)),
                  pl.BlockSpec((tn, D), lambda i, j: (j, 0))],
        out_specs=[pl.BlockSpec((tm, tn), lambda i, j: (i, j)), pl.BlockSpec((tm, D), lambda i, j: (i, 0)),
                   pl.BlockSpec((D, tm), lambda i, j: (0, i))],
        out_shape=[SDS((S, N), F32), SDS((S, D), BF16), SDS((D, S), BF16)],
        compiler_params=_params(48),
    )(x, g, wt)


def mm_nn(a, b, out_dtype, name, add=None, owners=1):
    M, K = a.shape
    N = b.shape[1]
    tm, tn = _tile(M, 1024 if owners > 1 else 512, 8), _tile(N // owners, 1024, LANE)
    per = N // owners // tn

    def body(*refs):
        a_ref, b_ref = refs[0], refs[1]
        o_ref = refs[-1]
        acc = jnp.dot(a_ref[...], b_ref[...], preferred_element_type=F32)
        if add is not None:
            acc = acc + refs[2][...]
        o_ref[...] = acc.astype(out_dtype)

    in_specs = [pl.BlockSpec((tm, K), lambda i, j: (i, 0)), pl.BlockSpec((K, tn), lambda i, j: (0, j))]
    args = [a, b]
    if add is not None:
        in_specs.append(pl.BlockSpec((tm, tn), lambda i, j: (i, j)))
        args.append(add)
    if owners > 1:
        out_spec = pl.BlockSpec((None, tm, tn), lambda i, j: (j // per, i, j % per))
        out_shape = SDS((owners, M, N // owners), out_dtype)
    else:
        out_spec, out_shape = pl.BlockSpec((tm, tn), lambda i, j: (i, j)), SDS((M, N), out_dtype)
    return _call(
        body, name=name, grid=(M // tm, N // tn), in_specs=in_specs, out_specs=out_spec, out_shape=out_shape,
        compiler_params=_params(48),
    )(*args)


def mm_core_blocks(a, b, core, name):
    M, K = a.shape
    blk = b.shape[1] // N_DEV
    tm, tn = _tile(M, 1024, 8), _tile(blk, 1024, LANE)
    per = blk // tn

    def body(core_ref, a_ref, b_ref, o_ref):
        o_ref[...] = jnp.dot(a_ref[...], b_ref[...], preferred_element_type=F32).astype(BF16)

    return _call(
        body, name=name,
        grid_spec=pltpu.PrefetchScalarGridSpec(
            num_scalar_prefetch=1, grid=(M // tm, 4 * per),
            in_specs=[pl.BlockSpec((tm, K), lambda i, j, core_ref: (i, 0)),
                      pl.BlockSpec((K, tn), lambda i, j, core_ref: (0, (2 * (j // per) + core_ref[0]) * per + j % per))],
            out_specs=pl.BlockSpec((None, tm, tn), lambda i, j, core_ref: (j // per, i, j % per))),
        out_shape=SDS((4, M, blk), BF16), compiler_params=_params(48),
    )(core, a, b)


def add_blocks(a, b, name):
    n, R, C = a.shape
    tr = _tile(R, 512, 16)
    spec = pl.BlockSpec((None, tr, C), lambda k, i: (k, i, 0))

    def body(a_ref, b_ref, o_ref):
        o_ref[...] = (a_ref[...].astype(F32) + b_ref[...].astype(F32)).astype(BF16)

    return _call(body, name=name, grid=(n, R // tr), in_specs=[spec, spec], out_specs=spec, out_shape=SDS(a.shape, BF16),
                 compiler_params=_params(32))(a, b)


def mm_nt(a, b, out_dtype, name):
    M, K = a.shape
    N = b.shape[0]
    tm, tn = _tile(M, 512, 8), _tile(N, 2048, LANE)

    def body(a_ref, b_ref, o_ref):
        o_ref[...] = lax.dot_general(a_ref[...], b_ref[...], NT, preferred_element_type=F32).astype(out_dtype)

    return _call(
        body, name=name, grid=(M // tm, N // tn),
        in_specs=[pl.BlockSpec((tm, K), lambda i, j: (i, 0)), pl.BlockSpec((tn, K), lambda i, j: (j, 0))],
        out_specs=pl.BlockSpec((tm, tn), lambda i, j: (i, j)), out_shape=SDS((M, N), out_dtype),
        compiler_params=_params(40),
    )(a, b)


def in_proj_bwd(dproj, wt, x, g, dx_out, name):
    S, K = dproj.shape
    D = wt.shape[1]
    tm, tk = _tile(S, 512, 8), _tile(K, 1024, LANE)
    nk = K // tk

    def body(dp_ref, w_ref, x_ref, g_ref, dxo_ref, dx_ref, dxb_ref, gacc_ref):
        i, k = pl.program_id(0), pl.program_id(1)
        part = jnp.dot(dp_ref[...], w_ref[...], preferred_element_type=F32)

        @pl.when(k == 0)
        def _():
            dx_ref[...] = part

        @pl.when(k > 0)
        def _():
            dx_ref[...] += part

        @pl.when((k == 0) & (i == 0))
        def _():
            gacc_ref[...] = jnp.zeros_like(gacc_ref)

        @pl.when(k == nk - 1)
        def _():
            dh = dx_ref[...]
            xf = x_ref[...]
            r = _rstd(xf)
            n = xf * r
            gacc_ref[...] += jnp.sum(dh * n, axis=0, keepdims=True)
            dn = dh * g_ref[...]
            dx = r * (dn - n * jnp.mean(dn * n, axis=-1, keepdims=True)) + dxo_ref[...]
            dx_ref[...] = dx
            dxb_ref[...] = dx.astype(BF16)

    return _call(
        body, name=name, grid=(S // tm, nk),
        in_specs=[pl.BlockSpec((tm, tk), lambda i, k: (i, k)), pl.BlockSpec((tk, D), lambda i, k: (k, 0)),
                  pl.BlockSpec((tm, D), lambda i, k: (i, 0)), pl.BlockSpec((1, D), lambda i, k: (0, 0)),
                  pl.BlockSpec((tm, D), lambda i, k: (i, 0))],
        out_specs=[pl.BlockSpec((tm, D), lambda i, k: (i, 0)), pl.BlockSpec((tm, D), lambda i, k: (i, 0)),
                   pl.BlockSpec((1, D), lambda i, k: (0, 0))],
        out_shape=[SDS((S, D), F32), SDS((S, D), BF16), SDS((1, D), F32)],
        compiler_params=_params(54),
    )(dproj, wt, x, g, dx_out)


def _dil_specs(S, DA, tm, dtype):
    specs = [pl.BlockSpec((tm // d, d * DA), lambda i: (i, 0)) for d in DILATIONS]
    shapes = [SDS((S // d, d * DA), dtype) for d in DILATIONS]
    return specs, shapes


def _head_buf(tm, DA):
    return pltpu.VMEM((DA // HEAD_DIM, tm, HEAD_DIM), F32)


def _emit_dilated(buf_ref, dsts, tm, DA):
    for d, dst in zip(DILATIONS, dsts):
        for h in range(DA // HEAD_DIM):
            for r in range(d):
                rows = slice(None) if d == 1 else pl.ds(r, tm // d, stride=d)
                dst[:, r * DA + h * HEAD_DIM:r * DA + (h + 1) * HEAD_DIM] = buf_ref.at[h][rows, :].astype(BF16)


def _collect_dilated(acc_ref, parts, tm, DA):
    for d, p in zip(DILATIONS, parts):
        for h in range(DA // HEAD_DIM):
            for r in range(d):
                part = p[:, r * DA + h * HEAD_DIM:r * DA + (h + 1) * HEAD_DIM].astype(F32)
                if d == 1:
                    acc_ref[h] = part
                else:
                    rows = pl.ds(r, tm // d, stride=d)
                    acc_ref.at[h][rows, :] = acc_ref.at[h][rows, :] + part


def qk_prep(proj, gq, gk, cos2, sin2, name):
    S = proj.shape[0]
    DA = proj.shape[1] // 7
    H = DA // HEAD_DIM
    tm = _tile(S, 256, 16 * DILATIONS[-1])
    nd = len(DILATIONS)

    def body(q_ref, k_ref, v_ref, gq_ref, gk_ref, c_ref, s_ref, *rest):
        outs, buf_ref = rest[:3 * nd], rest[3 * nd]
        ct, st = c_ref[...], s_ref[...]
        for t, (src, g_ref) in enumerate(((q_ref, gq_ref), (k_ref, gk_ref))):
            gain = g_ref[...]
            for h in range(H):
                sl = slice(h * HEAD_DIM, (h + 1) * HEAD_DIM)
                xh = src[:, sl]
                n = xh * _rstd(xh) * gain
                buf_ref[h] = n * ct + pltpu.roll(n, HEAD_DIM // 2, 1) * st
            _emit_dilated(buf_ref, outs[t * nd:(t + 1) * nd], tm, DA)
        for h in range(H):
            buf_ref[h] = v_ref[:, h * HEAD_DIM:(h + 1) * HEAD_DIM]
        _emit_dilated(buf_ref, outs[2 * nd:], tm, DA)

    specs, shapes = _dil_specs(S, DA, tm, BF16)
    outs = _call(
        body, name=name, grid=(S // tm,),
        in_specs=[_row(tm, DA, 0), _row(tm, DA, 1), _row(tm, DA, 2), _full((1, HEAD_DIM)), _full((1, HEAD_DIM)),
                  _row(tm, HEAD_DIM), _row(tm, HEAD_DIM)],
        out_specs=specs * 3, out_shape=shapes * 3, scratch_shapes=[_head_buf(tm, DA)],
        compiler_params=_params(48),
    )(proj, proj, proj, gq, gk, cos2, sin2)
    return outs[:nd], outs[nd:2 * nd], outs[2 * nd:]


def qk_prep_bwd(dqs, dks, dvs, proj, gq, gk, cos2, sin2, dproj, name):
    S = proj.shape[0]
    DA = proj.shape[1] // 7
    H = DA // HEAD_DIM
    tm = _tile(S, 256, 16 * DILATIONS[-1])
    nb = len(dqs)

    def body(*refs):
        dq_refs, dk_refs, dv_refs = refs[:nb], refs[nb:2 * nb], refs[2 * nb:3 * nb]
        q_ref, k_ref, gq_ref, gk_ref, c_ref, s_ref = refs[3 * nb:3 * nb + 6]
        out_ref, gqa_ref, gka_ref, acc_ref = refs[3 * nb + 7:]
        ct, st = c_ref[...], s_ref[...]

        @pl.when(pl.program_id(0) == 0)
        def _():
            gqa_ref[...] = jnp.zeros_like(gqa_ref)
            gka_ref[...] = jnp.zeros_like(gka_ref)

        for parts, x_ref, g_ref, col, gacc in ((dq_refs, q_ref, gq_ref, 0, gqa_ref),
                                               (dk_refs, k_ref, gk_ref, DA, gka_ref)):
            gain = g_ref[...]
            gsum = jnp.zeros((1, HEAD_DIM), F32)
            _collect_dilated(acc_ref, parts, tm, DA)
            for h in range(H):
                sl = slice(h * HEAD_DIM, (h + 1) * HEAD_DIM)
                dout = acc_ref[h]
                dn = dout * ct + pltpu.roll(dout * st, HEAD_DIM // 2, 1)
                xh = x_ref[:, sl]
                r = _rstd(xh)
                xn = xh * r
                gsum = gsum + jnp.sum(dn * xn, axis=0, keepdims=True)
                dnn = dn * gain
                dx = r * (dnn - xn * jnp.mean(dnn * xn, axis=-1, keepdims=True))
                out_ref[:, col + h * HEAD_DIM:col + (h + 1) * HEAD_DIM] = dx.astype(BF16)
            gacc[...] += gsum
        _collect_dilated(acc_ref, dv_refs, tm, DA)
        for h in range(H):
            out_ref[:, 2 * DA + h * HEAD_DIM:2 * DA + (h + 1) * HEAD_DIM] = acc_ref[h].astype(BF16)

    specs, _ = _dil_specs(S, DA, tm, BF16)
    return _call(
        body, name=name, grid=(S // tm,),
        in_specs=specs * 3 + [_row(tm, DA, 0), _row(tm, DA, 1), _full((1, HEAD_DIM)),
                              _full((1, HEAD_DIM)), _row(tm, HEAD_DIM), _row(tm, HEAD_DIM), pl.BlockSpec(memory_space=pl.ANY)],
        out_specs=[_row(tm, 3 * DA)] + [_full((1, HEAD_DIM))] * 2,
        out_shape=[SDS((S, 7 * DA), BF16)] + [SDS((1, HEAD_DIM), F32)] * 2,
        input_output_aliases={3 * nb + 6: 0},
        scratch_shapes=[_head_buf(tm, DA)],
        compiler_params=_params(48),
    )(*dqs, *dks, *dvs, proj, proj, gq, gk, cos2, sin2, dproj)


def _band_mask(n):
    row = lax.broadcasted_iota(jnp.int32, (Q_BLOCK, 2 * Q_BLOCK), 0)
    col = lax.broadcasted_iota(jnp.int32, (Q_BLOCK, 2 * Q_BLOCK), 1)
    first = jnp.where(n > 0, Q_BLOCK, 2 * Q_BLOCK + 1)
    return (col <= row) | ((col - row) >= first)


def attn_fwd(qh, kh, vb, d, name):
    L = qh.shape[0]
    DA = qh.shape[1] // d
    H = DA // HEAD_DIM
    nb = L // Q_BLOCK
    scale = HEAD_DIM ** -0.5
    view = (L, d * DA)

    def body(q_ref, kc_ref, kp_ref, vc_ref, vp_ref, o_ref, lse_ref):
        mask = _band_mask(pl.program_id(1))
        ones = jnp.ones((2 * Q_BLOCK, HEAD_DIM), BF16)
        for h in range(H):
            sl = slice(h * HEAD_DIM, (h + 1) * HEAD_DIM)
            keys = jnp.concatenate([kc_ref[:, sl], kp_ref[:, sl]], axis=0)
            s = lax.dot_general(q_ref[:, sl], keys, NT, preferred_element_type=F32) * scale
            s = jnp.where(mask, s, NEG)
            m = jnp.max(s, axis=-1, keepdims=True)
            p = jnp.exp(s - m).astype(BF16)
            vals = jnp.concatenate([jnp.concatenate([vc_ref[:, sl], vp_ref[:, sl]], axis=0), ones], axis=1)
            ol = jnp.dot(p, vals, preferred_element_type=F32)
            l = ol[:, HEAD_DIM:]
            o_ref[:, sl] = ol[:, :HEAD_DIM] / l
            lse_ref[0, :, h:h + 1] = m + jnp.log(l[:, 0:1])

    cur = pl.BlockSpec((Q_BLOCK, DA), lambda r, n: (n, r))
    prev = pl.BlockSpec((Q_BLOCK, DA), lambda r, n: (jnp.maximum(n - 1, 0), r))
    o, lse = _call(
        body, name=name, grid=(d, nb), in_specs=[cur, cur, prev, cur, prev],
        out_specs=[cur, pl.BlockSpec((1, Q_BLOCK, H), lambda r, n: (r, n, 0))],
        out_shape=[SDS(view, F32), SDS((d, L, H), F32)],
        compiler_params=_params(32),
    )(qh, kh, kh, vb, vb)
    return o, lse


def attn_bwd(qh, kh, vb, da, lse_d, delta_d, d, name):
    L = qh.shape[0]
    DA = qh.shape[1] // d
    H = DA // HEAD_DIM
    nb = L // Q_BLOCK
    scale = HEAD_DIM ** -0.5
    view = (L, d * DA)
    resident = d > 1

    def body(q_ref, kc_ref, kp_ref, vc_ref, vp_ref, do_ref, lse_ref, dl_ref, dq_ref, dk_ref, dv_ref, dkc_ref, dvc_ref):
        n = pl.program_id(1)
        mask = _band_mask(n)
        if resident:
            done = pl.ds(pl.multiple_of(jnp.maximum(n - 1, 0) * Q_BLOCK, Q_BLOCK), Q_BLOCK)
            fresh = pl.ds(pl.multiple_of(n * Q_BLOCK, Q_BLOCK), Q_BLOCK)
        else:
            done = slice(None)

        @pl.when(n == 0)
        def _():
            dkc_ref[...] = jnp.zeros_like(dkc_ref)
            dvc_ref[...] = jnp.zeros_like(dvc_ref)

        @pl.when(n < nb)
        def _():
            for h in range(H):
                sl = slice(h * HEAD_DIM, (h + 1) * HEAD_DIM)
                q, do = q_ref[:, sl], do_ref[:, sl]
                keys = jnp.concatenate([kc_ref[:, sl], kp_ref[:, sl]], axis=0)
                vals = jnp.concatenate([vc_ref[:, sl], vp_ref[:, sl]], axis=0)
                lse = jnp.broadcast_to(lse_ref[0, :, h:h + 1], (Q_BLOCK, 2 * Q_BLOCK))
                dl = jnp.broadcast_to(dl_ref[0, :, h:h + 1], (Q_BLOCK, 2 * Q_BLOCK))
                s = lax.dot_general(q, keys, NT, preferred_element_type=F32) * scale
                p = jnp.exp(jnp.where(mask, s, NEG) - lse)
                dp = lax.dot_general(do, vals, NT, preferred_element_type=F32)
                ds = (p * (dp - dl) * scale).astype(BF16)
                dq_ref[:, sl] = jnp.dot(ds, keys, preferred_element_type=F32).astype(BF16)
                dk = lax.dot_general(ds, q, TN, preferred_element_type=F32)
                dv = lax.dot_general(p.astype(BF16), do, TN, preferred_element_type=F32)
                dk_ref[done, sl] = (dkc_ref[:, sl] + dk[Q_BLOCK:]).astype(BF16)
                dv_ref[done, sl] = (dvc_ref[:, sl] + dv[Q_BLOCK:]).astype(BF16)
                dkc_ref[:, sl] = dk[:Q_BLOCK]
                dvc_ref[:, sl] = dv[:Q_BLOCK]
                if resident:
                    dk_ref[fresh, sl] = dk[:Q_BLOCK].astype(BF16)
                    dv_ref[fresh, sl] = dv[:Q_BLOCK].astype(BF16)

        if not resident:
            @pl.when(n == nb)
            def _():
                dk_ref[...] = dkc_ref[...].astype(BF16)
                dv_ref[...] = dvc_ref[...].astype(BF16)

    steps = nb if resident else nb + 1
    cur = pl.BlockSpec((Q_BLOCK, DA), lambda r, n: (jnp.minimum(n, nb - 1), r))
    prev = pl.BlockSpec((Q_BLOCK, DA), lambda r, n: (jnp.clip(n - 1, 0, nb - 1), r))
    if resident:
        keyside = pl.BlockSpec((L, DA), lambda r, n: (0, r))
    else:
        keyside = pl.BlockSpec((Q_BLOCK, DA), lambda r, n: (jnp.maximum(n - 1, 0), r))
    stat = pl.BlockSpec((1, Q_BLOCK, H), lambda r, n: (r, jnp.minimum(n, nb - 1), 0))
    dq, dk, dv = _call(
        body, name=name, grid=(d, steps), in_specs=[cur, cur, prev, cur, prev, cur, stat, stat],
        out_specs=[cur, keyside, keyside], out_shape=[SDS(view, BF16)] * 3,
        scratch_shapes=[pltpu.VMEM((Q_BLOCK, DA), F32), pltpu.VMEM((Q_BLOCK, DA), F32)],
        compiler_params=_params(40),
    )(qh, kh, kh, vb, vb, da, lse_d, delta_d)
    return dq, dk, dv


def _to_branch(stat, d):
    S, H = stat.shape
    return stat.reshape(S // d, d, H).transpose(1, 0, 2)


def _from_branch(stat):
    d, L, H = stat.shape
    return stat.transpose(1, 0, 2).reshape(L * d, H)


def att_combine(os_, lses, proj, gain, name):
    S, DA = os_[0].shape
    H = DA // HEAD_DIM
    tm = _tile(S, 256, 16 * DILATIONS[-1])
    nb = len(os_)

    def body(*refs):
        o_views, l_refs = refs[:nb], refs[nb:2 * nb]
        gate_ref, gain_ref, att_ref, y_ref, lse_ref, yt_ref = refs[2 * nb:2 * nb + 6]
        bufs = refs[2 * nb + 6:]
        for d, view, buf in zip(DILATIONS[1:], o_views[1:], bufs):
            for h in range(H):
                for r in range(d):
                    buf.at[h][pl.ds(r, tm // d, stride=d), :] = view[:, r * DA + h * HEAD_DIM:r * DA + (h + 1) * HEAD_DIM]
        ls = [r[...] for r in l_refs]
        top = ls[0]
        for l in ls[1:]:
            top = jnp.maximum(top, l)
        den = jnp.exp(ls[0] - top)
        for l in ls[1:]:
            den = den + jnp.exp(l - top)
        lse = top + jnp.log(den)
        lse_ref[...] = lse
        ws = [jnp.exp(l - lse) for l in ls]
        for h in range(H):
            sl = slice(h * HEAD_DIM, (h + 1) * HEAD_DIM)
            acc = ws[0][:, h:h + 1] * o_views[0][:, sl]
            for w, buf in zip(ws[1:], bufs):
                acc = acc + w[:, h:h + 1] * buf[h]
            att_ref[:, sl] = acc
        a = att_ref[...]
        g = gate_ref[...]
        y = a * _rstd(a) * gain_ref[...] * (g * _sig(g))
        y_ref[...] = y.astype(BF16)
        yt_ref[...] = y.T.astype(BF16)

    specs, _ = _dil_specs(S, DA, tm, F32)
    return _call(
        body, name=name, grid=(S // tm,),
        in_specs=specs + [_row(tm, H)] * nb + [_row(tm, DA, 3), _full((1, DA))],
        out_specs=[_row(tm, DA), _row(tm, DA), _row(tm, H), pl.BlockSpec((DA, tm), lambda i: (0, i))],
        out_shape=[SDS((S, DA), F32), SDS((S, 2 * DA), BF16), SDS((S, H), F32), SDS((2 * DA, S), BF16)],
        scratch_shapes=[_head_buf(tm, DA)] * (nb - 1),
        compiler_params=_params(48),
    )(*os_, *lses, proj, gain)


def gate_bwd(dcat, cblk, a, proj, gate_blk, gain, dilated, dproj, name):
    S, DA = a.shape
    H = DA // HEAD_DIM
    tm = _tile(S, 256, 16 * DILATIONS[-1])
    nd = len(DILATIONS) if dilated else 1

    def body(dy_ref, a_ref, gate_ref, gain_ref, *rest):
        dg_ref, gacc_ref, *rest = rest[0 if dproj is None else 1:]

        @pl.when(pl.program_id(0) == 0)
        def _():
            gacc_ref[...] = jnp.zeros_like(gacc_ref)

        dy, av, g, gain_v = dy_ref[...], a_ref[...], gate_ref[...], gain_ref[...]
        r = _rstd(av)
        n = av * r
        sg = _sig(g)
        dg_ref[...] = (dy * (n * gain_v) * (sg * (1.0 + g * (1.0 - sg)))).astype(BF16)
        drn = dy * (g * sg)
        gacc_ref[...] += jnp.sum(drn * n, axis=0, keepdims=True)
        dn = drn * gain_v
        da = r * (dn - n * jnp.mean(dn * n, axis=-1, keepdims=True))
        if dilated:
            da_refs, delta_ref, buf_ref = rest[:nd], rest[nd], rest[nd + 1]
            for h in range(H):
                buf_ref[h] = da[:, h * HEAD_DIM:(h + 1) * HEAD_DIM]
            _emit_dilated(buf_ref, da_refs, tm, DA)
            prod = da * av
            for h in range(H):
                delta_ref[:, h:h + 1] = jnp.sum(prod[:, h * HEAD_DIM:(h + 1) * HEAD_DIM], axis=-1, keepdims=True)
        else:
            rest[0][...] = da.astype(BF16)

    out_specs = [_row(tm, DA, gate_blk), _full((1, DA))]
    out_shape = [SDS((S, 7 * DA), BF16), SDS((1, DA), F32)]
    scratch = []
    carried = [] if dproj is None else [dproj]
    if dilated:
        specs, shapes = _dil_specs(S, DA, tm, BF16)
        out_specs += specs + [_row(tm, H)]
        out_shape += shapes + [SDS((S, H), F32)]
        scratch = [_head_buf(tm, DA)]
    else:
        out_specs.append(_row(tm, DA))
        out_shape.append(SDS((S, DA), BF16))
    return _call(
        body, name=name, grid=(S // tm,),
        in_specs=[_row(tm, DA, cblk), _row(tm, DA), _row(tm, DA, gate_blk), _full((1, DA))] + [pl.BlockSpec(memory_space=pl.ANY)] * len(carried),
        out_specs=out_specs, out_shape=out_shape, scratch_shapes=scratch, compiler_params=_params(48),
        input_output_aliases={4: 0} if carried else {},
    )(dcat, a, proj, gain, *carried)


def _by_sublane_phase(offsets):
    groups = [(p, [o for o in offsets if o % 8 == p]) for p in range(8)]
    return [(p, sorted(os_)) for p, os_ in groups if os_]


def _shifted_rows(src_ref, tmp_ref, base, phase, offsets, lanes):
    if phase == 0:
        return lambda o: src_ref[pl.ds(base + o, ROW_CHUNK), lanes]
    span = offsets[-1] - phase + ROW_CHUNK
    tmp_ref[phase, pl.ds(0, span), :] = src_ref[pl.ds(base + phase, span), lanes]
    return lambda o: tmp_ref[phase, pl.ds(o - phase, ROW_CHUNK), :]


def _shift_scratch():
    return pltpu.VMEM((8, CONV_PAD + ROW_CHUNK, LANE), F32)


def _fill_u(i, a_ref, b_ref, ah_ref, bh_ref, uext_ref, tm):
    uext_ref[pl.ds(CONV_PAD, tm), :] = a_ref[...] * _sig(b_ref[...])
    uh = ah_ref[...] * _sig(bh_ref[...])
    uext_ref[pl.ds(0, CONV_PAD), :] = jnp.where(i > 0, uh, 0.0)


def conv_fwd(proj, wk, bias, ln_g, ln_b, out_g, wpw, cat, cat_t, name):
    S = proj.shape[0]
    DC = proj.shape[1] // 7
    tm = _tile(S, 128, ROW_CHUNK)
    lead = CONV_PAD - (CONV_WIDTH - 1)

    def body(a_ref, b_ref, ah_ref, bh_ref, gate_ref, wk_ref, bias_ref, lg_ref, lb_ref, og_ref, wpw_ref, cat_ref, catt_ref,
             cy_ref, cyt_ref, zt_ref, conv_ref, y_ref, uext_ref, tmp_ref):
        _fill_u(pl.program_id(0), a_ref, b_ref, ah_ref, bh_ref, uext_ref, tm)

        def cols(cc, carry):
            c0 = pl.multiple_of(cc * LANE, LANE)
            lanes = pl.ds(c0, LANE)
            for rr in range(tm // ROW_CHUNK):
                acc = jnp.broadcast_to(bias_ref[:, lanes], (ROW_CHUNK, LANE))
                for phase, offsets in _by_sublane_phase(range(lead, lead + CONV_WIDTH)):
                    rows = _shifted_rows(uext_ref, tmp_ref, rr * ROW_CHUNK, phase, offsets, lanes)
                    for o in offsets:
                        acc = acc + wk_ref[o - lead:o - lead + 1, lanes] * rows(o)
                y_ref[pl.ds(rr * ROW_CHUNK, ROW_CHUNK), lanes] = acc
            return carry

        lax.fori_loop(0, DC // LANE, cols, 0)
        y = y_ref[...]
        yc = y - jnp.mean(y, axis=-1, keepdims=True)
        ln = yc * _rstd(yc) * lg_ref[...] + lb_ref[...]
        z = ln * _sig(ln)
        zt_ref[...] = z.T.astype(BF16)
        conv = jnp.dot(z.astype(BF16), wpw_ref[...], preferred_element_type=F32)
        conv_ref[...] = conv
        g = gate_ref[...]
        cy = conv * _rstd(conv) * og_ref[...] * (g * _sig(g))
        cy_ref[...] = cy.astype(BF16)
        cyt_ref[...] = cy.T.astype(BF16)

    vec = _full((1, DC))
    anywhere = pl.BlockSpec(memory_space=pl.ANY)
    return _call(
        body, name=name, grid=(S // tm,),
        in_specs=[_row(tm, DC, 4), _row(tm, DC, 5), _halo_prev(tm, DC, 4), _halo_prev(tm, DC, 5), _row(tm, DC, 6),
                  _full((CONV_PAD, DC)), vec, vec, vec, vec, _full((DC, DC)), anywhere, anywhere],
        out_specs=[_row(tm, DC, 1), pl.BlockSpec((DC, tm), lambda i: (1, i)), pl.BlockSpec((DC, tm), lambda i: (0, i)),
                   _row(tm, DC), _row(tm, DC)],
        out_shape=[SDS((S, 2 * DC), BF16), SDS((2 * DC, S), BF16), SDS((DC, S), BF16), SDS((S, DC), F32), SDS((S, DC), F32)],
        input_output_aliases={11: 0, 12: 1},
        scratch_shapes=[pltpu.VMEM((CONV_PAD + tm, DC), F32), _shift_scratch()],
        compiler_params=_params(48),
    )(proj, proj, proj, proj, proj, wk, bias, ln_g, ln_b, out_g, wpw, cat, cat_t)


def conv_bwd_ln(dconv, wpw, y, ln_g, ln_b, name):
    S, DC = y.shape
    tm = _tile(S, 256, 8)

    def body(dc_ref, wpw_ref, y_ref, lg_ref, lb_ref, dy_ref, glg_ref, glb_ref, gb_ref):
        @pl.when(pl.program_id(0) == 0)
        def _():
            glg_ref[...] = jnp.zeros_like(glg_ref)
            glb_ref[...] = jnp.zeros_like(glb_ref)
            gb_ref[...] = jnp.zeros_like(gb_ref)

        dz = lax.dot_general(dc_ref[...], wpw_ref[...], NT, preferred_element_type=F32)
        yv = y_ref[...]
        yc = yv - jnp.mean(yv, axis=-1, keepdims=True)
        rstd = _rstd(yc)
        yhat = yc * rstd
        ln = yhat * lg_ref[...] + lb_ref[...]
        sg = _sig(ln)
        dln = dz * (sg * (1.0 + ln * (1.0 - sg)))
        glb_ref[...] += jnp.sum(dln, axis=0, keepdims=True)
        glg_ref[...] += jnp.sum(dln * yhat, axis=0, keepdims=True)
        dyh = dln * lg_ref[...]
        dy = rstd * (dyh - jnp.mean(dyh, axis=-1, keepdims=True) - yhat * jnp.mean(dyh * yhat, axis=-1, keepdims=True))
        dy_ref[...] = dy
        gb_ref[...] += jnp.sum(dy, axis=0, keepdims=True)

    vec = _full((1, DC))
    return _call(
        body, name=name, grid=(S // tm,),
        in_specs=[_row(tm, DC), _full((DC, DC)), _row(tm, DC), vec, vec],
        out_specs=[_row(tm, DC), vec, vec, vec],
        out_shape=[SDS((S, DC), F32)] + [SDS((1, DC), F32)] * 3,
        compiler_params=_params(48),
    )(dconv, wpw, y, ln_g, ln_b)


def conv_bwd_dw(dy, proj, wk, dproj, name):
    S, DC = dy.shape
    tm = _tile(S, 128, ROW_CHUNK)
    nsteps = S // tm
    lead = CONV_PAD - (CONV_WIDTH - 1)
    groups = ROW_CHUNK // 8

    def body(dy_ref, dyn_ref, a_ref, b_ref, ah_ref, bh_ref, wk_ref, dproj_ref, dab_ref, gw_ref, uext_ref, dyext_ref, du_ref,
             tmp_dy_ref, tmp_u_ref):
        i = pl.program_id(0)

        @pl.when(i == 0)
        def _():
            gw_ref[...] = jnp.zeros_like(gw_ref)

        _fill_u(i, a_ref, b_ref, ah_ref, bh_ref, uext_ref, tm)
        dyext_ref[pl.ds(0, tm), :] = dy_ref[...]
        dyext_ref[pl.ds(tm, CONV_PAD), :] = jnp.where(i < nsteps - 1, dyn_ref[...], 0.0)

        def cols(cc, carry):
            c0 = pl.multiple_of(cc * LANE, LANE)
            lanes = pl.ds(c0, LANE)
            for rr in range(tm // ROW_CHUNK):
                base = rr * ROW_CHUNK
                acc = jnp.zeros((ROW_CHUNK, LANE), F32)
                for phase, offsets in _by_sublane_phase(range(CONV_WIDTH)):
                    rows = _shifted_rows(dyext_ref, tmp_dy_ref, base, phase, offsets, lanes)
                    for o in offsets:
                        j = CONV_WIDTH - 1 - o
                        acc = acc + wk_ref[j:j + 1, lanes] * rows(o)
                du_ref[pl.ds(base, ROW_CHUNK), lanes] = acc
                dyc = dyext_ref[pl.ds(base, ROW_CHUNK), lanes]
                for phase, offsets in _by_sublane_phase(range(lead, lead + CONV_WIDTH)):
                    rows = _shifted_rows(uext_ref, tmp_u_ref, base, phase, offsets, lanes)
                    for o in offsets:
                        prod = dyc * rows(o)
                        part = prod[0:8]
                        for k in range(1, groups):
                            part = part + prod[8 * k:8 * k + 8]
                        gw_ref[o - lead, :, lanes] += part
            return carry

        lax.fori_loop(0, DC // LANE, cols, 0)
        du = du_ref[...]
        sb = _sig(b_ref[...])
        dab_ref[:, :DC] = (du * sb).astype(BF16)
        dab_ref[:, DC:] = (du * a_ref[...] * sb * (1.0 - sb)).astype(BF16)

    return _call(
        body, name=name, grid=(nsteps,),
        in_specs=[_row(tm, DC), _halo_next(tm, DC, S // CONV_PAD), _row(tm, DC, 4), _row(tm, DC, 5),
                  _halo_prev(tm, DC, 4), _halo_prev(tm, DC, 5), _full((CONV_PAD, DC)), pl.BlockSpec(memory_space=pl.ANY)],
        out_specs=[_row(tm, 2 * DC, 2), _full((CONV_PAD, 8, DC))],
        out_shape=[SDS((S, 7 * DC), BF16), SDS((CONV_PAD, 8, DC), F32)],
        input_output_aliases={7: 0},
        scratch_shapes=[pltpu.VMEM((CONV_PAD + tm, DC), F32), pltpu.VMEM((tm + CONV_PAD, DC), F32), pltpu.VMEM((tm, DC), F32),
                        _shift_scratch(), _shift_scratch()],
        compiler_params=_params(40),
    )(dy, dy, proj, proj, proj, proj, wk, dproj)


def loss_head(xo, target, name):
    S, D = xo.shape
    tm = _tile(S, 256, 8)

    def body(x_ref, t_ref, dy_ref, dyb_ref, acc_ref):
        @pl.when(pl.program_id(0) == 0)
        def _():
            acc_ref[...] = jnp.zeros_like(acc_ref)

        err = x_ref[...] - t_ref[...]
        dy = err * (1.0 / D)
        dy_ref[...] = dy
        dyb_ref[...] = dy.astype(BF16)
        acc_ref[...] += jnp.sum(err * dy, axis=0, keepdims=True) * 0.5

    return _call(
        body, name=name, grid=(S // tm,), in_specs=[_row(tm, D), _row(tm, D)],
        out_specs=[_row(tm, D), _row(tm, D), _full((1, D))],
        out_shape=[SDS((S, D), F32), SDS((S, D), BF16), SDS((1, D), F32)],
        compiler_params=_params(32),
    )(xo, target)


def _coords():
    x, y, c = lax.axis_index("x"), lax.axis_index("y"), lax.axis_index("c")
    return x, y, c


def _lin(p):
    return 4 * p[0] + 2 * p[1] + p[2]


def _chip(p):
    return 2 * p[0] + p[1]


def _slot(ref, axis, idx, size):
    index = [slice(None)] * len(ref.shape)
    index[axis] = pl.ds(idx * size, size)
    return ref.at[tuple(index)]


def all_gather(blocks, axes, name):
    na = len(blocks)
    sizes = [b.shape[ax] for b, ax in zip(blocks, axes)]
    fulls = [SDS(b.shape[:ax] + (N_DEV * b.shape[ax],) + b.shape[ax + 1:], b.dtype) for b, ax in zip(blocks, axes)]

    def body(*refs):
        in_refs, out_refs = refs[:na], refs[na:2 * na]
        send_sems, recv_sems, local_sems = refs[2 * na:]
        x, y, c = _coords()
        me, sibling = (x, y, c), (x, y, 1 - c)
        chips = [(1 - x, y), (x, 1 - y), (1 - x, 1 - y)]
        south = c == 0
        relayed = (jnp.where(south, 1 - x, x), jnp.where(south, y, 1 - y), c)
        onward = (jnp.where(south, x, 1 - x), jnp.where(south, 1 - y, y), c)

        def place(a, p):
            return _slot(out_refs[a], axes[a], _lin(p), sizes[a])

        def copy(a, k, block, to, src=None):
            return pltpu.make_async_remote_copy(
                src_ref=place(a, block) if src is None else src, dst_ref=place(a, block),
                send_sem=send_sems.at[a, k], recv_sem=recv_sems.at[a, k], device_id=to, device_id_type=MESH)

        mine = [pltpu.make_async_copy(in_refs[a], place(a, me), local_sems.at[a]) for a in range(na)]
        for cp in mine:
            cp.start()
        first = []
        for a in range(na):
            first.append(copy(a, 0, me, sibling, src=in_refs[a]))
            first += [copy(a, 1 + j, me, (*chip, c), src=in_refs[a]) for j, chip in enumerate(chips[:2])]
        for cp in first:
            cp.start()
        later = []
        for a in range(na):
            for j, chip in enumerate(chips[:2]):
                copy(a, 1 + j, (*chip, c), me).wait_recv()
            later.append(copy(a, 3, relayed, onward))
            later += [copy(a, 4 + j, (*chip, c), sibling) for j, chip in enumerate(chips[:2])]
            for cp in later[-3:]:
                cp.start()
        for a in range(na):
            copy(a, 3, (*chips[2], c), me).wait_recv()
            later.append(copy(a, 6, (*chips[2], c), sibling))
            later[-1].start()
        for a in range(na):
            copy(a, 0, sibling, me).wait_recv()
            for j, chip in enumerate(chips):
                copy(a, 4 + j, (*chip, 1 - c), me).wait_recv()
        for cp in first + later:
            cp.wait_send()
        for cp in mine:
            cp.wait()

    hbm = pl.BlockSpec(memory_space=pltpu.HBM)
    return _call(
        body, name=name, in_specs=[hbm] * na, out_specs=[hbm] * na, out_shape=fulls,
        scratch_shapes=[pltpu.SemaphoreType.DMA((na, 7)), pltpu.SemaphoreType.DMA((na, 7)), pltpu.SemaphoreType.DMA((na,))],
    )(*blocks)


class _Exchange:
    def __init__(self, gather, srcs, axes, name, after, route="all", lands=None):
        self.gather, self.axes, self.name, self.route = gather, axes, name, route
        if route == "pass":
            self.na, self.ns = len(lands), 0
            self.sizes = [l.shape[ax] // N_DEV for l, ax in zip(lands, axes)]
            self.kinds = [pltpu.HBM(l.shape, l.dtype) for l in lands]
            self._start([pltpu.with_memory_space_constraint(t, pltpu.HBM) for t in lands], after)
            return
        self.na = self.ns = len(srcs)
        if gather:
            self.sizes = [s.shape[ax] for s, ax in zip(srcs, axes)]
            lands = [s.shape[:ax] + (N_DEV * s.shape[ax],) + s.shape[ax + 1:] for s, ax in zip(srcs, axes)]
        else:
            self.sizes = [None if ax is None else s.shape[ax] // N_DEV for s, ax in zip(srcs, axes)]
            lands = [s.shape if ax is None else (N_DEV,) + s.shape[:ax] + (sz,) + s.shape[ax + 1:]
                     for s, ax, sz in zip(srcs, axes, self.sizes)]
        self.kinds = [pltpu.HBM(s.shape, s.dtype) for s in srcs] + [pltpu.HBM(l, s.dtype) for l, s in zip(lands, srcs)]
        lands = [lax.empty(l, s.dtype) for l, s in zip(lands, srcs)]
        after = jnp.zeros((8, LANE), F32) if after is None else after
        self._start([pltpu.with_memory_space_constraint(t, pltpu.HBM) for t in list(srcs) + lands], after)

    def _src(self, a, ref, owner):
        if self.gather:
            return ref
        return ref.at[_lin(owner)] if self.axes[a] is None else _slot(ref, self.axes[a], _lin(owner), self.sizes[a])

    def _dst(self, a, land, sender):
        return _slot(land, self.axes[a], _lin(sender), self.sizes[a]) if self.gather else land.at[_lin(sender)]

    def _flips(self):
        if self.route == "all":
            return [(k >> 2 & 1, k >> 1 & 1, k & 1) for k in range(1, N_DEV)]
        if self.route == "chips":
            return [(0, 0, 1), (1, 0, 0), (0, 1, 0), (1, 1, 0)]
        if self.route == "sibling":
            return [(0, 0, 1)] * 4
        return [(1, 0, 0), (0, 1, 0), (1, 1, 0)]

    def _copies(self, refs, send_sems, recv_sems):
        na, ns = self.na, self.ns
        me = _coords()
        flips = self._flips()
        n = len(flips)
        others = [tuple(1 - v if f else v for v, f in zip(me, flip)) for flip in flips]
        sends, arrivals = [], []
        for a in range(na):
            land = refs[ns + a]
            for k, other in enumerate(others):
                if self.route == "pass":
                    peer = (me[0], me[1], 1 - me[2])
                    theirs = (other[0], other[1], 1 - me[2])
                    send = dict(src_ref=self._dst(a, land, other), dst_ref=self._dst(a, land, other))
                    arrive = dict(src_ref=self._dst(a, land, theirs), dst_ref=self._dst(a, land, theirs))
                elif self.route == "sibling":
                    peer = other
                    send = dict(src_ref=refs[a].at[k], dst_ref=land.at[k])
                    arrive = send
                elif self.route == "chip_parts":
                    peer = other
                    send = dict(src_ref=refs[a].at[_chip(peer)], dst_ref=land.at[_chip(me)])
                    arrive = dict(src_ref=refs[a].at[_chip(me)], dst_ref=land.at[_chip(peer)])
                else:
                    peer = other
                    send = dict(src_ref=self._src(a, refs[a], peer), dst_ref=self._dst(a, land, me))
                    arrive = dict(src_ref=self._src(a, refs[a], me), dst_ref=self._dst(a, land, peer))
                pair = dict(send_sem=send_sems.at[n * a + k], recv_sem=recv_sems.at[n * a + k], device_id=peer, device_id_type=MESH)
                sends.append(pltpu.make_async_remote_copy(**send, **pair))
                arrivals.append(pltpu.make_async_remote_copy(**arrive, **pair))
        return sends, arrivals

    def _place_own(self, operands):
        na = self.na
        mine = _chip(_coords()) if self.route == "chip_parts" else _lin(_coords())
        me = jnp.reshape(mine, (1,)).astype(jnp.int32)
        lands = []
        for a in range(na):
            src, land, ax = operands[a], operands[na + a], self.axes[a]
            if ax == 1:
                R, C = src.shape[0], self.sizes[a]
                steps, tile = 1, (R, C)
                in_map = lambda i, me_ref: (0, me_ref[0])
            elif ax is None:
                R, C = src.shape[1:]
                tr = _tile(R, 512, 16)
                steps, tile = R // tr, (None, tr, C)
                in_map = lambda i, me_ref: (me_ref[0], i, 0)
            else:
                R, C = (src.shape[0] if self.gather else self.sizes[a]), src.shape[1]
                tr = _tile(R, 512, 16)
                steps, tile = R // tr, (tr, C)
                in_map = (lambda i, me_ref: (i, 0)) if self.gather else (lambda i, me_ref, n=R // tr: (me_ref[0] * n + i, 0))
            if self.gather:
                out_spec = pl.BlockSpec(tile, lambda i, me_ref, n=steps: (me_ref[0] * n + i, 0))
            else:
                out_spec = pl.BlockSpec((None,) + tuple(t for t in tile if t is not None), lambda i, me_ref: (me_ref[0], i, 0))

            def body(me_ref, src_ref, land_ref, out_ref):
                out_ref[...] = src_ref[...]

            lands.append(_call(
                body, name=f"{self.name}_own{a}",
                grid_spec=pltpu.PrefetchScalarGridSpec(
                    num_scalar_prefetch=1, grid=(steps,),
                    in_specs=[pl.BlockSpec(tile, in_map), pl.BlockSpec(memory_space=pl.ANY)], out_specs=out_spec),
                out_shape=SDS(land.shape, land.dtype), input_output_aliases={2: 0}, compiler_params=_params(32),
            )(me, src, land))
        return operands[:na] + lands

    def _start(self, operands, after):
        nops = self.ns + self.na
        nsem = len(self._flips()) * self.na
        if self.ns and self.route != "sibling":
            operands = self._place_own(operands)

        def body(*refs):
            ins = refs[:nops]
            send_sems, recv_sems, token_ref = refs[nops + 1], refs[nops + 2], refs[2 * nops + 3]
            for cp in self._copies(ins, send_sems, recv_sems)[0]:
                cp.start()
            token_ref[...] = jnp.zeros_like(token_ref)

        hbm = pl.BlockSpec(memory_space=pltpu.HBM)
        sem = pl.BlockSpec(memory_space=pltpu.SEMAPHORE)
        outs = _call(
            body, name=self.name + "_start",
            in_specs=[hbm] * nops + [pl.BlockSpec(memory_space=pl.ANY)],
            out_specs=[sem, sem] + [hbm] * nops + [pl.BlockSpec(memory_space=pltpu.VMEM)],
            out_shape=[pltpu.SemaphoreType.DMA((nsem,)), pltpu.SemaphoreType.DMA((nsem,))] + self.kinds + [SDS((8, LANE), F32)],
            input_output_aliases={i: 2 + i for i in range(nops)},
            compiler_params=pltpu.CompilerParams(has_side_effects=pltpu.SideEffectType.DATAFLOW_SIDE_EFFECTING),
        )(*operands, after)
        self.sems, self.thru, self.token = outs[:2], outs[2:2 + nops], outs[2 + nops][0:1, 0:1]

    def wait(self, after):
        nops = self.ns + self.na

        def body(*refs):
            ins, send_sems, recv_sems = refs[:nops], refs[nops], refs[nops + 1]
            sends, arrivals = self._copies(ins, send_sems, recv_sems)
            for cp in sends:
                cp.wait_send()
            for cp in arrivals:
                cp.wait_recv()

        hbm = pl.BlockSpec(memory_space=pltpu.HBM)
        sem = pl.BlockSpec(memory_space=pltpu.SEMAPHORE)
        outs = _call(
            body, name=self.name + "_wait",
            in_specs=[hbm] * nops + [sem, sem, pl.BlockSpec(memory_space=pl.ANY)],
            out_specs=[hbm] * nops, out_shape=self.kinds,
            input_output_aliases={i: i for i in range(nops)},
            compiler_params=pltpu.CompilerParams(has_side_effects=pltpu.SideEffectType.DATAFLOW_SIDE_EFFECTING),
        )(*self.thru, *self.sems, after)
        return outs[self.ns:]


def adamw(w, m, v, parts, layer, prev, name):
    nl, R, C = w.shape
    nparts = parts.shape[0]
    tr = _tile(R, 128, 8) if R % 8 == 0 else R

    def body(w_ref, m_ref, v_ref, p_ref, *rest):
        g_ref, d_ref, mo_ref, vo_ref = rest[-4:]
        g = p_ref[0].astype(F32)
        for s in range(1, nparts):
            g = g + p_ref[s].astype(F32)
        mn = ADAM_B1 * m_ref[0] + (1.0 - ADAM_B1) * g
        vn = ADAM_B2 * v_ref[0] + (1.0 - ADAM_B2) * (g * g)
        m_hat = mn / (1.0 - ADAM_B1 ** ADAM_STEP)
        v_hat = vn / (1.0 - ADAM_B2 ** ADAM_STEP)
        g_ref[0] = g
        d_ref[0] = -ADAM_LR * (m_hat / (jnp.sqrt(v_hat) + ADAM_EPS) + ADAM_WD * w_ref[0])
        mo_ref[0] = mn
        vo_ref[0] = vn

    row = pl.BlockSpec((1, tr, C), lambda i: (layer, i, 0))
    carried = [] if prev is None else list(prev)
    return _call(
        body, name=name, grid=(R // tr,),
        in_specs=[row, row, row, pl.BlockSpec((nparts, tr, C), lambda i: (0, i, 0))] + [pl.BlockSpec(memory_space=pl.ANY)] * len(carried),
        out_specs=[row] * 4, out_shape=[SDS((nl, R, C), F32)] * 4,
        input_output_aliases={4 + k: k for k in range(len(carried))},
        compiler_params=_params(48),
    )(w, m, v, parts, *carried)


def _rope_tables(S):
    inv_freq = 1.0 / (ROPE_THETA ** (jnp.arange(0, HEAD_DIM, 2, dtype=F32) / HEAD_DIM))
    ang = jnp.arange(S, dtype=F32)[:, None] * inv_freq[None, :]
    cos, sin = jnp.cos(ang), jnp.sin(ang)
    return jnp.concatenate([cos, cos], axis=-1), jnp.concatenate([-sin, sin], axis=-1)


def _pack_small(D, norm_g, dw_bias, conv_ln_g, conv_ln_b, att_out_g, conv_out_g, q_norm_g, k_norm_g, extra=None):
    qk = jnp.concatenate([q_norm_g.reshape(-1), k_norm_g.reshape(-1)])
    qk = jnp.pad(qk, (0, D - qk.shape[0])).reshape(1, D)
    zero = jnp.zeros((1, D), F32)
    return jnp.concatenate([norm_g, dw_bias, conv_ln_g, conv_ln_b, att_out_g, conv_out_g, qk, zero,
                            zero if extra is None else extra, zero], axis=0)


def _unpack_small(p):
    rows = [p[2 * i:2 * i + 2] for i in range(6)]
    qk = p[12, :4 * HEAD_DIM].reshape(2, DEPTH, HEAD_DIM)
    return rows + [qk[0], qk[1]]


def kernel(x, norm_g, w_in, q_norm_g, k_norm_g, dw_kernel, dw_bias, conv_ln_g, conv_ln_b, w_pw, att_out_g, conv_out_g, w_out, loss_target, m_norm_g, m_w_in, m_q_norm_g, m_k_norm_g, m_dw_kernel, m_dw_bias, m_conv_ln_g, m_conv_ln_b, m_w_pw, m_att_out_g, m_conv_out_g, m_w_out, v_norm_g, v_w_in, v_q_norm_g, v_k_norm_g, v_dw_kernel, v_dw_bias, v_conv_ln_g, v_conv_ln_b, v_w_pw, v_att_out_g, v_conv_out_g, v_w_out):
    xs = x[0]
    D = xs.shape[1]
    bf = lambda t, l: t[l].astype(BF16)
    wint0, dwk_f = all_gather([bf(w_in, 0).T, dw_kernel], [0, 2], "gather_first")
    early = _Exchange(True, [bf(w_pw, 0), bf(w_out, 0)], [0, 0], "gather_layer0", after=wint0)
    later = _Exchange(True, [bf(w_in, 1).T, bf(w_pw, 1), bf(w_out, 1)], [0, 0, 0], "gather_layer1", after=early.token,
                      route="chips")
    landed = {}

    def weights(l, cur):
        if l == 0:
            return wint0, later.token
        landed[1] = landed["passing"].wait(cur)
        return landed[1][0], None

    def mixer_weights(l, cur):
        if l == 0:
            mine = early.wait(cur)
            landed["passing"] = _Exchange(True, None, [0, 0, 0], "pass_layer1", after=mine[0], route="pass",
                                          lands=later.wait(mine[0]))
            return (*mine, landed["passing"].token)
        return (*landed[1][1:], None)

    sent = [[] for _ in range(DEPTH)]

    def send_grads(l, grads, axes, tag):
        sent[l].append(_Exchange(False, grads, axes, f"scatter_{tag}_layer{l}", after=None))
        return sent[l][-1].token

    def win_grads(l, ht, dproj):
        if l > 0:
            return send_grads(l, [mm_nn(ht, dproj, BF16, f"dwin_{l}", owners=N_DEV)], [None], "w_in")
        core = jnp.reshape(lax.axis_index("c"), (1,)).astype(jnp.int32)
        theirs = mm_core_blocks(ht, dproj, 1 - core, f"dwin_{l}_for_sibling")
        to_sibling = _Exchange(False, [theirs], [None], f"scatter_w_in_sibling_layer{l}", after=None, route="sibling")
        mine = mm_core_blocks(ht, dproj, core + to_sibling.token[0].astype(jnp.int32), f"dwin_{l}_own")
        (from_sibling,) = to_sibling.wait(mine)
        both = add_blocks(mine, from_sibling, f"dwin_{l}_chip_sum")
        sent[l].append(_Exchange(False, [both], [None], f"scatter_w_in_chips_layer{l}", after=None, route="chip_parts"))
        return sent[l][-1].token

    dx, loss_cols, small = local_step(xs, loss_target[0], weights, mixer_weights, dwk_f, norm_g, q_norm_g, k_norm_g, dw_bias,
                                      conv_ln_g, conv_ln_b, att_out_g, conv_out_g, send_grads, win_grads)

    big = ((w_pw, m_w_pw, v_w_pw), (w_out, m_w_out, v_w_out), (dw_kernel, m_dw_kernel, v_dw_kernel), (w_in, m_w_in, v_w_in))
    results = [None] * len(big)
    after = dx
    for l in reversed(range(DEPTH)):
        parts = [p for ex in sent[l] for p in ex.wait(after)]
        for i, ((w, m, v), p) in enumerate(zip(big, parts)):
            results[i] = adamw(w, m, v, p, l, results[i], f"adamw_{w.shape[1]}_{w.shape[2]}_{l}")
        after = results[0][3]
    r_wpw, r_wout, r_dwk, r_win = results

    stack = lambda k: jnp.concatenate(small[k], axis=0)
    mine = _pack_small(D, stack("norm_g"), stack("dw_bias"), stack("conv_ln_g"), stack("conv_ln_b"), stack("att_out_g"),
                       stack("conv_out_g"), stack("q"), stack("k"), extra=loss_cols)
    (p_small,) = all_gather([mine], [0], "gather_small")
    pk = lambda n, dw, lg, lb, ao, co, q, k: _pack_small(D, n, dw, lg, lb, ao, co, q, k)[None]
    r_small = adamw(pk(norm_g, dw_bias, conv_ln_g, conv_ln_b, att_out_g, conv_out_g, q_norm_g, k_norm_g),
                    pk(m_norm_g, m_dw_bias, m_conv_ln_g, m_conv_ln_b, m_att_out_g, m_conv_out_g, m_q_norm_g, m_k_norm_g),
                    pk(v_norm_g, v_dw_bias, v_conv_ln_g, v_conv_ln_b, v_att_out_g, v_conv_out_g, v_q_norm_g, v_k_norm_g),
                    p_small.reshape(N_DEV, 16, D), 0, None, "adamw_small")
    r_small = [r[0] for r in r_small]
    loss = jnp.sum(r_small[0][14])

    outs = [loss, dx[None]]
    for i in range(4):
        n_, dwb, lg, lb, ao, co, q_, k_ = _unpack_small(r_small[i])
        outs += [n_, r_win[i], q_, k_, r_dwk[i], dwb, lg, lb, r_wpw[i], ao, co, r_wout[i]]
    return tuple(outs)


def local_step(xs, target, weights, mixer_weights, dwk_f, norm_g, q_norm_g, k_norm_g, dw_bias, conv_ln_g, conv_ln_b,
               att_out_g, conv_out_g, send_grads, win_grads):
    S, D = xs.shape
    cos2, sin2 = _rope_tables(S)
    dwk_f = jnp.pad(dwk_f, ((0, 0), (0, CONV_PAD - CONV_WIDTH), (0, 0)))

    def vec(p, l, zero=None):
        row = p[l].reshape(1, -1)
        return row if zero is None else row + zero

    saved = []
    cur = xs
    for l in range(DEPTH):
        wint, zero = weights(l, cur)
        proj, _, h_t = in_proj(cur, vec(norm_g, l, zero), wint, f"in_proj_{l}")
        qs, ks, vs = qk_prep(proj, vec(q_norm_g, l), vec(k_norm_g, l), cos2, sin2, f"qk_prep_{l}")
        os_, lses = [], []
        for i, d in enumerate(DILATIONS):
            o, lse = attn_fwd(qs[i], ks[i], vs[i], d, f"attn_fwd_{l}_d{d}")
            os_.append(o)
            lses.append(_from_branch(lse))
        att, cat, lse, cat_t = att_combine(os_, lses, proj, vec(att_out_g, l), f"att_combine_{l}")
        wpw, wout, zero = mixer_weights(l, lse)
        cat, cat_t, z_t, conv, y = conv_fwd(proj, dwk_f[l], vec(dw_bias, l, zero), vec(conv_ln_g, l), vec(conv_ln_b, l),
                                            vec(conv_out_g, l), wpw, cat, cat_t, f"conv_fwd_{l}")
        nxt = mm_nn(cat, wout, F32, f"out_proj_{l}", add=cur)
        saved.append(dict(x=cur, proj=proj, h_t=h_t, qs=qs, ks=ks, vs=vs, att=att, lse=lse, cat_t=cat_t, z_t=z_t, conv=conv,
                          y=y, wint=wint, wpw=wpw, wout=wout))
        cur = nxt

    dx, dxb, loss_cols = loss_head(cur, target, "loss_head")

    small = {k: [None] * DEPTH for k in ("norm_g", "dw_bias", "conv_ln_g", "conv_ln_b", "att_out_g", "conv_out_g", "q", "k")}
    for l in reversed(range(DEPTH)):
        sv = saved[l]
        proj = sv["proj"]
        dcat = mm_nt(dxb, sv["wout"], F32, f"dcat_{l}")
        g_wout = mm_nn(sv["cat_t"], dxb, BF16, f"dwout_{l}")
        dproj, small["conv_out_g"][l], dconv = gate_bwd(dcat, 1, sv["conv"], proj, 6, vec(conv_out_g, l), False, None,
                                                        f"conv_gate_bwd_{l}")
        dy, small["conv_ln_g"][l], small["conv_ln_b"][l], small["dw_bias"][l] = conv_bwd_ln(
            dconv, sv["wpw"], sv["y"], vec(conv_ln_g, l), vec(conv_ln_b, l), f"conv_bwd_ln_{l}")
        g_wpw = mm_nn(sv["z_t"], dconv, BF16, f"dwpw_{l}")
        dproj, gw = conv_bwd_dw(dy, proj, dwk_f[l], dproj, f"conv_bwd_dw_{l}")
        g_dwk = jnp.sum(gw, axis=1)[:CONV_WIDTH]
        zero = send_grads(l, [g_wpw, g_wout, g_dwk], [0, 0, 1], "mixer")
        dproj, small["att_out_g"][l], *datts, delta = gate_bwd(dcat, 0, sv["att"], proj, 3, vec(att_out_g, l, zero), True,
                                                                dproj, f"att_gate_bwd_{l}")
        dqs, dks, dvs = [], [], []
        for i, d in enumerate(DILATIONS):
            dq, dk, dv = attn_bwd(sv["qs"][i], sv["ks"][i], sv["vs"][i], datts[i], _to_branch(sv["lse"], d),
                                  _to_branch(delta, d), d, f"attn_bwd_{l}_d{d}")
            dqs.append(dq)
            dks.append(dk)
            dvs.append(dv)
        dproj, small["q"][l], small["k"][l] = qk_prep_bwd(dqs, dks, dvs, proj, vec(q_norm_g, l), vec(k_norm_g, l),
                                                          cos2, sin2, dproj, f"qk_prep_bwd_{l}")
        zero = win_grads(l, sv["h_t"], dproj)
        dx, dxb, small["norm_g"][l] = in_proj_bwd(dproj, sv["wint"], sv["x"], vec(norm_g, l, zero), dx, f"in_proj_bwd_{l}")

    return dx, loss_cols, small
```

```python
import jax
import jax.numpy as jnp
from jax import lax
from jax.experimental import pallas as pl
from jax.experimental.pallas import tpu as pltpu

F32 = jnp.float32
BF16 = jnp.bfloat16
SDS = jax.ShapeDtypeStruct
MESH = pl.DeviceIdType.MESH

N_DEV = 8
DEPTH = 2
HEAD_DIM = 128
CONV_WIDTH = 31
CONV_PAD = 32
DILATIONS = (1, 4, 16)
Q_BLOCK = 128
ROPE_THETA = 10000.0
EPS = 1e-6
NEG = -1e30
ADAM_LR, ADAM_B1, ADAM_B2, ADAM_EPS, ADAM_WD, ADAM_STEP = 0.001, 0.9, 0.999, 1e-08, 0.01, 10
LANE = 128
ROW_CHUNK = 64
MIB = 1 << 20
NT = (((1,), (1,)), ((), ()))
TN = (((0,), (0,)), ((), ()))


def _call(body, **kw):
    return pl.pallas_call(body, **kw)


def _params(vmem_mib):
    return pltpu.CompilerParams(vmem_limit_bytes=vmem_mib * MIB)


def _tile(dim, pref, mult):
    t = min(pref, dim)
    while dim % t or t % mult:
        t -= mult
    return t


def _sig(v):
    return jax.nn.sigmoid(v)


def _rstd(v):
    return lax.rsqrt(jnp.mean(v * v, axis=-1, keepdims=True) + EPS)


def _row(tm, cb, c=0):
    return pl.BlockSpec((tm, cb), lambda i: (i, c))


def _full(shape):
    return pl.BlockSpec(shape, lambda i: (0,) * len(shape))


def _halo_prev(tm, cb, c=0):
    k = tm // CONV_PAD
    return pl.BlockSpec((CONV_PAD, cb), lambda i: (jnp.maximum(i * k - 1, 0), c))


def _halo_next(tm, cb, nblk, c=0):
    k = tm // CONV_PAD
    return pl.BlockSpec((CONV_PAD, cb), lambda i: (jnp.minimum((i + 1) * k, nblk - 1), c))


def in_proj(x, g, wt, name):
    S, D = x.shape
    N = wt.shape[0]
    tm, tn = _tile(S, 512, 8), _tile(N, 1792, LANE)

    def body(x_ref, g_ref, w_ref, o_ref, h_ref, ht_ref):
        @pl.when(pl.program_id(1) == 0)
        def _():
            xf = x_ref[...]
            hf = xf * _rstd(xf) * g_ref[...]
            h_ref[...] = hf.astype(BF16)
            ht_ref[...] = hf.T.astype(BF16)

        o_ref[...] = lax.dot_general(h_ref[...], w_ref[...], NT, preferred_element_type=F32)

    return _call(
        body, name=name, grid=(S // tm, N // tn),
        in_specs=[pl.BlockSpec((tm, D), lambda i, j: (i, 0)), pl.BlockSpec((1, D), lambda i, j: (0, 0)),
                  pl.BlockSpec((tn, D), lambda i, j: (j, 0))],
        out_specs=[pl.BlockSpec((tm, tn), lambda i, j: (i, j)), pl.BlockSpec((tm, D), lambda i, j: (i, 0)),
                   pl.BlockSpec((D, tm), lambda i, j: (0, i))],
        out_shape=[SDS((S, N), F32), SDS((S, D), BF16), SDS((D, S), BF16)],
        compiler_params=_params(48),
    )(x, g, wt)


def mm_nn(a, b, out_dtype, name, add=None, owners=1):
    M, K = a.shape
    N = b.shape[1]
    tm, tn = _tile(M, 1024 if owners > 1 else 512, 8), _tile(N // owners, 1024, LANE)
    per = N // owners // tn

    def body(*refs):
        a_ref, b_ref = refs[0], refs[1]
        o_ref = refs[-1]
        acc = jnp.dot(a_ref[...], b_ref[...], preferred_element_type=F32)
        if add is not None:
            acc = acc + refs[2][...]
        o_ref[...] = acc.astype(out_dtype)

    in_specs = [pl.BlockSpec((tm, K), lambda i, j: (i, 0)), pl.BlockSpec((K, tn), lambda i, j: (0, j))]
    args = [a, b]
    if add is not None:
        in_specs.append(pl.BlockSpec((tm, tn), lambda i, j: (i, j)))
        args.append(add)
    if owners > 1:
        out_spec = pl.BlockSpec((None, tm, tn), lambda i, j: (j // per, i, j % per))
        out_shape = SDS((owners, M, N // owners), out_dtype)
    else:
        out_spec, out_shape = pl.BlockSpec((tm, tn), lambda i, j: (i, j)), SDS((M, N), out_dtype)
    return _call(
        body, name=name, grid=(M // tm, N // tn), in_specs=in_specs, out_specs=out_spec, out_shape=out_shape,
        compiler_params=_params(48),
    )(*args)


def mm_core_blocks(a, b, core, name):
    M, K = a.shape
    blk = b.shape[1] // N_DEV
    tm, tn = _tile(M, 1024, 8), _tile(blk, 1024, LANE)
    per = blk // tn

    def body(core_ref, a_ref, b_ref, o_ref):
        o_ref[...] = jnp.dot(a_ref[...], b_ref[...], preferred_element_type=F32).astype(BF16)

    return _call(
        body, name=name,
        grid_spec=pltpu.PrefetchScalarGridSpec(
            num_scalar_prefetch=1, grid=(M // tm, 4 * per),
            in_specs=[pl.BlockSpec((tm, K), lambda i, j, core_ref: (i, 0)),
                      pl.BlockSpec((K, tn), lambda i, j, core_ref: (0, (2 * (j // per) + core_ref[0]) * per + j % per))],
            out_specs=pl.BlockSpec((None, tm, tn), lambda i, j, core_ref: (j // per, i, j % per))),
        out_shape=SDS((4, M, blk), BF16), compiler_params=_params(48),
    )(core, a, b)


def add_blocks(a, b, name):
    n, R, C = a.shape
    tr = _tile(R, 512, 16)
    spec = pl.BlockSpec((None, tr, C), lambda k, i: (k, i, 0))

    def body(a_ref, b_ref, o_ref):
        o_ref[...] = (a_ref[...].astype(F32) + b_ref[...].astype(F32)).astype(BF16)

    return _call(body, name=name, grid=(n, R // tr), in_specs=[spec, spec], out_specs=spec, out_shape=SDS(a.shape, BF16),
                 compiler_params=_params(32))(a, b)


def mm_nt(a, b, out_dtype, name):
    M, K = a.shape
    N = b.shape[0]
    tm, tn = _tile(M, 512, 8), _tile(N, 2048, LANE)

    def body(a_ref, b_ref, o_ref):
        o_ref[...] = lax.dot_general(a_ref[...], b_ref[...], NT, preferred_element_type=F32).astype(out_dtype)

    return _call(
        body, name=name, grid=(M // tm, N // tn),
        in_specs=[pl.BlockSpec((tm, K), lambda i, j: (i, 0)), pl.BlockSpec((tn, K), lambda i, j: (j, 0))],
        out_specs=pl.BlockSpec((tm, tn), lambda i, j: (i, j)), out_shape=SDS((M, N), out_dtype),
        compiler_params=_params(40),
    )(a, b)


def in_proj_bwd(dproj, wt, x, g, dx_out, name):
    S, K = dproj.shape
    D = wt.shape[1]
    tm, tk = _tile(S, 512, 8), _tile(K, 1024, LANE)
    nk = K // tk

    def body(dp_ref, w_ref, x_ref, g_ref, dxo_ref, dx_ref, dxb_ref, gacc_ref):
        i, k = pl.program_id(0), pl.program_id(1)
        part = jnp.dot(dp_ref[...], w_ref[...], preferred_element_type=F32)

        @pl.when(k == 0)
        def _():
            dx_ref[...] = part

        @pl.when(k > 0)
        def _():
            dx_ref[...] += part

        @pl.when((k == 0) & (i == 0))
        def _():
            gacc_ref[...] = jnp.zeros_like(gacc_ref)

        @pl.when(k == nk - 1)
        def _():
            dh = dx_ref[...]
            xf = x_ref[...]
            r = _rstd(xf)
            n = xf * r
            gacc_ref[...] += jnp.sum(dh * n, axis=0, keepdims=True)
            dn = dh * g_ref[...]
            dx = r * (dn - n * jnp.mean(dn * n, axis=-1, keepdims=True)) + dxo_ref[...]
            dx_ref[...] = dx
            dxb_ref[...] = dx.astype(BF16)

    return _call(
        body, name=name, grid=(S // tm, nk),
        in_specs=[pl.BlockSpec((tm, tk), lambda i, k: (i, k)), pl.BlockSpec((tk, D), lambda i, k: (k, 0)),
                  pl.BlockSpec((tm, D), lambda i, k: (i, 0)), pl.BlockSpec((1, D), lambda i, k: (0, 0)),
                  pl.BlockSpec((tm, D), lambda i, k: (i, 0))],
        out_specs=[pl.BlockSpec((tm, D), lambda i, k: (i, 0)), pl.BlockSpec((tm, D), lambda i, k: (i, 0)),
                   pl.BlockSpec((1, D), lambda i, k: (0, 0))],
        out_shape=[SDS((S, D), F32), SDS((S, D), BF16), SDS((1, D), F32)],
        compiler_params=_params(54),
    )(dproj, wt, x, g, dx_out)


def _dil_specs(S, DA, tm, dtype):
    specs = [pl.BlockSpec((tm // d, d * DA), lambda i: (i, 0)) for d in DILATIONS]
    shapes = [SDS((S // d, d * DA), dtype) for d in DILATIONS]
    return specs, shapes


def _head_buf(tm, DA):
    return pltpu.VMEM((DA // HEAD_DIM, tm, HEAD_DIM), F32)


def _emit_dilated(buf_ref, dsts, tm, DA):
    for d, dst in zip(DILATIONS, dsts):
        for h in range(DA // HEAD_DIM):
            for r in range(d):
                rows = slice(None) if d == 1 else pl.ds(r, tm // d, stride=d)
                dst[:, r * DA + h * HEAD_DIM:r * DA + (h + 1) * HEAD_DIM] = buf_ref.at[h][rows, :].astype(BF16)


def _collect_dilated(acc_ref, parts, tm, DA):
    for d, p in zip(DILATIONS, parts):
        for h in range(DA // HEAD_DIM):
            for r in range(d):
                part = p[:, r * DA + h * HEAD_DIM:r * DA + (h + 1) * HEAD_DIM].astype(F32)
                if d == 1:
                    acc_ref[h] = part
                else:
                    rows = pl.ds(r, tm // d, stride=d)
                    acc_ref.at[h][rows, :] = acc_ref.at[h][rows, :] + part


def qk_prep(proj, gq, gk, cos2, sin2, name):
    S = proj.shape[0]
    DA = proj.shape[1] // 7
    H = DA // HEAD_DIM
    tm = _tile(S, 256, 16 * DILATIONS[-1])
    nd = len(DILATIONS)

    def body(q_ref, k_ref, v_ref, gq_ref, gk_ref, c_ref, s_ref, *rest):
        outs, buf_ref = rest[:3 * nd], rest[3 * nd]
        for t, (src, g_ref) in enumerate(((q_ref, gq_ref), (k_ref, gk_ref))):
            gain = g_ref[...]
            for h in range(H):
                sl = slice(h * HEAD_DIM, (h + 1) * HEAD_DIM)
                for r0 in range(0, tm, ROW_CHUNK):
                    rows = slice(r0, r0 + ROW_CHUNK)
                    xh = src[rows, sl]
                    n = xh * _rstd(xh) * gain
                    buf_ref[h, rows, :] = n * c_ref[rows, :] + pltpu.roll(n, HEAD_DIM // 2, 1) * s_ref[rows, :]
            _emit_dilated(buf_ref, outs[t * nd:(t + 1) * nd], tm, DA)
        for h in range(H):
            buf_ref[h] = v_ref[:, h * HEAD_DIM:(h + 1) * HEAD_DIM]
        _emit_dilated(buf_ref, outs[2 * nd:], tm, DA)

    specs, shapes = _dil_specs(S, DA, tm, BF16)
    outs = _call(
        body, name=name, grid=(S // tm,),
        in_specs=[_row(tm, DA, 0), _row(tm, DA, 1), _row(tm, DA, 2), _full((1, HEAD_DIM)), _full((1, HEAD_DIM)),
                  _row(tm, HEAD_DIM), _row(tm, HEAD_DIM)],
        out_specs=specs * 3, out_shape=shapes * 3, scratch_shapes=[_head_buf(tm, DA)],
        compiler_params=_params(48),
    )(proj, proj, proj, gq, gk, cos2, sin2)
    return outs[:nd], outs[nd:2 * nd], outs[2 * nd:]


def qk_prep_bwd(dqs, dks, dvs, proj, gq, gk, cos2, sin2, dproj, name):
    S = proj.shape[0]
    DA = proj.shape[1] // 7
    H = DA // HEAD_DIM
    tm = _tile(S, 256, 16 * DILATIONS[-1])
    nb = len(dqs)

    def body(*refs):
        dq_refs, dk_refs, dv_refs = refs[:nb], refs[nb:2 * nb], refs[2 * nb:3 * nb]
        q_ref, k_ref, gq_ref, gk_ref, c_ref, s_ref = refs[3 * nb:3 * nb + 6]
        out_ref, gqa_ref, gka_ref, acc_ref = refs[3 * nb + 7:]

        @pl.when(pl.program_id(0) == 0)
        def _():
            gqa_ref[...] = jnp.zeros_like(gqa_ref)
            gka_ref[...] = jnp.zeros_like(gka_ref)

        for parts, x_ref, g_ref, col, gacc in ((dq_refs, q_ref, gq_ref, 0, gqa_ref),
                                               (dk_refs, k_ref, gk_ref, DA, gka_ref)):
            gain = g_ref[...]
            gsum = jnp.zeros((1, HEAD_DIM), F32)
            _collect_dilated(acc_ref, parts, tm, DA)
            for h in range(H):
                sl = slice(h * HEAD_DIM, (h + 1) * HEAD_DIM)
                for r0 in range(0, tm, ROW_CHUNK):
                    rows = slice(r0, r0 + ROW_CHUNK)
                    dout = acc_ref[h, rows, :]
                    dn = dout * c_ref[rows, :] + pltpu.roll(dout * s_ref[rows, :], HEAD_DIM // 2, 1)
                    xh = x_ref[rows, sl]
                    r = _rstd(xh)
                    xn = xh * r
                    gsum = gsum + jnp.sum(dn * xn, axis=0, keepdims=True)
                    dnn = dn * gain
                    dx = r * (dnn - xn * jnp.mean(dnn * xn, axis=-1, keepdims=True))
                    out_ref[rows, col + h * HEAD_DIM:col + (h + 1) * HEAD_DIM] = dx.astype(BF16)
            gacc[...] += gsum
        _collect_dilated(acc_ref, dv_refs, tm, DA)
        for h in range(H):
            out_ref[:, 2 * DA + h * HEAD_DIM:2 * DA + (h + 1) * HEAD_DIM] = acc_ref[h].astype(BF16)

    specs, _ = _dil_specs(S, DA, tm, BF16)
    return _call(
        body, name=name, grid=(S // tm,),
        in_specs=specs * 3 + [_row(tm, DA, 0), _row(tm, DA, 1), _full((1, HEAD_DIM)),
                              _full((1, HEAD_DIM)), _row(tm, HEAD_DIM), _row(tm, HEAD_DIM), pl.BlockSpec(memory_space=pl.ANY)],
        out_specs=[_row(tm, 3 * DA)] + [_full((1, HEAD_DIM))] * 2,
        out_shape=[SDS((S, 7 * DA), BF16)] + [SDS((1, HEAD_DIM), F32)] * 2,
        input_output_aliases={3 * nb + 6: 0},
        scratch_shapes=[_head_buf(tm, DA)],
        compiler_params=_params(48),
    )(*dqs, *dks, *dvs, proj, proj, gq, gk, cos2, sin2, dproj)


def _band_mask(n):
    row = lax.broadcasted_iota(jnp.int32, (Q_BLOCK, 2 * Q_BLOCK), 0)
    col = lax.broadcasted_iota(jnp.int32, (Q_BLOCK, 2 * Q_BLOCK), 1)
    first = jnp.where(n > 0, Q_BLOCK, 2 * Q_BLOCK + 1)
    return (col <= row) | ((col - row) >= first)


def attn_fwd(qh, kh, vb, d, name):
    L = qh.shape[0]
    DA = qh.shape[1] // d
    H = DA // HEAD_DIM
    nb = L // Q_BLOCK
    scale = HEAD_DIM ** -0.5
    view = (L, d * DA)

    def body(q_ref, kc_ref, kp_ref, vc_ref, vp_ref, o_ref, lse_ref):
        mask = _band_mask(pl.program_id(1))
        ones = jnp.ones((2 * Q_BLOCK, HEAD_DIM), BF16)
        for h in range(H):
            sl = slice(h * HEAD_DIM, (h + 1) * HEAD_DIM)
            keys = jnp.concatenate([kc_ref[:, sl], kp_ref[:, sl]], axis=0)
            s = lax.dot_general(q_ref[:, sl], keys, NT, preferred_element_type=F32) * scale
            s = jnp.where(mask, s, NEG)
            m = jnp.max(s, axis=-1, keepdims=True)
            p = jnp.exp(s - m).astype(BF16)
            vals = jnp.concatenate([jnp.concatenate([vc_ref[:, sl], vp_ref[:, sl]], axis=0), ones], axis=1)
            ol = jnp.dot(p, vals, preferred_element_type=F32)
            l = ol[:, HEAD_DIM:]
            o_ref[:, sl] = ol[:, :HEAD_DIM] / l
            lse_ref[0, :, h:h + 1] = m + jnp.log(l[:, 0:1])

    cur = pl.BlockSpec((Q_BLOCK, DA), lambda r, n: (n, r))
    prev = pl.BlockSpec((Q_BLOCK, DA), lambda r, n: (jnp.maximum(n - 1, 0), r))
    o, lse = _call(
        body, name=name, grid=(d, nb), in_specs=[cur, cur, prev, cur, prev],
        out_specs=[cur, pl.BlockSpec((1, Q_BLOCK, H), lambda r, n: (r, n, 0))],
        out_shape=[SDS(view, F32), SDS((d, L, H), F32)],
        compiler_params=_params(32),
    )(qh, kh, kh, vb, vb)
    return o, lse


def attn_bwd(qh, kh, vb, da, lse_d, delta_d, d, name):
    L = qh.shape[0]
    DA = qh.shape[1] // d
    H = DA // HEAD_DIM
    nb = L // Q_BLOCK
    scale = HEAD_DIM ** -0.5
    view = (L, d * DA)
    resident = d > 1

    def body(q_ref, kc_ref, kp_ref, vc_ref, vp_ref, do_ref, lse_ref, dl_ref, dq_ref, dk_ref, dv_ref, dkc_ref, dvc_ref):
        n = pl.program_id(1)
        mask = _band_mask(n)
        if resident:
            done = pl.ds(pl.multiple_of(jnp.maximum(n - 1, 0) * Q_BLOCK, Q_BLOCK), Q_BLOCK)
            fresh = pl.ds(pl.multiple_of(n * Q_BLOCK, Q_BLOCK), Q_BLOCK)
        else:
            done = slice(None)

        @pl.when(n == 0)
        def _():
            dkc_ref[...] = jnp.zeros_like(dkc_ref)
            dvc_ref[...] = jnp.zeros_like(dvc_ref)

        @pl.when(n < nb)
        def _():
            for h in range(H):
                sl = slice(h * HEAD_DIM, (h + 1) * HEAD_DIM)
                q, do = q_ref[:, sl], do_ref[:, sl]
                keys = jnp.concatenate([kc_ref[:, sl], kp_ref[:, sl]], axis=0)
                vals = jnp.concatenate([vc_ref[:, sl], vp_ref[:, sl]], axis=0)
                lse = jnp.broadcast_to(lse_ref[0, :, h:h + 1], (Q_BLOCK, 2 * Q_BLOCK))
                dl = jnp.broadcast_to(dl_ref[0, :, h:h + 1], (Q_BLOCK, 2 * Q_BLOCK))
                s = lax.dot_general(q, keys, NT, preferred_element_type=F32) * scale
                p = jnp.exp(jnp.where(mask, s, NEG) - lse)
                dp = lax.dot_general(do, vals, NT, preferred_element_type=F32)
                ds = (p * (dp - dl) * scale).astype(BF16)
                dq_ref[:, sl] = jnp.dot(ds, keys, preferred_element_type=F32).astype(BF16)
                dk = lax.dot_general(ds, q, TN, preferred_element_type=F32)
                dv = lax.dot_general(p.astype(BF16), do, TN, preferred_element_type=F32)
                dk_ref[done, sl] = (dkc_ref[:, sl] + dk[Q_BLOCK:]).astype(BF16)
                dv_ref[done, sl] = (dvc_ref[:, sl] + dv[Q_BLOCK:]).astype(BF16)
                dkc_ref[:, sl] = dk[:Q_BLOCK]
                dvc_ref[:, sl] = dv[:Q_BLOCK]
                if resident:
                    dk_ref[fresh, sl] = dk[:Q_BLOCK].astype(BF16)
                    dv_ref[fresh, sl] = dv[:Q_BLOCK].astype(BF16)

        if not resident:
            @pl.when(n == nb)
            def _():
                dk_ref[...] = dkc_ref[...].astype(BF16)
                dv_ref[...] = dvc_ref[...].astype(BF16)

    steps = nb if resident else nb + 1
    cur = pl.BlockSpec((Q_BLOCK, DA), lambda r, n: (jnp.minimum(n, nb - 1), r))
    prev = pl.BlockSpec((Q_BLOCK, DA), lambda r, n: (jnp.clip(n - 1, 0, nb - 1), r))
    if resident:
        keyside = pl.BlockSpec((L, DA), lambda r, n: (0, r))
    else:
        keyside = pl.BlockSpec((Q_BLOCK, DA), lambda r, n: (jnp.maximum(n - 1, 0), r))
    stat = pl.BlockSpec((1, Q_BLOCK, H), lambda r, n: (r, jnp.minimum(n, nb - 1), 0))
    dq, dk, dv = _call(
        body, name=name, grid=(d, steps), in_specs=[cur, cur, prev, cur, prev, cur, stat, stat],
        out_specs=[cur, keyside, keyside], out_shape=[SDS(view, BF16)] * 3,
        scratch_shapes=[pltpu.VMEM((Q_BLOCK, DA), F32), pltpu.VMEM((Q_BLOCK, DA), F32)],
        compiler_params=_params(40),
    )(qh, kh, kh, vb, vb, da, lse_d, delta_d)
    return dq, dk, dv


def _to_branch(stat, d):
    S, H = stat.shape
    return stat.reshape(S // d, d, H).transpose(1, 0, 2)


def _from_branch(stat):
    d, L, H = stat.shape
    return stat.transpose(1, 0, 2).reshape(L * d, H)


def att_combine(os_, lses, proj, gain, name):
    S, DA = os_[0].shape
    H = DA // HEAD_DIM
    tm = _tile(S, 256, 16 * DILATIONS[-1])
    nb = len(os_)

    def body(*refs):
        o_views, l_refs = refs[:nb], refs[nb:2 * nb]
        gate_ref, gain_ref, att_ref, y_ref, lse_ref, yt_ref = refs[2 * nb:2 * nb + 6]
        bufs = refs[2 * nb + 6:]
        for d, view, buf in zip(DILATIONS[1:], o_views[1:], bufs):
            for h in range(H):
                for r in range(d):
                    buf.at[h][pl.ds(r, tm // d, stride=d), :] = view[:, r * DA + h * HEAD_DIM:r * DA + (h + 1) * HEAD_DIM]
        ls = [r[...] for r in l_refs]
        top = ls[0]
        for l in ls[1:]:
            top = jnp.maximum(top, l)
        den = jnp.exp(ls[0] - top)
        for l in ls[1:]:
            den = den + jnp.exp(l - top)
        lse = top + jnp.log(den)
        lse_ref[...] = lse
        ws = [jnp.exp(l - lse) for l in ls]
        for h in range(H):
            sl = slice(h * HEAD_DIM, (h + 1) * HEAD_DIM)
            acc = ws[0][:, h:h + 1] * o_views[0][:, sl]
            for w, buf in zip(ws[1:], bufs):
                acc = acc + w[:, h:h + 1] * buf[h]
            att_ref[:, sl] = acc
        a = att_ref[...]
        g = gate_ref[...]
        y = a * _rstd(a) * gain_ref[...] * (g * _sig(g))
        y_ref[...] = y.astype(BF16)
        yt_ref[...] = y.T.astype(BF16)

    specs, _ = _dil_specs(S, DA, tm, F32)
    return _call(
        body, name=name, grid=(S // tm,),
        in_specs=specs + [_row(tm, H)] * nb + [_row(tm, DA, 3), _full((1, DA))],
        out_specs=[_row(tm, DA), _row(tm, DA), _row(tm, H), pl.BlockSpec((DA, tm), lambda i: (0, i))],
        out_shape=[SDS((S, DA), F32), SDS((S, 2 * DA), BF16), SDS((S, H), F32), SDS((2 * DA, S), BF16)],
        scratch_shapes=[_head_buf(tm, DA)] * (nb - 1),
        compiler_params=_params(48),
    )(*os_, *lses, proj, gain)


def gate_bwd(dcat, cblk, a, proj, gate_blk, gain, dilated, dproj, name):
    S, DA = a.shape
    H = DA // HEAD_DIM
    tm = _tile(S, 256, 16 * DILATIONS[-1])
    nd = len(DILATIONS) if dilated else 1

    def body(dy_ref, a_ref, gate_ref, gain_ref, *rest):
        dg_ref, gacc_ref, *rest = rest[0 if dproj is None else 1:]

        @pl.when(pl.program_id(0) == 0)
        def _():
            gacc_ref[...] = jnp.zeros_like(gacc_ref)

        dy, av, g, gain_v = dy_ref[...], a_ref[...], gate_ref[...], gain_ref[...]
        r = _rstd(av)
        n = av * r
        sg = _sig(g)
        dg_ref[...] = (dy * (n * gain_v) * (sg * (1.0 + g * (1.0 - sg)))).astype(BF16)
        drn = dy * (g * sg)
        gacc_ref[...] += jnp.sum(drn * n, axis=0, keepdims=True)
        dn = drn * gain_v
        da = r * (dn - n * jnp.mean(dn * n, axis=-1, keepdims=True))
        if dilated:
            da_refs, delta_ref, buf_ref = rest[:nd], rest[nd], rest[nd + 1]
            for h in range(H):
                buf_ref[h] = da[:, h * HEAD_DIM:(h + 1) * HEAD_DIM]
            _emit_dilated(buf_ref, da_refs, tm, DA)
            prod = da * av
            for h in range(H):
                delta_ref[:, h:h + 1] = jnp.sum(prod[:, h * HEAD_DIM:(h + 1) * HEAD_DIM], axis=-1, keepdims=True)
        else:
            rest[0][...] = da.astype(BF16)

    out_specs = [_row(tm, DA, gate_blk), _full((1, DA))]
    out_shape = [SDS((S, 7 * DA), BF16), SDS((1, DA), F32)]
    scratch = []
    carried = [] if dproj is None else [dproj]
    if dilated:
        specs, shapes = _dil_specs(S, DA, tm, BF16)
        out_specs += specs + [_row(tm, H)]
        out_shape += shapes + [SDS((S, H), F32)]
        scratch = [_head_buf(tm, DA)]
    else:
        out_specs.append(_row(tm, DA))
        out_shape.append(SDS((S, DA), BF16))
    return _call(
        body, name=name, grid=(S // tm,),
        in_specs=[_row(tm, DA, cblk), _row(tm, DA), _row(tm, DA, gate_blk), _full((1, DA))] + [pl.BlockSpec(memory_space=pl.ANY)] * len(carried),
        out_specs=out_specs, out_shape=out_shape, scratch_shapes=scratch, compiler_params=_params(48),
        input_output_aliases={4: 0} if carried else {},
    )(dcat, a, proj, gain, *carried)


def _by_sublane_phase(offsets):
    groups = [(p, [o for o in offsets if o % 8 == p]) for p in range(8)]
    return [(p, sorted(os_)) for p, os_ in groups if os_]


def _shifted_rows(src_ref, tmp_ref, base, phase, offsets, lanes):
    if phase == 0:
        return lambda o: src_ref[pl.ds(base + o, ROW_CHUNK), lanes]
    span = offsets[-1] - phase + ROW_CHUNK
    tmp_ref[phase, pl.ds(0, span), :] = src_ref[pl.ds(base + phase, span), lanes]
    return lambda o: tmp_ref[phase, pl.ds(o - phase, ROW_CHUNK), :]


def _shift_scratch():
    return pltpu.VMEM((8, CONV_PAD + ROW_CHUNK, LANE), F32)


def _fill_u(i, a_ref, b_ref, ah_ref, bh_ref, uext_ref, tm):
    uext_ref[pl.ds(CONV_PAD, tm), :] = a_ref[...] * _sig(b_ref[...])
    uh = ah_ref[...] * _sig(bh_ref[...])
    uext_ref[pl.ds(0, CONV_PAD), :] = jnp.where(i > 0, uh, 0.0)


def conv_fwd(proj, wk, bias, ln_g, ln_b, out_g, wpw, cat, cat_t, name):
    S = proj.shape[0]
    DC = proj.shape[1] // 7
    tm = _tile(S, 128, ROW_CHUNK)
    lead = CONV_PAD - (CONV_WIDTH - 1)

    def body(a_ref, b_ref, ah_ref, bh_ref, gate_ref, wk_ref, bias_ref, lg_ref, lb_ref, og_ref, wpw_ref, cat_ref, catt_ref,
             cy_ref, cyt_ref, zt_ref, conv_ref, y_ref, uext_ref, tmp_ref):
        _fill_u(pl.program_id(0), a_ref, b_ref, ah_ref, bh_ref, uext_ref, tm)

        def cols(cc, carry):
            c0 = pl.multiple_of(cc * LANE, LANE)
            lanes = pl.ds(c0, LANE)
            for rr in range(tm // ROW_CHUNK):
                acc = jnp.broadcast_to(bias_ref[:, lanes], (ROW_CHUNK, LANE))
                for phase, offsets in _by_sublane_phase(range(lead, lead + CONV_WIDTH)):
                    rows = _shifted_rows(uext_ref, tmp_ref, rr * ROW_CHUNK, phase, offsets, lanes)
                    for o in offsets:
                        acc = acc + wk_ref[o - lead:o - lead + 1, lanes] * rows(o)
                y_ref[pl.ds(rr * ROW_CHUNK, ROW_CHUNK), lanes] = acc
            return carry

        lax.fori_loop(0, DC // LANE, cols, 0)
        y = y_ref[...]
        yc = y - jnp.mean(y, axis=-1, keepdims=True)
        ln = yc * _rstd(yc) * lg_ref[...] + lb_ref[...]
        z = ln * _sig(ln)
        zt_ref[...] = z.T.astype(BF16)
        conv = jnp.dot(z.astype(BF16), wpw_ref[...], preferred_element_type=F32)
        conv_ref[...] = conv
        g = gate_ref[...]
        cy = conv * _rstd(conv) * og_ref[...] * (g * _sig(g))
        cy_ref[...] = cy.astype(BF16)
        cyt_ref[...] = cy.T.astype(BF16)

    vec = _full((1, DC))
    anywhere = pl.BlockSpec(memory_space=pl.ANY)
    return _call(
        body, name=name, grid=(S // tm,),
        in_specs=[_row(tm, DC, 4), _row(tm, DC, 5), _halo_prev(tm, DC, 4), _halo_prev(tm, DC, 5), _row(tm, DC, 6),
                  _full((CONV_PAD, DC)), vec, vec, vec, vec, _full((DC, DC)), anywhere, anywhere],
        out_specs=[_row(tm, DC, 1), pl.BlockSpec((DC, tm), lambda i: (1, i)), pl.BlockSpec((DC, tm), lambda i: (0, i)),
                   _row(tm, DC), _row(tm, DC)],
        out_shape=[SDS((S, 2 * DC), BF16), SDS((2 * DC, S), BF16), SDS((DC, S), BF16), SDS((S, DC), F32), SDS((S, DC), F32)],
        input_output_aliases={11: 0, 12: 1},
        scratch_shapes=[pltpu.VMEM((CONV_PAD + tm, DC), F32), _shift_scratch()],
        compiler_params=_params(48),
    )(proj, proj, proj, proj, proj, wk, bias, ln_g, ln_b, out_g, wpw, cat, cat_t)


def conv_bwd_ln(dconv, wpw, y, ln_g, ln_b, name):
    S, DC = y.shape
    tm = _tile(S, 256, 8)

    def body(dc_ref, wpw_ref, y_ref, lg_ref, lb_ref, dy_ref, glg_ref, glb_ref, gb_ref):
        @pl.when(pl.program_id(0) == 0)
        def _():
            glg_ref[...] = jnp.zeros_like(glg_ref)
            glb_ref[...] = jnp.zeros_like(glb_ref)
            gb_ref[...] = jnp.zeros_like(gb_ref)

        dz = lax.dot_general(dc_ref[...], wpw_ref[...], NT, preferred_element_type=F32)
        yv = y_ref[...]
        yc = yv - jnp.mean(yv, axis=-1, keepdims=True)
        rstd = _rstd(yc)
        yhat = yc * rstd
        ln = yhat * lg_ref[...] + lb_ref[...]
        sg = _sig(ln)
        dln = dz * (sg * (1.0 + ln * (1.0 - sg)))
        glb_ref[...] += jnp.sum(dln, axis=0, keepdims=True)
        glg_ref[...] += jnp.sum(dln * yhat, axis=0, keepdims=True)
        dyh = dln * lg_ref[...]
        dy = rstd * (dyh - jnp.mean(dyh, axis=-1, keepdims=True) - yhat * jnp.mean(dyh * yhat, axis=-1, keepdims=True))
        dy_ref[...] = dy
        gb_ref[...] += jnp.sum(dy, axis=0, keepdims=True)

    vec = _full((1, DC))
    return _call(
        body, name=name, grid=(S // tm,),
        in_specs=[_row(tm, DC), _full((DC, DC)), _row(tm, DC), vec, vec],
        out_specs=[_row(tm, DC), vec, vec, vec],
        out_shape=[SDS((S, DC), F32)] + [SDS((1, DC), F32)] * 3,
        compiler_params=_params(48),
    )(dconv, wpw, y, ln_g, ln_b)


def conv_bwd_dw(dy, proj, wk, dproj, name):
    S, DC = dy.shape
    tm = _tile(S, 128, ROW_CHUNK)
    nsteps = S // tm
    lead = CONV_PAD - (CONV_WIDTH - 1)
    groups = ROW_CHUNK // 8

    def body(dy_ref, dyn_ref, a_ref, b_ref, ah_ref, bh_ref, wk_ref, dproj_ref, dab_ref, gw_ref, uext_ref, dyext_ref, du_ref,
             tmp_dy_ref, tmp_u_ref):
        i = pl.program_id(0)

        @pl.when(i == 0)
        def _():
            gw_ref[...] = jnp.zeros_like(gw_ref)

        _fill_u(i, a_ref, b_ref, ah_ref, bh_ref, uext_ref, tm)
        dyext_ref[pl.ds(0, tm), :] = dy_ref[...]
        dyext_ref[pl.ds(tm, CONV_PAD), :] = jnp.where(i < nsteps - 1, dyn_ref[...], 0.0)

        def cols(cc, carry):
            c0 = pl.multiple_of(cc * LANE, LANE)
            lanes = pl.ds(c0, LANE)
            for rr in range(tm // ROW_CHUNK):
                base = rr * ROW_CHUNK
                acc = jnp.zeros((ROW_CHUNK, LANE), F32)
                for phase, offsets in _by_sublane_phase(range(CONV_WIDTH)):
                    rows = _shifted_rows(dyext_ref, tmp_dy_ref, base, phase, offsets, lanes)
                    for o in offsets:
                        j = CONV_WIDTH - 1 - o
                        acc = acc + wk_ref[j:j + 1, lanes] * rows(o)
                du_ref[pl.ds(base, ROW_CHUNK), lanes] = acc
                dyc = dyext_ref[pl.ds(base, ROW_CHUNK), lanes]
                for phase, offsets in _by_sublane_phase(range(lead, lead + CONV_WIDTH)):
                    rows = _shifted_rows(uext_ref, tmp_u_ref, base, phase, offsets, lanes)
                    for o in offsets:
                        prod = dyc * rows(o)
                        part = prod[0:8]
                        for k in range(1, groups):
                            part = part + prod[8 * k:8 * k + 8]
                        gw_ref[o - lead, :, lanes] += part
            return carry

        lax.fori_loop(0, DC // LANE, cols, 0)
        du = du_ref[...]
        sb = _sig(b_ref[...])
        dab_ref[:, :DC] = (du * sb).astype(BF16)
        dab_ref[:, DC:] = (du * a_ref[...] * sb * (1.0 - sb)).astype(BF16)

    return _call(
        body, name=name, grid=(nsteps,),
        in_specs=[_row(tm, DC), _halo_next(tm, DC, S // CONV_PAD), _row(tm, DC, 4), _row(tm, DC, 5),
                  _halo_prev(tm, DC, 4), _halo_prev(tm, DC, 5), _full((CONV_PAD, DC)), pl.BlockSpec(memory_space=pl.ANY)],
        out_specs=[_row(tm, 2 * DC, 2), _full((CONV_PAD, 8, DC))],
        out_shape=[SDS((S, 7 * DC), BF16), SDS((CONV_PAD, 8, DC), F32)],
        input_output_aliases={7: 0},
        scratch_shapes=[pltpu.VMEM((CONV_PAD + tm, DC), F32), pltpu.VMEM((tm + CONV_PAD, DC), F32), pltpu.VMEM((tm, DC), F32),
                        _shift_scratch(), _shift_scratch()],
        compiler_params=_params(40),
    )(dy, dy, proj, proj, proj, proj, wk, dproj)


def loss_head(xo, target, name):
    S, D = xo.shape
    tm = _tile(S, 256, 8)

    def body(x_ref, t_ref, dy_ref, dyb_ref, acc_ref):
        @pl.when(pl.program_id(0) == 0)
        def _():
            acc_ref[...] = jnp.zeros_like(acc_ref)

        err = x_ref[...] - t_ref[...]
        dy = err * (1.0 / D)
        dy_ref[...] = dy
        dyb_ref[...] = dy.astype(BF16)
        acc_ref[...] += jnp.sum(err * dy, axis=0, keepdims=True) * 0.5

    return _call(
        body, name=name, grid=(S // tm,), in_specs=[_row(tm, D), _row(tm, D)],
        out_specs=[_row(tm, D), _row(tm, D), _full((1, D))],
        out_shape=[SDS((S, D), F32), SDS((S, D), BF16), SDS((1, D), F32)],
        compiler_params=_params(32),
    )(xo, target)


def _coords():
    x, y, c = lax.axis_index("x"), lax.axis_index("y"), lax.axis_index("c")
    return x, y, c


def _lin(p):
    return 4 * p[0] + 2 * p[1] + p[2]


def _chip(p):
    return 2 * p[0] + p[1]


def _slot(ref, axis, idx, size):
    index = [slice(None)] * len(ref.shape)
    index[axis] = pl.ds(idx * size, size)
    return ref.at[tuple(index)]


def all_gather(blocks, axes, name):
    na = len(blocks)
    sizes = [b.shape[ax] for b, ax in zip(blocks, axes)]
    fulls = [SDS(b.shape[:ax] + (N_DEV * b.shape[ax],) + b.shape[ax + 1:], b.dtype) for b, ax in zip(blocks, axes)]

    def body(*refs):
        in_refs, out_refs = refs[:na], refs[na:2 * na]
        send_sems, recv_sems, local_sems = refs[2 * na:]
        x, y, c = _coords()
        me, sibling = (x, y, c), (x, y, 1 - c)
        chips = [(1 - x, y), (x, 1 - y), (1 - x, 1 - y)]
        south = c == 0
        relayed = (jnp.where(south, 1 - x, x), jnp.where(south, y, 1 - y), c)
        onward = (jnp.where(south, x, 1 - x), jnp.where(south, 1 - y, y), c)

        def place(a, p):
            return _slot(out_refs[a], axes[a], _lin(p), sizes[a])

        def copy(a, k, block, to, src=None):
            return pltpu.make_async_remote_copy(
                src_ref=place(a, block) if src is None else src, dst_ref=place(a, block),
                send_sem=send_sems.at[a, k], recv_sem=recv_sems.at[a, k], device_id=to, device_id_type=MESH)

        mine = [pltpu.make_async_copy(in_refs[a], place(a, me), local_sems.at[a]) for a in range(na)]
        for cp in mine:
            cp.start()
        first = []
        for a in range(na):
            first.append(copy(a, 0, me, sibling, src=in_refs[a]))
            first += [copy(a, 1 + j, me, (*chip, c), src=in_refs[a]) for j, chip in enumerate(chips[:2])]
        for cp in first:
            cp.start()
        later = []
        for a in range(na):
            for j, chip in enumerate(chips[:2]):
                copy(a, 1 + j, (*chip, c), me).wait_recv()
            later.append(copy(a, 3, relayed, onward))
            later += [copy(a, 4 + j, (*chip, c), sibling) for j, chip in enumerate(chips[:2])]
            for cp in later[-3:]:
                cp.start()
        for a in range(na):
            copy(a, 3, (*chips[2], c), me).wait_recv()
            later.append(copy(a, 6, (*chips[2], c), sibling))
            later[-1].start()
        for a in range(na):
            copy(a, 0, sibling, me).wait_recv()
            for j, chip in enumerate(chips):
                copy(a, 4 + j, (*chip, 1 - c), me).wait_recv()
        for cp in first + later:
            cp.wait_send()
        for cp in mine:
            cp.wait()

    hbm = pl.BlockSpec(memory_space=pltpu.HBM)
    return _call(
        body, name=name, in_specs=[hbm] * na, out_specs=[hbm] * na, out_shape=fulls,
        scratch_shapes=[pltpu.SemaphoreType.DMA((na, 7)), pltpu.SemaphoreType.DMA((na, 7)), pltpu.SemaphoreType.DMA((na,))],
    )(*blocks)


class _Exchange:
    def __init__(self, gather, srcs, axes, name, after, route="all", lands=None):
        self.gather, self.axes, self.name, self.route = gather, axes, name, route
        if route == "pass":
            self.na, self.ns = len(lands), 0
            self.sizes = [l.shape[ax] // N_DEV for l, ax in zip(lands, axes)]
            self.kinds = [pltpu.HBM(l.shape, l.dtype) for l in lands]
            self._start([pltpu.with_memory_space_constraint(t, pltpu.HBM) for t in lands], after)
            return
        self.na = self.ns = len(srcs)
        if gather:
            self.sizes = [s.shape[ax] for s, ax in zip(srcs, axes)]
            lands = [s.shape[:ax] + (N_DEV * s.shape[ax],) + s.shape[ax + 1:] for s, ax in zip(srcs, axes)]
        else:
            self.sizes = [None if ax is None else s.shape[ax] // N_DEV for s, ax in zip(srcs, axes)]
            lands = [s.shape if ax is None else (N_DEV,) + s.shape[:ax] + (sz,) + s.shape[ax + 1:]
                     for s, ax, sz in zip(srcs, axes, self.sizes)]
        self.kinds = [pltpu.HBM(s.shape, s.dtype) for s in srcs] + [pltpu.HBM(l, s.dtype) for l, s in zip(lands, srcs)]
        lands = [lax.empty(l, s.dtype) for l, s in zip(lands, srcs)]
        after = jnp.zeros((8, LANE), F32) if after is None else after
        self._start([pltpu.with_memory_space_constraint(t, pltpu.HBM) for t in list(srcs) + lands], after)

    def _src(self, a, ref, owner):
        if self.gather:
            return ref
        return ref.at[_lin(owner)] if self.axes[a] is None else _slot(ref, self.axes[a], _lin(owner), self.sizes[a])

    def _dst(self, a, land, sender):
        return _slot(land, self.axes[a], _lin(sender), self.sizes[a]) if self.gather else land.at[_lin(sender)]

    def _flips(self):
        if self.route == "all":
            return [(k >> 2 & 1, k >> 1 & 1, k & 1) for k in range(1, N_DEV)]
        if self.route == "chips":
            return [(0, 0, 1), (1, 0, 0), (0, 1, 0), (1, 1, 0)]
        if self.route == "sibling":
            return [(0, 0, 1)] * 4
        return [(1, 0, 0), (0, 1, 0), (1, 1, 0)]

    def _copies(self, refs, send_sems, recv_sems):
        na, ns = self.na, self.ns
        me = _coords()
        flips = self._flips()
        n = len(flips)
        others = [tuple(1 - v if f else v for v, f in zip(me, flip)) for flip in flips]
        sends, arrivals = [], []
        for a in range(na):
            land = refs[ns + a]
            for k, other in enumerate(others):
                if self.route == "pass":
                    peer = (me[0], me[1], 1 - me[2])
                    theirs = (other[0], other[1], 1 - me[2])
                    send = dict(src_ref=self._dst(a, land, other), dst_ref=self._dst(a, land, other))
                    arrive = dict(src_ref=self._dst(a, land, theirs), dst_ref=self._dst(a, land, theirs))
                elif self.route == "sibling":
                    peer = other
                    send = dict(src_ref=refs[a].at[k], dst_ref=land.at[k])
                    arrive = send
                elif self.route == "chip_parts":
                    peer = other
                    send = dict(src_ref=refs[a].at[_chip(peer)], dst_ref=land.at[_chip(me)])
                    arrive = dict(src_ref=refs[a].at[_chip(me)], dst_ref=land.at[_chip(peer)])
                else:
                    peer = other
                    send = dict(src_ref=self._src(a, refs[a], peer), dst_ref=self._dst(a, land, me))
                    arrive = dict(src_ref=self._src(a, refs[a], me), dst_ref=self._dst(a, land, peer))
                pair = dict(send_sem=send_sems.at[n * a + k], recv_sem=recv_sems.at[n * a + k], device_id=peer, device_id_type=MESH)
                sends.append(pltpu.make_async_remote_copy(**send, **pair))
                arrivals.append(pltpu.make_async_remote_copy(**arrive, **pair))
        return sends, arrivals

    def _place_own(self, operands):
        na = self.na
        mine = _chip(_coords()) if self.route == "chip_parts" else _lin(_coords())
        me = jnp.reshape(mine, (1,)).astype(jnp.int32)
        lands = []
        for a in range(na):
            src, land, ax = operands[a], operands[na + a], self.axes[a]
            if ax == 1:
                R, C = src.shape[0], self.sizes[a]
                steps, tile = 1, (R, C)
                in_map = lambda i, me_ref: (0, me_ref[0])
            elif ax is None:
                R, C = src.shape[1:]
                tr = _tile(R, 512, 16)
                steps, tile = R // tr, (None, tr, C)
                in_map = lambda i, me_ref: (me_ref[0], i, 0)
            else:
                R, C = (src.shape[0] if self.gather else self.sizes[a]), src.shape[1]
                tr = _tile(R, 512, 16)
                steps, tile = R // tr, (tr, C)
                in_map = (lambda i, me_ref: (i, 0)) if self.gather else (lambda i, me_ref, n=R // tr: (me_ref[0] * n + i, 0))
            if self.gather:
                out_spec = pl.BlockSpec(tile, lambda i, me_ref, n=steps: (me_ref[0] * n + i, 0))
            else:
                out_spec = pl.BlockSpec((None,) + tuple(t for t in tile if t is not None), lambda i, me_ref: (me_ref[0], i, 0))

            def body(me_ref, src_ref, land_ref, out_ref):
                out_ref[...] = src_ref[...]

            lands.append(_call(
                body, name=f"{self.name}_own{a}",
                grid_spec=pltpu.PrefetchScalarGridSpec(
                    num_scalar_prefetch=1, grid=(steps,),
                    in_specs=[pl.BlockSpec(tile, in_map), pl.BlockSpec(memory_space=pl.ANY)], out_specs=out_spec),
                out_shape=SDS(land.shape, land.dtype), input_output_aliases={2: 0}, compiler_params=_params(32),
            )(me, src, land))
        return operands[:na] + lands

    def _start(self, operands, after):
        nops = self.ns + self.na
        nsem = len(self._flips()) * self.na
        if self.ns and self.route != "sibling":
            operands = self._place_own(operands)

        def body(*refs):
            ins = refs[:nops]
            send_sems, recv_sems, token_ref = refs[nops + 1], refs[nops + 2], refs[2 * nops + 3]
            for cp in self._copies(ins, send_sems, recv_sems)[0]:
                cp.start()
            token_ref[...] = jnp.zeros_like(token_ref)

        hbm = pl.BlockSpec(memory_space=pltpu.HBM)
        sem = pl.BlockSpec(memory_space=pltpu.SEMAPHORE)
        outs = _call(
            body, name=self.name + "_start",
            in_specs=[hbm] * nops + [pl.BlockSpec(memory_space=pl.ANY)],
            out_specs=[sem, sem] + [hbm] * nops + [pl.BlockSpec(memory_space=pltpu.VMEM)],
            out_shape=[pltpu.SemaphoreType.DMA((nsem,)), pltpu.SemaphoreType.DMA((nsem,))] + self.kinds + [SDS((8, LANE), F32)],
            input_output_aliases={i: 2 + i for i in range(nops)},
            compiler_params=pltpu.CompilerParams(has_side_effects=pltpu.SideEffectType.DATAFLOW_SIDE_EFFECTING),
        )(*operands, after)
        self.sems, self.thru, self.token = outs[:2], outs[2:2 + nops], outs[2 + nops][0:1, 0:1]

    def wait(self, after):
        nops = self.ns + self.na

        def body(*refs):
            ins, send_sems, recv_sems = refs[:nops], refs[nops], refs[nops + 1]
            sends, arrivals = self._copies(ins, send_sems, recv_sems)
            for cp in sends:
                cp.wait_send()
            for cp in arrivals:
                cp.wait_recv()

        hbm = pl.BlockSpec(memory_space=pltpu.HBM)
        sem = pl.BlockSpec(memory_space=pltpu.SEMAPHORE)
        outs = _call(
            body, name=self.name + "_wait",
            in_specs=[hbm] * nops + [sem, sem, pl.BlockSpec(memory_space=pl.ANY)],
            out_specs=[hbm] * nops, out_shape=self.kinds,
            input_output_aliases={i: i for i in range(nops)},
            compiler_params=pltpu.CompilerParams(has_side_effects=pltpu.SideEffectType.DATAFLOW_SIDE_EFFECTING),
        )(*self.thru, *self.sems, after)
        return outs[self.ns:]


def adamw(w, m, v, parts, layer, prev, name):
    nl, R, C = w.shape
    nparts = parts.shape[0]
    tr = _tile(R, 128, 8) if R % 8 == 0 else R

    def body(w_ref, m_ref, v_ref, p_ref, *rest):
        g_ref, d_ref, mo_ref, vo_ref = rest[-4:]
        g = p_ref[0].astype(F32)
        for s in range(1, nparts):
            g = g + p_ref[s].astype(F32)
        mn = ADAM_B1 * m_ref[0] + (1.0 - ADAM_B1) * g
        vn = ADAM_B2 * v_ref[0] + (1.0 - ADAM_B2) * (g * g)
        m_hat = mn / (1.0 - ADAM_B1 ** ADAM_STEP)
        v_hat = vn / (1.0 - ADAM_B2 ** ADAM_STEP)
        g_ref[0] = g
        d_ref[0] = -ADAM_LR * (m_hat / (jnp.sqrt(v_hat) + ADAM_EPS) + ADAM_WD * w_ref[0])
        mo_ref[0] = mn
        vo_ref[0] = vn

    row = pl.BlockSpec((1, tr, C), lambda i: (layer, i, 0))
    carried = [] if prev is None else list(prev)
    return _call(
        body, name=name, grid=(R // tr,),
        in_specs=[row, row, row, pl.BlockSpec((nparts, tr, C), lambda i: (0, i, 0))] + [pl.BlockSpec(memory_space=pl.ANY)] * len(carried),
        out_specs=[row] * 4, out_shape=[SDS((nl, R, C), F32)] * 4,
        input_output_aliases={4 + k: k for k in range(len(carried))},
        compiler_params=_params(48),
    )(w, m, v, parts, *carried)


def _rope_tables(S):
    inv_freq = 1.0 / (ROPE_THETA ** (jnp.arange(0, HEAD_DIM, 2, dtype=F32) / HEAD_DIM))
    ang = jnp.arange(S, dtype=F32)[:, None] * inv_freq[None, :]
    cos, sin = jnp.cos(ang), jnp.sin(ang)
    return jnp.concatenate([cos, cos], axis=-1), jnp.concatenate([-sin, sin], axis=-1)


def _pack_small(D, norm_g, dw_bias, conv_ln_g, conv_ln_b, att_out_g, conv_out_g, q_norm_g, k_norm_g, extra=None):
    qk = jnp.concatenate([q_norm_g.reshape(-1), k_norm_g.reshape(-1)])
    qk = jnp.pad(qk, (0, D - qk.shape[0])).reshape(1, D)
    zero = jnp.zeros((1, D), F32)
    return jnp.concatenate([norm_g, dw_bias, conv_ln_g, conv_ln_b, att_out_g, conv_out_g, qk, zero,
                            zero if extra is None else extra, zero], axis=0)


def _unpack_small(p):
    rows = [p[2 * i:2 * i + 2] for i in range(6)]
    qk = p[12, :4 * HEAD_DIM].reshape(2, DEPTH, HEAD_DIM)
    return rows + [qk[0], qk[1]]


def kernel(x, norm_g, w_in, q_norm_g, k_norm_g, dw_kernel, dw_bias, conv_ln_g, conv_ln_b, w_pw, att_out_g, conv_out_g, w_out, loss_target, m_norm_g, m_w_in, m_q_norm_g, m_k_norm_g, m_dw_kernel, m_dw_bias, m_conv_ln_g, m_conv_ln_b, m_w_pw, m_att_out_g, m_conv_out_g, m_w_out, v_norm_g, v_w_in, v_q_norm_g, v_k_norm_g, v_dw_kernel, v_dw_bias, v_conv_ln_g, v_conv_ln_b, v_w_pw, v_att_out_g, v_conv_out_g, v_w_out):
    xs = x[0]
    D = xs.shape[1]
    bf = lambda t, l: t[l].astype(BF16)
    wint0, dwk_f = all_gather([bf(w_in, 0).T, dw_kernel], [0, 2], "gather_first")
    early = _Exchange(True, [bf(w_pw, 0), bf(w_out, 0)], [0, 0], "gather_layer0", after=wint0)
    later = _Exchange(True, [bf(w_in, 1).T, bf(w_pw, 1), bf(w_out, 1)], [0, 0, 0], "gather_layer1", after=early.token,
                      route="chips")
    landed = {}

    def weights(l, cur):
        if l == 0:
            return wint0, later.token
        landed[1] = landed["passing"].wait(cur)
        return landed[1][0], None

    def mixer_weights(l, cur):
        if l == 0:
            mine = early.wait(cur)
            landed["passing"] = _Exchange(True, None, [0, 0, 0], "pass_layer1", after=mine[0], route="pass",
                                          lands=later.wait(mine[0]))
            return (*mine, landed["passing"].token)
        return (*landed[1][1:], None)

    sent = [[] for _ in range(DEPTH)]

    def send_grads(l, grads, axes, tag):
        sent[l].append(_Exchange(False, grads, axes, f"scatter_{tag}_layer{l}", after=None))
        return sent[l][-1].token

    def win_grads(l, ht, dproj):
        if l > 0:
            return send_grads(l, [mm_nn(ht, dproj, BF16, f"dwin_{l}", owners=N_DEV)], [None], "w_in")
        core = jnp.reshape(lax.axis_index("c"), (1,)).astype(jnp.int32)
        theirs = mm_core_blocks(ht, dproj, 1 - core, f"dwin_{l}_for_sibling")
        to_sibling = _Exchange(False, [theirs], [None], f"scatter_w_in_sibling_layer{l}", after=None, route="sibling")
        mine = mm_core_blocks(ht, dproj, core + to_sibling.token[0].astype(jnp.int32), f"dwin_{l}_own")
        (from_sibling,) = to_sibling.wait(mine)
        both = add_blocks(mine, from_sibling, f"dwin_{l}_chip_sum")
        sent[l].append(_Exchange(False, [both], [None], f"scatter_w_in_chips_layer{l}", after=None, route="chip_parts"))
        return sent[l][-1].token

    dx, loss_cols, small = local_step(xs, loss_target[0], weights, mixer_weights, dwk_f, norm_g, q_norm_g, k_norm_g, dw_bias,
                                      conv_ln_g, conv_ln_b, att_out_g, conv_out_g, send_grads, win_grads)

    big = ((w_pw, m_w_pw, v_w_pw), (w_out, m_w_out, v_w_out), (dw_kernel, m_dw_kernel, v_dw_kernel), (w_in, m_w_in, v_w_in))
    results = [None] * len(big)
    after = dx
    for l in reversed(range(DEPTH)):
        parts = [p for ex in sent[l] for p in ex.wait(after)]
        for i, ((w, m, v), p) in enumerate(zip(big, parts)):
            results[i] = adamw(w, m, v, p, l, results[i], f"adamw_{w.shape[1]}_{w.shape[2]}_{l}")
        after = results[0][3]
    r_wpw, r_wout, r_dwk, r_win = results

    stack = lambda k: jnp.concatenate(small[k], axis=0)
    mine = _pack_small(D, stack("norm_g"), stack("dw_bias"), stack("conv_ln_g"), stack("conv_ln_b"), stack("att_out_g"),
                       stack("conv_out_g"), stack("q"), stack("k"), extra=loss_cols)
    (p_small,) = all_gather([mine], [0], "gather_small")
    pk = lambda n, dw, lg, lb, ao, co, q, k: _pack_small(D, n, dw, lg, lb, ao, co, q, k)[None]
    r_small = adamw(pk(norm_g, dw_bias, conv_ln_g, conv_ln_b, att_out_g, conv_out_g, q_norm_g, k_norm_g),
                    pk(m_norm_g, m_dw_bias, m_conv_ln_g, m_conv_ln_b, m_att_out_g, m_conv_out_g, m_q_norm_g, m_k_norm_g),
                    pk(v_norm_g, v_dw_bias, v_conv_ln_g, v_conv_ln_b, v_att_out_g, v_conv_out_g, v_q_norm_g, v_k_norm_g),
                    p_small.reshape(N_DEV, 16, D), 0, None, "adamw_small")
    r_small = [r[0] for r in r_small]
    loss = jnp.sum(r_small[0][14])

    outs = [loss, dx[None]]
    for i in range(4):
        n_, dwb, lg, lb, ao, co, q_, k_ = _unpack_small(r_small[i])
        outs += [n_, r_win[i], q_, k_, r_dwk[i], dwb, lg, lb, r_wpw[i], ao, co, r_wout[i]]
    return tuple(outs)


def local_step(xs, target, weights, mixer_weights, dwk_f, norm_g, q_norm_g, k_norm_g, dw_bias, conv_ln_g, conv_ln_b,
               att_out_g, conv_out_g, send_grads, win_grads):
    S, D = xs.shape
    cos2, sin2 = _rope_tables(S)
    dwk_f = jnp.pad(dwk_f, ((0, 0), (0, CONV_PAD - CONV_WIDTH), (0, 0)))

    def vec(p, l, zero=None):
        row = p[l].reshape(1, -1)
        return row if zero is None else row + zero

    saved = []
    cur = xs
    for l in range(DEPTH):
        wint, zero = weights(l, cur)
        proj, _, h_t = in_proj(cur, vec(norm_g, l, zero), wint, f"in_proj_{l}")
        qs, ks, vs = qk_prep(proj, vec(q_norm_g, l), vec(k_norm_g, l), cos2, sin2, f"qk_prep_{l}")
        os_, lses = [], []
        for i, d in enumerate(DILATIONS):
            o, lse = attn_fwd(qs[i], ks[i], vs[i], d, f"attn_fwd_{l}_d{d}")
            os_.append(o)
            lses.append(_from_branch(lse))
        att, cat, lse, cat_t = att_combine(os_, lses, proj, vec(att_out_g, l), f"att_combine_{l}")
        wpw, wout, zero = mixer_weights(l, lse)
        cat, cat_t, z_t, conv, y = conv_fwd(proj, dwk_f[l], vec(dw_bias, l, zero), vec(conv_ln_g, l), vec(conv_ln_b, l),
                                            vec(conv_out_g, l), wpw, cat, cat_t, f"conv_fwd_{l}")
        nxt = mm_nn(cat, wout, F32, f"out_proj_{l}", add=cur)
        saved.append(dict(x=cur, proj=proj, h_t=h_t, qs=qs, ks=ks, vs=vs, att=att, lse=lse, cat_t=cat_t, z_t=z_t, conv=conv,
                          y=y, wint=wint, wpw=wpw, wout=wout))
        cur = nxt

    dx, dxb, loss_cols = loss_head(cur, target, "loss_head")

    small = {k: [None] * DEPTH for k in ("norm_g", "dw_bias", "conv_ln_g", "conv_ln_b", "att_out_g", "conv_out_g", "q", "k")}
    for l in reversed(range(DEPTH)):
        sv = saved[l]
        proj = sv["proj"]
        dcat = mm_nt(dxb, sv["wout"], F32, f"dcat_{l}")
        g_wout = mm_nn(sv["cat_t"], dxb, BF16, f"dwout_{l}")
        dproj, small["conv_out_g"][l], dconv = gate_bwd(dcat, 1, sv["conv"], proj, 6, vec(conv_out_g, l), False, None,
                                                        f"conv_gate_bwd_{l}")
        dy, small["conv_ln_g"][l], small["conv_ln_b"][l], small["dw_bias"][l] = conv_bwd_ln(
            dconv, sv["wpw"], sv["y"], vec(conv_ln_g, l), vec(conv_ln_b, l), f"conv_bwd_ln_{l}")
        g_wpw = mm_nn(sv["z_t"], dconv, BF16, f"dwpw_{l}")
        dproj, gw = conv_bwd_dw(dy, proj, dwk_f[l], dproj, f"conv_bwd_dw_{l}")
        g_dwk = jnp.sum(gw, axis=1)[:CONV_WIDTH]
        zero = send_grads(l, [g_wpw, g_wout, g_dwk], [0, 0, 1], "mixer")
        dproj, small["att_out_g"][l], *datts, delta = gate_bwd(dcat, 0, sv["att"], proj, 3, vec(att_out_g, l, zero), True,
                                                                dproj, f"att_gate_bwd_{l}")
        dqs, dks, dvs = [], [], []
        for i, d in enumerate(DILATIONS):
            dq, dk, dv = attn_bwd(sv["qs"][i], sv["ks"][i], sv["vs"][i], datts[i], _to_branch(sv["lse"], d),
                                  _to_branch(delta, d), d, f"attn_bwd_{l}_d{d}")
            dqs.append(dq)
            dks.append(dk)
            dvs.append(dv)
        dproj, small["q"][l], small["k"][l] = qk_prep_bwd(dqs, dks, dvs, proj, vec(q_norm_g, l), vec(k_norm_g, l),
                                                          cos2, sin2, dproj, f"qk_prep_bwd_{l}")
        zero = win_grads(l, sv["h_t"], dproj)
        dx, dxb, small["norm_g"][l] = in_proj_bwd(dproj, sv["wint"], sv["x"], vec(norm_g, l, zero), dx, f"in_proj_bwd_{l}")

    return dx, loss_cols, small
```
